```python
import math
import jax, jax.numpy as jnp
from jax import lax
import numpy as np

D_MODEL = 2048
BATCH = 8
SEQ = 4096
DEPTH = 2

N_META = 16
D_MIX = D_MODEL
CONV_W = D_MIX // 4
CONV_K = 31
HEAD_DIM = 64
N_HEADS = (D_MIX // 2) // HEAD_DIM
N_KV = 4
GROUP = N_HEADS // N_KV
ATT_W = N_HEADS * HEAD_DIM
KV_W = N_KV * HEAD_DIM
WINDOW = 128
BLOCK = 128
ROT_DIM = HEAD_DIM // 4
ROPE_THETA = 500000.0
LRU_W = D_MIX // 4
LRU_HEADS = 8
LRU_HEAD_DIM = LRU_W // LRU_HEADS
LRU_CONV_K = 4
LRU_C = 8.0
IN_WIDTHS = (CONV_W, CONV_W, CONV_W,
             ATT_W, KV_W, KV_W, ATT_W,
             LRU_W, LRU_W)
IN_TOTAL = sum(IN_WIDTHS)
OUT_IN = CONV_W + ATT_W + LRU_W
LN_EPS = 1e-5
DEEPNORM_ALPHA = (2.0 * DEPTH) ** 0.25
DEEPNORM_BETA = (8.0 * DEPTH) ** -0.25
NEG_INF = -1e30

kernel_name = "hymba_style_conv_swa_rglru_deepnorm"


def _layer_norm(x, g, b):
    xf = x.astype(jnp.float32)
    mu = jnp.mean(xf, axis=-1, keepdims=True)
    var = jnp.mean(jnp.square(xf - mu), axis=-1, keepdims=True)
    y = (xf - mu) * lax.rsqrt(var + LN_EPS)
    return (y * g.astype(jnp.float32) + b.astype(jnp.float32)).astype(x.dtype)


def _causal_depthwise_conv(x, w, b):
    k = w.shape[0]
    y = lax.conv_general_dilated(
        x, w[:, None, :].astype(x.dtype), window_strides=(1,), padding=[(k - 1, 0)],
        dimension_numbers=("NWC", "WIO", "NWC"), feature_group_count=x.shape[-1])
    return y + b


def _partial_rotary(x, pos):
    half = ROT_DIM // 2
    inv_freq = ROPE_THETA ** (-jnp.arange(half, dtype=jnp.float32) / half)
    ang = pos.astype(jnp.float32)[:, None] * inv_freq[None, :]
    cos = jnp.cos(ang)[None, :, None, :]
    sin = jnp.sin(ang)[None, :, None, :]
    x1 = x[..., :half].astype(jnp.float32)
    x2 = x[..., half:ROT_DIM].astype(jnp.float32)
    rot = jnp.concatenate([x1 * cos - x2 * sin, x2 * cos + x1 * sin], axis=-1).astype(x.dtype)
    return jnp.concatenate([rot, x[..., ROT_DIM:]], axis=-1)


def _sliding_window_sink_attention(q, k, v, sinks):
    B, L, H, Dh = q.shape
    S = L - N_META
    nblk = S // BLOCK
    scale = Dh ** -0.5
    sink_kg = sinks.astype(jnp.float32).reshape(N_KV, GROUP)

    qm, qr = q[:, :N_META], q[:, N_META:]
    km, kr = k[:, :N_META], k[:, N_META:]
    vm, vr = v[:, :N_META], v[:, N_META:]

    qb = qr.reshape(B, nblk, BLOCK, N_KV, GROUP, Dh)
    kb = kr.reshape(B, nblk, BLOCK, N_KV, Dh)
    vb = vr.reshape(B, nblk, BLOCK, N_KV, Dh)
    pad = ((0, 0), (1, 0), (0, 0), (0, 0), (0, 0))
    k_band = jnp.concatenate([jnp.pad(kb, pad)[:, :-1], kb], axis=2)
    v_band = jnp.concatenate([jnp.pad(vb, pad)[:, :-1], vb], axis=2)

    s_meta = jnp.einsum("bnqkgd,bmkd->bkgnqm", qb, km).astype(jnp.float32) * scale
    s_band = jnp.einsum("bnqkgd,bnjkd->bkgnqj", qb, k_band).astype(jnp.float32) * scale
    qi = jnp.arange(BLOCK)[:, None]
    kj = jnp.arange(2 * BLOCK)[None, :]
    diff = BLOCK + qi - kj
    in_window = (diff >= 0) & (diff < WINDOW)
    blk = jnp.arange(nblk)[:, None, None]
    valid = in_window[None] & ((kj[None] >= BLOCK) | (blk > 0))
    s_band = jnp.where(valid[None, None, None], s_band, NEG_INF)
    sink_col = jnp.broadcast_to(sink_kg[None, :, :, None, None, None], s_meta.shape[:-1] + (1,))
    probs = jax.nn.softmax(jnp.concatenate([s_meta, s_band, sink_col], axis=-1), axis=-1)
    p_meta = probs[..., :N_META].astype(v.dtype)
    p_band = probs[..., N_META:N_META + 2 * BLOCK].astype(v.dtype)
    out_r = (jnp.einsum("bkgnqm,bmkd->bnqkgd", p_meta, vm)
             + jnp.einsum("bkgnqj,bnjkd->bnqkgd", p_band, v_band)).reshape(B, S, H, Dh)

    qmg = qm.reshape(B, N_META, N_KV, GROUP, Dh)
    s_mm = jnp.einsum("bqkgd,bmkd->bkgqm", qmg, km).astype(jnp.float32) * scale
    causal = jnp.tril(jnp.ones((N_META, N_META), dtype=bool))
    s_mm = jnp.where(causal[None, None, None], s_mm, NEG_INF)
    sink_mm = jnp.broadcast_to(sink_kg[None, :, :, None, None], s_mm.shape[:-1] + (1,))
    p_mm = jax.nn.softmax(jnp.concatenate([s_mm, sink_mm], axis=-1), axis=-1)[..., :N_META]
    out_m = jnp.einsum("bkgqm,bmkd->bqkgd", p_mm.astype(v.dtype), vm).reshape(B, N_META, H, Dh)

    return jnp.concatenate([out_m, out_r], axis=1)


def _rg_lru(x, w_a, b_a, w_x, b_x, lam):
    B, L, C = x.shape
    xh = x.reshape(B, L, LRU_HEADS, LRU_HEAD_DIM)
    r = jax.nn.sigmoid(jnp.einsum("blhi,hij->blhj", xh, w_a).reshape(B, L, C) + b_a)
    i = jax.nn.sigmoid(jnp.einsum("blhi,hij->blhj", xh, w_x).reshape(B, L, C) + b_x)
    log_a = -LRU_C * r.astype(jnp.float32) * jax.nn.softplus(-lam.astype(jnp.float32))
    a = jnp.exp(log_a)
    u = jnp.sqrt(-jnp.expm1(2.0 * log_a)) * (i * x).astype(jnp.float32)

    def combine(c1, c2):
        a1, b1 = c1
        a2, b2 = c2
        return a1 * a2, a2 * b1 + b2

    _, h = lax.associative_scan(combine, (a, u), axis=1)
    return h.astype(x.dtype)


def _hybrid_layer(h, pos, w_in, conv_dw_w, conv_dw_b, conv_ln_g, conv_ln_b, conv_pw_w, conv_pw_b,
                  attn_sinks, lru_conv_w, lru_conv_b, lru_wa, lru_ba, lru_wx, lru_bx, lru_lambda,
                  w_out, ln_post_g, ln_post_b):
    B, L, _ = h.shape
    proj = h @ w_in
    split_pts = [int(s) for s in np.cumsum(IN_WIDTHS)[:-1]]
    (c_val, c_glu, c_gate, q, k, v, a_gate, r_x, r_gate) = jnp.split(proj, split_pts, axis=-1)

    c = c_val * jax.nn.sigmoid(c_glu)
    c = _causal_depthwise_conv(c, conv_dw_w, conv_dw_b)
    c = jax.nn.silu(_layer_norm(c, conv_ln_g, conv_ln_b))
    c = c @ conv_pw_w + conv_pw_b
    y_conv = c * jax.nn.silu(c_gate)

    q = _partial_rotary(q.reshape(B, L, N_HEADS, HEAD_DIM), pos)
    k = _partial_rotary(k.reshape(B, L, N_KV, HEAD_DIM), pos)
    v = v.reshape(B, L, N_KV, HEAD_DIM)
    att = _sliding_window_sink_attention(q, k, v, attn_sinks).reshape(B, L, ATT_W)
    y_attn = att * jax.nn.silu(a_gate)

    r = _causal_depthwise_conv(r_x, lru_conv_w, lru_conv_b)
    r = _rg_lru(r, lru_wa, lru_ba, lru_wx, lru_bx, lru_lambda)
    y_lru = r * jax.nn.silu(r_gate)

    mixed = jnp.concatenate([y_conv, y_attn, y_lru], axis=-1) @ w_out
    return _layer_norm(DEEPNORM_ALPHA * h + mixed, ln_post_g, ln_post_b)


def _fwd_setup_inputs(seed: int = 0) -> dict:
    key = jax.random.key(seed)
    ks = jax.random.split(key, 24)
    f32 = jnp.float32
    nrm = lambda k, shape, s: jax.random.normal(k, shape, f32) * s
    u = jax.random.uniform(ks[17], (DEPTH, LRU_W), f32, 0.9, 0.999)
    s = u ** (1.0 / LRU_C)
    lam = jnp.log(s) - jnp.log1p(-s)
    return {
        "x": nrm(ks[0], (BATCH, SEQ, D_MODEL), 1.0),
        "meta_tokens": nrm(ks[1], (N_META, D_MODEL), 1.0),
        "ln_in_g": 1.0 + nrm(ks[2], (D_MODEL,), 0.02),
        "ln_in_b": nrm(ks[3], (D_MODEL,), 0.02),
        "w_in": nrm(ks[4], (DEPTH, D_MODEL, IN_TOTAL), D_MODEL ** -0.5),
        "conv_dw_w": nrm(ks[5], (DEPTH, CONV_K, CONV_W), CONV_K ** -0.5),
        "conv_dw_b": nrm(ks[6], (DEPTH, CONV_W), 0.01),
        "conv_ln_g": 1.0 + nrm(ks[7], (DEPTH, CONV_W), 0.02),
        "conv_ln_b": nrm(ks[8], (DEPTH, CONV_W), 0.02),
        "conv_pw_w": nrm(ks[9], (DEPTH, CONV_W, CONV_W), DEEPNORM_BETA * CONV_W ** -0.5),
        "conv_pw_b": nrm(ks[10], (DEPTH, CONV_W), 0.01),
        "attn_sinks": nrm(ks[11], (DEPTH, N_HEADS), 0.5),
        "lru_conv_w": nrm(ks[12], (DEPTH, LRU_CONV_K, LRU_W), LRU_CONV_K ** -0.5),
        "lru_conv_b": nrm(ks[13], (DEPTH, LRU_W), 0.01),
        "lru_wa": nrm(ks[14], (DEPTH, LRU_HEADS, LRU_HEAD_DIM, LRU_HEAD_DIM), LRU_HEAD_DIM ** -0.5),
        "lru_ba": nrm(ks[15], (DEPTH, LRU_W), 0.01),
        "lru_wx": nrm(ks[16], (DEPTH, LRU_HEADS, LRU_HEAD_DIM, LRU_HEAD_DIM), LRU_HEAD_DIM ** -0.5),
        "lru_bx": nrm(ks[18], (DEPTH, LRU_W), 0.01),
        "lru_lambda": lam,
        "w_out": nrm(ks[19], (DEPTH, OUT_IN, D_MODEL), DEEPNORM_BETA * OUT_IN ** -0.5),
        "ln_post_g": 1.0 + nrm(ks[20], (DEPTH, D_MODEL), 0.02),
        "ln_post_b": nrm(ks[21], (DEPTH, D_MODEL), 0.02),
    }


def _fwd_reference(x, meta_tokens, ln_in_g, ln_in_b, w_in, conv_dw_w, conv_dw_b, conv_ln_g, conv_ln_b,
              conv_pw_w, conv_pw_b, attn_sinks, lru_conv_w, lru_conv_b, lru_wa, lru_ba, lru_wx,
              lru_bx, lru_lambda, w_out, ln_post_g, ln_post_b):
    B = x.shape[0]
    meta = jnp.broadcast_to(meta_tokens[None].astype(x.dtype), (B, N_META, x.shape[-1]))
    h = jnp.concatenate([meta, x], axis=1)
    h = _layer_norm(h, ln_in_g, ln_in_b)
    pos = jnp.arange(h.shape[1], dtype=jnp.int32)
    for l in range(DEPTH):
        h = _hybrid_layer(h, pos, w_in[l], conv_dw_w[l], conv_dw_b[l], conv_ln_g[l], conv_ln_b[l],
                          conv_pw_w[l], conv_pw_b[l], attn_sinks[l], lru_conv_w[l], lru_conv_b[l],
                          lru_wa[l], lru_ba[l], lru_wx[l], lru_bx[l], lru_lambda[l],
                          w_out[l], ln_post_g[l], ln_post_b[l])
    return h[:, N_META:]


import jax as _jax
import jax.numpy as _jnp

TWIN_FORMAT = 'train_step'
FWD_PARAMS = ['x', 'meta_tokens', 'ln_in_g', 'ln_in_b', 'w_in', 'conv_dw_w', 'conv_dw_b', 'conv_ln_g', 'conv_ln_b', 'conv_pw_w', 'conv_pw_b', 'attn_sinks', 'lru_conv_w', 'lru_conv_b', 'lru_wa', 'lru_ba', 'lru_wx', 'lru_bx', 'lru_lambda', 'w_out', 'ln_post_g', 'ln_post_b']
TWIN_WEIGHTS = ['meta_tokens', 'ln_in_g', 'ln_in_b', 'w_in', 'conv_dw_w', 'conv_dw_b', 'conv_ln_g', 'conv_ln_b', 'conv_pw_w', 'conv_pw_b', 'attn_sinks', 'lru_conv_w', 'lru_conv_b', 'lru_wa', 'lru_ba', 'lru_wx', 'lru_bx', 'lru_lambda', 'w_out', 'ln_post_g', 'ln_post_b']
TWIN_DIFF_INPUT = 'x'
TWIN_INPUTS = ['x', 'meta_tokens', 'ln_in_g', 'ln_in_b', 'w_in', 'conv_dw_w', 'conv_dw_b', 'conv_ln_g', 'conv_ln_b', 'conv_pw_w', 'conv_pw_b', 'attn_sinks', 'lru_conv_w', 'lru_conv_b', 'lru_wa', 'lru_ba', 'lru_wx', 'lru_bx', 'lru_lambda', 'w_out', 'ln_post_g', 'ln_post_b', 'loss_target', 'm_meta_tokens', 'm_ln_in_g', 'm_ln_in_b', 'm_w_in', 'm_conv_dw_w', 'm_conv_dw_b', 'm_conv_ln_g', 'm_conv_ln_b', 'm_conv_pw_w', 'm_conv_pw_b', 'm_attn_sinks', 'm_lru_conv_w', 'm_lru_conv_b', 'm_lru_wa', 'm_lru_ba', 'm_lru_wx', 'm_lru_bx', 'm_lru_lambda', 'm_w_out', 'm_ln_post_g', 'm_ln_post_b', 'v_meta_tokens', 'v_ln_in_g', 'v_ln_in_b', 'v_w_in', 'v_conv_dw_w', 'v_conv_dw_b', 'v_conv_ln_g', 'v_conv_ln_b', 'v_conv_pw_w', 'v_conv_pw_b', 'v_attn_sinks', 'v_lru_conv_w', 'v_lru_conv_b', 'v_lru_wa', 'v_lru_ba', 'v_lru_wx', 'v_lru_bx', 'v_lru_lambda', 'v_w_out', 'v_ln_post_g', 'v_ln_post_b']
TWIN_OUTPUTS = ['loss', 'grad_x', 'grad_meta_tokens', 'grad_ln_in_g', 'grad_ln_in_b', 'grad_w_in', 'grad_conv_dw_w', 'grad_conv_dw_b', 'grad_conv_ln_g', 'grad_conv_ln_b', 'grad_conv_pw_w', 'grad_conv_pw_b', 'grad_attn_sinks', 'grad_lru_conv_w', 'grad_lru_conv_b', 'grad_lru_wa', 'grad_lru_ba', 'grad_lru_wx', 'grad_lru_bx', 'grad_lru_lambda', 'grad_w_out', 'grad_ln_post_g', 'grad_ln_post_b', 'delta_meta_tokens', 'delta_ln_in_g', 'delta_ln_in_b', 'delta_w_in', 'delta_conv_dw_w', 'delta_conv_dw_b', 'delta_conv_ln_g', 'delta_conv_ln_b', 'delta_conv_pw_w', 'delta_conv_pw_b', 'delta_attn_sinks', 'delta_lru_conv_w', 'delta_lru_conv_b', 'delta_lru_wa', 'delta_lru_ba', 'delta_lru_wx', 'delta_lru_bx', 'delta_lru_lambda', 'delta_w_out', 'delta_ln_post_g', 'delta_ln_post_b', 'new_m_meta_tokens', 'new_m_ln_in_g', 'new_m_ln_in_b', 'new_m_w_in', 'new_m_conv_dw_w', 'new_m_conv_dw_b', 'new_m_conv_ln_g', 'new_m_conv_ln_b', 'new_m_conv_pw_w', 'new_m_conv_pw_b', 'new_m_attn_sinks', 'new_m_lru_conv_w', 'new_m_lru_conv_b', 'new_m_lru_wa', 'new_m_lru_ba', 'new_m_lru_wx', 'new_m_lru_bx', 'new_m_lru_lambda', 'new_m_w_out', 'new_m_ln_post_g', 'new_m_ln_post_b', 'new_v_meta_tokens', 'new_v_ln_in_g', 'new_v_ln_in_b', 'new_v_w_in', 'new_v_conv_dw_w', 'new_v_conv_dw_b', 'new_v_conv_ln_g', 'new_v_conv_ln_b', 'new_v_conv_pw_w', 'new_v_conv_pw_b', 'new_v_attn_sinks', 'new_v_lru_conv_w', 'new_v_lru_conv_b', 'new_v_lru_wa', 'new_v_lru_ba', 'new_v_lru_wx', 'new_v_lru_bx', 'new_v_lru_lambda', 'new_v_w_out', 'new_v_ln_post_g', 'new_v_ln_post_b']
TWIN_LEAF_KINDS = {'loss': 'loss', 'grad_x': 'grad_x', 'grad_meta_tokens': 'grad_w', 'grad_ln_in_g': 'grad_w', 'grad_ln_in_b': 'grad_w', 'grad_w_in': 'grad_w', 'grad_conv_dw_w': 'grad_w', 'grad_conv_dw_b': 'grad_w', 'grad_conv_ln_g': 'grad_w', 'grad_conv_ln_b': 'grad_w', 'grad_conv_pw_w': 'grad_w', 'grad_conv_pw_b': 'grad_w', 'grad_attn_sinks': 'grad_w', 'grad_lru_conv_w': 'grad_w', 'grad_lru_conv_b': 'grad_w', 'grad_lru_wa': 'grad_w', 'grad_lru_ba': 'grad_w', 'grad_lru_wx': 'grad_w', 'grad_lru_bx': 'grad_w', 'grad_lru_lambda': 'grad_w', 'grad_w_out': 'grad_w', 'grad_ln_post_g': 'grad_w', 'grad_ln_post_b': 'grad_w', 'delta_meta_tokens': 'delta_w', 'delta_ln_in_g': 'delta_w', 'delta_ln_in_b': 'delta_w', 'delta_w_in': 'delta_w', 'delta_conv_dw_w': 'delta_w', 'delta_conv_dw_b': 'delta_w', 'delta_conv_ln_g': 'delta_w', 'delta_conv_ln_b': 'delta_w', 'delta_conv_pw_w': 'delta_w', 'delta_conv_pw_b': 'delta_w', 'delta_attn_sinks': 'delta_w', 'delta_lru_conv_w': 'delta_w', 'delta_lru_conv_b': 'delta_w', 'delta_lru_wa': 'delta_w', 'delta_lru_ba': 'delta_w', 'delta_lru_wx': 'delta_w', 'delta_lru_bx': 'delta_w', 'delta_lru_lambda': 'delta_w', 'delta_w_out': 'delta_w', 'delta_ln_post_g': 'delta_w', 'delta_ln_post_b': 'delta_w', 'new_m_meta_tokens': 'new_m', 'new_m_ln_in_g': 'new_m', 'new_m_ln_in_b': 'new_m', 'new_m_w_in': 'new_m', 'new_m_conv_dw_w': 'new_m', 'new_m_conv_dw_b': 'new_m', 'new_m_conv_ln_g': 'new_m', 'new_m_conv_ln_b': 'new_m', 'new_m_conv_pw_w': 'new_m', 'new_m_conv_pw_b': 'new_m', 'new_m_attn_sinks': 'new_m', 'new_m_lru_conv_w': 'new_m', 'new_m_lru_conv_b': 'new_m', 'new_m_lru_wa': 'new_m', 'new_m_lru_ba': 'new_m', 'new_m_lru_wx': 'new_m', 'new_m_lru_bx': 'new_m', 'new_m_lru_lambda': 'new_m', 'new_m_w_out': 'new_m', 'new_m_ln_post_g': 'new_m', 'new_m_ln_post_b': 'new_m', 'new_v_meta_tokens': 'new_v', 'new_v_ln_in_g': 'new_v', 'new_v_ln_in_b': 'new_v', 'new_v_w_in': 'new_v', 'new_v_conv_dw_w': 'new_v', 'new_v_conv_dw_b': 'new_v', 'new_v_conv_ln_g': 'new_v', 'new_v_conv_ln_b': 'new_v', 'new_v_conv_pw_w': 'new_v', 'new_v_conv_pw_b': 'new_v', 'new_v_attn_sinks': 'new_v', 'new_v_lru_conv_w': 'new_v', 'new_v_lru_conv_b': 'new_v', 'new_v_lru_wa': 'new_v', 'new_v_lru_ba': 'new_v', 'new_v_lru_wx': 'new_v', 'new_v_lru_bx': 'new_v', 'new_v_lru_lambda': 'new_v', 'new_v_w_out': 'new_v', 'new_v_ln_post_g': 'new_v', 'new_v_ln_post_b': 'new_v'}


def _forward(args):
    return _fwd_reference(*[args[k] for k in FWD_PARAMS])


def _output_shape():
    out = _jax.eval_shape(lambda: _forward(_fwd_setup_inputs(0)))
    return out.shape, out.dtype

N_MICROBATCH = 1
ADAM_LR = 0.001
ADAM_B1 = 0.9
ADAM_B2 = 0.999
ADAM_EPS = 1e-08
ADAM_WD = 0.01
ADAM_STEP = 10
PER_EXAMPLE_BATCH_AXIS = {'x': 0, 'loss_target': 0}
SHARED_INPUTS = []
_WEIGHT_DTYPES = {'meta_tokens': _jnp.float32, 'ln_in_g': _jnp.float32, 'ln_in_b': _jnp.float32, 'w_in': _jnp.float32, 'conv_dw_w': _jnp.float32, 'conv_dw_b': _jnp.float32, 'conv_ln_g': _jnp.float32, 'conv_ln_b': _jnp.float32, 'conv_pw_w': _jnp.float32, 'conv_pw_b': _jnp.float32, 'attn_sinks': _jnp.float32, 'lru_conv_w': _jnp.float32, 'lru_conv_b': _jnp.float32, 'lru_wa': _jnp.float32, 'lru_ba': _jnp.float32, 'lru_wx': _jnp.float32, 'lru_bx': _jnp.float32, 'lru_lambda': _jnp.float32, 'w_out': _jnp.float32, 'ln_post_g': _jnp.float32, 'ln_post_b': _jnp.float32}
MOMENT_SCALE = {'meta_tokens': 7.694035e-04, 'ln_in_g': 5.368339e-01, 'ln_in_b': 2.909311e-01, 'w_in': 6.849379e-03, 'conv_dw_w': 5.812405e-03, 'conv_dw_b': 1.251699e-02, 'conv_ln_g': 6.970353e-03, 'conv_ln_b': 6.835413e-03, 'conv_pw_w': 1.171769e-02, 'conv_pw_b': 2.564736e-02, 'attn_sinks': 4.956210e-04, 'lru_conv_w': 1.616884e-02, 'lru_conv_b': 2.093493e-01, 'lru_wa': 6.288263e-03, 'lru_ba': 3.790875e-03, 'lru_wx': 1.155772e-02, 'lru_bx': 5.221132e-03, 'lru_lambda': 7.363424e-03, 'w_out': 1.659306e-02, 'ln_post_g': 1.133004e+01, 'ln_post_b': 4.054418e-01}


def _to_microbatches(a, axis):
    t = _jnp.moveaxis(a, axis, 0)
    t = t.reshape((N_MICROBATCH, t.shape[0] // N_MICROBATCH) + t.shape[1:])
    return _jnp.moveaxis(t, 1, axis + 1)


def setup_inputs(seed: int = 0) -> dict:
    inp = _fwd_setup_inputs(seed)
    key = _jax.random.fold_in(_jax.random.key(seed), 7919)
    shape, _ = _output_shape()
    out = dict(inp)
    out["loss_target"] = _jax.random.normal(_jax.random.fold_in(key, 0), shape, _jnp.float32)
    for i, name in enumerate(TWIN_WEIGHTS):
        w = inp[name].astype(_jnp.float32)
        if MOMENT_SCALE is None:
            s = _jnp.sqrt(_jnp.mean(_jnp.square(w)) + 1e-30)
        else:
            s = MOMENT_SCALE[name]
        km, kv = _jax.random.split(_jax.random.fold_in(key, i + 1))
        out[name] = w
        out["m_" + name] = s * _jax.random.normal(km, w.shape, _jnp.float32)
        out["v_" + name] = (s * s) * _jax.random.uniform(kv, w.shape, _jnp.float32, 0.5, 1.5)
    if N_MICROBATCH > 1:
        for name, axis in PER_EXAMPLE_BATCH_AXIS.items():
            out[name] = _to_microbatches(out[name], axis)
    return {'x': out['x'], 'meta_tokens': out['meta_tokens'], 'ln_in_g': out['ln_in_g'], 'ln_in_b': out['ln_in_b'], 'w_in': out['w_in'], 'conv_dw_w': out['conv_dw_w'], 'conv_dw_b': out['conv_dw_b'], 'conv_ln_g': out['conv_ln_g'], 'conv_ln_b': out['conv_ln_b'], 'conv_pw_w': out['conv_pw_w'], 'conv_pw_b': out['conv_pw_b'], 'attn_sinks': out['attn_sinks'], 'lru_conv_w': out['lru_conv_w'], 'lru_conv_b': out['lru_conv_b'], 'lru_wa': out['lru_wa'], 'lru_ba': out['lru_ba'], 'lru_wx': out['lru_wx'], 'lru_bx': out['lru_bx'], 'lru_lambda': out['lru_lambda'], 'w_out': out['w_out'], 'ln_post_g': out['ln_post_g'], 'ln_post_b': out['ln_post_b'], 'loss_target': out['loss_target'], 'm_meta_tokens': out['m_meta_tokens'], 'm_ln_in_g': out['m_ln_in_g'], 'm_ln_in_b': out['m_ln_in_b'], 'm_w_in': out['m_w_in'], 'm_conv_dw_w': out['m_conv_dw_w'], 'm_conv_dw_b': out['m_conv_dw_b'], 'm_conv_ln_g': out['m_conv_ln_g'], 'm_conv_ln_b': out['m_conv_ln_b'], 'm_conv_pw_w': out['m_conv_pw_w'], 'm_conv_pw_b': out['m_conv_pw_b'], 'm_attn_sinks': out['m_attn_sinks'], 'm_lru_conv_w': out['m_lru_conv_w'], 'm_lru_conv_b': out['m_lru_conv_b'], 'm_lru_wa': out['m_lru_wa'], 'm_lru_ba': out['m_lru_ba'], 'm_lru_wx': out['m_lru_wx'], 'm_lru_bx': out['m_lru_bx'], 'm_lru_lambda': out['m_lru_lambda'], 'm_w_out': out['m_w_out'], 'm_ln_post_g': out['m_ln_post_g'], 'm_ln_post_b': out['m_ln_post_b'], 'v_meta_tokens': out['v_meta_tokens'], 'v_ln_in_g': out['v_ln_in_g'], 'v_ln_in_b': out['v_ln_in_b'], 'v_w_in': out['v_w_in'], 'v_conv_dw_w': out['v_conv_dw_w'], 'v_conv_dw_b': out['v_conv_dw_b'], 'v_conv_ln_g': out['v_conv_ln_g'], 'v_conv_ln_b': out['v_conv_ln_b'], 'v_conv_pw_w': out['v_conv_pw_w'], 'v_conv_pw_b': out['v_conv_pw_b'], 'v_attn_sinks': out['v_attn_sinks'], 'v_lru_conv_w': out['v_lru_conv_w'], 'v_lru_conv_b': out['v_lru_conv_b'], 'v_lru_wa': out['v_lru_wa'], 'v_lru_ba': out['v_lru_ba'], 'v_lru_wx': out['v_lru_wx'], 'v_lru_bx': out['v_lru_bx'], 'v_lru_lambda': out['v_lru_lambda'], 'v_w_out': out['v_w_out'], 'v_ln_post_g': out['v_ln_post_g'], 'v_ln_post_b': out['v_ln_post_b']}


def _loss(weights, diff, rest, loss_target):
    with _jax.named_scope("forward"):
        args = {**rest, TWIN_DIFF_INPUT: diff, **{k: w.astype(_WEIGHT_DTYPES[k]) for k, w in weights.items()}}
        y = _forward(args)
    with _jax.named_scope("loss_head"):
        err = _jnp.square(y.astype(_jnp.float32) - loss_target)
        return 0.5 * _jnp.sum(_jnp.mean(err, axis=-1)) if err.ndim else 0.5 * err


def _adamw(w, g, m, v):
    m = ADAM_B1 * m + (1.0 - ADAM_B1) * g
    v = ADAM_B2 * v + (1.0 - ADAM_B2) * _jnp.square(g)
    m_hat = m / (1.0 - ADAM_B1 ** ADAM_STEP)
    v_hat = v / (1.0 - ADAM_B2 ** ADAM_STEP)
    delta = -ADAM_LR * (m_hat / (_jnp.sqrt(v_hat) + ADAM_EPS) + ADAM_WD * w)
    return delta, m, v


def reference(x, meta_tokens, ln_in_g, ln_in_b, w_in, conv_dw_w, conv_dw_b, conv_ln_g, conv_ln_b, conv_pw_w, conv_pw_b, attn_sinks, lru_conv_w, lru_conv_b, lru_wa, lru_ba, lru_wx, lru_bx, lru_lambda, w_out, ln_post_g, ln_post_b, loss_target, m_meta_tokens, m_ln_in_g, m_ln_in_b, m_w_in, m_conv_dw_w, m_conv_dw_b, m_conv_ln_g, m_conv_ln_b, m_conv_pw_w, m_conv_pw_b, m_attn_sinks, m_lru_conv_w, m_lru_conv_b, m_lru_wa, m_lru_ba, m_lru_wx, m_lru_bx, m_lru_lambda, m_w_out, m_ln_post_g, m_ln_post_b, v_meta_tokens, v_ln_in_g, v_ln_in_b, v_w_in, v_conv_dw_w, v_conv_dw_b, v_conv_ln_g, v_conv_ln_b, v_conv_pw_w, v_conv_pw_b, v_attn_sinks, v_lru_conv_w, v_lru_conv_b, v_lru_wa, v_lru_ba, v_lru_wx, v_lru_bx, v_lru_lambda, v_w_out, v_ln_post_g, v_ln_post_b):
    given = dict(x=x, meta_tokens=meta_tokens, ln_in_g=ln_in_g, ln_in_b=ln_in_b, w_in=w_in, conv_dw_w=conv_dw_w, conv_dw_b=conv_dw_b, conv_ln_g=conv_ln_g, conv_ln_b=conv_ln_b, conv_pw_w=conv_pw_w, conv_pw_b=conv_pw_b, attn_sinks=attn_sinks, lru_conv_w=lru_conv_w, lru_conv_b=lru_conv_b, lru_wa=lru_wa, lru_ba=lru_ba, lru_wx=lru_wx, lru_bx=lru_bx, lru_lambda=lru_lambda, w_out=w_out, ln_post_g=ln_post_g, ln_post_b=ln_post_b, loss_target=loss_target, m_meta_tokens=m_meta_tokens, m_ln_in_g=m_ln_in_g, m_ln_in_b=m_ln_in_b, m_w_in=m_w_in, m_conv_dw_w=m_conv_dw_w, m_conv_dw_b=m_conv_dw_b, m_conv_ln_g=m_conv_ln_g, m_conv_ln_b=m_conv_ln_b, m_conv_pw_w=m_conv_pw_w, m_conv_pw_b=m_conv_pw_b, m_attn_sinks=m_attn_sinks, m_lru_conv_w=m_lru_conv_w, m_lru_conv_b=m_lru_conv_b, m_lru_wa=m_lru_wa, m_lru_ba=m_lru_ba, m_lru_wx=m_lru_wx, m_lru_bx=m_lru_bx, m_lru_lambda=m_lru_lambda, m_w_out=m_w_out, m_ln_post_g=m_ln_post_g, m_ln_post_b=m_ln_post_b, v_meta_tokens=v_meta_tokens, v_ln_in_g=v_ln_in_g, v_ln_in_b=v_ln_in_b, v_w_in=v_w_in, v_conv_dw_w=v_conv_dw_w, v_conv_dw_b=v_conv_dw_b, v_conv_ln_g=v_conv_ln_g, v_conv_ln_b=v_conv_ln_b, v_conv_pw_w=v_conv_pw_w, v_conv_pw_b=v_conv_pw_b, v_attn_sinks=v_attn_sinks, v_lru_conv_w=v_lru_conv_w, v_lru_conv_b=v_lru_conv_b, v_lru_wa=v_lru_wa, v_lru_ba=v_lru_ba, v_lru_wx=v_lru_wx, v_lru_bx=v_lru_bx, v_lru_lambda=v_lru_lambda, v_w_out=v_w_out, v_ln_post_g=v_ln_post_g, v_ln_post_b=v_ln_post_b)
    weights = {n: given[n] for n in TWIN_WEIGHTS}
    shared = {n: given[n] for n in SHARED_INPUTS}
    per_example = {n: given[n] for n in ['x']}
    grad_fn = _jax.value_and_grad(_loss, argnums=(0, 1))

    def one_microbatch(ex, loss_target):
        ex = dict(ex)
        diff = ex.pop(TWIN_DIFF_INPUT)
        return grad_fn(weights, diff, {**shared, **ex}, loss_target)

    if N_MICROBATCH == 1:
        loss, (grad_w, grad_x) = one_microbatch(per_example, given["loss_target"])
    else:
        def body(carry, xs):
            loss_sum, grad_sum = carry
            l_k, (gw_k, gx_k) = one_microbatch(xs[0], xs[1])
            with _jax.named_scope("update"):
                return (loss_sum + l_k, _jax.tree.map(_jnp.add, grad_sum, gw_k)), gx_k

        init = (_jnp.zeros((), _jnp.float32), _jax.tree.map(_jnp.zeros_like, weights))
        (loss, grad_w), grad_x = _jax.lax.scan(body, init, (per_example, given["loss_target"]))
    with _jax.named_scope("update"):
        delta_w, new_m, new_v = {}, {}, {}
        for n in TWIN_WEIGHTS:
            delta_w[n], new_m[n], new_v[n] = _adamw(weights[n], grad_w[n], given["m_" + n], given["v_" + n])
    return (loss, grad_x, *[grad_w[n] for n in TWIN_WEIGHTS], *[delta_w[n] for n in TWIN_WEIGHTS],
            *[new_m[n] for n in TWIN_WEIGHTS], *[new_v[n] for n in TWIN_WEIGHTS])
```

```python
import functools
import math

import numpy as np
import jax
import jax.numpy as jnp
from jax import lax
from jax.experimental import pallas as pl
from jax.experimental.pallas import tpu as pltpu

F32 = jnp.float32
BF16 = jnp.bfloat16

D_MODEL = 2048
DEPTH = 2
N_META = 16
TB = 128
PAD0 = TB - N_META
CONV_W = 512
CONV_K = 31
HEAD_DIM = 64
N_HEADS = 16
N_KV = 4
GROUP = 4
ATT_W = 1024
KV_W = 256
ROT_DIM = 16
ROPE_THETA = 500000.0
LRU_W = 512
LRU_HEADS = 8
LRU_CONV_K = 4
LRU_C = 8.0
IN_TOTAL = 5120
N_DEV = 8
W_IN_SHARD = IN_TOTAL // N_DEV
LN_EPS = 1e-5
ALPHA = (2.0 * DEPTH) ** 0.25
NEG_INF = -1e30
ATT_SCALE = HEAD_DIM ** -0.5

ADAM_LR = 0.001
ADAM_B1 = 0.9
ADAM_B2 = 0.999
ADAM_EPS = 1e-08
ADAM_WD = 0.01
ADAM_STEP = 10

VMEM_LIMIT = 56 * 1024 * 1024

COL_CV, COL_CG, COL_CGATE = 0, 1, 2
COL_Q0 = 3
COL_K256 = 10
COL_V256 = 11
COL_AGATE1024 = 3
COL_RX, COL_RGATE = 8, 9
YC_CONV, YC_LRU = 2, 3


def _wout_block(kk):
    return jnp.where(kk < 2, kk + 1, jnp.where(kk == 2, 0, 3))


def _cp(n_axes, vmem=VMEM_LIMIT):
    return pltpu.CompilerParams(dimension_semantics=("arbitrary",) * n_axes, vmem_limit_bytes=vmem)


def _row_tile(lp, max_blocks):
    nb = lp // TB
    d = max(k for k in range(1, max_blocks + 1) if nb % k == 0)
    return TB * d


def _sig(x):
    return jax.nn.sigmoid(x)


def _dsilu(x, s):
    return s * (1.0 + x * (1.0 - s))


def _ln_core(z):
    mu = jnp.mean(z, axis=-1, keepdims=True)
    zc = z - mu
    var = jnp.mean(zc * zc, axis=-1, keepdims=True)
    rstd = lax.rsqrt(var + LN_EPS)
    return zc * rstd, rstd


def _ln_bwd_core(dy, xh, rstd, g):
    dxh = dy * g
    m1 = jnp.mean(dxh, axis=-1, keepdims=True)
    m2 = jnp.mean(dxh * xh, axis=-1, keepdims=True)
    return rstd * (dxh - m1 - xh * m2)


def _row_ids(shape, base):
    return lax.broadcasted_iota(jnp.int32, shape, 0) + base


def _colsum(x):
    return jnp.sum(x, axis=0, keepdims=True)


def _dot(a, b, dims):
    return lax.dot_general(a, b, (dims, ((), ())), preferred_element_type=F32)


NN = ((1,), (0,))
NT = ((1,), (1,))
TN = ((0,), (0,))


def _embed(x, meta_full):
    s = x.shape[1]
    lp = s + TB
    nb = lp // TB

    def body(x_ref, m_ref, o_ref):
        i = pl.program_id(0)

        @pl.when(i == 0)
        def _():
            o_ref[0:PAD0, :] = jnp.zeros((PAD0, D_MODEL), F32)
            o_ref[PAD0:TB, :] = m_ref[...]

        @pl.when(i > 0)
        def _():
            o_ref[...] = x_ref[...]

    return pl.pallas_call(
        body, name="embed", grid=(nb,),
        in_specs=[pl.BlockSpec((None, TB, D_MODEL), lambda i: (0, jnp.maximum(i - 1, 0), 0)),
                  pl.BlockSpec((N_META, D_MODEL), lambda i: (0, 0))],
        out_specs=pl.BlockSpec((TB, D_MODEL), lambda i: (i, 0)),
        out_shape=jax.ShapeDtypeStruct((lp, D_MODEL), F32),
        compiler_params=_cp(1),
    )(x, meta_full)


def _ln_fwd(name, z, g, b):
    lp = z.shape[0]
    tr = _row_tile(lp, 3)

    def body(z_ref, g_ref, b_ref, o_ref):
        i = pl.program_id(0)
        xh, _ = _ln_core(z_ref[...])
        h = xh * g_ref[...] + b_ref[...]
        rows = _row_ids(h.shape, i * tr)
        o_ref[...] = jnp.where(rows >= PAD0, h, 0.0).astype(BF16)

    return pl.pallas_call(
        body, name=name, grid=(lp // tr,),
        in_specs=[pl.BlockSpec((tr, D_MODEL), lambda i: (i, 0)),
                  pl.BlockSpec((1, D_MODEL), lambda i: (0, 0)),
                  pl.BlockSpec((1, D_MODEL), lambda i: (0, 0))],
        out_specs=pl.BlockSpec((tr, D_MODEL), lambda i: (i, 0)),
        out_shape=jax.ShapeDtypeStruct((lp, D_MODEL), BF16),
        compiler_params=_cp(1),
    )(z, g, b)


def _loss_head(z, target, g, b):
    lp = z.shape[0]
    nb = lp // TB

    def body(z_ref, t_ref, g_ref, b_ref, dz_ref, st_ref, loss_ref):
        i = pl.program_id(0)

        @pl.when(i == 0)
        def _():
            st_ref[...] = jnp.zeros(st_ref.shape, F32)
            loss_ref[...] = jnp.zeros(loss_ref.shape, F32)
            dz_ref[...] = jnp.zeros(dz_ref.shape, F32)

        @pl.when(i > 0)
        def _():
            xh, rstd = _ln_core(z_ref[...])
            gg = g_ref[...]
            y = xh * gg + b_ref[...]
            e = y - t_ref[...]
            part = 0.5 * jnp.sum(jnp.mean(e * e, axis=-1, keepdims=True), axis=0, keepdims=True)
            loss_ref[...] += jnp.broadcast_to(part, loss_ref.shape)
            dy = e / float(D_MODEL)
            st_ref[0:1, :] += _colsum(dy * xh)
            st_ref[1:2, :] += _colsum(dy)
            dz_ref[...] = _ln_bwd_core(dy, xh, rstd, gg)

    return pl.pallas_call(
        body, name="loss_head", grid=(nb,),
        in_specs=[pl.BlockSpec((TB, D_MODEL), lambda i: (i, 0)),
                  pl.BlockSpec((None, TB, D_MODEL), lambda i: (0, jnp.maximum(i - 1, 0), 0)),
                  pl.BlockSpec((1, D_MODEL), lambda i: (0, 0)),
                  pl.BlockSpec((1, D_MODEL), lambda i: (0, 0))],
        out_specs=[pl.BlockSpec((TB, D_MODEL), lambda i: (i, 0)),
                   pl.BlockSpec((8, D_MODEL), lambda i: (0, 0)),
                   pl.BlockSpec((8, 128), lambda i: (0, 0))],
        out_shape=[jax.ShapeDtypeStruct((lp, D_MODEL), F32),
                   jax.ShapeDtypeStruct((8, D_MODEL), F32),
                   jax.ShapeDtypeStruct((8, 128), F32)],
        compiler_params=_cp(1),
    )(z, target, g, b)


def _ln_bwd(name, dh, z, g):
    lp = z.shape[0]
    nb = lp // TB

    def body(dh_ref, z_ref, g_ref, dz_ref, st_ref):
        i = pl.program_id(0)

        @pl.when(i == 0)
        def _():
            st_ref[...] = jnp.zeros(st_ref.shape, F32)

        xh, rstd = _ln_core(z_ref[...])
        rows = _row_ids(xh.shape, i * TB)
        dy = jnp.where(rows >= PAD0, dh_ref[...], 0.0)
        st_ref[0:1, :] += _colsum(dy * xh)
        st_ref[1:2, :] += _colsum(dy)
        dz_ref[...] = _ln_bwd_core(dy, xh, rstd, g_ref[...])

    return pl.pallas_call(
        body, name=name, grid=(nb,),
        in_specs=[pl.BlockSpec((TB, D_MODEL), lambda i: (i, 0)),
                  pl.BlockSpec((TB, D_MODEL), lambda i: (i, 0)),
                  pl.BlockSpec((1, D_MODEL), lambda i: (0, 0))],
        out_specs=[pl.BlockSpec((TB, D_MODEL), lambda i: (i, 0)),
                   pl.BlockSpec((8, D_MODEL), lambda i: (0, 0))],
        out_shape=[jax.ShapeDtypeStruct((lp, D_MODEL), F32),
                   jax.ShapeDtypeStruct((8, D_MODEL), F32)],
        compiler_params=_cp(1),
    )(dh, z, g)


def _ln_bwd_input(dh, z, g):
    lp = z.shape[0]
    nb = lp // TB
    s = lp - TB

    def body(dh_ref, z_ref, g_ref, gx_ref, gm_ref, st_ref):
        i = pl.program_id(0)

        @pl.when(i == 0)
        def _():
            st_ref[...] = jnp.zeros(st_ref.shape, F32)

        xh, rstd = _ln_core(z_ref[...])
        rows = _row_ids(xh.shape, i * TB)
        dy = jnp.where(rows >= PAD0, dh_ref[...], 0.0)
        st_ref[0:1, :] += _colsum(dy * xh)
        st_ref[1:2, :] += _colsum(dy)
        dz = _ln_bwd_core(dy, xh, rstd, g_ref[...])
        gx_ref[...] = dz

        @pl.when(i == 0)
        def _():
            gm_ref[...] = dz[PAD0:TB, :]

    return pl.pallas_call(
        body, name="ln_in_bwd", grid=(nb,),
        in_specs=[pl.BlockSpec((TB, D_MODEL), lambda i: (i, 0)),
                  pl.BlockSpec((TB, D_MODEL), lambda i: (i, 0)),
                  pl.BlockSpec((1, D_MODEL), lambda i: (0, 0))],
        out_specs=[pl.BlockSpec((None, TB, D_MODEL), lambda i: (0, jnp.maximum(i - 1, 0), 0)),
                   pl.BlockSpec((N_META, D_MODEL), lambda i: (0, 0)),
                   pl.BlockSpec((8, D_MODEL), lambda i: (0, 0))],
        out_shape=[jax.ShapeDtypeStruct((1, s, D_MODEL), F32),
                   jax.ShapeDtypeStruct((N_META, D_MODEL), F32),
                   jax.ShapeDtypeStruct((8, D_MODEL), F32)],
        compiler_params=_cp(1),
    )(dh, z, g)


def _mm_proj(name, hb, wg_in, layer):
    lp = hb.shape[0]
    tm = lp // 4

    def body(a_ref, b_ref, o_ref):
        o_ref[...] = _dot(a_ref[...], b_ref[...], NN)

    return pl.pallas_call(
        body, name=name, grid=(4, N_DEV),
        in_specs=[pl.BlockSpec((tm, D_MODEL), lambda i, j: (i, 0)),
                  pl.BlockSpec((None, None, D_MODEL, W_IN_SHARD), lambda i, j: (j, layer, 0, 0))],
        out_specs=pl.BlockSpec((tm, W_IN_SHARD), lambda i, j: (i, j)),
        out_shape=jax.ShapeDtypeStruct((lp, IN_TOTAL), F32),
        compiler_params=_cp(2),
    )(hb, wg_in)


def _mm_out(name, ycat, wout, z, g, b):
    lp = ycat.shape[0]
    tm = lp // 6
    nk = 4

    def body(a_ref, w_ref, z_ref, g_ref, b_ref, o_ref, acc_ref):
        i = pl.program_id(0)
        k = pl.program_id(1)

        @pl.when(k == 0)
        def _():
            acc_ref[...] = jnp.zeros(acc_ref.shape, F32)

        acc_ref[...] += _dot(a_ref[...], w_ref[...], NN)

        @pl.when(k == nk - 1)
        def _():
            xh, _ = _ln_core(z_ref[...])
            h = xh * g_ref[...] + b_ref[...]
            rows = _row_ids(h.shape, i * tm)
            h = jnp.where(rows >= PAD0, h, 0.0)
            o_ref[...] = ALPHA * h + acc_ref[...]

    return pl.pallas_call(
        body, name=name, grid=(6, nk),
        in_specs=[pl.BlockSpec((tm, 512), lambda i, k: (i, k)),
                  pl.BlockSpec((512, D_MODEL), lambda i, k: (_wout_block(k), 0)),
                  pl.BlockSpec((tm, D_MODEL), lambda i, k: (i, 0)),
                  pl.BlockSpec((1, D_MODEL), lambda i, k: (0, 0)),
                  pl.BlockSpec((1, D_MODEL), lambda i, k: (0, 0))],
        out_specs=pl.BlockSpec((tm, D_MODEL), lambda i, k: (i, 0)),
        out_shape=jax.ShapeDtypeStruct((lp, D_MODEL), F32),
        scratch_shapes=[pltpu.VMEM((tm, D_MODEL), F32)],
        compiler_params=_cp(2),
    )(ycat, wout, z, g, b)


def _mm_dycat(name, dz, wout):
    lp = dz.shape[0]
    tm = lp // 6

    def body(a_ref, w_ref, o_ref):
        o_ref[...] = _dot(a_ref[...].astype(BF16), w_ref[...], NT).astype(BF16)

    return pl.pallas_call(
        body, name=name, grid=(6, 4),
        in_specs=[pl.BlockSpec((tm, D_MODEL), lambda i, j: (i, 0)),
                  pl.BlockSpec((512, D_MODEL), lambda i, j: (_wout_block(j), 0))],
        out_specs=pl.BlockSpec((tm, 512), lambda i, j: (i, j)),
        out_shape=jax.ShapeDtypeStruct((lp, D_MODEL), BF16),
        compiler_params=_cp(2),
    )(dz, wout)


def _mm_dwout(name, ycat, dz):
    lp = ycat.shape[0]
    tk = _row_tile(lp, 3)
    nk = lp // tk
    wblk = (1, 2, 0, 3)

    def body(a_ref, b_ref, o_ref, acc_ref):
        k = pl.program_id(0)

        @pl.when(k == 0)
        def _():
            acc_ref[...] = jnp.zeros(acc_ref.shape, F32)

        acc_ref[...] += _dot(a_ref[...], b_ref[...].astype(BF16), TN)

        @pl.when(k == nk - 1)
        def _():
            for kk in range(4):
                o_ref[wblk[kk] * 512:(wblk[kk] + 1) * 512, :] = (
                    acc_ref[kk * 512:(kk + 1) * 512, :].astype(BF16))

    return pl.pallas_call(
        body, name=name, grid=(nk,),
        in_specs=[pl.BlockSpec((tk, D_MODEL), lambda k: (k, 0)),
                  pl.BlockSpec((tk, D_MODEL), lambda k: (k, 0))],
        out_specs=pl.BlockSpec((D_MODEL, D_MODEL), lambda k: (0, 0)),
        out_shape=jax.ShapeDtypeStruct((D_MODEL, D_MODEL), BF16),
        scratch_shapes=[pltpu.VMEM((D_MODEL, D_MODEL), F32)],
        compiler_params=_cp(1),
    )(ycat, dz)


def _mm_dwin(name, hb, dproj):
    lp = hb.shape[0]
    tk = _row_tile(lp, 3)
    nk = lp // tk

    def body(a_ref, b_ref, o_ref, acc_ref):
        k = pl.program_id(1)

        @pl.when(k == 0)
        def _():
            acc_ref[...] = jnp.zeros(acc_ref.shape, F32)

        acc_ref[...] += _dot(a_ref[...], b_ref[...], TN)

        @pl.when(k == nk - 1)
        def _():
            o_ref[0] = acc_ref[:, 0:W_IN_SHARD].astype(BF16)
            o_ref[1] = acc_ref[:, W_IN_SHARD:2 * W_IN_SHARD].astype(BF16)

    return pl.pallas_call(
        body, name=name, grid=(4, nk),
        in_specs=[pl.BlockSpec((tk, D_MODEL), lambda j, k: (k, 0)),
                  pl.BlockSpec((tk, 2 * W_IN_SHARD), lambda j, k: (k, j))],
        out_specs=pl.BlockSpec((2, D_MODEL, W_IN_SHARD), lambda j, k: (j, 0, 0)),
        out_shape=jax.ShapeDtypeStruct((N_DEV, D_MODEL, W_IN_SHARD), BF16),
        scratch_shapes=[pltpu.VMEM((D_MODEL, 2 * W_IN_SHARD), F32)],
        compiler_params=_cp(2),
    )(hb, dproj)


def _mm_dh(name, dproj, wg_in, layer, dz):
    lp = dproj.shape[0]
    tm = lp // 6

    def body(a_ref, w_ref, dz_ref, o_ref, acc_ref):
        k = pl.program_id(1)

        @pl.when(k == 0)
        def _():
            acc_ref[...] = jnp.zeros(acc_ref.shape, F32)

        acc_ref[...] += _dot(a_ref[...], w_ref[...], NT)

        @pl.when(k == N_DEV - 1)
        def _():
            o_ref[...] = acc_ref[...] + ALPHA * dz_ref[...]

    return pl.pallas_call(
        body, name=name, grid=(6, N_DEV),
        in_specs=[pl.BlockSpec((tm, W_IN_SHARD), lambda i, k: (i, k)),
                  pl.BlockSpec((None, None, D_MODEL, W_IN_SHARD), lambda i, k: (k, layer, 0, 0)),
                  pl.BlockSpec((tm, D_MODEL), lambda i, k: (i, 0))],
        out_specs=pl.BlockSpec((tm, D_MODEL), lambda i, k: (i, 0)),
        out_shape=jax.ShapeDtypeStruct((lp, D_MODEL), F32),
        scratch_shapes=[pltpu.VMEM((tm, D_MODEL), F32)],
        compiler_params=_cp(2),
    )(dproj, wg_in, dz)


def _shifted_views(cat, n_shift, base, rows):
    total = cat.shape[0]
    rolled = [cat] + [pltpu.roll(cat, b, axis=0) for b in range(1, 8)]
    views = []
    for s in range(n_shift):
        a, b = divmod(s, 8)
        views.append(rolled[b][base - 8 * a: base - 8 * a + rows, :])
    del total
    return views


def _conv_chain(j, cv_ref, cg_ref, cvp_ref, cgp_ref, wdw_ref, vec_ref, wpw_ref):
    cv = cv_ref[...]
    sg = _sig(cg_ref[...])
    c0 = cv * sg
    c0p = jnp.where(j > 0, cvp_ref[...] * _sig(cgp_ref[...]), 0.0)
    cat = jnp.concatenate([c0p, c0], axis=0)
    views = _shifted_views(cat, CONV_K, TB, TB)
    c1 = jnp.broadcast_to(vec_ref[0:1, :], (TB, CONV_W))
    for k in range(CONV_K):
        c1 = c1 + wdw_ref[k:k + 1, :] * views[CONV_K - 1 - k]
    xh, rstd = _ln_core(c1)
    c2 = xh * vec_ref[1:2, :] + vec_ref[2:3, :]
    s2 = _sig(c2)
    c3 = c2 * s2
    c4 = _dot(c3.astype(BF16), wpw_ref[...], NN) + vec_ref[3:4, :]
    return dict(cv=cv, sg=sg, views=views, xh=xh, rstd=rstd, c2=c2, s2=s2, c3=c3, c4=c4)


def _conv_in_specs(jmap):
    def cur(col):
        return pl.BlockSpec((TB, 512), lambda n: (jmap(n), col))

    def prev(col):
        return pl.BlockSpec((TB, 512), lambda n: (jnp.maximum(jmap(n) - 1, 0), col))

    return [cur(COL_CV), cur(COL_CG), prev(COL_CV), prev(COL_CG), cur(COL_CGATE)]


def _conv_param_specs():
    return [pl.BlockSpec((32, CONV_W), lambda n: (0, 0)),
            pl.BlockSpec((8, CONV_W), lambda n: (0, 0)),
            pl.BlockSpec((CONV_W, CONV_W), lambda n: (0, 0))]


def _conv_fwd(name, proj, wdw, vec, wpw):
    lp = proj.shape[0]
    nb = lp // TB

    def body(cv_ref, cg_ref, cvp_ref, cgp_ref, gate_ref, wdw_ref, vec_ref, wpw_ref, o_ref):
        j = pl.program_id(0)
        c = _conv_chain(j, cv_ref, cg_ref, cvp_ref, cgp_ref, wdw_ref, vec_ref, wpw_ref)
        gate = gate_ref[...]
        o_ref[...] = (c["c4"] * (gate * _sig(gate))).astype(BF16)

    return pl.pallas_call(
        body, name=name, grid=(nb,),
        in_specs=_conv_in_specs(lambda n: n) + _conv_param_specs(),
        out_specs=pl.BlockSpec((TB, 512), lambda n: (n, YC_CONV)),
        out_shape=jax.ShapeDtypeStruct((lp, D_MODEL), BF16),
        compiler_params=_cp(1),
    )(proj, proj, proj, proj, proj, wdw, vec, wpw)


def _conv_bwd(name, proj, dycat, wdw, vec, wpw):
    lp = proj.shape[0]
    nb = lp // TB
    halo = 32

    def body(cv_ref, cg_ref, cvp_ref, cgp_ref, gate_ref, dy_ref, wdw_ref, vec_ref, wpw_ref,
             dp_ref, dwdw_ref, dvec_ref, dwpw_ref, carry_ref):
        n = pl.program_id(0)
        j = nb - 1 - n

        @pl.when(n == 0)
        def _():
            carry_ref[...] = jnp.zeros(carry_ref.shape, F32)
            dwdw_ref[...] = jnp.zeros(dwdw_ref.shape, F32)
            dvec_ref[...] = jnp.zeros(dvec_ref.shape, F32)
            dwpw_ref[...] = jnp.zeros(dwpw_ref.shape, F32)

        c = _conv_chain(j, cv_ref, cg_ref, cvp_ref, cgp_ref, wdw_ref, vec_ref, wpw_ref)
        dy = dy_ref[...].astype(F32)
        gate = gate_ref[...]
        sgate = _sig(gate)
        dc4 = dy * (gate * sgate)
        dgate = dy * c["c4"] * _dsilu(gate, sgate)
        dc4b = dc4.astype(BF16)
        dvec_ref[3:4, :] += _colsum(dc4)
        dwpw_ref[...] += _dot(c["c3"].astype(BF16), dc4b, TN)
        dc3 = _dot(dc4b, wpw_ref[...], NT)
        dc2 = dc3 * _dsilu(c["c2"], c["s2"])
        dvec_ref[1:2, :] += _colsum(dc2 * c["xh"])
        dvec_ref[2:3, :] += _colsum(dc2)
        dc1 = _ln_bwd_core(dc2, c["xh"], c["rstd"], vec_ref[1:2, :])
        dvec_ref[0:1, :] += _colsum(dc1)
        for k in range(CONV_K):
            dwdw_ref[k:k + 1, :] += _colsum(dc1 * c["views"][CONV_K - 1 - k])
        dcat = jnp.concatenate([dc1, carry_ref[...]], axis=0)
        total = TB + halo
        up = [dcat] + [pltpu.roll(dcat, total - b, axis=0) for b in range(1, 8)]
        dc0 = jnp.zeros((TB, CONV_W), F32)
        for k in range(CONV_K):
            a, b = divmod(CONV_K - 1 - k, 8)
            dc0 = dc0 + wdw_ref[k:k + 1, :] * up[b][8 * a: 8 * a + TB, :]
        carry_ref[...] = dc1[0:halo, :]
        sg = c["sg"]
        dcv = dc0 * sg
        dcg = dc0 * c["cv"] * sg * (1.0 - sg)
        dp_ref[:, 0:512] = dcv.astype(BF16)
        dp_ref[:, 512:1024] = dcg.astype(BF16)
        dp_ref[:, 1024:1536] = dgate.astype(BF16)

    jmap = lambda n: nb - 1 - n
    return pl.pallas_call(
        body, name=name, grid=(nb,),
        in_specs=(_conv_in_specs(jmap)
                  + [pl.BlockSpec((TB, 512), lambda n: (jmap(n), YC_CONV))]
                  + _conv_param_specs()),
        out_specs=[pl.BlockSpec((TB, 1536), lambda n: (jmap(n), 0)),
                   pl.BlockSpec((32, CONV_W), lambda n: (0, 0)),
                   pl.BlockSpec((8, CONV_W), lambda n: (0, 0)),
                   pl.BlockSpec((CONV_W, CONV_W), lambda n: (0, 0))],
        out_shape=[jax.ShapeDtypeStruct((lp, IN_TOTAL), BF16),
                   jax.ShapeDtypeStruct((32, CONV_W), F32),
                   jax.ShapeDtypeStruct((8, CONV_W), F32),
                   jax.ShapeDtypeStruct((CONV_W, CONV_W), F32)],
        scratch_shapes=[pltpu.VMEM((halo, CONV_W), F32)],
        compiler_params=_cp(1),
    )(proj, proj, proj, proj, proj, dycat, wdw, vec, wpw)


def _rope_tables(lp):
    half = ROT_DIM // 2
    inv_freq = ROPE_THETA ** (-jnp.arange(half, dtype=F32) / half)
    pos = (jnp.arange(lp, dtype=jnp.int32) - PAD0).astype(F32)
    ang = pos[:, None] * inv_freq[None, :]
    cos, sin = jnp.cos(ang), jnp.sin(ang)
    ones = jnp.ones((lp, HEAD_DIM - ROT_DIM), F32)
    zeros = jnp.zeros((lp, HEAD_DIM - ROT_DIM), F32)
    zh = jnp.zeros((lp, half), F32)
    c = jnp.concatenate([cos, cos, ones], axis=1)
    sa = jnp.concatenate([-sin, zh, zeros], axis=1)
    sb = jnp.concatenate([zh, sin, zeros], axis=1)
    tile = lambda t: jnp.tile(t, (1, KV_W // HEAD_DIM))
    return tile(c), tile(sa), tile(sb)


def _rot(x, c, sa, sb):
    w = x.shape[1]
    return x * c + pltpu.roll(x, w - 8, axis=1) * sa + pltpu.roll(x, 8, axis=1) * sb


def _rot_t(dy, c, sa, sb):
    w = dy.shape[1]
    return dy * c + pltpu.roll(dy * sa, 8, axis=1) + pltpu.roll(dy * sb, w - 8, axis=1)


def _rope_fwd(name, proj, tabs):
    lp = proj.shape[0]
    tr = _row_tile(lp, 3)

    def body(q0_ref, q1_ref, k_ref, c_ref, sa_ref, sb_ref, qr_ref, kr_ref):
        c, sa, sb = c_ref[...], sa_ref[...], sb_ref[...]
        c2 = jnp.concatenate([c, c], axis=1)
        sa2 = jnp.concatenate([sa, sa], axis=1)
        sb2 = jnp.concatenate([sb, sb], axis=1)
        qr_ref[:, 0:512] = _rot(q0_ref[...], c2, sa2, sb2).astype(BF16)
        qr_ref[:, 512:1024] = _rot(q1_ref[...], c2, sa2, sb2).astype(BF16)
        kr_ref[...] = _rot(k_ref[...], c, sa, sb).astype(BF16)

    tab = pl.BlockSpec((tr, KV_W), lambda i: (i, 0))
    return pl.pallas_call(
        body, name=name, grid=(lp // tr,),
        in_specs=[pl.BlockSpec((tr, 512), lambda i: (i, COL_Q0)),
                  pl.BlockSpec((tr, 512), lambda i: (i, COL_Q0 + 1)),
                  pl.BlockSpec((tr, KV_W), lambda i: (i, COL_K256)),
                  tab, tab, tab],
        out_specs=[pl.BlockSpec((tr, ATT_W), lambda i: (i, 0)),
                   pl.BlockSpec((tr, KV_W), lambda i: (i, 0))],
        out_shape=[jax.ShapeDtypeStruct((lp, ATT_W), BF16),
                   jax.ShapeDtypeStruct((lp, KV_W), BF16)],
        compiler_params=_cp(1),
    )(proj, proj, proj, *tabs)


def _attn_mask(j):
    qi = lax.broadcasted_iota(jnp.int32, (TB, 3 * TB), 0)
    cc = lax.broadcasted_iota(jnp.int32, (TB, 3 * TB), 1)
    jj = cc & (TB - 1)
    is_meta = jj >= PAD0
    p0 = (cc < TB) & is_meta & (j >= 1)
    p1 = (cc >= TB) & (cc < 2 * TB) & (jj > qi) & (j >= 2)
    p2 = (cc >= 2 * TB) & (jj <= qi) & ((j >= 1) | is_meta)
    return p0 | p1 | p2


def _lane_group(rows):
    return lax.broadcasted_iota(jnp.int32, (rows, KV_W), 1) // HEAD_DIM


def _rolled4(x):
    return [x] + [pltpu.roll(x, HEAD_DIM * s, axis=1) for s in range(1, 4)]


def _attn_specs(jmap):
    blk = lambda col: pl.BlockSpec((TB, KV_W), lambda n: (jmap(n), col))
    prv = lambda col: pl.BlockSpec((TB, KV_W), lambda n: (jnp.maximum(jmap(n) - 1, 0), col))
    met = lambda col: pl.BlockSpec((TB, KV_W), lambda n: (0, col))
    return dict(
        qr=pl.BlockSpec((TB, ATT_W), lambda n: (jmap(n), 0)),
        k=[met(0), prv(0), blk(0)],
        v=[met(COL_V256), prv(COL_V256), blk(COL_V256)],
        gate=pl.BlockSpec((TB, ATT_W), lambda n: (jmap(n), COL_AGATE1024)),
        sinks=pl.BlockSpec((8, 128), lambda n: (0, 0)),
    )


def _attn_head(j, qkv, kz, vz, sink, valid):
    s = _dot(qkv, kz, NT) * ATT_SCALE
    s = jnp.where(valid, s, NEG_INF)
    m = jnp.maximum(jnp.max(s, axis=-1, keepdims=True), sink)
    e = jnp.exp(s - m)
    es = jnp.exp(sink - m)
    inv = 1.0 / (jnp.sum(e, axis=-1, keepdims=True) + es)
    return e * inv, es * inv


def _attn_fwd(name, qr, kr, proj, sinks_row, ycat):
    lp = proj.shape[0]
    nb = lp // TB
    sp = _attn_specs(lambda n: n)

    def body(qr_ref, km_ref, kp_ref, kc_ref, vm_ref, vp_ref, vc_ref, gate_ref, sink_ref, yin_ref, o_ref):
        del yin_ref
        j = pl.program_id(0)
        valid = _attn_mask(j)
        kall = jnp.concatenate([km_ref[...], kp_ref[...], kc_ref[...]], axis=0).astype(F32)
        vall = jnp.concatenate([vm_ref[...], vp_ref[...], vc_ref[...]], axis=0)
        vall = vall.astype(BF16).astype(F32)
        kro, vro = _rolled4(kall), _rolled4(vall)
        lg = _lane_group(3 * TB)
        sinks = sink_ref[0:1, :]
        lane = lax.broadcasted_iota(jnp.int32, (1, 128), 1)
        for kv in range(N_KV):
            qkv = qr_ref[:, kv * KV_W:(kv + 1) * KV_W]
            acc = jnp.zeros((TB, KV_W), F32)
            for g in range(GROUP):
                sh = (g - kv) % 4
                kz = jnp.where(lg == g, kro[sh], 0.0).astype(BF16)
                vz = jnp.where(lg == g, vro[sh], 0.0).astype(BF16)
                sink = jnp.sum(jnp.where(lane == kv * GROUP + g, sinks, 0.0), axis=1, keepdims=True)
                p, _ = _attn_head(j, qkv, kz, vz, sink, valid)
                acc = acc + _dot(p.astype(BF16), vz, NN)
            gate = gate_ref[:, kv * KV_W:(kv + 1) * KV_W]
            o_ref[:, kv * KV_W:(kv + 1) * KV_W] = (acc * (gate * _sig(gate))).astype(BF16)

    return pl.pallas_call(
        body, name=name, grid=(nb,),
        in_specs=[sp["qr"]] + sp["k"] + sp["v"] + [sp["gate"], sp["sinks"],
                                                   pl.BlockSpec(memory_space=pl.ANY)],
        out_specs=pl.BlockSpec((TB, ATT_W), lambda n: (n, 0)),
        out_shape=jax.ShapeDtypeStruct((lp, D_MODEL), BF16),
        input_output_aliases={9: 0},
        compiler_params=_cp(1),
    )(qr, kr, kr, kr, proj, proj, proj, proj, sinks_row, ycat)


def _attn_bwd(name, qr, kr, proj, sinks_row, dycat):
    lp = proj.shape[0]
    nb = lp // TB
    sp = _attn_specs(lambda n: n)

    def body(qr_ref, km_ref, kp_ref, kc_ref, vm_ref, vp_ref, vc_ref, gate_ref, sink_ref, dy_ref,
             dq_ref, dgate_ref, dk_ref, dv_ref, dsink_ref):
        j = pl.program_id(0)

        @pl.when(j == 0)
        def _():
            dk_ref[...] = jnp.zeros(dk_ref.shape, F32)
            dv_ref[...] = jnp.zeros(dv_ref.shape, F32)
            dsink_ref[...] = jnp.zeros(dsink_ref.shape, F32)

        valid = _attn_mask(j)
        kall = jnp.concatenate([km_ref[...], kp_ref[...], kc_ref[...]], axis=0).astype(F32)
        vall = jnp.concatenate([vm_ref[...], vp_ref[...], vc_ref[...]], axis=0)
        vall = vall.astype(BF16).astype(F32)
        kro, vro = _rolled4(kall), _rolled4(vall)
        lg = _lane_group(3 * TB)
        lgq = _lane_group(TB)
        sinks = sink_ref[0:1, :]
        lane = lax.broadcasted_iota(jnp.int32, (1, 128), 1)
        dks = [jnp.zeros((3 * TB, KV_W), F32) for _ in range(4)]
        dvs = [jnp.zeros((3 * TB, KV_W), F32) for _ in range(4)]
        dsink = jnp.zeros((1, 128), F32)
        for kv in range(N_KV):
            cols = slice(kv * KV_W, (kv + 1) * KV_W)
            qkv = qr_ref[:, cols]
            gate = gate_ref[:, cols]
            sgate = _sig(gate)
            dy = dy_ref[:, cols].astype(F32)
            dout = dy * (gate * sgate)
            doutb = dout.astype(BF16)
            ps, psinks = [], []
            att = jnp.zeros((TB, KV_W), F32)
            for g in range(GROUP):
                sh = (g - kv) % 4
                kz = jnp.where(lg == g, kro[sh], 0.0).astype(BF16)
                vz = jnp.where(lg == g, vro[sh], 0.0).astype(BF16)
                sink = jnp.sum(jnp.where(lane == kv * GROUP + g, sinks, 0.0), axis=1, keepdims=True)
                p, psink = _attn_head(j, qkv, kz, vz, sink, valid)
                ps.append(p)
                psinks.append(psink)
                att = att + _dot(p.astype(BF16), vz, NN)
            dgate_ref[:, cols] = (dy * att * _dsilu(gate, sgate)).astype(BF16)
            dd = dout * att
            dq = jnp.zeros((TB, KV_W), F32)
            for g in range(GROUP):
                sh = (g - kv) % 4
                kz = jnp.where(lg == g, kro[sh], 0.0).astype(BF16)
                vz = jnp.where(lg == g, vro[sh], 0.0).astype(BF16)
                dp = _dot(doutb, vz, NT)
                dg = jnp.sum(jnp.where(lgq == g, dd, 0.0), axis=1, keepdims=True)
                ds = (ps[g] * (dp - dg) * ATT_SCALE).astype(BF16)
                hsel = lane == kv * GROUP + g
                dsink = dsink - jnp.where(hsel, jnp.sum(psinks[g] * dg, axis=0, keepdims=True), 0.0)
                dq = dq + _dot(ds, kz, NN)
                dks[sh] = dks[sh] + jnp.where(lg == g, _dot(ds, qkv, TN), 0.0)
                dvs[sh] = dvs[sh] + jnp.where(lg == g, _dot(ps[g].astype(BF16), doutb, TN), 0.0)
            dq_ref[:, cols] = dq
        dkall, dvall = dks[0], dvs[0]
        for sh in range(1, 4):
            dkall = dkall + pltpu.roll(dks[sh], KV_W - HEAD_DIM * sh, axis=1)
            dvall = dvall + pltpu.roll(dvs[sh], KV_W - HEAD_DIM * sh, axis=1)
        dsink_ref[0:1, :] += dsink
        prev = pl.multiple_of(jnp.maximum(j - 1, 0) * TB, TB)
        cur = pl.multiple_of(j * TB, TB)
        dk_ref[0:TB, :] += dkall[0:TB]
        dv_ref[0:TB, :] += dvall[0:TB]
        dk_ref[pl.ds(prev, TB), :] += dkall[TB:2 * TB]
        dv_ref[pl.ds(prev, TB), :] += dvall[TB:2 * TB]
        dk_ref[pl.ds(cur, TB), :] += dkall[2 * TB:3 * TB]
        dv_ref[pl.ds(cur, TB), :] += dvall[2 * TB:3 * TB]

    return pl.pallas_call(
        body, name=name, grid=(nb,),
        in_specs=[sp["qr"]] + sp["k"] + sp["v"] + [sp["gate"], sp["sinks"],
                                                   pl.BlockSpec((TB, ATT_W), lambda n: (n, 0))],
        out_specs=[pl.BlockSpec((TB, ATT_W), lambda n: (n, 0)),
                   pl.BlockSpec((TB, ATT_W), lambda n: (n, 0)),
                   pl.BlockSpec((lp, KV_W), lambda n: (0, 0)),
                   pl.BlockSpec((lp, KV_W), lambda n: (0, 0)),
                   pl.BlockSpec((8, 128), lambda n: (0, 0))],
        out_shape=[jax.ShapeDtypeStruct((lp, ATT_W), F32),
                   jax.ShapeDtypeStruct((lp, ATT_W), BF16),
                   jax.ShapeDtypeStruct((lp, KV_W), F32),
                   jax.ShapeDtypeStruct((lp, KV_W), F32),
                   jax.ShapeDtypeStruct((8, 128), F32)],
        compiler_params=_cp(1),
    )(qr, kr, kr, kr, proj, proj, proj, proj, sinks_row, dycat)


def _attn_assemble(name, dq, dgate, dk, dv, tabs, dproj):
    lp = dq.shape[0]
    nb = lp // TB

    def body(dq_ref, dg_ref, dk_ref, dv_ref, c_ref, sa_ref, sb_ref, din_ref, o_ref):
        del din_ref
        cidx = pl.program_id(1)
        c, sa, sb = c_ref[...], sa_ref[...], sb_ref[...]

        @pl.when(cidx < 2)
        def _():
            c2 = jnp.concatenate([c, c], axis=1)
            sa2 = jnp.concatenate([sa, sa], axis=1)
            sb2 = jnp.concatenate([sb, sb], axis=1)
            o_ref[...] = _rot_t(dq_ref[...], c2, sa2, sb2).astype(BF16)

        @pl.when(cidx == 2)
        def _():
            o_ref[:, 0:KV_W] = _rot_t(dk_ref[...], c, sa, sb).astype(BF16)
            o_ref[:, KV_W:2 * KV_W] = dv_ref[...].astype(BF16)

        @pl.when(cidx > 2)
        def _():
            o_ref[...] = dg_ref[...]

    tab = pl.BlockSpec((TB, KV_W), lambda n, c: (n, 0))
    return pl.pallas_call(
        body, name=name, grid=(nb, 5),
        in_specs=[pl.BlockSpec((TB, 512), lambda n, c: (n, jnp.minimum(c, 1))),
                  pl.BlockSpec((TB, 512), lambda n, c: (n, jnp.clip(c - 3, 0, 1))),
                  pl.BlockSpec((TB, KV_W), lambda n, c: (n, 0)),
                  pl.BlockSpec((TB, KV_W), lambda n, c: (n, 0)),
                  tab, tab, tab,
                  pl.BlockSpec(memory_space=pl.ANY)],
        out_specs=pl.BlockSpec((TB, 512), lambda n, c: (n, COL_Q0 + c)),
        out_shape=jax.ShapeDtypeStruct((lp, IN_TOTAL), BF16),
        input_output_aliases={7: 0},
        compiler_params=_cp(2),
    )(dq, dgate, dk, dv, *tabs, dproj)


def _softplus_neg(lam):
    t = jnp.exp(-jnp.abs(lam))
    u = 1.0 + t
    den = jnp.where(u == 1.0, 1.0, u - 1.0)
    l1p = jnp.where(u == 1.0, t, jnp.log(u) * (t / den))
    return jnp.maximum(-lam, 0.0) + l1p


def _lru_chain(j, rx_ref, rxp_ref, wl_ref, vec_ref, wa_ref, wx_ref):
    rx = rx_ref[...]
    rxp = jnp.where(j > 0, rxp_ref[...], 0.0)
    cat = jnp.concatenate([rxp, rx], axis=0)
    views = [cat[8:8 + TB, :]] + [pltpu.roll(cat, s, axis=0)[8:8 + TB, :] for s in range(1, LRU_CONV_K)]
    x1 = jnp.broadcast_to(vec_ref[0:1, :], (TB, LRU_W))
    for k in range(LRU_CONV_K):
        x1 = x1 + wl_ref[k:k + 1, :] * views[LRU_CONV_K - 1 - k]
    x1b = x1.astype(BF16)
    r = _sig(_dot(x1b, wa_ref[...], NN) + vec_ref[1:2, :])
    ig = _sig(_dot(x1b, wx_ref[...], NN) + vec_ref[2:3, :])
    sp = _softplus_neg(vec_ref[3:4, :])
    log_a = -LRU_C * r * sp
    rows = _row_ids((TB, LRU_W), j * TB)
    live = rows >= PAD0
    a = jnp.where(live, jnp.exp(log_a), 0.0)
    y2 = 2.0 * log_a
    em = -jnp.tanh(0.5 * y2) * (jnp.exp(y2) + 1.0)
    mult = jnp.sqrt(em)
    return dict(views=views, x1=x1, x1b=x1b, r=r, ig=ig, sp=sp, a=a, mult=mult, live=live, a_raw=jnp.exp(log_a))


def _lru_specs(jmap):
    return [pl.BlockSpec((TB, 512), lambda n: (jmap(n), COL_RX)),
            pl.BlockSpec((8, 512), lambda n: (jnp.maximum(jmap(n) * (TB // 8) - 1, 0), COL_RX)),
            pl.BlockSpec((TB, 512), lambda n: (jmap(n), COL_RGATE))]


def _lru_param_specs():
    return [pl.BlockSpec((8, LRU_W), lambda n: (0, 0)),
            pl.BlockSpec((8, LRU_W), lambda n: (0, 0)),
            pl.BlockSpec((LRU_W, LRU_W), lambda n: (0, 0)),
            pl.BlockSpec((LRU_W, LRU_W), lambda n: (0, 0))]


def _lru_fwd(name, proj, wl, vec, wa, wx, ycat):
    lp = proj.shape[0]
    nb = lp // TB

    def body(rx_ref, rxp_ref, gate_ref, wl_ref, vec_ref, wa_ref, wx_ref, yin_ref, o_ref, h_ref, carry_ref):
        del yin_ref
        j = pl.program_id(0)

        @pl.when(j == 0)
        def _():
            carry_ref[...] = jnp.zeros(carry_ref.shape, F32)

        c = _lru_chain(j, rx_ref, rxp_ref, wl_ref, vec_ref, wa_ref, wx_ref)
        a = c["a"]
        u = jnp.where(c["live"], c["mult"] * (c["ig"] * c["x1"]), 0.0)
        rows = lax.broadcasted_iota(jnp.int32, (TB, LRU_W), 0)
        d = 1
        while d < TB:
            ap = jnp.where(rows >= d, pltpu.roll(a, d, axis=0), 1.0)
            up = jnp.where(rows >= d, pltpu.roll(u, d, axis=0), 0.0)
            u = a * up + u
            a = a * ap
            d *= 2
        h = u + a * carry_ref[0:1, :]
        carry_ref[...] = h[TB - 8:TB, :]
        carry_ref[0:1, :] = h[TB - 1:TB, :]
        h_ref[...] = h
        gate = gate_ref[...]
        o_ref[...] = (h * (gate * _sig(gate))).astype(BF16)

    return pl.pallas_call(
        body, name=name, grid=(nb,),
        in_specs=_lru_specs(lambda n: n) + _lru_param_specs() + [pl.BlockSpec(memory_space=pl.ANY)],
        out_specs=[pl.BlockSpec((TB, 512), lambda n: (n, YC_LRU)),
                   pl.BlockSpec((TB, LRU_W), lambda n: (n, 0))],
        out_shape=[jax.ShapeDtypeStruct((lp, D_MODEL), BF16),
                   jax.ShapeDtypeStruct((lp, LRU_W), F32)],
        input_output_aliases={7: 0},
        scratch_shapes=[pltpu.VMEM((8, LRU_W), F32)],
        compiler_params=_cp(1),
    )(proj, proj, proj, wl, vec, wa, wx, ycat)


def _lru_bwd(name, proj, dycat, hstate, wl, vec, wa, wx, dproj):
    lp = proj.shape[0]
    nb = lp // TB

    def body(rx_ref, rxp_ref, gate_ref, dy_ref, h_ref, hp_ref, wl_ref, vec_ref, wa_ref, wx_ref, din_ref,
             dp_ref, dwl_ref, dvec_ref, dwa_ref, dwx_ref, dhc_ref, anx_ref, dxc_ref):
        del din_ref
        n = pl.program_id(0)
        j = nb - 1 - n

        @pl.when(n == 0)
        def _():
            dhc_ref[...] = jnp.zeros(dhc_ref.shape, F32)
            anx_ref[...] = jnp.zeros(anx_ref.shape, F32)
            dxc_ref[...] = jnp.zeros(dxc_ref.shape, F32)
            dwl_ref[...] = jnp.zeros(dwl_ref.shape, F32)
            dvec_ref[...] = jnp.zeros(dvec_ref.shape, F32)
            dwa_ref[...] = jnp.zeros(dwa_ref.shape, F32)
            dwx_ref[...] = jnp.zeros(dwx_ref.shape, F32)

        c = _lru_chain(j, rx_ref, rxp_ref, wl_ref, vec_ref, wa_ref, wx_ref)
        a, mult, r, ig, x1, live = c["a"], c["mult"], c["r"], c["ig"], c["x1"], c["live"]
        h = h_ref[...]
        gate = gate_ref[...]
        sgate = _sig(gate)
        dy = dy_ref[...].astype(F32)
        gsum = dy * (gate * sgate)
        dgate = dy * h * _dsilu(gate, sgate)
        rows = lax.broadcasted_iota(jnp.int32, (TB, LRU_W), 0)
        bb = jnp.where(rows == TB - 1, anx_ref[0:1, :], pltpu.roll(a, TB - 1, axis=0))
        gg = gsum
        d = 1
        while d < TB:
            keep = rows < TB - d
            bn = jnp.where(keep, pltpu.roll(bb, TB - d, axis=0), 1.0)
            gn = jnp.where(keep, pltpu.roll(gg, TB - d, axis=0), 0.0)
            gg = gg + bb * gn
            bb = bb * bn
            d *= 2
        dh = gg + bb * dhc_ref[0:1, :]
        dhc_ref[...] = dh[0:8, :]
        anx_ref[...] = a[0:8, :]
        hprev = jnp.where(rows == 0, jnp.where(j > 0, hp_ref[7:8, :], 0.0), pltpu.roll(h, 1, axis=0))
        du = jnp.where(live, dh, 0.0)
        da = jnp.where(live, dh * hprev, 0.0)
        ar = c["a_raw"]
        dmult = du * (ig * x1)
        di = du * mult * x1
        dx1 = du * mult * ig
        dloga = da * ar - dmult * ar * ar / mult
        dr = dloga * (-LRU_C * c["sp"])
        dvec_ref[3:4, :] += _colsum(dloga * (-LRU_C * r))
        dza = dr * r * (1.0 - r)
        dzx = di * ig * (1.0 - ig)
        dzab, dzxb = dza.astype(BF16), dzx.astype(BF16)
        dvec_ref[1:2, :] += _colsum(dza)
        dvec_ref[2:3, :] += _colsum(dzx)
        dwa_ref[...] += _dot(c["x1b"], dzab, TN)
        dwx_ref[...] += _dot(c["x1b"], dzxb, TN)
        dx1 = dx1 + _dot(dzab, wa_ref[...], NT) + _dot(dzxb, wx_ref[...], NT)
        dvec_ref[0:1, :] += _colsum(dx1)
        for k in range(LRU_CONV_K):
            dwl_ref[k:k + 1, :] += _colsum(dx1 * c["views"][LRU_CONV_K - 1 - k])
        dcat = jnp.concatenate([dx1, dxc_ref[...]], axis=0)
        drx = jnp.zeros((TB, LRU_W), F32)
        for k in range(LRU_CONV_K):
            s = LRU_CONV_K - 1 - k
            view = dcat[0:TB, :] if s == 0 else pltpu.roll(dcat, TB + 8 - s, axis=0)[0:TB, :]
            drx = drx + wl_ref[k:k + 1, :] * view
        dxc_ref[...] = dx1[0:8, :]
        dp_ref[:, 0:512] = drx.astype(BF16)
        dp_ref[:, 512:1024] = dgate.astype(BF16)

        @pl.when(n == nb - 1)
        def _():
            lam = vec_ref[3:4, :]
            dvec_ref[3:4, :] = dvec_ref[3:4, :] * (-_sig(-lam))

    jmap = lambda n: nb - 1 - n
    return pl.pallas_call(
        body, name=name, grid=(nb,),
        in_specs=(_lru_specs(jmap)
                  + [pl.BlockSpec((TB, 512), lambda n: (jmap(n), YC_LRU)),
                     pl.BlockSpec((TB, LRU_W), lambda n: (jmap(n), 0)),
                     pl.BlockSpec((8, LRU_W), lambda n: (jnp.maximum(jmap(n) * (TB // 8) - 1, 0), 0))]
                  + _lru_param_specs() + [pl.BlockSpec(memory_space=pl.ANY)]),
        out_specs=[pl.BlockSpec((TB, 1024), lambda n: (jmap(n), 4)),
                   pl.BlockSpec((8, LRU_W), lambda n: (0, 0)),
                   pl.BlockSpec((8, LRU_W), lambda n: (0, 0)),
                   pl.BlockSpec((LRU_W, LRU_W), lambda n: (0, 0)),
                   pl.BlockSpec((LRU_W, LRU_W), lambda n: (0, 0))],
        out_shape=[jax.ShapeDtypeStruct((lp, IN_TOTAL), BF16),
                   jax.ShapeDtypeStruct((8, LRU_W), F32),
                   jax.ShapeDtypeStruct((8, LRU_W), F32),
                   jax.ShapeDtypeStruct((LRU_W, LRU_W), F32),
                   jax.ShapeDtypeStruct((LRU_W, LRU_W), F32)],
        input_output_aliases={10: 0},
        scratch_shapes=[pltpu.VMEM((8, LRU_W), F32), pltpu.VMEM((8, LRU_W), F32), pltpu.VMEM((8, LRU_W), F32)],
        compiler_params=_cp(1),
    )(proj, proj, proj, dycat, hstate, hstate, wl, vec, wa, wx, dproj)


def _exchange(name, items):
    flat_srcs, out_shapes, plan = [], [], []
    for it, (kind, srcs) in enumerate(items):
        shape = srcs[0].shape if kind == "gather" else srcs[0].shape[1:]
        out_shapes.append(jax.ShapeDtypeStruct((N_DEV, len(srcs)) + tuple(shape), srcs[0].dtype))
        for l, s in enumerate(srcs):
            plan.append((it, l, kind, len(flat_srcs)))
            flat_srcs.append(s)
    n_in, n_out, n_cp = len(flat_srcs), len(out_shapes), len(plan)

    def body(*refs):
        ins, outs = refs[:n_in], refs[n_in:n_in + n_out]
        send_sems, recv_sems, loc_sems = refs[n_in + n_out:]
        x, y, c = lax.axis_index("x"), lax.axis_index("y"), lax.axis_index("c")
        me = 4 * x + 2 * y + c
        remote, local = [], []
        for q, (it, l, kind, si) in enumerate(plan):
            src, out = ins[si], outs[it]
            for k in range(1, N_DEV):
                px = 1 - x if k & 4 else x
                py = 1 - y if k & 2 else y
                pc = 1 - c if k & 1 else c
                peer = 4 * px + 2 * py + pc
                cp = pltpu.make_async_remote_copy(
                    src_ref=src if kind == "gather" else src.at[peer],
                    dst_ref=out.at[me, l],
                    send_sem=send_sems.at[q * 7 + k - 1], recv_sem=recv_sems.at[q * 7 + k - 1],
                    device_id=(px, py, pc), device_id_type=pl.DeviceIdType.MESH)
                cp.start()
                remote.append(cp)
            lc = pltpu.make_async_copy(src if kind == "gather" else src.at[me], out.at[me, l], loc_sems.at[q])
            lc.start()
            local.append(lc)
        for lc in local:
            lc.wait()
        for cp in remote:
            cp.wait_send()
            cp.wait_recv()

    anyspec = pl.BlockSpec(memory_space=pl.ANY)
    outs = pl.pallas_call(
        body, name=name,
        in_specs=[anyspec] * n_in, out_specs=[anyspec] * n_out, out_shape=out_shapes,
        scratch_shapes=[pltpu.SemaphoreType.DMA((n_cp * 7,)), pltpu.SemaphoreType.DMA((n_cp * 7,)),
                        pltpu.SemaphoreType.DMA((n_cp,))],
    )(*flat_srcs)
    return list(outs)


def _adamw(name, w, m, v, recv):
    r, cdim = w.shape
    tr = r
    for cand in (512, 256, 128, 64, 32, 16, 8):
        if r % cand == 0 and r > cand:
            tr = cand
            break

    def body(w_ref, m_ref, v_ref, r_ref, g_ref, d_ref, mo_ref, vo_ref):
        g = r_ref[0].astype(F32)
        for s in range(1, N_DEV):
            g = g + r_ref[s].astype(F32)
        mn = ADAM_B1 * m_ref[...] + (1.0 - ADAM_B1) * g
        vn = ADAM_B2 * v_ref[...] + (1.0 - ADAM_B2) * (g * g)
        m_hat = mn / (1.0 - ADAM_B1 ** ADAM_STEP)
        v_hat = vn / (1.0 - ADAM_B2 ** ADAM_STEP)
        g_ref[...] = g
        d_ref[...] = -ADAM_LR * (m_hat / (jnp.sqrt(v_hat) + ADAM_EPS) + ADAM_WD * w_ref[...])
        mo_ref[...] = mn
        vo_ref[...] = vn

    blk = pl.BlockSpec((tr, cdim), lambda i: (i, 0))
    return pl.pallas_call(
        body, name=name, grid=(r // tr,),
        in_specs=[blk, blk, blk, pl.BlockSpec((N_DEV, tr, cdim), lambda i: (0, i, 0))],
        out_specs=[blk, blk, blk, blk],
        out_shape=[jax.ShapeDtypeStruct((r, cdim), F32)] * 4,
        compiler_params=_cp(1),
    )(w, m, v, recv)


def _pack_rows(arrs, lead=()):
    n = len(lead)
    flat = jnp.concatenate([a.reshape(a.shape[:n] + (-1,)) for a in arrs], axis=-1)
    size = flat.shape[-1]
    padded = -(-size // 1024) * 1024
    flat = jnp.pad(flat, [(0, 0)] * n + [(0, padded - size)])
    return flat.reshape(flat.shape[:n] + (padded // 128, 128))


def _unpack_rows(packed, shapes, lead=()):
    n = len(lead)
    flat = packed.reshape(packed.shape[:n] + (-1,))
    out, off = [], 0
    for s in shapes:
        size = int(np.prod(s))
        out.append(flat[..., off:off + size].reshape(packed.shape[:n] + tuple(s)))
        off += size
    return out


def _block_diag(w):
    eye = jnp.eye(LRU_HEADS, dtype=w.dtype)
    return (eye[:, None, :, None] * w[:, :, None, :]).reshape(LRU_W, LRU_W)


def _diag_blocks(dense):
    t = dense.reshape(LRU_HEADS, 64, LRU_HEADS, 64)
    return jnp.stack([t[h, :, h, :] for h in range(LRU_HEADS)], axis=0)


def _cols_to_slots(full):
    lead = full.shape[:-1]
    t = full.reshape(lead + (N_DEV, full.shape[-1] // N_DEV))
    return jnp.moveaxis(t, -2, 0)


def _slots_to_cols(slots):
    t = jnp.moveaxis(slots, 0, -2)
    return t.reshape(t.shape[:-2] + (t.shape[-2] * t.shape[-1],))


def kernel(x, meta_tokens, ln_in_g, ln_in_b, w_in, conv_dw_w, conv_dw_b, conv_ln_g, conv_ln_b, conv_pw_w, conv_pw_b, attn_sinks, lru_conv_w, lru_conv_b, lru_wa, lru_ba, lru_wx, lru_bx, lru_lambda, w_out, ln_post_g, ln_post_b, loss_target, m_meta_tokens, m_ln_in_g, m_ln_in_b, m_w_in, m_conv_dw_w, m_conv_dw_b, m_conv_ln_g, m_conv_ln_b, m_conv_pw_w, m_conv_pw_b, m_attn_sinks, m_lru_conv_w, m_lru_conv_b, m_lru_wa, m_lru_ba, m_lru_wx, m_lru_bx, m_lru_lambda, m_w_out, m_ln_post_g, m_ln_post_b, v_meta_tokens, v_ln_in_g, v_ln_in_b, v_w_in, v_conv_dw_w, v_conv_dw_b, v_conv_ln_g, v_conv_ln_b, v_conv_pw_w, v_conv_pw_b, v_attn_sinks, v_lru_conv_w, v_lru_conv_b, v_lru_wa, v_lru_ba, v_lru_wx, v_lru_bx, v_lru_lambda, v_w_out, v_ln_post_g, v_ln_post_b):
    seq = x.shape[1]
    lp = seq + TB
    row = lambda a: a.reshape(1, -1)

    small_shard_shapes = [conv_dw_w.shape, lru_conv_w.shape, meta_tokens.shape]
    small_shard = _pack_rows([conv_dw_w, lru_conv_w, meta_tokens])
    wg_in, wg_out, wg_pw, wg_small = _exchange("gather_weights", [
        ("gather", [w_in.astype(BF16)]),
        ("gather", [w_out.astype(BF16)]),
        ("gather", [conv_pw_w.astype(BF16)]),
        ("gather", [small_shard]),
    ])
    wg_in = wg_in[:, 0]
    wg_out = wg_out[:, 0]
    wg_pw = wg_pw[:, 0]
    g_dw, g_lc, g_meta = _unpack_rows(wg_small[:, 0], small_shard_shapes, lead=(N_DEV,))
    conv_dw_full = _slots_to_cols(g_dw)
    lru_conv_full = _slots_to_cols(g_lc)
    meta_full = _slots_to_cols(g_meta)

    tabs = _rope_tables(lp)
    ln_g = [ln_in_g, ln_post_g[0], ln_post_g[1]]
    ln_b = [ln_in_b, ln_post_b[0], ln_post_b[1]]

    def layer_params(l):
        wdw = jnp.pad(conv_dw_full[l], ((0, 1), (0, 0)))
        cvec = jnp.pad(jnp.stack([conv_dw_b[l], conv_ln_g[l], conv_ln_b[l], conv_pw_b[l]]), ((0, 4), (0, 0)))
        wpw = wg_pw[:, l].reshape(CONV_W, CONV_W)
        sinks = jnp.pad(attn_sinks[l].reshape(1, N_HEADS), ((0, 7), (0, 128 - N_HEADS)))
        wl = jnp.pad(lru_conv_full[l], ((0, 4), (0, 0)))
        lvec = jnp.pad(jnp.stack([lru_conv_b[l], lru_ba[l], lru_bx[l], lru_lambda[l]]), ((0, 4), (0, 0)))
        wa = _block_diag(lru_wa[l]).astype(BF16)
        wx = _block_diag(lru_wx[l]).astype(BF16)
        wout = wg_out[:, l].reshape(D_MODEL, D_MODEL)
        return dict(wdw=wdw, cvec=cvec, wpw=wpw, sinks=sinks, wl=wl, lvec=lvec, wa=wa, wx=wx, wout=wout)

    params = [layer_params(l) for l in range(DEPTH)]

    z = [_embed(x, meta_full)]
    saved = []
    for l in range(DEPTH):
        p = params[l]
        hb = _ln_fwd(f"ln_fwd{l}", z[l], row(ln_g[l]), row(ln_b[l]))
        proj = _mm_proj(f"proj{l}", hb, wg_in, l)
        ycat = _conv_fwd(f"conv_fwd{l}", proj, p["wdw"], p["cvec"], p["wpw"])
        qr, kr = _rope_fwd(f"rope{l}", proj, tabs)
        ycat = _attn_fwd(f"attn_fwd{l}", qr, kr, proj, p["sinks"], ycat)
        ycat, hstate = _lru_fwd(f"lru_fwd{l}", proj, p["wl"], p["lvec"], p["wa"], p["wx"], ycat)
        z.append(_mm_out(f"out{l}", ycat, p["wout"], z[l], row(ln_g[l]), row(ln_b[l])))
        saved.append(dict(hb=hb, proj=proj, ycat=ycat, qr=qr, kr=kr, hstate=hstate))

    dz, st_post1, loss_blk = _loss_head(z[DEPTH], loss_target, row(ln_g[DEPTH]), row(ln_b[DEPTH]))
    loss = lax.psum(loss_blk[0, 0], ("x", "y", "c"))

    ln_stats = {DEPTH: st_post1}
    g_layers = [None] * DEPTH
    dwin_l, dwout_l = [None] * DEPTH, [None] * DEPTH
    grad_x = gmeta = None
    for l in reversed(range(DEPTH)):
        p, s = params[l], saved[l]
        dycat = _mm_dycat(f"dycat{l}", dz, p["wout"])
        dwout_l[l] = _mm_dwout(f"dwout{l}", s["ycat"], dz)
        dproj, dwdw, dcvec, dwpw = _conv_bwd(f"conv_bwd{l}", s["proj"], dycat, p["wdw"], p["cvec"], p["wpw"])
        dq, dgate, dk, dv, dsink = _attn_bwd(f"attn_bwd{l}", s["qr"], s["kr"], s["proj"], p["sinks"], dycat)
        dproj = _attn_assemble(f"attn_asm{l}", dq, dgate, dk, dv, tabs, dproj)
        dproj, dwl, dlvec, dwa, dwx = _lru_bwd(f"lru_bwd{l}", s["proj"], dycat, s["hstate"],
                                                p["wl"], p["lvec"], p["wa"], p["wx"], dproj)
        dwin_l[l] = _mm_dwin(f"dwin{l}", s["hb"], dproj)
        dh = _mm_dh(f"dh{l}", dproj, wg_in, l, dz)
        if l > 0:
            dz, ln_stats[l] = _ln_bwd(f"ln_bwd{l}", dh, z[l], row(ln_g[l]))
        else:
            grad_x, gmeta, ln_stats[0] = _ln_bwd_input(dh, z[0], row(ln_g[0]))
        g_layers[l] = dict(dwdw=dwdw[:CONV_K], dcvec=dcvec, dwpw=dwpw, dsink=dsink[0, :N_HEADS],
                           dwl=dwl[:LRU_CONV_K], dlvec=dlvec, dwa=_diag_blocks(dwa), dwx=_diag_blocks(dwx))

    stack = lambda f: jnp.stack([f(g_layers[l]) for l in range(DEPTH)])
    g_local = dict(
        ln_in_g=ln_stats[0][0], ln_in_b=ln_stats[0][1],
        conv_dw_b=stack(lambda g: g["dcvec"][0]), conv_ln_g=stack(lambda g: g["dcvec"][1]),
        conv_ln_b=stack(lambda g: g["dcvec"][2]), conv_pw_b=stack(lambda g: g["dcvec"][3]),
        attn_sinks=stack(lambda g: g["dsink"]),
        lru_conv_b=stack(lambda g: g["dlvec"][0]), lru_wa=stack(lambda g: g["dwa"]),
        lru_ba=stack(lambda g: g["dlvec"][1]), lru_wx=stack(lambda g: g["dwx"]),
        lru_bx=stack(lambda g: g["dlvec"][2]), lru_lambda=stack(lambda g: g["dlvec"][3]),
        ln_post_g=jnp.stack([ln_stats[1][0], ln_stats[2][0]]),
        ln_post_b=jnp.stack([ln_stats[1][1], ln_stats[2][1]]),
    )
    rep_names = ["ln_in_g", "ln_in_b", "conv_dw_b", "conv_ln_g", "conv_ln_b", "conv_pw_b", "attn_sinks",
                 "lru_conv_b", "lru_wa", "lru_ba", "lru_wx", "lru_bx", "lru_lambda", "ln_post_g", "ln_post_b"]
    weights = dict(meta_tokens=meta_tokens, ln_in_g=ln_in_g, ln_in_b=ln_in_b, w_in=w_in, conv_dw_w=conv_dw_w,
                   conv_dw_b=conv_dw_b, conv_ln_g=conv_ln_g, conv_ln_b=conv_ln_b, conv_pw_w=conv_pw_w,
                   conv_pw_b=conv_pw_b, attn_sinks=attn_sinks, lru_conv_w=lru_conv_w, lru_conv_b=lru_conv_b,
                   lru_wa=lru_wa, lru_ba=lru_ba, lru_wx=lru_wx, lru_bx=lru_bx, lru_lambda=lru_lambda,
                   w_out=w_out, ln_post_g=ln_post_g, ln_post_b=ln_post_b)
    mom1 = dict(meta_tokens=m_meta_tokens, ln_in_g=m_ln_in_g, ln_in_b=m_ln_in_b, w_in=m_w_in, conv_dw_w=m_conv_dw_w,
                conv_dw_b=m_conv_dw_b, conv_ln_g=m_conv_ln_g, conv_ln_b=m_conv_ln_b, conv_pw_w=m_conv_pw_w,
                conv_pw_b=m_conv_pw_b, attn_sinks=m_attn_sinks, lru_conv_w=m_lru_conv_w, lru_conv_b=m_lru_conv_b,
                lru_wa=m_lru_wa, lru_ba=m_lru_ba, lru_wx=m_lru_wx, lru_bx=m_lru_bx, lru_lambda=m_lru_lambda,
                w_out=m_w_out, ln_post_g=m_ln_post_g, ln_post_b=m_ln_post_b)
    mom2 = dict(meta_tokens=v_meta_tokens, ln_in_g=v_ln_in_g, ln_in_b=v_ln_in_b, w_in=v_w_in, conv_dw_w=v_conv_dw_w,
                conv_dw_b=v_conv_dw_b, conv_ln_g=v_conv_ln_g, conv_ln_b=v_conv_ln_b, conv_pw_w=v_conv_pw_w,
                conv_pw_b=v_conv_pw_b, attn_sinks=v_attn_sinks, lru_conv_w=v_lru_conv_w, lru_conv_b=v_lru_conv_b,
                lru_wa=v_lru_wa, lru_ba=v_lru_ba, lru_wx=v_lru_wx, lru_bx=v_lru_bx, lru_lambda=v_lru_lambda,
                w_out=v_w_out, ln_post_g=v_ln_post_g, ln_post_b=v_ln_post_b)

    rep_pack = _pack_rows([g_local[n] for n in rep_names])
    shard_small_names = ["conv_dw_w", "lru_conv_w", "meta_tokens"]
    g_dw_full = jnp.stack([g_layers[l]["dwdw"] for l in range(DEPTH)])
    g_lc_full = jnp.stack([g_layers[l]["dwl"] for l in range(DEPTH)])
    shard_pack = _pack_rows([_cols_to_slots(g_dw_full), _cols_to_slots(g_lc_full), _cols_to_slots(gmeta)],
                            lead=(N_DEV,))
    r_win, r_wout, r_pw, r_small, r_rep = _exchange("exchange_grads", [
        ("scatter", dwin_l),
        ("scatter", [d.reshape(N_DEV, D_MODEL // N_DEV, D_MODEL) for d in dwout_l]),
        ("scatter", [g_layers[l]["dwpw"].reshape(N_DEV, CONV_W // N_DEV, CONV_W) for l in range(DEPTH)]),
        ("scatter", [shard_pack]),
        ("gather", [rep_pack]),
    ])

    res = {}

    def flat2(a, cols):
        return a.reshape(-1, cols)

    for name_, recv, cols in (("w_in", r_win, W_IN_SHARD), ("w_out", r_wout, D_MODEL), ("conv_pw_w", r_pw, CONV_W)):
        w_ = weights[name_]
        outs = _adamw(f"adamw_{name_}", flat2(w_, cols), flat2(mom1[name_], cols), flat2(mom2[name_], cols),
                      recv.reshape(N_DEV, -1, cols))
        res[name_] = [o.reshape(w_.shape) for o in outs]

    sshapes = [weights[n].shape for n in shard_small_names]
    outs = _adamw("adamw_small_sharded",
                  _pack_rows([weights[n] for n in shard_small_names]),
                  _pack_rows([mom1[n] for n in shard_small_names]),
                  _pack_rows([mom2[n] for n in shard_small_names]),
                  r_small[:, 0])
    for k, o in enumerate(outs):
        for n, a in zip(shard_small_names, _unpack_rows(o, sshapes)):
            res.setdefault(n, [None] * 4)[k] = a

    rshapes = [weights[n].shape for n in rep_names]
    outs = _adamw("adamw_replicated",
                  _pack_rows([weights[n] for n in rep_names]),
                  _pack_rows([mom1[n] for n in rep_names]),
                  _pack_rows([mom2[n] for n in rep_names]),
                  r_rep[:, 0])
    for k, o in enumerate(outs):
        for n, a in zip(rep_names, _unpack_rows(o, rshapes)):
            res.setdefault(n, [None] * 4)[k] = a

    order = ["meta_tokens", "ln_in_g", "ln_in_b", "w_in", "conv_dw_w", "conv_dw_b", "conv_ln_g", "conv_ln_b",
             "conv_pw_w", "conv_pw_b", "attn_sinks", "lru_conv_w", "lru_conv_b", "lru_wa", "lru_ba", "lru_wx",
             "lru_bx", "lru_lambda", "w_out", "ln_post_g", "ln_post_b"]
    return (loss, grad_x,
            *[res[n][0] for n in order], *[res[n][1] for n in order],
            *[res[n][2] for n in order], *[res[n][3] for n in order])
```

```python
import functools
import math

import numpy as np
import jax
import jax.numpy as jnp
from jax import lax
from jax.experimental import pallas as pl
from jax.experimental.pallas import tpu as pltpu

F32 = jnp.float32
BF16 = jnp.bfloat16

D_MODEL = 2048
DEPTH = 2
N_META = 16
TB = 128
PAD0 = TB - N_META
CONV_W = 512
CONV_K = 31
HEAD_DIM = 64
N_HEADS = 16
N_KV = 4
GROUP = 4
ATT_W = 1024
KV_W = 256
ROT_DIM = 16
ROPE_THETA = 500000.0
LRU_W = 512
LRU_HEADS = 8
LRU_CONV_K = 4
LRU_C = 8.0
IN_TOTAL = 5120
N_DEV = 8
W_IN_SHARD = IN_TOTAL // N_DEV
LN_EPS = 1e-5
ALPHA = (2.0 * DEPTH) ** 0.25
NEG_INF = -1e30
ATT_SCALE = HEAD_DIM ** -0.5

ADAM_LR = 0.001
ADAM_B1 = 0.9
ADAM_B2 = 0.999
ADAM_EPS = 1e-08
ADAM_WD = 0.01
ADAM_STEP = 10

VMEM_LIMIT = 56 * 1024 * 1024
PACK_QUANTUM = 256 * 128

COL_CV, COL_CG, COL_CGATE = 0, 1, 2
COL_Q0 = 3
COL_K256 = 10
COL_V256 = 11
COL_AGATE1024 = 3
COL_RX, COL_RGATE = 8, 9
YC_CONV, YC_LRU = 2, 3


def _wout_block(kk):
    return jnp.where(kk < 2, kk + 1, jnp.where(kk == 2, 0, 3))


def _cp(n_axes, vmem=VMEM_LIMIT):
    return pltpu.CompilerParams(dimension_semantics=("arbitrary",) * n_axes, vmem_limit_bytes=vmem)


def _row_tile(lp, max_blocks):
    nb = lp // TB
    d = max(k for k in range(1, max_blocks + 1) if nb % k == 0)
    return TB * d


def _sig(x):
    return jax.nn.sigmoid(x)


def _dsilu(x, s):
    return s * (1.0 + x * (1.0 - s))


def _ln_core(z):
    mu = jnp.mean(z, axis=-1, keepdims=True)
    zc = z - mu
    var = jnp.mean(zc * zc, axis=-1, keepdims=True)
    rstd = lax.rsqrt(var + LN_EPS)
    return zc * rstd, rstd


def _ln_bwd_core(dy, xh, rstd, g):
    dxh = dy * g
    m1 = jnp.mean(dxh, axis=-1, keepdims=True)
    m2 = jnp.mean(dxh * xh, axis=-1, keepdims=True)
    return rstd * (dxh - m1 - xh * m2)


def _row_ids(shape, base):
    return lax.broadcasted_iota(jnp.int32, shape, 0) + base


def _colsum(x):
    return jnp.sum(x, axis=0, keepdims=True)


def _dot(a, b, dims):
    return lax.dot_general(a, b, (dims, ((), ())), preferred_element_type=F32)


NN = ((1,), (0,))
NT = ((1,), (1,))
TN = ((0,), (0,))


def _embed(x, meta_full):
    s = x.shape[1]
    lp = s + TB
    nb = lp // TB

    def body(x_ref, m_ref, o_ref):
        i = pl.program_id(0)

        @pl.when(i == 0)
        def _():
            o_ref[0:PAD0, :] = jnp.zeros((PAD0, D_MODEL), F32)
            o_ref[PAD0:TB, :] = m_ref[...]

        @pl.when(i > 0)
        def _():
            o_ref[...] = x_ref[...]

    return pl.pallas_call(
        body, name="embed", grid=(nb,),
        in_specs=[pl.BlockSpec((None, TB, D_MODEL), lambda i: (0, jnp.maximum(i - 1, 0), 0)),
                  pl.BlockSpec((N_META, D_MODEL), lambda i: (0, 0))],
        out_specs=pl.BlockSpec((TB, D_MODEL), lambda i: (i, 0)),
        out_shape=jax.ShapeDtypeStruct((lp, D_MODEL), F32),
        compiler_params=_cp(1),
    )(x, meta_full)


def _ln_fwd(name, z, g, b):
    lp = z.shape[0]
    tr = _row_tile(lp, 3)

    def body(z_ref, g_ref, b_ref, o_ref):
        i = pl.program_id(0)
        xh, _ = _ln_core(z_ref[...])
        h = xh * g_ref[...] + b_ref[...]
        rows = _row_ids(h.shape, i * tr)
        o_ref[...] = jnp.where(rows >= PAD0, h, 0.0).astype(BF16)

    return pl.pallas_call(
        body, name=name, grid=(lp // tr,),
        in_specs=[pl.BlockSpec((tr, D_MODEL), lambda i: (i, 0)),
                  pl.BlockSpec((1, D_MODEL), lambda i: (0, 0)),
                  pl.BlockSpec((1, D_MODEL), lambda i: (0, 0))],
        out_specs=pl.BlockSpec((tr, D_MODEL), lambda i: (i, 0)),
        out_shape=jax.ShapeDtypeStruct((lp, D_MODEL), BF16),
        compiler_params=_cp(1),
    )(z, g, b)


def _loss_head(z, target, g, b):
    lp = z.shape[0]
    nb = lp // TB

    def body(z_ref, t_ref, g_ref, b_ref, dz_ref, st_ref, loss_ref):
        i = pl.program_id(0)

        @pl.when(i == 0)
        def _():
            st_ref[...] = jnp.zeros(st_ref.shape, F32)
            loss_ref[...] = jnp.zeros(loss_ref.shape, F32)
            dz_ref[...] = jnp.zeros(dz_ref.shape, F32)

        @pl.when(i > 0)
        def _():
            xh, rstd = _ln_core(z_ref[...])
            gg = g_ref[...]
            y = xh * gg + b_ref[...]
            e = y - t_ref[...]
            part = 0.5 * jnp.sum(jnp.mean(e * e, axis=-1, keepdims=True), axis=0, keepdims=True)
            loss_ref[...] += jnp.broadcast_to(part, loss_ref.shape)
            dy = e / float(D_MODEL)
            st_ref[0:1, :] += _colsum(dy * xh)
            st_ref[1:2, :] += _colsum(dy)
            dz_ref[...] = _ln_bwd_core(dy, xh, rstd, gg)

    return pl.pallas_call(
        body, name="loss_head", grid=(nb,),
        in_specs=[pl.BlockSpec((TB, D_MODEL), lambda i: (i, 0)),
                  pl.BlockSpec((None, TB, D_MODEL), lambda i: (0, jnp.maximum(i - 1, 0), 0)),
                  pl.BlockSpec((1, D_MODEL), lambda i: (0, 0)),
                  pl.BlockSpec((1, D_MODEL), lambda i: (0, 0))],
        out_specs=[pl.BlockSpec((TB, D_MODEL), lambda i: (i, 0)),
                   pl.BlockSpec((8, D_MODEL), lambda i: (0, 0)),
                   pl.BlockSpec((8, 128), lambda i: (0, 0))],
        out_shape=[jax.ShapeDtypeStruct((lp, D_MODEL), F32),
                   jax.ShapeDtypeStruct((8, D_MODEL), F32),
                   jax.ShapeDtypeStruct((8, 128), F32)],
        compiler_params=_cp(1),
    )(z, target, g, b)


def _ln_bwd(name, dh, z, g):
    lp = z.shape[0]
    nb = lp // TB

    def body(dh_ref, z_ref, g_ref, dz_ref, st_ref):
        i = pl.program_id(0)

        @pl.when(i == 0)
        def _():
            st_ref[...] = jnp.zeros(st_ref.shape, F32)

        xh, rstd = _ln_core(z_ref[...])
        rows = _row_ids(xh.shape, i * TB)
        dy = jnp.where(rows >= PAD0, dh_ref[...], 0.0)
        st_ref[0:1, :] += _colsum(dy * xh)
        st_ref[1:2, :] += _colsum(dy)
        dz_ref[...] = _ln_bwd_core(dy, xh, rstd, g_ref[...])

    return pl.pallas_call(
        body, name=name, grid=(nb,),
        in_specs=[pl.BlockSpec((TB, D_MODEL), lambda i: (i, 0)),
                  pl.BlockSpec((TB, D_MODEL), lambda i: (i, 0)),
                  pl.BlockSpec((1, D_MODEL), lambda i: (0, 0))],
        out_specs=[pl.BlockSpec((TB, D_MODEL), lambda i: (i, 0)),
                   pl.BlockSpec((8, D_MODEL), lambda i: (0, 0))],
        out_shape=[jax.ShapeDtypeStruct((lp, D_MODEL), F32),
                   jax.ShapeDtypeStruct((8, D_MODEL), F32)],
        compiler_params=_cp(1),
    )(dh, z, g)


def _ln_bwd_input(dh, z, g):
    lp = z.shape[0]
    nb = lp // TB
    s = lp - TB

    def body(dh_ref, z_ref, g_ref, gx_ref, gm_ref, st_ref):
        i = pl.program_id(0)

        @pl.when(i == 0)
        def _():
            st_ref[...] = jnp.zeros(st_ref.shape, F32)

        xh, rstd = _ln_core(z_ref[...])
        rows = _row_ids(xh.shape, i * TB)
        dy = jnp.where(rows >= PAD0, dh_ref[...], 0.0)
        st_ref[0:1, :] += _colsum(dy * xh)
        st_ref[1:2, :] += _colsum(dy)
        dz = _ln_bwd_core(dy, xh, rstd, g_ref[...])
        gx_ref[...] = dz

        @pl.when(i == 0)
        def _():
            gm_ref[...] = dz[PAD0:TB, :]

    return pl.pallas_call(
        body, name="ln_in_bwd", grid=(nb,),
        in_specs=[pl.BlockSpec((TB, D_MODEL), lambda i: (i, 0)),
                  pl.BlockSpec((TB, D_MODEL), lambda i: (i, 0)),
                  pl.BlockSpec((1, D_MODEL), lambda i: (0, 0))],
        out_specs=[pl.BlockSpec((None, TB, D_MODEL), lambda i: (0, jnp.maximum(i - 1, 0), 0)),
                   pl.BlockSpec((N_META, D_MODEL), lambda i: (0, 0)),
                   pl.BlockSpec((8, D_MODEL), lambda i: (0, 0))],
        out_shape=[jax.ShapeDtypeStruct((1, s, D_MODEL), F32),
                   jax.ShapeDtypeStruct((N_META, D_MODEL), F32),
                   jax.ShapeDtypeStruct((8, D_MODEL), F32)],
        compiler_params=_cp(1),
    )(dh, z, g)


def _mm_proj(name, hb, wg_in, layer):
    lp = hb.shape[0]
    tm = lp // 4

    def body(a_ref, b_ref, o_ref):
        o_ref[...] = _dot(a_ref[...], b_ref[...], NN)

    return pl.pallas_call(
        body, name=name, grid=(4, N_DEV),
        in_specs=[pl.BlockSpec((tm, D_MODEL), lambda i, j: (i, 0)),
                  pl.BlockSpec((None, None, D_MODEL, W_IN_SHARD), lambda i, j: (j, layer, 0, 0))],
        out_specs=pl.BlockSpec((tm, W_IN_SHARD), lambda i, j: (i, j)),
        out_shape=jax.ShapeDtypeStruct((lp, IN_TOTAL), F32),
        compiler_params=_cp(2),
    )(hb, wg_in)


def _mm_out(name, ycat, wout, z, g, b):
    lp = ycat.shape[0]
    tm = lp // 6
    nk = 4

    def body(a_ref, w_ref, z_ref, g_ref, b_ref, o_ref, acc_ref):
        i = pl.program_id(0)
        k = pl.program_id(1)

        @pl.when(k == 0)
        def _():
            acc_ref[...] = jnp.zeros(acc_ref.shape, F32)

        acc_ref[...] += _dot(a_ref[...], w_ref[...], NN)

        @pl.when(k == nk - 1)
        def _():
            xh, _ = _ln_core(z_ref[...])
            h = xh * g_ref[...] + b_ref[...]
            rows = _row_ids(h.shape, i * tm)
            h = jnp.where(rows >= PAD0, h, 0.0)
            o_ref[...] = ALPHA * h + acc_ref[...]

    return pl.pallas_call(
        body, name=name, grid=(6, nk),
        in_specs=[pl.BlockSpec((tm, 512), lambda i, k: (i, k)),
                  pl.BlockSpec((512, D_MODEL), lambda i, k: (_wout_block(k), 0)),
                  pl.BlockSpec((tm, D_MODEL), lambda i, k: (i, 0)),
                  pl.BlockSpec((1, D_MODEL), lambda i, k: (0, 0)),
                  pl.BlockSpec((1, D_MODEL), lambda i, k: (0, 0))],
        out_specs=pl.BlockSpec((tm, D_MODEL), lambda i, k: (i, 0)),
        out_shape=jax.ShapeDtypeStruct((lp, D_MODEL), F32),
        scratch_shapes=[pltpu.VMEM((tm, D_MODEL), F32)],
        compiler_params=_cp(2),
    )(ycat, wout, z, g, b)


def _mm_dycat(name, dz, wout):
    lp = dz.shape[0]
    tm = lp // 6

    def body(a_ref, w_ref, o_ref):
        o_ref[...] = _dot(a_ref[...].astype(BF16), w_ref[...], NT).astype(BF16)

    return pl.pallas_call(
        body, name=name, grid=(6, 4),
        in_specs=[pl.BlockSpec((tm, D_MODEL), lambda i, j: (i, 0)),
                  pl.BlockSpec((512, D_MODEL), lambda i, j: (_wout_block(j), 0))],
        out_specs=pl.BlockSpec((tm, 512), lambda i, j: (i, j)),
        out_shape=jax.ShapeDtypeStruct((lp, D_MODEL), BF16),
        compiler_params=_cp(2),
    )(dz, wout)


def _mm_dwout(name, ycat, dz):
    lp = ycat.shape[0]
    tk = _row_tile(lp, 3)
    nk = lp // tk
    wblk = (1, 2, 0, 3)

    def body(a_ref, b_ref, o_ref, acc_ref):
        k = pl.program_id(0)

        @pl.when(k == 0)
        def _():
            acc_ref[...] = jnp.zeros(acc_ref.shape, F32)

        acc_ref[...] += _dot(a_ref[...], b_ref[...].astype(BF16), TN)

        @pl.when(k == nk - 1)
        def _():
            for kk in range(4):
                o_ref[wblk[kk] * 512:(wblk[kk] + 1) * 512, :] = (
                    acc_ref[kk * 512:(kk + 1) * 512, :].astype(BF16))

    return pl.pallas_call(
        body, name=name, grid=(nk,),
        in_specs=[pl.BlockSpec((tk, D_MODEL), lambda k: (k, 0)),
                  pl.BlockSpec((tk, D_MODEL), lambda k: (k, 0))],
        out_specs=pl.BlockSpec((D_MODEL, D_MODEL), lambda k: (0, 0)),
        out_shape=jax.ShapeDtypeStruct((D_MODEL, D_MODEL), BF16),
        scratch_shapes=[pltpu.VMEM((D_MODEL, D_MODEL), F32)],
        compiler_params=_cp(1),
    )(ycat, dz)


def _mm_dwin(name, hb, dproj):
    lp = hb.shape[0]
    tk = _row_tile(lp, 3)
    nk = lp // tk

    def body(a_ref, b_ref, o_ref, acc_ref):
        k = pl.program_id(1)

        @pl.when(k == 0)
        def _():
            acc_ref[...] = jnp.zeros(acc_ref.shape, F32)

        acc_ref[...] += _dot(a_ref[...], b_ref[...], TN)

        @pl.when(k == nk - 1)
        def _():
            o_ref[0] = acc_ref[:, 0:W_IN_SHARD].astype(BF16)
            o_ref[1] = acc_ref[:, W_IN_SHARD:2 * W_IN_SHARD].astype(BF16)

    return pl.pallas_call(
        body, name=name, grid=(4, nk),
        in_specs=[pl.BlockSpec((tk, D_MODEL), lambda j, k: (k, 0)),
                  pl.BlockSpec((tk, 2 * W_IN_SHARD), lambda j, k: (k, j))],
        out_specs=pl.BlockSpec((2, D_MODEL, W_IN_SHARD), lambda j, k: (j, 0, 0)),
        out_shape=jax.ShapeDtypeStruct((N_DEV, D_MODEL, W_IN_SHARD), BF16),
        scratch_shapes=[pltpu.VMEM((D_MODEL, 2 * W_IN_SHARD), F32)],
        compiler_params=_cp(2),
    )(hb, dproj)


def _mm_dh(name, dproj, wg_in, layer, dz):
    lp = dproj.shape[0]
    tm = lp // 6

    def body(a_ref, w_ref, dz_ref, o_ref, acc_ref):
        k = pl.program_id(1)

        @pl.when(k == 0)
        def _():
            acc_ref[...] = jnp.zeros(acc_ref.shape, F32)

        acc_ref[...] += _dot(a_ref[...], w_ref[...], NT)

        @pl.when(k == N_DEV - 1)
        def _():
            o_ref[...] = acc_ref[...] + ALPHA * dz_ref[...]

    return pl.pallas_call(
        body, name=name, grid=(6, N_DEV),
        in_specs=[pl.BlockSpec((tm, W_IN_SHARD), lambda i, k: (i, k)),
                  pl.BlockSpec((None, None, D_MODEL, W_IN_SHARD), lambda i, k: (k, layer, 0, 0)),
                  pl.BlockSpec((tm, D_MODEL), lambda i, k: (i, 0))],
        out_specs=pl.BlockSpec((tm, D_MODEL), lambda i, k: (i, 0)),
        out_shape=jax.ShapeDtypeStruct((lp, D_MODEL), F32),
        scratch_shapes=[pltpu.VMEM((tm, D_MODEL), F32)],
        compiler_params=_cp(2),
    )(dproj, wg_in, dz)


def _shifted_views(cat, n_shift, base, rows):
    total = cat.shape[0]
    rolled = [cat] + [pltpu.roll(cat, b, axis=0) for b in range(1, 8)]
    views = []
    for s in range(n_shift):
        a, b = divmod(s, 8)
        views.append(rolled[b][base - 8 * a: base - 8 * a + rows, :])
    del total
    return views


def _conv_chain(j, cv_ref, cg_ref, cvp_ref, cgp_ref, wdw_ref, vec_ref, wpw_ref):
    cv = cv_ref[...]
    sg = _sig(cg_ref[...])
    c0 = cv * sg
    c0p = jnp.where(j > 0, cvp_ref[...] * _sig(cgp_ref[...]), 0.0)
    cat = jnp.concatenate([c0p, c0], axis=0)
    views = _shifted_views(cat, CONV_K, TB, TB)
    c1 = jnp.broadcast_to(vec_ref[0:1, :], (TB, CONV_W))
    for k in range(CONV_K):
        c1 = c1 + wdw_ref[k:k + 1, :] * views[CONV_K - 1 - k]
    xh, rstd = _ln_core(c1)
    c2 = xh * vec_ref[1:2, :] + vec_ref[2:3, :]
    s2 = _sig(c2)
    c3 = c2 * s2
    c4 = _dot(c3.astype(BF16), wpw_ref[...], NN) + vec_ref[3:4, :]
    return dict(cv=cv, sg=sg, views=views, xh=xh, rstd=rstd, c2=c2, s2=s2, c3=c3, c4=c4)


def _conv_in_specs(jmap):
    def cur(col):
        return pl.BlockSpec((TB, 512), lambda n: (jmap(n), col))

    def prev(col):
        return pl.BlockSpec((TB, 512), lambda n: (jnp.maximum(jmap(n) - 1, 0), col))

    return [cur(COL_CV), cur(COL_CG), prev(COL_CV), prev(COL_CG), cur(COL_CGATE)]


def _conv_param_specs():
    return [pl.BlockSpec((32, CONV_W), lambda n: (0, 0)),
            pl.BlockSpec((8, CONV_W), lambda n: (0, 0)),
            pl.BlockSpec((CONV_W, CONV_W), lambda n: (0, 0))]


def _conv_fwd(name, proj, wdw, vec, wpw):
    lp = proj.shape[0]
    nb = lp // TB

    def body(cv_ref, cg_ref, cvp_ref, cgp_ref, gate_ref, wdw_ref, vec_ref, wpw_ref, o_ref):
        j = pl.program_id(0)
        c = _conv_chain(j, cv_ref, cg_ref, cvp_ref, cgp_ref, wdw_ref, vec_ref, wpw_ref)
        gate = gate_ref[...]
        o_ref[...] = (c["c4"] * (gate * _sig(gate))).astype(BF16)

    return pl.pallas_call(
        body, name=name, grid=(nb,),
        in_specs=_conv_in_specs(lambda n: n) + _conv_param_specs(),
        out_specs=pl.BlockSpec((TB, 512), lambda n: (n, YC_CONV)),
        out_shape=jax.ShapeDtypeStruct((lp, D_MODEL), BF16),
        compiler_params=_cp(1),
    )(proj, proj, proj, proj, proj, wdw, vec, wpw)


def _conv_bwd(name, proj, dycat, wdw, vec, wpw):
    lp = proj.shape[0]
    nb = lp // TB
    halo = 32

    def body(cv_ref, cg_ref, cvp_ref, cgp_ref, gate_ref, dy_ref, wdw_ref, vec_ref, wpw_ref,
             dp_ref, dwdw_ref, dvec_ref, dwpw_ref, carry_ref):
        n = pl.program_id(0)
        j = nb - 1 - n

        @pl.when(n == 0)
        def _():
            carry_ref[...] = jnp.zeros(carry_ref.shape, F32)
            dwdw_ref[...] = jnp.zeros(dwdw_ref.shape, F32)
            dvec_ref[...] = jnp.zeros(dvec_ref.shape, F32)
            dwpw_ref[...] = jnp.zeros(dwpw_ref.shape, F32)

        c = _conv_chain(j, cv_ref, cg_ref, cvp_ref, cgp_ref, wdw_ref, vec_ref, wpw_ref)
        dy = dy_ref[...].astype(F32)
        gate = gate_ref[...]
        sgate = _sig(gate)
        dc4 = dy * (gate * sgate)
        dgate = dy * c["c4"] * _dsilu(gate, sgate)
        dc4b = dc4.astype(BF16)
        dvec_ref[3:4, :] += _colsum(dc4)
        dwpw_ref[...] += _dot(c["c3"].astype(BF16), dc4b, TN)
        dc3 = _dot(dc4b, wpw_ref[...], NT)
        dc2 = dc3 * _dsilu(c["c2"], c["s2"])
        dvec_ref[1:2, :] += _colsum(dc2 * c["xh"])
        dvec_ref[2:3, :] += _colsum(dc2)
        dc1 = _ln_bwd_core(dc2, c["xh"], c["rstd"], vec_ref[1:2, :])
        dvec_ref[0:1, :] += _colsum(dc1)
        for k in range(CONV_K):
            dwdw_ref[k:k + 1, :] += _colsum(dc1 * c["views"][CONV_K - 1 - k])
        dcat = jnp.concatenate([dc1, carry_ref[...]], axis=0)
        total = TB + halo
        up = [dcat] + [pltpu.roll(dcat, total - b, axis=0) for b in range(1, 8)]
        dc0 = jnp.zeros((TB, CONV_W), F32)
        for k in range(CONV_K):
            a, b = divmod(CONV_K - 1 - k, 8)
            dc0 = dc0 + wdw_ref[k:k + 1, :] * up[b][8 * a: 8 * a + TB, :]
        carry_ref[...] = dc1[0:halo, :]
        sg = c["sg"]
        dcv = dc0 * sg
        dcg = dc0 * c["cv"] * sg * (1.0 - sg)
        dp_ref[:, 0:512] = dcv.astype(BF16)
        dp_ref[:, 512:1024] = dcg.astype(BF16)
        dp_ref[:, 1024:1536] = dgate.astype(BF16)

    jmap = lambda n: nb - 1 - n
    return pl.pallas_call(
        body, name=name, grid=(nb,),
        in_specs=(_conv_in_specs(jmap)
                  + [pl.BlockSpec((TB, 512), lambda n: (jmap(n), YC_CONV))]
                  + _conv_param_specs()),
        out_specs=[pl.BlockSpec((TB, 1536), lambda n: (jmap(n), 0)),
                   pl.BlockSpec((32, CONV_W), lambda n: (0, 0)),
                   pl.BlockSpec((8, CONV_W), lambda n: (0, 0)),
                   pl.BlockSpec((CONV_W, CONV_W), lambda n: (0, 0))],
        out_shape=[jax.ShapeDtypeStruct((lp, IN_TOTAL), BF16),
                   jax.ShapeDtypeStruct((32, CONV_W), F32),
                   jax.ShapeDtypeStruct((8, CONV_W), F32),
                   jax.ShapeDtypeStruct((CONV_W, CONV_W), F32)],
        scratch_shapes=[pltpu.VMEM((halo, CONV_W), F32)],
        compiler_params=_cp(1),
    )(proj, proj, proj, proj, proj, dycat, wdw, vec, wpw)


def _rope_tables(lp):
    half = ROT_DIM // 2
    inv_freq = ROPE_THETA ** (-jnp.arange(half, dtype=F32) / half)
    pos = (jnp.arange(lp, dtype=jnp.int32) - PAD0).astype(F32)
    ang = pos[:, None] * inv_freq[None, :]
    cos, sin = jnp.cos(ang), jnp.sin(ang)
    ones = jnp.ones((lp, HEAD_DIM - ROT_DIM), F32)
    zeros = jnp.zeros((lp, HEAD_DIM - ROT_DIM), F32)
    zh = jnp.zeros((lp, half), F32)
    c = jnp.concatenate([cos, cos, ones], axis=1)
    sa = jnp.concatenate([-sin, zh, zeros], axis=1)
    sb = jnp.concatenate([zh, sin, zeros], axis=1)
    tile = lambda t: jnp.tile(t, (1, KV_W // HEAD_DIM))
    return tile(c), tile(sa), tile(sb)


def _rot(x, c, sa, sb):
    w = x.shape[1]
    return x * c + pltpu.roll(x, w - 8, axis=1) * sa + pltpu.roll(x, 8, axis=1) * sb


def _rot_t(dy, c, sa, sb):
    w = dy.shape[1]
    return dy * c + pltpu.roll(dy * sa, 8, axis=1) + pltpu.roll(dy * sb, w - 8, axis=1)


def _rope_fwd(name, proj, tabs):
    lp = proj.shape[0]
    tr = _row_tile(lp, 3)

    def body(q0_ref, q1_ref, k_ref, c_ref, sa_ref, sb_ref, qr_ref, kr_ref):
        c, sa, sb = c_ref[...], sa_ref[...], sb_ref[...]
        c2 = jnp.concatenate([c, c], axis=1)
        sa2 = jnp.concatenate([sa, sa], axis=1)
        sb2 = jnp.concatenate([sb, sb], axis=1)
        qr_ref[:, 0:512] = _rot(q0_ref[...], c2, sa2, sb2).astype(BF16)
        qr_ref[:, 512:1024] = _rot(q1_ref[...], c2, sa2, sb2).astype(BF16)
        kr_ref[...] = _rot(k_ref[...], c, sa, sb).astype(BF16)

    tab = pl.BlockSpec((tr, KV_W), lambda i: (i, 0))
    return pl.pallas_call(
        body, name=name, grid=(lp // tr,),
        in_specs=[pl.BlockSpec((tr, 512), lambda i: (i, COL_Q0)),
                  pl.BlockSpec((tr, 512), lambda i: (i, COL_Q0 + 1)),
                  pl.BlockSpec((tr, KV_W), lambda i: (i, COL_K256)),
                  tab, tab, tab],
        out_specs=[pl.BlockSpec((tr, ATT_W), lambda i: (i, 0)),
                   pl.BlockSpec((tr, KV_W), lambda i: (i, 0))],
        out_shape=[jax.ShapeDtypeStruct((lp, ATT_W), BF16),
                   jax.ShapeDtypeStruct((lp, KV_W), BF16)],
        compiler_params=_cp(1),
    )(proj, proj, proj, *tabs)


def _attn_mask(j):
    qi = lax.broadcasted_iota(jnp.int32, (GROUP * TB, 3 * TB), 0) & (TB - 1)
    cc = lax.broadcasted_iota(jnp.int32, (GROUP * TB, 3 * TB), 1)
    jj = cc & (TB - 1)
    is_meta = jj >= PAD0
    p0 = (cc < TB) & is_meta & (j >= 1)
    p1 = (cc >= TB) & (cc < 2 * TB) & (jj > qi) & (j >= 2)
    p2 = (cc >= 2 * TB) & (jj <= qi) & ((j >= 1) | is_meta)
    return p0 | p1 | p2


def _lane_group(rows):
    return lax.broadcasted_iota(jnp.int32, (rows, KV_W), 1) // HEAD_DIM


def _stack_heads(x, kv, lgq):
    parts = []
    for g in range(GROUP):
        sh = ((kv - g) % GROUP) * HEAD_DIM
        moved = x if sh == 0 else pltpu.roll(x, sh, axis=1)
        parts.append(jnp.where(lgq == kv, moved, 0.0))
    return jnp.concatenate(parts, axis=0).astype(BF16)


def _unstack_heads(r, kv):
    out = None
    for g in range(GROUP):
        blk = r[g * TB:(g + 1) * TB, :]
        sh = ((g - kv) % GROUP) * HEAD_DIM
        blk = blk if sh == 0 else pltpu.roll(blk, sh, axis=1)
        out = blk if out is None else out + blk
    return out


def _sink_column(sinks, kv):
    lane = lax.broadcasted_iota(jnp.int32, (1, 128), 1)
    cols = []
    for g in range(GROUP):
        sg = jnp.sum(jnp.where(lane == kv * GROUP + g, sinks, 0.0), axis=1, keepdims=True)
        cols.append(jnp.broadcast_to(sg, (TB, 1)))
    return jnp.concatenate(cols, axis=0)


def _attn_probs(qst, km, sinkcol, valid):
    s = _dot(qst, km, NT) * ATT_SCALE
    s = jnp.where(valid, s, NEG_INF)
    m = jnp.maximum(jnp.max(s, axis=-1, keepdims=True), sinkcol)
    e = jnp.exp(s - m)
    es = jnp.exp(sinkcol - m)
    inv = 1.0 / (jnp.sum(e, axis=-1, keepdims=True) + es)
    return e * inv, es * inv


def _attn_specs(jmap):
    blk = lambda col: pl.BlockSpec((TB, KV_W), lambda n: (jmap(n), col))
    prv = lambda col: pl.BlockSpec((TB, KV_W), lambda n: (jnp.maximum(jmap(n) - 1, 0), col))
    met = lambda col: pl.BlockSpec((TB, KV_W), lambda n: (0, col))
    return dict(
        qr=pl.BlockSpec((TB, ATT_W), lambda n: (jmap(n), 0)),
        k=[met(0), prv(0), blk(0)],
        v=[met(COL_V256), prv(COL_V256), blk(COL_V256)],
        gate=pl.BlockSpec((TB, ATT_W), lambda n: (jmap(n), COL_AGATE1024)),
        sinks=pl.BlockSpec((8, 128), lambda n: (0, 0)),
    )


def _attn_fwd(name, qr, kr, proj, sinks_row, ycat):
    lp = proj.shape[0]
    nb = lp // TB
    sp = _attn_specs(lambda n: n)

    def body(qr_ref, km_ref, kp_ref, kc_ref, vm_ref, vp_ref, vc_ref, gate_ref, sink_ref, yin_ref, o_ref):
        del yin_ref
        j = pl.program_id(0)
        valid = _attn_mask(j)
        kall = jnp.concatenate([km_ref[...], kp_ref[...], kc_ref[...]], axis=0).astype(F32)
        vall = jnp.concatenate([vm_ref[...], vp_ref[...], vc_ref[...]], axis=0)
        lg = _lane_group(3 * TB)
        lgq = _lane_group(TB)
        sinks = sink_ref[0:1, :]
        for kv in range(N_KV):
            cols = slice(kv * KV_W, (kv + 1) * KV_W)
            km = jnp.where(lg == kv, kall, 0.0).astype(BF16)
            vm = jnp.where(lg == kv, vall, 0.0).astype(BF16)
            qst = _stack_heads(qr_ref[:, cols].astype(F32), kv, lgq)
            p, _ = _attn_probs(qst, km, _sink_column(sinks, kv), valid)
            att = _unstack_heads(_dot(p.astype(BF16), vm, NN), kv)
            gate = gate_ref[:, cols]
            o_ref[:, cols] = (att * (gate * _sig(gate))).astype(BF16)

    return pl.pallas_call(
        body, name=name, grid=(nb,),
        in_specs=[sp["qr"]] + sp["k"] + sp["v"] + [sp["gate"], sp["sinks"],
                                                   pl.BlockSpec(memory_space=pl.ANY)],
        out_specs=pl.BlockSpec((TB, ATT_W), lambda n: (n, 0)),
        out_shape=jax.ShapeDtypeStruct((lp, D_MODEL), BF16),
        input_output_aliases={9: 0},
        compiler_params=_cp(1),
    )(qr, kr, kr, kr, proj, proj, proj, proj, sinks_row, ycat)


def _attn_bwd(name, qr, kr, proj, sinks_row, dycat):
    lp = proj.shape[0]
    nb = lp // TB
    sp = _attn_specs(lambda n: n)

    def body(qr_ref, km_ref, kp_ref, kc_ref, vm_ref, vp_ref, vc_ref, gate_ref, sink_ref, dy_ref,
             dq_ref, dgate_ref, dk_ref, dv_ref, dsink_ref):
        j = pl.program_id(0)

        @pl.when(j == 0)
        def _():
            dk_ref[...] = jnp.zeros(dk_ref.shape, F32)
            dv_ref[...] = jnp.zeros(dv_ref.shape, F32)
            dsink_ref[...] = jnp.zeros(dsink_ref.shape, F32)

        valid = _attn_mask(j)
        kall = jnp.concatenate([km_ref[...], kp_ref[...], kc_ref[...]], axis=0).astype(F32)
        vall = jnp.concatenate([vm_ref[...], vp_ref[...], vc_ref[...]], axis=0)
        lg = _lane_group(3 * TB)
        lgq = _lane_group(TB)
        sinks = sink_ref[0:1, :]
        lane = lax.broadcasted_iota(jnp.int32, (1, 128), 1)
        dkall = jnp.zeros((3 * TB, KV_W), F32)
        dvall = jnp.zeros((3 * TB, KV_W), F32)
        dsink = jnp.zeros((1, 128), F32)
        for kv in range(N_KV):
            cols = slice(kv * KV_W, (kv + 1) * KV_W)
            km = jnp.where(lg == kv, kall, 0.0).astype(BF16)
            vm = jnp.where(lg == kv, vall, 0.0).astype(BF16)
            qst = _stack_heads(qr_ref[:, cols].astype(F32), kv, lgq)
            gate = gate_ref[:, cols]
            sgate = _sig(gate)
            dy = dy_ref[:, cols].astype(F32)
            dout = dy * (gate * sgate)
            dost = _stack_heads(dout, kv, lgq)
            p, psink = _attn_probs(qst, km, _sink_column(sinks, kv), valid)
            pb = p.astype(BF16)
            att = _unstack_heads(_dot(pb, vm, NN), kv)
            dgate_ref[:, cols] = (dy * att * _dsilu(gate, sgate)).astype(BF16)
            dd = dout * att
            dcol = jnp.concatenate(
                [jnp.sum(jnp.where(lgq == g, dd, 0.0), axis=1, keepdims=True) for g in range(GROUP)], axis=0)
            dp = _dot(dost, vm, NT)
            ds = (p * (dp - dcol) * ATT_SCALE).astype(BF16)
            pd = psink * dcol
            for g in range(GROUP):
                tot = jnp.sum(pd[g * TB:(g + 1) * TB, :], axis=0, keepdims=True)
                dsink = dsink - jnp.where(lane == kv * GROUP + g, tot, 0.0)
            dq_ref[:, cols] = _unstack_heads(_dot(ds, km, NN), kv)
            dkall = dkall + _dot(ds, qst, TN)
            dvall = dvall + _dot(pb, dost, TN)
        dsink_ref[0:1, :] += dsink
        prev = pl.multiple_of(jnp.maximum(j - 1, 0) * TB, TB)
        cur = pl.multiple_of(j * TB, TB)
        dk_ref[0:TB, :] += dkall[0:TB]
        dv_ref[0:TB, :] += dvall[0:TB]
        dk_ref[pl.ds(prev, TB), :] += dkall[TB:2 * TB]
        dv_ref[pl.ds(prev, TB), :] += dvall[TB:2 * TB]
        dk_ref[pl.ds(cur, TB), :] += dkall[2 * TB:3 * TB]
        dv_ref[pl.ds(cur, TB), :] += dvall[2 * TB:3 * TB]

    return pl.pallas_call(
        body, name=name, grid=(nb,),
        in_specs=[sp["qr"]] + sp["k"] + sp["v"] + [sp["gate"], sp["sinks"],
                                                   pl.BlockSpec((TB, ATT_W), lambda n: (n, 0))],
        out_specs=[pl.BlockSpec((TB, ATT_W), lambda n: (n, 0)),
                   pl.BlockSpec((TB, ATT_W), lambda n: (n, 0)),
                   pl.BlockSpec((lp, KV_W), lambda n: (0, 0)),
                   pl.BlockSpec((lp, KV_W), lambda n: (0, 0)),
                   pl.BlockSpec((8, 128), lambda n: (0, 0))],
        out_shape=[jax.ShapeDtypeStruct((lp, ATT_W), F32),
                   jax.ShapeDtypeStruct((lp, ATT_W), BF16),
                   jax.ShapeDtypeStruct((lp, KV_W), F32),
                   jax.ShapeDtypeStruct((lp, KV_W), F32),
                   jax.ShapeDtypeStruct((8, 128), F32)],
        compiler_params=_cp(1),
    )(qr, kr, kr, kr, proj, proj, proj, proj, sinks_row, dycat)


def _attn_assemble(name, dq, dgate, dk, dv, tabs, dproj):
    lp = dq.shape[0]
    tr = _row_tile(lp, 3)

    def body(dq_ref, dg_ref, dk_ref, dv_ref, c_ref, sa_ref, sb_ref, din_ref, o_ref):
        del din_ref
        cidx = pl.program_id(1)
        c, sa, sb = c_ref[...], sa_ref[...], sb_ref[...]

        @pl.when(cidx < 2)
        def _():
            c2 = jnp.concatenate([c, c], axis=1)
            sa2 = jnp.concatenate([sa, sa], axis=1)
            sb2 = jnp.concatenate([sb, sb], axis=1)
            o_ref[...] = _rot_t(dq_ref[...], c2, sa2, sb2).astype(BF16)

        @pl.when(cidx == 2)
        def _():
            o_ref[:, 0:KV_W] = _rot_t(dk_ref[...], c, sa, sb).astype(BF16)
            o_ref[:, KV_W:2 * KV_W] = dv_ref[...].astype(BF16)

        @pl.when(cidx > 2)
        def _():
            o_ref[...] = dg_ref[...]

    tab = pl.BlockSpec((tr, KV_W), lambda n, c: (n, 0))
    return pl.pallas_call(
        body, name=name, grid=(lp // tr, 5),
        in_specs=[pl.BlockSpec((tr, 512), lambda n, c: (n, jnp.minimum(c, 1))),
                  pl.BlockSpec((tr, 512), lambda n, c: (n, jnp.clip(c - 3, 0, 1))),
                  tab, tab,
                  tab, tab, tab,
                  pl.BlockSpec(memory_space=pl.ANY)],
        out_specs=pl.BlockSpec((tr, 512), lambda n, c: (n, COL_Q0 + c)),
        out_shape=jax.ShapeDtypeStruct((lp, IN_TOTAL), BF16),
        input_output_aliases={7: 0},
        compiler_params=_cp(2),
    )(dq, dgate, dk, dv, *tabs, dproj)


def _softplus_neg(lam):
    t = jnp.exp(-jnp.abs(lam))
    u = 1.0 + t
    den = jnp.where(u == 1.0, 1.0, u - 1.0)
    l1p = jnp.where(u == 1.0, t, jnp.log(u) * (t / den))
    return jnp.maximum(-lam, 0.0) + l1p


def _lru_chain(j, rx_ref, rxp_ref, wl_ref, vec_ref, wa_ref, wx_ref):
    rx = rx_ref[...]
    rxp = jnp.where(j > 0, rxp_ref[...], 0.0)
    cat = jnp.concatenate([rxp, rx], axis=0)
    views = [cat[8:8 + TB, :]] + [pltpu.roll(cat, s, axis=0)[8:8 + TB, :] for s in range(1, LRU_CONV_K)]
    x1 = jnp.broadcast_to(vec_ref[0:1, :], (TB, LRU_W))
    for k in range(LRU_CONV_K):
        x1 = x1 + wl_ref[k:k + 1, :] * views[LRU_CONV_K - 1 - k]
    x1b = x1.astype(BF16)
    r = _sig(_dot(x1b, wa_ref[...], NN) + vec_ref[1:2, :])
    ig = _sig(_dot(x1b, wx_ref[...], NN) + vec_ref[2:3, :])
    sp = _softplus_neg(vec_ref[3:4, :])
    log_a = -LRU_C * r * sp
    rows = _row_ids((TB, LRU_W), j * TB)
    live = rows >= PAD0
    a = jnp.where(live, jnp.exp(log_a), 0.0)
    y2 = 2.0 * log_a
    em = -jnp.tanh(0.5 * y2) * (jnp.exp(y2) + 1.0)
    mult = jnp.sqrt(em)
    return dict(views=views, x1=x1, x1b=x1b, r=r, ig=ig, sp=sp, a=a, mult=mult, live=live, a_raw=jnp.exp(log_a))


def _lru_specs(jmap):
    return [pl.BlockSpec((TB, 512), lambda n: (jmap(n), COL_RX)),
            pl.BlockSpec((8, 512), lambda n: (jnp.maximum(jmap(n) * (TB // 8) - 1, 0), COL_RX)),
            pl.BlockSpec((TB, 512), lambda n: (jmap(n), COL_RGATE))]


def _lru_param_specs():
    return [pl.BlockSpec((8, LRU_W), lambda n: (0, 0)),
            pl.BlockSpec((8, LRU_W), lambda n: (0, 0)),
            pl.BlockSpec((LRU_W, LRU_W), lambda n: (0, 0)),
            pl.BlockSpec((LRU_W, LRU_W), lambda n: (0, 0))]


def _lru_fwd(name, proj, wl, vec, wa, wx, ycat):
    lp = proj.shape[0]
    nb = lp // TB

    def body(rx_ref, rxp_ref, gate_ref, wl_ref, vec_ref, wa_ref, wx_ref, yin_ref, o_ref, h_ref, carry_ref):
        del yin_ref
        j = pl.program_id(0)

        @pl.when(j == 0)
        def _():
            carry_ref[...] = jnp.zeros(carry_ref.shape, F32)

        c = _lru_chain(j, rx_ref, rxp_ref, wl_ref, vec_ref, wa_ref, wx_ref)
        a = c["a"]
        u = jnp.where(c["live"], c["mult"] * (c["ig"] * c["x1"]), 0.0)
        rows = lax.broadcasted_iota(jnp.int32, (TB, LRU_W), 0)
        d = 1
        while d < TB:
            ap = jnp.where(rows >= d, pltpu.roll(a, d, axis=0), 1.0)
            up = jnp.where(rows >= d, pltpu.roll(u, d, axis=0), 0.0)
            u = a * up + u
            a = a * ap
            d *= 2
        h = u + a * carry_ref[0:1, :]
        carry_ref[...] = h[TB - 8:TB, :]
        carry_ref[0:1, :] = h[TB - 1:TB, :]
        h_ref[...] = h
        gate = gate_ref[...]
        o_ref[...] = (h * (gate * _sig(gate))).astype(BF16)

    return pl.pallas_call(
        body, name=name, grid=(nb,),
        in_specs=_lru_specs(lambda n: n) + _lru_param_specs() + [pl.BlockSpec(memory_space=pl.ANY)],
        out_specs=[pl.BlockSpec((TB, 512), lambda n: (n, YC_LRU)),
                   pl.BlockSpec((TB, LRU_W), lambda n: (n, 0))],
        out_shape=[jax.ShapeDtypeStruct((lp, D_MODEL), BF16),
                   jax.ShapeDtypeStruct((lp, LRU_W), F32)],
        input_output_aliases={7: 0},
        scratch_shapes=[pltpu.VMEM((8, LRU_W), F32)],
        compiler_params=_cp(1),
    )(proj, proj, proj, wl, vec, wa, wx, ycat)


def _lru_bwd(name, proj, dycat, hstate, wl, vec, wa, wx, dproj):
    lp = proj.shape[0]
    nb = lp // TB

    def body(rx_ref, rxp_ref, gate_ref, dy_ref, h_ref, hp_ref, wl_ref, vec_ref, wa_ref, wx_ref, din_ref,
             dp_ref, dwl_ref, dvec_ref, dwa_ref, dwx_ref, dhc_ref, anx_ref, dxc_ref):
        del din_ref
        n = pl.program_id(0)
        j = nb - 1 - n

        @pl.when(n == 0)
        def _():
            dhc_ref[...] = jnp.zeros(dhc_ref.shape, F32)
            anx_ref[...] = jnp.zeros(anx_ref.shape, F32)
            dxc_ref[...] = jnp.zeros(dxc_ref.shape, F32)
            dwl_ref[...] = jnp.zeros(dwl_ref.shape, F32)
            dvec_ref[...] = jnp.zeros(dvec_ref.shape, F32)
            dwa_ref[...] = jnp.zeros(dwa_ref.shape, F32)
            dwx_ref[...] = jnp.zeros(dwx_ref.shape, F32)

        c = _lru_chain(j, rx_ref, rxp_ref, wl_ref, vec_ref, wa_ref, wx_ref)
        a, mult, r, ig, x1, live = c["a"], c["mult"], c["r"], c["ig"], c["x1"], c["live"]
        h = h_ref[...]
        gate = gate_ref[...]
        sgate = _sig(gate)
        dy = dy_ref[...].astype(F32)
        gsum = dy * (gate * sgate)
        dgate = dy * h * _dsilu(gate, sgate)
        rows = lax.broadcasted_iota(jnp.int32, (TB, LRU_W), 0)
        bb = jnp.where(rows == TB - 1, anx_ref[0:1, :], pltpu.roll(a, TB - 1, axis=0))
        gg = gsum
        d = 1
        while d < TB:
            keep = rows < TB - d
            bn = jnp.where(keep, pltpu.roll(bb, TB - d, axis=0), 1.0)
            gn = jnp.where(keep, pltpu.roll(gg, TB - d, axis=0), 0.0)
            gg = gg + bb * gn
            bb = bb * bn
            d *= 2
        dh = gg + bb * dhc_ref[0:1, :]
        dhc_ref[...] = dh[0:8, :]
        anx_ref[...] = a[0:8, :]
        hprev = jnp.where(rows == 0, jnp.where(j > 0, hp_ref[7:8, :], 0.0), pltpu.roll(h, 1, axis=0))
        du = jnp.where(live, dh, 0.0)
        da = jnp.where(live, dh * hprev, 0.0)
        ar = c["a_raw"]
        dmult = du * (ig * x1)
        di = du * mult * x1
        dx1 = du * mult * ig
        dloga = da * ar - dmult * ar * ar / mult
        dr = dloga * (-LRU_C * c["sp"])
        dvec_ref[3:4, :] += _colsum(dloga * (-LRU_C * r))
        dza = dr * r * (1.0 - r)
        dzx = di * ig * (1.0 - ig)
        dzab, dzxb = dza.astype(BF16), dzx.astype(BF16)
        dvec_ref[1:2, :] += _colsum(dza)
        dvec_ref[2:3, :] += _colsum(dzx)
        dwa_ref[...] += _dot(c["x1b"], dzab, TN)
        dwx_ref[...] += _dot(c["x1b"], dzxb, TN)
        dx1 = dx1 + _dot(dzab, wa_ref[...], NT) + _dot(dzxb, wx_ref[...], NT)
        dvec_ref[0:1, :] += _colsum(dx1)
        for k in range(LRU_CONV_K):
            dwl_ref[k:k + 1, :] += _colsum(dx1 * c["views"][LRU_CONV_K - 1 - k])
        dcat = jnp.concatenate([dx1, dxc_ref[...]], axis=0)
        drx = jnp.zeros((TB, LRU_W), F32)
        for k in range(LRU_CONV_K):
            s = LRU_CONV_K - 1 - k
            view = dcat[0:TB, :] if s == 0 else pltpu.roll(dcat, TB + 8 - s, axis=0)[0:TB, :]
            drx = drx + wl_ref[k:k + 1, :] * view
        dxc_ref[...] = dx1[0:8, :]
        dp_ref[:, 0:512] = drx.astype(BF16)
        dp_ref[:, 512:1024] = dgate.astype(BF16)

        @pl.when(n == nb - 1)
        def _():
            lam = vec_ref[3:4, :]
            dvec_ref[3:4, :] = dvec_ref[3:4, :] * (-_sig(-lam))

    jmap = lambda n: nb - 1 - n
    return pl.pallas_call(
        body, name=name, grid=(nb,),
        in_specs=(_lru_specs(jmap)
                  + [pl.BlockSpec((TB, 512), lambda n: (jmap(n), YC_LRU)),
                     pl.BlockSpec((TB, LRU_W), lambda n: (jmap(n), 0)),
                     pl.BlockSpec((8, LRU_W), lambda n: (jnp.maximum(jmap(n) * (TB // 8) - 1, 0), 0))]
                  + _lru_param_specs() + [pl.BlockSpec(memory_space=pl.ANY)]),
        out_specs=[pl.BlockSpec((TB, 1024), lambda n: (jmap(n), 4)),
                   pl.BlockSpec((8, LRU_W), lambda n: (0, 0)),
                   pl.BlockSpec((8, LRU_W), lambda n: (0, 0)),
                   pl.BlockSpec((LRU_W, LRU_W), lambda n: (0, 0)),
                   pl.BlockSpec((LRU_W, LRU_W), lambda n: (0, 0))],
        out_shape=[jax.ShapeDtypeStruct((lp, IN_TOTAL), BF16),
                   jax.ShapeDtypeStruct((8, LRU_W), F32),
                   jax.ShapeDtypeStruct((8, LRU_W), F32),
                   jax.ShapeDtypeStruct((LRU_W, LRU_W), F32),
                   jax.ShapeDtypeStruct((LRU_W, LRU_W), F32)],
        input_output_aliases={10: 0},
        scratch_shapes=[pltpu.VMEM((8, LRU_W), F32), pltpu.VMEM((8, LRU_W), F32), pltpu.VMEM((8, LRU_W), F32)],
        compiler_params=_cp(1),
    )(proj, proj, proj, dycat, hstate, hstate, wl, vec, wa, wx, dproj)


def _exchange(name, items):
    flat_srcs, out_shapes, plan = [], [], []
    for it, (kind, srcs) in enumerate(items):
        shape = srcs[0].shape if kind == "gather" else srcs[0].shape[1:]
        out_shapes.append(jax.ShapeDtypeStruct((N_DEV, len(srcs)) + tuple(shape), srcs[0].dtype))
        for l, s in enumerate(srcs):
            plan.append((it, l, kind, len(flat_srcs)))
            flat_srcs.append(s)
    n_in, n_out, n_cp = len(flat_srcs), len(out_shapes), len(plan)

    def body(*refs):
        ins, outs = refs[:n_in], refs[n_in:n_in + n_out]
        send_sems, recv_sems, loc_sems = refs[n_in + n_out:]
        x, y, c = lax.axis_index("x"), lax.axis_index("y"), lax.axis_index("c")
        me = 4 * x + 2 * y + c
        remote, local = [], []
        for q, (it, l, kind, si) in enumerate(plan):
            src, out = ins[si], outs[it]
            for k in range(1, N_DEV):
                px = 1 - x if k & 4 else x
                py = 1 - y if k & 2 else y
                pc = 1 - c if k & 1 else c
                peer = 4 * px + 2 * py + pc
                cp = pltpu.make_async_remote_copy(
                    src_ref=src if kind == "gather" else src.at[peer],
                    dst_ref=out.at[me, l],
                    send_sem=send_sems.at[q * 7 + k - 1], recv_sem=recv_sems.at[q * 7 + k - 1],
                    device_id=(px, py, pc), device_id_type=pl.DeviceIdType.MESH)
                cp.start()
                remote.append(cp)
            lc = pltpu.make_async_copy(src if kind == "gather" else src.at[me], out.at[me, l], loc_sems.at[q])
            lc.start()
            local.append(lc)
        for lc in local:
            lc.wait()
        for cp in remote:
            cp.wait_send()
            cp.wait_recv()

    anyspec = pl.BlockSpec(memory_space=pl.ANY)
    outs = pl.pallas_call(
        body, name=name,
        in_specs=[anyspec] * n_in, out_specs=[anyspec] * n_out, out_shape=out_shapes,
        scratch_shapes=[pltpu.SemaphoreType.DMA((n_cp * 7,)), pltpu.SemaphoreType.DMA((n_cp * 7,)),
                        pltpu.SemaphoreType.DMA((n_cp,))],
    )(*flat_srcs)
    return list(outs)


def _adamw(name, w, m, v, recv):
    r, cdim = w.shape
    tr = r
    for cand in (512, 256, 128, 64, 32, 16, 8):
        if r % cand == 0 and r > cand:
            tr = cand
            break

    def body(w_ref, m_ref, v_ref, r_ref, g_ref, d_ref, mo_ref, vo_ref):
        g = r_ref[0].astype(F32)
        for s in range(1, N_DEV):
            g = g + r_ref[s].astype(F32)
        mn = ADAM_B1 * m_ref[...] + (1.0 - ADAM_B1) * g
        vn = ADAM_B2 * v_ref[...] + (1.0 - ADAM_B2) * (g * g)
        m_hat = mn / (1.0 - ADAM_B1 ** ADAM_STEP)
        v_hat = vn / (1.0 - ADAM_B2 ** ADAM_STEP)
        g_ref[...] = g
        d_ref[...] = -ADAM_LR * (m_hat / (jnp.sqrt(v_hat) + ADAM_EPS) + ADAM_WD * w_ref[...])
        mo_ref[...] = mn
        vo_ref[...] = vn

    blk = pl.BlockSpec((tr, cdim), lambda i: (i, 0))
    return pl.pallas_call(
        body, name=name, grid=(r // tr,),
        in_specs=[blk, blk, blk, pl.BlockSpec((N_DEV, tr, cdim), lambda i: (0, i, 0))],
        out_specs=[blk, blk, blk, blk],
        out_shape=[jax.ShapeDtypeStruct((r, cdim), F32)] * 4,
        compiler_params=_cp(1),
    )(w, m, v, recv)


def _pack_rows(arrs, lead=()):
    n = len(lead)
    flat = jnp.concatenate([a.reshape(a.shape[:n] + (-1,)) for a in arrs], axis=-1)
    size = flat.shape[-1]
    padded = -(-size // PACK_QUANTUM) * PACK_QUANTUM
    flat = jnp.pad(flat, [(0, 0)] * n + [(0, padded - size)])
    return flat.reshape(flat.shape[:n] + (padded // 128, 128))


def _unpack_rows(packed, shapes, lead=()):
    n = len(lead)
    flat = packed.reshape(packed.shape[:n] + (-1,))
    out, off = [], 0
    for s in shapes:
        size = int(np.prod(s))
        out.append(flat[..., off:off + size].reshape(packed.shape[:n] + tuple(s)))
        off += size
    return out


def _block_diag(w):
    eye = jnp.eye(LRU_HEADS, dtype=w.dtype)
    return (eye[:, None, :, None] * w[:, :, None, :]).reshape(LRU_W, LRU_W)


def _diag_blocks(dense):
    t = dense.reshape(LRU_HEADS, 64, LRU_HEADS, 64)
    return jnp.stack([t[h, :, h, :] for h in range(LRU_HEADS)], axis=0)


def _cols_to_slots(full):
    lead = full.shape[:-1]
    t = full.reshape(lead + (N_DEV, full.shape[-1] // N_DEV))
    return jnp.moveaxis(t, -2, 0)


def _slots_to_cols(slots):
    t = jnp.moveaxis(slots, 0, -2)
    return t.reshape(t.shape[:-2] + (t.shape[-2] * t.shape[-1],))


def kernel(x, meta_tokens, ln_in_g, ln_in_b, w_in, conv_dw_w, conv_dw_b, conv_ln_g, conv_ln_b, conv_pw_w, conv_pw_b, attn_sinks, lru_conv_w, lru_conv_b, lru_wa, lru_ba, lru_wx, lru_bx, lru_lambda, w_out, ln_post_g, ln_post_b, loss_target, m_meta_tokens, m_ln_in_g, m_ln_in_b, m_w_in, m_conv_dw_w, m_conv_dw_b, m_conv_ln_g, m_conv_ln_b, m_conv_pw_w, m_conv_pw_b, m_attn_sinks, m_lru_conv_w, m_lru_conv_b, m_lru_wa, m_lru_ba, m_lru_wx, m_lru_bx, m_lru_lambda, m_w_out, m_ln_post_g, m_ln_post_b, v_meta_tokens, v_ln_in_g, v_ln_in_b, v_w_in, v_conv_dw_w, v_conv_dw_b, v_conv_ln_g, v_conv_ln_b, v_conv_pw_w, v_conv_pw_b, v_attn_sinks, v_lru_conv_w, v_lru_conv_b, v_lru_wa, v_lru_ba, v_lru_wx, v_lru_bx, v_lru_lambda, v_w_out, v_ln_post_g, v_ln_post_b):
    seq = x.shape[1]
    lp = seq + TB
    row = lambda a: a.reshape(1, -1)

    small_shard_shapes = [conv_dw_w.shape, lru_conv_w.shape, meta_tokens.shape]
    small_shard = _pack_rows([conv_dw_w, lru_conv_w, meta_tokens])
    wg_in, wg_out, wg_pw, wg_small = _exchange("gather_weights", [
        ("gather", [w_in.astype(BF16)]),
        ("gather", [w_out.astype(BF16)]),
        ("gather", [conv_pw_w.astype(BF16)]),
        ("gather", [small_shard]),
    ])
    wg_in = wg_in[:, 0]
    wg_out = wg_out[:, 0]
    wg_pw = wg_pw[:, 0]
    g_dw, g_lc, g_meta = _unpack_rows(wg_small[:, 0], small_shard_shapes, lead=(N_DEV,))
    conv_dw_full = _slots_to_cols(g_dw)
    lru_conv_full = _slots_to_cols(g_lc)
    meta_full = _slots_to_cols(g_meta)

    tabs = _rope_tables(lp)
    ln_g = [ln_in_g, ln_post_g[0], ln_post_g[1]]
    ln_b = [ln_in_b, ln_post_b[0], ln_post_b[1]]

    def layer_params(l):
        wdw = jnp.pad(conv_dw_full[l], ((0, 1), (0, 0)))
        cvec = jnp.pad(jnp.stack([conv_dw_b[l], conv_ln_g[l], conv_ln_b[l], conv_pw_b[l]]), ((0, 4), (0, 0)))
        wpw = wg_pw[:, l].reshape(CONV_W, CONV_W)
        sinks = jnp.pad(attn_sinks[l].reshape(1, N_HEADS), ((0, 7), (0, 128 - N_HEADS)))
        wl = jnp.pad(lru_conv_full[l], ((0, 4), (0, 0)))
        lvec = jnp.pad(jnp.stack([lru_conv_b[l], lru_ba[l], lru_bx[l], lru_lambda[l]]), ((0, 4), (0, 0)))
        wa = _block_diag(lru_wa[l]).astype(BF16)
        wx = _block_diag(lru_wx[l]).astype(BF16)
        wout = wg_out[:, l].reshape(D_MODEL, D_MODEL)
        return dict(wdw=wdw, cvec=cvec, wpw=wpw, sinks=sinks, wl=wl, lvec=lvec, wa=wa, wx=wx, wout=wout)

    params = [layer_params(l) for l in range(DEPTH)]

    z = [_embed(x, meta_full)]
    saved = []
    for l in range(DEPTH):
        p = params[l]
        hb = _ln_fwd(f"ln_fwd{l}", z[l], row(ln_g[l]), row(ln_b[l]))
        proj = _mm_proj(f"proj{l}", hb, wg_in, l)
        ycat = _conv_fwd(f"conv_fwd{l}", proj, p["wdw"], p["cvec"], p["wpw"])
        qr, kr = _rope_fwd(f"rope{l}", proj, tabs)
        ycat = _attn_fwd(f"attn_fwd{l}", qr, kr, proj, p["sinks"], ycat)
        ycat, hstate = _lru_fwd(f"lru_fwd{l}", proj, p["wl"], p["lvec"], p["wa"], p["wx"], ycat)
        z.append(_mm_out(f"out{l}", ycat, p["wout"], z[l], row(ln_g[l]), row(ln_b[l])))
        saved.append(dict(hb=hb, proj=proj, ycat=ycat, qr=qr, kr=kr, hstate=hstate))

    dz, st_post1, loss_blk = _loss_head(z[DEPTH], loss_target, row(ln_g[DEPTH]), row(ln_b[DEPTH]))
    loss = lax.psum(loss_blk[0, 0], ("x", "y", "c"))

    ln_stats = {DEPTH: st_post1}
    g_layers = [None] * DEPTH
    dwin_l, dwout_l = [None] * DEPTH, [None] * DEPTH
    grad_x = gmeta = None
    for l in reversed(range(DEPTH)):
        p, s = params[l], saved[l]
        dycat = _mm_dycat(f"dycat{l}", dz, p["wout"])
        dwout_l[l] = _mm_dwout(f"dwout{l}", s["ycat"], dz)
        dproj, dwdw, dcvec, dwpw = _conv_bwd(f"conv_bwd{l}", s["proj"], dycat, p["wdw"], p["cvec"], p["wpw"])
        dq, dgate, dk, dv, dsink = _attn_bwd(f"attn_bwd{l}", s["qr"], s["kr"], s["proj"], p["sinks"], dycat)
        dproj = _attn_assemble(f"attn_asm{l}", dq, dgate, dk, dv, tabs, dproj)
        dproj, dwl, dlvec, dwa, dwx = _lru_bwd(f"lru_bwd{l}", s["proj"], dycat, s["hstate"],
                                                p["wl"], p["lvec"], p["wa"], p["wx"], dproj)
        dwin_l[l] = _mm_dwin(f"dwin{l}", s["hb"], dproj)
        dh = _mm_dh(f"dh{l}", dproj, wg_in, l, dz)
        if l > 0:
            dz, ln_stats[l] = _ln_bwd(f"ln_bwd{l}", dh, z[l], row(ln_g[l]))
        else:
            grad_x, gmeta, ln_stats[0] = _ln_bwd_input(dh, z[0], row(ln_g[0]))
        g_layers[l] = dict(dwdw=dwdw[:CONV_K], dcvec=dcvec, dwpw=dwpw, dsink=dsink[0, :N_HEADS],
                           dwl=dwl[:LRU_CONV_K], dlvec=dlvec, dwa=_diag_blocks(dwa), dwx=_diag_blocks(dwx))

    stack = lambda f: jnp.stack([f(g_layers[l]) for l in range(DEPTH)])
    g_local = dict(
        ln_in_g=ln_stats[0][0], ln_in_b=ln_stats[0][1],
        conv_dw_b=stack(lambda g: g["dcvec"][0]), conv_ln_g=stack(lambda g: g["dcvec"][1]),
        conv_ln_b=stack(lambda g: g["dcvec"][2]), conv_pw_b=stack(lambda g: g["dcvec"][3]),
        attn_sinks=stack(lambda g: g["dsink"]),
        lru_conv_b=stack(lambda g: g["dlvec"][0]), lru_wa=stack(lambda g: g["dwa"]),
        lru_ba=stack(lambda g: g["dlvec"][1]), lru_wx=stack(lambda g: g["dwx"]),
        lru_bx=stack(lambda g: g["dlvec"][2]), lru_lambda=stack(lambda g: g["dlvec"][3]),
        ln_post_g=jnp.stack([ln_stats[1][0], ln_stats[2][0]]),
        ln_post_b=jnp.stack([ln_stats[1][1], ln_stats[2][1]]),
    )
    rep_names = ["ln_in_g", "ln_in_b", "conv_dw_b", "conv_ln_g", "conv_ln_b", "conv_pw_b", "attn_sinks",
                 "lru_conv_b", "lru_wa", "lru_ba", "lru_wx", "lru_bx", "lru_lambda", "ln_post_g", "ln_post_b"]
    weights = dict(meta_tokens=meta_tokens, ln_in_g=ln_in_g, ln_in_b=ln_in_b, w_in=w_in, conv_dw_w=conv_dw_w,
                   conv_dw_b=conv_dw_b, conv_ln_g=conv_ln_g, conv_ln_b=conv_ln_b, conv_pw_w=conv_pw_w,
                   conv_pw_b=conv_pw_b, attn_sinks=attn_sinks, lru_conv_w=lru_conv_w, lru_conv_b=lru_conv_b,
                   lru_wa=lru_wa, lru_ba=lru_ba, lru_wx=lru_wx, lru_bx=lru_bx, lru_lambda=lru_lambda,
                   w_out=w_out, ln_post_g=ln_post_g, ln_post_b=ln_post_b)
    mom1 = dict(meta_tokens=m_meta_tokens, ln_in_g=m_ln_in_g, ln_in_b=m_ln_in_b, w_in=m_w_in, conv_dw_w=m_conv_dw_w,
                conv_dw_b=m_conv_dw_b, conv_ln_g=m_conv_ln_g, conv_ln_b=m_conv_ln_b, conv_pw_w=m_conv_pw_w,
                conv_pw_b=m_conv_pw_b, attn_sinks=m_attn_sinks, lru_conv_w=m_lru_conv_w, lru_conv_b=m_lru_conv_b,
                lru_wa=m_lru_wa, lru_ba=m_lru_ba, lru_wx=m_lru_wx, lru_bx=m_lru_bx, lru_lambda=m_lru_lambda,
                w_out=m_w_out, ln_post_g=m_ln_post_g, ln_post_b=m_ln_post_b)
    mom2 = dict(meta_tokens=v_meta_tokens, ln_in_g=v_ln_in_g, ln_in_b=v_ln_in_b, w_in=v_w_in, conv_dw_w=v_conv_dw_w,
                conv_dw_b=v_conv_dw_b, conv_ln_g=v_conv_ln_g, conv_ln_b=v_conv_ln_b, conv_pw_w=v_conv_pw_w,
                conv_pw_b=v_conv_pw_b, attn_sinks=v_attn_sinks, lru_conv_w=v_lru_conv_w, lru_conv_b=v_lru_conv_b,
                lru_wa=v_lru_wa, lru_ba=v_lru_ba, lru_wx=v_lru_wx, lru_bx=v_lru_bx, lru_lambda=v_lru_lambda,
                w_out=v_w_out, ln_post_g=v_ln_post_g, ln_post_b=v_ln_post_b)

    rep_pack = _pack_rows([g_local[n] for n in rep_names])
    shard_small_names = ["conv_dw_w", "lru_conv_w", "meta_tokens"]
    g_dw_full = jnp.stack([g_layers[l]["dwdw"] for l in range(DEPTH)])
    g_lc_full = jnp.stack([g_layers[l]["dwl"] for l in range(DEPTH)])
    shard_pack = _pack_rows([_cols_to_slots(g_dw_full), _cols_to_slots(g_lc_full), _cols_to_slots(gmeta)],
                            lead=(N_DEV,))
    r_win, r_wout, r_pw, r_small, r_rep = _exchange("exchange_grads", [
        ("scatter", dwin_l),
        ("scatter", [d.reshape(N_DEV, D_MODEL // N_DEV, D_MODEL) for d in dwout_l]),
        ("scatter", [g_layers[l]["dwpw"].reshape(N_DEV, CONV_W // N_DEV, CONV_W) for l in range(DEPTH)]),
        ("scatter", [shard_pack]),
        ("gather", [rep_pack]),
    ])

    res = {}

    def flat2(a, cols):
        return a.reshape(-1, cols)

    for name_, recv, cols in (("w_in", r_win, W_IN_SHARD), ("w_out", r_wout, D_MODEL), ("conv_pw_w", r_pw, CONV_W)):
        w_ = weights[name_]
        outs = _adamw(f"adamw_{name_}", flat2(w_, cols), flat2(mom1[name_], cols), flat2(mom2[name_], cols),
                      recv.reshape(N_DEV, -1, cols))
        res[name_] = [o.reshape(w_.shape) for o in outs]

    sshapes = [weights[n].shape for n in shard_small_names]
    outs = _adamw("adamw_small_sharded",
                  _pack_rows([weights[n] for n in shard_small_names]),
                  _pack_rows([mom1[n] for n in shard_small_names]),
                  _pack_rows([mom2[n] for n in shard_small_names]),
                  r_small[:, 0])
    for k, o in enumerate(outs):
        for n, a in zip(shard_small_names, _unpack_rows(o, sshapes)):
            res.setdefault(n, [None] * 4)[k] = a

    rshapes = [weights[n].shape for n in rep_names]
    outs = _adamw("adamw_replicated",
                  _pack_rows([weights[n] for n in rep_names]),
                  _pack_rows([mom1[n] for n in rep_names]),
                  _pack_rows([mom2[n] for n in rep_names]),
                  r_rep[:, 0])
    for k, o in enumerate(outs):
        for n, a in zip(rep_names, _unpack_rows(o, rshapes)):
            res.setdefault(n, [None] * 4)[k] = a

    order = ["meta_tokens", "ln_in_g", "ln_in_b", "w_in", "conv_dw_w", "conv_dw_b", "conv_ln_g", "conv_ln_b",
             "conv_pw_w", "conv_pw_b", "attn_sinks", "lru_conv_w", "lru_conv_b", "lru_wa", "lru_ba", "lru_wx",
             "lru_bx", "lru_lambda", "w_out", "ln_post_g", "ln_post_b"]
    return (loss, grad_x,
            *[res[n][0] for n in order], *[res[n][1] for n in order],
            *[res[n][2] for n in order], *[res[n][3] for n in order])
```

```python
import functools
import math

import numpy as np
import jax
import jax.numpy as jnp
from jax import lax
from jax.experimental import pallas as pl
from jax.experimental.pallas import tpu as pltpu

F32 = jnp.float32
BF16 = jnp.bfloat16

D_MODEL = 2048
DEPTH = 2
N_META = 16
TB = 128
PAD0 = TB - N_META
CONV_W = 512
CONV_K = 31
HEAD_DIM = 64
N_HEADS = 16
N_KV = 4
GROUP = 4
ATT_W = 1024
KV_W = 256
ROT_DIM = 16
ROPE_THETA = 500000.0
LRU_W = 512
LRU_HEADS = 8
LRU_CONV_K = 4
LRU_C = 8.0
IN_TOTAL = 5120
N_DEV = 8
W_IN_SHARD = IN_TOTAL // N_DEV
LN_EPS = 1e-5
ALPHA = (2.0 * DEPTH) ** 0.25
NEG_INF = -1e30
ATT_SCALE = HEAD_DIM ** -0.5

ADAM_LR = 0.001
ADAM_B1 = 0.9
ADAM_B2 = 0.999
ADAM_EPS = 1e-08
ADAM_WD = 0.01
ADAM_STEP = 10

VMEM_LIMIT = 56 * 1024 * 1024
PACK_QUANTUM = 256 * 128

COL_CV, COL_CG, COL_CGATE = 0, 1, 2
COL_Q0 = 3
COL_K256 = 10
COL_V256 = 11
COL_AGATE1024 = 3
COL_RX, COL_RGATE = 8, 9
YC_CONV, YC_LRU = 2, 3


def _wout_block(kk):
    return jnp.where(kk < 2, kk + 1, jnp.where(kk == 2, 0, 3))


def _cp(n_axes, vmem=VMEM_LIMIT):
    return pltpu.CompilerParams(dimension_semantics=("arbitrary",) * n_axes, vmem_limit_bytes=vmem)


def _row_tile(lp, max_blocks):
    nb = lp // TB
    d = max(k for k in range(1, max_blocks + 1) if nb % k == 0)
    return TB * d


def _sig(x):
    return jax.nn.sigmoid(x)


def _dsilu(x, s):
    return s * (1.0 + x * (1.0 - s))


def _ln_core(z):
    mu = jnp.mean(z, axis=-1, keepdims=True)
    zc = z - mu
    var = jnp.mean(zc * zc, axis=-1, keepdims=True)
    rstd = lax.rsqrt(var + LN_EPS)
    return zc * rstd, rstd


def _ln_bwd_core(dy, xh, rstd, g):
    dxh = dy * g
    m1 = jnp.mean(dxh, axis=-1, keepdims=True)
    m2 = jnp.mean(dxh * xh, axis=-1, keepdims=True)
    return rstd * (dxh - m1 - xh * m2)


def _row_ids(shape, base):
    return lax.broadcasted_iota(jnp.int32, shape, 0) + base


def _colsum(x):
    return jnp.sum(x, axis=0, keepdims=True)


def _dot(a, b, dims):
    return lax.dot_general(a, b, (dims, ((), ())), preferred_element_type=F32)


NN = ((1,), (0,))
NT = ((1,), (1,))
TN = ((0,), (0,))


def _embed(x, meta_full):
    s = x.shape[1]
    lp = s + TB
    nb = lp // TB

    def body(x_ref, m_ref, o_ref):
        i = pl.program_id(0)

        @pl.when(i == 0)
        def _():
            o_ref[0:PAD0, :] = jnp.zeros((PAD0, D_MODEL), F32)
            o_ref[PAD0:TB, :] = m_ref[...]

        @pl.when(i > 0)
        def _():
            o_ref[...] = x_ref[...]

    return pl.pallas_call(
        body, name="embed", grid=(nb,),
        in_specs=[pl.BlockSpec((None, TB, D_MODEL), lambda i: (0, jnp.maximum(i - 1, 0), 0)),
                  pl.BlockSpec((N_META, D_MODEL), lambda i: (0, 0))],
        out_specs=pl.BlockSpec((TB, D_MODEL), lambda i: (i, 0)),
        out_shape=jax.ShapeDtypeStruct((lp, D_MODEL), F32),
        compiler_params=_cp(1),
    )(x, meta_full)


def _ln_fwd(name, z, g, b):
    lp = z.shape[0]
    tr = _row_tile(lp, 3)

    def body(z_ref, g_ref, b_ref, o_ref):
        i = pl.program_id(0)
        xh, _ = _ln_core(z_ref[...])
        h = xh * g_ref[...] + b_ref[...]
        rows = _row_ids(h.shape, i * tr)
        o_ref[...] = jnp.where(rows >= PAD0, h, 0.0).astype(BF16)

    return pl.pallas_call(
        body, name=name, grid=(lp // tr,),
        in_specs=[pl.BlockSpec((tr, D_MODEL), lambda i: (i, 0)),
                  pl.BlockSpec((1, D_MODEL), lambda i: (0, 0)),
                  pl.BlockSpec((1, D_MODEL), lambda i: (0, 0))],
        out_specs=pl.BlockSpec((tr, D_MODEL), lambda i: (i, 0)),
        out_shape=jax.ShapeDtypeStruct((lp, D_MODEL), BF16),
        compiler_params=_cp(1),
    )(z, g, b)


def _loss_head(z, target, g, b):
    lp = z.shape[0]
    nb = lp // TB

    def body(z_ref, t_ref, g_ref, b_ref, dz_ref, st_ref, loss_ref):
        i = pl.program_id(0)

        @pl.when(i == 0)
        def _():
            st_ref[...] = jnp.zeros(st_ref.shape, F32)
            loss_ref[...] = jnp.zeros(loss_ref.shape, F32)
            dz_ref[...] = jnp.zeros(dz_ref.shape, F32)

        @pl.when(i > 0)
        def _():
            xh, rstd = _ln_core(z_ref[...])
            gg = g_ref[...]
            y = xh * gg + b_ref[...]
            e = y - t_ref[...]
            part = 0.5 * jnp.sum(jnp.mean(e * e, axis=-1, keepdims=True), axis=0, keepdims=True)
            loss_ref[...] += jnp.broadcast_to(part, loss_ref.shape)
            dy = e / float(D_MODEL)
            st_ref[0:1, :] += _colsum(dy * xh)
            st_ref[1:2, :] += _colsum(dy)
            dz_ref[...] = _ln_bwd_core(dy, xh, rstd, gg)

    return pl.pallas_call(
        body, name="loss_head", grid=(nb,),
        in_specs=[pl.BlockSpec((TB, D_MODEL), lambda i: (i, 0)),
                  pl.BlockSpec((None, TB, D_MODEL), lambda i: (0, jnp.maximum(i - 1, 0), 0)),
                  pl.BlockSpec((1, D_MODEL), lambda i: (0, 0)),
                  pl.BlockSpec((1, D_MODEL), lambda i: (0, 0))],
        out_specs=[pl.BlockSpec((TB, D_MODEL), lambda i: (i, 0)),
                   pl.BlockSpec((8, D_MODEL), lambda i: (0, 0)),
                   pl.BlockSpec((8, 128), lambda i: (0, 0))],
        out_shape=[jax.ShapeDtypeStruct((lp, D_MODEL), F32),
                   jax.ShapeDtypeStruct((8, D_MODEL), F32),
                   jax.ShapeDtypeStruct((8, 128), F32)],
        compiler_params=_cp(1),
    )(z, target, g, b)


def _ln_bwd(name, dh, z, g):
    lp = z.shape[0]
    nb = lp // TB

    def body(dh_ref, z_ref, g_ref, dz_ref, st_ref):
        i = pl.program_id(0)

        @pl.when(i == 0)
        def _():
            st_ref[...] = jnp.zeros(st_ref.shape, F32)

        xh, rstd = _ln_core(z_ref[...])
        rows = _row_ids(xh.shape, i * TB)
        dy = jnp.where(rows >= PAD0, dh_ref[...], 0.0)
        st_ref[0:1, :] += _colsum(dy * xh)
        st_ref[1:2, :] += _colsum(dy)
        dz_ref[...] = _ln_bwd_core(dy, xh, rstd, g_ref[...])

    return pl.pallas_call(
        body, name=name, grid=(nb,),
        in_specs=[pl.BlockSpec((TB, D_MODEL), lambda i: (i, 0)),
                  pl.BlockSpec((TB, D_MODEL), lambda i: (i, 0)),
                  pl.BlockSpec((1, D_MODEL), lambda i: (0, 0))],
        out_specs=[pl.BlockSpec((TB, D_MODEL), lambda i: (i, 0)),
                   pl.BlockSpec((8, D_MODEL), lambda i: (0, 0))],
        out_shape=[jax.ShapeDtypeStruct((lp, D_MODEL), F32),
                   jax.ShapeDtypeStruct((8, D_MODEL), F32)],
        compiler_params=_cp(1),
    )(dh, z, g)


def _ln_bwd_input(dh, z, g):
    lp = z.shape[0]
    nb = lp // TB
    s = lp - TB

    def body(dh_ref, z_ref, g_ref, gx_ref, gm_ref, st_ref):
        i = pl.program_id(0)

        @pl.when(i == 0)
        def _():
            st_ref[...] = jnp.zeros(st_ref.shape, F32)

        xh, rstd = _ln_core(z_ref[...])
        rows = _row_ids(xh.shape, i * TB)
        dy = jnp.where(rows >= PAD0, dh_ref[...], 0.0)
        st_ref[0:1, :] += _colsum(dy * xh)
        st_ref[1:2, :] += _colsum(dy)
        dz = _ln_bwd_core(dy, xh, rstd, g_ref[...])
        gx_ref[...] = dz

        @pl.when(i == 0)
        def _():
            gm_ref[...] = dz[PAD0:TB, :]

    return pl.pallas_call(
        body, name="ln_in_bwd", grid=(nb,),
        in_specs=[pl.BlockSpec((TB, D_MODEL), lambda i: (i, 0)),
                  pl.BlockSpec((TB, D_MODEL), lambda i: (i, 0)),
                  pl.BlockSpec((1, D_MODEL), lambda i: (0, 0))],
        out_specs=[pl.BlockSpec((None, TB, D_MODEL), lambda i: (0, jnp.maximum(i - 1, 0), 0)),
                   pl.BlockSpec((N_META, D_MODEL), lambda i: (0, 0)),
                   pl.BlockSpec((8, D_MODEL), lambda i: (0, 0))],
        out_shape=[jax.ShapeDtypeStruct((1, s, D_MODEL), F32),
                   jax.ShapeDtypeStruct((N_META, D_MODEL), F32),
                   jax.ShapeDtypeStruct((8, D_MODEL), F32)],
        compiler_params=_cp(1),
    )(dh, z, g)


def _mm_proj(name, hb, wg_in):
    lp = hb.shape[0]
    tm = lp // 4

    def body(a_ref, b_ref, o_ref):
        o_ref[...] = _dot(a_ref[...], b_ref[...], NN)

    return pl.pallas_call(
        body, name=name, grid=(4, N_DEV),
        in_specs=[pl.BlockSpec((tm, D_MODEL), lambda i, j: (i, 0)),
                  pl.BlockSpec((None, D_MODEL, W_IN_SHARD), lambda i, j: (j, 0, 0))],
        out_specs=pl.BlockSpec((tm, W_IN_SHARD), lambda i, j: (i, j)),
        out_shape=jax.ShapeDtypeStruct((lp, IN_TOTAL), F32),
        compiler_params=_cp(2),
    )(hb, wg_in)


def _mm_out(name, ycat, wout, z, g, b):
    lp = ycat.shape[0]
    tm = lp // 6
    nk = 4

    def body(a_ref, w_ref, z_ref, g_ref, b_ref, o_ref, acc_ref):
        i = pl.program_id(0)
        k = pl.program_id(1)

        @pl.when(k == 0)
        def _():
            acc_ref[...] = jnp.zeros(acc_ref.shape, F32)

        acc_ref[...] += _dot(a_ref[...], w_ref[...], NN)

        @pl.when(k == nk - 1)
        def _():
            xh, _ = _ln_core(z_ref[...])
            h = xh * g_ref[...] + b_ref[...]
            rows = _row_ids(h.shape, i * tm)
            h = jnp.where(rows >= PAD0, h, 0.0)
            o_ref[...] = ALPHA * h + acc_ref[...]

    return pl.pallas_call(
        body, name=name, grid=(6, nk),
        in_specs=[pl.BlockSpec((tm, 512), lambda i, k: (i, k)),
                  pl.BlockSpec((512, D_MODEL), lambda i, k: (_wout_block(k), 0)),
                  pl.BlockSpec((tm, D_MODEL), lambda i, k: (i, 0)),
                  pl.BlockSpec((1, D_MODEL), lambda i, k: (0, 0)),
                  pl.BlockSpec((1, D_MODEL), lambda i, k: (0, 0))],
        out_specs=pl.BlockSpec((tm, D_MODEL), lambda i, k: (i, 0)),
        out_shape=jax.ShapeDtypeStruct((lp, D_MODEL), F32),
        scratch_shapes=[pltpu.VMEM((tm, D_MODEL), F32)],
        compiler_params=_cp(2),
    )(ycat, wout, z, g, b)


def _mm_dycat(name, dz, wout, dep):
    lp = dz.shape[0]
    tm = lp // 6

    def body(a_ref, w_ref, dep_ref, o_ref):
        del dep_ref
        o_ref[...] = _dot(a_ref[...].astype(BF16), w_ref[...], NT).astype(BF16)

    return pl.pallas_call(
        body, name=name, grid=(6, 4),
        in_specs=[pl.BlockSpec((tm, D_MODEL), lambda i, j: (i, 0)),
                  pl.BlockSpec((512, D_MODEL), lambda i, j: (_wout_block(j), 0)),
                  pl.BlockSpec(memory_space=pl.ANY)],
        out_specs=pl.BlockSpec((tm, 512), lambda i, j: (i, j)),
        out_shape=jax.ShapeDtypeStruct((lp, D_MODEL), BF16),
        compiler_params=_cp(2),
    )(dz, wout, dep)


def _mm_dwout(name, ycat, dz):
    lp = ycat.shape[0]
    tk = _row_tile(lp, 3)
    nk = lp // tk
    wblk = (1, 2, 0, 3)

    def body(a_ref, b_ref, o_ref, acc_ref):
        k = pl.program_id(0)

        @pl.when(k == 0)
        def _():
            acc_ref[...] = jnp.zeros(acc_ref.shape, F32)

        acc_ref[...] += _dot(a_ref[...], b_ref[...].astype(BF16), TN)

        @pl.when(k == nk - 1)
        def _():
            for kk in range(4):
                o_ref[wblk[kk] * 512:(wblk[kk] + 1) * 512, :] = (
                    acc_ref[kk * 512:(kk + 1) * 512, :].astype(BF16))

    return pl.pallas_call(
        body, name=name, grid=(nk,),
        in_specs=[pl.BlockSpec((tk, D_MODEL), lambda k: (k, 0)),
                  pl.BlockSpec((tk, D_MODEL), lambda k: (k, 0))],
        out_specs=pl.BlockSpec((D_MODEL, D_MODEL), lambda k: (0, 0)),
        out_shape=jax.ShapeDtypeStruct((D_MODEL, D_MODEL), BF16),
        scratch_shapes=[pltpu.VMEM((D_MODEL, D_MODEL), F32)],
        compiler_params=_cp(1),
    )(ycat, dz)


def _mm_dwin(name, hb, dproj):
    lp = hb.shape[0]
    tk = _row_tile(lp, 3)
    nk = lp // tk

    def body(a_ref, b_ref, o_ref, acc_ref):
        k = pl.program_id(1)

        @pl.when(k == 0)
        def _():
            acc_ref[...] = jnp.zeros(acc_ref.shape, F32)

        acc_ref[...] += _dot(a_ref[...], b_ref[...], TN)

        @pl.when(k == nk - 1)
        def _():
            o_ref[0] = acc_ref[:, 0:W_IN_SHARD].astype(BF16)
            o_ref[1] = acc_ref[:, W_IN_SHARD:2 * W_IN_SHARD].astype(BF16)

    return pl.pallas_call(
        body, name=name, grid=(4, nk),
        in_specs=[pl.BlockSpec((tk, D_MODEL), lambda j, k: (k, 0)),
                  pl.BlockSpec((tk, 2 * W_IN_SHARD), lambda j, k: (k, j))],
        out_specs=pl.BlockSpec((2, D_MODEL, W_IN_SHARD), lambda j, k: (j, 0, 0)),
        out_shape=jax.ShapeDtypeStruct((N_DEV, D_MODEL, W_IN_SHARD), BF16),
        scratch_shapes=[pltpu.VMEM((D_MODEL, 2 * W_IN_SHARD), F32)],
        compiler_params=_cp(2),
    )(hb, dproj)


def _mm_dh(name, dproj, wg_in, dz, dep):
    lp = dproj.shape[0]
    tm = lp // 6

    def body(a_ref, w_ref, dz_ref, dep_ref, o_ref, acc_ref):
        del dep_ref
        k = pl.program_id(1)

        @pl.when(k == 0)
        def _():
            acc_ref[...] = jnp.zeros(acc_ref.shape, F32)

        acc_ref[...] += _dot(a_ref[...], w_ref[...], NT)

        @pl.when(k == N_DEV - 1)
        def _():
            o_ref[...] = acc_ref[...] + ALPHA * dz_ref[...]

    return pl.pallas_call(
        body, name=name, grid=(6, N_DEV),
        in_specs=[pl.BlockSpec((tm, W_IN_SHARD), lambda i, k: (i, k)),
                  pl.BlockSpec((None, D_MODEL, W_IN_SHARD), lambda i, k: (k, 0, 0)),
                  pl.BlockSpec((tm, D_MODEL), lambda i, k: (i, 0)),
                  pl.BlockSpec(memory_space=pl.ANY)],
        out_specs=pl.BlockSpec((tm, D_MODEL), lambda i, k: (i, 0)),
        out_shape=jax.ShapeDtypeStruct((lp, D_MODEL), F32),
        scratch_shapes=[pltpu.VMEM((tm, D_MODEL), F32)],
        compiler_params=_cp(2),
    )(dproj, wg_in, dz, dep)


def _shifted_views(cat, n_shift, base, rows):
    total = cat.shape[0]
    rolled = [cat] + [pltpu.roll(cat, b, axis=0) for b in range(1, 8)]
    views = []
    for s in range(n_shift):
        a, b = divmod(s, 8)
        views.append(rolled[b][base - 8 * a: base - 8 * a + rows, :])
    del total
    return views


def _conv_chain(j, cv_ref, cg_ref, cvp_ref, cgp_ref, wdw_ref, vec_ref, wpw_ref):
    cv = cv_ref[...]
    sg = _sig(cg_ref[...])
    c0 = cv * sg
    c0p = jnp.where(j > 0, cvp_ref[...] * _sig(cgp_ref[...]), 0.0)
    cat = jnp.concatenate([c0p, c0], axis=0)
    views = _shifted_views(cat, CONV_K, TB, TB)
    c1 = jnp.broadcast_to(vec_ref[0:1, :], (TB, CONV_W))
    for k in range(CONV_K):
        c1 = c1 + wdw_ref[k:k + 1, :] * views[CONV_K - 1 - k]
    xh, rstd = _ln_core(c1)
    c2 = xh * vec_ref[1:2, :] + vec_ref[2:3, :]
    s2 = _sig(c2)
    c3 = c2 * s2
    c4 = _dot(c3.astype(BF16), wpw_ref[...], NN) + vec_ref[3:4, :]
    return dict(cv=cv, sg=sg, views=views, xh=xh, rstd=rstd, c2=c2, s2=s2, c3=c3, c4=c4)


def _conv_in_specs(jmap):
    def cur(col):
        return pl.BlockSpec((TB, 512), lambda n: (jmap(n), col))

    def prev(col):
        return pl.BlockSpec((TB, 512), lambda n: (jnp.maximum(jmap(n) - 1, 0), col))

    return [cur(COL_CV), cur(COL_CG), prev(COL_CV), prev(COL_CG), cur(COL_CGATE)]


def _conv_param_specs():
    return [pl.BlockSpec((32, CONV_W), lambda n: (0, 0)),
            pl.BlockSpec((8, CONV_W), lambda n: (0, 0)),
            pl.BlockSpec((CONV_W, CONV_W), lambda n: (0, 0))]


def _conv_fwd(name, proj, wdw, vec, wpw):
    lp = proj.shape[0]
    nb = lp // TB

    def body(cv_ref, cg_ref, cvp_ref, cgp_ref, gate_ref, wdw_ref, vec_ref, wpw_ref, o_ref):
        j = pl.program_id(0)
        c = _conv_chain(j, cv_ref, cg_ref, cvp_ref, cgp_ref, wdw_ref, vec_ref, wpw_ref)
        gate = gate_ref[...]
        o_ref[...] = (c["c4"] * (gate * _sig(gate))).astype(BF16)

    return pl.pallas_call(
        body, name=name, grid=(nb,),
        in_specs=_conv_in_specs(lambda n: n) + _conv_param_specs(),
        out_specs=pl.BlockSpec((TB, 512), lambda n: (n, YC_CONV)),
        out_shape=jax.ShapeDtypeStruct((lp, D_MODEL), BF16),
        compiler_params=_cp(1),
    )(proj, proj, proj, proj, proj, wdw, vec, wpw)


def _conv_bwd(name, proj, dycat, wdw, vec, wpw):
    lp = proj.shape[0]
    nb = lp // TB
    halo = 32

    def body(cv_ref, cg_ref, cvp_ref, cgp_ref, gate_ref, dy_ref, wdw_ref, vec_ref, wpw_ref,
             dp_ref, dwdw_ref, dvec_ref, dwpw_ref, carry_ref):
        n = pl.program_id(0)
        j = nb - 1 - n

        @pl.when(n == 0)
        def _():
            carry_ref[...] = jnp.zeros(carry_ref.shape, F32)
            dwdw_ref[...] = jnp.zeros(dwdw_ref.shape, F32)
            dvec_ref[...] = jnp.zeros(dvec_ref.shape, F32)
            dwpw_ref[...] = jnp.zeros(dwpw_ref.shape, F32)

        c = _conv_chain(j, cv_ref, cg_ref, cvp_ref, cgp_ref, wdw_ref, vec_ref, wpw_ref)
        dy = dy_ref[...].astype(F32)
        gate = gate_ref[...]
        sgate = _sig(gate)
        dc4 = dy * (gate * sgate)
        dgate = dy * c["c4"] * _dsilu(gate, sgate)
        dc4b = dc4.astype(BF16)
        dvec_ref[3:4, :] += _colsum(dc4)
        dwpw_ref[...] += _dot(c["c3"].astype(BF16), dc4b, TN)
        dc3 = _dot(dc4b, wpw_ref[...], NT)
        dc2 = dc3 * _dsilu(c["c2"], c["s2"])
        dvec_ref[1:2, :] += _colsum(dc2 * c["xh"])
        dvec_ref[2:3, :] += _colsum(dc2)
        dc1 = _ln_bwd_core(dc2, c["xh"], c["rstd"], vec_ref[1:2, :])
        dvec_ref[0:1, :] += _colsum(dc1)
        for k in range(CONV_K):
            dwdw_ref[k:k + 1, :] += _colsum(dc1 * c["views"][CONV_K - 1 - k])
        dcat = jnp.concatenate([dc1, carry_ref[...]], axis=0)
        total = TB + halo
        up = [dcat] + [pltpu.roll(dcat, total - b, axis=0) for b in range(1, 8)]
        dc0 = jnp.zeros((TB, CONV_W), F32)
        for k in range(CONV_K):
            a, b = divmod(CONV_K - 1 - k, 8)
            dc0 = dc0 + wdw_ref[k:k + 1, :] * up[b][8 * a: 8 * a + TB, :]
        carry_ref[...] = dc1[0:halo, :]
        sg = c["sg"]
        dcv = dc0 * sg
        dcg = dc0 * c["cv"] * sg * (1.0 - sg)
        dp_ref[:, 0:512] = dcv.astype(BF16)
        dp_ref[:, 512:1024] = dcg.astype(BF16)
        dp_ref[:, 1024:1536] = dgate.astype(BF16)

    jmap = lambda n: nb - 1 - n
    return pl.pallas_call(
        body, name=name, grid=(nb,),
        in_specs=(_conv_in_specs(jmap)
                  + [pl.BlockSpec((TB, 512), lambda n: (jmap(n), YC_CONV))]
                  + _conv_param_specs()),
        out_specs=[pl.BlockSpec((TB, 1536), lambda n: (jmap(n), 0)),
                   pl.BlockSpec((32, CONV_W), lambda n: (0, 0)),
                   pl.BlockSpec((8, CONV_W), lambda n: (0, 0)),
                   pl.BlockSpec((CONV_W, CONV_W), lambda n: (0, 0))],
        out_shape=[jax.ShapeDtypeStruct((lp, IN_TOTAL), BF16),
                   jax.ShapeDtypeStruct((32, CONV_W), F32),
                   jax.ShapeDtypeStruct((8, CONV_W), F32),
                   jax.ShapeDtypeStruct((CONV_W, CONV_W), F32)],
        scratch_shapes=[pltpu.VMEM((halo, CONV_W), F32)],
        compiler_params=_cp(1),
    )(proj, proj, proj, proj, proj, dycat, wdw, vec, wpw)


def _rope_tables(lp):
    half = ROT_DIM // 2
    inv_freq = ROPE_THETA ** (-jnp.arange(half, dtype=F32) / half)
    pos = (jnp.arange(lp, dtype=jnp.int32) - PAD0).astype(F32)
    ang = pos[:, None] * inv_freq[None, :]
    cos, sin = jnp.cos(ang), jnp.sin(ang)
    ones = jnp.ones((lp, HEAD_DIM - ROT_DIM), F32)
    zeros = jnp.zeros((lp, HEAD_DIM - ROT_DIM), F32)
    zh = jnp.zeros((lp, half), F32)
    c = jnp.concatenate([cos, cos, ones], axis=1)
    sa = jnp.concatenate([-sin, zh, zeros], axis=1)
    sb = jnp.concatenate([zh, sin, zeros], axis=1)
    tile = lambda t: jnp.tile(t, (1, KV_W // HEAD_DIM))
    return tile(c), tile(sa), tile(sb)


def _rot(x, c, sa, sb):
    w = x.shape[1]
    return x * c + pltpu.roll(x, w - 8, axis=1) * sa + pltpu.roll(x, 8, axis=1) * sb


def _rot_t(dy, c, sa, sb):
    w = dy.shape[1]
    return dy * c + pltpu.roll(dy * sa, 8, axis=1) + pltpu.roll(dy * sb, w - 8, axis=1)


def _rope_fwd(name, proj, tabs):
    lp = proj.shape[0]
    tr = _row_tile(lp, 3)

    def body(q0_ref, q1_ref, k_ref, c_ref, sa_ref, sb_ref, qr_ref, kr_ref):
        c, sa, sb = c_ref[...], sa_ref[...], sb_ref[...]
        c2 = jnp.concatenate([c, c], axis=1)
        sa2 = jnp.concatenate([sa, sa], axis=1)
        sb2 = jnp.concatenate([sb, sb], axis=1)
        qr_ref[:, 0:512] = _rot(q0_ref[...], c2, sa2, sb2).astype(BF16)
        qr_ref[:, 512:1024] = _rot(q1_ref[...], c2, sa2, sb2).astype(BF16)
        kr_ref[...] = _rot(k_ref[...], c, sa, sb).astype(BF16)

    tab = pl.BlockSpec((tr, KV_W), lambda i: (i, 0))
    return pl.pallas_call(
        body, name=name, grid=(lp // tr,),
        in_specs=[pl.BlockSpec((tr, 512), lambda i: (i, COL_Q0)),
                  pl.BlockSpec((tr, 512), lambda i: (i, COL_Q0 + 1)),
                  pl.BlockSpec((tr, KV_W), lambda i: (i, COL_K256)),
                  tab, tab, tab],
        out_specs=[pl.BlockSpec((tr, ATT_W), lambda i: (i, 0)),
                   pl.BlockSpec((tr, KV_W), lambda i: (i, 0))],
        out_shape=[jax.ShapeDtypeStruct((lp, ATT_W), BF16),
                   jax.ShapeDtypeStruct((lp, KV_W), BF16)],
        compiler_params=_cp(1),
    )(proj, proj, proj, *tabs)


def _attn_mask(j):
    qi = lax.broadcasted_iota(jnp.int32, (GROUP * TB, 3 * TB), 0) & (TB - 1)
    cc = lax.broadcasted_iota(jnp.int32, (GROUP * TB, 3 * TB), 1)
    jj = cc & (TB - 1)
    is_meta = jj >= PAD0
    p0 = (cc < TB) & is_meta & (j >= 1)
    p1 = (cc >= TB) & (cc < 2 * TB) & (jj > qi) & (j >= 2)
    p2 = (cc >= 2 * TB) & (jj <= qi) & ((j >= 1) | is_meta)
    return p0 | p1 | p2


def _lane_group(rows):
    return lax.broadcasted_iota(jnp.int32, (rows, KV_W), 1) // HEAD_DIM


def _stack_heads(x, kv, lgq):
    parts = []
    for g in range(GROUP):
        sh = ((kv - g) % GROUP) * HEAD_DIM
        moved = x if sh == 0 else pltpu.roll(x, sh, axis=1)
        parts.append(jnp.where(lgq == kv, moved, 0.0))
    return jnp.concatenate(parts, axis=0).astype(BF16)


def _unstack_heads(r, kv):
    out = None
    for g in range(GROUP):
        blk = r[g * TB:(g + 1) * TB, :]
        sh = ((g - kv) % GROUP) * HEAD_DIM
        blk = blk if sh == 0 else pltpu.roll(blk, sh, axis=1)
        out = blk if out is None else out + blk
    return out


def _sink_column(sinks, kv):
    lane = lax.broadcasted_iota(jnp.int32, (1, 128), 1)
    cols = []
    for g in range(GROUP):
        sg = jnp.sum(jnp.where(lane == kv * GROUP + g, sinks, 0.0), axis=1, keepdims=True)
        cols.append(jnp.broadcast_to(sg, (TB, 1)))
    return jnp.concatenate(cols, axis=0)


def _attn_probs(qst, km, sinkcol, valid):
    s = _dot(qst, km, NT) * ATT_SCALE
    s = jnp.where(valid, s, NEG_INF)
    m = jnp.maximum(jnp.max(s, axis=-1, keepdims=True), sinkcol)
    e = jnp.exp(s - m)
    es = jnp.exp(sinkcol - m)
    inv = 1.0 / (jnp.sum(e, axis=-1, keepdims=True) + es)
    return e * inv, es * inv


def _attn_specs(jmap):
    blk = lambda col: pl.BlockSpec((TB, KV_W), lambda n: (jmap(n), col))
    prv = lambda col: pl.BlockSpec((TB, KV_W), lambda n: (jnp.maximum(jmap(n) - 1, 0), col))
    met = lambda col: pl.BlockSpec((TB, KV_W), lambda n: (0, col))
    return dict(
        qr=pl.BlockSpec((TB, ATT_W), lambda n: (jmap(n), 0)),
        k=[met(0), prv(0), blk(0)],
        v=[met(COL_V256), prv(COL_V256), blk(COL_V256)],
        gate=pl.BlockSpec((TB, ATT_W), lambda n: (jmap(n), COL_AGATE1024)),
        sinks=pl.BlockSpec((8, 128), lambda n: (0, 0)),
    )


def _attn_fwd(name, qr, kr, proj, sinks_row, ycat):
    lp = proj.shape[0]
    nb = lp // TB
    sp = _attn_specs(lambda n: n)

    def body(qr_ref, km_ref, kp_ref, kc_ref, vm_ref, vp_ref, vc_ref, gate_ref, sink_ref, yin_ref, o_ref):
        del yin_ref
        j = pl.program_id(0)
        valid = _attn_mask(j)
        kall = jnp.concatenate([km_ref[...], kp_ref[...], kc_ref[...]], axis=0).astype(F32)
        vall = jnp.concatenate([vm_ref[...], vp_ref[...], vc_ref[...]], axis=0)
        lg = _lane_group(3 * TB)
        lgq = _lane_group(TB)
        sinks = sink_ref[0:1, :]
        for kv in range(N_KV):
            cols = slice(kv * KV_W, (kv + 1) * KV_W)
            km = jnp.where(lg == kv, kall, 0.0).astype(BF16)
            vm = jnp.where(lg == kv, vall, 0.0).astype(BF16)
            qst = _stack_heads(qr_ref[:, cols].astype(F32), kv, lgq)
            p, _ = _attn_probs(qst, km, _sink_column(sinks, kv), valid)
            att = _unstack_heads(_dot(p.astype(BF16), vm, NN), kv)
            gate = gate_ref[:, cols]
            o_ref[:, cols] = (att * (gate * _sig(gate))).astype(BF16)

    return pl.pallas_call(
        body, name=name, grid=(nb,),
        in_specs=[sp["qr"]] + sp["k"] + sp["v"] + [sp["gate"], sp["sinks"],
                                                   pl.BlockSpec(memory_space=pl.ANY)],
        out_specs=pl.BlockSpec((TB, ATT_W), lambda n: (n, 0)),
        out_shape=jax.ShapeDtypeStruct((lp, D_MODEL), BF16),
        input_output_aliases={9: 0},
        compiler_params=_cp(1),
    )(qr, kr, kr, kr, proj, proj, proj, proj, sinks_row, ycat)


def _attn_bwd(name, qr, kr, proj, sinks_row, dycat):
    lp = proj.shape[0]
    nb = lp // TB
    sp = _attn_specs(lambda n: n)

    def body(qr_ref, km_ref, kp_ref, kc_ref, vm_ref, vp_ref, vc_ref, gate_ref, sink_ref, dy_ref,
             dq_ref, dgate_ref, dk_ref, dv_ref, dsink_ref):
        j = pl.program_id(0)

        @pl.when(j == 0)
        def _():
            dk_ref[...] = jnp.zeros(dk_ref.shape, F32)
            dv_ref[...] = jnp.zeros(dv_ref.shape, F32)
            dsink_ref[...] = jnp.zeros(dsink_ref.shape, F32)

        valid = _attn_mask(j)
        kall = jnp.concatenate([km_ref[...], kp_ref[...], kc_ref[...]], axis=0).astype(F32)
        vall = jnp.concatenate([vm_ref[...], vp_ref[...], vc_ref[...]], axis=0)
        lg = _lane_group(3 * TB)
        lgq = _lane_group(TB)
        sinks = sink_ref[0:1, :]
        lane = lax.broadcasted_iota(jnp.int32, (1, 128), 1)
        dkall = jnp.zeros((3 * TB, KV_W), F32)
        dvall = jnp.zeros((3 * TB, KV_W), F32)
        dsink = jnp.zeros((1, 128), F32)
        for kv in range(N_KV):
            cols = slice(kv * KV_W, (kv + 1) * KV_W)
            km = jnp.where(lg == kv, kall, 0.0).astype(BF16)
            vm = jnp.where(lg == kv, vall, 0.0).astype(BF16)
            qst = _stack_heads(qr_ref[:, cols].astype(F32), kv, lgq)
            gate = gate_ref[:, cols]
            sgate = _sig(gate)
            dy = dy_ref[:, cols].astype(F32)
            dout = dy * (gate * sgate)
            dost = _stack_heads(dout, kv, lgq)
            p, psink = _attn_probs(qst, km, _sink_column(sinks, kv), valid)
            pb = p.astype(BF16)
            att = _unstack_heads(_dot(pb, vm, NN), kv)
            dgate_ref[:, cols] = (dy * att * _dsilu(gate, sgate)).astype(BF16)
            dd = dout * att
            dcol = jnp.concatenate(
                [jnp.sum(jnp.where(lgq == g, dd, 0.0), axis=1, keepdims=True) for g in range(GROUP)], axis=0)
            dp = _dot(dost, vm, NT)
            ds = (p * (dp - dcol) * ATT_SCALE).astype(BF16)
            pd = psink * dcol
            for g in range(GROUP):
                tot = jnp.sum(pd[g * TB:(g + 1) * TB, :], axis=0, keepdims=True)
                dsink = dsink - jnp.where(lane == kv * GROUP + g, tot, 0.0)
            dq_ref[:, cols] = _unstack_heads(_dot(ds, km, NN), kv)
            dkall = dkall + _dot(ds, qst, TN)
            dvall = dvall + _dot(pb, dost, TN)
        dsink_ref[0:1, :] += dsink
        prev = pl.multiple_of(jnp.maximum(j - 1, 0) * TB, TB)
        cur = pl.multiple_of(j * TB, TB)
        dk_ref[0:TB, :] += dkall[0:TB]
        dv_ref[0:TB, :] += dvall[0:TB]
        dk_ref[pl.ds(prev, TB), :] += dkall[TB:2 * TB]
        dv_ref[pl.ds(prev, TB), :] += dvall[TB:2 * TB]
        dk_ref[pl.ds(cur, TB), :] += dkall[2 * TB:3 * TB]
        dv_ref[pl.ds(cur, TB), :] += dvall[2 * TB:3 * TB]

    return pl.pallas_call(
        body, name=name, grid=(nb,),
        in_specs=[sp["qr"]] + sp["k"] + sp["v"] + [sp["gate"], sp["sinks"],
                                                   pl.BlockSpec((TB, ATT_W), lambda n: (n, 0))],
        out_specs=[pl.BlockSpec((TB, ATT_W), lambda n: (n, 0)),
                   pl.BlockSpec((TB, ATT_W), lambda n: (n, 0)),
                   pl.BlockSpec((lp, KV_W), lambda n: (0, 0)),
                   pl.BlockSpec((lp, KV_W), lambda n: (0, 0)),
                   pl.BlockSpec((8, 128), lambda n: (0, 0))],
        out_shape=[jax.ShapeDtypeStruct((lp, ATT_W), F32),
                   jax.ShapeDtypeStruct((lp, ATT_W), BF16),
                   jax.ShapeDtypeStruct((lp, KV_W), F32),
                   jax.ShapeDtypeStruct((lp, KV_W), F32),
                   jax.ShapeDtypeStruct((8, 128), F32)],
        compiler_params=_cp(1),
    )(qr, kr, kr, kr, proj, proj, proj, proj, sinks_row, dycat)


def _attn_assemble(name, dq, dgate, dk, dv, tabs, dproj):
    lp = dq.shape[0]
    tr = _row_tile(lp, 3)

    def body(dq_ref, dg_ref, dk_ref, dv_ref, c_ref, sa_ref, sb_ref, din_ref, o_ref):
        del din_ref
        cidx = pl.program_id(1)
        c, sa, sb = c_ref[...], sa_ref[...], sb_ref[...]

        @pl.when(cidx < 2)
        def _():
            c2 = jnp.concatenate([c, c], axis=1)
            sa2 = jnp.concatenate([sa, sa], axis=1)
            sb2 = jnp.concatenate([sb, sb], axis=1)
            o_ref[...] = _rot_t(dq_ref[...], c2, sa2, sb2).astype(BF16)

        @pl.when(cidx == 2)
        def _():
            o_ref[:, 0:KV_W] = _rot_t(dk_ref[...], c, sa, sb).astype(BF16)
            o_ref[:, KV_W:2 * KV_W] = dv_ref[...].astype(BF16)

        @pl.when(cidx > 2)
        def _():
            o_ref[...] = dg_ref[...]

    tab = pl.BlockSpec((tr, KV_W), lambda n, c: (n, 0))
    return pl.pallas_call(
        body, name=name, grid=(lp // tr, 5),
        in_specs=[pl.BlockSpec((tr, 512), lambda n, c: (n, jnp.minimum(c, 1))),
                  pl.BlockSpec((tr, 512), lambda n, c: (n, jnp.clip(c - 3, 0, 1))),
                  tab, tab,
                  tab, tab, tab,
                  pl.BlockSpec(memory_space=pl.ANY)],
        out_specs=pl.BlockSpec((tr, 512), lambda n, c: (n, COL_Q0 + c)),
        out_shape=jax.ShapeDtypeStruct((lp, IN_TOTAL), BF16),
        input_output_aliases={7: 0},
        compiler_params=_cp(2),
    )(dq, dgate, dk, dv, *tabs, dproj)


def _softplus_neg(lam):
    t = jnp.exp(-jnp.abs(lam))
    u = 1.0 + t
    den = jnp.where(u == 1.0, 1.0, u - 1.0)
    l1p = jnp.where(u == 1.0, t, jnp.log(u) * (t / den))
    return jnp.maximum(-lam, 0.0) + l1p


def _lru_chain(j, rx_ref, rxp_ref, wl_ref, vec_ref, wa_ref, wx_ref):
    rx = rx_ref[...]
    rxp = jnp.where(j > 0, rxp_ref[...], 0.0)
    cat = jnp.concatenate([rxp, rx], axis=0)
    views = [cat[8:8 + TB, :]] + [pltpu.roll(cat, s, axis=0)[8:8 + TB, :] for s in range(1, LRU_CONV_K)]
    x1 = jnp.broadcast_to(vec_ref[0:1, :], (TB, LRU_W))
    for k in range(LRU_CONV_K):
        x1 = x1 + wl_ref[k:k + 1, :] * views[LRU_CONV_K - 1 - k]
    x1b = x1.astype(BF16)
    r = _sig(_dot(x1b, wa_ref[...], NN) + vec_ref[1:2, :])
    ig = _sig(_dot(x1b, wx_ref[...], NN) + vec_ref[2:3, :])
    sp = _softplus_neg(vec_ref[3:4, :])
    log_a = -LRU_C * r * sp
    rows = _row_ids((TB, LRU_W), j * TB)
    live = rows >= PAD0
    a = jnp.where(live, jnp.exp(log_a), 0.0)
    y2 = 2.0 * log_a
    em = -jnp.tanh(0.5 * y2) * (jnp.exp(y2) + 1.0)
    mult = jnp.sqrt(em)
    return dict(views=views, x1=x1, x1b=x1b, r=r, ig=ig, sp=sp, a=a, mult=mult, live=live, a_raw=jnp.exp(log_a))


def _lru_specs(jmap):
    return [pl.BlockSpec((TB, 512), lambda n: (jmap(n), COL_RX)),
            pl.BlockSpec((8, 512), lambda n: (jnp.maximum(jmap(n) * (TB // 8) - 1, 0), COL_RX)),
            pl.BlockSpec((TB, 512), lambda n: (jmap(n), COL_RGATE))]


def _lru_param_specs():
    return [pl.BlockSpec((8, LRU_W), lambda n: (0, 0)),
            pl.BlockSpec((8, LRU_W), lambda n: (0, 0)),
            pl.BlockSpec((LRU_W, LRU_W), lambda n: (0, 0)),
            pl.BlockSpec((LRU_W, LRU_W), lambda n: (0, 0))]


def _lru_fwd(name, proj, wl, vec, wa, wx, ycat):
    lp = proj.shape[0]
    nb = lp // TB

    def body(rx_ref, rxp_ref, gate_ref, wl_ref, vec_ref, wa_ref, wx_ref, yin_ref, o_ref, h_ref, carry_ref):
        del yin_ref
        j = pl.program_id(0)

        @pl.when(j == 0)
        def _():
            carry_ref[...] = jnp.zeros(carry_ref.shape, F32)

        c = _lru_chain(j, rx_ref, rxp_ref, wl_ref, vec_ref, wa_ref, wx_ref)
        a = c["a"]
        u = jnp.where(c["live"], c["mult"] * (c["ig"] * c["x1"]), 0.0)
        rows = lax.broadcasted_iota(jnp.int32, (TB, LRU_W), 0)
        d = 1
        while d < TB:
            ap = jnp.where(rows >= d, pltpu.roll(a, d, axis=0), 1.0)
            up = jnp.where(rows >= d, pltpu.roll(u, d, axis=0), 0.0)
            u = a * up + u
            a = a * ap
            d *= 2
        h = u + a * carry_ref[0:1, :]
        carry_ref[...] = h[TB - 8:TB, :]
        carry_ref[0:1, :] = h[TB - 1:TB, :]
        h_ref[...] = h
        gate = gate_ref[...]
        o_ref[...] = (h * (gate * _sig(gate))).astype(BF16)

    return pl.pallas_call(
        body, name=name, grid=(nb,),
        in_specs=_lru_specs(lambda n: n) + _lru_param_specs() + [pl.BlockSpec(memory_space=pl.ANY)],
        out_specs=[pl.BlockSpec((TB, 512), lambda n: (n, YC_LRU)),
                   pl.BlockSpec((TB, LRU_W), lambda n: (n, 0))],
        out_shape=[jax.ShapeDtypeStruct((lp, D_MODEL), BF16),
                   jax.ShapeDtypeStruct((lp, LRU_W), F32)],
        input_output_aliases={7: 0},
        scratch_shapes=[pltpu.VMEM((8, LRU_W), F32)],
        compiler_params=_cp(1),
    )(proj, proj, proj, wl, vec, wa, wx, ycat)


def _lru_bwd(name, proj, dycat, hstate, wl, vec, wa, wx, dproj):
    lp = proj.shape[0]
    nb = lp // TB

    def body(rx_ref, rxp_ref, gate_ref, dy_ref, h_ref, hp_ref, wl_ref, vec_ref, wa_ref, wx_ref, din_ref,
             dp_ref, dwl_ref, dvec_ref, dwa_ref, dwx_ref, dhc_ref, anx_ref, dxc_ref):
        del din_ref
        n = pl.program_id(0)
        j = nb - 1 - n

        @pl.when(n == 0)
        def _():
            dhc_ref[...] = jnp.zeros(dhc_ref.shape, F32)
            anx_ref[...] = jnp.zeros(anx_ref.shape, F32)
            dxc_ref[...] = jnp.zeros(dxc_ref.shape, F32)
            dwl_ref[...] = jnp.zeros(dwl_ref.shape, F32)
            dvec_ref[...] = jnp.zeros(dvec_ref.shape, F32)
            dwa_ref[...] = jnp.zeros(dwa_ref.shape, F32)
            dwx_ref[...] = jnp.zeros(dwx_ref.shape, F32)

        c = _lru_chain(j, rx_ref, rxp_ref, wl_ref, vec_ref, wa_ref, wx_ref)
        a, mult, r, ig, x1, live = c["a"], c["mult"], c["r"], c["ig"], c["x1"], c["live"]
        h = h_ref[...]
        gate = gate_ref[...]
        sgate = _sig(gate)
        dy = dy_ref[...].astype(F32)
        gsum = dy * (gate * sgate)
        dgate = dy * h * _dsilu(gate, sgate)
        rows = lax.broadcasted_iota(jnp.int32, (TB, LRU_W), 0)
        bb = jnp.where(rows == TB - 1, anx_ref[0:1, :], pltpu.roll(a, TB - 1, axis=0))
        gg = gsum
        d = 1
        while d < TB:
            keep = rows < TB - d
            bn = jnp.where(keep, pltpu.roll(bb, TB - d, axis=0), 1.0)
            gn = jnp.where(keep, pltpu.roll(gg, TB - d, axis=0), 0.0)
            gg = gg + bb * gn
            bb = bb * bn
            d *= 2
        dh = gg + bb * dhc_ref[0:1, :]
        dhc_ref[...] = dh[0:8, :]
        anx_ref[...] = a[0:8, :]
        hprev = jnp.where(rows == 0, jnp.where(j > 0, hp_ref[7:8, :], 0.0), pltpu.roll(h, 1, axis=0))
        du = jnp.where(live, dh, 0.0)
        da = jnp.where(live, dh * hprev, 0.0)
        ar = c["a_raw"]
        dmult = du * (ig * x1)
        di = du * mult * x1
        dx1 = du * mult * ig
        dloga = da * ar - dmult * ar * ar / mult
        dr = dloga * (-LRU_C * c["sp"])
        dvec_ref[3:4, :] += _colsum(dloga * (-LRU_C * r))
        dza = dr * r * (1.0 - r)
        dzx = di * ig * (1.0 - ig)
        dzab, dzxb = dza.astype(BF16), dzx.astype(BF16)
        dvec_ref[1:2, :] += _colsum(dza)
        dvec_ref[2:3, :] += _colsum(dzx)
        dwa_ref[...] += _dot(c["x1b"], dzab, TN)
        dwx_ref[...] += _dot(c["x1b"], dzxb, TN)
        dx1 = dx1 + _dot(dzab, wa_ref[...], NT) + _dot(dzxb, wx_ref[...], NT)
        dvec_ref[0:1, :] += _colsum(dx1)
        for k in range(LRU_CONV_K):
            dwl_ref[k:k + 1, :] += _colsum(dx1 * c["views"][LRU_CONV_K - 1 - k])
        dcat = jnp.concatenate([dx1, dxc_ref[...]], axis=0)
        drx = jnp.zeros((TB, LRU_W), F32)
        for k in range(LRU_CONV_K):
            s = LRU_CONV_K - 1 - k
            view = dcat[0:TB, :] if s == 0 else pltpu.roll(dcat, TB + 8 - s, axis=0)[0:TB, :]
            drx = drx + wl_ref[k:k + 1, :] * view
        dxc_ref[...] = dx1[0:8, :]
        dp_ref[:, 0:512] = drx.astype(BF16)
        dp_ref[:, 512:1024] = dgate.astype(BF16)

        @pl.when(n == nb - 1)
        def _():
            lam = vec_ref[3:4, :]
            dvec_ref[3:4, :] = dvec_ref[3:4, :] * (-_sig(-lam))

    jmap = lambda n: nb - 1 - n
    return pl.pallas_call(
        body, name=name, grid=(nb,),
        in_specs=(_lru_specs(jmap)
                  + [pl.BlockSpec((TB, 512), lambda n: (jmap(n), YC_LRU)),
                     pl.BlockSpec((TB, LRU_W), lambda n: (jmap(n), 0)),
                     pl.BlockSpec((8, LRU_W), lambda n: (jnp.maximum(jmap(n) * (TB // 8) - 1, 0), 0))]
                  + _lru_param_specs() + [pl.BlockSpec(memory_space=pl.ANY)]),
        out_specs=[pl.BlockSpec((TB, 1024), lambda n: (jmap(n), 4)),
                   pl.BlockSpec((8, LRU_W), lambda n: (0, 0)),
                   pl.BlockSpec((8, LRU_W), lambda n: (0, 0)),
                   pl.BlockSpec((LRU_W, LRU_W), lambda n: (0, 0)),
                   pl.BlockSpec((LRU_W, LRU_W), lambda n: (0, 0))],
        out_shape=[jax.ShapeDtypeStruct((lp, IN_TOTAL), BF16),
                   jax.ShapeDtypeStruct((8, LRU_W), F32),
                   jax.ShapeDtypeStruct((8, LRU_W), F32),
                   jax.ShapeDtypeStruct((LRU_W, LRU_W), F32),
                   jax.ShapeDtypeStruct((LRU_W, LRU_W), F32)],
        input_output_aliases={10: 0},
        scratch_shapes=[pltpu.VMEM((8, LRU_W), F32), pltpu.VMEM((8, LRU_W), F32), pltpu.VMEM((8, LRU_W), F32)],
        compiler_params=_cp(1),
    )(proj, proj, proj, dycat, hstate, hstate, wl, vec, wa, wx, dproj)


def _exchange(name, items):
    flat_srcs, out_shapes, plan = [], [], []
    for it, (kind, srcs) in enumerate(items):
        shape = srcs[0].shape if kind == "gather" else srcs[0].shape[1:]
        out_shapes.append(jax.ShapeDtypeStruct((N_DEV, len(srcs)) + tuple(shape), srcs[0].dtype))
        for l, s in enumerate(srcs):
            plan.append((it, l, kind, len(flat_srcs)))
            flat_srcs.append(s)
    n_in, n_out, n_cp = len(flat_srcs), len(out_shapes), len(plan)

    def body(*refs):
        ins, outs = refs[:n_in], refs[n_in:n_in + n_out]
        send_sems, recv_sems, loc_sems = refs[n_in + n_out:]
        x, y, c = lax.axis_index("x"), lax.axis_index("y"), lax.axis_index("c")
        me = 4 * x + 2 * y + c
        remote, local = [], []
        for q, (it, l, kind, si) in enumerate(plan):
            src, out = ins[si], outs[it]
            for k in range(1, N_DEV):
                px = 1 - x if k & 4 else x
                py = 1 - y if k & 2 else y
                pc = 1 - c if k & 1 else c
                peer = 4 * px + 2 * py + pc
                cp = pltpu.make_async_remote_copy(
                    src_ref=src if kind == "gather" else src.at[peer],
                    dst_ref=out.at[me, l],
                    send_sem=send_sems.at[q * 7 + k - 1], recv_sem=recv_sems.at[q * 7 + k - 1],
                    device_id=(px, py, pc), device_id_type=pl.DeviceIdType.MESH)
                cp.start()
                remote.append(cp)
            lc = pltpu.make_async_copy(src if kind == "gather" else src.at[me], out.at[me, l], loc_sems.at[q])
            lc.start()
            local.append(lc)
        for lc in local:
            lc.wait()
        for cp in remote:
            cp.wait_send()
            cp.wait_recv()

    anyspec = pl.BlockSpec(memory_space=pl.ANY)
    outs = pl.pallas_call(
        body, name=name,
        in_specs=[anyspec] * n_in, out_specs=[anyspec] * n_out, out_shape=out_shapes,
        scratch_shapes=[pltpu.SemaphoreType.DMA((n_cp * 7,)), pltpu.SemaphoreType.DMA((n_cp * 7,)),
                        pltpu.SemaphoreType.DMA((n_cp,))],
    )(*flat_srcs)
    return list(outs)


_HBM = pl.BlockSpec(memory_space=pltpu.HBM)
_SEM = pl.BlockSpec(memory_space=pltpu.SEMAPHORE)
_ANY = pl.BlockSpec(memory_space=pl.ANY)
_EFFECT = pltpu.SideEffectType.DATAFLOW_SIDE_EFFECTING


def _hbm(a):
    return pltpu.with_memory_space_constraint(a, pltpu.HBM)


def _split_descriptors(copies, srcs, lands, send_sems, recv_sems):
    x, y, c = lax.axis_index("x"), lax.axis_index("y"), lax.axis_index("c")
    me = 4 * x + 2 * y + c
    out = []
    for q, (si, scatter, li, ll) in enumerate(copies):
        for k in range(1, N_DEV):
            px = 1 - x if k & 4 else x
            py = 1 - y if k & 2 else y
            pc = 1 - c if k & 1 else c
            peer = 4 * px + 2 * py + pc
            dst = lands[li].at[me] if ll is None else lands[li].at[me, ll]
            out.append(pltpu.make_async_remote_copy(
                src_ref=srcs[si].at[peer] if scatter else srcs[si], dst_ref=dst,
                send_sem=send_sems.at[q * 7 + k - 1], recv_sem=recv_sems.at[q * 7 + k - 1],
                device_id=(px, py, pc), device_id_type=pl.DeviceIdType.MESH))
    return out


def _xchg_start(name, groups):
    n_src = [len(g[0]) for g in groups]
    n_land = [len(g[1]) for g in groups]
    srcs = [s for g in groups for s in g[0]]
    lands = [l for g in groups for l in g[1]]
    ns, nl, ng = len(srcs), len(lands), len(groups)

    def body(*refs):
        src_refs, land_refs = refs[:ns], refs[ns:ns + nl]
        sems = refs[ns + nl:ns + nl + 2 * ng]
        token = refs[-1]
        so = lo = 0
        for gi, (_, _, copies) in enumerate(groups):
            for d in _split_descriptors(copies, src_refs[so:so + n_src[gi]], land_refs[lo:lo + n_land[gi]],
                                        sems[2 * gi], sems[2 * gi + 1]):
                d.start()
            so += n_src[gi]
            lo += n_land[gi]
        token[...] = jnp.zeros(token.shape, F32)

    out_shape, out_specs = [], []
    for g in groups:
        n = 7 * len(g[2])
        out_shape += [pltpu.SemaphoreType.DMA((n,)), pltpu.SemaphoreType.DMA((n,))]
        out_specs += [_SEM, _SEM]
    out_shape += [pltpu.HBM(l.shape, l.dtype) for l in lands]
    out_specs += [_HBM] * nl
    out_shape.append(jax.ShapeDtypeStruct((8, 128), F32))
    out_specs.append(pl.BlockSpec(memory_space=pltpu.VMEM))
    outs = pl.pallas_call(
        body, name=name, in_specs=[_HBM] * (ns + nl), out_specs=out_specs, out_shape=out_shape,
        input_output_aliases={ns + i: 2 * ng + i for i in range(nl)},
        compiler_params=pltpu.CompilerParams(has_side_effects=_EFFECT),
    )(*[_hbm(a) for a in srcs + lands])
    res, lo = [], 2 * ng
    for gi in range(ng):
        res.append((outs[2 * gi], outs[2 * gi + 1], list(outs[lo:lo + n_land[gi]])))
        lo += n_land[gi]
    return res, outs[-1]


def _xchg_wait(name, group, started, after):
    srcs, _, copies = group
    send_sems, recv_sems, lands = started
    ns, nl = len(srcs), len(lands)

    def body(*refs):
        src_refs, land_refs = refs[:ns], refs[ns:ns + nl]
        send_ref, recv_ref = refs[ns + nl], refs[ns + nl + 1]
        for d in _split_descriptors(copies, src_refs, land_refs, send_ref, recv_ref):
            d.wait_send()
            d.wait_recv()

    outs = pl.pallas_call(
        body, name=name, in_specs=[_HBM] * (ns + nl) + [_SEM, _SEM, _ANY],
        out_specs=[_HBM] * nl, out_shape=[pltpu.HBM(l.shape, l.dtype) for l in lands],
        input_output_aliases={ns + i: i for i in range(nl)},
        compiler_params=pltpu.CompilerParams(has_side_effects=_EFFECT),
    )(*[_hbm(a) for a in srcs], *lands, send_sems, recv_sems, after)
    return list(outs)


def _landing(own, me):
    land = lax.empty((N_DEV,) + own.shape, own.dtype)
    return lax.dynamic_update_slice(land, own[None], (me,) + (0,) * own.ndim)


def _adamw(name, w, m, v, recv, row0=0, prev=None):
    cdim = w.shape[1]
    r = recv.shape[1]
    tr = r
    for cand in (512, 256, 128, 64, 32, 16, 8):
        if r % cand == 0 and r > cand:
            tr = cand
            break
    assert row0 % tr == 0
    blk0 = row0 // tr
    n_prev = 0 if prev is None else 4

    def body(w_ref, m_ref, v_ref, r_ref, *rest):
        g_ref, d_ref, mo_ref, vo_ref = rest[n_prev:]
        g = r_ref[0].astype(F32)
        for s in range(1, N_DEV):
            g = g + r_ref[s].astype(F32)
        mn = ADAM_B1 * m_ref[...] + (1.0 - ADAM_B1) * g
        vn = ADAM_B2 * v_ref[...] + (1.0 - ADAM_B2) * (g * g)
        m_hat = mn / (1.0 - ADAM_B1 ** ADAM_STEP)
        v_hat = vn / (1.0 - ADAM_B2 ** ADAM_STEP)
        g_ref[...] = g
        d_ref[...] = -ADAM_LR * (m_hat / (jnp.sqrt(v_hat) + ADAM_EPS) + ADAM_WD * w_ref[...])
        mo_ref[...] = mn
        vo_ref[...] = vn

    blk = pl.BlockSpec((tr, cdim), lambda i: (i + blk0, 0))
    return pl.pallas_call(
        body, name=name, grid=(r // tr,),
        in_specs=[blk, blk, blk, pl.BlockSpec((N_DEV, tr, cdim), lambda i: (0, i, 0))] + [_ANY] * n_prev,
        out_specs=[blk, blk, blk, blk],
        out_shape=[jax.ShapeDtypeStruct(w.shape, F32)] * 4,
        input_output_aliases={4 + i: i for i in range(n_prev)},
        compiler_params=_cp(1),
    )(w, m, v, recv, *(prev or []))


def _pack_rows(arrs, lead=()):
    n = len(lead)
    flat = jnp.concatenate([a.reshape(a.shape[:n] + (-1,)) for a in arrs], axis=-1)
    size = flat.shape[-1]
    padded = -(-size // PACK_QUANTUM) * PACK_QUANTUM
    flat = jnp.pad(flat, [(0, 0)] * n + [(0, padded - size)])
    return flat.reshape(flat.shape[:n] + (padded // 128, 128))


def _unpack_rows(packed, shapes, lead=()):
    n = len(lead)
    flat = packed.reshape(packed.shape[:n] + (-1,))
    out, off = [], 0
    for s in shapes:
        size = int(np.prod(s))
        out.append(flat[..., off:off + size].reshape(packed.shape[:n] + tuple(s)))
        off += size
    return out


def _block_diag(w):
    eye = jnp.eye(LRU_HEADS, dtype=w.dtype)
    return (eye[:, None, :, None] * w[:, :, None, :]).reshape(LRU_W, LRU_W)


def _diag_blocks(dense):
    t = dense.reshape(LRU_HEADS, 64, LRU_HEADS, 64)
    return jnp.stack([t[h, :, h, :] for h in range(LRU_HEADS)], axis=0)


def _cols_to_slots(full):
    lead = full.shape[:-1]
    t = full.reshape(lead + (N_DEV, full.shape[-1] // N_DEV))
    return jnp.moveaxis(t, -2, 0)


def _slots_to_cols(slots):
    t = jnp.moveaxis(slots, 0, -2)
    return t.reshape(t.shape[:-2] + (t.shape[-2] * t.shape[-1],))


def kernel(x, meta_tokens, ln_in_g, ln_in_b, w_in, conv_dw_w, conv_dw_b, conv_ln_g, conv_ln_b, conv_pw_w, conv_pw_b, attn_sinks, lru_conv_w, lru_conv_b, lru_wa, lru_ba, lru_wx, lru_bx, lru_lambda, w_out, ln_post_g, ln_post_b, loss_target, m_meta_tokens, m_ln_in_g, m_ln_in_b, m_w_in, m_conv_dw_w, m_conv_dw_b, m_conv_ln_g, m_conv_ln_b, m_conv_pw_w, m_conv_pw_b, m_attn_sinks, m_lru_conv_w, m_lru_conv_b, m_lru_wa, m_lru_ba, m_lru_wx, m_lru_bx, m_lru_lambda, m_w_out, m_ln_post_g, m_ln_post_b, v_meta_tokens, v_ln_in_g, v_ln_in_b, v_w_in, v_conv_dw_w, v_conv_dw_b, v_conv_ln_g, v_conv_ln_b, v_conv_pw_w, v_conv_pw_b, v_attn_sinks, v_lru_conv_w, v_lru_conv_b, v_lru_wa, v_lru_ba, v_lru_wx, v_lru_bx, v_lru_lambda, v_w_out, v_ln_post_g, v_ln_post_b):
    seq = x.shape[1]
    lp = seq + TB
    row = lambda a: a.reshape(1, -1)

    small_shard_shapes = [conv_dw_w.shape, lru_conv_w.shape, meta_tokens.shape]
    small_shard = _pack_rows([conv_dw_w, lru_conv_w, meta_tokens])
    me = 4 * lax.axis_index("x") + 2 * lax.axis_index("y") + lax.axis_index("c")
    w_in_b = [w_in[l].astype(BF16) for l in range(DEPTH)]
    w_out_b = [w_out[l].astype(BF16) for l in range(DEPTH)]
    pw_b = conv_pw_w.astype(BF16)
    wgroups = [
        ([small_shard, w_in_b[0]], [_landing(small_shard, me), _landing(w_in_b[0], me)],
         [(0, False, 0, None), (1, False, 1, None)]),
        ([pw_b, w_out_b[0]], [_landing(pw_b, me), _landing(w_out_b[0], me)],
         [(0, False, 0, None), (1, False, 1, None)]),
        ([w_in_b[1], w_out_b[1]], [_landing(w_in_b[1], me), _landing(w_out_b[1], me)],
         [(0, False, 0, None), (1, False, 1, None)]),
    ]
    wstarted, wtoken = _xchg_start("weights_start", wgroups)
    wg_small, wg_in0 = _xchg_wait("weights_wait_a", wgroups[0], wstarted[0], wtoken)
    g_dw, g_lc, g_meta = _unpack_rows(wg_small, small_shard_shapes, lead=(N_DEV,))
    conv_dw_full = _slots_to_cols(g_dw)
    lru_conv_full = _slots_to_cols(g_lc)
    meta_full = _slots_to_cols(g_meta)
    wg_in = [wg_in0, None]
    wg_out = [None, None]
    wg_pw = None

    tabs = _rope_tables(lp)
    ln_g = [ln_in_g, ln_post_g[0], ln_post_g[1]]
    ln_b = [ln_in_b, ln_post_b[0], ln_post_b[1]]

    def layer_params(l):
        wdw = jnp.pad(conv_dw_full[l], ((0, 1), (0, 0)))
        cvec = jnp.pad(jnp.stack([conv_dw_b[l], conv_ln_g[l], conv_ln_b[l], conv_pw_b[l]]), ((0, 4), (0, 0)))
        wpw = wg_pw[:, l].reshape(CONV_W, CONV_W)
        sinks = jnp.pad(attn_sinks[l].reshape(1, N_HEADS), ((0, 7), (0, 128 - N_HEADS)))
        wl = jnp.pad(lru_conv_full[l], ((0, 4), (0, 0)))
        lvec = jnp.pad(jnp.stack([lru_conv_b[l], lru_ba[l], lru_bx[l], lru_lambda[l]]), ((0, 4), (0, 0)))
        wa = _block_diag(lru_wa[l]).astype(BF16)
        wx = _block_diag(lru_wx[l]).astype(BF16)
        wout = wg_out[l].reshape(D_MODEL, D_MODEL)
        return dict(wdw=wdw, cvec=cvec, wpw=wpw, sinks=sinks, wl=wl, lvec=lvec, wa=wa, wx=wx, wout=wout)

    params = [None] * DEPTH

    z = [_embed(x, meta_full)]
    saved = []
    for l in range(DEPTH):
        hb = _ln_fwd(f"ln_fwd{l}", z[l], row(ln_g[l]), row(ln_b[l]))
        if l == 1:
            wg_in[1], wg_out[1] = _xchg_wait("weights_wait_c", wgroups[2], wstarted[2], z[1])
        proj = _mm_proj(f"proj{l}", hb, wg_in[l])
        if l == 0:
            wg_pw, wg_out[0] = _xchg_wait("weights_wait_b", wgroups[1], wstarted[1], proj)
        p = params[l] = layer_params(l)
        ycat = _conv_fwd(f"conv_fwd{l}", proj, p["wdw"], p["cvec"], p["wpw"])
        qr, kr = _rope_fwd(f"rope{l}", proj, tabs)
        ycat = _attn_fwd(f"attn_fwd{l}", qr, kr, proj, p["sinks"], ycat)
        ycat, hstate = _lru_fwd(f"lru_fwd{l}", proj, p["wl"], p["lvec"], p["wa"], p["wx"], ycat)
        z.append(_mm_out(f"out{l}", ycat, p["wout"], z[l], row(ln_g[l]), row(ln_b[l])))
        saved.append(dict(hb=hb, proj=proj, ycat=ycat, qr=qr, kr=kr, hstate=hstate))

    dz, st_post1, loss_blk = _loss_head(z[DEPTH], loss_target, row(ln_g[DEPTH]), row(ln_b[DEPTH]))
    loss = lax.psum(loss_blk[0, 0], ("x", "y", "c"))

    ln_stats = {DEPTH: st_post1}
    g_layers = [None] * DEPTH
    dwin_l, dwout_l = [None] * DEPTH, [None] * DEPTH
    grad_x = gmeta = None
    token = wtoken
    ggroups, gstarted = [None] * DEPTH, [None] * DEPTH
    own = lambda a: lax.dynamic_index_in_dim(a, me, 0, keepdims=False)
    for l in reversed(range(DEPTH)):
        p, s = params[l], saved[l]
        dycat = _mm_dycat(f"dycat{l}", dz, p["wout"], token)
        dwout_l[l] = _mm_dwout(f"dwout{l}", s["ycat"], dz)
        dproj, dwdw, dcvec, dwpw = _conv_bwd(f"conv_bwd{l}", s["proj"], dycat, p["wdw"], p["cvec"], p["wpw"])
        dq, dgate, dk, dv, dsink = _attn_bwd(f"attn_bwd{l}", s["qr"], s["kr"], s["proj"], p["sinks"], dycat)
        dproj = _attn_assemble(f"attn_asm{l}", dq, dgate, dk, dv, tabs, dproj)
        dproj, dwl, dlvec, dwa, dwx = _lru_bwd(f"lru_bwd{l}", s["proj"], dycat, s["hstate"],
                                                p["wl"], p["lvec"], p["wa"], p["wx"], dproj)
        dwin_l[l] = _mm_dwin(f"dwin{l}", s["hb"], dproj)
        dwo = dwout_l[l].reshape(N_DEV, D_MODEL // N_DEV, D_MODEL)
        dpw = dwpw.reshape(N_DEV, CONV_W // N_DEV, CONV_W)
        ggroups[l] = ([dwin_l[l], dwo, dpw],
                      [_landing(own(dwin_l[l]), me), _landing(own(dwo), me), _landing(own(dpw), me)],
                      [(0, True, 0, None), (1, True, 1, None), (2, True, 2, None)])
        started, token = _xchg_start(f"grads_start{l}", [ggroups[l]])
        gstarted[l] = started[0]
        dh = _mm_dh(f"dh{l}", dproj, wg_in[l], dz, token)
        if l > 0:
            dz, ln_stats[l] = _ln_bwd(f"ln_bwd{l}", dh, z[l], row(ln_g[l]))
        else:
            grad_x, gmeta, ln_stats[0] = _ln_bwd_input(dh, z[0], row(ln_g[0]))
        g_layers[l] = dict(dwdw=dwdw[:CONV_K], dcvec=dcvec, dwpw=dwpw, dsink=dsink[0, :N_HEADS],
                           dwl=dwl[:LRU_CONV_K], dlvec=dlvec, dwa=_diag_blocks(dwa), dwx=_diag_blocks(dwx))

    stack = lambda f: jnp.stack([f(g_layers[l]) for l in range(DEPTH)])
    g_local = dict(
        ln_in_g=ln_stats[0][0], ln_in_b=ln_stats[0][1],
        conv_dw_b=stack(lambda g: g["dcvec"][0]), conv_ln_g=stack(lambda g: g["dcvec"][1]),
        conv_ln_b=stack(lambda g: g["dcvec"][2]), conv_pw_b=stack(lambda g: g["dcvec"][3]),
        attn_sinks=stack(lambda g: g["dsink"]),
        lru_conv_b=stack(lambda g: g["dlvec"][0]), lru_wa=stack(lambda g: g["dwa"]),
        lru_ba=stack(lambda g: g["dlvec"][1]), lru_wx=stack(lambda g: g["dwx"]),
        lru_bx=stack(lambda g: g["dlvec"][2]), lru_lambda=stack(lambda g: g["dlvec"][3]),
        ln_post_g=jnp.stack([ln_stats[1][0], ln_stats[2][0]]),
        ln_post_b=jnp.stack([ln_stats[1][1], ln_stats[2][1]]),
    )
    rep_names = ["ln_in_g", "ln_in_b", "conv_dw_b", "conv_ln_g", "conv_ln_b", "conv_pw_b", "attn_sinks",
                 "lru_conv_b", "lru_wa", "lru_ba", "lru_wx", "lru_bx", "lru_lambda", "ln_post_g", "ln_post_b"]
    weights = dict(meta_tokens=meta_tokens, ln_in_g=ln_in_g, ln_in_b=ln_in_b, w_in=w_in, conv_dw_w=conv_dw_w,
                   conv_dw_b=conv_dw_b, conv_ln_g=conv_ln_g, conv_ln_b=conv_ln_b, conv_pw_w=conv_pw_w,
                   conv_pw_b=conv_pw_b, attn_sinks=attn_sinks, lru_conv_w=lru_conv_w, lru_conv_b=lru_conv_b,
                   lru_wa=lru_wa, lru_ba=lru_ba, lru_wx=lru_wx, lru_bx=lru_bx, lru_lambda=lru_lambda,
                   w_out=w_out, ln_post_g=ln_post_g, ln_post_b=ln_post_b)
    mom1 = dict(meta_tokens=m_meta_tokens, ln_in_g=m_ln_in_g, ln_in_b=m_ln_in_b, w_in=m_w_in, conv_dw_w=m_conv_dw_w,
                conv_dw_b=m_conv_dw_b, conv_ln_g=m_conv_ln_g, conv_ln_b=m_conv_ln_b, conv_pw_w=m_conv_pw_w,
                conv_pw_b=m_conv_pw_b, attn_sinks=m_attn_sinks, lru_conv_w=m_lru_conv_w, lru_conv_b=m_lru_conv_b,
                lru_wa=m_lru_wa, lru_ba=m_lru_ba, lru_wx=m_lru_wx, lru_bx=m_lru_bx, lru_lambda=m_lru_lambda,
                w_out=m_w_out, ln_post_g=m_ln_post_g, ln_post_b=m_ln_post_b)
    mom2 = dict(meta_tokens=v_meta_tokens, ln_in_g=v_ln_in_g, ln_in_b=v_ln_in_b, w_in=v_w_in, conv_dw_w=v_conv_dw_w,
                conv_dw_b=v_conv_dw_b, conv_ln_g=v_conv_ln_g, conv_ln_b=v_conv_ln_b, conv_pw_w=v_conv_pw_w,
                conv_pw_b=v_conv_pw_b, attn_sinks=v_attn_sinks, lru_conv_w=v_lru_conv_w, lru_conv_b=v_lru_conv_b,
                lru_wa=v_lru_wa, lru_ba=v_lru_ba, lru_wx=v_lru_wx, lru_bx=v_lru_bx, lru_lambda=v_lru_lambda,
                w_out=v_w_out, ln_post_g=v_ln_post_g, ln_post_b=v_ln_post_b)

    rep_pack = _pack_rows([g_local[n] for n in rep_names])
    shard_small_names = ["conv_dw_w", "lru_conv_w", "meta_tokens"]
    g_dw_full = jnp.stack([g_layers[l]["dwdw"] for l in range(DEPTH)])
    g_lc_full = jnp.stack([g_layers[l]["dwl"] for l in range(DEPTH)])
    shard_pack = _pack_rows([_cols_to_slots(g_dw_full), _cols_to_slots(g_lc_full), _cols_to_slots(gmeta)],
                            lead=(N_DEV,))
    r_small, r_rep = _exchange("exchange_small_grads", [
        ("scatter", [shard_pack]),
        ("gather", [rep_pack]),
    ])

    res = {}

    def flat2(a, cols):
        return a.reshape(-1, cols)

    big = (("w_in", 0, W_IN_SHARD), ("w_out", 1, D_MODEL), ("conv_pw_w", 2, CONV_W))
    prev = {n: None for n, _, _ in big}
    after = r_rep
    for l in reversed(range(DEPTH)):
        recvs = _xchg_wait(f"grads_wait{l}", ggroups[l], gstarted[l], after)
        for name_, gi, cols in big:
            w_ = weights[name_]
            rows = w_.shape[1]
            prev[name_] = _adamw(f"adamw_{name_}{l}", flat2(w_, cols), flat2(mom1[name_], cols),
                                 flat2(mom2[name_], cols), recvs[gi], row0=l * rows, prev=prev[name_])
        after = prev["w_in"][0]
    for name_, _, _ in big:
        res[name_] = [o.reshape(weights[name_].shape) for o in prev[name_]]

    sshapes = [weights[n].shape for n in shard_small_names]
    outs = _adamw("adamw_small_sharded",
                  _pack_rows([weights[n] for n in shard_small_names]),
                  _pack_rows([mom1[n] for n in shard_small_names]),
                  _pack_rows([mom2[n] for n in shard_small_names]),
                  r_small[:, 0])
    for k, o in enumerate(outs):
        for n, a in zip(shard_small_names, _unpack_rows(o, sshapes)):
            res.setdefault(n, [None] * 4)[k] = a

    rshapes = [weights[n].shape for n in rep_names]
    outs = _adamw("adamw_replicated",
                  _pack_rows([weights[n] for n in rep_names]),
                  _pack_rows([mom1[n] for n in rep_names]),
                  _pack_rows([mom2[n] for n in rep_names]),
                  r_rep[:, 0])
    for k, o in enumerate(outs):
        for n, a in zip(rep_names, _unpack_rows(o, rshapes)):
            res.setdefault(n, [None] * 4)[k] = a

    order = ["meta_tokens", "ln_in_g", "ln_in_b", "w_in", "conv_dw_w", "conv_dw_b", "conv_ln_g", "conv_ln_b",
             "conv_pw_w", "conv_pw_b", "attn_sinks", "lru_conv_w", "lru_conv_b", "lru_wa", "lru_ba", "lru_wx",
             "lru_bx", "lru_lambda", "w_out", "ln_post_g", "ln_post_b"]
    return (loss, grad_x,
            *[res[n][0] for n in order], *[res[n][1] for n in order],
            *[res[n][2] for n in order], *[res[n][3] for n in order])
```

```python
import functools
import math

import numpy as np
import jax
import jax.numpy as jnp
from jax import lax
from jax.experimental import pallas as pl
from jax.experimental.pallas import tpu as pltpu

F32 = jnp.float32
BF16 = jnp.bfloat16

D_MODEL = 2048
DEPTH = 2
N_META = 16
TB = 128
PAD0 = TB - N_META
CONV_W = 512
CONV_K = 31
HEAD_DIM = 64
N_HEADS = 16
N_KV = 4
GROUP = 4
ATT_W = 1024
KV_W = 256
ROT_DIM = 16
ROPE_THETA = 500000.0
LRU_W = 512
LRU_HEADS = 8
LRU_CONV_K = 4
LRU_C = 8.0
IN_TOTAL = 5120
N_DEV = 8
W_IN_SHARD = IN_TOTAL // N_DEV
LN_EPS = 1e-5
ALPHA = (2.0 * DEPTH) ** 0.25
NEG_INF = -1e30
ATT_SCALE = HEAD_DIM ** -0.5

ADAM_LR = 0.001
ADAM_B1 = 0.9
ADAM_B2 = 0.999
ADAM_EPS = 1e-08
ADAM_WD = 0.01
ADAM_STEP = 10

VMEM_LIMIT = 56 * 1024 * 1024
PACK_QUANTUM = 256 * 128

COL_CV, COL_CG, COL_CGATE = 0, 1, 2
COL_Q0 = 3
COL_K256 = 10
COL_V256 = 11
COL_AGATE1024 = 3
COL_RX, COL_RGATE = 8, 9
YC_CONV, YC_LRU = 2, 3


def _cp(n_axes, vmem=VMEM_LIMIT):
    return pltpu.CompilerParams(dimension_semantics=("arbitrary",) * n_axes, vmem_limit_bytes=vmem)


def _row_tile(lp, max_blocks):
    nb = lp // TB
    d = max(k for k in range(1, max_blocks + 1) if nb % k == 0)
    return TB * d


def _sig(x):
    return jax.nn.sigmoid(x)


def _dsilu(x, s):
    return s * (1.0 + x * (1.0 - s))


def _ln_core(z):
    mu = jnp.mean(z, axis=-1, keepdims=True)
    zc = z - mu
    var = jnp.mean(zc * zc, axis=-1, keepdims=True)
    rstd = lax.rsqrt(var + LN_EPS)
    return zc * rstd, rstd


def _ln_bwd_core(dy, xh, rstd, g):
    dxh = dy * g
    m1 = jnp.mean(dxh, axis=-1, keepdims=True)
    m2 = jnp.mean(dxh * xh, axis=-1, keepdims=True)
    return rstd * (dxh - m1 - xh * m2)


def _row_ids(shape, base):
    return lax.broadcasted_iota(jnp.int32, shape, 0) + base


def _colsum(x):
    return jnp.sum(x, axis=0, keepdims=True)


def _dot(a, b, dims):
    return lax.dot_general(a, b, (dims, ((), ())), preferred_element_type=F32)


NN = ((1,), (0,))
NT = ((1,), (1,))
TN = ((0,), (0,))


def _embed(x, meta_full):
    s = x.shape[1]
    lp = s + TB
    nb = lp // TB

    def body(x_ref, m_ref, o_ref):
        i = pl.program_id(0)

        @pl.when(i == 0)
        def _():
            o_ref[0:PAD0, :] = jnp.zeros((PAD0, D_MODEL), F32)
            o_ref[PAD0:TB, :] = m_ref[...]

        @pl.when(i > 0)
        def _():
            o_ref[...] = x_ref[...]

    return pl.pallas_call(
        body, name="embed", grid=(nb,),
        in_specs=[pl.BlockSpec((None, TB, D_MODEL), lambda i: (0, jnp.maximum(i - 1, 0), 0)),
                  pl.BlockSpec((N_META, D_MODEL), lambda i: (0, 0))],
        out_specs=pl.BlockSpec((TB, D_MODEL), lambda i: (i, 0)),
        out_shape=jax.ShapeDtypeStruct((lp, D_MODEL), F32),
        compiler_params=_cp(1),
    )(x, meta_full)


def _ln_fwd(name, z, g, b):
    lp = z.shape[0]
    tr = _row_tile(lp, 3)

    def body(z_ref, g_ref, b_ref, o_ref):
        i = pl.program_id(0)
        xh, _ = _ln_core(z_ref[...])
        h = xh * g_ref[...] + b_ref[...]
        rows = _row_ids(h.shape, i * tr)
        o_ref[...] = jnp.where(rows >= PAD0, h, 0.0).astype(BF16)

    return pl.pallas_call(
        body, name=name, grid=(lp // tr,),
        in_specs=[pl.BlockSpec((tr, D_MODEL), lambda i: (i, 0)),
                  pl.BlockSpec((1, D_MODEL), lambda i: (0, 0)),
                  pl.BlockSpec((1, D_MODEL), lambda i: (0, 0))],
        out_specs=pl.BlockSpec((tr, D_MODEL), lambda i: (i, 0)),
        out_shape=jax.ShapeDtypeStruct((lp, D_MODEL), BF16),
        compiler_params=_cp(1),
    )(z, g, b)


def _loss_head(z, target, g, b):
    lp = z.shape[0]
    nb = lp // TB

    def body(z_ref, t_ref, g_ref, b_ref, dz_ref, st_ref, loss_ref):
        i = pl.program_id(0)

        @pl.when(i == 0)
        def _():
            st_ref[...] = jnp.zeros(st_ref.shape, F32)
            loss_ref[...] = jnp.zeros(loss_ref.shape, F32)
            dz_ref[...] = jnp.zeros(dz_ref.shape, F32)

        @pl.when(i > 0)
        def _():
            xh, rstd = _ln_core(z_ref[...])
            gg = g_ref[...]
            y = xh * gg + b_ref[...]
            e = y - t_ref[...]
            part = 0.5 * jnp.sum(jnp.mean(e * e, axis=-1, keepdims=True), axis=0, keepdims=True)
            loss_ref[...] += jnp.broadcast_to(part, loss_ref.shape)
            dy = e / float(D_MODEL)
            st_ref[0:1, :] += _colsum(dy * xh)
            st_ref[1:2, :] += _colsum(dy)
            dz_ref[...] = _ln_bwd_core(dy, xh, rstd, gg)

    return pl.pallas_call(
        body, name="loss_head", grid=(nb,),
        in_specs=[pl.BlockSpec((TB, D_MODEL), lambda i: (i, 0)),
                  pl.BlockSpec((None, TB, D_MODEL), lambda i: (0, jnp.maximum(i - 1, 0), 0)),
                  pl.BlockSpec((1, D_MODEL), lambda i: (0, 0)),
                  pl.BlockSpec((1, D_MODEL), lambda i: (0, 0))],
        out_specs=[pl.BlockSpec((TB, D_MODEL), lambda i: (i, 0)),
                   pl.BlockSpec((8, D_MODEL), lambda i: (0, 0)),
                   pl.BlockSpec((8, 128), lambda i: (0, 0))],
        out_shape=[jax.ShapeDtypeStruct((lp, D_MODEL), F32),
                   jax.ShapeDtypeStruct((8, D_MODEL), F32),
                   jax.ShapeDtypeStruct((8, 128), F32)],
        compiler_params=_cp(1),
    )(z, target, g, b)


def _ln_bwd(name, dh, z, g):
    lp = z.shape[0]
    nb = lp // TB

    def body(dh_ref, z_ref, g_ref, dz_ref, st_ref):
        i = pl.program_id(0)

        @pl.when(i == 0)
        def _():
            st_ref[...] = jnp.zeros(st_ref.shape, F32)

        xh, rstd = _ln_core(z_ref[...])
        rows = _row_ids(xh.shape, i * TB)
        dy = jnp.where(rows >= PAD0, dh_ref[...], 0.0)
        st_ref[0:1, :] += _colsum(dy * xh)
        st_ref[1:2, :] += _colsum(dy)
        dz_ref[...] = _ln_bwd_core(dy, xh, rstd, g_ref[...])

    return pl.pallas_call(
        body, name=name, grid=(nb,),
        in_specs=[pl.BlockSpec((TB, D_MODEL), lambda i: (i, 0)),
                  pl.BlockSpec((TB, D_MODEL), lambda i: (i, 0)),
                  pl.BlockSpec((1, D_MODEL), lambda i: (0, 0))],
        out_specs=[pl.BlockSpec((TB, D_MODEL), lambda i: (i, 0)),
                   pl.BlockSpec((8, D_MODEL), lambda i: (0, 0))],
        out_shape=[jax.ShapeDtypeStruct((lp, D_MODEL), F32),
                   jax.ShapeDtypeStruct((8, D_MODEL), F32)],
        compiler_params=_cp(1),
    )(dh, z, g)


def _ln_bwd_input(dh, z, g):
    lp = z.shape[0]
    nb = lp // TB
    s = lp - TB

    def body(dh_ref, z_ref, g_ref, gx_ref, gm_ref, st_ref):
        i = pl.program_id(0)

        @pl.when(i == 0)
        def _():
            st_ref[...] = jnp.zeros(st_ref.shape, F32)

        xh, rstd = _ln_core(z_ref[...])
        rows = _row_ids(xh.shape, i * TB)
        dy = jnp.where(rows >= PAD0, dh_ref[...], 0.0)
        st_ref[0:1, :] += _colsum(dy * xh)
        st_ref[1:2, :] += _colsum(dy)
        dz = _ln_bwd_core(dy, xh, rstd, g_ref[...])
        gx_ref[...] = dz

        @pl.when(i == 0)
        def _():
            gm_ref[...] = dz[PAD0:TB, :]

    return pl.pallas_call(
        body, name="ln_in_bwd", grid=(nb,),
        in_specs=[pl.BlockSpec((TB, D_MODEL), lambda i: (i, 0)),
                  pl.BlockSpec((TB, D_MODEL), lambda i: (i, 0)),
                  pl.BlockSpec((1, D_MODEL), lambda i: (0, 0))],
        out_specs=[pl.BlockSpec((None, TB, D_MODEL), lambda i: (0, jnp.maximum(i - 1, 0), 0)),
                   pl.BlockSpec((N_META, D_MODEL), lambda i: (0, 0)),
                   pl.BlockSpec((8, D_MODEL), lambda i: (0, 0))],
        out_shape=[jax.ShapeDtypeStruct((1, s, D_MODEL), F32),
                   jax.ShapeDtypeStruct((N_META, D_MODEL), F32),
                   jax.ShapeDtypeStruct((8, D_MODEL), F32)],
        compiler_params=_cp(1),
    )(dh, z, g)


def _mm_proj(name, hb, wg_in):
    lp = hb.shape[0]
    tm = lp // 3

    def body(a_ref, b_ref, o_ref):
        b = jnp.concatenate([b_ref[0], b_ref[1]], axis=1)
        o_ref[...] = _dot(a_ref[...], b, NN)

    return pl.pallas_call(
        body, name=name, grid=(3, N_DEV // 2),
        in_specs=[pl.BlockSpec((tm, D_MODEL), lambda i, j: (i, 0)),
                  pl.BlockSpec((2, D_MODEL, W_IN_SHARD), lambda i, j: (j, 0, 0))],
        out_specs=pl.BlockSpec((tm, 2 * W_IN_SHARD), lambda i, j: (i, j)),
        out_shape=jax.ShapeDtypeStruct((lp, IN_TOTAL), F32),
        compiler_params=_cp(2),
    )(hb, wg_in)


def _mm_out(name, ycat, wout, z, g, b):
    lp = ycat.shape[0]
    tm = lp // 6

    def body(a_ref, w_ref, z_ref, g_ref, b_ref, o_ref):
        i = pl.program_id(0)
        xh, _ = _ln_core(z_ref[...])
        h = xh * g_ref[...] + b_ref[...]
        rows = _row_ids(h.shape, i * tm)
        h = jnp.where(rows >= PAD0, h, 0.0)
        o_ref[...] = ALPHA * h + _dot(a_ref[...], w_ref[...], NN)

    return pl.pallas_call(
        body, name=name, grid=(6,),
        in_specs=[pl.BlockSpec((tm, D_MODEL), lambda i: (i, 0)),
                  pl.BlockSpec((D_MODEL, D_MODEL), lambda i: (0, 0)),
                  pl.BlockSpec((tm, D_MODEL), lambda i: (i, 0)),
                  pl.BlockSpec((1, D_MODEL), lambda i: (0, 0)),
                  pl.BlockSpec((1, D_MODEL), lambda i: (0, 0))],
        out_specs=pl.BlockSpec((tm, D_MODEL), lambda i: (i, 0)),
        out_shape=jax.ShapeDtypeStruct((lp, D_MODEL), F32),
        compiler_params=_cp(1),
    )(ycat, wout, z, g, b)


def _mm_dycat(name, dz, wout, dep):
    lp = dz.shape[0]
    tm = lp // 6

    def body(a_ref, w_ref, dep_ref, o_ref):
        del dep_ref
        o_ref[...] = _dot(a_ref[...].astype(BF16), w_ref[...], NT).astype(BF16)

    return pl.pallas_call(
        body, name=name, grid=(6,),
        in_specs=[pl.BlockSpec((tm, D_MODEL), lambda i: (i, 0)),
                  pl.BlockSpec((D_MODEL, D_MODEL), lambda i: (0, 0)),
                  pl.BlockSpec(memory_space=pl.ANY)],
        out_specs=pl.BlockSpec((tm, D_MODEL), lambda i: (i, 0)),
        out_shape=jax.ShapeDtypeStruct((lp, D_MODEL), BF16),
        compiler_params=_cp(1),
    )(dz, wout, dep)


def _mm_dwout(name, ycat, dz):
    lp = ycat.shape[0]
    tk = _row_tile(lp, 11)
    nk = lp // tk
    half = D_MODEL // 2

    def body(a_ref, b_ref, o_ref, acc_ref):
        k = pl.program_id(1)

        @pl.when(k == 0)
        def _():
            acc_ref[...] = jnp.zeros(acc_ref.shape, F32)

        acc_ref[...] += _dot(a_ref[...], b_ref[...].astype(BF16), TN)

        @pl.when(k == nk - 1)
        def _():
            o_ref[...] = acc_ref[...].astype(BF16)

    return pl.pallas_call(
        body, name=name, grid=(2, nk),
        in_specs=[pl.BlockSpec((tk, half), lambda h, k: (k, h)),
                  pl.BlockSpec((tk, D_MODEL), lambda h, k: (k, 0))],
        out_specs=pl.BlockSpec((half, D_MODEL), lambda h, k: (h, 0)),
        out_shape=jax.ShapeDtypeStruct((D_MODEL, D_MODEL), BF16),
        scratch_shapes=[pltpu.VMEM((half, D_MODEL), F32)],
        compiler_params=_cp(2),
    )(ycat, dz)


def _mm_dwin(name, hb, dproj):
    lp = hb.shape[0]
    tk = _row_tile(lp, 11)
    nk = lp // tk

    def body(a_ref, b_ref, o_ref, acc_ref):
        k = pl.program_id(1)

        @pl.when(k == 0)
        def _():
            acc_ref[...] = jnp.zeros(acc_ref.shape, F32)

        acc_ref[...] += _dot(a_ref[...], b_ref[...], TN)

        @pl.when(k == nk - 1)
        def _():
            o_ref[0] = acc_ref[:, 0:W_IN_SHARD].astype(BF16)
            o_ref[1] = acc_ref[:, W_IN_SHARD:2 * W_IN_SHARD].astype(BF16)

    return pl.pallas_call(
        body, name=name, grid=(4, nk),
        in_specs=[pl.BlockSpec((tk, D_MODEL), lambda j, k: (k, 0)),
                  pl.BlockSpec((tk, 2 * W_IN_SHARD), lambda j, k: (k, j))],
        out_specs=pl.BlockSpec((2, D_MODEL, W_IN_SHARD), lambda j, k: (j, 0, 0)),
        out_shape=jax.ShapeDtypeStruct((N_DEV, D_MODEL, W_IN_SHARD), BF16),
        scratch_shapes=[pltpu.VMEM((D_MODEL, 2 * W_IN_SHARD), F32)],
        compiler_params=_cp(2),
    )(hb, dproj)


def _mm_dh(name, dproj, wg_in, dz, dep):
    lp = dproj.shape[0]
    tm = lp // 6

    def body(a_ref, w_ref, dz_ref, dep_ref, o_ref, acc_ref):
        del dep_ref
        k = pl.program_id(1)

        @pl.when(k == 0)
        def _():
            acc_ref[...] = jnp.zeros(acc_ref.shape, F32)

        w = jnp.concatenate([w_ref[0], w_ref[1]], axis=1)
        acc_ref[...] += _dot(a_ref[...], w, NT)

        @pl.when(k == N_DEV // 2 - 1)
        def _():
            o_ref[...] = acc_ref[...] + ALPHA * dz_ref[...]

    return pl.pallas_call(
        body, name=name, grid=(6, N_DEV // 2),
        in_specs=[pl.BlockSpec((tm, 2 * W_IN_SHARD), lambda i, k: (i, k)),
                  pl.BlockSpec((2, D_MODEL, W_IN_SHARD), lambda i, k: (k, 0, 0)),
                  pl.BlockSpec((tm, D_MODEL), lambda i, k: (i, 0)),
                  pl.BlockSpec(memory_space=pl.ANY)],
        out_specs=pl.BlockSpec((tm, D_MODEL), lambda i, k: (i, 0)),
        out_shape=jax.ShapeDtypeStruct((lp, D_MODEL), F32),
        scratch_shapes=[pltpu.VMEM((tm, D_MODEL), F32)],
        compiler_params=_cp(2),
    )(dproj, wg_in, dz, dep)


def _shifted_views(cat, n_shift, base, rows):
    total = cat.shape[0]
    rolled = [cat] + [pltpu.roll(cat, b, axis=0) for b in range(1, 8)]
    views = []
    for s in range(n_shift):
        a, b = divmod(s, 8)
        views.append(rolled[b][base - 8 * a: base - 8 * a + rows, :])
    del total
    return views


def _conv_chain(j, cv_ref, cg_ref, cvp_ref, cgp_ref, wdw_ref, vec_ref, wpw_ref):
    cv = cv_ref[...]
    sg = _sig(cg_ref[...])
    c0 = cv * sg
    c0p = jnp.where(j > 0, cvp_ref[...] * _sig(cgp_ref[...]), 0.0)
    cat = jnp.concatenate([c0p, c0], axis=0)
    views = _shifted_views(cat, CONV_K, TB, TB)
    c1 = jnp.broadcast_to(vec_ref[0:1, :], (TB, CONV_W))
    for k in range(CONV_K):
        c1 = c1 + wdw_ref[k:k + 1, :] * views[CONV_K - 1 - k]
    xh, rstd = _ln_core(c1)
    c2 = xh * vec_ref[1:2, :] + vec_ref[2:3, :]
    s2 = _sig(c2)
    c3 = c2 * s2
    c4 = _dot(c3.astype(BF16), wpw_ref[...], NN) + vec_ref[3:4, :]
    return dict(cv=cv, sg=sg, views=views, xh=xh, rstd=rstd, c2=c2, s2=s2, c3=c3, c4=c4)


def _conv_in_specs(jmap):
    def cur(col):
        return pl.BlockSpec((TB, 512), lambda n: (jmap(n), col))

    def prev(col):
        return pl.BlockSpec((TB, 512), lambda n: (jnp.maximum(jmap(n) - 1, 0), col))

    return [cur(COL_CV), cur(COL_CG), prev(COL_CV), prev(COL_CG), cur(COL_CGATE)]


def _conv_param_specs():
    return [pl.BlockSpec((32, CONV_W), lambda n: (0, 0)),
            pl.BlockSpec((8, CONV_W), lambda n: (0, 0)),
            pl.BlockSpec((CONV_W, CONV_W), lambda n: (0, 0))]


def _conv_fwd(name, proj, wdw, vec, wpw):
    lp = proj.shape[0]
    nb = lp // TB

    def body(cv_ref, cg_ref, cvp_ref, cgp_ref, gate_ref, wdw_ref, vec_ref, wpw_ref, o_ref):
        j = pl.program_id(0)
        c = _conv_chain(j, cv_ref, cg_ref, cvp_ref, cgp_ref, wdw_ref, vec_ref, wpw_ref)
        gate = gate_ref[...]
        o_ref[...] = (c["c4"] * (gate * _sig(gate))).astype(BF16)

    return pl.pallas_call(
        body, name=name, grid=(nb,),
        in_specs=_conv_in_specs(lambda n: n) + _conv_param_specs(),
        out_specs=pl.BlockSpec((TB, 512), lambda n: (n, YC_CONV)),
        out_shape=jax.ShapeDtypeStruct((lp, D_MODEL), BF16),
        compiler_params=_cp(1),
    )(proj, proj, proj, proj, proj, wdw, vec, wpw)


def _conv_bwd(name, proj, dycat, wdw, vec, wpw):
    lp = proj.shape[0]
    nb = lp // TB
    halo = 32

    def body(cv_ref, cg_ref, cvp_ref, cgp_ref, gate_ref, dy_ref, wdw_ref, vec_ref, wpw_ref,
             dp_ref, dwdw_ref, dvec_ref, dwpw_ref, carry_ref):
        n = pl.program_id(0)
        j = nb - 1 - n

        @pl.when(n == 0)
        def _():
            carry_ref[...] = jnp.zeros(carry_ref.shape, F32)
            dwdw_ref[...] = jnp.zeros(dwdw_ref.shape, F32)
            dvec_ref[...] = jnp.zeros(dvec_ref.shape, F32)
            dwpw_ref[...] = jnp.zeros(dwpw_ref.shape, F32)

        c = _conv_chain(j, cv_ref, cg_ref, cvp_ref, cgp_ref, wdw_ref, vec_ref, wpw_ref)
        dy = dy_ref[...].astype(F32)
        gate = gate_ref[...]
        sgate = _sig(gate)
        dc4 = dy * (gate * sgate)
        dgate = dy * c["c4"] * _dsilu(gate, sgate)
        dc4b = dc4.astype(BF16)
        dvec_ref[3:4, :] += _colsum(dc4)
        dwpw_ref[...] += _dot(c["c3"].astype(BF16), dc4b, TN)
        dc3 = _dot(dc4b, wpw_ref[...], NT)
        dc2 = dc3 * _dsilu(c["c2"], c["s2"])
        dvec_ref[1:2, :] += _colsum(dc2 * c["xh"])
        dvec_ref[2:3, :] += _colsum(dc2)
        dc1 = _ln_bwd_core(dc2, c["xh"], c["rstd"], vec_ref[1:2, :])
        dvec_ref[0:1, :] += _colsum(dc1)
        for k in range(CONV_K):
            dwdw_ref[k:k + 1, :] += _colsum(dc1 * c["views"][CONV_K - 1 - k])
        dcat = jnp.concatenate([dc1, carry_ref[...]], axis=0)
        total = TB + halo
        up = [dcat] + [pltpu.roll(dcat, total - b, axis=0) for b in range(1, 8)]
        dc0 = jnp.zeros((TB, CONV_W), F32)
        for k in range(CONV_K):
            a, b = divmod(CONV_K - 1 - k, 8)
            dc0 = dc0 + wdw_ref[k:k + 1, :] * up[b][8 * a: 8 * a + TB, :]
        carry_ref[...] = dc1[0:halo, :]
        sg = c["sg"]
        dcv = dc0 * sg
        dcg = dc0 * c["cv"] * sg * (1.0 - sg)
        dp_ref[:, 0:512] = dcv.astype(BF16)
        dp_ref[:, 512:1024] = dcg.astype(BF16)
        dp_ref[:, 1024:1536] = dgate.astype(BF16)

    jmap = lambda n: nb - 1 - n
    return pl.pallas_call(
        body, name=name, grid=(nb,),
        in_specs=(_conv_in_specs(jmap)
                  + [pl.BlockSpec((TB, 512), lambda n: (jmap(n), YC_CONV))]
                  + _conv_param_specs()),
        out_specs=[pl.BlockSpec((TB, 1536), lambda n: (jmap(n), 0)),
                   pl.BlockSpec((32, CONV_W), lambda n: (0, 0)),
                   pl.BlockSpec((8, CONV_W), lambda n: (0, 0)),
                   pl.BlockSpec((CONV_W, CONV_W), lambda n: (0, 0))],
        out_shape=[jax.ShapeDtypeStruct((lp, IN_TOTAL), BF16),
                   jax.ShapeDtypeStruct((32, CONV_W), F32),
                   jax.ShapeDtypeStruct((8, CONV_W), F32),
                   jax.ShapeDtypeStruct((CONV_W, CONV_W), F32)],
        scratch_shapes=[pltpu.VMEM((halo, CONV_W), F32)],
        compiler_params=_cp(1),
    )(proj, proj, proj, proj, proj, dycat, wdw, vec, wpw)


def _rope_tables(lp):
    half = ROT_DIM // 2
    inv_freq = ROPE_THETA ** (-jnp.arange(half, dtype=F32) / half)
    pos = (jnp.arange(lp, dtype=jnp.int32) - PAD0).astype(F32)
    ang = pos[:, None] * inv_freq[None, :]
    cos, sin = jnp.cos(ang), jnp.sin(ang)
    ones = jnp.ones((lp, HEAD_DIM - ROT_DIM), F32)
    zeros = jnp.zeros((lp, HEAD_DIM - ROT_DIM), F32)
    zh = jnp.zeros((lp, half), F32)
    c = jnp.concatenate([cos, cos, ones], axis=1)
    sa = jnp.concatenate([-sin, zh, zeros], axis=1)
    sb = jnp.concatenate([zh, sin, zeros], axis=1)
    tile = lambda t: jnp.tile(t, (1, KV_W // HEAD_DIM))
    return tile(c), tile(sa), tile(sb)


def _rot(x, c, sa, sb):
    w = x.shape[1]
    return x * c + pltpu.roll(x, w - 8, axis=1) * sa + pltpu.roll(x, 8, axis=1) * sb


def _rot_t(dy, c, sa, sb):
    w = dy.shape[1]
    return dy * c + pltpu.roll(dy * sa, 8, axis=1) + pltpu.roll(dy * sb, w - 8, axis=1)


def _rope_fwd(name, proj, tabs):
    lp = proj.shape[0]
    tr = _row_tile(lp, 3)

    def body(q0_ref, q1_ref, k_ref, c_ref, sa_ref, sb_ref, qr_ref, kr_ref):
        c, sa, sb = c_ref[...], sa_ref[...], sb_ref[...]
        c2 = jnp.concatenate([c, c], axis=1)
        sa2 = jnp.concatenate([sa, sa], axis=1)
        sb2 = jnp.concatenate([sb, sb], axis=1)
        qr_ref[:, 0:512] = _rot(q0_ref[...], c2, sa2, sb2).astype(BF16)
        qr_ref[:, 512:1024] = _rot(q1_ref[...], c2, sa2, sb2).astype(BF16)
        kr_ref[...] = _rot(k_ref[...], c, sa, sb).astype(BF16)

    tab = pl.BlockSpec((tr, KV_W), lambda i: (i, 0))
    return pl.pallas_call(
        body, name=name, grid=(lp // tr,),
        in_specs=[pl.BlockSpec((tr, 512), lambda i: (i, COL_Q0)),
                  pl.BlockSpec((tr, 512), lambda i: (i, COL_Q0 + 1)),
                  pl.BlockSpec((tr, KV_W), lambda i: (i, COL_K256)),
                  tab, tab, tab],
        out_specs=[pl.BlockSpec((tr, ATT_W), lambda i: (i, 0)),
                   pl.BlockSpec((tr, KV_W), lambda i: (i, 0))],
        out_shape=[jax.ShapeDtypeStruct((lp, ATT_W), BF16),
                   jax.ShapeDtypeStruct((lp, KV_W), BF16)],
        compiler_params=_cp(1),
    )(proj, proj, proj, *tabs)


def _attn_mask(j):
    qi = lax.broadcasted_iota(jnp.int32, (GROUP * TB, 3 * TB), 0) & (TB - 1)
    cc = lax.broadcasted_iota(jnp.int32, (GROUP * TB, 3 * TB), 1)
    jj = cc & (TB - 1)
    is_meta = jj >= PAD0
    p0 = (cc < TB) & is_meta & (j >= 1)
    p1 = (cc >= TB) & (cc < 2 * TB) & (jj > qi) & (j >= 2)
    p2 = (cc >= 2 * TB) & (jj <= qi) & ((j >= 1) | is_meta)
    return p0 | p1 | p2


def _lane_group(rows):
    return lax.broadcasted_iota(jnp.int32, (rows, KV_W), 1) // HEAD_DIM


def _stack_heads(x, kv, lgq):
    parts = []
    for g in range(GROUP):
        sh = ((kv - g) % GROUP) * HEAD_DIM
        moved = x if sh == 0 else pltpu.roll(x, sh, axis=1)
        parts.append(jnp.where(lgq == kv, moved, 0.0))
    return jnp.concatenate(parts, axis=0).astype(BF16)


def _unstack_heads(r, kv):
    out = None
    for g in range(GROUP):
        blk = r[g * TB:(g + 1) * TB, :]
        sh = ((g - kv) % GROUP) * HEAD_DIM
        blk = blk if sh == 0 else pltpu.roll(blk, sh, axis=1)
        out = blk if out is None else out + blk
    return out


def _sink_column(sinks, kv):
    lane = lax.broadcasted_iota(jnp.int32, (1, 128), 1)
    cols = []
    for g in range(GROUP):
        sg = jnp.sum(jnp.where(lane == kv * GROUP + g, sinks, 0.0), axis=1, keepdims=True)
        cols.append(jnp.broadcast_to(sg, (TB, 1)))
    return jnp.concatenate(cols, axis=0)


def _attn_probs(qst, km, sinkcol, valid):
    s = _dot(qst, km, NT) * ATT_SCALE
    s = jnp.where(valid, s, NEG_INF)
    m = jnp.maximum(jnp.max(s, axis=-1, keepdims=True), sinkcol)
    e = jnp.exp(s - m)
    es = jnp.exp(sinkcol - m)
    inv = 1.0 / (jnp.sum(e, axis=-1, keepdims=True) + es)
    return e * inv, es * inv


def _attn_specs(jmap):
    blk = lambda col: pl.BlockSpec((TB, KV_W), lambda n: (jmap(n), col))
    prv = lambda col: pl.BlockSpec((TB, KV_W), lambda n: (jnp.maximum(jmap(n) - 1, 0), col))
    met = lambda col: pl.BlockSpec((TB, KV_W), lambda n: (0, col))
    return dict(
        qr=pl.BlockSpec((TB, ATT_W), lambda n: (jmap(n), 0)),
        k=[met(0), prv(0), blk(0)],
        v=[met(COL_V256), prv(COL_V256), blk(COL_V256)],
        gate=pl.BlockSpec((TB, ATT_W), lambda n: (jmap(n), COL_AGATE1024)),
        sinks=pl.BlockSpec((8, 128), lambda n: (0, 0)),
    )


def _attn_fwd(name, qr, kr, proj, sinks_row, ycat):
    lp = proj.shape[0]
    nb = lp // TB
    sp = _attn_specs(lambda n: n)

    def body(qr_ref, km_ref, kp_ref, kc_ref, vm_ref, vp_ref, vc_ref, gate_ref, sink_ref, yin_ref, o_ref):
        del yin_ref
        j = pl.program_id(0)
        valid = _attn_mask(j)
        kall = jnp.concatenate([km_ref[...], kp_ref[...], kc_ref[...]], axis=0).astype(F32)
        vall = jnp.concatenate([vm_ref[...], vp_ref[...], vc_ref[...]], axis=0)
        lg = _lane_group(3 * TB)
        lgq = _lane_group(TB)
        sinks = sink_ref[0:1, :]
        for kv in range(N_KV):
            cols = slice(kv * KV_W, (kv + 1) * KV_W)
            km = jnp.where(lg == kv, kall, 0.0).astype(BF16)
            vm = jnp.where(lg == kv, vall, 0.0).astype(BF16)
            qst = _stack_heads(qr_ref[:, cols].astype(F32), kv, lgq)
            p, _ = _attn_probs(qst, km, _sink_column(sinks, kv), valid)
            att = _unstack_heads(_dot(p.astype(BF16), vm, NN), kv)
            gate = gate_ref[:, cols]
            o_ref[:, cols] = (att * (gate * _sig(gate))).astype(BF16)

    return pl.pallas_call(
        body, name=name, grid=(nb,),
        in_specs=[sp["qr"]] + sp["k"] + sp["v"] + [sp["gate"], sp["sinks"],
                                                   pl.BlockSpec(memory_space=pl.ANY)],
        out_specs=pl.BlockSpec((TB, ATT_W), lambda n: (n, 0)),
        out_shape=jax.ShapeDtypeStruct((lp, D_MODEL), BF16),
        input_output_aliases={9: 0},
        compiler_params=_cp(1),
    )(qr, kr, kr, kr, proj, proj, proj, proj, sinks_row, ycat)


def _attn_bwd(name, qr, kr, proj, sinks_row, dycat, dep):
    lp = proj.shape[0]
    nb = lp // TB
    sp = _attn_specs(lambda n: n)

    def body(qr_ref, km_ref, kp_ref, kc_ref, vm_ref, vp_ref, vc_ref, gate_ref, sink_ref, dy_ref, dep_ref,
             dq_ref, dgate_ref, dk_ref, dv_ref, dsink_ref):
        del dep_ref
        j = pl.program_id(0)

        @pl.when(j == 0)
        def _():
            dk_ref[...] = jnp.zeros(dk_ref.shape, F32)
            dv_ref[...] = jnp.zeros(dv_ref.shape, F32)
            dsink_ref[...] = jnp.zeros(dsink_ref.shape, F32)

        valid = _attn_mask(j)
        kall = jnp.concatenate([km_ref[...], kp_ref[...], kc_ref[...]], axis=0).astype(F32)
        vall = jnp.concatenate([vm_ref[...], vp_ref[...], vc_ref[...]], axis=0)
        lg = _lane_group(3 * TB)
        lgq = _lane_group(TB)
        sinks = sink_ref[0:1, :]
        lane = lax.broadcasted_iota(jnp.int32, (1, 128), 1)
        dkall = jnp.zeros((3 * TB, KV_W), F32)
        dvall = jnp.zeros((3 * TB, KV_W), F32)
        dsink = jnp.zeros((1, 128), F32)
        for kv in range(N_KV):
            cols = slice(kv * KV_W, (kv + 1) * KV_W)
            km = jnp.where(lg == kv, kall, 0.0).astype(BF16)
            vm = jnp.where(lg == kv, vall, 0.0).astype(BF16)
            qst = _stack_heads(qr_ref[:, cols].astype(F32), kv, lgq)
            gate = gate_ref[:, cols]
            sgate = _sig(gate)
            dy = dy_ref[:, cols].astype(F32)
            dout = dy * (gate * sgate)
            dost = _stack_heads(dout, kv, lgq)
            p, psink = _attn_probs(qst, km, _sink_column(sinks, kv), valid)
            pb = p.astype(BF16)
            att = _unstack_heads(_dot(pb, vm, NN), kv)
            dgate_ref[:, cols] = (dy * att * _dsilu(gate, sgate)).astype(BF16)
            dd = dout * att
            dcol = jnp.concatenate(
                [jnp.sum(jnp.where(lgq == g, dd, 0.0), axis=1, keepdims=True) for g in range(GROUP)], axis=0)
            dp = _dot(dost, vm, NT)
            ds = (p * (dp - dcol) * ATT_SCALE).astype(BF16)
            pd = psink * dcol
            for g in range(GROUP):
                tot = jnp.sum(pd[g * TB:(g + 1) * TB, :], axis=0, keepdims=True)
                dsink = dsink - jnp.where(lane == kv * GROUP + g, tot, 0.0)
            dq_ref[:, cols] = _unstack_heads(_dot(ds, km, NN), kv)
            dkall = dkall + _dot(ds, qst, TN)
            dvall = dvall + _dot(pb, dost, TN)
        dsink_ref[0:1, :] += dsink
        prev = pl.multiple_of(jnp.maximum(j - 1, 0) * TB, TB)
        cur = pl.multiple_of(j * TB, TB)
        dk_ref[0:TB, :] += dkall[0:TB]
        dv_ref[0:TB, :] += dvall[0:TB]
        dk_ref[pl.ds(prev, TB), :] += dkall[TB:2 * TB]
        dv_ref[pl.ds(prev, TB), :] += dvall[TB:2 * TB]
        dk_ref[pl.ds(cur, TB), :] += dkall[2 * TB:3 * TB]
        dv_ref[pl.ds(cur, TB), :] += dvall[2 * TB:3 * TB]

    return pl.pallas_call(
        body, name=name, grid=(nb,),
        in_specs=[sp["qr"]] + sp["k"] + sp["v"] + [sp["gate"], sp["sinks"],
                                                   pl.BlockSpec((TB, ATT_W), lambda n: (n, 0)),
                                                   pl.BlockSpec(memory_space=pl.ANY)],
        out_specs=[pl.BlockSpec((TB, ATT_W), lambda n: (n, 0)),
                   pl.BlockSpec((TB, ATT_W), lambda n: (n, 0)),
                   pl.BlockSpec((lp, KV_W), lambda n: (0, 0)),
                   pl.BlockSpec((lp, KV_W), lambda n: (0, 0)),
                   pl.BlockSpec((8, 128), lambda n: (0, 0))],
        out_shape=[jax.ShapeDtypeStruct((lp, ATT_W), F32),
                   jax.ShapeDtypeStruct((lp, ATT_W), BF16),
                   jax.ShapeDtypeStruct((lp, KV_W), F32),
                   jax.ShapeDtypeStruct((lp, KV_W), F32),
                   jax.ShapeDtypeStruct((8, 128), F32)],
        compiler_params=_cp(1),
    )(qr, kr, kr, kr, proj, proj, proj, proj, sinks_row, dycat, dep)


def _attn_assemble(name, dq, dgate, dk, dv, tabs, dproj):
    lp = dq.shape[0]
    tr = _row_tile(lp, 3)

    def body(dq_ref, dg_ref, dk_ref, dv_ref, c_ref, sa_ref, sb_ref, din_ref, o_ref):
        del din_ref
        cidx = pl.program_id(1)
        c, sa, sb = c_ref[...], sa_ref[...], sb_ref[...]

        @pl.when(cidx < 2)
        def _():
            c2 = jnp.concatenate([c, c], axis=1)
            sa2 = jnp.concatenate([sa, sa], axis=1)
            sb2 = jnp.concatenate([sb, sb], axis=1)
            o_ref[...] = _rot_t(dq_ref[...], c2, sa2, sb2).astype(BF16)

        @pl.when(cidx == 2)
        def _():
            o_ref[:, 0:KV_W] = _rot_t(dk_ref[...], c, sa, sb).astype(BF16)
            o_ref[:, KV_W:2 * KV_W] = dv_ref[...].astype(BF16)

        @pl.when(cidx > 2)
        def _():
            o_ref[...] = dg_ref[...]

    tab = pl.BlockSpec((tr, KV_W), lambda n, c: (n, 0))
    return pl.pallas_call(
        body, name=name, grid=(lp // tr, 5),
        in_specs=[pl.BlockSpec((tr, 512), lambda n, c: (n, jnp.minimum(c, 1))),
                  pl.BlockSpec((tr, 512), lambda n, c: (n, jnp.clip(c - 3, 0, 1))),
                  tab, tab,
                  tab, tab, tab,
                  pl.BlockSpec(memory_space=pl.ANY)],
        out_specs=pl.BlockSpec((tr, 512), lambda n, c: (n, COL_Q0 + c)),
        out_shape=jax.ShapeDtypeStruct((lp, IN_TOTAL), BF16),
        input_output_aliases={7: 0},
        compiler_params=_cp(2),
    )(dq, dgate, dk, dv, *tabs, dproj)


def _softplus_neg(lam):
    t = jnp.exp(-jnp.abs(lam))
    u = 1.0 + t
    den = jnp.where(u == 1.0, 1.0, u - 1.0)
    l1p = jnp.where(u == 1.0, t, jnp.log(u) * (t / den))
    return jnp.maximum(-lam, 0.0) + l1p


def _lru_chain(j, rx_ref, rxp_ref, wl_ref, vec_ref, wa_ref, wx_ref):
    rx = rx_ref[...]
    rxp = jnp.where(j > 0, rxp_ref[...], 0.0)
    cat = jnp.concatenate([rxp, rx], axis=0)
    views = [cat[8:8 + TB, :]] + [pltpu.roll(cat, s, axis=0)[8:8 + TB, :] for s in range(1, LRU_CONV_K)]
    x1 = jnp.broadcast_to(vec_ref[0:1, :], (TB, LRU_W))
    for k in range(LRU_CONV_K):
        x1 = x1 + wl_ref[k:k + 1, :] * views[LRU_CONV_K - 1 - k]
    x1b = x1.astype(BF16)
    r = _sig(_dot(x1b, wa_ref[...], NN) + vec_ref[1:2, :])
    ig = _sig(_dot(x1b, wx_ref[...], NN) + vec_ref[2:3, :])
    sp = _softplus_neg(vec_ref[3:4, :])
    log_a = -LRU_C * r * sp
    rows = _row_ids((TB, LRU_W), j * TB)
    live = rows >= PAD0
    a = jnp.where(live, jnp.exp(log_a), 0.0)
    y2 = 2.0 * log_a
    em = -jnp.tanh(0.5 * y2) * (jnp.exp(y2) + 1.0)
    mult = jnp.sqrt(em)
    return dict(views=views, x1=x1, x1b=x1b, r=r, ig=ig, sp=sp, a=a, mult=mult, live=live, a_raw=jnp.exp(log_a))


def _lru_specs(jmap):
    return [pl.BlockSpec((TB, 512), lambda n: (jmap(n), COL_RX)),
            pl.BlockSpec((8, 512), lambda n: (jnp.maximum(jmap(n) * (TB // 8) - 1, 0), COL_RX)),
            pl.BlockSpec((TB, 512), lambda n: (jmap(n), COL_RGATE))]


def _lru_param_specs():
    return [pl.BlockSpec((8, LRU_W), lambda n: (0, 0)),
            pl.BlockSpec((8, LRU_W), lambda n: (0, 0)),
            pl.BlockSpec((LRU_W, LRU_W), lambda n: (0, 0)),
            pl.BlockSpec((LRU_W, LRU_W), lambda n: (0, 0))]


def _lru_fwd(name, proj, wl, vec, wa, wx, ycat):
    lp = proj.shape[0]
    nb = lp // TB

    def body(rx_ref, rxp_ref, gate_ref, wl_ref, vec_ref, wa_ref, wx_ref, yin_ref, o_ref, h_ref, carry_ref):
        del yin_ref
        j = pl.program_id(0)

        @pl.when(j == 0)
        def _():
            carry_ref[...] = jnp.zeros(carry_ref.shape, F32)

        c = _lru_chain(j, rx_ref, rxp_ref, wl_ref, vec_ref, wa_ref, wx_ref)
        a = c["a"]
        u = jnp.where(c["live"], c["mult"] * (c["ig"] * c["x1"]), 0.0)
        rows = lax.broadcasted_iota(jnp.int32, (TB, LRU_W), 0)
        d = 1
        while d < TB:
            ap = jnp.where(rows >= d, pltpu.roll(a, d, axis=0), 1.0)
            up = jnp.where(rows >= d, pltpu.roll(u, d, axis=0), 0.0)
            u = a * up + u
            a = a * ap
            d *= 2
        h = u + a * carry_ref[0:1, :]
        carry_ref[...] = h[TB - 8:TB, :]
        carry_ref[0:1, :] = h[TB - 1:TB, :]
        h_ref[...] = h
        gate = gate_ref[...]
        o_ref[...] = (h * (gate * _sig(gate))).astype(BF16)

    return pl.pallas_call(
        body, name=name, grid=(nb,),
        in_specs=_lru_specs(lambda n: n) + _lru_param_specs() + [pl.BlockSpec(memory_space=pl.ANY)],
        out_specs=[pl.BlockSpec((TB, 512), lambda n: (n, YC_LRU)),
                   pl.BlockSpec((TB, LRU_W), lambda n: (n, 0))],
        out_shape=[jax.ShapeDtypeStruct((lp, D_MODEL), BF16),
                   jax.ShapeDtypeStruct((lp, LRU_W), F32)],
        input_output_aliases={7: 0},
        scratch_shapes=[pltpu.VMEM((8, LRU_W), F32)],
        compiler_params=_cp(1),
    )(proj, proj, proj, wl, vec, wa, wx, ycat)


def _lru_bwd(name, proj, dycat, hstate, wl, vec, wa, wx, dproj):
    lp = proj.shape[0]
    nb = lp // TB

    def body(rx_ref, rxp_ref, gate_ref, dy_ref, h_ref, hp_ref, wl_ref, vec_ref, wa_ref, wx_ref, din_ref,
             dp_ref, dwl_ref, dvec_ref, dwa_ref, dwx_ref, dhc_ref, anx_ref, dxc_ref):
        del din_ref
        n = pl.program_id(0)
        j = nb - 1 - n

        @pl.when(n == 0)
        def _():
            dhc_ref[...] = jnp.zeros(dhc_ref.shape, F32)
            anx_ref[...] = jnp.zeros(anx_ref.shape, F32)
            dxc_ref[...] = jnp.zeros(dxc_ref.shape, F32)
            dwl_ref[...] = jnp.zeros(dwl_ref.shape, F32)
            dvec_ref[...] = jnp.zeros(dvec_ref.shape, F32)
            dwa_ref[...] = jnp.zeros(dwa_ref.shape, F32)
            dwx_ref[...] = jnp.zeros(dwx_ref.shape, F32)

        c = _lru_chain(j, rx_ref, rxp_ref, wl_ref, vec_ref, wa_ref, wx_ref)
        a, mult, r, ig, x1, live = c["a"], c["mult"], c["r"], c["ig"], c["x1"], c["live"]
        h = h_ref[...]
        gate = gate_ref[...]
        sgate = _sig(gate)
        dy = dy_ref[...].astype(F32)
        gsum = dy * (gate * sgate)
        dgate = dy * h * _dsilu(gate, sgate)
        rows = lax.broadcasted_iota(jnp.int32, (TB, LRU_W), 0)
        bb = jnp.where(rows == TB - 1, anx_ref[0:1, :], pltpu.roll(a, TB - 1, axis=0))
        gg = gsum
        d = 1
        while d < TB:
            keep = rows < TB - d
            bn = jnp.where(keep, pltpu.roll(bb, TB - d, axis=0), 1.0)
            gn = jnp.where(keep, pltpu.roll(gg, TB - d, axis=0), 0.0)
            gg = gg + bb * gn
            bb = bb * bn
            d *= 2
        dh = gg + bb * dhc_ref[0:1, :]
        dhc_ref[...] = dh[0:8, :]
        anx_ref[...] = a[0:8, :]
        hprev = jnp.where(rows == 0, jnp.where(j > 0, hp_ref[7:8, :], 0.0), pltpu.roll(h, 1, axis=0))
        du = jnp.where(live, dh, 0.0)
        da = jnp.where(live, dh * hprev, 0.0)
        ar = c["a_raw"]
        dmult = du * (ig * x1)
        di = du * mult * x1
        dx1 = du * mult * ig
        dloga = da * ar - dmult * ar * ar / mult
        dr = dloga * (-LRU_C * c["sp"])
        dvec_ref[3:4, :] += _colsum(dloga * (-LRU_C * r))
        dza = dr * r * (1.0 - r)
        dzx = di * ig * (1.0 - ig)
        dzab, dzxb = dza.astype(BF16), dzx.astype(BF16)
        dvec_ref[1:2, :] += _colsum(dza)
        dvec_ref[2:3, :] += _colsum(dzx)
        dwa_ref[...] += _dot(c["x1b"], dzab, TN)
        dwx_ref[...] += _dot(c["x1b"], dzxb, TN)
        dx1 = dx1 + _dot(dzab, wa_ref[...], NT) + _dot(dzxb, wx_ref[...], NT)
        dvec_ref[0:1, :] += _colsum(dx1)
        for k in range(LRU_CONV_K):
            dwl_ref[k:k + 1, :] += _colsum(dx1 * c["views"][LRU_CONV_K - 1 - k])
        dcat = jnp.concatenate([dx1, dxc_ref[...]], axis=0)
        drx = jnp.zeros((TB, LRU_W), F32)
        for k in range(LRU_CONV_K):
            s = LRU_CONV_K - 1 - k
            view = dcat[0:TB, :] if s == 0 else pltpu.roll(dcat, TB + 8 - s, axis=0)[0:TB, :]
            drx = drx + wl_ref[k:k + 1, :] * view
        dxc_ref[...] = dx1[0:8, :]
        dp_ref[:, 0:512] = drx.astype(BF16)
        dp_ref[:, 512:1024] = dgate.astype(BF16)

        @pl.when(n == nb - 1)
        def _():
            lam = vec_ref[3:4, :]
            dvec_ref[3:4, :] = dvec_ref[3:4, :] * (-_sig(-lam))

    jmap = lambda n: nb - 1 - n
    return pl.pallas_call(
        body, name=name, grid=(nb,),
        in_specs=(_lru_specs(jmap)
                  + [pl.BlockSpec((TB, 512), lambda n: (jmap(n), YC_LRU)),
                     pl.BlockSpec((TB, LRU_W), lambda n: (jmap(n), 0)),
                     pl.BlockSpec((8, LRU_W), lambda n: (jnp.maximum(jmap(n) * (TB // 8) - 1, 0), 0))]
                  + _lru_param_specs() + [pl.BlockSpec(memory_space=pl.ANY)]),
        out_specs=[pl.BlockSpec((TB, 1024), lambda n: (jmap(n), 4)),
                   pl.BlockSpec((8, LRU_W), lambda n: (0, 0)),
                   pl.BlockSpec((8, LRU_W), lambda n: (0, 0)),
                   pl.BlockSpec((LRU_W, LRU_W), lambda n: (0, 0)),
                   pl.BlockSpec((LRU_W, LRU_W), lambda n: (0, 0))],
        out_shape=[jax.ShapeDtypeStruct((lp, IN_TOTAL), BF16),
                   jax.ShapeDtypeStruct((8, LRU_W), F32),
                   jax.ShapeDtypeStruct((8, LRU_W), F32),
                   jax.ShapeDtypeStruct((LRU_W, LRU_W), F32),
                   jax.ShapeDtypeStruct((LRU_W, LRU_W), F32)],
        input_output_aliases={10: 0},
        scratch_shapes=[pltpu.VMEM((8, LRU_W), F32), pltpu.VMEM((8, LRU_W), F32), pltpu.VMEM((8, LRU_W), F32)],
        compiler_params=_cp(1),
    )(proj, proj, proj, dycat, hstate, hstate, wl, vec, wa, wx, dproj)


def _exchange(name, items):
    flat_srcs, out_shapes, plan = [], [], []
    for it, (kind, srcs) in enumerate(items):
        shape = srcs[0].shape if kind == "gather" else srcs[0].shape[1:]
        out_shapes.append(jax.ShapeDtypeStruct((N_DEV, len(srcs)) + tuple(shape), srcs[0].dtype))
        for l, s in enumerate(srcs):
            plan.append((it, l, kind, len(flat_srcs)))
            flat_srcs.append(s)
    n_in, n_out, n_cp = len(flat_srcs), len(out_shapes), len(plan)

    def body(*refs):
        ins, outs = refs[:n_in], refs[n_in:n_in + n_out]
        send_sems, recv_sems, loc_sems = refs[n_in + n_out:]
        x, y, c = lax.axis_index("x"), lax.axis_index("y"), lax.axis_index("c")
        me = 4 * x + 2 * y + c
        remote, local = [], []
        for q, (it, l, kind, si) in enumerate(plan):
            src, out = ins[si], outs[it]
            for k in range(1, N_DEV):
                px = 1 - x if k & 4 else x
                py = 1 - y if k & 2 else y
                pc = 1 - c if k & 1 else c
                peer = 4 * px + 2 * py + pc
                cp = pltpu.make_async_remote_copy(
                    src_ref=src if kind == "gather" else src.at[peer],
                    dst_ref=out.at[me, l],
                    send_sem=send_sems.at[q * 7 + k - 1], recv_sem=recv_sems.at[q * 7 + k - 1],
                    device_id=(px, py, pc), device_id_type=pl.DeviceIdType.MESH)
                cp.start()
                remote.append(cp)
            lc = pltpu.make_async_copy(src if kind == "gather" else src.at[me], out.at[me, l], loc_sems.at[q])
            lc.start()
            local.append(lc)
        for lc in local:
            lc.wait()
        for cp in remote:
            cp.wait_send()
            cp.wait_recv()

    anyspec = pl.BlockSpec(memory_space=pl.ANY)
    outs = pl.pallas_call(
        body, name=name,
        in_specs=[anyspec] * n_in, out_specs=[anyspec] * n_out, out_shape=out_shapes,
        scratch_shapes=[pltpu.SemaphoreType.DMA((n_cp * 7,)), pltpu.SemaphoreType.DMA((n_cp * 7,)),
                        pltpu.SemaphoreType.DMA((n_cp,))],
    )(*flat_srcs)
    return list(outs)


_HBM = pl.BlockSpec(memory_space=pltpu.HBM)
_SEM = pl.BlockSpec(memory_space=pltpu.SEMAPHORE)
_ANY = pl.BlockSpec(memory_space=pl.ANY)
_EFFECT = pltpu.SideEffectType.DATAFLOW_SIDE_EFFECTING


def _hbm(a):
    return pltpu.with_memory_space_constraint(a, pltpu.HBM)


def _split_descriptors(copies, srcs, lands, send_sems, recv_sems):
    x, y, c = lax.axis_index("x"), lax.axis_index("y"), lax.axis_index("c")
    me = 4 * x + 2 * y + c
    out = []
    for q, (si, scatter, li, ll) in enumerate(copies):
        for k in range(1, N_DEV):
            px = 1 - x if k & 4 else x
            py = 1 - y if k & 2 else y
            pc = 1 - c if k & 1 else c
            peer = 4 * px + 2 * py + pc
            dst = lands[li].at[me] if ll is None else lands[li].at[me, ll]
            out.append(pltpu.make_async_remote_copy(
                src_ref=srcs[si].at[peer] if scatter else srcs[si], dst_ref=dst,
                send_sem=send_sems.at[q * 7 + k - 1], recv_sem=recv_sems.at[q * 7 + k - 1],
                device_id=(px, py, pc), device_id_type=pl.DeviceIdType.MESH))
    return out


def _xchg_start(name, groups):
    n_src = [len(g[0]) for g in groups]
    n_land = [len(g[1]) for g in groups]
    srcs = [s for g in groups for s in g[0]]
    lands = [l for g in groups for l in g[1]]
    ns, nl, ng = len(srcs), len(lands), len(groups)

    def body(*refs):
        src_refs, land_refs = refs[:ns], refs[ns:ns + nl]
        sems = refs[ns + nl:ns + nl + 2 * ng]
        token = refs[-1]
        so = lo = 0
        for gi, (_, _, copies) in enumerate(groups):
            for d in _split_descriptors(copies, src_refs[so:so + n_src[gi]], land_refs[lo:lo + n_land[gi]],
                                        sems[2 * gi], sems[2 * gi + 1]):
                d.start()
            so += n_src[gi]
            lo += n_land[gi]
        token[...] = jnp.zeros(token.shape, F32)

    out_shape, out_specs = [], []
    for g in groups:
        n = 7 * len(g[2])
        out_shape += [pltpu.SemaphoreType.DMA((n,)), pltpu.SemaphoreType.DMA((n,))]
        out_specs += [_SEM, _SEM]
    out_shape += [pltpu.HBM(l.shape, l.dtype) for l in lands]
    out_specs += [_HBM] * nl
    out_shape.append(jax.ShapeDtypeStruct((8, 128), F32))
    out_specs.append(pl.BlockSpec(memory_space=pltpu.VMEM))
    outs = pl.pallas_call(
        body, name=name, in_specs=[_HBM] * (ns + nl), out_specs=out_specs, out_shape=out_shape,
        input_output_aliases={ns + i: 2 * ng + i for i in range(nl)},
        compiler_params=pltpu.CompilerParams(has_side_effects=_EFFECT),
    )(*[_hbm(a) for a in srcs + lands])
    res, lo = [], 2 * ng
    for gi in range(ng):
        res.append((outs[2 * gi], outs[2 * gi + 1], list(outs[lo:lo + n_land[gi]])))
        lo += n_land[gi]
    return res, outs[-1]


def _xchg_wait(name, group, started, after):
    srcs, _, copies = group
    send_sems, recv_sems, lands = started
    ns, nl = len(srcs), len(lands)

    def body(*refs):
        src_refs, land_refs = refs[:ns], refs[ns:ns + nl]
        send_ref, recv_ref = refs[ns + nl], refs[ns + nl + 1]
        for d in _split_descriptors(copies, src_refs, land_refs, send_ref, recv_ref):
            d.wait_send()
            d.wait_recv()

    outs = pl.pallas_call(
        body, name=name, in_specs=[_HBM] * (ns + nl) + [_SEM, _SEM, _ANY],
        out_specs=[_HBM] * nl, out_shape=[pltpu.HBM(l.shape, l.dtype) for l in lands],
        input_output_aliases={ns + i: i for i in range(nl)},
        compiler_params=pltpu.CompilerParams(has_side_effects=_EFFECT),
    )(*[_hbm(a) for a in srcs], *lands, send_sems, recv_sems, after)
    return list(outs)


def _landing(own, me):
    land = lax.empty((N_DEV,) + own.shape, own.dtype)
    return lax.dynamic_update_slice(land, own[None], (me,) + (0,) * own.ndim)


def _adamw(name, w, m, v, recv, row0=0, prev=None):
    cdim = w.shape[1]
    r = recv.shape[1]
    tr = r
    for cand in (512, 256, 128, 64, 32, 16, 8):
        if r % cand == 0 and r > cand:
            tr = cand
            break
    assert row0 % tr == 0
    blk0 = row0 // tr
    n_prev = 0 if prev is None else 4

    def body(w_ref, m_ref, v_ref, r_ref, *rest):
        g_ref, d_ref, mo_ref, vo_ref = rest[n_prev:]
        g = r_ref[0].astype(F32)
        for s in range(1, N_DEV):
            g = g + r_ref[s].astype(F32)
        mn = ADAM_B1 * m_ref[...] + (1.0 - ADAM_B1) * g
        vn = ADAM_B2 * v_ref[...] + (1.0 - ADAM_B2) * (g * g)
        m_hat = mn / (1.0 - ADAM_B1 ** ADAM_STEP)
        v_hat = vn / (1.0 - ADAM_B2 ** ADAM_STEP)
        g_ref[...] = g
        d_ref[...] = -ADAM_LR * (m_hat / (jnp.sqrt(v_hat) + ADAM_EPS) + ADAM_WD * w_ref[...])
        mo_ref[...] = mn
        vo_ref[...] = vn

    blk = pl.BlockSpec((tr, cdim), lambda i: (i + blk0, 0))
    return pl.pallas_call(
        body, name=name, grid=(r // tr,),
        in_specs=[blk, blk, blk, pl.BlockSpec((N_DEV, tr, cdim), lambda i: (0, i, 0))] + [_ANY] * n_prev,
        out_specs=[blk, blk, blk, blk],
        out_shape=[jax.ShapeDtypeStruct(w.shape, F32)] * 4,
        input_output_aliases={4 + i: i for i in range(n_prev)},
        compiler_params=_cp(1),
    )(w, m, v, recv, *(prev or []))


def _pack_rows(arrs, lead=()):
    n = len(lead)
    flat = jnp.concatenate([a.reshape(a.shape[:n] + (-1,)) for a in arrs], axis=-1)
    size = flat.shape[-1]
    padded = -(-size // PACK_QUANTUM) * PACK_QUANTUM
    flat = jnp.pad(flat, [(0, 0)] * n + [(0, padded - size)])
    return flat.reshape(flat.shape[:n] + (padded // 128, 128))


def _unpack_rows(packed, shapes, lead=()):
    n = len(lead)
    flat = packed.reshape(packed.shape[:n] + (-1,))
    out, off = [], 0
    for s in shapes:
        size = int(np.prod(s))
        out.append(flat[..., off:off + size].reshape(packed.shape[:n] + tuple(s)))
        off += size
    return out


def _block_diag(w):
    eye = jnp.eye(LRU_HEADS, dtype=w.dtype)
    return (eye[:, None, :, None] * w[:, :, None, :]).reshape(LRU_W, LRU_W)


def _diag_blocks(dense):
    t = dense.reshape(LRU_HEADS, 64, LRU_HEADS, 64)
    return jnp.stack([t[h, :, h, :] for h in range(LRU_HEADS)], axis=0)


def _cols_to_slots(full):
    lead = full.shape[:-1]
    t = full.reshape(lead + (N_DEV, full.shape[-1] // N_DEV))
    return jnp.moveaxis(t, -2, 0)


def _slots_to_cols(slots):
    t = jnp.moveaxis(slots, 0, -2)
    return t.reshape(t.shape[:-2] + (t.shape[-2] * t.shape[-1],))


def kernel(x, meta_tokens, ln_in_g, ln_in_b, w_in, conv_dw_w, conv_dw_b, conv_ln_g, conv_ln_b, conv_pw_w, conv_pw_b, attn_sinks, lru_conv_w, lru_conv_b, lru_wa, lru_ba, lru_wx, lru_bx, lru_lambda, w_out, ln_post_g, ln_post_b, loss_target, m_meta_tokens, m_ln_in_g, m_ln_in_b, m_w_in, m_conv_dw_w, m_conv_dw_b, m_conv_ln_g, m_conv_ln_b, m_conv_pw_w, m_conv_pw_b, m_attn_sinks, m_lru_conv_w, m_lru_conv_b, m_lru_wa, m_lru_ba, m_lru_wx, m_lru_bx, m_lru_lambda, m_w_out, m_ln_post_g, m_ln_post_b, v_meta_tokens, v_ln_in_g, v_ln_in_b, v_w_in, v_conv_dw_w, v_conv_dw_b, v_conv_ln_g, v_conv_ln_b, v_conv_pw_w, v_conv_pw_b, v_attn_sinks, v_lru_conv_w, v_lru_conv_b, v_lru_wa, v_lru_ba, v_lru_wx, v_lru_bx, v_lru_lambda, v_w_out, v_ln_post_g, v_ln_post_b):
    seq = x.shape[1]
    lp = seq + TB
    row = lambda a: a.reshape(1, -1)

    small_shard_shapes = [conv_dw_w.shape, lru_conv_w.shape, meta_tokens.shape]
    small_shard = _pack_rows([conv_dw_w, lru_conv_w, meta_tokens])
    me = 4 * lax.axis_index("x") + 2 * lax.axis_index("y") + lax.axis_index("c")
    w_in_b = [w_in[l].astype(BF16) for l in range(DEPTH)]
    w_out_b = [w_out[l].astype(BF16) for l in range(DEPTH)]
    pw_b = conv_pw_w.astype(BF16)
    wgroups = [
        ([small_shard], [_landing(small_shard, me)], [(0, False, 0, None)]),
        ([w_in_b[0]], [_landing(w_in_b[0], me)], [(0, False, 0, None)]),
        ([pw_b, w_out_b[0]], [_landing(pw_b, me), _landing(w_out_b[0], me)],
         [(0, False, 0, None), (1, False, 1, None)]),
        ([w_in_b[1], w_out_b[1]], [_landing(w_in_b[1], me), _landing(w_out_b[1], me)],
         [(0, False, 0, None), (1, False, 1, None)]),
    ]
    wstarted, wtoken = _xchg_start("weights_start", wgroups)
    wg_small, = _xchg_wait("weights_wait_s", wgroups[0], wstarted[0], wtoken)
    g_dw, g_lc, g_meta = _unpack_rows(wg_small, small_shard_shapes, lead=(N_DEV,))
    conv_dw_full = _slots_to_cols(g_dw)
    lru_conv_full = _slots_to_cols(g_lc)
    meta_full = _slots_to_cols(g_meta)
    wg_in = [None, None]
    wg_out = [None, None]
    wg_pw = None

    tabs = _rope_tables(lp)
    ln_g = [ln_in_g, ln_post_g[0], ln_post_g[1]]
    ln_b = [ln_in_b, ln_post_b[0], ln_post_b[1]]

    def layer_params(l):
        wdw = jnp.pad(conv_dw_full[l], ((0, 1), (0, 0)))
        cvec = jnp.pad(jnp.stack([conv_dw_b[l], conv_ln_g[l], conv_ln_b[l], conv_pw_b[l]]), ((0, 4), (0, 0)))
        wpw = wg_pw[:, l].reshape(CONV_W, CONV_W)
        sinks = jnp.pad(attn_sinks[l].reshape(1, N_HEADS), ((0, 7), (0, 128 - N_HEADS)))
        wl = jnp.pad(lru_conv_full[l], ((0, 4), (0, 0)))
        lvec = jnp.pad(jnp.stack([lru_conv_b[l], lru_ba[l], lru_bx[l], lru_lambda[l]]), ((0, 4), (0, 0)))
        wa = _block_diag(lru_wa[l]).astype(BF16)
        wx = _block_diag(lru_wx[l]).astype(BF16)
        wo = wg_out[l].reshape(D_MODEL, D_MODEL)
        wout = jnp.concatenate([wo[512:1536], wo[0:512], wo[1536:]], axis=0)
        return dict(wdw=wdw, cvec=cvec, wpw=wpw, sinks=sinks, wl=wl, lvec=lvec, wa=wa, wx=wx, wout=wout)

    params = [None] * DEPTH

    z = [_embed(x, meta_full)]
    saved = []
    for l in range(DEPTH):
        hb = _ln_fwd(f"ln_fwd{l}", z[l], row(ln_g[l]), row(ln_b[l]))
        if l == 0:
            wg_in[0], = _xchg_wait("weights_wait_a", wgroups[1], wstarted[1], hb)
        else:
            wg_in[1], wg_out[1] = _xchg_wait("weights_wait_c", wgroups[3], wstarted[3], hb)
        proj = _mm_proj(f"proj{l}", hb, wg_in[l])
        if l == 0:
            wg_pw, wg_out[0] = _xchg_wait("weights_wait_b", wgroups[2], wstarted[2], proj)
        p = params[l] = layer_params(l)
        ycat = _conv_fwd(f"conv_fwd{l}", proj, p["wdw"], p["cvec"], p["wpw"])
        qr, kr = _rope_fwd(f"rope{l}", proj, tabs)
        ycat = _attn_fwd(f"attn_fwd{l}", qr, kr, proj, p["sinks"], ycat)
        ycat, hstate = _lru_fwd(f"lru_fwd{l}", proj, p["wl"], p["lvec"], p["wa"], p["wx"], ycat)
        z.append(_mm_out(f"out{l}", ycat, p["wout"], z[l], row(ln_g[l]), row(ln_b[l])))
        saved.append(dict(hb=hb, proj=proj, ycat=ycat, qr=qr, kr=kr, hstate=hstate))

    dz, st_post1, loss_blk = _loss_head(z[DEPTH], loss_target, row(ln_g[DEPTH]), row(ln_b[DEPTH]))
    loss = lax.psum(loss_blk[0, 0], ("x", "y", "c"))

    ln_stats = {DEPTH: st_post1}
    g_layers = [None] * DEPTH
    dwin_l, dwout_l = [None] * DEPTH, [None] * DEPTH
    grad_x = gmeta = None
    token = wtoken
    ggroups = [None] * DEPTH
    own = lambda a: lax.dynamic_index_in_dim(a, me, 0, keepdims=False)
    for l in reversed(range(DEPTH)):
        p, s = params[l], saved[l]
        dycat = _mm_dycat(f"dycat{l}", dz, p["wout"], token)
        dwout_l[l] = _mm_dwout(f"dwout{l}", s["ycat"], dz)
        dproj, dwdw, dcvec, dwpw = _conv_bwd(f"conv_bwd{l}", s["proj"], dycat, p["wdw"], p["cvec"], p["wpw"])
        dwo = jnp.concatenate([dwout_l[l][1024:1536], dwout_l[l][0:1024], dwout_l[l][1536:]], axis=0)
        dwo = dwo.reshape(N_DEV, D_MODEL // N_DEV, D_MODEL)
        dpw = dwpw.reshape(N_DEV, CONV_W // N_DEV, CONV_W)
        early = ([dwo, dpw], [_landing(own(dwo), me), _landing(own(dpw), me)],
                 [(0, True, 0, None), (1, True, 1, None)])
        started_early, token = _xchg_start(f"grads_start_out{l}", [early])
        dq, dgate, dk, dv, dsink = _attn_bwd(f"attn_bwd{l}", s["qr"], s["kr"], s["proj"], p["sinks"], dycat, token)
        dproj = _attn_assemble(f"attn_asm{l}", dq, dgate, dk, dv, tabs, dproj)
        dproj, dwl, dlvec, dwa, dwx = _lru_bwd(f"lru_bwd{l}", s["proj"], dycat, s["hstate"],
                                                p["wl"], p["lvec"], p["wa"], p["wx"], dproj)
        dwin_l[l] = _mm_dwin(f"dwin{l}", s["hb"], dproj)
        late = ([dwin_l[l]], [_landing(own(dwin_l[l]), me)], [(0, True, 0, None)])
        started_late, token = _xchg_start(f"grads_start_in{l}", [late])
        ggroups[l] = [(late, started_late[0]), (early, started_early[0])]
        dh = _mm_dh(f"dh{l}", dproj, wg_in[l], dz, token)
        if l > 0:
            dz, ln_stats[l] = _ln_bwd(f"ln_bwd{l}", dh, z[l], row(ln_g[l]))
        else:
            grad_x, gmeta, ln_stats[0] = _ln_bwd_input(dh, z[0], row(ln_g[0]))
        g_layers[l] = dict(dwdw=dwdw[:CONV_K], dcvec=dcvec, dwpw=dwpw, dsink=dsink[0, :N_HEADS],
                           dwl=dwl[:LRU_CONV_K], dlvec=dlvec, dwa=_diag_blocks(dwa), dwx=_diag_blocks(dwx))

    stack = lambda f: jnp.stack([f(g_layers[l]) for l in range(DEPTH)])
    g_local = dict(
        ln_in_g=ln_stats[0][0], ln_in_b=ln_stats[0][1],
        conv_dw_b=stack(lambda g: g["dcvec"][0]), conv_ln_g=stack(lambda g: g["dcvec"][1]),
        conv_ln_b=stack(lambda g: g["dcvec"][2]), conv_pw_b=stack(lambda g: g["dcvec"][3]),
        attn_sinks=stack(lambda g: g["dsink"]),
        lru_conv_b=stack(lambda g: g["dlvec"][0]), lru_wa=stack(lambda g: g["dwa"]),
        lru_ba=stack(lambda g: g["dlvec"][1]), lru_wx=stack(lambda g: g["dwx"]),
        lru_bx=stack(lambda g: g["dlvec"][2]), lru_lambda=stack(lambda g: g["dlvec"][3]),
        ln_post_g=jnp.stack([ln_stats[1][0], ln_stats[2][0]]),
        ln_post_b=jnp.stack([ln_stats[1][1], ln_stats[2][1]]),
    )
    rep_names = ["ln_in_g", "ln_in_b", "conv_dw_b", "conv_ln_g", "conv_ln_b", "conv_pw_b", "attn_sinks",
                 "lru_conv_b", "lru_wa", "lru_ba", "lru_wx", "lru_bx", "lru_lambda", "ln_post_g", "ln_post_b"]
    weights = dict(meta_tokens=meta_tokens, ln_in_g=ln_in_g, ln_in_b=ln_in_b, w_in=w_in, conv_dw_w=conv_dw_w,
                   conv_dw_b=conv_dw_b, conv_ln_g=conv_ln_g, conv_ln_b=conv_ln_b, conv_pw_w=conv_pw_w,
                   conv_pw_b=conv_pw_b, attn_sinks=attn_sinks, lru_conv_w=lru_conv_w, lru_conv_b=lru_conv_b,
                   lru_wa=lru_wa, lru_ba=lru_ba, lru_wx=lru_wx, lru_bx=lru_bx, lru_lambda=lru_lambda,
                   w_out=w_out, ln_post_g=ln_post_g, ln_post_b=ln_post_b)
    mom1 = dict(meta_tokens=m_meta_tokens, ln_in_g=m_ln_in_g, ln_in_b=m_ln_in_b, w_in=m_w_in, conv_dw_w=m_conv_dw_w,
                conv_dw_b=m_conv_dw_b, conv_ln_g=m_conv_ln_g, conv_ln_b=m_conv_ln_b, conv_pw_w=m_conv_pw_w,
                conv_pw_b=m_conv_pw_b, attn_sinks=m_attn_sinks, lru_conv_w=m_lru_conv_w, lru_conv_b=m_lru_conv_b,
                lru_wa=m_lru_wa, lru_ba=m_lru_ba, lru_wx=m_lru_wx, lru_bx=m_lru_bx, lru_lambda=m_lru_lambda,
                w_out=m_w_out, ln_post_g=m_ln_post_g, ln_post_b=m_ln_post_b)
    mom2 = dict(meta_tokens=v_meta_tokens, ln_in_g=v_ln_in_g, ln_in_b=v_ln_in_b, w_in=v_w_in, conv_dw_w=v_conv_dw_w,
                conv_dw_b=v_conv_dw_b, conv_ln_g=v_conv_ln_g, conv_ln_b=v_conv_ln_b, conv_pw_w=v_conv_pw_w,
                conv_pw_b=v_conv_pw_b, attn_sinks=v_attn_sinks, lru_conv_w=v_lru_conv_w, lru_conv_b=v_lru_conv_b,
                lru_wa=v_lru_wa, lru_ba=v_lru_ba, lru_wx=v_lru_wx, lru_bx=v_lru_bx, lru_lambda=v_lru_lambda,
                w_out=v_w_out, ln_post_g=v_ln_post_g, ln_post_b=v_ln_post_b)

    rep_pack = _pack_rows([g_local[n] for n in rep_names])
    shard_small_names = ["conv_dw_w", "lru_conv_w", "meta_tokens"]
    g_dw_full = jnp.stack([g_layers[l]["dwdw"] for l in range(DEPTH)])
    g_lc_full = jnp.stack([g_layers[l]["dwl"] for l in range(DEPTH)])
    shard_pack = _pack_rows([_cols_to_slots(g_dw_full), _cols_to_slots(g_lc_full), _cols_to_slots(gmeta)],
                            lead=(N_DEV,))
    r_small, r_rep = _exchange("exchange_small_grads", [
        ("scatter", [shard_pack]),
        ("gather", [rep_pack]),
    ])

    res = {}

    def flat2(a, cols):
        return a.reshape(-1, cols)

    big = (("w_in", 0, W_IN_SHARD), ("w_out", 1, D_MODEL), ("conv_pw_w", 2, CONV_W))
    prev = {n: None for n, _, _ in big}
    after = r_rep
    for l in reversed(range(DEPTH)):
        recvs = []
        for gi, (grp, started) in enumerate(ggroups[l]):
            recvs += _xchg_wait(f"grads_wait{l}_{gi}", grp, started, after)
        for name_, gi, cols in big:
            w_ = weights[name_]
            rows = w_.shape[1]
            prev[name_] = _adamw(f"adamw_{name_}{l}", flat2(w_, cols), flat2(mom1[name_], cols),
                                 flat2(mom2[name_], cols), recvs[gi], row0=l * rows, prev=prev[name_])
        after = prev["w_in"][0]
    for name_, _, _ in big:
        res[name_] = [o.reshape(weights[name_].shape) for o in prev[name_]]

    sshapes = [weights[n].shape for n in shard_small_names]
    outs = _adamw("adamw_small_sharded",
                  _pack_rows([weights[n] for n in shard_small_names]),
                  _pack_rows([mom1[n] for n in shard_small_names]),
                  _pack_rows([mom2[n] for n in shard_small_names]),
                  r_small[:, 0])
    for k, o in enumerate(outs):
        for n, a in zip(shard_small_names, _unpack_rows(o, sshapes)):
            res.setdefault(n, [None] * 4)[k] = a

    rshapes = [weights[n].shape for n in rep_names]
    outs = _adamw("adamw_replicated",
                  _pack_rows([weights[n] for n in rep_names]),
                  _pack_rows([mom1[n] for n in rep_names]),
                  _pack_rows([mom2[n] for n in rep_names]),
                  r_rep[:, 0])
    for k, o in enumerate(outs):
        for n, a in zip(rep_names, _unpack_rows(o, rshapes)):
            res.setdefault(n, [None] * 4)[k] = a

    order = ["meta_tokens", "ln_in_g", "ln_in_b", "w_in", "conv_dw_w", "conv_dw_b", "conv_ln_g", "conv_ln_b",
             "conv_pw_w", "conv_pw_b", "attn_sinks", "lru_conv_w", "lru_conv_b", "lru_wa", "lru_ba", "lru_wx",
             "lru_bx", "lru_lambda", "w_out", "ln_post_g", "ln_post_b"]
    return (loss, grad_x,
            *[res[n][0] for n in order], *[res[n][1] for n in order],
            *[res[n][2] for n in order], *[res[n][3] for n in order])
```

```python
import functools
import math

import numpy as np
import jax
import jax.numpy as jnp
from jax import lax
from jax.experimental import pallas as pl
from jax.experimental.pallas import tpu as pltpu

F32 = jnp.float32
BF16 = jnp.bfloat16

D_MODEL = 2048
DEPTH = 2
N_META = 16
TB = 128
PAD0 = TB - N_META
CONV_W = 512
CONV_K = 31
HEAD_DIM = 64
N_HEADS = 16
N_KV = 4
GROUP = 4
ATT_W = 1024
KV_W = 256
ROT_DIM = 16
ROPE_THETA = 500000.0
LRU_W = 512
LRU_HEADS = 8
LRU_CONV_K = 4
LRU_C = 8.0
IN_TOTAL = 5120
N_DEV = 8
W_IN_SHARD = IN_TOTAL // N_DEV
LN_EPS = 1e-5
ALPHA = (2.0 * DEPTH) ** 0.25
NEG_INF = -1e30
ATT_SCALE = HEAD_DIM ** -0.5

ADAM_LR = 0.001
ADAM_B1 = 0.9
ADAM_B2 = 0.999
ADAM_EPS = 1e-08
ADAM_WD = 0.01
ADAM_STEP = 10

VMEM_LIMIT = 56 * 1024 * 1024
PACK_QUANTUM = 256 * 128

COL_CV, COL_CG, COL_CGATE = 0, 1, 2
COL_Q0 = 3
COL_K256 = 10
COL_V256 = 11
COL_AGATE1024 = 3
COL_RX, COL_RGATE = 8, 9
YC_CONV, YC_LRU = 2, 3


def _cp(n_axes, vmem=VMEM_LIMIT):
    return pltpu.CompilerParams(dimension_semantics=("arbitrary",) * n_axes, vmem_limit_bytes=vmem)


def _row_tile(lp, max_blocks):
    nb = lp // TB
    d = max(k for k in range(1, max_blocks + 1) if nb % k == 0)
    return TB * d


def _sig(x):
    return jax.nn.sigmoid(x)


def _dsilu(x, s):
    return s * (1.0 + x * (1.0 - s))


def _ln_core(z):
    mu = jnp.mean(z, axis=-1, keepdims=True)
    zc = z - mu
    var = jnp.mean(zc * zc, axis=-1, keepdims=True)
    rstd = lax.rsqrt(var + LN_EPS)
    return zc * rstd, rstd


def _ln_bwd_core(dy, xh, rstd, g):
    dxh = dy * g
    m1 = jnp.mean(dxh, axis=-1, keepdims=True)
    m2 = jnp.mean(dxh * xh, axis=-1, keepdims=True)
    return rstd * (dxh - m1 - xh * m2)


def _row_ids(shape, base):
    return lax.broadcasted_iota(jnp.int32, shape, 0) + base


def _colsum(x):
    return jnp.sum(x, axis=0, keepdims=True)


def _dot(a, b, dims):
    return lax.dot_general(a, b, (dims, ((), ())), preferred_element_type=F32)


NN = ((1,), (0,))
NT = ((1,), (1,))
TN = ((0,), (0,))


def _embed(x, meta_full, g, b):
    s = x.shape[1]
    lp = s + TB
    nb = lp // TB

    def body(x_ref, m_ref, g_ref, b_ref, o_ref, hb_ref):
        i = pl.program_id(0)

        @pl.when(i == 0)
        def _():
            o_ref[0:PAD0, :] = jnp.zeros((PAD0, D_MODEL), F32)
            o_ref[PAD0:TB, :] = m_ref[...]

        @pl.when(i > 0)
        def _():
            o_ref[...] = x_ref[...]

        xh, _ = _ln_core(o_ref[...])
        h = xh * g_ref[...] + b_ref[...]
        rows = _row_ids(h.shape, i * TB)
        hb_ref[...] = jnp.where(rows >= PAD0, h, 0.0).astype(BF16)

    return pl.pallas_call(
        body, name="embed", grid=(nb,),
        in_specs=[pl.BlockSpec((None, TB, D_MODEL), lambda i: (0, jnp.maximum(i - 1, 0), 0)),
                  pl.BlockSpec((N_META, D_MODEL), lambda i: (0, 0)),
                  pl.BlockSpec((1, D_MODEL), lambda i: (0, 0)),
                  pl.BlockSpec((1, D_MODEL), lambda i: (0, 0))],
        out_specs=[pl.BlockSpec((TB, D_MODEL), lambda i: (i, 0)),
                   pl.BlockSpec((TB, D_MODEL), lambda i: (i, 0))],
        out_shape=[jax.ShapeDtypeStruct((lp, D_MODEL), F32),
                   jax.ShapeDtypeStruct((lp, D_MODEL), BF16)],
        compiler_params=_cp(1),
    )(x, meta_full, g, b)


def _loss_head(z, target, g, b):
    lp = z.shape[0]
    nb = lp // TB

    def body(z_ref, t_ref, g_ref, b_ref, dz_ref, st_ref, loss_ref):
        i = pl.program_id(0)

        @pl.when(i == 0)
        def _():
            st_ref[...] = jnp.zeros(st_ref.shape, F32)
            loss_ref[...] = jnp.zeros(loss_ref.shape, F32)
            dz_ref[...] = jnp.zeros(dz_ref.shape, F32)

        @pl.when(i > 0)
        def _():
            xh, rstd = _ln_core(z_ref[...])
            gg = g_ref[...]
            y = xh * gg + b_ref[...]
            e = y - t_ref[...]
            part = 0.5 * jnp.sum(jnp.mean(e * e, axis=-1, keepdims=True), axis=0, keepdims=True)
            loss_ref[...] += jnp.broadcast_to(part, loss_ref.shape)
            dy = e / float(D_MODEL)
            st_ref[0:1, :] += _colsum(dy * xh)
            st_ref[1:2, :] += _colsum(dy)
            dz_ref[...] = _ln_bwd_core(dy, xh, rstd, gg)

    return pl.pallas_call(
        body, name="loss_head", grid=(nb,),
        in_specs=[pl.BlockSpec((TB, D_MODEL), lambda i: (i, 0)),
                  pl.BlockSpec((None, TB, D_MODEL), lambda i: (0, jnp.maximum(i - 1, 0), 0)),
                  pl.BlockSpec((1, D_MODEL), lambda i: (0, 0)),
                  pl.BlockSpec((1, D_MODEL), lambda i: (0, 0))],
        out_specs=[pl.BlockSpec((TB, D_MODEL), lambda i: (i, 0)),
                   pl.BlockSpec((8, D_MODEL), lambda i: (0, 0)),
                   pl.BlockSpec((8, 128), lambda i: (0, 0))],
        out_shape=[jax.ShapeDtypeStruct((lp, D_MODEL), F32),
                   jax.ShapeDtypeStruct((8, D_MODEL), F32),
                   jax.ShapeDtypeStruct((8, 128), F32)],
        compiler_params=_cp(1),
    )(z, target, g, b)


def _ln_bwd(name, dh, z, g):
    lp = z.shape[0]
    nb = lp // TB

    def body(dh_ref, z_ref, g_ref, dz_ref, st_ref):
        i = pl.program_id(0)

        @pl.when(i == 0)
        def _():
            st_ref[...] = jnp.zeros(st_ref.shape, F32)

        xh, rstd = _ln_core(z_ref[...])
        rows = _row_ids(xh.shape, i * TB)
        dy = jnp.where(rows >= PAD0, dh_ref[...], 0.0)
        st_ref[0:1, :] += _colsum(dy * xh)
        st_ref[1:2, :] += _colsum(dy)
        dz_ref[...] = _ln_bwd_core(dy, xh, rstd, g_ref[...])

    return pl.pallas_call(
        body, name=name, grid=(nb,),
        in_specs=[pl.BlockSpec((TB, D_MODEL), lambda i: (i, 0)),
                  pl.BlockSpec((TB, D_MODEL), lambda i: (i, 0)),
                  pl.BlockSpec((1, D_MODEL), lambda i: (0, 0))],
        out_specs=[pl.BlockSpec((TB, D_MODEL), lambda i: (i, 0)),
                   pl.BlockSpec((8, D_MODEL), lambda i: (0, 0))],
        out_shape=[jax.ShapeDtypeStruct((lp, D_MODEL), F32),
                   jax.ShapeDtypeStruct((8, D_MODEL), F32)],
        compiler_params=_cp(1),
    )(dh, z, g)


def _ln_bwd_input(dh, z, g):
    lp = z.shape[0]
    nb = lp // TB
    s = lp - TB

    def body(dh_ref, z_ref, g_ref, gx_ref, gm_ref, st_ref):
        i = pl.program_id(0)

        @pl.when(i == 0)
        def _():
            st_ref[...] = jnp.zeros(st_ref.shape, F32)

        xh, rstd = _ln_core(z_ref[...])
        rows = _row_ids(xh.shape, i * TB)
        dy = jnp.where(rows >= PAD0, dh_ref[...], 0.0)
        st_ref[0:1, :] += _colsum(dy * xh)
        st_ref[1:2, :] += _colsum(dy)
        dz = _ln_bwd_core(dy, xh, rstd, g_ref[...])
        gx_ref[...] = dz

        @pl.when(i == 0)
        def _():
            gm_ref[...] = dz[PAD0:TB, :]

    return pl.pallas_call(
        body, name="ln_in_bwd", grid=(nb,),
        in_specs=[pl.BlockSpec((TB, D_MODEL), lambda i: (i, 0)),
                  pl.BlockSpec((TB, D_MODEL), lambda i: (i, 0)),
                  pl.BlockSpec((1, D_MODEL), lambda i: (0, 0))],
        out_specs=[pl.BlockSpec((None, TB, D_MODEL), lambda i: (0, jnp.maximum(i - 1, 0), 0)),
                   pl.BlockSpec((N_META, D_MODEL), lambda i: (0, 0)),
                   pl.BlockSpec((8, D_MODEL), lambda i: (0, 0))],
        out_shape=[jax.ShapeDtypeStruct((1, s, D_MODEL), F32),
                   jax.ShapeDtypeStruct((N_META, D_MODEL), F32),
                   jax.ShapeDtypeStruct((8, D_MODEL), F32)],
        compiler_params=_cp(1),
    )(dh, z, g)


def _mm_proj(name, hb, wg_in):
    lp = hb.shape[0]
    tm = lp // 3

    def body(a_ref, b_ref, o_ref):
        b = jnp.concatenate([b_ref[0], b_ref[1]], axis=1)
        o_ref[...] = _dot(a_ref[...], b, NN)

    return pl.pallas_call(
        body, name=name, grid=(3, N_DEV // 2),
        in_specs=[pl.BlockSpec((tm, D_MODEL), lambda i, j: (i, 0)),
                  pl.BlockSpec((2, D_MODEL, W_IN_SHARD), lambda i, j: (j, 0, 0))],
        out_specs=pl.BlockSpec((tm, 2 * W_IN_SHARD), lambda i, j: (i, j)),
        out_shape=jax.ShapeDtypeStruct((lp, IN_TOTAL), F32),
        compiler_params=_cp(2),
    )(hb, wg_in)


def _mm_out(name, ycat, wout, z, g, b, g2, b2):
    lp = ycat.shape[0]
    tm = lp // 6

    def body(a_ref, w_ref, z_ref, g_ref, b_ref, g2_ref, b2_ref, o_ref, hb_ref):
        i = pl.program_id(0)
        xh, _ = _ln_core(z_ref[...])
        h = xh * g_ref[...] + b_ref[...]
        live = _row_ids(h.shape, i * tm) >= PAD0
        h = jnp.where(live, h, 0.0)
        zn = ALPHA * h + _dot(a_ref[...], w_ref[...], NN)
        o_ref[...] = zn
        xh2, _ = _ln_core(zn)
        hb_ref[...] = jnp.where(live, xh2 * g2_ref[...] + b2_ref[...], 0.0).astype(BF16)

    vec = pl.BlockSpec((1, D_MODEL), lambda i: (0, 0))
    return pl.pallas_call(
        body, name=name, grid=(6,),
        in_specs=[pl.BlockSpec((tm, D_MODEL), lambda i: (i, 0)),
                  pl.BlockSpec((D_MODEL, D_MODEL), lambda i: (0, 0), pipeline_mode=pl.Buffered(1)),
                  pl.BlockSpec((tm, D_MODEL), lambda i: (i, 0)),
                  vec, vec, vec, vec],
        out_specs=[pl.BlockSpec((tm, D_MODEL), lambda i: (i, 0)),
                   pl.BlockSpec((tm, D_MODEL), lambda i: (i, 0))],
        out_shape=[jax.ShapeDtypeStruct((lp, D_MODEL), F32),
                   jax.ShapeDtypeStruct((lp, D_MODEL), BF16)],
        compiler_params=_cp(1),
    )(ycat, wout, z, g, b, g2, b2)


def _mm_dycat(name, dz, wout, dep):
    lp = dz.shape[0]
    tm = lp // 6

    def body(a_ref, w_ref, dep_ref, o_ref):
        del dep_ref
        o_ref[...] = _dot(a_ref[...].astype(BF16), w_ref[...], NT).astype(BF16)

    return pl.pallas_call(
        body, name=name, grid=(6,),
        in_specs=[pl.BlockSpec((tm, D_MODEL), lambda i: (i, 0)),
                  pl.BlockSpec((D_MODEL, D_MODEL), lambda i: (0, 0), pipeline_mode=pl.Buffered(1)),
                  pl.BlockSpec(memory_space=pl.ANY)],
        out_specs=pl.BlockSpec((tm, D_MODEL), lambda i: (i, 0)),
        out_shape=jax.ShapeDtypeStruct((lp, D_MODEL), BF16),
        compiler_params=_cp(1),
    )(dz, wout, dep)


def _mm_dwout(name, ycat, dz):
    lp = ycat.shape[0]
    tk = _row_tile(lp, 11)
    nk = lp // tk
    half = D_MODEL // 2

    def body(a_ref, b_ref, o_ref, acc_ref):
        k = pl.program_id(1)

        @pl.when(k == 0)
        def _():
            acc_ref[...] = jnp.zeros(acc_ref.shape, F32)

        acc_ref[...] += _dot(a_ref[...], b_ref[...].astype(BF16), TN)

        @pl.when(k == nk - 1)
        def _():
            o_ref[...] = acc_ref[...].astype(BF16)

    return pl.pallas_call(
        body, name=name, grid=(2, nk),
        in_specs=[pl.BlockSpec((tk, half), lambda h, k: (k, h)),
                  pl.BlockSpec((tk, D_MODEL), lambda h, k: (k, 0))],
        out_specs=pl.BlockSpec((half, D_MODEL), lambda h, k: (h, 0)),
        out_shape=jax.ShapeDtypeStruct((D_MODEL, D_MODEL), BF16),
        scratch_shapes=[pltpu.VMEM((half, D_MODEL), F32)],
        compiler_params=_cp(2),
    )(ycat, dz)


def _mm_dwin(name, hb, dproj):
    lp = hb.shape[0]
    tk = _row_tile(lp, 11)
    nk = lp // tk

    def body(a_ref, b_ref, o_ref, acc_ref):
        k = pl.program_id(1)

        @pl.when(k == 0)
        def _():
            acc_ref[...] = jnp.zeros(acc_ref.shape, F32)

        acc_ref[...] += _dot(a_ref[...], b_ref[...], TN)

        @pl.when(k == nk - 1)
        def _():
            o_ref[0] = acc_ref[:, 0:W_IN_SHARD].astype(BF16)
            o_ref[1] = acc_ref[:, W_IN_SHARD:2 * W_IN_SHARD].astype(BF16)

    return pl.pallas_call(
        body, name=name, grid=(4, nk),
        in_specs=[pl.BlockSpec((tk, D_MODEL), lambda j, k: (k, 0)),
                  pl.BlockSpec((tk, 2 * W_IN_SHARD), lambda j, k: (k, j))],
        out_specs=pl.BlockSpec((2, D_MODEL, W_IN_SHARD), lambda j, k: (j, 0, 0)),
        out_shape=jax.ShapeDtypeStruct((N_DEV, D_MODEL, W_IN_SHARD), BF16),
        scratch_shapes=[pltpu.VMEM((D_MODEL, 2 * W_IN_SHARD), F32)],
        compiler_params=_cp(2),
    )(hb, dproj)


def _mm_dh(name, dproj, wg_in, dz, dep):
    lp = dproj.shape[0]
    tm = lp // 6

    def body(a_ref, w_ref, dz_ref, dep_ref, o_ref, acc_ref):
        del dep_ref
        k = pl.program_id(1)

        @pl.when(k == 0)
        def _():
            acc_ref[...] = jnp.zeros(acc_ref.shape, F32)

        w = jnp.concatenate([w_ref[0], w_ref[1]], axis=1)
        acc_ref[...] += _dot(a_ref[...], w, NT)

        @pl.when(k == N_DEV // 2 - 1)
        def _():
            o_ref[...] = acc_ref[...] + ALPHA * dz_ref[...]

    return pl.pallas_call(
        body, name=name, grid=(6, N_DEV // 2),
        in_specs=[pl.BlockSpec((tm, 2 * W_IN_SHARD), lambda i, k: (i, k)),
                  pl.BlockSpec((2, D_MODEL, W_IN_SHARD), lambda i, k: (k, 0, 0)),
                  pl.BlockSpec((tm, D_MODEL), lambda i, k: (i, 0)),
                  pl.BlockSpec(memory_space=pl.ANY)],
        out_specs=pl.BlockSpec((tm, D_MODEL), lambda i, k: (i, 0)),
        out_shape=jax.ShapeDtypeStruct((lp, D_MODEL), F32),
        scratch_shapes=[pltpu.VMEM((tm, D_MODEL), F32)],
        compiler_params=_cp(2),
    )(dproj, wg_in, dz, dep)


def _shifted_views(cat, n_shift, base, rows):
    total = cat.shape[0]
    rolled = [cat] + [pltpu.roll(cat, b, axis=0) for b in range(1, 8)]
    views = []
    for s in range(n_shift):
        a, b = divmod(s, 8)
        views.append(rolled[b][base - 8 * a: base - 8 * a + rows, :])
    del total
    return views


def _conv_chain(j, cv_ref, cg_ref, cvp_ref, cgp_ref, wdw_ref, vec_ref, wpw_ref):
    cv = cv_ref[...]
    sg = _sig(cg_ref[...])
    c0 = cv * sg
    c0p = jnp.where(j > 0, cvp_ref[...] * _sig(cgp_ref[...]), 0.0)
    cat = jnp.concatenate([c0p, c0], axis=0)
    views = _shifted_views(cat, CONV_K, TB, TB)
    c1 = jnp.broadcast_to(vec_ref[0:1, :], (TB, CONV_W))
    for k in range(CONV_K):
        c1 = c1 + wdw_ref[k:k + 1, :] * views[CONV_K - 1 - k]
    xh, rstd = _ln_core(c1)
    c2 = xh * vec_ref[1:2, :] + vec_ref[2:3, :]
    s2 = _sig(c2)
    c3 = c2 * s2
    c4 = _dot(c3.astype(BF16), wpw_ref[...], NN) + vec_ref[3:4, :]
    return dict(cv=cv, sg=sg, views=views, xh=xh, rstd=rstd, c2=c2, s2=s2, c3=c3, c4=c4)


def _conv_in_specs(jmap):
    def cur(col):
        return pl.BlockSpec((TB, 512), lambda n: (jmap(n), col))

    def prev(col):
        return pl.BlockSpec((TB, 512), lambda n: (jnp.maximum(jmap(n) - 1, 0), col))

    return [cur(COL_CV), cur(COL_CG), prev(COL_CV), prev(COL_CG), cur(COL_CGATE)]


def _conv_param_specs():
    return [pl.BlockSpec((32, CONV_W), lambda n: (0, 0)),
            pl.BlockSpec((8, CONV_W), lambda n: (0, 0)),
            pl.BlockSpec((CONV_W, CONV_W), lambda n: (0, 0))]


def _conv_fwd(name, proj, wdw, vec, wpw):
    lp = proj.shape[0]
    nb = lp // TB

    def body(cv_ref, cg_ref, cvp_ref, cgp_ref, gate_ref, wdw_ref, vec_ref, wpw_ref, o_ref):
        j = pl.program_id(0)
        c = _conv_chain(j, cv_ref, cg_ref, cvp_ref, cgp_ref, wdw_ref, vec_ref, wpw_ref)
        gate = gate_ref[...]
        o_ref[...] = (c["c4"] * (gate * _sig(gate))).astype(BF16)

    return pl.pallas_call(
        body, name=name, grid=(nb,),
        in_specs=_conv_in_specs(lambda n: n) + _conv_param_specs(),
        out_specs=pl.BlockSpec((TB, 512), lambda n: (n, YC_CONV)),
        out_shape=jax.ShapeDtypeStruct((lp, D_MODEL), BF16),
        compiler_params=_cp(1),
    )(proj, proj, proj, proj, proj, wdw, vec, wpw)


def _conv_bwd(name, proj, dycat, wdw, vec, wpw):
    lp = proj.shape[0]
    nb = lp // TB
    halo = 32

    def body(cv_ref, cg_ref, cvp_ref, cgp_ref, gate_ref, dy_ref, wdw_ref, vec_ref, wpw_ref,
             dp_ref, dwdw_ref, dvec_ref, dwpw_ref, carry_ref):
        n = pl.program_id(0)
        j = nb - 1 - n

        @pl.when(n == 0)
        def _():
            carry_ref[...] = jnp.zeros(carry_ref.shape, F32)
            dwdw_ref[...] = jnp.zeros(dwdw_ref.shape, F32)
            dvec_ref[...] = jnp.zeros(dvec_ref.shape, F32)
            dwpw_ref[...] = jnp.zeros(dwpw_ref.shape, F32)

        c = _conv_chain(j, cv_ref, cg_ref, cvp_ref, cgp_ref, wdw_ref, vec_ref, wpw_ref)
        dy = dy_ref[...].astype(F32)
        gate = gate_ref[...]
        sgate = _sig(gate)
        dc4 = dy * (gate * sgate)
        dgate = dy * c["c4"] * _dsilu(gate, sgate)
        dc4b = dc4.astype(BF16)
        dvec_ref[3:4, :] += _colsum(dc4)
        dwpw_ref[...] += _dot(c["c3"].astype(BF16), dc4b, TN)
        dc3 = _dot(dc4b, wpw_ref[...], NT)
        dc2 = dc3 * _dsilu(c["c2"], c["s2"])
        dvec_ref[1:2, :] += _colsum(dc2 * c["xh"])
        dvec_ref[2:3, :] += _colsum(dc2)
        dc1 = _ln_bwd_core(dc2, c["xh"], c["rstd"], vec_ref[1:2, :])
        dvec_ref[0:1, :] += _colsum(dc1)
        for k in range(CONV_K):
            dwdw_ref[k:k + 1, :] += _colsum(dc1 * c["views"][CONV_K - 1 - k])
        dcat = jnp.concatenate([dc1, carry_ref[...]], axis=0)
        total = TB + halo
        up = [dcat] + [pltpu.roll(dcat, total - b, axis=0) for b in range(1, 8)]
        dc0 = jnp.zeros((TB, CONV_W), F32)
        for k in range(CONV_K):
            a, b = divmod(CONV_K - 1 - k, 8)
            dc0 = dc0 + wdw_ref[k:k + 1, :] * up[b][8 * a: 8 * a + TB, :]
        carry_ref[...] = dc1[0:halo, :]
        sg = c["sg"]
        dcv = dc0 * sg
        dcg = dc0 * c["cv"] * sg * (1.0 - sg)
        dp_ref[:, 0:512] = dcv.astype(BF16)
        dp_ref[:, 512:1024] = dcg.astype(BF16)
        dp_ref[:, 1024:1536] = dgate.astype(BF16)

    jmap = lambda n: nb - 1 - n
    return pl.pallas_call(
        body, name=name, grid=(nb,),
        in_specs=(_conv_in_specs(jmap)
                  + [pl.BlockSpec((TB, 512), lambda n: (jmap(n), YC_CONV))]
                  + _conv_param_specs()),
        out_specs=[pl.BlockSpec((TB, 1536), lambda n: (jmap(n), 0)),
                   pl.BlockSpec((32, CONV_W), lambda n: (0, 0)),
                   pl.BlockSpec((8, CONV_W), lambda n: (0, 0)),
                   pl.BlockSpec((CONV_W, CONV_W), lambda n: (0, 0))],
        out_shape=[jax.ShapeDtypeStruct((lp, IN_TOTAL), BF16),
                   jax.ShapeDtypeStruct((32, CONV_W), F32),
                   jax.ShapeDtypeStruct((8, CONV_W), F32),
                   jax.ShapeDtypeStruct((CONV_W, CONV_W), F32)],
        scratch_shapes=[pltpu.VMEM((halo, CONV_W), F32)],
        compiler_params=_cp(1),
    )(proj, proj, proj, proj, proj, dycat, wdw, vec, wpw)


def _rope_tables(lp):
    half = ROT_DIM // 2
    inv_freq = ROPE_THETA ** (-jnp.arange(half, dtype=F32) / half)
    pos = (jnp.arange(lp, dtype=jnp.int32) - PAD0).astype(F32)
    ang = pos[:, None] * inv_freq[None, :]
    cos, sin = jnp.cos(ang), jnp.sin(ang)
    ones = jnp.ones((lp, HEAD_DIM - ROT_DIM), F32)
    zeros = jnp.zeros((lp, HEAD_DIM - ROT_DIM), F32)
    zh = jnp.zeros((lp, half), F32)
    c = jnp.concatenate([cos, cos, ones], axis=1)
    sa = jnp.concatenate([-sin, zh, zeros], axis=1)
    sb = jnp.concatenate([zh, sin, zeros], axis=1)
    tile = lambda t: jnp.tile(t, (1, KV_W // HEAD_DIM))
    return tile(c), tile(sa), tile(sb)


def _rot(x, c, sa, sb):
    w = x.shape[1]
    return x * c + pltpu.roll(x, w - 8, axis=1) * sa + pltpu.roll(x, 8, axis=1) * sb


def _rot_t(dy, c, sa, sb):
    w = dy.shape[1]
    return dy * c + pltpu.roll(dy * sa, 8, axis=1) + pltpu.roll(dy * sb, w - 8, axis=1)


def _rope_fwd(name, proj, tabs):
    lp = proj.shape[0]
    tr = _row_tile(lp, 3)

    def body(q0_ref, q1_ref, k_ref, c_ref, sa_ref, sb_ref, qr_ref, kr_ref):
        c, sa, sb = c_ref[...], sa_ref[...], sb_ref[...]
        c2 = jnp.concatenate([c, c], axis=1)
        sa2 = jnp.concatenate([sa, sa], axis=1)
        sb2 = jnp.concatenate([sb, sb], axis=1)
        qr_ref[:, 0:512] = (_rot(q0_ref[...], c2, sa2, sb2) * ATT_SCALE).astype(BF16)
        qr_ref[:, 512:1024] = (_rot(q1_ref[...], c2, sa2, sb2) * ATT_SCALE).astype(BF16)
        kr_ref[...] = _rot(k_ref[...], c, sa, sb).astype(BF16)

    tab = pl.BlockSpec((tr, KV_W), lambda i: (i, 0))
    return pl.pallas_call(
        body, name=name, grid=(lp // tr,),
        in_specs=[pl.BlockSpec((tr, 512), lambda i: (i, COL_Q0)),
                  pl.BlockSpec((tr, 512), lambda i: (i, COL_Q0 + 1)),
                  pl.BlockSpec((tr, KV_W), lambda i: (i, COL_K256)),
                  tab, tab, tab],
        out_specs=[pl.BlockSpec((tr, ATT_W), lambda i: (i, 0)),
                   pl.BlockSpec((tr, KV_W), lambda i: (i, 0))],
        out_shape=[jax.ShapeDtypeStruct((lp, ATT_W), BF16),
                   jax.ShapeDtypeStruct((lp, KV_W), BF16)],
        compiler_params=_cp(1),
    )(proj, proj, proj, *tabs)


def _attn_mask(j):
    qi = lax.broadcasted_iota(jnp.int32, (GROUP * TB, 3 * TB), 0) & (TB - 1)
    cc = lax.broadcasted_iota(jnp.int32, (GROUP * TB, 3 * TB), 1)
    jj = cc & (TB - 1)
    is_meta = jj >= PAD0
    p0 = (cc < TB) & is_meta & (j >= 1)
    p1 = (cc >= TB) & (cc < 2 * TB) & (jj > qi) & (j >= 2)
    p2 = (cc >= 2 * TB) & (jj <= qi) & ((j >= 1) | is_meta)
    return p0 | p1 | p2


def _lane_group(rows):
    return lax.broadcasted_iota(jnp.int32, (rows, KV_W), 1) // HEAD_DIM


def _stack_heads(x, kv, lgq):
    parts = []
    for g in range(GROUP):
        sh = ((kv - g) % GROUP) * HEAD_DIM
        moved = x if sh == 0 else pltpu.roll(x, sh, axis=1)
        parts.append(jnp.where(lgq == kv, moved, 0.0))
    return jnp.concatenate(parts, axis=0).astype(BF16)


def _unstack_heads(r, kv):
    out = None
    for g in range(GROUP):
        blk = r[g * TB:(g + 1) * TB, :]
        sh = ((g - kv) % GROUP) * HEAD_DIM
        blk = blk if sh == 0 else pltpu.roll(blk, sh, axis=1)
        out = blk if out is None else out + blk
    return out


def _sink_column(sinks, kv):
    lane = lax.broadcasted_iota(jnp.int32, (1, 128), 1)
    cols = []
    for g in range(GROUP):
        sg = jnp.sum(jnp.where(lane == kv * GROUP + g, sinks, 0.0), axis=1, keepdims=True)
        cols.append(jnp.broadcast_to(sg, (TB, 1)))
    return jnp.concatenate(cols, axis=0)


def _attn_kv(kall, vall, lg, kv):
    km = jnp.where(lg == kv, kall, 0.0).astype(BF16)
    ones = jnp.where(lg == (kv + 1) % N_KV, 1.0, 0.0)
    vm = jnp.where(lg == kv, vall, ones).astype(BF16)
    return km, vm


def _attn_weights(qst, km, vm, sinkcol, valid, lg4, kv):
    s = jnp.where(valid, _dot(qst, km, NT), NEG_INF)
    m = jnp.maximum(jnp.max(s, axis=-1, keepdims=True), sinkcol)
    eb = jnp.exp(s - m).astype(BF16)
    es = jnp.exp(sinkcol - m)
    r = _dot(eb, vm, NN)
    rowsum = pltpu.roll(r, KV_W - HEAD_DIM, axis=1)
    inv = 1.0 / (rowsum + es)
    out = jnp.where(lg4 == kv, r * inv, 0.0)
    return eb, es, inv, out


def _attn_specs(jmap):
    blk = lambda col: pl.BlockSpec((TB, KV_W), lambda n: (jmap(n), col))
    prv = lambda col: pl.BlockSpec((TB, KV_W), lambda n: (jnp.maximum(jmap(n) - 1, 0), col))
    met = lambda col: pl.BlockSpec((TB, KV_W), lambda n: (0, col))
    return dict(
        qr=pl.BlockSpec((TB, ATT_W), lambda n: (jmap(n), 0)),
        k=[met(0), prv(0), blk(0)],
        v=[met(COL_V256), prv(COL_V256), blk(COL_V256)],
        gate=pl.BlockSpec((TB, ATT_W), lambda n: (jmap(n), COL_AGATE1024)),
        sinks=pl.BlockSpec((8, 128), lambda n: (0, 0)),
    )


def _attn_fwd(name, qr, kr, proj, sinks_row, ycat):
    lp = proj.shape[0]
    nb = lp // TB
    sp = _attn_specs(lambda n: n)

    def body(qr_ref, km_ref, kp_ref, kc_ref, vm_ref, vp_ref, vc_ref, gate_ref, sink_ref, yin_ref, o_ref):
        del yin_ref
        j = pl.program_id(0)
        valid = _attn_mask(j)
        kall = jnp.concatenate([km_ref[...], kp_ref[...], kc_ref[...]], axis=0).astype(F32)
        vall = jnp.concatenate([vm_ref[...], vp_ref[...], vc_ref[...]], axis=0)
        lg = _lane_group(3 * TB)
        lgq = _lane_group(TB)
        lg4 = _lane_group(GROUP * TB)
        sinks = sink_ref[0:1, :]
        for kv in range(N_KV):
            cols = slice(kv * KV_W, (kv + 1) * KV_W)
            km, vm = _attn_kv(kall, vall, lg, kv)
            qst = _stack_heads(qr_ref[:, cols].astype(F32), kv, lgq)
            _, _, _, out = _attn_weights(qst, km, vm, _sink_column(sinks, kv), valid, lg4, kv)
            att = _unstack_heads(out, kv)
            gate = gate_ref[:, cols]
            o_ref[:, cols] = (att * (gate * _sig(gate))).astype(BF16)

    return pl.pallas_call(
        body, name=name, grid=(nb,),
        in_specs=[sp["qr"]] + sp["k"] + sp["v"] + [sp["gate"], sp["sinks"],
                                                   pl.BlockSpec(memory_space=pl.ANY)],
        out_specs=pl.BlockSpec((TB, ATT_W), lambda n: (n, 0)),
        out_shape=jax.ShapeDtypeStruct((lp, D_MODEL), BF16),
        input_output_aliases={9: 0},
        compiler_params=_cp(1),
    )(qr, kr, kr, kr, proj, proj, proj, proj, sinks_row, ycat)


def _attn_bwd(name, qr, kr, proj, sinks_row, dycat, dep):
    lp = proj.shape[0]
    nb = lp // TB
    sp = _attn_specs(lambda n: n)

    def body(qr_ref, km_ref, kp_ref, kc_ref, vm_ref, vp_ref, vc_ref, gate_ref, sink_ref, dy_ref, dep_ref,
             dq_ref, dgate_ref, dk_ref, dv_ref, dsink_ref):
        del dep_ref
        j = pl.program_id(0)

        @pl.when(j == 0)
        def _():
            dk_ref[...] = jnp.zeros(dk_ref.shape, F32)
            dv_ref[...] = jnp.zeros(dv_ref.shape, F32)
            dsink_ref[...] = jnp.zeros(dsink_ref.shape, F32)

        valid = _attn_mask(j)
        kall = jnp.concatenate([km_ref[...], kp_ref[...], kc_ref[...]], axis=0).astype(F32)
        vall = jnp.concatenate([vm_ref[...], vp_ref[...], vc_ref[...]], axis=0)
        lg = _lane_group(3 * TB)
        lgq = _lane_group(TB)
        lg4 = _lane_group(GROUP * TB)
        sinks = sink_ref[0:1, :]
        lane = lax.broadcasted_iota(jnp.int32, (1, 128), 1)
        dkall = jnp.zeros((3 * TB, KV_W), F32)
        dvall = jnp.zeros((3 * TB, KV_W), F32)
        dsink = jnp.zeros((1, 128), F32)
        for kv in range(N_KV):
            cols = slice(kv * KV_W, (kv + 1) * KV_W)
            km, vm = _attn_kv(kall, vall, lg, kv)
            qst = _stack_heads(qr_ref[:, cols].astype(F32), kv, lgq)
            gate = gate_ref[:, cols]
            sgate = _sig(gate)
            dy = dy_ref[:, cols].astype(F32)
            dout = dy * (gate * sgate)
            eb, es, inv, out = _attn_weights(qst, km, vm, _sink_column(sinks, kv), valid, lg4, kv)
            att = _unstack_heads(out, kv)
            dgate_ref[:, cols] = (dy * att * _dsilu(gate, sgate)).astype(BF16)
            dsc = dout * _unstack_heads(jnp.where(lg4 == kv, inv, 0.0), kv)
            dost = _stack_heads(dsc, kv, lgq)
            dd = dsc * att
            dcol = jnp.concatenate(
                [jnp.sum(jnp.where(lgq == g, dd, 0.0), axis=1, keepdims=True) for g in range(GROUP)], axis=0)
            dp = _dot(dost, vm, NT)
            ds = (eb.astype(F32) * (dp - dcol)).astype(BF16)
            pd = es * dcol
            for g in range(GROUP):
                tot = jnp.sum(pd[g * TB:(g + 1) * TB, :], axis=0, keepdims=True)
                dsink = dsink - jnp.where(lane == kv * GROUP + g, tot, 0.0)
            dq_ref[:, cols] = _unstack_heads(_dot(ds, km, NN), kv)
            dkall = dkall + _dot(ds, qst, TN)
            dvall = dvall + _dot(eb, dost, TN)
        dsink_ref[0:1, :] += dsink
        prev = pl.multiple_of(jnp.maximum(j - 1, 0) * TB, TB)
        cur = pl.multiple_of(j * TB, TB)
        dk_ref[0:TB, :] += dkall[0:TB]
        dv_ref[0:TB, :] += dvall[0:TB]
        dk_ref[pl.ds(prev, TB), :] += dkall[TB:2 * TB]
        dv_ref[pl.ds(prev, TB), :] += dvall[TB:2 * TB]
        dk_ref[pl.ds(cur, TB), :] += dkall[2 * TB:3 * TB]
        dv_ref[pl.ds(cur, TB), :] += dvall[2 * TB:3 * TB]

    return pl.pallas_call(
        body, name=name, grid=(nb,),
        in_specs=[sp["qr"]] + sp["k"] + sp["v"] + [sp["gate"], sp["sinks"],
                                                   pl.BlockSpec((TB, ATT_W), lambda n: (n, 0)),
                                                   pl.BlockSpec(memory_space=pl.ANY)],
        out_specs=[pl.BlockSpec((TB, ATT_W), lambda n: (n, 0)),
                   pl.BlockSpec((TB, ATT_W), lambda n: (n, 0)),
                   pl.BlockSpec((lp, KV_W), lambda n: (0, 0)),
                   pl.BlockSpec((lp, KV_W), lambda n: (0, 0)),
                   pl.BlockSpec((8, 128), lambda n: (0, 0))],
        out_shape=[jax.ShapeDtypeStruct((lp, ATT_W), F32),
                   jax.ShapeDtypeStruct((lp, ATT_W), BF16),
                   jax.ShapeDtypeStruct((lp, KV_W), F32),
                   jax.ShapeDtypeStruct((lp, KV_W), F32),
                   jax.ShapeDtypeStruct((8, 128), F32)],
        compiler_params=_cp(1),
    )(qr, kr, kr, kr, proj, proj, proj, proj, sinks_row, dycat, dep)


def _attn_assemble(name, dq, dgate, dk, dv, tabs, dproj):
    lp = dq.shape[0]
    tr = _row_tile(lp, 3)

    def body(dq_ref, dg_ref, dk_ref, dv_ref, c_ref, sa_ref, sb_ref, din_ref, o_ref):
        del din_ref
        cidx = pl.program_id(1)
        c, sa, sb = c_ref[...], sa_ref[...], sb_ref[...]

        @pl.when(cidx < 2)
        def _():
            c2 = jnp.concatenate([c, c], axis=1)
            sa2 = jnp.concatenate([sa, sa], axis=1)
            sb2 = jnp.concatenate([sb, sb], axis=1)
            o_ref[...] = (_rot_t(dq_ref[...], c2, sa2, sb2) * ATT_SCALE).astype(BF16)

        @pl.when(cidx == 2)
        def _():
            o_ref[:, 0:KV_W] = _rot_t(dk_ref[...], c, sa, sb).astype(BF16)
            o_ref[:, KV_W:2 * KV_W] = dv_ref[...].astype(BF16)

        @pl.when(cidx > 2)
        def _():
            o_ref[...] = dg_ref[...]

    tab = pl.BlockSpec((tr, KV_W), lambda n, c: (n, 0))
    return pl.pallas_call(
        body, name=name, grid=(lp // tr, 5),
        in_specs=[pl.BlockSpec((tr, 512), lambda n, c: (n, jnp.minimum(c, 1))),
                  pl.BlockSpec((tr, 512), lambda n, c: (n, jnp.clip(c - 3, 0, 1))),
                  tab, tab,
                  tab, tab, tab,
                  pl.BlockSpec(memory_space=pl.ANY)],
        out_specs=pl.BlockSpec((tr, 512), lambda n, c: (n, COL_Q0 + c)),
        out_shape=jax.ShapeDtypeStruct((lp, IN_TOTAL), BF16),
        input_output_aliases={7: 0},
        compiler_params=_cp(2),
    )(dq, dgate, dk, dv, *tabs, dproj)


def _softplus_neg(lam):
    t = jnp.exp(-jnp.abs(lam))
    u = 1.0 + t
    den = jnp.where(u == 1.0, 1.0, u - 1.0)
    l1p = jnp.where(u == 1.0, t, jnp.log(u) * (t / den))
    return jnp.maximum(-lam, 0.0) + l1p


def _lru_chain(j, rx_ref, rxp_ref, wl_ref, vec_ref, wa_ref, wx_ref):
    rx = rx_ref[...]
    rxp = jnp.where(j > 0, rxp_ref[...], 0.0)
    cat = jnp.concatenate([rxp, rx], axis=0)
    views = [cat[8:8 + TB, :]] + [pltpu.roll(cat, s, axis=0)[8:8 + TB, :] for s in range(1, LRU_CONV_K)]
    x1 = jnp.broadcast_to(vec_ref[0:1, :], (TB, LRU_W))
    for k in range(LRU_CONV_K):
        x1 = x1 + wl_ref[k:k + 1, :] * views[LRU_CONV_K - 1 - k]
    x1b = x1.astype(BF16)
    r = _sig(_dot(x1b, wa_ref[...], NN) + vec_ref[1:2, :])
    ig = _sig(_dot(x1b, wx_ref[...], NN) + vec_ref[2:3, :])
    sp = _softplus_neg(vec_ref[3:4, :])
    log_a = -LRU_C * r * sp
    rows = _row_ids((TB, LRU_W), j * TB)
    live = rows >= PAD0
    a = jnp.where(live, jnp.exp(log_a), 0.0)
    y2 = 2.0 * log_a
    em = -jnp.tanh(0.5 * y2) * (jnp.exp(y2) + 1.0)
    mult = jnp.sqrt(em)
    return dict(views=views, x1=x1, x1b=x1b, r=r, ig=ig, sp=sp, a=a, mult=mult, live=live, a_raw=jnp.exp(log_a))


def _lru_specs(jmap):
    return [pl.BlockSpec((TB, 512), lambda n: (jmap(n), COL_RX)),
            pl.BlockSpec((8, 512), lambda n: (jnp.maximum(jmap(n) * (TB // 8) - 1, 0), COL_RX)),
            pl.BlockSpec((TB, 512), lambda n: (jmap(n), COL_RGATE))]


def _lru_param_specs():
    return [pl.BlockSpec((8, LRU_W), lambda n: (0, 0)),
            pl.BlockSpec((8, LRU_W), lambda n: (0, 0)),
            pl.BlockSpec((LRU_W, LRU_W), lambda n: (0, 0)),
            pl.BlockSpec((LRU_W, LRU_W), lambda n: (0, 0))]


def _lru_fwd(name, proj, wl, vec, wa, wx, ycat):
    lp = proj.shape[0]
    nb = lp // TB

    def body(rx_ref, rxp_ref, gate_ref, wl_ref, vec_ref, wa_ref, wx_ref, yin_ref, o_ref, h_ref, carry_ref):
        del yin_ref
        j = pl.program_id(0)

        @pl.when(j == 0)
        def _():
            carry_ref[...] = jnp.zeros(carry_ref.shape, F32)

        c = _lru_chain(j, rx_ref, rxp_ref, wl_ref, vec_ref, wa_ref, wx_ref)
        a = c["a"]
        u = jnp.where(c["live"], c["mult"] * (c["ig"] * c["x1"]), 0.0)
        rows = lax.broadcasted_iota(jnp.int32, (TB, LRU_W), 0)
        d = 1
        while d < TB:
            ap = jnp.where(rows >= d, pltpu.roll(a, d, axis=0), 1.0)
            up = jnp.where(rows >= d, pltpu.roll(u, d, axis=0), 0.0)
            u = a * up + u
            a = a * ap
            d *= 2
        h = u + a * carry_ref[0:1, :]
        carry_ref[...] = h[TB - 8:TB, :]
        carry_ref[0:1, :] = h[TB - 1:TB, :]
        h_ref[...] = h
        gate = gate_ref[...]
        o_ref[...] = (h * (gate * _sig(gate))).astype(BF16)

    return pl.pallas_call(
        body, name=name, grid=(nb,),
        in_specs=_lru_specs(lambda n: n) + _lru_param_specs() + [pl.BlockSpec(memory_space=pl.ANY)],
        out_specs=[pl.BlockSpec((TB, 512), lambda n: (n, YC_LRU)),
                   pl.BlockSpec((TB, LRU_W), lambda n: (n, 0))],
        out_shape=[jax.ShapeDtypeStruct((lp, D_MODEL), BF16),
                   jax.ShapeDtypeStruct((lp, LRU_W), F32)],
        input_output_aliases={7: 0},
        scratch_shapes=[pltpu.VMEM((8, LRU_W), F32)],
        compiler_params=_cp(1),
    )(proj, proj, proj, wl, vec, wa, wx, ycat)


def _lru_bwd(name, proj, dycat, hstate, wl, vec, wa, wx, dproj):
    lp = proj.shape[0]
    nb = lp // TB

    def body(rx_ref, rxp_ref, gate_ref, dy_ref, h_ref, hp_ref, wl_ref, vec_ref, wa_ref, wx_ref, din_ref,
             dp_ref, dwl_ref, dvec_ref, dwa_ref, dwx_ref, dhc_ref, anx_ref, dxc_ref):
        del din_ref
        n = pl.program_id(0)
        j = nb - 1 - n

        @pl.when(n == 0)
        def _():
            dhc_ref[...] = jnp.zeros(dhc_ref.shape, F32)
            anx_ref[...] = jnp.zeros(anx_ref.shape, F32)
            dxc_ref[...] = jnp.zeros(dxc_ref.shape, F32)
            dwl_ref[...] = jnp.zeros(dwl_ref.shape, F32)
            dvec_ref[...] = jnp.zeros(dvec_ref.shape, F32)
            dwa_ref[...] = jnp.zeros(dwa_ref.shape, F32)
            dwx_ref[...] = jnp.zeros(dwx_ref.shape, F32)

        c = _lru_chain(j, rx_ref, rxp_ref, wl_ref, vec_ref, wa_ref, wx_ref)
        a, mult, r, ig, x1, live = c["a"], c["mult"], c["r"], c["ig"], c["x1"], c["live"]
        h = h_ref[...]
        gate = gate_ref[...]
        sgate = _sig(gate)
        dy = dy_ref[...].astype(F32)
        gsum = dy * (gate * sgate)
        dgate = dy * h * _dsilu(gate, sgate)
        rows = lax.broadcasted_iota(jnp.int32, (TB, LRU_W), 0)
        bb = jnp.where(rows == TB - 1, anx_ref[0:1, :], pltpu.roll(a, TB - 1, axis=0))
        gg = gsum
        d = 1
        while d < TB:
            keep = rows < TB - d
            bn = jnp.where(keep, pltpu.roll(bb, TB - d, axis=0), 1.0)
            gn = jnp.where(keep, pltpu.roll(gg, TB - d, axis=0), 0.0)
            gg = gg + bb * gn
            bb = bb * bn
            d *= 2
        dh = gg + bb * dhc_ref[0:1, :]
        dhc_ref[...] = dh[0:8, :]
        anx_ref[...] = a[0:8, :]
        hprev = jnp.where(rows == 0, jnp.where(j > 0, hp_ref[7:8, :], 0.0), pltpu.roll(h, 1, axis=0))
        du = jnp.where(live, dh, 0.0)
        da = jnp.where(live, dh * hprev, 0.0)
        ar = c["a_raw"]
        dmult = du * (ig * x1)
        di = du * mult * x1
        dx1 = du * mult * ig
        dloga = da * ar - dmult * ar * ar / mult
        dr = dloga * (-LRU_C * c["sp"])
        dvec_ref[3:4, :] += _colsum(dloga * (-LRU_C * r))
        dza = dr * r * (1.0 - r)
        dzx = di * ig * (1.0 - ig)
        dzab, dzxb = dza.astype(BF16), dzx.astype(BF16)
        dvec_ref[1:2, :] += _colsum(dza)
        dvec_ref[2:3, :] += _colsum(dzx)
        dwa_ref[...] += _dot(c["x1b"], dzab, TN)
        dwx_ref[...] += _dot(c["x1b"], dzxb, TN)
        dx1 = dx1 + _dot(dzab, wa_ref[...], NT) + _dot(dzxb, wx_ref[...], NT)
        dvec_ref[0:1, :] += _colsum(dx1)
        for k in range(LRU_CONV_K):
            dwl_ref[k:k + 1, :] += _colsum(dx1 * c["views"][LRU_CONV_K - 1 - k])
        dcat = jnp.concatenate([dx1, dxc_ref[...]], axis=0)
        drx = jnp.zeros((TB, LRU_W), F32)
        for k in range(LRU_CONV_K):
            s = LRU_CONV_K - 1 - k
            view = dcat[0:TB, :] if s == 0 else pltpu.roll(dcat, TB + 8 - s, axis=0)[0:TB, :]
            drx = drx + wl_ref[k:k + 1, :] * view
        dxc_ref[...] = dx1[0:8, :]
        dp_ref[:, 0:512] = drx.astype(BF16)
        dp_ref[:, 512:1024] = dgate.astype(BF16)

        @pl.when(n == nb - 1)
        def _():
            lam = vec_ref[3:4, :]
            dvec_ref[3:4, :] = dvec_ref[3:4, :] * (-_sig(-lam))

    jmap = lambda n: nb - 1 - n
    return pl.pallas_call(
        body, name=name, grid=(nb,),
        in_specs=(_lru_specs(jmap)
                  + [pl.BlockSpec((TB, 512), lambda n: (jmap(n), YC_LRU)),
                     pl.BlockSpec((TB, LRU_W), lambda n: (jmap(n), 0)),
                     pl.BlockSpec((8, LRU_W), lambda n: (jnp.maximum(jmap(n) * (TB // 8) - 1, 0), 0))]
                  + _lru_param_specs() + [pl.BlockSpec(memory_space=pl.ANY)]),
        out_specs=[pl.BlockSpec((TB, 1024), lambda n: (jmap(n), 4)),
                   pl.BlockSpec((8, LRU_W), lambda n: (0, 0)),
                   pl.BlockSpec((8, LRU_W), lambda n: (0, 0)),
                   pl.BlockSpec((LRU_W, LRU_W), lambda n: (0, 0)),
                   pl.BlockSpec((LRU_W, LRU_W), lambda n: (0, 0))],
        out_shape=[jax.ShapeDtypeStruct((lp, IN_TOTAL), BF16),
                   jax.ShapeDtypeStruct((8, LRU_W), F32),
                   jax.ShapeDtypeStruct((8, LRU_W), F32),
                   jax.ShapeDtypeStruct((LRU_W, LRU_W), F32),
                   jax.ShapeDtypeStruct((LRU_W, LRU_W), F32)],
        input_output_aliases={10: 0},
        scratch_shapes=[pltpu.VMEM((8, LRU_W), F32), pltpu.VMEM((8, LRU_W), F32), pltpu.VMEM((8, LRU_W), F32)],
        compiler_params=_cp(1),
    )(proj, proj, proj, dycat, hstate, hstate, wl, vec, wa, wx, dproj)


def _exchange(name, items):
    flat_srcs, out_shapes, plan = [], [], []
    for it, (kind, srcs) in enumerate(items):
        shape = srcs[0].shape if kind == "gather" else srcs[0].shape[1:]
        out_shapes.append(jax.ShapeDtypeStruct((N_DEV, len(srcs)) + tuple(shape), srcs[0].dtype))
        for l, s in enumerate(srcs):
            plan.append((it, l, kind, len(flat_srcs)))
            flat_srcs.append(s)
    n_in, n_out, n_cp = len(flat_srcs), len(out_shapes), len(plan)

    def body(*refs):
        ins, outs = refs[:n_in], refs[n_in:n_in + n_out]
        send_sems, recv_sems, loc_sems = refs[n_in + n_out:]
        x, y, c = lax.axis_index("x"), lax.axis_index("y"), lax.axis_index("c")
        me = 4 * x + 2 * y + c
        remote, local = [], []
        for q, (it, l, kind, si) in enumerate(plan):
            src, out = ins[si], outs[it]
            for k in range(1, N_DEV):
                px = 1 - x if k & 4 else x
                py = 1 - y if k & 2 else y
                pc = 1 - c if k & 1 else c
                peer = 4 * px + 2 * py + pc
                cp = pltpu.make_async_remote_copy(
                    src_ref=src if kind == "gather" else src.at[peer],
                    dst_ref=out.at[me, l],
                    send_sem=send_sems.at[q * 7 + k - 1], recv_sem=recv_sems.at[q * 7 + k - 1],
                    device_id=(px, py, pc), device_id_type=pl.DeviceIdType.MESH)
                cp.start()
                remote.append(cp)
            lc = pltpu.make_async_copy(src if kind == "gather" else src.at[me], out.at[me, l], loc_sems.at[q])
            lc.start()
            local.append(lc)
        for lc in local:
            lc.wait()
        for cp in remote:
            cp.wait_send()
            cp.wait_recv()

    anyspec = pl.BlockSpec(memory_space=pl.ANY)
    outs = pl.pallas_call(
        body, name=name,
        in_specs=[anyspec] * n_in, out_specs=[anyspec] * n_out, out_shape=out_shapes,
        scratch_shapes=[pltpu.SemaphoreType.DMA((n_cp * 7,)), pltpu.SemaphoreType.DMA((n_cp * 7,)),
                        pltpu.SemaphoreType.DMA((n_cp,))],
    )(*flat_srcs)
    return list(outs)


_HBM = pl.BlockSpec(memory_space=pltpu.HBM)
_SEM = pl.BlockSpec(memory_space=pltpu.SEMAPHORE)
_ANY = pl.BlockSpec(memory_space=pl.ANY)
_EFFECT = pltpu.SideEffectType.DATAFLOW_SIDE_EFFECTING


def _hbm(a):
    return pltpu.with_memory_space_constraint(a, pltpu.HBM)


def _split_descriptors(copies, srcs, lands, send_sems, recv_sems):
    x, y, c = lax.axis_index("x"), lax.axis_index("y"), lax.axis_index("c")
    me = 4 * x + 2 * y + c
    out = []
    for q, (si, scatter, li, ll) in enumerate(copies):
        for k in range(1, N_DEV):
            px = 1 - x if k & 4 else x
            py = 1 - y if k & 2 else y
            pc = 1 - c if k & 1 else c
            peer = 4 * px + 2 * py + pc
            dst = lands[li].at[me] if ll is None else lands[li].at[me, ll]
            out.append(pltpu.make_async_remote_copy(
                src_ref=srcs[si].at[peer] if scatter else srcs[si], dst_ref=dst,
                send_sem=send_sems.at[q * 7 + k - 1], recv_sem=recv_sems.at[q * 7 + k - 1],
                device_id=(px, py, pc), device_id_type=pl.DeviceIdType.MESH))
    return out


def _xchg_start(name, groups):
    n_src = [len(g[0]) for g in groups]
    n_land = [len(g[1]) for g in groups]
    srcs = [s for g in groups for s in g[0]]
    lands = [l for g in groups for l in g[1]]
    ns, nl, ng = len(srcs), len(lands), len(groups)

    def body(*refs):
        src_refs, land_refs = refs[:ns], refs[ns:ns + nl]
        sems = refs[ns + nl:ns + nl + 2 * ng]
        token = refs[-1]
        so = lo = 0
        for gi, (_, _, copies) in enumerate(groups):
            for d in _split_descriptors(copies, src_refs[so:so + n_src[gi]], land_refs[lo:lo + n_land[gi]],
                                        sems[2 * gi], sems[2 * gi + 1]):
                d.start()
            so += n_src[gi]
            lo += n_land[gi]
        token[...] = jnp.zeros(token.shape, F32)

    out_shape, out_specs = [], []
    for g in groups:
        n = 7 * len(g[2])
        out_shape += [pltpu.SemaphoreType.DMA((n,)), pltpu.SemaphoreType.DMA((n,))]
        out_specs += [_SEM, _SEM]
    out_shape += [pltpu.HBM(l.shape, l.dtype) for l in lands]
    out_specs += [_HBM] * nl
    out_shape.append(jax.ShapeDtypeStruct((8, 128), F32))
    out_specs.append(pl.BlockSpec(memory_space=pltpu.VMEM))
    outs = pl.pallas_call(
        body, name=name, in_specs=[_HBM] * (ns + nl), out_specs=out_specs, out_shape=out_shape,
        input_output_aliases={ns + i: 2 * ng + i for i in range(nl)},
        compiler_params=pltpu.CompilerParams(has_side_effects=_EFFECT),
    )(*[_hbm(a) for a in srcs + lands])
    res, lo = [], 2 * ng
    for gi in range(ng):
        res.append((outs[2 * gi], outs[2 * gi + 1], list(outs[lo:lo + n_land[gi]])))
        lo += n_land[gi]
    return res, outs[-1]


def _xchg_wait(name, group, started, after):
    srcs, _, copies = group
    send_sems, recv_sems, lands = started
    ns, nl = len(srcs), len(lands)

    def body(*refs):
        src_refs, land_refs = refs[:ns], refs[ns:ns + nl]
        send_ref, recv_ref = refs[ns + nl], refs[ns + nl + 1]
        for d in _split_descriptors(copies, src_refs, land_refs, send_ref, recv_ref):
            d.wait_send()
            d.wait_recv()

    outs = pl.pallas_call(
        body, name=name, in_specs=[_HBM] * (ns + nl) + [_SEM, _SEM, _ANY],
        out_specs=[_HBM] * nl, out_shape=[pltpu.HBM(l.shape, l.dtype) for l in lands],
        input_output_aliases={ns + i: i for i in range(nl)},
        compiler_params=pltpu.CompilerParams(has_side_effects=_EFFECT),
    )(*[_hbm(a) for a in srcs], *lands, send_sems, recv_sems, after)
    return list(outs)


def _landing(own, me):
    land = lax.empty((N_DEV,) + own.shape, own.dtype)
    return lax.dynamic_update_slice(land, own[None], (me,) + (0,) * own.ndim)


def _adamw(name, w, m, v, recv, row0=0, prev=None):
    cdim = w.shape[1]
    r = recv.shape[1]
    tr = r
    for cand in (512, 256, 128, 64, 32, 16, 8):
        if r % cand == 0 and r > cand:
            tr = cand
            break
    assert row0 % tr == 0
    blk0 = row0 // tr
    n_prev = 0 if prev is None else 4

    def body(w_ref, m_ref, v_ref, r_ref, *rest):
        g_ref, d_ref, mo_ref, vo_ref = rest[n_prev:]
        g = r_ref[0].astype(F32)
        for s in range(1, N_DEV):
            g = g + r_ref[s].astype(F32)
        mn = ADAM_B1 * m_ref[...] + (1.0 - ADAM_B1) * g
        vn = ADAM_B2 * v_ref[...] + (1.0 - ADAM_B2) * (g * g)
        m_hat = mn / (1.0 - ADAM_B1 ** ADAM_STEP)
        v_hat = vn / (1.0 - ADAM_B2 ** ADAM_STEP)
        g_ref[...] = g
        d_ref[...] = -ADAM_LR * (m_hat / (jnp.sqrt(v_hat) + ADAM_EPS) + ADAM_WD * w_ref[...])
        mo_ref[...] = mn
        vo_ref[...] = vn

    blk = pl.BlockSpec((tr, cdim), lambda i: (i + blk0, 0))
    return pl.pallas_call(
        body, name=name, grid=(r // tr,),
        in_specs=[blk, blk, blk, pl.BlockSpec((N_DEV, tr, cdim), lambda i: (0, i, 0))] + [_ANY] * n_prev,
        out_specs=[blk, blk, blk, blk],
        out_shape=[jax.ShapeDtypeStruct(w.shape, F32)] * 4,
        input_output_aliases={4 + i: i for i in range(n_prev)},
        compiler_params=_cp(1),
    )(w, m, v, recv, *(prev or []))


def _pack_rows(arrs, lead=()):
    n = len(lead)
    flat = jnp.concatenate([a.reshape(a.shape[:n] + (-1,)) for a in arrs], axis=-1)
    size = flat.shape[-1]
    padded = -(-size // PACK_QUANTUM) * PACK_QUANTUM
    flat = jnp.pad(flat, [(0, 0)] * n + [(0, padded - size)])
    return flat.reshape(flat.shape[:n] + (padded // 128, 128))


def _unpack_rows(packed, shapes, lead=()):
    n = len(lead)
    flat = packed.reshape(packed.shape[:n] + (-1,))
    out, off = [], 0
    for s in shapes:
        size = int(np.prod(s))
        out.append(flat[..., off:off + size].reshape(packed.shape[:n] + tuple(s)))
        off += size
    return out


def _block_diag(w):
    eye = jnp.eye(LRU_HEADS, dtype=w.dtype)
    return (eye[:, None, :, None] * w[:, :, None, :]).reshape(LRU_W, LRU_W)


def _diag_blocks(dense):
    t = dense.reshape(LRU_HEADS, 64, LRU_HEADS, 64)
    return jnp.stack([t[h, :, h, :] for h in range(LRU_HEADS)], axis=0)


def _cols_to_slots(full):
    lead = full.shape[:-1]
    t = full.reshape(lead + (N_DEV, full.shape[-1] // N_DEV))
    return jnp.moveaxis(t, -2, 0)


def _slots_to_cols(slots):
    t = jnp.moveaxis(slots, 0, -2)
    return t.reshape(t.shape[:-2] + (t.shape[-2] * t.shape[-1],))


def kernel(x, meta_tokens, ln_in_g, ln_in_b, w_in, conv_dw_w, conv_dw_b, conv_ln_g, conv_ln_b, conv_pw_w, conv_pw_b, attn_sinks, lru_conv_w, lru_conv_b, lru_wa, lru_ba, lru_wx, lru_bx, lru_lambda, w_out, ln_post_g, ln_post_b, loss_target, m_meta_tokens, m_ln_in_g, m_ln_in_b, m_w_in, m_conv_dw_w, m_conv_dw_b, m_conv_ln_g, m_conv_ln_b, m_conv_pw_w, m_conv_pw_b, m_attn_sinks, m_lru_conv_w, m_lru_conv_b, m_lru_wa, m_lru_ba, m_lru_wx, m_lru_bx, m_lru_lambda, m_w_out, m_ln_post_g, m_ln_post_b, v_meta_tokens, v_ln_in_g, v_ln_in_b, v_w_in, v_conv_dw_w, v_conv_dw_b, v_conv_ln_g, v_conv_ln_b, v_conv_pw_w, v_conv_pw_b, v_attn_sinks, v_lru_conv_w, v_lru_conv_b, v_lru_wa, v_lru_ba, v_lru_wx, v_lru_bx, v_lru_lambda, v_w_out, v_ln_post_g, v_ln_post_b):
    seq = x.shape[1]
    lp = seq + TB
    row = lambda a: a.reshape(1, -1)

    small_shard_shapes = [conv_dw_w.shape, lru_conv_w.shape, meta_tokens.shape]
    small_shard = _pack_rows([conv_dw_w, lru_conv_w, meta_tokens])
    me = 4 * lax.axis_index("x") + 2 * lax.axis_index("y") + lax.axis_index("c")
    w_in_b = [w_in[l].astype(BF16) for l in range(DEPTH)]
    w_out_b = [w_out[l].astype(BF16) for l in range(DEPTH)]
    pw_b = conv_pw_w.astype(BF16)
    wgroups = [
        ([small_shard], [_landing(small_shard, me)], [(0, False, 0, None)]),
        ([w_in_b[0]], [_landing(w_in_b[0], me)], [(0, False, 0, None)]),
        ([pw_b, w_out_b[0]], [_landing(pw_b, me), _landing(w_out_b[0], me)],
         [(0, False, 0, None), (1, False, 1, None)]),
        ([w_in_b[1], w_out_b[1]], [_landing(w_in_b[1], me), _landing(w_out_b[1], me)],
         [(0, False, 0, None), (1, False, 1, None)]),
    ]
    wstarted, wtoken = _xchg_start("weights_start", wgroups)
    wg_small, = _xchg_wait("weights_wait_s", wgroups[0], wstarted[0], wtoken)
    g_dw, g_lc, g_meta = _unpack_rows(wg_small, small_shard_shapes, lead=(N_DEV,))
    conv_dw_full = _slots_to_cols(g_dw)
    lru_conv_full = _slots_to_cols(g_lc)
    meta_full = _slots_to_cols(g_meta)
    wg_in = [None, None]
    wg_out = [None, None]
    wg_pw = None

    tabs = _rope_tables(lp)
    ln_g = [ln_in_g, ln_post_g[0], ln_post_g[1]]
    ln_b = [ln_in_b, ln_post_b[0], ln_post_b[1]]

    def layer_params(l):
        wdw = jnp.pad(conv_dw_full[l], ((0, 1), (0, 0)))
        cvec = jnp.pad(jnp.stack([conv_dw_b[l], conv_ln_g[l], conv_ln_b[l], conv_pw_b[l]]), ((0, 4), (0, 0)))
        wpw = wg_pw[:, l].reshape(CONV_W, CONV_W)
        sinks = jnp.pad(attn_sinks[l].reshape(1, N_HEADS), ((0, 7), (0, 128 - N_HEADS)))
        wl = jnp.pad(lru_conv_full[l], ((0, 4), (0, 0)))
        lvec = jnp.pad(jnp.stack([lru_conv_b[l], lru_ba[l], lru_bx[l], lru_lambda[l]]), ((0, 4), (0, 0)))
        wa = _block_diag(lru_wa[l]).astype(BF16)
        wx = _block_diag(lru_wx[l]).astype(BF16)
        wo = wg_out[l].reshape(D_MODEL, D_MODEL)
        wout = jnp.concatenate([wo[512:1536], wo[0:512], wo[1536:]], axis=0)
        return dict(wdw=wdw, cvec=cvec, wpw=wpw, sinks=sinks, wl=wl, lvec=lvec, wa=wa, wx=wx, wout=wout)

    params = [None] * DEPTH

    z0, hb = _embed(x, meta_full, row(ln_g[0]), row(ln_b[0]))
    z = [z0]
    saved = []
    for l in range(DEPTH):
        if l == 0:
            wg_in[0], = _xchg_wait("weights_wait_a", wgroups[1], wstarted[1], hb)
        else:
            wg_in[1], wg_out[1] = _xchg_wait("weights_wait_c", wgroups[3], wstarted[3], hb)
        proj = _mm_proj(f"proj{l}", hb, wg_in[l])
        if l == 0:
            wg_pw, wg_out[0] = _xchg_wait("weights_wait_b", wgroups[2], wstarted[2], proj)
        p = params[l] = layer_params(l)
        ycat = _conv_fwd(f"conv_fwd{l}", proj, p["wdw"], p["cvec"], p["wpw"])
        qr, kr = _rope_fwd(f"rope{l}", proj, tabs)
        ycat = _attn_fwd(f"attn_fwd{l}", qr, kr, proj, p["sinks"], ycat)
        ycat, hstate = _lru_fwd(f"lru_fwd{l}", proj, p["wl"], p["lvec"], p["wa"], p["wx"], ycat)
        saved.append(dict(hb=hb, proj=proj, ycat=ycat, qr=qr, kr=kr, hstate=hstate))
        z_next, hb = _mm_out(f"out{l}", ycat, p["wout"], z[l], row(ln_g[l]), row(ln_b[l]),
                             row(ln_g[l + 1]), row(ln_b[l + 1]))
        z.append(z_next)

    dz, st_post1, loss_blk = _loss_head(z[DEPTH], loss_target, row(ln_g[DEPTH]), row(ln_b[DEPTH]))
    loss = lax.psum(loss_blk[0, 0], ("x", "y", "c"))

    ln_stats = {DEPTH: st_post1}
    g_layers = [None] * DEPTH
    dwin_l, dwout_l = [None] * DEPTH, [None] * DEPTH
    grad_x = gmeta = None
    token = wtoken
    ggroups = [None] * DEPTH
    own = lambda a: lax.dynamic_index_in_dim(a, me, 0, keepdims=False)
    for l in reversed(range(DEPTH)):
        p, s = params[l], saved[l]
        dycat = _mm_dycat(f"dycat{l}", dz, p["wout"], token)
        dwout_l[l] = _mm_dwout(f"dwout{l}", s["ycat"], dz)
        dproj, dwdw, dcvec, dwpw = _conv_bwd(f"conv_bwd{l}", s["proj"], dycat, p["wdw"], p["cvec"], p["wpw"])
        dwo = jnp.concatenate([dwout_l[l][1024:1536], dwout_l[l][0:1024], dwout_l[l][1536:]], axis=0)
        dwo = dwo.reshape(N_DEV, D_MODEL // N_DEV, D_MODEL)
        dpw = dwpw.reshape(N_DEV, CONV_W // N_DEV, CONV_W)
        early = ([dwo, dpw], [_landing(own(dwo), me), _landing(own(dpw), me)],
                 [(0, True, 0, None), (1, True, 1, None)])
        started_early, token = _xchg_start(f"grads_start_out{l}", [early])
        dq, dgate, dk, dv, dsink = _attn_bwd(f"attn_bwd{l}", s["qr"], s["kr"], s["proj"], p["sinks"], dycat, token)
        dproj = _attn_assemble(f"attn_asm{l}", dq, dgate, dk, dv, tabs, dproj)
        dproj, dwl, dlvec, dwa, dwx = _lru_bwd(f"lru_bwd{l}", s["proj"], dycat, s["hstate"],
                                                p["wl"], p["lvec"], p["wa"], p["wx"], dproj)
        dwin_l[l] = _mm_dwin(f"dwin{l}", s["hb"], dproj)
        late = ([dwin_l[l]], [_landing(own(dwin_l[l]), me)], [(0, True, 0, None)])
        started_late, token = _xchg_start(f"grads_start_in{l}", [late])
        ggroups[l] = [(late, started_late[0]), (early, started_early[0])]
        dh = _mm_dh(f"dh{l}", dproj, wg_in[l], dz, token)
        if l > 0:
            dz, ln_stats[l] = _ln_bwd(f"ln_bwd{l}", dh, z[l], row(ln_g[l]))
        else:
            grad_x, gmeta, ln_stats[0] = _ln_bwd_input(dh, z[0], row(ln_g[0]))
        g_layers[l] = dict(dwdw=dwdw[:CONV_K], dcvec=dcvec, dwpw=dwpw, dsink=dsink[0, :N_HEADS],
                           dwl=dwl[:LRU_CONV_K], dlvec=dlvec, dwa=_diag_blocks(dwa), dwx=_diag_blocks(dwx))

    stack = lambda f: jnp.stack([f(g_layers[l]) for l in range(DEPTH)])
    g_local = dict(
        ln_in_g=ln_stats[0][0], ln_in_b=ln_stats[0][1],
        conv_dw_b=stack(lambda g: g["dcvec"][0]), conv_ln_g=stack(lambda g: g["dcvec"][1]),
        conv_ln_b=stack(lambda g: g["dcvec"][2]), conv_pw_b=stack(lambda g: g["dcvec"][3]),
        attn_sinks=stack(lambda g: g["dsink"]),
        lru_conv_b=stack(lambda g: g["dlvec"][0]), lru_wa=stack(lambda g: g["dwa"]),
        lru_ba=stack(lambda g: g["dlvec"][1]), lru_wx=stack(lambda g: g["dwx"]),
        lru_bx=stack(lambda g: g["dlvec"][2]), lru_lambda=stack(lambda g: g["dlvec"][3]),
        ln_post_g=jnp.stack([ln_stats[1][0], ln_stats[2][0]]),
        ln_post_b=jnp.stack([ln_stats[1][1], ln_stats[2][1]]),
    )
    rep_names = ["ln_in_g", "ln_in_b", "conv_dw_b", "conv_ln_g", "conv_ln_b", "conv_pw_b", "attn_sinks",
                 "lru_conv_b", "lru_wa", "lru_ba", "lru_wx", "lru_bx", "lru_lambda", "ln_post_g", "ln_post_b"]
    weights = dict(meta_tokens=meta_tokens, ln_in_g=ln_in_g, ln_in_b=ln_in_b, w_in=w_in, conv_dw_w=conv_dw_w,
                   conv_dw_b=conv_dw_b, conv_ln_g=conv_ln_g, conv_ln_b=conv_ln_b, conv_pw_w=conv_pw_w,
                   conv_pw_b=conv_pw_b, attn_sinks=attn_sinks, lru_conv_w=lru_conv_w, lru_conv_b=lru_conv_b,
                   lru_wa=lru_wa, lru_ba=lru_ba, lru_wx=lru_wx, lru_bx=lru_bx, lru_lambda=lru_lambda,
                   w_out=w_out, ln_post_g=ln_post_g, ln_post_b=ln_post_b)
    mom1 = dict(meta_tokens=m_meta_tokens, ln_in_g=m_ln_in_g, ln_in_b=m_ln_in_b, w_in=m_w_in, conv_dw_w=m_conv_dw_w,
                conv_dw_b=m_conv_dw_b, conv_ln_g=m_conv_ln_g, conv_ln_b=m_conv_ln_b, conv_pw_w=m_conv_pw_w,
                conv_pw_b=m_conv_pw_b, attn_sinks=m_attn_sinks, lru_conv_w=m_lru_conv_w, lru_conv_b=m_lru_conv_b,
                lru_wa=m_lru_wa, lru_ba=m_lru_ba, lru_wx=m_lru_wx, lru_bx=m_lru_bx, lru_lambda=m_lru_lambda,
                w_out=m_w_out, ln_post_g=m_ln_post_g, ln_post_b=m_ln_post_b)
    mom2 = dict(meta_tokens=v_meta_tokens, ln_in_g=v_ln_in_g, ln_in_b=v_ln_in_b, w_in=v_w_in, conv_dw_w=v_conv_dw_w,
                conv_dw_b=v_conv_dw_b, conv_ln_g=v_conv_ln_g, conv_ln_b=v_conv_ln_b, conv_pw_w=v_conv_pw_w,
                conv_pw_b=v_conv_pw_b, attn_sinks=v_attn_sinks, lru_conv_w=v_lru_conv_w, lru_conv_b=v_lru_conv_b,
                lru_wa=v_lru_wa, lru_ba=v_lru_ba, lru_wx=v_lru_wx, lru_bx=v_lru_bx, lru_lambda=v_lru_lambda,
                w_out=v_w_out, ln_post_g=v_ln_post_g, ln_post_b=v_ln_post_b)

    rep_pack = _pack_rows([g_local[n] for n in rep_names])
    shard_small_names = ["conv_dw_w", "lru_conv_w", "meta_tokens"]
    g_dw_full = jnp.stack([g_layers[l]["dwdw"] for l in range(DEPTH)])
    g_lc_full = jnp.stack([g_layers[l]["dwl"] for l in range(DEPTH)])
    shard_pack = _pack_rows([_cols_to_slots(g_dw_full), _cols_to_slots(g_lc_full), _cols_to_slots(gmeta)],
                            lead=(N_DEV,))
    r_small, r_rep = _exchange("exchange_small_grads", [
        ("scatter", [shard_pack]),
        ("gather", [rep_pack]),
    ])

    res = {}

    def flat2(a, cols):
        return a.reshape(-1, cols)

    big = (("w_in", 0, W_IN_SHARD), ("w_out", 1, D_MODEL), ("conv_pw_w", 2, CONV_W))
    prev = {n: None for n, _, _ in big}
    after = r_rep
    for l in reversed(range(DEPTH)):
        recvs = []
        for gi, (grp, started) in enumerate(ggroups[l]):
            recvs += _xchg_wait(f"grads_wait{l}_{gi}", grp, started, after)
        for name_, gi, cols in big:
            w_ = weights[name_]
            rows = w_.shape[1]
            prev[name_] = _adamw(f"adamw_{name_}{l}", flat2(w_, cols), flat2(mom1[name_], cols),
                                 flat2(mom2[name_], cols), recvs[gi], row0=l * rows, prev=prev[name_])
        after = prev["w_in"][0]
    for name_, _, _ in big:
        res[name_] = [o.reshape(weights[name_].shape) for o in prev[name_]]

    sshapes = [weights[n].shape for n in shard_small_names]
    outs = _adamw("adamw_small_sharded",
                  _pack_rows([weights[n] for n in shard_small_names]),
                  _pack_rows([mom1[n] for n in shard_small_names]),
                  _pack_rows([mom2[n] for n in shard_small_names]),
                  r_small[:, 0])
    for k, o in enumerate(outs):
        for n, a in zip(shard_small_names, _unpack_rows(o, sshapes)):
            res.setdefault(n, [None] * 4)[k] = a

    rshapes = [weights[n].shape for n in rep_names]
    outs = _adamw("adamw_replicated",
                  _pack_rows([weights[n] for n in rep_names]),
                  _pack_rows([mom1[n] for n in rep_names]),
                  _pack_rows([mom2[n] for n in rep_names]),
                  r_rep[:, 0])
    for k, o in enumerate(outs):
        for n, a in zip(rep_names, _unpack_rows(o, rshapes)):
            res.setdefault(n, [None] * 4)[k] = a

    order = ["meta_tokens", "ln_in_g", "ln_in_b", "w_in", "conv_dw_w", "conv_dw_b", "conv_ln_g", "conv_ln_b",
             "conv_pw_w", "conv_pw_b", "attn_sinks", "lru_conv_w", "lru_conv_b", "lru_wa", "lru_ba", "lru_wx",
             "lru_bx", "lru_lambda", "w_out", "ln_post_g", "ln_post_b"]
    return (loss, grad_x,
            *[res[n][0] for n in order], *[res[n][1] for n in order],
            *[res[n][2] for n in order], *[res[n][3] for n in order])
```

```python
import functools
import math

import numpy as np
import jax
import jax.numpy as jnp
from jax import lax
from jax.experimental import pallas as pl
from jax.experimental.pallas import tpu as pltpu

F32 = jnp.float32
BF16 = jnp.bfloat16

D_MODEL = 2048
DEPTH = 2
N_META = 16
TB = 128
PAD0 = TB - N_META
CONV_W = 512
CONV_K = 31
HEAD_DIM = 64
N_HEADS = 16
N_KV = 4
GROUP = 4
ATT_W = 1024
KV_W = 256
ROT_DIM = 16
ROPE_THETA = 500000.0
LRU_W = 512
LRU_HEADS = 8
LRU_CONV_K = 4
LRU_C = 8.0
IN_TOTAL = 5120
N_DEV = 8
W_IN_SHARD = IN_TOTAL // N_DEV
LN_EPS = 1e-5
ALPHA = (2.0 * DEPTH) ** 0.25
NEG_INF = -1e30
ATT_SCALE = HEAD_DIM ** -0.5

ADAM_LR = 0.001
ADAM_B1 = 0.9
ADAM_B2 = 0.999
ADAM_EPS = 1e-08
ADAM_WD = 0.01
ADAM_STEP = 10

VMEM_LIMIT = 56 * 1024 * 1024
PACK_QUANTUM = 256 * 128

COL_CV, COL_CG, COL_CGATE = 0, 1, 2
COL_Q0 = 3
COL_K256 = 10
COL_V256 = 11
COL_AGATE1024 = 3
COL_RX, COL_RGATE = 8, 9
YC_CONV, YC_LRU = 2, 3


def _cp(n_axes, vmem=VMEM_LIMIT):
    return pltpu.CompilerParams(dimension_semantics=("arbitrary",) * n_axes, vmem_limit_bytes=vmem)


def _row_tile(lp, max_blocks):
    nb = lp // TB
    d = max(k for k in range(1, max_blocks + 1) if nb % k == 0)
    return TB * d


def _sig(x):
    return jax.nn.sigmoid(x)


def _dsilu(x, s):
    return s * (1.0 + x * (1.0 - s))


def _ln_core(z):
    mu = jnp.mean(z, axis=-1, keepdims=True)
    zc = z - mu
    var = jnp.mean(zc * zc, axis=-1, keepdims=True)
    rstd = lax.rsqrt(var + LN_EPS)
    return zc * rstd, rstd


def _ln_bwd_core(dy, xh, rstd, g):
    dxh = dy * g
    m1 = jnp.mean(dxh, axis=-1, keepdims=True)
    m2 = jnp.mean(dxh * xh, axis=-1, keepdims=True)
    return rstd * (dxh - m1 - xh * m2)


def _row_ids(shape, base):
    return lax.broadcasted_iota(jnp.int32, shape, 0) + base


def _colsum(x):
    return jnp.sum(x, axis=0, keepdims=True)


def _dot(a, b, dims):
    return lax.dot_general(a, b, (dims, ((), ())), preferred_element_type=F32)


NN = ((1,), (0,))
NT = ((1,), (1,))
TN = ((0,), (0,))


def _embed(x, meta_full, g, b):
    s = x.shape[1]
    lp = s + TB
    nb = lp // TB

    def body(x_ref, m_ref, g_ref, b_ref, o_ref, hb_ref):
        i = pl.program_id(0)

        @pl.when(i == 0)
        def _():
            o_ref[0:PAD0, :] = jnp.zeros((PAD0, D_MODEL), F32)
            o_ref[PAD0:TB, :] = m_ref[...]

        @pl.when(i > 0)
        def _():
            o_ref[...] = x_ref[...]

        xh, _ = _ln_core(o_ref[...])
        h = xh * g_ref[...] + b_ref[...]
        rows = _row_ids(h.shape, i * TB)
        hb_ref[...] = jnp.where(rows >= PAD0, h, 0.0).astype(BF16)

    return pl.pallas_call(
        body, name="embed", grid=(nb,),
        in_specs=[pl.BlockSpec((None, TB, D_MODEL), lambda i: (0, jnp.maximum(i - 1, 0), 0)),
                  pl.BlockSpec((N_META, D_MODEL), lambda i: (0, 0)),
                  pl.BlockSpec((1, D_MODEL), lambda i: (0, 0)),
                  pl.BlockSpec((1, D_MODEL), lambda i: (0, 0))],
        out_specs=[pl.BlockSpec((TB, D_MODEL), lambda i: (i, 0)),
                   pl.BlockSpec((TB, D_MODEL), lambda i: (i, 0))],
        out_shape=[jax.ShapeDtypeStruct((lp, D_MODEL), F32),
                   jax.ShapeDtypeStruct((lp, D_MODEL), BF16)],
        compiler_params=_cp(1),
    )(x, meta_full, g, b)


def _loss_head(z, target, g, b):
    lp = z.shape[0]
    nb = lp // TB

    def body(z_ref, t_ref, g_ref, b_ref, dz_ref, st_ref, loss_ref):
        i = pl.program_id(0)

        @pl.when(i == 0)
        def _():
            st_ref[...] = jnp.zeros(st_ref.shape, F32)
            loss_ref[...] = jnp.zeros(loss_ref.shape, F32)
            dz_ref[...] = jnp.zeros(dz_ref.shape, F32)

        @pl.when(i > 0)
        def _():
            xh, rstd = _ln_core(z_ref[...])
            gg = g_ref[...]
            y = xh * gg + b_ref[...]
            e = y - t_ref[...]
            part = 0.5 * jnp.sum(jnp.mean(e * e, axis=-1, keepdims=True), axis=0, keepdims=True)
            loss_ref[...] += jnp.broadcast_to(part, loss_ref.shape)
            dy = e / float(D_MODEL)
            st_ref[0:1, :] += _colsum(dy * xh)
            st_ref[1:2, :] += _colsum(dy)
            dz_ref[...] = _ln_bwd_core(dy, xh, rstd, gg)

    return pl.pallas_call(
        body, name="loss_head", grid=(nb,),
        in_specs=[pl.BlockSpec((TB, D_MODEL), lambda i: (i, 0)),
                  pl.BlockSpec((None, TB, D_MODEL), lambda i: (0, jnp.maximum(i - 1, 0), 0)),
                  pl.BlockSpec((1, D_MODEL), lambda i: (0, 0)),
                  pl.BlockSpec((1, D_MODEL), lambda i: (0, 0))],
        out_specs=[pl.BlockSpec((TB, D_MODEL), lambda i: (i, 0)),
                   pl.BlockSpec((8, D_MODEL), lambda i: (0, 0)),
                   pl.BlockSpec((8, 128), lambda i: (0, 0))],
        out_shape=[jax.ShapeDtypeStruct((lp, D_MODEL), F32),
                   jax.ShapeDtypeStruct((8, D_MODEL), F32),
                   jax.ShapeDtypeStruct((8, 128), F32)],
        compiler_params=_cp(1),
    )(z, target, g, b)


def _ln_bwd(name, dh, z, g):
    lp = z.shape[0]
    nb = lp // TB

    def body(dh_ref, z_ref, g_ref, dz_ref, st_ref):
        i = pl.program_id(0)

        @pl.when(i == 0)
        def _():
            st_ref[...] = jnp.zeros(st_ref.shape, F32)

        xh, rstd = _ln_core(z_ref[...])
        rows = _row_ids(xh.shape, i * TB)
        dy = jnp.where(rows >= PAD0, dh_ref[...], 0.0)
        st_ref[0:1, :] += _colsum(dy * xh)
        st_ref[1:2, :] += _colsum(dy)
        dz_ref[...] = _ln_bwd_core(dy, xh, rstd, g_ref[...])

    return pl.pallas_call(
        body, name=name, grid=(nb,),
        in_specs=[pl.BlockSpec((TB, D_MODEL), lambda i: (i, 0)),
                  pl.BlockSpec((TB, D_MODEL), lambda i: (i, 0)),
                  pl.BlockSpec((1, D_MODEL), lambda i: (0, 0))],
        out_specs=[pl.BlockSpec((TB, D_MODEL), lambda i: (i, 0)),
                   pl.BlockSpec((8, D_MODEL), lambda i: (0, 0))],
        out_shape=[jax.ShapeDtypeStruct((lp, D_MODEL), F32),
                   jax.ShapeDtypeStruct((8, D_MODEL), F32)],
        compiler_params=_cp(1),
    )(dh, z, g)


def _ln_bwd_input(dh, z, g):
    lp = z.shape[0]
    nb = lp // TB
    s = lp - TB

    def body(dh_ref, z_ref, g_ref, gx_ref, gm_ref, st_ref):
        i = pl.program_id(0)

        @pl.when(i == 0)
        def _():
            st_ref[...] = jnp.zeros(st_ref.shape, F32)

        xh, rstd = _ln_core(z_ref[...])
        rows = _row_ids(xh.shape, i * TB)
        dy = jnp.where(rows >= PAD0, dh_ref[...], 0.0)
        st_ref[0:1, :] += _colsum(dy * xh)
        st_ref[1:2, :] += _colsum(dy)
        dz = _ln_bwd_core(dy, xh, rstd, g_ref[...])
        gx_ref[...] = dz

        @pl.when(i == 0)
        def _():
            gm_ref[...] = dz[PAD0:TB, :]

    return pl.pallas_call(
        body, name="ln_in_bwd", grid=(nb,),
        in_specs=[pl.BlockSpec((TB, D_MODEL), lambda i: (i, 0)),
                  pl.BlockSpec((TB, D_MODEL), lambda i: (i, 0)),
                  pl.BlockSpec((1, D_MODEL), lambda i: (0, 0))],
        out_specs=[pl.BlockSpec((None, TB, D_MODEL), lambda i: (0, jnp.maximum(i - 1, 0), 0)),
                   pl.BlockSpec((N_META, D_MODEL), lambda i: (0, 0)),
                   pl.BlockSpec((8, D_MODEL), lambda i: (0, 0))],
        out_shape=[jax.ShapeDtypeStruct((1, s, D_MODEL), F32),
                   jax.ShapeDtypeStruct((N_META, D_MODEL), F32),
                   jax.ShapeDtypeStruct((8, D_MODEL), F32)],
        compiler_params=_cp(1),
    )(dh, z, g)


def _mm_proj(name, hb, wg_in):
    lp = hb.shape[0]
    tm = lp // 3

    def body(a_ref, b_ref, o_ref):
        b = jnp.concatenate([b_ref[0], b_ref[1]], axis=1)
        o_ref[...] = _dot(a_ref[...], b, NN)

    return pl.pallas_call(
        body, name=name, grid=(3, N_DEV // 2),
        in_specs=[pl.BlockSpec((tm, D_MODEL), lambda i, j: (i, 0)),
                  pl.BlockSpec((2, D_MODEL, W_IN_SHARD), lambda i, j: (j, 0, 0))],
        out_specs=pl.BlockSpec((tm, 2 * W_IN_SHARD), lambda i, j: (i, j)),
        out_shape=jax.ShapeDtypeStruct((lp, IN_TOTAL), F32),
        compiler_params=_cp(2),
    )(hb, wg_in)


def _mm_out(name, ycat, wout, z, g, b, g2, b2):
    lp = ycat.shape[0]
    tm = lp // 6

    def body(a_ref, w_ref, z_ref, g_ref, b_ref, g2_ref, b2_ref, o_ref, hb_ref):
        i = pl.program_id(0)
        xh, _ = _ln_core(z_ref[...])
        h = xh * g_ref[...] + b_ref[...]
        live = _row_ids(h.shape, i * tm) >= PAD0
        h = jnp.where(live, h, 0.0)
        zn = ALPHA * h + _dot(a_ref[...], w_ref[...], NN)
        o_ref[...] = zn
        xh2, _ = _ln_core(zn)
        hb_ref[...] = jnp.where(live, xh2 * g2_ref[...] + b2_ref[...], 0.0).astype(BF16)

    vec = pl.BlockSpec((1, D_MODEL), lambda i: (0, 0))
    return pl.pallas_call(
        body, name=name, grid=(6,),
        in_specs=[pl.BlockSpec((tm, D_MODEL), lambda i: (i, 0)),
                  pl.BlockSpec((D_MODEL, D_MODEL), lambda i: (0, 0), pipeline_mode=pl.Buffered(1)),
                  pl.BlockSpec((tm, D_MODEL), lambda i: (i, 0)),
                  vec, vec, vec, vec],
        out_specs=[pl.BlockSpec((tm, D_MODEL), lambda i: (i, 0)),
                   pl.BlockSpec((tm, D_MODEL), lambda i: (i, 0))],
        out_shape=[jax.ShapeDtypeStruct((lp, D_MODEL), F32),
                   jax.ShapeDtypeStruct((lp, D_MODEL), BF16)],
        compiler_params=_cp(1),
    )(ycat, wout, z, g, b, g2, b2)


def _mm_dycat(name, dz, wout, dep):
    lp = dz.shape[0]
    tm = lp // 6

    def body(a_ref, w_ref, dep_ref, o_ref):
        del dep_ref
        o_ref[...] = _dot(a_ref[...].astype(BF16), w_ref[...], NT).astype(BF16)

    return pl.pallas_call(
        body, name=name, grid=(6,),
        in_specs=[pl.BlockSpec((tm, D_MODEL), lambda i: (i, 0)),
                  pl.BlockSpec((D_MODEL, D_MODEL), lambda i: (0, 0), pipeline_mode=pl.Buffered(1)),
                  pl.BlockSpec(memory_space=pl.ANY)],
        out_specs=pl.BlockSpec((tm, D_MODEL), lambda i: (i, 0)),
        out_shape=jax.ShapeDtypeStruct((lp, D_MODEL), BF16),
        compiler_params=_cp(1),
    )(dz, wout, dep)


def _mm_dwout(name, ycat, dz):
    lp = ycat.shape[0]
    tk = _row_tile(lp, 11)
    nk = lp // tk
    half = D_MODEL // 2

    def body(a_ref, b_ref, o_ref, acc_ref):
        k = pl.program_id(1)

        @pl.when(k == 0)
        def _():
            acc_ref[...] = jnp.zeros(acc_ref.shape, F32)

        acc_ref[...] += _dot(a_ref[...], b_ref[...].astype(BF16), TN)

        @pl.when(k == nk - 1)
        def _():
            o_ref[...] = acc_ref[...].astype(BF16)

    return pl.pallas_call(
        body, name=name, grid=(2, nk),
        in_specs=[pl.BlockSpec((tk, half), lambda h, k: (k, h)),
                  pl.BlockSpec((tk, D_MODEL), lambda h, k: (k, 0))],
        out_specs=pl.BlockSpec((half, D_MODEL), lambda h, k: (h, 0)),
        out_shape=jax.ShapeDtypeStruct((D_MODEL, D_MODEL), BF16),
        scratch_shapes=[pltpu.VMEM((half, D_MODEL), F32)],
        compiler_params=_cp(2),
    )(ycat, dz)


def _mm_dwin(name, hb, dproj):
    lp = hb.shape[0]
    tk = _row_tile(lp, 11)
    nk = lp // tk

    def body(a_ref, b_ref, o_ref, acc_ref):
        k = pl.program_id(1)

        @pl.when(k == 0)
        def _():
            acc_ref[...] = jnp.zeros(acc_ref.shape, F32)

        acc_ref[...] += _dot(a_ref[...], b_ref[...], TN)

        @pl.when(k == nk - 1)
        def _():
            o_ref[0] = acc_ref[:, 0:W_IN_SHARD].astype(BF16)
            o_ref[1] = acc_ref[:, W_IN_SHARD:2 * W_IN_SHARD].astype(BF16)

    return pl.pallas_call(
        body, name=name, grid=(4, nk),
        in_specs=[pl.BlockSpec((tk, D_MODEL), lambda j, k: (k, 0)),
                  pl.BlockSpec((tk, 2 * W_IN_SHARD), lambda j, k: (k, j))],
        out_specs=pl.BlockSpec((2, D_MODEL, W_IN_SHARD), lambda j, k: (j, 0, 0)),
        out_shape=jax.ShapeDtypeStruct((N_DEV, D_MODEL, W_IN_SHARD), BF16),
        scratch_shapes=[pltpu.VMEM((D_MODEL, 2 * W_IN_SHARD), F32)],
        compiler_params=_cp(2),
    )(hb, dproj)


def _mm_dh(name, dproj, wg_in, dz, dep):
    lp = dproj.shape[0]
    tm = lp // 6

    def body(a_ref, w_ref, dz_ref, dep_ref, o_ref, acc_ref):
        del dep_ref
        k = pl.program_id(1)

        @pl.when(k == 0)
        def _():
            acc_ref[...] = jnp.zeros(acc_ref.shape, F32)

        w = jnp.concatenate([w_ref[0], w_ref[1]], axis=1)
        acc_ref[...] += _dot(a_ref[...], w, NT)

        @pl.when(k == N_DEV // 2 - 1)
        def _():
            o_ref[...] = acc_ref[...] + ALPHA * dz_ref[...]

    return pl.pallas_call(
        body, name=name, grid=(6, N_DEV // 2),
        in_specs=[pl.BlockSpec((tm, 2 * W_IN_SHARD), lambda i, k: (i, k)),
                  pl.BlockSpec((2, D_MODEL, W_IN_SHARD), lambda i, k: (k, 0, 0)),
                  pl.BlockSpec((tm, D_MODEL), lambda i, k: (i, 0)),
                  pl.BlockSpec(memory_space=pl.ANY)],
        out_specs=pl.BlockSpec((tm, D_MODEL), lambda i, k: (i, 0)),
        out_shape=jax.ShapeDtypeStruct((lp, D_MODEL), F32),
        scratch_shapes=[pltpu.VMEM((tm, D_MODEL), F32)],
        compiler_params=_cp(2),
    )(dproj, wg_in, dz, dep)


def _shifted_views(cat, n_shift, base, rows):
    total = cat.shape[0]
    rolled = [cat] + [pltpu.roll(cat, b, axis=0) for b in range(1, 8)]
    views = []
    for s in range(n_shift):
        a, b = divmod(s, 8)
        views.append(rolled[b][base - 8 * a: base - 8 * a + rows, :])
    del total
    return views


def _conv_chain(j, cv_ref, cg_ref, cvp_ref, cgp_ref, wdw_ref, vec_ref, wpw_ref):
    cv = cv_ref[...]
    sg = _sig(cg_ref[...])
    c0 = cv * sg
    c0p = jnp.where(j > 0, cvp_ref[...] * _sig(cgp_ref[...]), 0.0)
    cat = jnp.concatenate([c0p, c0], axis=0)
    views = _shifted_views(cat, CONV_K, TB, TB)
    c1 = jnp.broadcast_to(vec_ref[0:1, :], (TB, CONV_W))
    for k in range(CONV_K):
        c1 = c1 + wdw_ref[k:k + 1, :] * views[CONV_K - 1 - k]
    xh, rstd = _ln_core(c1)
    c2 = xh * vec_ref[1:2, :] + vec_ref[2:3, :]
    s2 = _sig(c2)
    c3 = c2 * s2
    c4 = _dot(c3.astype(BF16), wpw_ref[...], NN) + vec_ref[3:4, :]
    return dict(cv=cv, sg=sg, views=views, xh=xh, rstd=rstd, c2=c2, s2=s2, c3=c3, c4=c4)


def _conv_in_specs(jmap):
    def cur(col):
        return pl.BlockSpec((TB, 512), lambda n: (jmap(n), col))

    def prev(col):
        return pl.BlockSpec((TB, 512), lambda n: (jnp.maximum(jmap(n) - 1, 0), col))

    return [cur(COL_CV), cur(COL_CG), prev(COL_CV), prev(COL_CG), cur(COL_CGATE)]


def _conv_param_specs():
    return [pl.BlockSpec((32, CONV_W), lambda n: (0, 0)),
            pl.BlockSpec((8, CONV_W), lambda n: (0, 0)),
            pl.BlockSpec((CONV_W, CONV_W), lambda n: (0, 0))]


def _conv_fwd(name, proj, wdw, vec, wpw):
    lp = proj.shape[0]
    nb = lp // TB

    def body(cv_ref, cg_ref, cvp_ref, cgp_ref, gate_ref, wdw_ref, vec_ref, wpw_ref, o_ref):
        j = pl.program_id(0)
        c = _conv_chain(j, cv_ref, cg_ref, cvp_ref, cgp_ref, wdw_ref, vec_ref, wpw_ref)
        gate = gate_ref[...]
        o_ref[...] = (c["c4"] * (gate * _sig(gate))).astype(BF16)

    return pl.pallas_call(
        body, name=name, grid=(nb,),
        in_specs=_conv_in_specs(lambda n: n) + _conv_param_specs(),
        out_specs=pl.BlockSpec((TB, 512), lambda n: (n, YC_CONV)),
        out_shape=jax.ShapeDtypeStruct((lp, D_MODEL), BF16),
        compiler_params=_cp(1),
    )(proj, proj, proj, proj, proj, wdw, vec, wpw)


def _conv_bwd(name, proj, dycat, wdw, vec, wpw):
    lp = proj.shape[0]
    nb = lp // TB
    halo = 32

    def body(cv_ref, cg_ref, cvp_ref, cgp_ref, gate_ref, dy_ref, wdw_ref, vec_ref, wpw_ref,
             dp_ref, dwdw_ref, dvec_ref, dwpw_ref, carry_ref):
        n = pl.program_id(0)
        j = nb - 1 - n

        @pl.when(n == 0)
        def _():
            carry_ref[...] = jnp.zeros(carry_ref.shape, F32)
            dwdw_ref[...] = jnp.zeros(dwdw_ref.shape, F32)
            dvec_ref[...] = jnp.zeros(dvec_ref.shape, F32)
            dwpw_ref[...] = jnp.zeros(dwpw_ref.shape, F32)

        c = _conv_chain(j, cv_ref, cg_ref, cvp_ref, cgp_ref, wdw_ref, vec_ref, wpw_ref)
        dy = dy_ref[...].astype(F32)
        gate = gate_ref[...]
        sgate = _sig(gate)
        dc4 = dy * (gate * sgate)
        dgate = dy * c["c4"] * _dsilu(gate, sgate)
        dc4b = dc4.astype(BF16)
        dvec_ref[3:4, :] += _colsum(dc4)
        dwpw_ref[...] += _dot(c["c3"].astype(BF16), dc4b, TN)
        dc3 = _dot(dc4b, wpw_ref[...], NT)
        dc2 = dc3 * _dsilu(c["c2"], c["s2"])
        dvec_ref[1:2, :] += _colsum(dc2 * c["xh"])
        dvec_ref[2:3, :] += _colsum(dc2)
        dc1 = _ln_bwd_core(dc2, c["xh"], c["rstd"], vec_ref[1:2, :])
        dvec_ref[0:1, :] += _colsum(dc1)
        for k in range(CONV_K):
            dwdw_ref[k:k + 1, :] += _colsum(dc1 * c["views"][CONV_K - 1 - k])
        dcat = jnp.concatenate([dc1, carry_ref[...]], axis=0)
        total = TB + halo
        up = [dcat] + [pltpu.roll(dcat, total - b, axis=0) for b in range(1, 8)]
        dc0 = jnp.zeros((TB, CONV_W), F32)
        for k in range(CONV_K):
            a, b = divmod(CONV_K - 1 - k, 8)
            dc0 = dc0 + wdw_ref[k:k + 1, :] * up[b][8 * a: 8 * a + TB, :]
        carry_ref[...] = dc1[0:halo, :]
        sg = c["sg"]
        dcv = dc0 * sg
        dcg = dc0 * c["cv"] * sg * (1.0 - sg)
        dp_ref[:, 0:512] = dcv.astype(BF16)
        dp_ref[:, 512:1024] = dcg.astype(BF16)
        dp_ref[:, 1024:1536] = dgate.astype(BF16)

    jmap = lambda n: nb - 1 - n
    return pl.pallas_call(
        body, name=name, grid=(nb,),
        in_specs=(_conv_in_specs(jmap)
                  + [pl.BlockSpec((TB, 512), lambda n: (jmap(n), YC_CONV))]
                  + _conv_param_specs()),
        out_specs=[pl.BlockSpec((TB, 1536), lambda n: (jmap(n), 0)),
                   pl.BlockSpec((32, CONV_W), lambda n: (0, 0)),
                   pl.BlockSpec((8, CONV_W), lambda n: (0, 0)),
                   pl.BlockSpec((CONV_W, CONV_W), lambda n: (0, 0))],
        out_shape=[jax.ShapeDtypeStruct((lp, IN_TOTAL), BF16),
                   jax.ShapeDtypeStruct((32, CONV_W), F32),
                   jax.ShapeDtypeStruct((8, CONV_W), F32),
                   jax.ShapeDtypeStruct((CONV_W, CONV_W), F32)],
        scratch_shapes=[pltpu.VMEM((halo, CONV_W), F32)],
        compiler_params=_cp(1),
    )(proj, proj, proj, proj, proj, dycat, wdw, vec, wpw)


def _rope_tables(lp):
    half = ROT_DIM // 2
    inv_freq = ROPE_THETA ** (-jnp.arange(half, dtype=F32) / half)
    pos = (jnp.arange(lp, dtype=jnp.int32) - PAD0).astype(F32)
    ang = pos[:, None] * inv_freq[None, :]
    cos, sin = jnp.cos(ang), jnp.sin(ang)
    ones = jnp.ones((lp, HEAD_DIM - ROT_DIM), F32)
    zeros = jnp.zeros((lp, HEAD_DIM - ROT_DIM), F32)
    zh = jnp.zeros((lp, half), F32)
    c = jnp.concatenate([cos, cos, ones], axis=1)
    sa = jnp.concatenate([-sin, zh, zeros], axis=1)
    sb = jnp.concatenate([zh, sin, zeros], axis=1)
    tile = lambda t: jnp.tile(t, (1, KV_W // HEAD_DIM))
    return tile(c), tile(sa), tile(sb)


def _rot(x, c, sa, sb):
    w = x.shape[1]
    return x * c + pltpu.roll(x, w - 8, axis=1) * sa + pltpu.roll(x, 8, axis=1) * sb


def _rot_t(dy, c, sa, sb):
    w = dy.shape[1]
    return dy * c + pltpu.roll(dy * sa, 8, axis=1) + pltpu.roll(dy * sb, w - 8, axis=1)


def _rope_fwd(name, proj, tabs):
    lp = proj.shape[0]
    tr = _row_tile(lp, 3)

    def body(q0_ref, q1_ref, k_ref, c_ref, sa_ref, sb_ref, qr_ref, kr_ref):
        c, sa, sb = c_ref[...], sa_ref[...], sb_ref[...]
        c2 = jnp.concatenate([c, c], axis=1)
        sa2 = jnp.concatenate([sa, sa], axis=1)
        sb2 = jnp.concatenate([sb, sb], axis=1)
        qr_ref[:, 0:512] = (_rot(q0_ref[...], c2, sa2, sb2) * ATT_SCALE).astype(BF16)
        qr_ref[:, 512:1024] = (_rot(q1_ref[...], c2, sa2, sb2) * ATT_SCALE).astype(BF16)
        kr_ref[...] = _rot(k_ref[...], c, sa, sb).astype(BF16)

    tab = pl.BlockSpec((tr, KV_W), lambda i: (i, 0))
    return pl.pallas_call(
        body, name=name, grid=(lp // tr,),
        in_specs=[pl.BlockSpec((tr, 512), lambda i: (i, COL_Q0)),
                  pl.BlockSpec((tr, 512), lambda i: (i, COL_Q0 + 1)),
                  pl.BlockSpec((tr, KV_W), lambda i: (i, COL_K256)),
                  tab, tab, tab],
        out_specs=[pl.BlockSpec((tr, ATT_W), lambda i: (i, 0)),
                   pl.BlockSpec((tr, KV_W), lambda i: (i, 0))],
        out_shape=[jax.ShapeDtypeStruct((lp, ATT_W), BF16),
                   jax.ShapeDtypeStruct((lp, KV_W), BF16)],
        compiler_params=_cp(1),
    )(proj, proj, proj, *tabs)


def _attn_mask(j):
    qi = lax.broadcasted_iota(jnp.int32, (GROUP * TB, 3 * TB), 0) & (TB - 1)
    cc = lax.broadcasted_iota(jnp.int32, (GROUP * TB, 3 * TB), 1)
    jj = cc & (TB - 1)
    is_meta = jj >= PAD0
    p0 = (cc < TB) & is_meta & (j >= 1)
    p1 = (cc >= TB) & (cc < 2 * TB) & (jj > qi) & (j >= 2)
    p2 = (cc >= 2 * TB) & (jj <= qi) & ((j >= 1) | is_meta)
    return p0 | p1 | p2


def _lane_group(rows):
    return lax.broadcasted_iota(jnp.int32, (rows, KV_W), 1) // HEAD_DIM


def _stack_heads(x, kv, lgq):
    parts = []
    for g in range(GROUP):
        sh = ((kv - g) % GROUP) * HEAD_DIM
        moved = x if sh == 0 else pltpu.roll(x, sh, axis=1)
        parts.append(jnp.where(lgq == kv, moved, 0.0))
    return jnp.concatenate(parts, axis=0).astype(BF16)


def _unstack_heads(r, kv):
    out = None
    for g in range(GROUP):
        blk = r[g * TB:(g + 1) * TB, :]
        sh = ((g - kv) % GROUP) * HEAD_DIM
        blk = blk if sh == 0 else pltpu.roll(blk, sh, axis=1)
        out = blk if out is None else out + blk
    return out


def _sink_column(sinks, kv):
    lane = lax.broadcasted_iota(jnp.int32, (1, 128), 1)
    cols = []
    for g in range(GROUP):
        sg = jnp.sum(jnp.where(lane == kv * GROUP + g, sinks, 0.0), axis=1, keepdims=True)
        cols.append(jnp.broadcast_to(sg, (TB, 1)))
    return jnp.concatenate(cols, axis=0)


def _attn_kv(kall, vall, lg, kv):
    km = jnp.where(lg == kv, kall, 0.0).astype(BF16)
    ones = jnp.where(lg == (kv + 1) % N_KV, 1.0, 0.0)
    vm = jnp.where(lg == kv, vall, ones).astype(BF16)
    return km, vm


def _attn_weights(qst, km, vm, sinkcol, valid, lg4, kv):
    s = jnp.where(valid, _dot(qst, km, NT), NEG_INF)
    m = jnp.maximum(jnp.max(s, axis=-1, keepdims=True), sinkcol)
    eb = jnp.exp(s - m).astype(BF16)
    es = jnp.exp(sinkcol - m)
    r = _dot(eb, vm, NN)
    rowsum = pltpu.roll(r, KV_W - HEAD_DIM, axis=1)
    inv = 1.0 / (rowsum + es)
    out = jnp.where(lg4 == kv, r * inv, 0.0)
    return eb, es, inv, out


def _attn_specs(jmap):
    blk = lambda col: pl.BlockSpec((TB, KV_W), lambda n: (jmap(n), col))
    prv = lambda col: pl.BlockSpec((TB, KV_W), lambda n: (jnp.maximum(jmap(n) - 1, 0), col))
    met = lambda col: pl.BlockSpec((TB, KV_W), lambda n: (0, col))
    return dict(
        qr=pl.BlockSpec((TB, ATT_W), lambda n: (jmap(n), 0)),
        k=[met(0), prv(0), blk(0)],
        v=[met(COL_V256), prv(COL_V256), blk(COL_V256)],
        gate=pl.BlockSpec((TB, ATT_W), lambda n: (jmap(n), COL_AGATE1024)),
        sinks=pl.BlockSpec((8, 128), lambda n: (0, 0)),
    )


def _attn_fwd(name, qr, kr, proj, sinks_row, ycat):
    lp = proj.shape[0]
    nb = lp // TB
    sp = _attn_specs(lambda n: n)

    def body(qr_ref, km_ref, kp_ref, kc_ref, vm_ref, vp_ref, vc_ref, gate_ref, sink_ref, yin_ref, o_ref):
        del yin_ref
        j = pl.program_id(0)
        valid = _attn_mask(j)
        kall = jnp.concatenate([km_ref[...], kp_ref[...], kc_ref[...]], axis=0).astype(F32)
        vall = jnp.concatenate([vm_ref[...], vp_ref[...], vc_ref[...]], axis=0)
        lg = _lane_group(3 * TB)
        lgq = _lane_group(TB)
        lg4 = _lane_group(GROUP * TB)
        sinks = sink_ref[0:1, :]
        for kv in range(N_KV):
            cols = slice(kv * KV_W, (kv + 1) * KV_W)
            km, vm = _attn_kv(kall, vall, lg, kv)
            qst = _stack_heads(qr_ref[:, cols].astype(F32), kv, lgq)
            _, _, _, out = _attn_weights(qst, km, vm, _sink_column(sinks, kv), valid, lg4, kv)
            att = _unstack_heads(out, kv)
            gate = gate_ref[:, cols]
            o_ref[:, cols] = (att * (gate * _sig(gate))).astype(BF16)

    return pl.pallas_call(
        body, name=name, grid=(nb,),
        in_specs=[sp["qr"]] + sp["k"] + sp["v"] + [sp["gate"], sp["sinks"],
                                                   pl.BlockSpec(memory_space=pl.ANY)],
        out_specs=pl.BlockSpec((TB, ATT_W), lambda n: (n, 0)),
        out_shape=jax.ShapeDtypeStruct((lp, D_MODEL), BF16),
        input_output_aliases={9: 0},
        compiler_params=_cp(1),
    )(qr, kr, kr, kr, proj, proj, proj, proj, sinks_row, ycat)


def _attn_bwd(name, qr, kr, proj, sinks_row, dycat, dep):
    lp = proj.shape[0]
    nb = lp // TB
    sp = _attn_specs(lambda n: n)

    def body(qr_ref, km_ref, kp_ref, kc_ref, vm_ref, vp_ref, vc_ref, gate_ref, sink_ref, dy_ref, dep_ref,
             dq_ref, dgate_ref, dk_ref, dv_ref, dsink_ref):
        del dep_ref
        j = pl.program_id(0)

        @pl.when(j == 0)
        def _():
            dk_ref[...] = jnp.zeros(dk_ref.shape, F32)
            dv_ref[...] = jnp.zeros(dv_ref.shape, F32)
            dsink_ref[...] = jnp.zeros(dsink_ref.shape, F32)

        valid = _attn_mask(j)
        kall = jnp.concatenate([km_ref[...], kp_ref[...], kc_ref[...]], axis=0).astype(F32)
        vall = jnp.concatenate([vm_ref[...], vp_ref[...], vc_ref[...]], axis=0)
        lg = _lane_group(3 * TB)
        lgq = _lane_group(TB)
        lg4 = _lane_group(GROUP * TB)
        sinks = sink_ref[0:1, :]
        lane = lax.broadcasted_iota(jnp.int32, (1, 128), 1)
        dkall = jnp.zeros((3 * TB, KV_W), F32)
        dvall = jnp.zeros((3 * TB, KV_W), F32)
        dsink = jnp.zeros((1, 128), F32)
        for kv in range(N_KV):
            cols = slice(kv * KV_W, (kv + 1) * KV_W)
            km, vm = _attn_kv(kall, vall, lg, kv)
            qst = _stack_heads(qr_ref[:, cols].astype(F32), kv, lgq)
            gate = gate_ref[:, cols]
            sgate = _sig(gate)
            dy = dy_ref[:, cols].astype(F32)
            dout = dy * (gate * sgate)
            eb, es, inv, out = _attn_weights(qst, km, vm, _sink_column(sinks, kv), valid, lg4, kv)
            att = _unstack_heads(out, kv)
            dgate_ref[:, cols] = (dy * att * _dsilu(gate, sgate)).astype(BF16)
            dsc = dout * _unstack_heads(jnp.where(lg4 == kv, inv, 0.0), kv)
            dost = _stack_heads(dsc, kv, lgq)
            dd = dsc * att
            dcol = jnp.concatenate(
                [jnp.sum(jnp.where(lgq == g, dd, 0.0), axis=1, keepdims=True) for g in range(GROUP)], axis=0)
            dp = _dot(dost, vm, NT)
            ds = (eb.astype(F32) * (dp - dcol)).astype(BF16)
            pd = es * dcol
            for g in range(GROUP):
                tot = jnp.sum(pd[g * TB:(g + 1) * TB, :], axis=0, keepdims=True)
                dsink = dsink - jnp.where(lane == kv * GROUP + g, tot, 0.0)
            dq_ref[:, cols] = _unstack_heads(_dot(ds, km, NN), kv)
            dkall = dkall + _dot(ds, qst, TN)
            dvall = dvall + _dot(eb, dost, TN)
        dsink_ref[0:1, :] += dsink
        prev = pl.multiple_of(jnp.maximum(j - 1, 0) * TB, TB)
        cur = pl.multiple_of(j * TB, TB)
        dk_ref[0:TB, :] += dkall[0:TB]
        dv_ref[0:TB, :] += dvall[0:TB]
        dk_ref[pl.ds(prev, TB), :] += dkall[TB:2 * TB]
        dv_ref[pl.ds(prev, TB), :] += dvall[TB:2 * TB]
        dk_ref[pl.ds(cur, TB), :] += dkall[2 * TB:3 * TB]
        dv_ref[pl.ds(cur, TB), :] += dvall[2 * TB:3 * TB]

    return pl.pallas_call(
        body, name=name, grid=(nb,),
        in_specs=[sp["qr"]] + sp["k"] + sp["v"] + [sp["gate"], sp["sinks"],
                                                   pl.BlockSpec((TB, ATT_W), lambda n: (n, 0)),
                                                   pl.BlockSpec(memory_space=pl.ANY)],
        out_specs=[pl.BlockSpec((TB, ATT_W), lambda n: (n, 0)),
                   pl.BlockSpec((TB, ATT_W), lambda n: (n, 0)),
                   pl.BlockSpec((lp, KV_W), lambda n: (0, 0)),
                   pl.BlockSpec((lp, KV_W), lambda n: (0, 0)),
                   pl.BlockSpec((8, 128), lambda n: (0, 0))],
        out_shape=[jax.ShapeDtypeStruct((lp, ATT_W), F32),
                   jax.ShapeDtypeStruct((lp, ATT_W), BF16),
                   jax.ShapeDtypeStruct((lp, KV_W), F32),
                   jax.ShapeDtypeStruct((lp, KV_W), F32),
                   jax.ShapeDtypeStruct((8, 128), F32)],
        compiler_params=_cp(1),
    )(qr, kr, kr, kr, proj, proj, proj, proj, sinks_row, dycat, dep)


def _attn_assemble(name, dq, dgate, dk, dv, tabs, dproj):
    lp = dq.shape[0]
    tr = _row_tile(lp, 3)

    def body(dq_ref, dg_ref, dk_ref, dv_ref, c_ref, sa_ref, sb_ref, din_ref, o_ref):
        del din_ref
        cidx = pl.program_id(1)
        c, sa, sb = c_ref[...], sa_ref[...], sb_ref[...]

        @pl.when(cidx < 2)
        def _():
            c2 = jnp.concatenate([c, c], axis=1)
            sa2 = jnp.concatenate([sa, sa], axis=1)
            sb2 = jnp.concatenate([sb, sb], axis=1)
            o_ref[...] = (_rot_t(dq_ref[...], c2, sa2, sb2) * ATT_SCALE).astype(BF16)

        @pl.when(cidx == 2)
        def _():
            o_ref[:, 0:KV_W] = _rot_t(dk_ref[...], c, sa, sb).astype(BF16)
            o_ref[:, KV_W:2 * KV_W] = dv_ref[...].astype(BF16)

        @pl.when(cidx > 2)
        def _():
            o_ref[...] = dg_ref[...]

    tab = pl.BlockSpec((tr, KV_W), lambda n, c: (n, 0))
    return pl.pallas_call(
        body, name=name, grid=(lp // tr, 5),
        in_specs=[pl.BlockSpec((tr, 512), lambda n, c: (n, jnp.minimum(c, 1))),
                  pl.BlockSpec((tr, 512), lambda n, c: (n, jnp.clip(c - 3, 0, 1))),
                  tab, tab,
                  tab, tab, tab,
                  pl.BlockSpec(memory_space=pl.ANY)],
        out_specs=pl.BlockSpec((tr, 512), lambda n, c: (n, COL_Q0 + c)),
        out_shape=jax.ShapeDtypeStruct((lp, IN_TOTAL), BF16),
        input_output_aliases={7: 0},
        compiler_params=_cp(2),
    )(dq, dgate, dk, dv, *tabs, dproj)


def _softplus_neg(lam):
    t = jnp.exp(-jnp.abs(lam))
    u = 1.0 + t
    den = jnp.where(u == 1.0, 1.0, u - 1.0)
    l1p = jnp.where(u == 1.0, t, jnp.log(u) * (t / den))
    return jnp.maximum(-lam, 0.0) + l1p


def _lru_chain(j, rx_ref, rxp_ref, wl_ref, vec_ref, wa_ref, wx_ref):
    rx = rx_ref[...]
    rxp = jnp.where(j > 0, rxp_ref[...], 0.0)
    cat = jnp.concatenate([rxp, rx], axis=0)
    views = [cat[8:8 + TB, :]] + [pltpu.roll(cat, s, axis=0)[8:8 + TB, :] for s in range(1, LRU_CONV_K)]
    x1 = jnp.broadcast_to(vec_ref[0:1, :], (TB, LRU_W))
    for k in range(LRU_CONV_K):
        x1 = x1 + wl_ref[k:k + 1, :] * views[LRU_CONV_K - 1 - k]
    x1b = x1.astype(BF16)
    r = _sig(_dot(x1b, wa_ref[...], NN) + vec_ref[1:2, :])
    ig = _sig(_dot(x1b, wx_ref[...], NN) + vec_ref[2:3, :])
    sp = _softplus_neg(vec_ref[3:4, :])
    log_a = -LRU_C * r * sp
    rows = _row_ids((TB, LRU_W), j * TB)
    live = rows >= PAD0
    a = jnp.where(live, jnp.exp(log_a), 0.0)
    y2 = 2.0 * log_a
    em = -jnp.tanh(0.5 * y2) * (jnp.exp(y2) + 1.0)
    mult = jnp.sqrt(em)
    return dict(views=views, x1=x1, x1b=x1b, r=r, ig=ig, sp=sp, a=a, mult=mult, live=live, a_raw=jnp.exp(log_a))


def _lru_specs(jmap):
    return [pl.BlockSpec((TB, 512), lambda n: (jmap(n), COL_RX)),
            pl.BlockSpec((8, 512), lambda n: (jnp.maximum(jmap(n) * (TB // 8) - 1, 0), COL_RX)),
            pl.BlockSpec((TB, 512), lambda n: (jmap(n), COL_RGATE))]


def _lru_param_specs():
    return [pl.BlockSpec((8, LRU_W), lambda n: (0, 0)),
            pl.BlockSpec((8, LRU_W), lambda n: (0, 0)),
            pl.BlockSpec((LRU_W, LRU_W), lambda n: (0, 0)),
            pl.BlockSpec((LRU_W, LRU_W), lambda n: (0, 0))]


def _lru_fwd(name, proj, wl, vec, wa, wx, ycat):
    lp = proj.shape[0]
    nb = lp // TB

    def body(rx_ref, rxp_ref, gate_ref, wl_ref, vec_ref, wa_ref, wx_ref, yin_ref, o_ref, h_ref, carry_ref):
        del yin_ref
        j = pl.program_id(0)

        @pl.when(j == 0)
        def _():
            carry_ref[...] = jnp.zeros(carry_ref.shape, F32)

        c = _lru_chain(j, rx_ref, rxp_ref, wl_ref, vec_ref, wa_ref, wx_ref)
        a = c["a"]
        u = jnp.where(c["live"], c["mult"] * (c["ig"] * c["x1"]), 0.0)
        rows = lax.broadcasted_iota(jnp.int32, (TB, LRU_W), 0)
        d = 1
        while d < TB:
            ap = jnp.where(rows >= d, pltpu.roll(a, d, axis=0), 1.0)
            up = jnp.where(rows >= d, pltpu.roll(u, d, axis=0), 0.0)
            u = a * up + u
            a = a * ap
            d *= 2
        h = u + a * carry_ref[0:1, :]
        carry_ref[...] = h[TB - 8:TB, :]
        carry_ref[0:1, :] = h[TB - 1:TB, :]
        h_ref[...] = h
        gate = gate_ref[...]
        o_ref[...] = (h * (gate * _sig(gate))).astype(BF16)

    return pl.pallas_call(
        body, name=name, grid=(nb,),
        in_specs=_lru_specs(lambda n: n) + _lru_param_specs() + [pl.BlockSpec(memory_space=pl.ANY)],
        out_specs=[pl.BlockSpec((TB, 512), lambda n: (n, YC_LRU)),
                   pl.BlockSpec((TB, LRU_W), lambda n: (n, 0))],
        out_shape=[jax.ShapeDtypeStruct((lp, D_MODEL), BF16),
                   jax.ShapeDtypeStruct((lp, LRU_W), F32)],
        input_output_aliases={7: 0},
        scratch_shapes=[pltpu.VMEM((8, LRU_W), F32)],
        compiler_params=_cp(1),
    )(proj, proj, proj, wl, vec, wa, wx, ycat)


def _lru_bwd(name, proj, dycat, hstate, wl, vec, wa, wx, dproj):
    lp = proj.shape[0]
    nb = lp // TB

    def body(rx_ref, rxp_ref, gate_ref, dy_ref, h_ref, hp_ref, wl_ref, vec_ref, wa_ref, wx_ref, din_ref,
             dp_ref, dwl_ref, dvec_ref, dwa_ref, dwx_ref, dhc_ref, anx_ref, dxc_ref):
        del din_ref
        n = pl.program_id(0)
        j = nb - 1 - n

        @pl.when(n == 0)
        def _():
            dhc_ref[...] = jnp.zeros(dhc_ref.shape, F32)
            anx_ref[...] = jnp.zeros(anx_ref.shape, F32)
            dxc_ref[...] = jnp.zeros(dxc_ref.shape, F32)
            dwl_ref[...] = jnp.zeros(dwl_ref.shape, F32)
            dvec_ref[...] = jnp.zeros(dvec_ref.shape, F32)
            dwa_ref[...] = jnp.zeros(dwa_ref.shape, F32)
            dwx_ref[...] = jnp.zeros(dwx_ref.shape, F32)

        c = _lru_chain(j, rx_ref, rxp_ref, wl_ref, vec_ref, wa_ref, wx_ref)
        a, mult, r, ig, x1, live = c["a"], c["mult"], c["r"], c["ig"], c["x1"], c["live"]
        h = h_ref[...]
        gate = gate_ref[...]
        sgate = _sig(gate)
        dy = dy_ref[...].astype(F32)
        gsum = dy * (gate * sgate)
        dgate = dy * h * _dsilu(gate, sgate)
        rows = lax.broadcasted_iota(jnp.int32, (TB, LRU_W), 0)
        bb = jnp.where(rows == TB - 1, anx_ref[0:1, :], pltpu.roll(a, TB - 1, axis=0))
        gg = gsum
        d = 1
        while d < TB:
            keep = rows < TB - d
            bn = jnp.where(keep, pltpu.roll(bb, TB - d, axis=0), 1.0)
            gn = jnp.where(keep, pltpu.roll(gg, TB - d, axis=0), 0.0)
            gg = gg + bb * gn
            bb = bb * bn
            d *= 2
        dh = gg + bb * dhc_ref[0:1, :]
        dhc_ref[...] = dh[0:8, :]
        anx_ref[...] = a[0:8, :]
        hprev = jnp.where(rows == 0, jnp.where(j > 0, hp_ref[7:8, :], 0.0), pltpu.roll(h, 1, axis=0))
        du = jnp.where(live, dh, 0.0)
        da = jnp.where(live, dh * hprev, 0.0)
        ar = c["a_raw"]
        dmult = du * (ig * x1)
        di = du * mult * x1
        dx1 = du * mult * ig
        dloga = da * ar - dmult * ar * ar / mult
        dr = dloga * (-LRU_C * c["sp"])
        dvec_ref[3:4, :] += _colsum(dloga * (-LRU_C * r))
        dza = dr * r * (1.0 - r)
        dzx = di * ig * (1.0 - ig)
        dzab, dzxb = dza.astype(BF16), dzx.astype(BF16)
        dvec_ref[1:2, :] += _colsum(dza)
        dvec_ref[2:3, :] += _colsum(dzx)
        dwa_ref[...] += _dot(c["x1b"], dzab, TN)
        dwx_ref[...] += _dot(c["x1b"], dzxb, TN)
        dx1 = dx1 + _dot(dzab, wa_ref[...], NT) + _dot(dzxb, wx_ref[...], NT)
        dvec_ref[0:1, :] += _colsum(dx1)
        for k in range(LRU_CONV_K):
            dwl_ref[k:k + 1, :] += _colsum(dx1 * c["views"][LRU_CONV_K - 1 - k])
        dcat = jnp.concatenate([dx1, dxc_ref[...]], axis=0)
        drx = jnp.zeros((TB, LRU_W), F32)
        for k in range(LRU_CONV_K):
            s = LRU_CONV_K - 1 - k
            view = dcat[0:TB, :] if s == 0 else pltpu.roll(dcat, TB + 8 - s, axis=0)[0:TB, :]
            drx = drx + wl_ref[k:k + 1, :] * view
        dxc_ref[...] = dx1[0:8, :]
        dp_ref[:, 0:512] = drx.astype(BF16)
        dp_ref[:, 512:1024] = dgate.astype(BF16)

        @pl.when(n == nb - 1)
        def _():
            lam = vec_ref[3:4, :]
            dvec_ref[3:4, :] = dvec_ref[3:4, :] * (-_sig(-lam))

    jmap = lambda n: nb - 1 - n
    return pl.pallas_call(
        body, name=name, grid=(nb,),
        in_specs=(_lru_specs(jmap)
                  + [pl.BlockSpec((TB, 512), lambda n: (jmap(n), YC_LRU)),
                     pl.BlockSpec((TB, LRU_W), lambda n: (jmap(n), 0)),
                     pl.BlockSpec((8, LRU_W), lambda n: (jnp.maximum(jmap(n) * (TB // 8) - 1, 0), 0))]
                  + _lru_param_specs() + [pl.BlockSpec(memory_space=pl.ANY)]),
        out_specs=[pl.BlockSpec((TB, 1024), lambda n: (jmap(n), 4)),
                   pl.BlockSpec((8, LRU_W), lambda n: (0, 0)),
                   pl.BlockSpec((8, LRU_W), lambda n: (0, 0)),
                   pl.BlockSpec((LRU_W, LRU_W), lambda n: (0, 0)),
                   pl.BlockSpec((LRU_W, LRU_W), lambda n: (0, 0))],
        out_shape=[jax.ShapeDtypeStruct((lp, IN_TOTAL), BF16),
                   jax.ShapeDtypeStruct((8, LRU_W), F32),
                   jax.ShapeDtypeStruct((8, LRU_W), F32),
                   jax.ShapeDtypeStruct((LRU_W, LRU_W), F32),
                   jax.ShapeDtypeStruct((LRU_W, LRU_W), F32)],
        input_output_aliases={10: 0},
        scratch_shapes=[pltpu.VMEM((8, LRU_W), F32), pltpu.VMEM((8, LRU_W), F32), pltpu.VMEM((8, LRU_W), F32)],
        compiler_params=_cp(1),
    )(proj, proj, proj, dycat, hstate, hstate, wl, vec, wa, wx, dproj)


_HBM = pl.BlockSpec(memory_space=pltpu.HBM)
_SEM = pl.BlockSpec(memory_space=pltpu.SEMAPHORE)
_ANY = pl.BlockSpec(memory_space=pl.ANY)
_EFFECT = pltpu.SideEffectType.DATAFLOW_SIDE_EFFECTING


def _hbm(a):
    return pltpu.with_memory_space_constraint(a, pltpu.HBM)


def _split_descriptors(copies, srcs, lands, send_sems, recv_sems):
    x, y, c = lax.axis_index("x"), lax.axis_index("y"), lax.axis_index("c")
    me = 4 * x + 2 * y + c
    out = []
    for q, (si, scatter, li, ll) in enumerate(copies):
        for k in range(1, N_DEV):
            px = 1 - x if k & 4 else x
            py = 1 - y if k & 2 else y
            pc = 1 - c if k & 1 else c
            peer = 4 * px + 2 * py + pc
            dst = lands[li].at[me] if ll is None else lands[li].at[me, ll]
            out.append(pltpu.make_async_remote_copy(
                src_ref=srcs[si].at[peer] if scatter else srcs[si], dst_ref=dst,
                send_sem=send_sems.at[q * 7 + k - 1], recv_sem=recv_sems.at[q * 7 + k - 1],
                device_id=(px, py, pc), device_id_type=pl.DeviceIdType.MESH))
    return out


def _xchg_start(name, groups):
    n_src = [len(g[0]) for g in groups]
    n_land = [len(g[1]) for g in groups]
    srcs = [s for g in groups for s in g[0]]
    lands = [l for g in groups for l in g[1]]
    ns, nl, ng = len(srcs), len(lands), len(groups)

    def body(*refs):
        src_refs, land_refs = refs[:ns], refs[ns:ns + nl]
        sems = refs[ns + nl:ns + nl + 2 * ng]
        token = refs[-1]
        so = lo = 0
        for gi, (_, _, copies) in enumerate(groups):
            for d in _split_descriptors(copies, src_refs[so:so + n_src[gi]], land_refs[lo:lo + n_land[gi]],
                                        sems[2 * gi], sems[2 * gi + 1]):
                d.start()
            so += n_src[gi]
            lo += n_land[gi]
        token[...] = jnp.zeros(token.shape, F32)

    out_shape, out_specs = [], []
    for g in groups:
        n = 7 * len(g[2])
        out_shape += [pltpu.SemaphoreType.DMA((n,)), pltpu.SemaphoreType.DMA((n,))]
        out_specs += [_SEM, _SEM]
    out_shape += [pltpu.HBM(l.shape, l.dtype) for l in lands]
    out_specs += [_HBM] * nl
    out_shape.append(jax.ShapeDtypeStruct((8, 128), F32))
    out_specs.append(pl.BlockSpec(memory_space=pltpu.VMEM))
    outs = pl.pallas_call(
        body, name=name, in_specs=[_HBM] * (ns + nl), out_specs=out_specs, out_shape=out_shape,
        input_output_aliases={ns + i: 2 * ng + i for i in range(nl)},
        compiler_params=pltpu.CompilerParams(has_side_effects=_EFFECT),
    )(*[_hbm(a) for a in srcs + lands])
    res, lo = [], 2 * ng
    for gi in range(ng):
        res.append((outs[2 * gi], outs[2 * gi + 1], list(outs[lo:lo + n_land[gi]])))
        lo += n_land[gi]
    return res, outs[-1]


def _xchg_wait(name, group, started, after):
    srcs, _, copies = group
    send_sems, recv_sems, lands = started
    ns, nl = len(srcs), len(lands)
    after = list(after)

    def body(*refs):
        src_refs, land_refs = refs[:ns], refs[ns:ns + nl]
        send_ref, recv_ref = refs[ns + nl], refs[ns + nl + 1]
        for d in _split_descriptors(copies, src_refs, land_refs, send_ref, recv_ref):
            d.wait_send()
            d.wait_recv()

    outs = pl.pallas_call(
        body, name=name, in_specs=[_HBM] * (ns + nl) + [_SEM, _SEM] + [_ANY] * len(after),
        out_specs=[_HBM] * nl, out_shape=[pltpu.HBM(l.shape, l.dtype) for l in lands],
        input_output_aliases={ns + i: i for i in range(nl)},
        compiler_params=pltpu.CompilerParams(has_side_effects=_EFFECT),
    )(*[_hbm(a) for a in srcs], *lands, send_sems, recv_sems, *after)
    return list(outs)


def _landing(own, me):
    land = lax.empty((N_DEV,) + own.shape, own.dtype)
    return lax.dynamic_update_slice(land, own[None], (me,) + (0,) * own.ndim)


def _adamw(name, w, m, v, recv, row0=0, prev=None):
    cdim = w.shape[1]
    r = recv.shape[1]
    tr = r
    for cand in (512, 256, 128, 64, 32, 16, 8):
        if r % cand == 0 and r > cand:
            tr = cand
            break
    assert row0 % tr == 0
    blk0 = row0 // tr
    n_prev = 0 if prev is None else 4

    def body(w_ref, m_ref, v_ref, r_ref, *rest):
        g_ref, d_ref, mo_ref, vo_ref = rest[n_prev:]
        g = r_ref[0].astype(F32)
        for s in range(1, N_DEV):
            g = g + r_ref[s].astype(F32)
        mn = ADAM_B1 * m_ref[...] + (1.0 - ADAM_B1) * g
        vn = ADAM_B2 * v_ref[...] + (1.0 - ADAM_B2) * (g * g)
        m_hat = mn / (1.0 - ADAM_B1 ** ADAM_STEP)
        v_hat = vn / (1.0 - ADAM_B2 ** ADAM_STEP)
        g_ref[...] = g
        d_ref[...] = -ADAM_LR * (m_hat / (jnp.sqrt(v_hat) + ADAM_EPS) + ADAM_WD * w_ref[...])
        mo_ref[...] = mn
        vo_ref[...] = vn

    blk = pl.BlockSpec((tr, cdim), lambda i: (i + blk0, 0))
    return pl.pallas_call(
        body, name=name, grid=(r // tr,),
        in_specs=[blk, blk, blk, pl.BlockSpec((N_DEV, tr, cdim), lambda i: (0, i, 0))] + [_ANY] * n_prev,
        out_specs=[blk, blk, blk, blk],
        out_shape=[jax.ShapeDtypeStruct(w.shape, F32)] * 4,
        input_output_aliases={4 + i: i for i in range(n_prev)},
        compiler_params=_cp(1),
    )(w, m, v, recv, *(prev or []))


def _pack_rows(arrs, lead=()):
    n = len(lead)
    flat = jnp.concatenate([a.reshape(a.shape[:n] + (-1,)) for a in arrs], axis=-1)
    size = flat.shape[-1]
    padded = -(-size // PACK_QUANTUM) * PACK_QUANTUM
    flat = jnp.pad(flat, [(0, 0)] * n + [(0, padded - size)])
    return flat.reshape(flat.shape[:n] + (padded // 128, 128))


def _unpack_rows(packed, shapes, lead=()):
    n = len(lead)
    flat = packed.reshape(packed.shape[:n] + (-1,))
    out, off = [], 0
    for s in shapes:
        size = int(np.prod(s))
        out.append(flat[..., off:off + size].reshape(packed.shape[:n] + tuple(s)))
        off += size
    return out


def _block_diag(w):
    eye = jnp.eye(LRU_HEADS, dtype=w.dtype)
    return (eye[:, None, :, None] * w[:, :, None, :]).reshape(LRU_W, LRU_W)


def _diag_blocks(dense):
    t = dense.reshape(LRU_HEADS, 64, LRU_HEADS, 64)
    return jnp.stack([t[h, :, h, :] for h in range(LRU_HEADS)], axis=0)


def _cols_to_slots(full):
    lead = full.shape[:-1]
    t = full.reshape(lead + (N_DEV, full.shape[-1] // N_DEV))
    return jnp.moveaxis(t, -2, 0)


def _slots_to_cols(slots):
    t = jnp.moveaxis(slots, 0, -2)
    return t.reshape(t.shape[:-2] + (t.shape[-2] * t.shape[-1],))


def kernel(x, meta_tokens, ln_in_g, ln_in_b, w_in, conv_dw_w, conv_dw_b, conv_ln_g, conv_ln_b, conv_pw_w, conv_pw_b, attn_sinks, lru_conv_w, lru_conv_b, lru_wa, lru_ba, lru_wx, lru_bx, lru_lambda, w_out, ln_post_g, ln_post_b, loss_target, m_meta_tokens, m_ln_in_g, m_ln_in_b, m_w_in, m_conv_dw_w, m_conv_dw_b, m_conv_ln_g, m_conv_ln_b, m_conv_pw_w, m_conv_pw_b, m_attn_sinks, m_lru_conv_w, m_lru_conv_b, m_lru_wa, m_lru_ba, m_lru_wx, m_lru_bx, m_lru_lambda, m_w_out, m_ln_post_g, m_ln_post_b, v_meta_tokens, v_ln_in_g, v_ln_in_b, v_w_in, v_conv_dw_w, v_conv_dw_b, v_conv_ln_g, v_conv_ln_b, v_conv_pw_w, v_conv_pw_b, v_attn_sinks, v_lru_conv_w, v_lru_conv_b, v_lru_wa, v_lru_ba, v_lru_wx, v_lru_bx, v_lru_lambda, v_w_out, v_ln_post_g, v_ln_post_b):
    seq = x.shape[1]
    lp = seq + TB
    row = lambda a: a.reshape(1, -1)
    rep_names = ["ln_in_g", "ln_in_b", "conv_dw_b", "conv_ln_g", "conv_ln_b", "conv_pw_b", "attn_sinks",
                 "lru_conv_b", "lru_wa", "lru_ba", "lru_wx", "lru_bx", "lru_lambda", "ln_post_g", "ln_post_b"]
    shard_small_names = ["conv_dw_w", "lru_conv_w", "meta_tokens"]
    weights = dict(meta_tokens=meta_tokens, ln_in_g=ln_in_g, ln_in_b=ln_in_b, w_in=w_in, conv_dw_w=conv_dw_w,
                   conv_dw_b=conv_dw_b, conv_ln_g=conv_ln_g, conv_ln_b=conv_ln_b, conv_pw_w=conv_pw_w,
                   conv_pw_b=conv_pw_b, attn_sinks=attn_sinks, lru_conv_w=lru_conv_w, lru_conv_b=lru_conv_b,
                   lru_wa=lru_wa, lru_ba=lru_ba, lru_wx=lru_wx, lru_bx=lru_bx, lru_lambda=lru_lambda,
                   w_out=w_out, ln_post_g=ln_post_g, ln_post_b=ln_post_b)
    mom1 = dict(meta_tokens=m_meta_tokens, ln_in_g=m_ln_in_g, ln_in_b=m_ln_in_b, w_in=m_w_in, conv_dw_w=m_conv_dw_w,
                conv_dw_b=m_conv_dw_b, conv_ln_g=m_conv_ln_g, conv_ln_b=m_conv_ln_b, conv_pw_w=m_conv_pw_w,
                conv_pw_b=m_conv_pw_b, attn_sinks=m_attn_sinks, lru_conv_w=m_lru_conv_w, lru_conv_b=m_lru_conv_b,
                lru_wa=m_lru_wa, lru_ba=m_lru_ba, lru_wx=m_lru_wx, lru_bx=m_lru_bx, lru_lambda=m_lru_lambda,
                w_out=m_w_out, ln_post_g=m_ln_post_g, ln_post_b=m_ln_post_b)
    mom2 = dict(meta_tokens=v_meta_tokens, ln_in_g=v_ln_in_g, ln_in_b=v_ln_in_b, w_in=v_w_in, conv_dw_w=v_conv_dw_w,
                conv_dw_b=v_conv_dw_b, conv_ln_g=v_conv_ln_g, conv_ln_b=v_conv_ln_b, conv_pw_w=v_conv_pw_w,
                conv_pw_b=v_conv_pw_b, attn_sinks=v_attn_sinks, lru_conv_w=v_lru_conv_w, lru_conv_b=v_lru_conv_b,
                lru_wa=v_lru_wa, lru_ba=v_lru_ba, lru_wx=v_lru_wx, lru_bx=v_lru_bx, lru_lambda=v_lru_lambda,
                w_out=v_w_out, ln_post_g=v_ln_post_g, ln_post_b=v_ln_post_b)
    shard_wmv = [_pack_rows([d[n] for n in shard_small_names]) for d in (weights, mom1, mom2)]
    rep_wmv = [_pack_rows([d[n] for n in rep_names]) for d in (weights, mom1, mom2)]
    gate_w = [(_block_diag(lru_wa[l]).astype(BF16), _block_diag(lru_wx[l]).astype(BF16)) for l in range(DEPTH)]
    tabs = _rope_tables(lp)
    prepared = shard_wmv + rep_wmv + [w for pair in gate_w for w in pair] + list(tabs)

    small_shard_shapes = [conv_dw_w.shape, lru_conv_w.shape, meta_tokens.shape]
    small_shard = _pack_rows([conv_dw_w, lru_conv_w, meta_tokens])
    me = 4 * lax.axis_index("x") + 2 * lax.axis_index("y") + lax.axis_index("c")
    w_in_b = [w_in[l].astype(BF16) for l in range(DEPTH)]
    w_out_b = [w_out[l].astype(BF16) for l in range(DEPTH)]
    pw_b = conv_pw_w.astype(BF16)
    wgroups = [
        ([small_shard], [_landing(small_shard, me)], [(0, False, 0, None)]),
        ([w_in_b[0]], [_landing(w_in_b[0], me)], [(0, False, 0, None)]),
        ([pw_b, w_out_b[0]], [_landing(pw_b, me), _landing(w_out_b[0], me)],
         [(0, False, 0, None), (1, False, 1, None)]),
        ([w_in_b[1], w_out_b[1]], [_landing(w_in_b[1], me), _landing(w_out_b[1], me)],
         [(0, False, 0, None), (1, False, 1, None)]),
    ]
    wstarted, wtoken = _xchg_start("weights_start", wgroups)
    wg_small, = _xchg_wait("weights_wait_s", wgroups[0], wstarted[0], [wtoken])
    g_dw, g_lc, g_meta = _unpack_rows(wg_small, small_shard_shapes, lead=(N_DEV,))
    conv_dw_full = _slots_to_cols(g_dw)
    lru_conv_full = _slots_to_cols(g_lc)
    meta_full = _slots_to_cols(g_meta)
    wg_in = [None, None]
    wg_out = [None, None]
    wg_pw = None

    ln_g = [ln_in_g, ln_post_g[0], ln_post_g[1]]
    ln_b = [ln_in_b, ln_post_b[0], ln_post_b[1]]

    def layer_params(l):
        wdw = jnp.pad(conv_dw_full[l], ((0, 1), (0, 0)))
        cvec = jnp.pad(jnp.stack([conv_dw_b[l], conv_ln_g[l], conv_ln_b[l], conv_pw_b[l]]), ((0, 4), (0, 0)))
        wpw = wg_pw[:, l].reshape(CONV_W, CONV_W)
        sinks = jnp.pad(attn_sinks[l].reshape(1, N_HEADS), ((0, 7), (0, 128 - N_HEADS)))
        wl = jnp.pad(lru_conv_full[l], ((0, 4), (0, 0)))
        lvec = jnp.pad(jnp.stack([lru_conv_b[l], lru_ba[l], lru_bx[l], lru_lambda[l]]), ((0, 4), (0, 0)))
        wa, wx = gate_w[l]
        wo = wg_out[l].reshape(D_MODEL, D_MODEL)
        wout = jnp.concatenate([wo[512:1536], wo[0:512], wo[1536:]], axis=0)
        return dict(wdw=wdw, cvec=cvec, wpw=wpw, sinks=sinks, wl=wl, lvec=lvec, wa=wa, wx=wx, wout=wout)

    params = [None] * DEPTH

    z0, hb = _embed(x, meta_full, row(ln_g[0]), row(ln_b[0]))
    z = [z0]
    saved = []
    for l in range(DEPTH):
        if l == 0:
            wg_in[0], = _xchg_wait("weights_wait_a", wgroups[1], wstarted[1], [hb] + prepared)
        else:
            wg_in[1], wg_out[1] = _xchg_wait("weights_wait_c", wgroups[3], wstarted[3], [hb])
        proj = _mm_proj(f"proj{l}", hb, wg_in[l])
        if l == 0:
            wg_pw, wg_out[0] = _xchg_wait("weights_wait_b", wgroups[2], wstarted[2], [proj])
        p = params[l] = layer_params(l)
        ycat = _conv_fwd(f"conv_fwd{l}", proj, p["wdw"], p["cvec"], p["wpw"])
        qr, kr = _rope_fwd(f"rope{l}", proj, tabs)
        ycat = _attn_fwd(f"attn_fwd{l}", qr, kr, proj, p["sinks"], ycat)
        ycat, hstate = _lru_fwd(f"lru_fwd{l}", proj, p["wl"], p["lvec"], p["wa"], p["wx"], ycat)
        saved.append(dict(hb=hb, proj=proj, ycat=ycat, qr=qr, kr=kr, hstate=hstate))
        z_next, hb = _mm_out(f"out{l}", ycat, p["wout"], z[l], row(ln_g[l]), row(ln_b[l]),
                             row(ln_g[l + 1]), row(ln_b[l + 1]))
        z.append(z_next)

    dz, st_post1, loss_blk = _loss_head(z[DEPTH], loss_target, row(ln_g[DEPTH]), row(ln_b[DEPTH]))
    loss = lax.psum(loss_blk[0, 0], ("x", "y", "c"))

    ln_stats = {DEPTH: st_post1}
    g_layers = [None] * DEPTH
    dwin_l, dwout_l = [None] * DEPTH, [None] * DEPTH
    grad_x = gmeta = None
    token = wtoken
    ggroups = [None] * DEPTH
    own = lambda a: lax.dynamic_index_in_dim(a, me, 0, keepdims=False)
    for l in reversed(range(DEPTH)):
        p, s = params[l], saved[l]
        dycat = _mm_dycat(f"dycat{l}", dz, p["wout"], token)
        dwout_l[l] = _mm_dwout(f"dwout{l}", s["ycat"], dz)
        dproj, dwdw, dcvec, dwpw = _conv_bwd(f"conv_bwd{l}", s["proj"], dycat, p["wdw"], p["cvec"], p["wpw"])
        dwo = jnp.concatenate([dwout_l[l][1024:1536], dwout_l[l][0:1024], dwout_l[l][1536:]], axis=0)
        dwo = dwo.reshape(N_DEV, D_MODEL // N_DEV, D_MODEL)
        dpw = dwpw.reshape(N_DEV, CONV_W // N_DEV, CONV_W)
        early = ([dwo, dpw], [_landing(own(dwo), me), _landing(own(dpw), me)],
                 [(0, True, 0, None), (1, True, 1, None)])
        started_early, token = _xchg_start(f"grads_start_out{l}", [early])
        dq, dgate, dk, dv, dsink = _attn_bwd(f"attn_bwd{l}", s["qr"], s["kr"], s["proj"], p["sinks"], dycat, token)
        dproj = _attn_assemble(f"attn_asm{l}", dq, dgate, dk, dv, tabs, dproj)
        dproj, dwl, dlvec, dwa, dwx = _lru_bwd(f"lru_bwd{l}", s["proj"], dycat, s["hstate"],
                                                p["wl"], p["lvec"], p["wa"], p["wx"], dproj)
        dwin_l[l] = _mm_dwin(f"dwin{l}", s["hb"], dproj)
        late = ([dwin_l[l]], [_landing(own(dwin_l[l]), me)], [(0, True, 0, None)])
        started_late, token = _xchg_start(f"grads_start_in{l}", [late])
        ggroups[l] = [(late, started_late[0]), (early, started_early[0])]
        dh = _mm_dh(f"dh{l}", dproj, wg_in[l], dz, token)
        if l > 0:
            dz, ln_stats[l] = _ln_bwd(f"ln_bwd{l}", dh, z[l], row(ln_g[l]))
        else:
            grad_x, gmeta, ln_stats[0] = _ln_bwd_input(dh, z[0], row(ln_g[0]))
        g_layers[l] = dict(dwdw=dwdw[:CONV_K], dcvec=dcvec, dwpw=dwpw, dsink=dsink[0, :N_HEADS],
                           dwl=dwl[:LRU_CONV_K], dlvec=dlvec, dwa=_diag_blocks(dwa), dwx=_diag_blocks(dwx))

    stack = lambda f: jnp.stack([f(g_layers[l]) for l in range(DEPTH)])
    g_local = dict(
        ln_in_g=ln_stats[0][0], ln_in_b=ln_stats[0][1],
        conv_dw_b=stack(lambda g: g["dcvec"][0]), conv_ln_g=stack(lambda g: g["dcvec"][1]),
        conv_ln_b=stack(lambda g: g["dcvec"][2]), conv_pw_b=stack(lambda g: g["dcvec"][3]),
        attn_sinks=stack(lambda g: g["dsink"]),
        lru_conv_b=stack(lambda g: g["dlvec"][0]), lru_wa=stack(lambda g: g["dwa"]),
        lru_ba=stack(lambda g: g["dlvec"][1]), lru_wx=stack(lambda g: g["dwx"]),
        lru_bx=stack(lambda g: g["dlvec"][2]), lru_lambda=stack(lambda g: g["dlvec"][3]),
        ln_post_g=jnp.stack([ln_stats[1][0], ln_stats[2][0]]),
        ln_post_b=jnp.stack([ln_stats[1][1], ln_stats[2][1]]),
    )
    rep_pack = _pack_rows([g_local[n] for n in rep_names])
    g_dw_full = jnp.stack([g_layers[l]["dwdw"] for l in range(DEPTH)])
    g_lc_full = jnp.stack([g_layers[l]["dwl"] for l in range(DEPTH)])
    shard_pack = _pack_rows([_cols_to_slots(g_dw_full), _cols_to_slots(g_lc_full), _cols_to_slots(gmeta)],
                            lead=(N_DEV,))
    sgroup = ([shard_pack, rep_pack], [_landing(own(shard_pack), me), _landing(rep_pack, me)],
              [(0, True, 0, None), (1, False, 1, None)])
    sstarted, token = _xchg_start("small_grads_start", [sgroup])

    res = {}

    def flat2(a, cols):
        return a.reshape(-1, cols)

    big = (("w_in", 0, W_IN_SHARD), ("w_out", 1, D_MODEL), ("conv_pw_w", 2, CONV_W))
    prev = {n: None for n, _, _ in big}
    after = token
    for l in reversed(range(DEPTH)):
        recvs = []
        for gi, (grp, started) in enumerate(ggroups[l]):
            recvs += _xchg_wait(f"grads_wait{l}_{gi}", grp, started, [after])
        for name_, gi, cols in big:
            w_ = weights[name_]
            rows = w_.shape[1]
            prev[name_] = _adamw(f"adamw_{name_}{l}", flat2(w_, cols), flat2(mom1[name_], cols),
                                 flat2(mom2[name_], cols), recvs[gi], row0=l * rows, prev=prev[name_])
        after = prev["w_in"][0]
    for name_, _, _ in big:
        res[name_] = [o.reshape(weights[name_].shape) for o in prev[name_]]

    r_small, r_rep = _xchg_wait("small_grads_wait", sgroup, sstarted[0], [after])
    sshapes = [weights[n].shape for n in shard_small_names]
    outs = _adamw("adamw_small_sharded", *shard_wmv, r_small)
    for k, o in enumerate(outs):
        for n, a in zip(shard_small_names, _unpack_rows(o, sshapes)):
            res.setdefault(n, [None] * 4)[k] = a

    rshapes = [weights[n].shape for n in rep_names]
    outs = _adamw("adamw_replicated", *rep_wmv, r_rep)
    for k, o in enumerate(outs):
        for n, a in zip(rep_names, _unpack_rows(o, rshapes)):
            res.setdefault(n, [None] * 4)[k] = a

    order = ["meta_tokens", "ln_in_g", "ln_in_b", "w_in", "conv_dw_w", "conv_dw_b", "conv_ln_g", "conv_ln_b",
             "conv_pw_w", "conv_pw_b", "attn_sinks", "lru_conv_w", "lru_conv_b", "lru_wa", "lru_ba", "lru_wx",
             "lru_bx", "lru_lambda", "w_out", "ln_post_g", "ln_post_b"]
    return (loss, grad_x,
            *[res[n][0] for n in order], *[res[n][1] for n in order],
            *[res[n][2] for n in order], *[res[n][3] for n in order])
```

```python
import functools
import math

import numpy as np
import jax
import jax.numpy as jnp
from jax import lax
from jax.experimental import pallas as pl
from jax.experimental.pallas import tpu as pltpu

F32 = jnp.float32
BF16 = jnp.bfloat16

D_MODEL = 2048
DEPTH = 2
N_META = 16
TB = 128
PAD0 = TB - N_META
CONV_W = 512
CONV_K = 31
HEAD_DIM = 64
N_HEADS = 16
N_KV = 4
GROUP = 4
ATT_W = 1024
KV_W = 256
ROT_DIM = 16
ROPE_THETA = 500000.0
LRU_W = 512
LRU_HEADS = 8
LRU_CONV_K = 4
LRU_C = 8.0
IN_TOTAL = 5120
N_DEV = 8
W_IN_SHARD = IN_TOTAL // N_DEV
LN_EPS = 1e-5
ALPHA = (2.0 * DEPTH) ** 0.25
NEG_INF = -1e30
ATT_SCALE = HEAD_DIM ** -0.5

ADAM_LR = 0.001
ADAM_B1 = 0.9
ADAM_B2 = 0.999
ADAM_EPS = 1e-08
ADAM_WD = 0.01
ADAM_STEP = 10

VMEM_LIMIT = 56 * 1024 * 1024
PACK_QUANTUM = 256 * 128

COL_CV, COL_CG, COL_CGATE = 0, 1, 2
COL_Q0 = 3
COL_K256 = 10
COL_V256 = 11
COL_AGATE1024 = 3
COL_RX, COL_RGATE = 8, 9
YC_CONV, YC_LRU = 2, 3


def _cp(n_axes, vmem=VMEM_LIMIT):
    return pltpu.CompilerParams(dimension_semantics=("arbitrary",) * n_axes, vmem_limit_bytes=vmem)


def _row_tile(lp, max_blocks):
    nb = lp // TB
    d = max(k for k in range(1, max_blocks + 1) if nb % k == 0)
    return TB * d


def _sig(x):
    return jax.nn.sigmoid(x)


def _dsilu(x, s):
    return s * (1.0 + x * (1.0 - s))


def _ln_core(z):
    mu = jnp.mean(z, axis=-1, keepdims=True)
    zc = z - mu
    var = jnp.mean(zc * zc, axis=-1, keepdims=True)
    rstd = lax.rsqrt(var + LN_EPS)
    return zc * rstd, rstd


def _ln_bwd_core(dy, xh, rstd, g):
    dxh = dy * g
    m1 = jnp.mean(dxh, axis=-1, keepdims=True)
    m2 = jnp.mean(dxh * xh, axis=-1, keepdims=True)
    return rstd * (dxh - m1 - xh * m2)


def _row_ids(shape, base):
    return lax.broadcasted_iota(jnp.int32, shape, 0) + base


def _colsum(x):
    return jnp.sum(x, axis=0, keepdims=True)


def _dot(a, b, dims):
    return lax.dot_general(a, b, (dims, ((), ())), preferred_element_type=F32)


NN = ((1,), (0,))
NT = ((1,), (1,))
TN = ((0,), (0,))


def _embed(x, meta_full, g, b):
    s = x.shape[1]
    lp = s + TB
    nb = lp // TB

    def body(x_ref, m_ref, g_ref, b_ref, o_ref, hb_ref):
        i = pl.program_id(0)

        @pl.when(i == 0)
        def _():
            o_ref[0:PAD0, :] = jnp.zeros((PAD0, D_MODEL), F32)
            o_ref[PAD0:TB, :] = m_ref[...]

        @pl.when(i > 0)
        def _():
            o_ref[...] = x_ref[...]

        xh, _ = _ln_core(o_ref[...])
        h = xh * g_ref[...] + b_ref[...]
        rows = _row_ids(h.shape, i * TB)
        hb_ref[...] = jnp.where(rows >= PAD0, h, 0.0).astype(BF16)

    return pl.pallas_call(
        body, name="embed", grid=(nb,),
        in_specs=[pl.BlockSpec((None, TB, D_MODEL), lambda i: (0, jnp.maximum(i - 1, 0), 0)),
                  pl.BlockSpec((N_META, D_MODEL), lambda i: (0, 0)),
                  pl.BlockSpec((1, D_MODEL), lambda i: (0, 0)),
                  pl.BlockSpec((1, D_MODEL), lambda i: (0, 0))],
        out_specs=[pl.BlockSpec((TB, D_MODEL), lambda i: (i, 0)),
                   pl.BlockSpec((TB, D_MODEL), lambda i: (i, 0))],
        out_shape=[jax.ShapeDtypeStruct((lp, D_MODEL), F32),
                   jax.ShapeDtypeStruct((lp, D_MODEL), BF16)],
        compiler_params=_cp(1),
    )(x, meta_full, g, b)


def _loss_head(z, target, g, b):
    lp = z.shape[0]
    nb = lp // TB

    def body(z_ref, t_ref, g_ref, b_ref, dz_ref, st_ref, loss_ref):
        i = pl.program_id(0)

        @pl.when(i == 0)
        def _():
            st_ref[...] = jnp.zeros(st_ref.shape, F32)
            loss_ref[...] = jnp.zeros(loss_ref.shape, F32)
            dz_ref[...] = jnp.zeros(dz_ref.shape, F32)

        @pl.when(i > 0)
        def _():
            xh, rstd = _ln_core(z_ref[...])
            gg = g_ref[...]
            y = xh * gg + b_ref[...]
            e = y - t_ref[...]
            part = 0.5 * jnp.sum(jnp.mean(e * e, axis=-1, keepdims=True), axis=0, keepdims=True)
            loss_ref[...] += jnp.broadcast_to(part, loss_ref.shape)
            dy = e / float(D_MODEL)
            st_ref[0:1, :] += _colsum(dy * xh)
            st_ref[1:2, :] += _colsum(dy)
            dz_ref[...] = _ln_bwd_core(dy, xh, rstd, gg)

    return pl.pallas_call(
        body, name="loss_head", grid=(nb,),
        in_specs=[pl.BlockSpec((TB, D_MODEL), lambda i: (i, 0)),
                  pl.BlockSpec((None, TB, D_MODEL), lambda i: (0, jnp.maximum(i - 1, 0), 0)),
                  pl.BlockSpec((1, D_MODEL), lambda i: (0, 0)),
                  pl.BlockSpec((1, D_MODEL), lambda i: (0, 0))],
        out_specs=[pl.BlockSpec((TB, D_MODEL), lambda i: (i, 0)),
                   pl.BlockSpec((8, D_MODEL), lambda i: (0, 0)),
                   pl.BlockSpec((8, 128), lambda i: (0, 0))],
        out_shape=[jax.ShapeDtypeStruct((lp, D_MODEL), F32),
                   jax.ShapeDtypeStruct((8, D_MODEL), F32),
                   jax.ShapeDtypeStruct((8, 128), F32)],
        compiler_params=_cp(1),
    )(z, target, g, b)


def _ln_bwd(name, dh, z, g):
    lp = z.shape[0]
    nb = lp // TB

    def body(dh_ref, z_ref, g_ref, dz_ref, st_ref):
        i = pl.program_id(0)

        @pl.when(i == 0)
        def _():
            st_ref[...] = jnp.zeros(st_ref.shape, F32)

        xh, rstd = _ln_core(z_ref[...])
        rows = _row_ids(xh.shape, i * TB)
        dy = jnp.where(rows >= PAD0, dh_ref[...], 0.0)
        st_ref[0:1, :] += _colsum(dy * xh)
        st_ref[1:2, :] += _colsum(dy)
        dz_ref[...] = _ln_bwd_core(dy, xh, rstd, g_ref[...])

    return pl.pallas_call(
        body, name=name, grid=(nb,),
        in_specs=[pl.BlockSpec((TB, D_MODEL), lambda i: (i, 0)),
                  pl.BlockSpec((TB, D_MODEL), lambda i: (i, 0)),
                  pl.BlockSpec((1, D_MODEL), lambda i: (0, 0))],
        out_specs=[pl.BlockSpec((TB, D_MODEL), lambda i: (i, 0)),
                   pl.BlockSpec((8, D_MODEL), lambda i: (0, 0))],
        out_shape=[jax.ShapeDtypeStruct((lp, D_MODEL), F32),
                   jax.ShapeDtypeStruct((8, D_MODEL), F32)],
        compiler_params=_cp(1),
    )(dh, z, g)


def _ln_bwd_input(dh, z, g):
    lp = z.shape[0]
    nb = lp // TB
    s = lp - TB

    def body(dh_ref, z_ref, g_ref, gx_ref, gm_ref, st_ref):
        i = pl.program_id(0)

        @pl.when(i == 0)
        def _():
            st_ref[...] = jnp.zeros(st_ref.shape, F32)

        xh, rstd = _ln_core(z_ref[...])
        rows = _row_ids(xh.shape, i * TB)
        dy = jnp.where(rows >= PAD0, dh_ref[...], 0.0)
        st_ref[0:1, :] += _colsum(dy * xh)
        st_ref[1:2, :] += _colsum(dy)
        dz = _ln_bwd_core(dy, xh, rstd, g_ref[...])
        gx_ref[...] = dz

        @pl.when(i == 0)
        def _():
            gm_ref[...] = dz[PAD0:TB, :]

    return pl.pallas_call(
        body, name="ln_in_bwd", grid=(nb,),
        in_specs=[pl.BlockSpec((TB, D_MODEL), lambda i: (i, 0)),
                  pl.BlockSpec((TB, D_MODEL), lambda i: (i, 0)),
                  pl.BlockSpec((1, D_MODEL), lambda i: (0, 0))],
        out_specs=[pl.BlockSpec((None, TB, D_MODEL), lambda i: (0, jnp.maximum(i - 1, 0), 0)),
                   pl.BlockSpec((N_META, D_MODEL), lambda i: (0, 0)),
                   pl.BlockSpec((8, D_MODEL), lambda i: (0, 0))],
        out_shape=[jax.ShapeDtypeStruct((1, s, D_MODEL), F32),
                   jax.ShapeDtypeStruct((N_META, D_MODEL), F32),
                   jax.ShapeDtypeStruct((8, D_MODEL), F32)],
        compiler_params=_cp(1),
    )(dh, z, g)


def _mm_proj(name, hb, wg_in):
    lp = hb.shape[0]
    tm = lp // 3

    def body(a_ref, b_ref, o_ref):
        b = jnp.concatenate([b_ref[0], b_ref[1]], axis=1)
        o_ref[...] = _dot(a_ref[...], b, NN)

    return pl.pallas_call(
        body, name=name, grid=(3, N_DEV // 2),
        in_specs=[pl.BlockSpec((tm, D_MODEL), lambda i, j: (i, 0)),
                  pl.BlockSpec((2, D_MODEL, W_IN_SHARD), lambda i, j: (j, 0, 0))],
        out_specs=pl.BlockSpec((tm, 2 * W_IN_SHARD), lambda i, j: (i, j)),
        out_shape=jax.ShapeDtypeStruct((lp, IN_TOTAL), F32),
        compiler_params=_cp(2),
    )(hb, wg_in)


def _mm_out(name, ycat, wout, z, g, b, g2, b2):
    lp = ycat.shape[0]
    tm = lp // 6

    def body(a_ref, w_ref, z_ref, g_ref, b_ref, g2_ref, b2_ref, o_ref, hb_ref):
        i = pl.program_id(0)
        xh, _ = _ln_core(z_ref[...])
        h = xh * g_ref[...] + b_ref[...]
        live = _row_ids(h.shape, i * tm) >= PAD0
        h = jnp.where(live, h, 0.0)
        zn = ALPHA * h + _dot(a_ref[...], w_ref[...], NN)
        o_ref[...] = zn
        xh2, _ = _ln_core(zn)
        hb_ref[...] = jnp.where(live, xh2 * g2_ref[...] + b2_ref[...], 0.0).astype(BF16)

    vec = pl.BlockSpec((1, D_MODEL), lambda i: (0, 0))
    return pl.pallas_call(
        body, name=name, grid=(6,),
        in_specs=[pl.BlockSpec((tm, D_MODEL), lambda i: (i, 0)),
                  pl.BlockSpec((D_MODEL, D_MODEL), lambda i: (0, 0), pipeline_mode=pl.Buffered(1)),
                  pl.BlockSpec((tm, D_MODEL), lambda i: (i, 0)),
                  vec, vec, vec, vec],
        out_specs=[pl.BlockSpec((tm, D_MODEL), lambda i: (i, 0)),
                   pl.BlockSpec((tm, D_MODEL), lambda i: (i, 0))],
        out_shape=[jax.ShapeDtypeStruct((lp, D_MODEL), F32),
                   jax.ShapeDtypeStruct((lp, D_MODEL), BF16)],
        compiler_params=_cp(1),
    )(ycat, wout, z, g, b, g2, b2)


def _mm_dycat(name, dz, wout, dep):
    lp = dz.shape[0]
    tm = lp // 6

    def body(a_ref, w_ref, dep_ref, o_ref):
        del dep_ref
        o_ref[...] = _dot(a_ref[...].astype(BF16), w_ref[...], NT).astype(BF16)

    return pl.pallas_call(
        body, name=name, grid=(6,),
        in_specs=[pl.BlockSpec((tm, D_MODEL), lambda i: (i, 0)),
                  pl.BlockSpec((D_MODEL, D_MODEL), lambda i: (0, 0), pipeline_mode=pl.Buffered(1)),
                  pl.BlockSpec(memory_space=pl.ANY)],
        out_specs=pl.BlockSpec((tm, D_MODEL), lambda i: (i, 0)),
        out_shape=jax.ShapeDtypeStruct((lp, D_MODEL), BF16),
        compiler_params=_cp(1),
    )(dz, wout, dep)


def _mm_dwout(name, ycat, dz):
    lp = ycat.shape[0]
    tk = _row_tile(lp, 11)
    nk = lp // tk
    half = D_MODEL // 2

    def body(a_ref, b_ref, o_ref, acc_ref):
        k = pl.program_id(1)

        @pl.when(k == 0)
        def _():
            acc_ref[...] = jnp.zeros(acc_ref.shape, F32)

        acc_ref[...] += _dot(a_ref[...], b_ref[...].astype(BF16), TN)

        @pl.when(k == nk - 1)
        def _():
            o_ref[...] = acc_ref[...].astype(BF16)

    return pl.pallas_call(
        body, name=name, grid=(2, nk),
        in_specs=[pl.BlockSpec((tk, half), lambda h, k: (k, h)),
                  pl.BlockSpec((tk, D_MODEL), lambda h, k: (k, 0))],
        out_specs=pl.BlockSpec((half, D_MODEL), lambda h, k: (h, 0)),
        out_shape=jax.ShapeDtypeStruct((D_MODEL, D_MODEL), BF16),
        scratch_shapes=[pltpu.VMEM((half, D_MODEL), F32)],
        compiler_params=_cp(2),
    )(ycat, dz)


def _mm_dwin(name, hb, dproj):
    lp = hb.shape[0]
    tk = _row_tile(lp, 11)
    nk = lp // tk

    def body(a_ref, b_ref, o_ref, acc_ref):
        k = pl.program_id(1)

        @pl.when(k == 0)
        def _():
            acc_ref[...] = jnp.zeros(acc_ref.shape, F32)

        acc_ref[...] += _dot(a_ref[...], b_ref[...], TN)

        @pl.when(k == nk - 1)
        def _():
            o_ref[0] = acc_ref[:, 0:W_IN_SHARD].astype(BF16)
            o_ref[1] = acc_ref[:, W_IN_SHARD:2 * W_IN_SHARD].astype(BF16)

    return pl.pallas_call(
        body, name=name, grid=(4, nk),
        in_specs=[pl.BlockSpec((tk, D_MODEL), lambda j, k: (k, 0)),
                  pl.BlockSpec((tk, 2 * W_IN_SHARD), lambda j, k: (k, j))],
        out_specs=pl.BlockSpec((2, D_MODEL, W_IN_SHARD), lambda j, k: (j, 0, 0)),
        out_shape=jax.ShapeDtypeStruct((N_DEV, D_MODEL, W_IN_SHARD), BF16),
        scratch_shapes=[pltpu.VMEM((D_MODEL, 2 * W_IN_SHARD), F32)],
        compiler_params=_cp(2),
    )(hb, dproj)


def _mm_dh(name, dproj, wg_in, dz, dep):
    lp = dproj.shape[0]
    tm = lp // 6

    def body(a_ref, w_ref, dz_ref, dep_ref, o_ref, acc_ref):
        del dep_ref
        k = pl.program_id(1)

        @pl.when(k == 0)
        def _():
            acc_ref[...] = jnp.zeros(acc_ref.shape, F32)

        w = jnp.concatenate([w_ref[0], w_ref[1]], axis=1)
        acc_ref[...] += _dot(a_ref[...], w, NT)

        @pl.when(k == N_DEV // 2 - 1)
        def _():
            o_ref[...] = acc_ref[...] + ALPHA * dz_ref[...]

    return pl.pallas_call(
        body, name=name, grid=(6, N_DEV // 2),
        in_specs=[pl.BlockSpec((tm, 2 * W_IN_SHARD), lambda i, k: (i, k)),
                  pl.BlockSpec((2, D_MODEL, W_IN_SHARD), lambda i, k: (k, 0, 0)),
                  pl.BlockSpec((tm, D_MODEL), lambda i, k: (i, 0)),
                  pl.BlockSpec(memory_space=pl.ANY)],
        out_specs=pl.BlockSpec((tm, D_MODEL), lambda i, k: (i, 0)),
        out_shape=jax.ShapeDtypeStruct((lp, D_MODEL), F32),
        scratch_shapes=[pltpu.VMEM((tm, D_MODEL), F32)],
        compiler_params=_cp(2),
    )(dproj, wg_in, dz, dep)


def _shifted_views(cat, n_shift, base, rows):
    total = cat.shape[0]
    rolled = [cat] + [pltpu.roll(cat, b, axis=0) for b in range(1, 8)]
    views = []
    for s in range(n_shift):
        a, b = divmod(s, 8)
        views.append(rolled[b][base - 8 * a: base - 8 * a + rows, :])
    del total
    return views


def _conv_chain(j, cv_ref, cg_ref, cvp_ref, cgp_ref, wdw_ref, vec_ref, wpw_ref, c1_ref=None):
    cv = cv_ref[...]
    sg = _sig(cg_ref[...])
    c0 = cv * sg
    c0p = jnp.where(j > 0, cvp_ref[...] * _sig(cgp_ref[...]), 0.0)
    cat = jnp.concatenate([c0p, c0], axis=0)
    views = _shifted_views(cat, CONV_K, TB, TB)
    if c1_ref is None:
        c1 = jnp.broadcast_to(vec_ref[0:1, :], (TB, CONV_W))
        for k in range(CONV_K):
            c1 = c1 + wdw_ref[k:k + 1, :] * views[CONV_K - 1 - k]
    else:
        c1 = c1_ref[...]
    xh, rstd = _ln_core(c1)
    c2 = xh * vec_ref[1:2, :] + vec_ref[2:3, :]
    s2 = _sig(c2)
    c3 = c2 * s2
    c4 = _dot(c3.astype(BF16), wpw_ref[...], NN) + vec_ref[3:4, :]
    return dict(cv=cv, sg=sg, views=views, c1=c1, xh=xh, rstd=rstd, c2=c2, s2=s2, c3=c3, c4=c4)


def _conv_in_specs(jmap):
    def cur(col):
        return pl.BlockSpec((TB, 512), lambda n: (jmap(n), col))

    def prev(col):
        return pl.BlockSpec((TB, 512), lambda n: (jnp.maximum(jmap(n) - 1, 0), col))

    return [cur(COL_CV), cur(COL_CG), prev(COL_CV), prev(COL_CG), cur(COL_CGATE)]


def _conv_param_specs():
    return [pl.BlockSpec((32, CONV_W), lambda n: (0, 0)),
            pl.BlockSpec((8, CONV_W), lambda n: (0, 0)),
            pl.BlockSpec((CONV_W, CONV_W), lambda n: (0, 0))]


def _conv_fwd(name, proj, wdw, vec, wpw):
    lp = proj.shape[0]
    nb = lp // TB

    def body(cv_ref, cg_ref, cvp_ref, cgp_ref, gate_ref, wdw_ref, vec_ref, wpw_ref, o_ref, c1_ref):
        j = pl.program_id(0)
        c = _conv_chain(j, cv_ref, cg_ref, cvp_ref, cgp_ref, wdw_ref, vec_ref, wpw_ref)
        gate = gate_ref[...]
        o_ref[...] = (c["c4"] * (gate * _sig(gate))).astype(BF16)
        c1_ref[...] = c["c1"]

    return pl.pallas_call(
        body, name=name, grid=(nb,),
        in_specs=_conv_in_specs(lambda n: n) + _conv_param_specs(),
        out_specs=[pl.BlockSpec((TB, 512), lambda n: (n, YC_CONV)),
                   pl.BlockSpec((TB, CONV_W), lambda n: (n, 0))],
        out_shape=[jax.ShapeDtypeStruct((lp, D_MODEL), BF16),
                   jax.ShapeDtypeStruct((lp, CONV_W), F32)],
        compiler_params=_cp(1),
    )(proj, proj, proj, proj, proj, wdw, vec, wpw)


def _conv_bwd(name, proj, dycat, c1, wdw, vec, wpw):
    lp = proj.shape[0]
    nb = lp // TB
    halo = 32

    def body(cv_ref, cg_ref, cvp_ref, cgp_ref, gate_ref, dy_ref, c1_ref, wdw_ref, vec_ref, wpw_ref,
             dp_ref, dwdw_ref, dvec_ref, dwpw_ref, carry_ref):
        n = pl.program_id(0)
        j = nb - 1 - n

        @pl.when(n == 0)
        def _():
            carry_ref[...] = jnp.zeros(carry_ref.shape, F32)
            dwdw_ref[...] = jnp.zeros(dwdw_ref.shape, F32)
            dvec_ref[...] = jnp.zeros(dvec_ref.shape, F32)
            dwpw_ref[...] = jnp.zeros(dwpw_ref.shape, F32)

        c = _conv_chain(j, cv_ref, cg_ref, cvp_ref, cgp_ref, wdw_ref, vec_ref, wpw_ref, c1_ref)
        dy = dy_ref[...].astype(F32)
        gate = gate_ref[...]
        sgate = _sig(gate)
        dc4 = dy * (gate * sgate)
        dgate = dy * c["c4"] * _dsilu(gate, sgate)
        dc4b = dc4.astype(BF16)
        dvec_ref[3:4, :] += _colsum(dc4)
        dwpw_ref[...] += _dot(c["c3"].astype(BF16), dc4b, TN)
        dc3 = _dot(dc4b, wpw_ref[...], NT)
        dc2 = dc3 * _dsilu(c["c2"], c["s2"])
        dvec_ref[1:2, :] += _colsum(dc2 * c["xh"])
        dvec_ref[2:3, :] += _colsum(dc2)
        dc1 = _ln_bwd_core(dc2, c["xh"], c["rstd"], vec_ref[1:2, :])
        dvec_ref[0:1, :] += _colsum(dc1)
        for k in range(CONV_K):
            dwdw_ref[k:k + 1, :] += _colsum(dc1 * c["views"][CONV_K - 1 - k])
        dcat = jnp.concatenate([dc1, carry_ref[...]], axis=0)
        total = TB + halo
        up = [dcat] + [pltpu.roll(dcat, total - b, axis=0) for b in range(1, 8)]
        dc0 = jnp.zeros((TB, CONV_W), F32)
        for k in range(CONV_K):
            a, b = divmod(CONV_K - 1 - k, 8)
            dc0 = dc0 + wdw_ref[k:k + 1, :] * up[b][8 * a: 8 * a + TB, :]
        carry_ref[...] = dc1[0:halo, :]
        sg = c["sg"]
        dcv = dc0 * sg
        dcg = dc0 * c["cv"] * sg * (1.0 - sg)
        dp_ref[:, 0:512] = dcv.astype(BF16)
        dp_ref[:, 512:1024] = dcg.astype(BF16)
        dp_ref[:, 1024:1536] = dgate.astype(BF16)

    jmap = lambda n: nb - 1 - n
    return pl.pallas_call(
        body, name=name, grid=(nb,),
        in_specs=(_conv_in_specs(jmap)
                  + [pl.BlockSpec((TB, 512), lambda n: (jmap(n), YC_CONV)),
                     pl.BlockSpec((TB, CONV_W), lambda n: (jmap(n), 0))]
                  + _conv_param_specs()),
        out_specs=[pl.BlockSpec((TB, 1536), lambda n: (jmap(n), 0)),
                   pl.BlockSpec((32, CONV_W), lambda n: (0, 0)),
                   pl.BlockSpec((8, CONV_W), lambda n: (0, 0)),
                   pl.BlockSpec((CONV_W, CONV_W), lambda n: (0, 0))],
        out_shape=[jax.ShapeDtypeStruct((lp, IN_TOTAL), BF16),
                   jax.ShapeDtypeStruct((32, CONV_W), F32),
                   jax.ShapeDtypeStruct((8, CONV_W), F32),
                   jax.ShapeDtypeStruct((CONV_W, CONV_W), F32)],
        scratch_shapes=[pltpu.VMEM((halo, CONV_W), F32)],
        compiler_params=_cp(1),
    )(proj, proj, proj, proj, proj, dycat, c1, wdw, vec, wpw)


def _rope_tables(lp):
    half = ROT_DIM // 2
    inv_freq = ROPE_THETA ** (-jnp.arange(half, dtype=F32) / half)
    pos = (jnp.arange(lp, dtype=jnp.int32) - PAD0).astype(F32)
    ang = pos[:, None] * inv_freq[None, :]
    cos, sin = jnp.cos(ang), jnp.sin(ang)
    ones = jnp.ones((lp, HEAD_DIM - ROT_DIM), F32)
    zeros = jnp.zeros((lp, HEAD_DIM - ROT_DIM), F32)
    zh = jnp.zeros((lp, half), F32)
    c = jnp.concatenate([cos, cos, ones], axis=1)
    sa = jnp.concatenate([-sin, zh, zeros], axis=1)
    sb = jnp.concatenate([zh, sin, zeros], axis=1)
    tile = lambda t: jnp.tile(t, (1, KV_W // HEAD_DIM))
    return tile(c), tile(sa), tile(sb)


def _rot(x, c, sa, sb):
    w = x.shape[1]
    return x * c + pltpu.roll(x, w - 8, axis=1) * sa + pltpu.roll(x, 8, axis=1) * sb


def _rot_t(dy, c, sa, sb):
    w = dy.shape[1]
    return dy * c + pltpu.roll(dy * sa, 8, axis=1) + pltpu.roll(dy * sb, w - 8, axis=1)


def _rope_fwd(name, proj, tabs):
    lp = proj.shape[0]
    tr = _row_tile(lp, 3)

    def body(q0_ref, q1_ref, k_ref, c_ref, sa_ref, sb_ref, qr_ref, kr_ref):
        c, sa, sb = c_ref[...], sa_ref[...], sb_ref[...]
        c2 = jnp.concatenate([c, c], axis=1)
        sa2 = jnp.concatenate([sa, sa], axis=1)
        sb2 = jnp.concatenate([sb, sb], axis=1)
        qr_ref[:, 0:512] = (_rot(q0_ref[...], c2, sa2, sb2) * ATT_SCALE).astype(BF16)
        qr_ref[:, 512:1024] = (_rot(q1_ref[...], c2, sa2, sb2) * ATT_SCALE).astype(BF16)
        kr_ref[...] = _rot(k_ref[...], c, sa, sb).astype(BF16)

    tab = pl.BlockSpec((tr, KV_W), lambda i: (i, 0))
    return pl.pallas_call(
        body, name=name, grid=(lp // tr,),
        in_specs=[pl.BlockSpec((tr, 512), lambda i: (i, COL_Q0)),
                  pl.BlockSpec((tr, 512), lambda i: (i, COL_Q0 + 1)),
                  pl.BlockSpec((tr, KV_W), lambda i: (i, COL_K256)),
                  tab, tab, tab],
        out_specs=[pl.BlockSpec((tr, ATT_W), lambda i: (i, 0)),
                   pl.BlockSpec((tr, KV_W), lambda i: (i, 0))],
        out_shape=[jax.ShapeDtypeStruct((lp, ATT_W), BF16),
                   jax.ShapeDtypeStruct((lp, KV_W), BF16)],
        compiler_params=_cp(1),
    )(proj, proj, proj, *tabs)


def _attn_mask(j):
    qi = lax.broadcasted_iota(jnp.int32, (GROUP * TB, 3 * TB), 0) & (TB - 1)
    cc = lax.broadcasted_iota(jnp.int32, (GROUP * TB, 3 * TB), 1)
    jj = cc & (TB - 1)
    is_meta = jj >= PAD0
    p0 = (cc < TB) & is_meta & (j >= 1)
    p1 = (cc >= TB) & (cc < 2 * TB) & (jj > qi) & (j >= 2)
    p2 = (cc >= 2 * TB) & (jj <= qi) & ((j >= 1) | is_meta)
    return p0 | p1 | p2


def _lane_group(rows):
    return lax.broadcasted_iota(jnp.int32, (rows, KV_W), 1) // HEAD_DIM


def _stack_heads(x, kv, lgq):
    parts = []
    for g in range(GROUP):
        sh = ((kv - g) % GROUP) * HEAD_DIM
        moved = x if sh == 0 else pltpu.roll(x, sh, axis=1)
        parts.append(jnp.where(lgq == kv, moved, 0.0))
    return jnp.concatenate(parts, axis=0).astype(BF16)


def _unstack_heads(r, kv):
    out = None
    for g in range(GROUP):
        blk = r[g * TB:(g + 1) * TB, :]
        sh = ((g - kv) % GROUP) * HEAD_DIM
        blk = blk if sh == 0 else pltpu.roll(blk, sh, axis=1)
        out = blk if out is None else out + blk
    return out


def _sink_column(sinks, kv):
    lane = lax.broadcasted_iota(jnp.int32, (1, 128), 1)
    cols = []
    for g in range(GROUP):
        sg = jnp.sum(jnp.where(lane == kv * GROUP + g, sinks, 0.0), axis=1, keepdims=True)
        cols.append(jnp.broadcast_to(sg, (TB, 1)))
    return jnp.concatenate(cols, axis=0)


def _attn_kv(kall, vall, lg, kv):
    km = jnp.where(lg == kv, kall, 0.0).astype(BF16)
    ones = jnp.where(lg == (kv + 1) % N_KV, 1.0, 0.0)
    vm = jnp.where(lg == kv, vall, ones).astype(BF16)
    return km, vm


def _attn_weights(qst, km, vm, sinkcol, valid, lg4, kv):
    s = jnp.where(valid, _dot(qst, km, NT), NEG_INF)
    m = jnp.maximum(jnp.max(s, axis=-1, keepdims=True), sinkcol)
    eb = jnp.exp(s - m).astype(BF16)
    es = jnp.exp(sinkcol - m)
    r = _dot(eb, vm, NN)
    rowsum = pltpu.roll(r, KV_W - HEAD_DIM, axis=1)
    inv = 1.0 / (rowsum + es)
    out = jnp.where(lg4 == kv, r * inv, 0.0)
    return eb, es, inv, out


def _attn_specs(jmap):
    blk = lambda col: pl.BlockSpec((TB, KV_W), lambda n: (jmap(n), col))
    prv = lambda col: pl.BlockSpec((TB, KV_W), lambda n: (jnp.maximum(jmap(n) - 1, 0), col))
    met = lambda col: pl.BlockSpec((TB, KV_W), lambda n: (0, col))
    return dict(
        qr=pl.BlockSpec((TB, ATT_W), lambda n: (jmap(n), 0)),
        k=[met(0), prv(0), blk(0)],
        v=[met(COL_V256), prv(COL_V256), blk(COL_V256)],
        gate=pl.BlockSpec((TB, ATT_W), lambda n: (jmap(n), COL_AGATE1024)),
        sinks=pl.BlockSpec((8, 128), lambda n: (0, 0)),
    )


def _attn_fwd(name, qr, kr, proj, sinks_row, ycat):
    lp = proj.shape[0]
    nb = lp // TB
    sp = _attn_specs(lambda n: n)

    def body(qr_ref, km_ref, kp_ref, kc_ref, vm_ref, vp_ref, vc_ref, gate_ref, sink_ref, yin_ref, o_ref):
        del yin_ref
        j = pl.program_id(0)
        valid = _attn_mask(j)
        kall = jnp.concatenate([km_ref[...], kp_ref[...], kc_ref[...]], axis=0).astype(F32)
        vall = jnp.concatenate([vm_ref[...], vp_ref[...], vc_ref[...]], axis=0)
        lg = _lane_group(3 * TB)
        lgq = _lane_group(TB)
        lg4 = _lane_group(GROUP * TB)
        sinks = sink_ref[0:1, :]
        for kv in range(N_KV):
            cols = slice(kv * KV_W, (kv + 1) * KV_W)
            km, vm = _attn_kv(kall, vall, lg, kv)
            qst = _stack_heads(qr_ref[:, cols].astype(F32), kv, lgq)
            _, _, _, out = _attn_weights(qst, km, vm, _sink_column(sinks, kv), valid, lg4, kv)
            att = _unstack_heads(out, kv)
            gate = gate_ref[:, cols]
            o_ref[:, cols] = (att * (gate * _sig(gate))).astype(BF16)

    return pl.pallas_call(
        body, name=name, grid=(nb,),
        in_specs=[sp["qr"]] + sp["k"] + sp["v"] + [sp["gate"], sp["sinks"],
                                                   pl.BlockSpec(memory_space=pl.ANY)],
        out_specs=pl.BlockSpec((TB, ATT_W), lambda n: (n, 0)),
        out_shape=jax.ShapeDtypeStruct((lp, D_MODEL), BF16),
        input_output_aliases={9: 0},
        compiler_params=_cp(1),
    )(qr, kr, kr, kr, proj, proj, proj, proj, sinks_row, ycat)


def _attn_bwd(name, qr, kr, proj, sinks_row, dycat, dep):
    lp = proj.shape[0]
    nb = lp // TB
    sp = _attn_specs(lambda n: n)

    def body(qr_ref, km_ref, kp_ref, kc_ref, vm_ref, vp_ref, vc_ref, gate_ref, sink_ref, dy_ref, dep_ref,
             dq_ref, dgate_ref, dk_ref, dv_ref, dsink_ref):
        del dep_ref
        j = pl.program_id(0)

        @pl.when(j == 0)
        def _():
            dk_ref[...] = jnp.zeros(dk_ref.shape, F32)
            dv_ref[...] = jnp.zeros(dv_ref.shape, F32)
            dsink_ref[...] = jnp.zeros(dsink_ref.shape, F32)

        valid = _attn_mask(j)
        kall = jnp.concatenate([km_ref[...], kp_ref[...], kc_ref[...]], axis=0).astype(F32)
        vall = jnp.concatenate([vm_ref[...], vp_ref[...], vc_ref[...]], axis=0)
        lg = _lane_group(3 * TB)
        lgq = _lane_group(TB)
        lg4 = _lane_group(GROUP * TB)
        sinks = sink_ref[0:1, :]
        lane = lax.broadcasted_iota(jnp.int32, (1, 128), 1)
        dkall = jnp.zeros((3 * TB, KV_W), F32)
        dvall = jnp.zeros((3 * TB, KV_W), F32)
        dsink = jnp.zeros((1, 128), F32)
        for kv in range(N_KV):
            cols = slice(kv * KV_W, (kv + 1) * KV_W)
            km, vm = _attn_kv(kall, vall, lg, kv)
            qst = _stack_heads(qr_ref[:, cols].astype(F32), kv, lgq)
            gate = gate_ref[:, cols]
            sgate = _sig(gate)
            dy = dy_ref[:, cols].astype(F32)
            dout = dy * (gate * sgate)
            eb, es, inv, out = _attn_weights(qst, km, vm, _sink_column(sinks, kv), valid, lg4, kv)
            att = _unstack_heads(out, kv)
            dgate_ref[:, cols] = (dy * att * _dsilu(gate, sgate)).astype(BF16)
            dsc = dout * _unstack_heads(jnp.where(lg4 == kv, inv, 0.0), kv)
            dost = _stack_heads(dsc, kv, lgq)
            dd = dsc * att
            dcol = jnp.concatenate(
                [jnp.sum(jnp.where(lgq == g, dd, 0.0), axis=1, keepdims=True) for g in range(GROUP)], axis=0)
            dp = _dot(dost, vm, NT)
            ds = (eb.astype(F32) * (dp - dcol)).astype(BF16)
            pd = es * dcol
            for g in range(GROUP):
                tot = jnp.sum(pd[g * TB:(g + 1) * TB, :], axis=0, keepdims=True)
                dsink = dsink - jnp.where(lane == kv * GROUP + g, tot, 0.0)
            dq_ref[:, cols] = _unstack_heads(_dot(ds, km, NN), kv)
            dkall = dkall + _dot(ds, qst, TN)
            dvall = dvall + _dot(eb, dost, TN)
        dsink_ref[0:1, :] += dsink
        prev = pl.multiple_of(jnp.maximum(j - 1, 0) * TB, TB)
        cur = pl.multiple_of(j * TB, TB)
        dk_ref[0:TB, :] += dkall[0:TB]
        dv_ref[0:TB, :] += dvall[0:TB]
        dk_ref[pl.ds(prev, TB), :] += dkall[TB:2 * TB]
        dv_ref[pl.ds(prev, TB), :] += dvall[TB:2 * TB]
        dk_ref[pl.ds(cur, TB), :] += dkall[2 * TB:3 * TB]
        dv_ref[pl.ds(cur, TB), :] += dvall[2 * TB:3 * TB]

    return pl.pallas_call(
        body, name=name, grid=(nb,),
        in_specs=[sp["qr"]] + sp["k"] + sp["v"] + [sp["gate"], sp["sinks"],
                                                   pl.BlockSpec((TB, ATT_W), lambda n: (n, 0)),
                                                   pl.BlockSpec(memory_space=pl.ANY)],
        out_specs=[pl.BlockSpec((TB, ATT_W), lambda n: (n, 0)),
                   pl.BlockSpec((TB, ATT_W), lambda n: (n, 0)),
                   pl.BlockSpec((lp, KV_W), lambda n: (0, 0)),
                   pl.BlockSpec((lp, KV_W), lambda n: (0, 0)),
                   pl.BlockSpec((8, 128), lambda n: (0, 0))],
        out_shape=[jax.ShapeDtypeStruct((lp, ATT_W), F32),
                   jax.ShapeDtypeStruct((lp, ATT_W), BF16),
                   jax.ShapeDtypeStruct((lp, KV_W), F32),
                   jax.ShapeDtypeStruct((lp, KV_W), F32),
                   jax.ShapeDtypeStruct((8, 128), F32)],
        compiler_params=_cp(1),
    )(qr, kr, kr, kr, proj, proj, proj, proj, sinks_row, dycat, dep)


def _attn_assemble(name, dq, dgate, dk, dv, tabs, dproj):
    lp = dq.shape[0]
    tr = _row_tile(lp, 3)

    def body(dq_ref, dg_ref, dk_ref, dv_ref, c_ref, sa_ref, sb_ref, din_ref, o_ref):
        del din_ref
        cidx = pl.program_id(1)
        c, sa, sb = c_ref[...], sa_ref[...], sb_ref[...]

        @pl.when(cidx < 2)
        def _():
            c2 = jnp.concatenate([c, c], axis=1)
            sa2 = jnp.concatenate([sa, sa], axis=1)
            sb2 = jnp.concatenate([sb, sb], axis=1)
            o_ref[...] = (_rot_t(dq_ref[...], c2, sa2, sb2) * ATT_SCALE).astype(BF16)

        @pl.when(cidx == 2)
        def _():
            o_ref[:, 0:KV_W] = _rot_t(dk_ref[...], c, sa, sb).astype(BF16)
            o_ref[:, KV_W:2 * KV_W] = dv_ref[...].astype(BF16)

        @pl.when(cidx > 2)
        def _():
            o_ref[...] = dg_ref[...]

    tab = pl.BlockSpec((tr, KV_W), lambda n, c: (n, 0))
    return pl.pallas_call(
        body, name=name, grid=(lp // tr, 5),
        in_specs=[pl.BlockSpec((tr, 512), lambda n, c: (n, jnp.minimum(c, 1))),
                  pl.BlockSpec((tr, 512), lambda n, c: (n, jnp.clip(c - 3, 0, 1))),
                  tab, tab,
                  tab, tab, tab,
                  pl.BlockSpec(memory_space=pl.ANY)],
        out_specs=pl.BlockSpec((tr, 512), lambda n, c: (n, COL_Q0 + c)),
        out_shape=jax.ShapeDtypeStruct((lp, IN_TOTAL), BF16),
        input_output_aliases={7: 0},
        compiler_params=_cp(2),
    )(dq, dgate, dk, dv, *tabs, dproj)


def _softplus_neg(lam):
    t = jnp.exp(-jnp.abs(lam))
    u = 1.0 + t
    den = jnp.where(u == 1.0, 1.0, u - 1.0)
    l1p = jnp.where(u == 1.0, t, jnp.log(u) * (t / den))
    return jnp.maximum(-lam, 0.0) + l1p


def _lru_chain(j, rx_ref, rxp_ref, wl_ref, vec_ref, wa_ref, wx_ref):
    rx = rx_ref[...]
    rxp = jnp.where(j > 0, rxp_ref[...], 0.0)
    cat = jnp.concatenate([rxp, rx], axis=0)
    views = [cat[8:8 + TB, :]] + [pltpu.roll(cat, s, axis=0)[8:8 + TB, :] for s in range(1, LRU_CONV_K)]
    x1 = jnp.broadcast_to(vec_ref[0:1, :], (TB, LRU_W))
    for k in range(LRU_CONV_K):
        x1 = x1 + wl_ref[k:k + 1, :] * views[LRU_CONV_K - 1 - k]
    x1b = x1.astype(BF16)
    r = _sig(_dot(x1b, wa_ref[...], NN) + vec_ref[1:2, :])
    ig = _sig(_dot(x1b, wx_ref[...], NN) + vec_ref[2:3, :])
    sp = _softplus_neg(vec_ref[3:4, :])
    log_a = -LRU_C * r * sp
    rows = _row_ids((TB, LRU_W), j * TB)
    live = rows >= PAD0
    a = jnp.where(live, jnp.exp(log_a), 0.0)
    y2 = 2.0 * log_a
    em = -jnp.tanh(0.5 * y2) * (jnp.exp(y2) + 1.0)
    mult = jnp.sqrt(em)
    return dict(views=views, x1=x1, x1b=x1b, r=r, ig=ig, sp=sp, a=a, mult=mult, live=live, a_raw=jnp.exp(log_a))


def _lru_specs(jmap):
    return [pl.BlockSpec((TB, 512), lambda n: (jmap(n), COL_RX)),
            pl.BlockSpec((8, 512), lambda n: (jnp.maximum(jmap(n) * (TB // 8) - 1, 0), COL_RX)),
            pl.BlockSpec((TB, 512), lambda n: (jmap(n), COL_RGATE))]


def _lru_param_specs():
    return [pl.BlockSpec((8, LRU_W), lambda n: (0, 0)),
            pl.BlockSpec((8, LRU_W), lambda n: (0, 0)),
            pl.BlockSpec((LRU_W, LRU_W), lambda n: (0, 0)),
            pl.BlockSpec((LRU_W, LRU_W), lambda n: (0, 0))]


def _lru_fwd(name, proj, wl, vec, wa, wx, ycat):
    lp = proj.shape[0]
    nb = lp // TB

    def body(rx_ref, rxp_ref, gate_ref, wl_ref, vec_ref, wa_ref, wx_ref, yin_ref, o_ref, h_ref, carry_ref):
        del yin_ref
        j = pl.program_id(0)

        @pl.when(j == 0)
        def _():
            carry_ref[...] = jnp.zeros(carry_ref.shape, F32)

        c = _lru_chain(j, rx_ref, rxp_ref, wl_ref, vec_ref, wa_ref, wx_ref)
        a = c["a"]
        u = jnp.where(c["live"], c["mult"] * (c["ig"] * c["x1"]), 0.0)
        rows = lax.broadcasted_iota(jnp.int32, (TB, LRU_W), 0)
        d = 1
        while d < TB:
            ap = jnp.where(rows >= d, pltpu.roll(a, d, axis=0), 1.0)
            up = jnp.where(rows >= d, pltpu.roll(u, d, axis=0), 0.0)
            u = a * up + u
            a = a * ap
            d *= 2
        h = u + a * carry_ref[0:1, :]
        carry_ref[...] = h[TB - 8:TB, :]
        carry_ref[0:1, :] = h[TB - 1:TB, :]
        h_ref[...] = h
        gate = gate_ref[...]
        o_ref[...] = (h * (gate * _sig(gate))).astype(BF16)

    return pl.pallas_call(
        body, name=name, grid=(nb,),
        in_specs=_lru_specs(lambda n: n) + _lru_param_specs() + [pl.BlockSpec(memory_space=pl.ANY)],
        out_specs=[pl.BlockSpec((TB, 512), lambda n: (n, YC_LRU)),
                   pl.BlockSpec((TB, LRU_W), lambda n: (n, 0))],
        out_shape=[jax.ShapeDtypeStruct((lp, D_MODEL), BF16),
                   jax.ShapeDtypeStruct((lp, LRU_W), F32)],
        input_output_aliases={7: 0},
        scratch_shapes=[pltpu.VMEM((8, LRU_W), F32)],
        compiler_params=_cp(1),
    )(proj, proj, proj, wl, vec, wa, wx, ycat)


def _lru_bwd(name, proj, dycat, hstate, wl, vec, wa, wx, dproj):
    lp = proj.shape[0]
    nb = lp // TB

    def body(rx_ref, rxp_ref, gate_ref, dy_ref, h_ref, hp_ref, wl_ref, vec_ref, wa_ref, wx_ref, din_ref,
             dp_ref, dwl_ref, dvec_ref, dwa_ref, dwx_ref, dhc_ref, anx_ref, dxc_ref):
        del din_ref
        n = pl.program_id(0)
        j = nb - 1 - n

        @pl.when(n == 0)
        def _():
            dhc_ref[...] = jnp.zeros(dhc_ref.shape, F32)
            anx_ref[...] = jnp.zeros(anx_ref.shape, F32)
            dxc_ref[...] = jnp.zeros(dxc_ref.shape, F32)
            dwl_ref[...] = jnp.zeros(dwl_ref.shape, F32)
            dvec_ref[...] = jnp.zeros(dvec_ref.shape, F32)
            dwa_ref[...] = jnp.zeros(dwa_ref.shape, F32)
            dwx_ref[...] = jnp.zeros(dwx_ref.shape, F32)

        c = _lru_chain(j, rx_ref, rxp_ref, wl_ref, vec_ref, wa_ref, wx_ref)
        a, mult, r, ig, x1, live = c["a"], c["mult"], c["r"], c["ig"], c["x1"], c["live"]
        h = h_ref[...]
        gate = gate_ref[...]
        sgate = _sig(gate)
        dy = dy_ref[...].astype(F32)
        gsum = dy * (gate * sgate)
        dgate = dy * h * _dsilu(gate, sgate)
        rows = lax.broadcasted_iota(jnp.int32, (TB, LRU_W), 0)
        bb = jnp.where(rows == TB - 1, anx_ref[0:1, :], pltpu.roll(a, TB - 1, axis=0))
        gg = gsum
        d = 1
        while d < TB:
            keep = rows < TB - d
            bn = jnp.where(keep, pltpu.roll(bb, TB - d, axis=0), 1.0)
            gn = jnp.where(keep, pltpu.roll(gg, TB - d, axis=0), 0.0)
            gg = gg + bb * gn
            bb = bb * bn
            d *= 2
        dh = gg + bb * dhc_ref[0:1, :]
        dhc_ref[...] = dh[0:8, :]
        anx_ref[...] = a[0:8, :]
        hprev = jnp.where(rows == 0, jnp.where(j > 0, hp_ref[7:8, :], 0.0), pltpu.roll(h, 1, axis=0))
        du = jnp.where(live, dh, 0.0)
        da = jnp.where(live, dh * hprev, 0.0)
        ar = c["a_raw"]
        dmult = du * (ig * x1)
        di = du * mult * x1
        dx1 = du * mult * ig
        dloga = da * ar - dmult * ar * ar / mult
        dr = dloga * (-LRU_C * c["sp"])
        dvec_ref[3:4, :] += _colsum(dloga * (-LRU_C * r))
        dza = dr * r * (1.0 - r)
        dzx = di * ig * (1.0 - ig)
        dzab, dzxb = dza.astype(BF16), dzx.astype(BF16)
        dvec_ref[1:2, :] += _colsum(dza)
        dvec_ref[2:3, :] += _colsum(dzx)
        dwa_ref[...] += _dot(c["x1b"], dzab, TN)
        dwx_ref[...] += _dot(c["x1b"], dzxb, TN)
        dx1 = dx1 + _dot(dzab, wa_ref[...], NT) + _dot(dzxb, wx_ref[...], NT)
        dvec_ref[0:1, :] += _colsum(dx1)
        for k in range(LRU_CONV_K):
            dwl_ref[k:k + 1, :] += _colsum(dx1 * c["views"][LRU_CONV_K - 1 - k])
        dcat = jnp.concatenate([dx1, dxc_ref[...]], axis=0)
        drx = jnp.zeros((TB, LRU_W), F32)
        for k in range(LRU_CONV_K):
            s = LRU_CONV_K - 1 - k
            view = dcat[0:TB, :] if s == 0 else pltpu.roll(dcat, TB + 8 - s, axis=0)[0:TB, :]
            drx = drx + wl_ref[k:k + 1, :] * view
        dxc_ref[...] = dx1[0:8, :]
        dp_ref[:, 0:512] = drx.astype(BF16)
        dp_ref[:, 512:1024] = dgate.astype(BF16)

        @pl.when(n == nb - 1)
        def _():
            lam = vec_ref[3:4, :]
            dvec_ref[3:4, :] = dvec_ref[3:4, :] * (-_sig(-lam))

    jmap = lambda n: nb - 1 - n
    return pl.pallas_call(
        body, name=name, grid=(nb,),
        in_specs=(_lru_specs(jmap)
                  + [pl.BlockSpec((TB, 512), lambda n: (jmap(n), YC_LRU)),
                     pl.BlockSpec((TB, LRU_W), lambda n: (jmap(n), 0)),
                     pl.BlockSpec((8, LRU_W), lambda n: (jnp.maximum(jmap(n) * (TB // 8) - 1, 0), 0))]
                  + _lru_param_specs() + [pl.BlockSpec(memory_space=pl.ANY)]),
        out_specs=[pl.BlockSpec((TB, 1024), lambda n: (jmap(n), 4)),
                   pl.BlockSpec((8, LRU_W), lambda n: (0, 0)),
                   pl.BlockSpec((8, LRU_W), lambda n: (0, 0)),
                   pl.BlockSpec((LRU_W, LRU_W), lambda n: (0, 0)),
                   pl.BlockSpec((LRU_W, LRU_W), lambda n: (0, 0))],
        out_shape=[jax.ShapeDtypeStruct((lp, IN_TOTAL), BF16),
                   jax.ShapeDtypeStruct((8, LRU_W), F32),
                   jax.ShapeDtypeStruct((8, LRU_W), F32),
                   jax.ShapeDtypeStruct((LRU_W, LRU_W), F32),
                   jax.ShapeDtypeStruct((LRU_W, LRU_W), F32)],
        input_output_aliases={10: 0},
        scratch_shapes=[pltpu.VMEM((8, LRU_W), F32), pltpu.VMEM((8, LRU_W), F32), pltpu.VMEM((8, LRU_W), F32)],
        compiler_params=_cp(1),
    )(proj, proj, proj, dycat, hstate, hstate, wl, vec, wa, wx, dproj)


_HBM = pl.BlockSpec(memory_space=pltpu.HBM)
_SEM = pl.BlockSpec(memory_space=pltpu.SEMAPHORE)
_ANY = pl.BlockSpec(memory_space=pl.ANY)
_EFFECT = pltpu.SideEffectType.DATAFLOW_SIDE_EFFECTING


def _hbm(a):
    return pltpu.with_memory_space_constraint(a, pltpu.HBM)


_ALL_PEERS = tuple(range(1, N_DEV))
_CHIP_PEERS = (1, 2, 4, 6)
_OTHER_CHIPS = (2, 4, 6)


def _spec_peers(mode):
    return {"ici": _CHIP_PEERS, "fwd": _OTHER_CHIPS}.get(mode, _ALL_PEERS)


def _split_descriptors(copies, srcs, lands, send_sems, recv_sems):
    x, y, c = lax.axis_index("x"), lax.axis_index("y"), lax.axis_index("c")
    me = 4 * x + 2 * y + c
    out, sem = [], 0
    for si, mode, li, ll in copies:
        for k in _spec_peers(mode):
            px = 1 - x if k & 4 else x
            py = 1 - y if k & 2 else y
            pc = 1 - c if k & 1 else c
            peer = 4 * px + 2 * py + pc
            if mode == "fwd":
                src = dst = lands[li].at[peer]
                target = (x, y, 1 - c)
            else:
                src = srcs[si].at[peer] if mode is True else srcs[si]
                dst = lands[li].at[me] if ll is None else lands[li].at[me, ll]
                target = (px, py, pc)
            out.append(pltpu.make_async_remote_copy(
                src_ref=src, dst_ref=dst, send_sem=send_sems.at[sem], recv_sem=recv_sems.at[sem],
                device_id=target, device_id_type=pl.DeviceIdType.MESH))
            sem += 1
    return out


def _n_copies(copies):
    return sum(len(_spec_peers(mode)) for _, mode, _, _ in copies)


def _xchg_start(name, groups):
    n_src = [len(g[0]) for g in groups]
    n_land = [len(g[1]) for g in groups]
    srcs = [s for g in groups for s in g[0]]
    lands = [l for g in groups for l in g[1]]
    ns, nl, ng = len(srcs), len(lands), len(groups)

    def body(*refs):
        src_refs, land_refs = refs[:ns], refs[ns:ns + nl]
        sems = refs[ns + nl:ns + nl + 2 * ng]
        token = refs[-1]
        so = lo = 0
        for gi, (_, _, copies) in enumerate(groups):
            for d in _split_descriptors(copies, src_refs[so:so + n_src[gi]], land_refs[lo:lo + n_land[gi]],
                                        sems[2 * gi], sems[2 * gi + 1]):
                d.start()
            so += n_src[gi]
            lo += n_land[gi]
        token[...] = jnp.zeros(token.shape, F32)

    out_shape, out_specs = [], []
    for g in groups:
        n = _n_copies(g[2])
        out_shape += [pltpu.SemaphoreType.DMA((n,)), pltpu.SemaphoreType.DMA((n,))]
        out_specs += [_SEM, _SEM]
    out_shape += [pltpu.HBM(l.shape, l.dtype) for l in lands]
    out_specs += [_HBM] * nl
    out_shape.append(jax.ShapeDtypeStruct((8, 128), F32))
    out_specs.append(pl.BlockSpec(memory_space=pltpu.VMEM))
    outs = pl.pallas_call(
        body, name=name, in_specs=[_HBM] * (ns + nl), out_specs=out_specs, out_shape=out_shape,
        input_output_aliases={ns + i: 2 * ng + i for i in range(nl)},
        compiler_params=pltpu.CompilerParams(has_side_effects=_EFFECT),
    )(*[_hbm(a) for a in srcs + lands])
    res, lo = [], 2 * ng
    for gi in range(ng):
        res.append((outs[2 * gi], outs[2 * gi + 1], list(outs[lo:lo + n_land[gi]])))
        lo += n_land[gi]
    return res, outs[-1]


def _xchg_wait(name, group, started, after):
    srcs, _, copies = group
    send_sems, recv_sems, lands = started
    ns, nl = len(srcs), len(lands)
    after = list(after)

    def body(*refs):
        src_refs, land_refs = refs[:ns], refs[ns:ns + nl]
        send_ref, recv_ref = refs[ns + nl], refs[ns + nl + 1]
        for d in _split_descriptors(copies, src_refs, land_refs, send_ref, recv_ref):
            d.wait_send()
            d.wait_recv()

    outs = pl.pallas_call(
        body, name=name, in_specs=[_HBM] * (ns + nl) + [_SEM, _SEM] + [_ANY] * len(after),
        out_specs=[_HBM] * nl, out_shape=[pltpu.HBM(l.shape, l.dtype) for l in lands],
        input_output_aliases={ns + i: i for i in range(nl)},
        compiler_params=pltpu.CompilerParams(has_side_effects=_EFFECT),
    )(*[_hbm(a) for a in srcs], *lands, send_sems, recv_sems, *after)
    return list(outs)


def _landing(own, me):
    land = lax.empty((N_DEV,) + own.shape, own.dtype)
    return lax.dynamic_update_slice(land, own[None], (me,) + (0,) * own.ndim)


def _adamw(name, w, m, v, recv, row0=0, prev=None):
    cdim = w.shape[1]
    r = recv.shape[1]
    tr = r
    for cand in (512, 256, 128, 64, 32, 16, 8):
        if r % cand == 0 and r > cand:
            tr = cand
            break
    assert row0 % tr == 0
    blk0 = row0 // tr
    n_prev = 0 if prev is None else 4

    def body(w_ref, m_ref, v_ref, r_ref, *rest):
        g_ref, d_ref, mo_ref, vo_ref = rest[n_prev:]
        g = r_ref[0].astype(F32)
        for s in range(1, N_DEV):
            g = g + r_ref[s].astype(F32)
        mn = ADAM_B1 * m_ref[...] + (1.0 - ADAM_B1) * g
        vn = ADAM_B2 * v_ref[...] + (1.0 - ADAM_B2) * (g * g)
        m_hat = mn / (1.0 - ADAM_B1 ** ADAM_STEP)
        v_hat = vn / (1.0 - ADAM_B2 ** ADAM_STEP)
        g_ref[...] = g
        d_ref[...] = -ADAM_LR * (m_hat / (jnp.sqrt(v_hat) + ADAM_EPS) + ADAM_WD * w_ref[...])
        mo_ref[...] = mn
        vo_ref[...] = vn

    blk = pl.BlockSpec((tr, cdim), lambda i: (i + blk0, 0))
    return pl.pallas_call(
        body, name=name, grid=(r // tr,),
        in_specs=[blk, blk, blk, pl.BlockSpec((N_DEV, tr, cdim), lambda i: (0, i, 0))] + [_ANY] * n_prev,
        out_specs=[blk, blk, blk, blk],
        out_shape=[jax.ShapeDtypeStruct(w.shape, F32)] * 4,
        input_output_aliases={4 + i: i for i in range(n_prev)},
        compiler_params=_cp(1),
    )(w, m, v, recv, *(prev or []))


def _pack_rows(arrs, lead=()):
    n = len(lead)
    flat = jnp.concatenate([a.reshape(a.shape[:n] + (-1,)) for a in arrs], axis=-1)
    size = flat.shape[-1]
    padded = -(-size // PACK_QUANTUM) * PACK_QUANTUM
    flat = jnp.pad(flat, [(0, 0)] * n + [(0, padded - size)])
    return flat.reshape(flat.shape[:n] + (padded // 128, 128))


def _unpack_rows(packed, shapes, lead=()):
    n = len(lead)
    flat = packed.reshape(packed.shape[:n] + (-1,))
    out, off = [], 0
    for s in shapes:
        size = int(np.prod(s))
        out.append(flat[..., off:off + size].reshape(packed.shape[:n] + tuple(s)))
        off += size
    return out


def _block_diag(w):
    eye = jnp.eye(LRU_HEADS, dtype=w.dtype)
    return (eye[:, None, :, None] * w[:, :, None, :]).reshape(LRU_W, LRU_W)


def _diag_blocks(dense):
    t = dense.reshape(LRU_HEADS, 64, LRU_HEADS, 64)
    return jnp.stack([t[h, :, h, :] for h in range(LRU_HEADS)], axis=0)


def _cols_to_slots(full):
    lead = full.shape[:-1]
    t = full.reshape(lead + (N_DEV, full.shape[-1] // N_DEV))
    return jnp.moveaxis(t, -2, 0)


def _slots_to_cols(slots):
    t = jnp.moveaxis(slots, 0, -2)
    return t.reshape(t.shape[:-2] + (t.shape[-2] * t.shape[-1],))


def kernel(x, meta_tokens, ln_in_g, ln_in_b, w_in, conv_dw_w, conv_dw_b, conv_ln_g, conv_ln_b, conv_pw_w, conv_pw_b, attn_sinks, lru_conv_w, lru_conv_b, lru_wa, lru_ba, lru_wx, lru_bx, lru_lambda, w_out, ln_post_g, ln_post_b, loss_target, m_meta_tokens, m_ln_in_g, m_ln_in_b, m_w_in, m_conv_dw_w, m_conv_dw_b, m_conv_ln_g, m_conv_ln_b, m_conv_pw_w, m_conv_pw_b, m_attn_sinks, m_lru_conv_w, m_lru_conv_b, m_lru_wa, m_lru_ba, m_lru_wx, m_lru_bx, m_lru_lambda, m_w_out, m_ln_post_g, m_ln_post_b, v_meta_tokens, v_ln_in_g, v_ln_in_b, v_w_in, v_conv_dw_w, v_conv_dw_b, v_conv_ln_g, v_conv_ln_b, v_conv_pw_w, v_conv_pw_b, v_attn_sinks, v_lru_conv_w, v_lru_conv_b, v_lru_wa, v_lru_ba, v_lru_wx, v_lru_bx, v_lru_lambda, v_w_out, v_ln_post_g, v_ln_post_b):
    seq = x.shape[1]
    lp = seq + TB
    row = lambda a: a.reshape(1, -1)
    rep_names = ["ln_in_g", "ln_in_b", "conv_dw_b", "conv_ln_g", "conv_ln_b", "conv_pw_b", "attn_sinks",
                 "lru_conv_b", "lru_wa", "lru_ba", "lru_wx", "lru_bx", "lru_lambda", "ln_post_g", "ln_post_b"]
    shard_small_names = ["conv_dw_w", "lru_conv_w", "meta_tokens"]
    weights = dict(meta_tokens=meta_tokens, ln_in_g=ln_in_g, ln_in_b=ln_in_b, w_in=w_in, conv_dw_w=conv_dw_w,
                   conv_dw_b=conv_dw_b, conv_ln_g=conv_ln_g, conv_ln_b=conv_ln_b, conv_pw_w=conv_pw_w,
                   conv_pw_b=conv_pw_b, attn_sinks=attn_sinks, lru_conv_w=lru_conv_w, lru_conv_b=lru_conv_b,
                   lru_wa=lru_wa, lru_ba=lru_ba, lru_wx=lru_wx, lru_bx=lru_bx, lru_lambda=lru_lambda,
                   w_out=w_out, ln_post_g=ln_post_g, ln_post_b=ln_post_b)
    mom1 = dict(meta_tokens=m_meta_tokens, ln_in_g=m_ln_in_g, ln_in_b=m_ln_in_b, w_in=m_w_in, conv_dw_w=m_conv_dw_w,
                conv_dw_b=m_conv_dw_b, conv_ln_g=m_conv_ln_g, conv_ln_b=m_conv_ln_b, conv_pw_w=m_conv_pw_w,
                conv_pw_b=m_conv_pw_b, attn_sinks=m_attn_sinks, lru_conv_w=m_lru_conv_w, lru_conv_b=m_lru_conv_b,
                lru_wa=m_lru_wa, lru_ba=m_lru_ba, lru_wx=m_lru_wx, lru_bx=m_lru_bx, lru_lambda=m_lru_lambda,
                w_out=m_w_out, ln_post_g=m_ln_post_g, ln_post_b=m_ln_post_b)
    mom2 = dict(meta_tokens=v_meta_tokens, ln_in_g=v_ln_in_g, ln_in_b=v_ln_in_b, w_in=v_w_in, conv_dw_w=v_conv_dw_w,
                conv_dw_b=v_conv_dw_b, conv_ln_g=v_conv_ln_g, conv_ln_b=v_conv_ln_b, conv_pw_w=v_conv_pw_w,
                conv_pw_b=v_conv_pw_b, attn_sinks=v_attn_sinks, lru_conv_w=v_lru_conv_w, lru_conv_b=v_lru_conv_b,
                lru_wa=v_lru_wa, lru_ba=v_lru_ba, lru_wx=v_lru_wx, lru_bx=v_lru_bx, lru_lambda=v_lru_lambda,
                w_out=v_w_out, ln_post_g=v_ln_post_g, ln_post_b=v_ln_post_b)
    shard_wmv = [_pack_rows([d[n] for n in shard_small_names]) for d in (weights, mom1, mom2)]
    rep_wmv = [_pack_rows([d[n] for n in rep_names]) for d in (weights, mom1, mom2)]
    gate_w = [(_block_diag(lru_wa[l]).astype(BF16), _block_diag(lru_wx[l]).astype(BF16)) for l in range(DEPTH)]
    tabs = _rope_tables(lp)
    prepared = shard_wmv + rep_wmv + [w for pair in gate_w for w in pair] + list(tabs)

    small_shard_shapes = [conv_dw_w.shape, lru_conv_w.shape, meta_tokens.shape]
    small_shard = _pack_rows([conv_dw_w, lru_conv_w, meta_tokens])
    me = 4 * lax.axis_index("x") + 2 * lax.axis_index("y") + lax.axis_index("c")
    w_in_b = [w_in[l].astype(BF16) for l in range(DEPTH)]
    w_out_b = [w_out[l].astype(BF16) for l in range(DEPTH)]
    pw_b = conv_pw_w.astype(BF16)
    wgroups = [
        ([small_shard], [_landing(small_shard, me)], [(0, False, 0, None)]),
        ([w_in_b[0]], [_landing(w_in_b[0], me)], [(0, "ici", 0, None)]),
        ([pw_b, w_out_b[0]], [_landing(pw_b, me), _landing(w_out_b[0], me)],
         [(0, False, 0, None), (1, False, 1, None)]),
        ([w_in_b[1], w_out_b[1]], [_landing(w_in_b[1], me), _landing(w_out_b[1], me)],
         [(0, False, 0, None), (1, False, 1, None)]),
    ]
    wstarted, wtoken = _xchg_start("weights_start", wgroups)
    wg_small, = _xchg_wait("weights_wait_s", wgroups[0], wstarted[0], [wtoken])
    g_dw, g_lc, g_meta = _unpack_rows(wg_small, small_shard_shapes, lead=(N_DEV,))
    conv_dw_full = _slots_to_cols(g_dw)
    lru_conv_full = _slots_to_cols(g_lc)
    meta_full = _slots_to_cols(g_meta)
    wg_in = [None, None]
    wg_out = [None, None]
    wg_pw = None

    ln_g = [ln_in_g, ln_post_g[0], ln_post_g[1]]
    ln_b = [ln_in_b, ln_post_b[0], ln_post_b[1]]

    def layer_params(l):
        wdw = jnp.pad(conv_dw_full[l], ((0, 1), (0, 0)))
        cvec = jnp.pad(jnp.stack([conv_dw_b[l], conv_ln_g[l], conv_ln_b[l], conv_pw_b[l]]), ((0, 4), (0, 0)))
        wpw = wg_pw[:, l].reshape(CONV_W, CONV_W)
        sinks = jnp.pad(attn_sinks[l].reshape(1, N_HEADS), ((0, 7), (0, 128 - N_HEADS)))
        wl = jnp.pad(lru_conv_full[l], ((0, 4), (0, 0)))
        lvec = jnp.pad(jnp.stack([lru_conv_b[l], lru_ba[l], lru_bx[l], lru_lambda[l]]), ((0, 4), (0, 0)))
        wa, wx = gate_w[l]
        wo = wg_out[l].reshape(D_MODEL, D_MODEL)
        wout = jnp.concatenate([wo[512:1536], wo[0:512], wo[1536:]], axis=0)
        return dict(wdw=wdw, cvec=cvec, wpw=wpw, sinks=sinks, wl=wl, lvec=lvec, wa=wa, wx=wx, wout=wout)

    params = [None] * DEPTH

    z0, hb = _embed(x, meta_full, row(ln_g[0]), row(ln_b[0]))
    z = [z0]
    saved = []
    for l in range(DEPTH):
        if l == 0:
            part, = _xchg_wait("weights_wait_a", wgroups[1], wstarted[1], [hb] + prepared)
            fwd = ([], [part], [(None, "fwd", 0, None)])
            fstarted, ftoken = _xchg_start("weights_fwd_start", [fwd])
            wg_in[0], = _xchg_wait("weights_fwd_wait", fwd, fstarted[0], [ftoken])
        else:
            wg_in[1], wg_out[1] = _xchg_wait("weights_wait_c", wgroups[3], wstarted[3], [hb])
        proj = _mm_proj(f"proj{l}", hb, wg_in[l])
        if l == 0:
            wg_pw, wg_out[0] = _xchg_wait("weights_wait_b", wgroups[2], wstarted[2], [proj])
        p = params[l] = layer_params(l)
        ycat, c1 = _conv_fwd(f"conv_fwd{l}", proj, p["wdw"], p["cvec"], p["wpw"])
        qr, kr = _rope_fwd(f"rope{l}", proj, tabs)
        ycat = _attn_fwd(f"attn_fwd{l}", qr, kr, proj, p["sinks"], ycat)
        ycat, hstate = _lru_fwd(f"lru_fwd{l}", proj, p["wl"], p["lvec"], p["wa"], p["wx"], ycat)
        saved.append(dict(hb=hb, proj=proj, ycat=ycat, qr=qr, kr=kr, hstate=hstate, c1=c1))
        z_next, hb = _mm_out(f"out{l}", ycat, p["wout"], z[l], row(ln_g[l]), row(ln_b[l]),
                             row(ln_g[l + 1]), row(ln_b[l + 1]))
        z.append(z_next)

    dz, st_post1, loss_blk = _loss_head(z[DEPTH], loss_target, row(ln_g[DEPTH]), row(ln_b[DEPTH]))
    loss = lax.psum(loss_blk[0, 0], ("x", "y", "c"))

    ln_stats = {DEPTH: st_post1}
    g_layers = [None] * DEPTH
    dwin_l, dwout_l = [None] * DEPTH, [None] * DEPTH
    grad_x = gmeta = None
    token = wtoken
    ggroups = [None] * DEPTH
    own = lambda a: lax.dynamic_index_in_dim(a, me, 0, keepdims=False)
    for l in reversed(range(DEPTH)):
        p, s = params[l], saved[l]
        dycat = _mm_dycat(f"dycat{l}", dz, p["wout"], token)
        dwout_l[l] = _mm_dwout(f"dwout{l}", s["ycat"], dz)
        dproj, dwdw, dcvec, dwpw = _conv_bwd(f"conv_bwd{l}", s["proj"], dycat, s["c1"], p["wdw"], p["cvec"], p["wpw"])
        dwo = jnp.concatenate([dwout_l[l][1024:1536], dwout_l[l][0:1024], dwout_l[l][1536:]], axis=0)
        dwo = dwo.reshape(N_DEV, D_MODEL // N_DEV, D_MODEL)
        dpw = dwpw.reshape(N_DEV, CONV_W // N_DEV, CONV_W)
        early = ([dwo, dpw], [_landing(own(dwo), me), _landing(own(dpw), me)],
                 [(0, True, 0, None), (1, True, 1, None)])
        started_early, token = _xchg_start(f"grads_start_out{l}", [early])
        dq, dgate, dk, dv, dsink = _attn_bwd(f"attn_bwd{l}", s["qr"], s["kr"], s["proj"], p["sinks"], dycat, token)
        dproj = _attn_assemble(f"attn_asm{l}", dq, dgate, dk, dv, tabs, dproj)
        dproj, dwl, dlvec, dwa, dwx = _lru_bwd(f"lru_bwd{l}", s["proj"], dycat, s["hstate"],
                                                p["wl"], p["lvec"], p["wa"], p["wx"], dproj)
        dwin_l[l] = _mm_dwin(f"dwin{l}", s["hb"], dproj)
        late = ([dwin_l[l]], [_landing(own(dwin_l[l]), me)], [(0, True, 0, None)])
        started_late, token = _xchg_start(f"grads_start_in{l}", [late])
        ggroups[l] = [(late, started_late[0]), (early, started_early[0])]
        dh = _mm_dh(f"dh{l}", dproj, wg_in[l], dz, token)
        if l > 0:
            dz, ln_stats[l] = _ln_bwd(f"ln_bwd{l}", dh, z[l], row(ln_g[l]))
        else:
            grad_x, gmeta, ln_stats[0] = _ln_bwd_input(dh, z[0], row(ln_g[0]))
        g_layers[l] = dict(dwdw=dwdw[:CONV_K], dcvec=dcvec, dwpw=dwpw, dsink=dsink[0, :N_HEADS],
                           dwl=dwl[:LRU_CONV_K], dlvec=dlvec, dwa=_diag_blocks(dwa), dwx=_diag_blocks(dwx))

    stack = lambda f: jnp.stack([f(g_layers[l]) for l in range(DEPTH)])
    g_local = dict(
        ln_in_g=ln_stats[0][0], ln_in_b=ln_stats[0][1],
        conv_dw_b=stack(lambda g: g["dcvec"][0]), conv_ln_g=stack(lambda g: g["dcvec"][1]),
        conv_ln_b=stack(lambda g: g["dcvec"][2]), conv_pw_b=stack(lambda g: g["dcvec"][3]),
        attn_sinks=stack(lambda g: g["dsink"]),
        lru_conv_b=stack(lambda g: g["dlvec"][0]), lru_wa=stack(lambda g: g["dwa"]),
        lru_ba=stack(lambda g: g["dlvec"][1]), lru_wx=stack(lambda g: g["dwx"]),
        lru_bx=stack(lambda g: g["dlvec"][2]), lru_lambda=stack(lambda g: g["dlvec"][3]),
        ln_post_g=jnp.stack([ln_stats[1][0], ln_stats[2][0]]),
        ln_post_b=jnp.stack([ln_stats[1][1], ln_stats[2][1]]),
    )
    rep_pack = _pack_rows([g_local[n] for n in rep_names])
    g_dw_full = jnp.stack([g_layers[l]["dwdw"] for l in range(DEPTH)])
    g_lc_full = jnp.stack([g_layers[l]["dwl"] for l in range(DEPTH)])
    shard_pack = _pack_rows([_cols_to_slots(g_dw_full), _cols_to_slots(g_lc_full), _cols_to_slots(gmeta)],
                            lead=(N_DEV,))
    sgroup = ([shard_pack, rep_pack], [_landing(own(shard_pack), me), _landing(rep_pack, me)],
              [(0, True, 0, None), (1, False, 1, None)])
    sstarted, token = _xchg_start("small_grads_start", [sgroup])

    res = {}

    def flat2(a, cols):
        return a.reshape(-1, cols)

    big = (("w_in", 0, W_IN_SHARD), ("w_out", 1, D_MODEL), ("conv_pw_w", 2, CONV_W))
    prev = {n: None for n, _, _ in big}
    after = token
    for l in reversed(range(DEPTH)):
        recvs = []
        for gi, (grp, started) in enumerate(ggroups[l]):
            recvs += _xchg_wait(f"grads_wait{l}_{gi}", grp, started, [after])
        for name_, gi, cols in big:
            w_ = weights[name_]
            rows = w_.shape[1]
            prev[name_] = _adamw(f"adamw_{name_}{l}", flat2(w_, cols), flat2(mom1[name_], cols),
                                 flat2(mom2[name_], cols), recvs[gi], row0=l * rows, prev=prev[name_])
        after = prev["w_in"][0]
    for name_, _, _ in big:
        res[name_] = [o.reshape(weights[name_].shape) for o in prev[name_]]

    r_small, r_rep = _xchg_wait("small_grads_wait", sgroup, sstarted[0], [prev[n][0] for n, _, _ in big])
    sshapes = [weights[n].shape for n in shard_small_names]
    outs = _adamw("adamw_small_sharded", *shard_wmv, r_small)
    for k, o in enumerate(outs):
        for n, a in zip(shard_small_names, _unpack_rows(o, sshapes)):
            res.setdefault(n, [None] * 4)[k] = a

    rshapes = [weights[n].shape for n in rep_names]
    outs = _adamw("adamw_replicated", *rep_wmv, r_rep)
    for k, o in enumerate(outs):
        for n, a in zip(rep_names, _unpack_rows(o, rshapes)):
            res.setdefault(n, [None] * 4)[k] = a

    order = ["meta_tokens", "ln_in_g", "ln_in_b", "w_in", "conv_dw_w", "conv_dw_b", "conv_ln_g", "conv_ln_b",
             "conv_pw_w", "conv_pw_b", "attn_sinks", "lru_conv_w", "lru_conv_b", "lru_wa", "lru_ba", "lru_wx",
             "lru_bx", "lru_lambda", "w_out", "ln_post_g", "ln_post_b"]
    return (loss, grad_x,
            *[res[n][0] for n in order], *[res[n][1] for n in order],
            *[res[n][2] for n in order], *[res[n][3] for n in order])
```

```python
import functools
import math

import numpy as np
import jax
import jax.numpy as jnp
from jax import lax
from jax.experimental import pallas as pl
from jax.experimental.pallas import tpu as pltpu

F32 = jnp.float32
BF16 = jnp.bfloat16

D_MODEL = 2048
DEPTH = 2
N_META = 16
TB = 128
PAD0 = TB - N_META
CONV_W = 512
CONV_K = 31
HEAD_DIM = 64
N_HEADS = 16
N_KV = 4
GROUP = 4
ATT_W = 1024
KV_W = 256
ROT_DIM = 16
ROPE_THETA = 500000.0
LRU_W = 512
LRU_HEADS = 8
LRU_CONV_K = 4
LRU_C = 8.0
IN_TOTAL = 5120
N_DEV = 8
W_IN_SHARD = IN_TOTAL // N_DEV
LN_EPS = 1e-5
ALPHA = (2.0 * DEPTH) ** 0.25
NEG_INF = -1e30
ATT_SCALE = HEAD_DIM ** -0.5

ADAM_LR = 0.001
ADAM_B1 = 0.9
ADAM_B2 = 0.999
ADAM_EPS = 1e-08
ADAM_WD = 0.01
ADAM_STEP = 10

VMEM_LIMIT = 56 * 1024 * 1024
PACK_QUANTUM = 256 * 128

COL_CV, COL_CG, COL_CGATE = 0, 1, 2
COL_Q0 = 3
COL_K256 = 10
COL_V256 = 11
COL_AGATE1024 = 3
COL_RX, COL_RGATE = 8, 9
YC_CONV, YC_LRU = 2, 3


def _cp(n_axes, vmem=VMEM_LIMIT):
    return pltpu.CompilerParams(dimension_semantics=("arbitrary",) * n_axes, vmem_limit_bytes=vmem)


def _row_tile(lp, max_blocks):
    nb = lp // TB
    d = max(k for k in range(1, max_blocks + 1) if nb % k == 0)
    return TB * d


def _sig(x):
    return jax.nn.sigmoid(x)


def _dsilu(x, s):
    return s * (1.0 + x * (1.0 - s))


def _ln_core(z):
    mu = jnp.mean(z, axis=-1, keepdims=True)
    zc = z - mu
    var = jnp.mean(zc * zc, axis=-1, keepdims=True)
    rstd = lax.rsqrt(var + LN_EPS)
    return zc * rstd, rstd


def _ln_bwd_core(dy, xh, rstd, g):
    dxh = dy * g
    m1 = jnp.mean(dxh, axis=-1, keepdims=True)
    m2 = jnp.mean(dxh * xh, axis=-1, keepdims=True)
    return rstd * (dxh - m1 - xh * m2)


def _row_ids(shape, base):
    return lax.broadcasted_iota(jnp.int32, shape, 0) + base


def _colsum(x):
    return jnp.sum(x, axis=0, keepdims=True)


def _dot(a, b, dims):
    return lax.dot_general(a, b, (dims, ((), ())), preferred_element_type=F32)


NN = ((1,), (0,))
NT = ((1,), (1,))
TN = ((0,), (0,))


def _embed(x, meta_full, g, b):
    s = x.shape[1]
    lp = s + TB
    nb = lp // TB

    def body(x_ref, m_ref, g_ref, b_ref, o_ref, hb_ref):
        i = pl.program_id(0)

        @pl.when(i == 0)
        def _():
            o_ref[0:PAD0, :] = jnp.zeros((PAD0, D_MODEL), F32)
            o_ref[PAD0:TB, :] = m_ref[...]

        @pl.when(i > 0)
        def _():
            o_ref[...] = x_ref[...]

        xh, _ = _ln_core(o_ref[...])
        h = xh * g_ref[...] + b_ref[...]
        rows = _row_ids(h.shape, i * TB)
        hb_ref[...] = jnp.where(rows >= PAD0, h, 0.0).astype(BF16)

    return pl.pallas_call(
        body, name="embed", grid=(nb,),
        in_specs=[pl.BlockSpec((None, TB, D_MODEL), lambda i: (0, jnp.maximum(i - 1, 0), 0)),
                  pl.BlockSpec((N_META, D_MODEL), lambda i: (0, 0)),
                  pl.BlockSpec((1, D_MODEL), lambda i: (0, 0)),
                  pl.BlockSpec((1, D_MODEL), lambda i: (0, 0))],
        out_specs=[pl.BlockSpec((TB, D_MODEL), lambda i: (i, 0)),
                   pl.BlockSpec((TB, D_MODEL), lambda i: (i, 0))],
        out_shape=[jax.ShapeDtypeStruct((lp, D_MODEL), F32),
                   jax.ShapeDtypeStruct((lp, D_MODEL), BF16)],
        compiler_params=_cp(1),
    )(x, meta_full, g, b)


def _loss_head(z, target, g, b):
    lp = z.shape[0]
    nb = lp // TB

    def body(z_ref, t_ref, g_ref, b_ref, dz_ref, st_ref, loss_ref):
        i = pl.program_id(0)

        @pl.when(i == 0)
        def _():
            st_ref[...] = jnp.zeros(st_ref.shape, F32)
            loss_ref[...] = jnp.zeros(loss_ref.shape, F32)
            dz_ref[...] = jnp.zeros(dz_ref.shape, F32)

        @pl.when(i > 0)
        def _():
            xh, rstd = _ln_core(z_ref[...])
            gg = g_ref[...]
            y = xh * gg + b_ref[...]
            e = y - t_ref[...]
            part = 0.5 * jnp.sum(jnp.mean(e * e, axis=-1, keepdims=True), axis=0, keepdims=True)
            loss_ref[...] += jnp.broadcast_to(part, loss_ref.shape)
            dy = e / float(D_MODEL)
            st_ref[0:1, :] += _colsum(dy * xh)
            st_ref[1:2, :] += _colsum(dy)
            dz_ref[...] = _ln_bwd_core(dy, xh, rstd, gg)

    return pl.pallas_call(
        body, name="loss_head", grid=(nb,),
        in_specs=[pl.BlockSpec((TB, D_MODEL), lambda i: (i, 0)),
                  pl.BlockSpec((None, TB, D_MODEL), lambda i: (0, jnp.maximum(i - 1, 0), 0)),
                  pl.BlockSpec((1, D_MODEL), lambda i: (0, 0)),
                  pl.BlockSpec((1, D_MODEL), lambda i: (0, 0))],
        out_specs=[pl.BlockSpec((TB, D_MODEL), lambda i: (i, 0)),
                   pl.BlockSpec((8, D_MODEL), lambda i: (0, 0)),
                   pl.BlockSpec((8, 128), lambda i: (0, 0))],
        out_shape=[jax.ShapeDtypeStruct((lp, D_MODEL), F32),
                   jax.ShapeDtypeStruct((8, D_MODEL), F32),
                   jax.ShapeDtypeStruct((8, 128), F32)],
        compiler_params=_cp(1),
    )(z, target, g, b)


def _ln_bwd(name, dh, z, g):
    lp = z.shape[0]
    nb = lp // TB

    def body(dh_ref, z_ref, g_ref, dz_ref, st_ref):
        i = pl.program_id(0)

        @pl.when(i == 0)
        def _():
            st_ref[...] = jnp.zeros(st_ref.shape, F32)

        xh, rstd = _ln_core(z_ref[...])
        rows = _row_ids(xh.shape, i * TB)
        dy = jnp.where(rows >= PAD0, dh_ref[...], 0.0)
        st_ref[0:1, :] += _colsum(dy * xh)
        st_ref[1:2, :] += _colsum(dy)
        dz_ref[...] = _ln_bwd_core(dy, xh, rstd, g_ref[...])

    return pl.pallas_call(
        body, name=name, grid=(nb,),
        in_specs=[pl.BlockSpec((TB, D_MODEL), lambda i: (i, 0)),
                  pl.BlockSpec((TB, D_MODEL), lambda i: (i, 0)),
                  pl.BlockSpec((1, D_MODEL), lambda i: (0, 0))],
        out_specs=[pl.BlockSpec((TB, D_MODEL), lambda i: (i, 0)),
                   pl.BlockSpec((8, D_MODEL), lambda i: (0, 0))],
        out_shape=[jax.ShapeDtypeStruct((lp, D_MODEL), F32),
                   jax.ShapeDtypeStruct((8, D_MODEL), F32)],
        compiler_params=_cp(1),
    )(dh, z, g)


def _ln_bwd_input(dh, z, g):
    lp = z.shape[0]
    nb = lp // TB
    s = lp - TB

    def body(dh_ref, z_ref, g_ref, gx_ref, gm_ref, st_ref):
        i = pl.program_id(0)

        @pl.when(i == 0)
        def _():
            st_ref[...] = jnp.zeros(st_ref.shape, F32)

        xh, rstd = _ln_core(z_ref[...])
        rows = _row_ids(xh.shape, i * TB)
        dy = jnp.where(rows >= PAD0, dh_ref[...], 0.0)
        st_ref[0:1, :] += _colsum(dy * xh)
        st_ref[1:2, :] += _colsum(dy)
        dz = _ln_bwd_core(dy, xh, rstd, g_ref[...])
        gx_ref[...] = dz

        @pl.when(i == 0)
        def _():
            gm_ref[...] = dz[PAD0:TB, :]

    return pl.pallas_call(
        body, name="ln_in_bwd", grid=(nb,),
        in_specs=[pl.BlockSpec((TB, D_MODEL), lambda i: (i, 0)),
                  pl.BlockSpec((TB, D_MODEL), lambda i: (i, 0)),
                  pl.BlockSpec((1, D_MODEL), lambda i: (0, 0))],
        out_specs=[pl.BlockSpec((None, TB, D_MODEL), lambda i: (0, jnp.maximum(i - 1, 0), 0)),
                   pl.BlockSpec((N_META, D_MODEL), lambda i: (0, 0)),
                   pl.BlockSpec((8, D_MODEL), lambda i: (0, 0))],
        out_shape=[jax.ShapeDtypeStruct((1, s, D_MODEL), F32),
                   jax.ShapeDtypeStruct((N_META, D_MODEL), F32),
                   jax.ShapeDtypeStruct((8, D_MODEL), F32)],
        compiler_params=_cp(1),
    )(dh, z, g)


def _mm_proj(name, hb, wg_in):
    lp = hb.shape[0]
    tm = lp // 3

    def body(a_ref, b_ref, o_ref):
        b = jnp.concatenate([b_ref[0], b_ref[1]], axis=1)
        o_ref[...] = _dot(a_ref[...], b, NN)

    return pl.pallas_call(
        body, name=name, grid=(3, N_DEV // 2),
        in_specs=[pl.BlockSpec((tm, D_MODEL), lambda i, j: (i, 0)),
                  pl.BlockSpec((2, D_MODEL, W_IN_SHARD), lambda i, j: (j, 0, 0))],
        out_specs=pl.BlockSpec((tm, 2 * W_IN_SHARD), lambda i, j: (i, j)),
        out_shape=jax.ShapeDtypeStruct((lp, IN_TOTAL), F32),
        compiler_params=_cp(2),
    )(hb, wg_in)


def _mm_out(name, ycat, wout, z, g, b, g2, b2):
    lp = ycat.shape[0]
    tm = lp // 6

    def body(a_ref, w_ref, z_ref, g_ref, b_ref, g2_ref, b2_ref, o_ref, hb_ref):
        i = pl.program_id(0)
        xh, _ = _ln_core(z_ref[...])
        h = xh * g_ref[...] + b_ref[...]
        live = _row_ids(h.shape, i * tm) >= PAD0
        h = jnp.where(live, h, 0.0)
        zn = ALPHA * h + _dot(a_ref[...], w_ref[...], NN)
        o_ref[...] = zn
        xh2, _ = _ln_core(zn)
        hb_ref[...] = jnp.where(live, xh2 * g2_ref[...] + b2_ref[...], 0.0).astype(BF16)

    vec = pl.BlockSpec((1, D_MODEL), lambda i: (0, 0))
    return pl.pallas_call(
        body, name=name, grid=(6,),
        in_specs=[pl.BlockSpec((tm, D_MODEL), lambda i: (i, 0)),
                  pl.BlockSpec((D_MODEL, D_MODEL), lambda i: (0, 0), pipeline_mode=pl.Buffered(1)),
                  pl.BlockSpec((tm, D_MODEL), lambda i: (i, 0)),
                  vec, vec, vec, vec],
        out_specs=[pl.BlockSpec((tm, D_MODEL), lambda i: (i, 0)),
                   pl.BlockSpec((tm, D_MODEL), lambda i: (i, 0))],
        out_shape=[jax.ShapeDtypeStruct((lp, D_MODEL), F32),
                   jax.ShapeDtypeStruct((lp, D_MODEL), BF16)],
        compiler_params=_cp(1),
    )(ycat, wout, z, g, b, g2, b2)


def _mm_dycat(name, dz, wout, dep):
    lp = dz.shape[0]
    tm = lp // 6

    def body(a_ref, w_ref, dep_ref, o_ref):
        del dep_ref
        o_ref[...] = _dot(a_ref[...].astype(BF16), w_ref[...], NT).astype(BF16)

    return pl.pallas_call(
        body, name=name, grid=(6,),
        in_specs=[pl.BlockSpec((tm, D_MODEL), lambda i: (i, 0)),
                  pl.BlockSpec((D_MODEL, D_MODEL), lambda i: (0, 0), pipeline_mode=pl.Buffered(1)),
                  pl.BlockSpec(memory_space=pl.ANY)],
        out_specs=pl.BlockSpec((tm, D_MODEL), lambda i: (i, 0)),
        out_shape=jax.ShapeDtypeStruct((lp, D_MODEL), BF16),
        compiler_params=_cp(1),
    )(dz, wout, dep)


def _mm_dwout(name, ycat, dz):
    lp = ycat.shape[0]
    tk = _row_tile(lp, 11)
    nk = lp // tk
    half = D_MODEL // 2

    def body(a_ref, b_ref, o_ref, acc_ref):
        k = pl.program_id(1)

        @pl.when(k == 0)
        def _():
            acc_ref[...] = jnp.zeros(acc_ref.shape, F32)

        acc_ref[...] += _dot(a_ref[...], b_ref[...].astype(BF16), TN)

        @pl.when(k == nk - 1)
        def _():
            o_ref[...] = acc_ref[...].astype(BF16)

    return pl.pallas_call(
        body, name=name, grid=(2, nk),
        in_specs=[pl.BlockSpec((tk, half), lambda h, k: (k, h)),
                  pl.BlockSpec((tk, D_MODEL), lambda h, k: (k, 0))],
        out_specs=pl.BlockSpec((half, D_MODEL), lambda h, k: (h, 0)),
        out_shape=jax.ShapeDtypeStruct((D_MODEL, D_MODEL), BF16),
        scratch_shapes=[pltpu.VMEM((half, D_MODEL), F32)],
        compiler_params=_cp(2),
    )(ycat, dz)


def _mm_dwin(name, hb, dproj):
    lp = hb.shape[0]
    tk = _row_tile(lp, 11)
    nk = lp // tk

    def body(a_ref, b_ref, o_ref, acc_ref):
        k = pl.program_id(1)

        @pl.when(k == 0)
        def _():
            acc_ref[...] = jnp.zeros(acc_ref.shape, F32)

        acc_ref[...] += _dot(a_ref[...], b_ref[...], TN)

        @pl.when(k == nk - 1)
        def _():
            o_ref[0] = acc_ref[:, 0:W_IN_SHARD].astype(BF16)
            o_ref[1] = acc_ref[:, W_IN_SHARD:2 * W_IN_SHARD].astype(BF16)

    return pl.pallas_call(
        body, name=name, grid=(4, nk),
        in_specs=[pl.BlockSpec((tk, D_MODEL), lambda j, k: (k, 0)),
                  pl.BlockSpec((tk, 2 * W_IN_SHARD), lambda j, k: (k, j))],
        out_specs=pl.BlockSpec((2, D_MODEL, W_IN_SHARD), lambda j, k: (j, 0, 0)),
        out_shape=jax.ShapeDtypeStruct((N_DEV, D_MODEL, W_IN_SHARD), BF16),
        scratch_shapes=[pltpu.VMEM((D_MODEL, 2 * W_IN_SHARD), F32)],
        compiler_params=_cp(2),
    )(hb, dproj)


def _mm_dh(name, dproj, wg_in, dz, dep):
    lp = dproj.shape[0]
    tm = lp // 6

    def body(a_ref, w_ref, dz_ref, dep_ref, o_ref, acc_ref):
        del dep_ref
        k = pl.program_id(1)

        @pl.when(k == 0)
        def _():
            acc_ref[...] = jnp.zeros(acc_ref.shape, F32)

        w = jnp.concatenate([w_ref[0], w_ref[1]], axis=1)
        acc_ref[...] += _dot(a_ref[...], w, NT)

        @pl.when(k == N_DEV // 2 - 1)
        def _():
            o_ref[...] = acc_ref[...] + ALPHA * dz_ref[...]

    return pl.pallas_call(
        body, name=name, grid=(6, N_DEV // 2),
        in_specs=[pl.BlockSpec((tm, 2 * W_IN_SHARD), lambda i, k: (i, k)),
                  pl.BlockSpec((2, D_MODEL, W_IN_SHARD), lambda i, k: (k, 0, 0)),
                  pl.BlockSpec((tm, D_MODEL), lambda i, k: (i, 0)),
                  pl.BlockSpec(memory_space=pl.ANY)],
        out_specs=pl.BlockSpec((tm, D_MODEL), lambda i, k: (i, 0)),
        out_shape=jax.ShapeDtypeStruct((lp, D_MODEL), F32),
        scratch_shapes=[pltpu.VMEM((tm, D_MODEL), F32)],
        compiler_params=_cp(2),
    )(dproj, wg_in, dz, dep)


def _shifted_views(cat, n_shift, base, rows):
    total = cat.shape[0]
    rolled = [cat] + [pltpu.roll(cat, b, axis=0) for b in range(1, 8)]
    views = []
    for s in range(n_shift):
        a, b = divmod(s, 8)
        views.append(rolled[b][base - 8 * a: base - 8 * a + rows, :])
    del total
    return views


CONV_HALO = 32


def _conv_chain(j, tb, cv_ref, cg_ref, cvp_ref, cgp_ref, wdw_ref, vec_ref, wpw_ref, c1_ref=None):
    cv = cv_ref[...]
    sg = _sig(cg_ref[...])
    c0 = cv * sg
    c0p = jnp.where(j > 0, cvp_ref[...] * _sig(cgp_ref[...]), 0.0)
    cat = jnp.concatenate([c0p, c0], axis=0)
    views = _shifted_views(cat, CONV_K, CONV_HALO, tb)
    if c1_ref is None:
        c1 = jnp.broadcast_to(vec_ref[0:1, :], (tb, CONV_W))
        for k in range(CONV_K):
            c1 = c1 + wdw_ref[k:k + 1, :] * views[CONV_K - 1 - k]
    else:
        c1 = c1_ref[...]
    xh, rstd = _ln_core(c1)
    c2 = xh * vec_ref[1:2, :] + vec_ref[2:3, :]
    s2 = _sig(c2)
    c3 = c2 * s2
    c4 = _dot(c3.astype(BF16), wpw_ref[...], NN) + vec_ref[3:4, :]
    return dict(cv=cv, sg=sg, views=views, c1=c1, xh=xh, rstd=rstd, c2=c2, s2=s2, c3=c3, c4=c4)


def _conv_in_specs(jmap, tb):
    def cur(col):
        return pl.BlockSpec((tb, 512), lambda n: (jmap(n), col))

    def prev(col):
        return pl.BlockSpec((CONV_HALO, 512),
                            lambda n: (jnp.maximum(jmap(n) * (tb // CONV_HALO) - 1, 0), col))

    return [cur(COL_CV), cur(COL_CG), prev(COL_CV), prev(COL_CG), cur(COL_CGATE)]


def _conv_param_specs():
    return [pl.BlockSpec((32, CONV_W), lambda n: (0, 0)),
            pl.BlockSpec((8, CONV_W), lambda n: (0, 0)),
            pl.BlockSpec((CONV_W, CONV_W), lambda n: (0, 0))]


def _conv_fwd(name, proj, wdw, vec, wpw):
    lp = proj.shape[0]
    tb = _row_tile(lp, 3)
    nb = lp // tb

    def body(cv_ref, cg_ref, cvp_ref, cgp_ref, gate_ref, wdw_ref, vec_ref, wpw_ref, o_ref, c1_ref):
        j = pl.program_id(0)
        c = _conv_chain(j, tb, cv_ref, cg_ref, cvp_ref, cgp_ref, wdw_ref, vec_ref, wpw_ref)
        gate = gate_ref[...]
        o_ref[...] = (c["c4"] * (gate * _sig(gate))).astype(BF16)
        c1_ref[...] = c["c1"]

    return pl.pallas_call(
        body, name=name, grid=(nb,),
        in_specs=_conv_in_specs(lambda n: n, tb) + _conv_param_specs(),
        out_specs=[pl.BlockSpec((tb, 512), lambda n: (n, YC_CONV)),
                   pl.BlockSpec((tb, CONV_W), lambda n: (n, 0))],
        out_shape=[jax.ShapeDtypeStruct((lp, D_MODEL), BF16),
                   jax.ShapeDtypeStruct((lp, CONV_W), F32)],
        compiler_params=_cp(1),
    )(proj, proj, proj, proj, proj, wdw, vec, wpw)


def _conv_bwd(name, proj, dycat, c1, wdw, vec, wpw):
    lp = proj.shape[0]
    tb = _row_tile(lp, 3)
    nb = lp // tb
    halo = CONV_HALO

    def body(cv_ref, cg_ref, cvp_ref, cgp_ref, gate_ref, dy_ref, c1_ref, wdw_ref, vec_ref, wpw_ref,
             dp_ref, dwdw_ref, dvec_ref, dwpw_ref, carry_ref):
        n = pl.program_id(0)
        j = nb - 1 - n

        @pl.when(n == 0)
        def _():
            carry_ref[...] = jnp.zeros(carry_ref.shape, F32)
            dwdw_ref[...] = jnp.zeros(dwdw_ref.shape, F32)
            dvec_ref[...] = jnp.zeros(dvec_ref.shape, F32)
            dwpw_ref[...] = jnp.zeros(dwpw_ref.shape, F32)

        c = _conv_chain(j, tb, cv_ref, cg_ref, cvp_ref, cgp_ref, wdw_ref, vec_ref, wpw_ref, c1_ref)
        dy = dy_ref[...].astype(F32)
        gate = gate_ref[...]
        sgate = _sig(gate)
        dc4 = dy * (gate * sgate)
        dgate = dy * c["c4"] * _dsilu(gate, sgate)
        dc4b = dc4.astype(BF16)
        dvec_ref[3:4, :] += _colsum(dc4)
        dwpw_ref[...] += _dot(c["c3"].astype(BF16), dc4b, TN)
        dc3 = _dot(dc4b, wpw_ref[...], NT)
        dc2 = dc3 * _dsilu(c["c2"], c["s2"])
        dvec_ref[1:2, :] += _colsum(dc2 * c["xh"])
        dvec_ref[2:3, :] += _colsum(dc2)
        dc1 = _ln_bwd_core(dc2, c["xh"], c["rstd"], vec_ref[1:2, :])
        dvec_ref[0:1, :] += _colsum(dc1)
        for k in range(CONV_K):
            dwdw_ref[k:k + 1, :] += _colsum(dc1 * c["views"][CONV_K - 1 - k])
        dcat = jnp.concatenate([dc1, carry_ref[...]], axis=0)
        total = tb + halo
        up = [dcat] + [pltpu.roll(dcat, total - b, axis=0) for b in range(1, 8)]
        dc0 = jnp.zeros((tb, CONV_W), F32)
        for k in range(CONV_K):
            a, b = divmod(CONV_K - 1 - k, 8)
            dc0 = dc0 + wdw_ref[k:k + 1, :] * up[b][8 * a: 8 * a + tb, :]
        carry_ref[...] = dc1[0:halo, :]
        sg = c["sg"]
        dcv = dc0 * sg
        dcg = dc0 * c["cv"] * sg * (1.0 - sg)
        dp_ref[:, 0:512] = dcv.astype(BF16)
        dp_ref[:, 512:1024] = dcg.astype(BF16)
        dp_ref[:, 1024:1536] = dgate.astype(BF16)

    jmap = lambda n: nb - 1 - n
    return pl.pallas_call(
        body, name=name, grid=(nb,),
        in_specs=(_conv_in_specs(jmap, tb)
                  + [pl.BlockSpec((tb, 512), lambda n: (jmap(n), YC_CONV)),
                     pl.BlockSpec((tb, CONV_W), lambda n: (jmap(n), 0))]
                  + _conv_param_specs()),
        out_specs=[pl.BlockSpec((tb, 1536), lambda n: (jmap(n), 0)),
                   pl.BlockSpec((32, CONV_W), lambda n: (0, 0)),
                   pl.BlockSpec((8, CONV_W), lambda n: (0, 0)),
                   pl.BlockSpec((CONV_W, CONV_W), lambda n: (0, 0))],
        out_shape=[jax.ShapeDtypeStruct((lp, IN_TOTAL), BF16),
                   jax.ShapeDtypeStruct((32, CONV_W), F32),
                   jax.ShapeDtypeStruct((8, CONV_W), F32),
                   jax.ShapeDtypeStruct((CONV_W, CONV_W), F32)],
        scratch_shapes=[pltpu.VMEM((halo, CONV_W), F32)],
        compiler_params=_cp(1),
    )(proj, proj, proj, proj, proj, dycat, c1, wdw, vec, wpw)


def _rope_tables(lp):
    half = ROT_DIM // 2
    inv_freq = ROPE_THETA ** (-jnp.arange(half, dtype=F32) / half)
    pos = (jnp.arange(lp, dtype=jnp.int32) - PAD0).astype(F32)
    ang = pos[:, None] * inv_freq[None, :]
    cos, sin = jnp.cos(ang), jnp.sin(ang)
    ones = jnp.ones((lp, HEAD_DIM - ROT_DIM), F32)
    zeros = jnp.zeros((lp, HEAD_DIM - ROT_DIM), F32)
    zh = jnp.zeros((lp, half), F32)
    c = jnp.concatenate([cos, cos, ones], axis=1)
    sa = jnp.concatenate([-sin, zh, zeros], axis=1)
    sb = jnp.concatenate([zh, sin, zeros], axis=1)
    tile = lambda t: jnp.tile(t, (1, KV_W // HEAD_DIM))
    return tile(c), tile(sa), tile(sb)


def _rot(x, c, sa, sb):
    w = x.shape[1]
    return x * c + pltpu.roll(x, w - 8, axis=1) * sa + pltpu.roll(x, 8, axis=1) * sb


def _rot_t(dy, c, sa, sb):
    w = dy.shape[1]
    return dy * c + pltpu.roll(dy * sa, 8, axis=1) + pltpu.roll(dy * sb, w - 8, axis=1)


def _rope_fwd(name, proj, tabs):
    lp = proj.shape[0]
    tr = _row_tile(lp, 3)

    def body(q0_ref, q1_ref, k_ref, c_ref, sa_ref, sb_ref, qr_ref, kr_ref):
        c, sa, sb = c_ref[...], sa_ref[...], sb_ref[...]
        c2 = jnp.concatenate([c, c], axis=1)
        sa2 = jnp.concatenate([sa, sa], axis=1)
        sb2 = jnp.concatenate([sb, sb], axis=1)
        qr_ref[:, 0:512] = (_rot(q0_ref[...], c2, sa2, sb2) * ATT_SCALE).astype(BF16)
        qr_ref[:, 512:1024] = (_rot(q1_ref[...], c2, sa2, sb2) * ATT_SCALE).astype(BF16)
        kr_ref[...] = _rot(k_ref[...], c, sa, sb).astype(BF16)

    tab = pl.BlockSpec((tr, KV_W), lambda i: (i, 0))
    return pl.pallas_call(
        body, name=name, grid=(lp // tr,),
        in_specs=[pl.BlockSpec((tr, 512), lambda i: (i, COL_Q0)),
                  pl.BlockSpec((tr, 512), lambda i: (i, COL_Q0 + 1)),
                  pl.BlockSpec((tr, KV_W), lambda i: (i, COL_K256)),
                  tab, tab, tab],
        out_specs=[pl.BlockSpec((tr, ATT_W), lambda i: (i, 0)),
                   pl.BlockSpec((tr, KV_W), lambda i: (i, 0))],
        out_shape=[jax.ShapeDtypeStruct((lp, ATT_W), BF16),
                   jax.ShapeDtypeStruct((lp, KV_W), BF16)],
        compiler_params=_cp(1),
    )(proj, proj, proj, *tabs)


def _attn_mask(j):
    qi = lax.broadcasted_iota(jnp.int32, (GROUP * TB, 3 * TB), 0) & (TB - 1)
    cc = lax.broadcasted_iota(jnp.int32, (GROUP * TB, 3 * TB), 1)
    jj = cc & (TB - 1)
    is_meta = jj >= PAD0
    p0 = (cc < TB) & is_meta & (j >= 1)
    p1 = (cc >= TB) & (cc < 2 * TB) & (jj > qi) & (j >= 2)
    p2 = (cc >= 2 * TB) & (jj <= qi) & ((j >= 1) | is_meta)
    return p0 | p1 | p2


def _lane_group(rows):
    return lax.broadcasted_iota(jnp.int32, (rows, KV_W), 1) // HEAD_DIM


def _stack_heads(x, kv, lgq):
    parts = []
    for g in range(GROUP):
        sh = ((kv - g) % GROUP) * HEAD_DIM
        moved = x if sh == 0 else pltpu.roll(x, sh, axis=1)
        parts.append(jnp.where(lgq == kv, moved, 0.0))
    return jnp.concatenate(parts, axis=0).astype(BF16)


def _unstack_heads(r, kv):
    out = None
    for g in range(GROUP):
        blk = r[g * TB:(g + 1) * TB, :]
        sh = ((g - kv) % GROUP) * HEAD_DIM
        blk = blk if sh == 0 else pltpu.roll(blk, sh, axis=1)
        out = blk if out is None else out + blk
    return out


def _sink_column(sinks, kv):
    lane = lax.broadcasted_iota(jnp.int32, (1, 128), 1)
    cols = []
    for g in range(GROUP):
        sg = jnp.sum(jnp.where(lane == kv * GROUP + g, sinks, 0.0), axis=1, keepdims=True)
        cols.append(jnp.broadcast_to(sg, (TB, 1)))
    return jnp.concatenate(cols, axis=0)


def _attn_kv(kall, vall, lg, kv):
    km = jnp.where(lg == kv, kall, 0.0).astype(BF16)
    ones = jnp.where(lg == (kv + 1) % N_KV, 1.0, 0.0)
    vm = jnp.where(lg == kv, vall, ones).astype(BF16)
    return km, vm


def _attn_weights(qst, km, vm, sinkcol, valid, lg4, kv):
    s = jnp.where(valid, _dot(qst, km, NT), NEG_INF)
    m = jnp.maximum(jnp.max(s, axis=-1, keepdims=True), sinkcol)
    eb = jnp.exp(s - m).astype(BF16)
    es = jnp.exp(sinkcol - m)
    r = _dot(eb, vm, NN)
    rowsum = pltpu.roll(r, KV_W - HEAD_DIM, axis=1)
    inv = 1.0 / (rowsum + es)
    out = jnp.where(lg4 == kv, r * inv, 0.0)
    return eb, es, inv, out


def _attn_specs(jmap):
    blk = lambda col: pl.BlockSpec((TB, KV_W), lambda n: (jmap(n), col))
    prv = lambda col: pl.BlockSpec((TB, KV_W), lambda n: (jnp.maximum(jmap(n) - 1, 0), col))
    met = lambda col: pl.BlockSpec((TB, KV_W), lambda n: (0, col))
    return dict(
        qr=pl.BlockSpec((TB, ATT_W), lambda n: (jmap(n), 0)),
        k=[met(0), prv(0), blk(0)],
        v=[met(COL_V256), prv(COL_V256), blk(COL_V256)],
        gate=pl.BlockSpec((TB, ATT_W), lambda n: (jmap(n), COL_AGATE1024)),
        sinks=pl.BlockSpec((8, 128), lambda n: (0, 0)),
    )


def _attn_fwd(name, qr, kr, proj, sinks_row, ycat):
    lp = proj.shape[0]
    nb = lp // TB
    sp = _attn_specs(lambda n: n)

    def body(qr_ref, km_ref, kp_ref, kc_ref, vm_ref, vp_ref, vc_ref, gate_ref, sink_ref, yin_ref, o_ref):
        del yin_ref
        j = pl.program_id(0)
        valid = _attn_mask(j)
        kall = jnp.concatenate([km_ref[...], kp_ref[...], kc_ref[...]], axis=0).astype(F32)
        vall = jnp.concatenate([vm_ref[...], vp_ref[...], vc_ref[...]], axis=0)
        lg = _lane_group(3 * TB)
        lgq = _lane_group(TB)
        lg4 = _lane_group(GROUP * TB)
        sinks = sink_ref[0:1, :]
        for kv in range(N_KV):
            cols = slice(kv * KV_W, (kv + 1) * KV_W)
            km, vm = _attn_kv(kall, vall, lg, kv)
            qst = _stack_heads(qr_ref[:, cols].astype(F32), kv, lgq)
            _, _, _, out = _attn_weights(qst, km, vm, _sink_column(sinks, kv), valid, lg4, kv)
            att = _unstack_heads(out, kv)
            gate = gate_ref[:, cols]
            o_ref[:, cols] = (att * (gate * _sig(gate))).astype(BF16)

    return pl.pallas_call(
        body, name=name, grid=(nb,),
        in_specs=[sp["qr"]] + sp["k"] + sp["v"] + [sp["gate"], sp["sinks"],
                                                   pl.BlockSpec(memory_space=pl.ANY)],
        out_specs=pl.BlockSpec((TB, ATT_W), lambda n: (n, 0)),
        out_shape=jax.ShapeDtypeStruct((lp, D_MODEL), BF16),
        input_output_aliases={9: 0},
        compiler_params=_cp(1),
    )(qr, kr, kr, kr, proj, proj, proj, proj, sinks_row, ycat)


def _attn_bwd(name, qr, kr, proj, sinks_row, dycat, dep):
    lp = proj.shape[0]
    nb = lp // TB
    sp = _attn_specs(lambda n: n)

    def body(qr_ref, km_ref, kp_ref, kc_ref, vm_ref, vp_ref, vc_ref, gate_ref, sink_ref, dy_ref, dep_ref,
             dq_ref, dgate_ref, dk_ref, dv_ref, dsink_ref):
        del dep_ref
        j = pl.program_id(0)

        @pl.when(j == 0)
        def _():
            dk_ref[...] = jnp.zeros(dk_ref.shape, F32)
            dv_ref[...] = jnp.zeros(dv_ref.shape, F32)
            dsink_ref[...] = jnp.zeros(dsink_ref.shape, F32)

        valid = _attn_mask(j)
        kall = jnp.concatenate([km_ref[...], kp_ref[...], kc_ref[...]], axis=0).astype(F32)
        vall = jnp.concatenate([vm_ref[...], vp_ref[...], vc_ref[...]], axis=0)
        lg = _lane_group(3 * TB)
        lgq = _lane_group(TB)
        lg4 = _lane_group(GROUP * TB)
        sinks = sink_ref[0:1, :]
        lane = lax.broadcasted_iota(jnp.int32, (1, 128), 1)
        dkall = jnp.zeros((3 * TB, KV_W), F32)
        dvall = jnp.zeros((3 * TB, KV_W), F32)
        dsink = jnp.zeros((1, 128), F32)
        for kv in range(N_KV):
            cols = slice(kv * KV_W, (kv + 1) * KV_W)
            km, vm = _attn_kv(kall, vall, lg, kv)
            qst = _stack_heads(qr_ref[:, cols].astype(F32), kv, lgq)
            gate = gate_ref[:, cols]
            sgate = _sig(gate)
            dy = dy_ref[:, cols].astype(F32)
            dout = dy * (gate * sgate)
            eb, es, inv, out = _attn_weights(qst, km, vm, _sink_column(sinks, kv), valid, lg4, kv)
            att = _unstack_heads(out, kv)
            dgate_ref[:, cols] = (dy * att * _dsilu(gate, sgate)).astype(BF16)
            dsc = dout * _unstack_heads(jnp.where(lg4 == kv, inv, 0.0), kv)
            dost = _stack_heads(dsc, kv, lgq)
            dd = dsc * att
            dcol = jnp.concatenate(
                [jnp.sum(jnp.where(lgq == g, dd, 0.0), axis=1, keepdims=True) for g in range(GROUP)], axis=0)
            dp = _dot(dost, vm, NT)
            ds = (eb.astype(F32) * (dp - dcol)).astype(BF16)
            pd = es * dcol
            for g in range(GROUP):
                tot = jnp.sum(pd[g * TB:(g + 1) * TB, :], axis=0, keepdims=True)
                dsink = dsink - jnp.where(lane == kv * GROUP + g, tot, 0.0)
            dq_ref[:, cols] = _unstack_heads(_dot(ds, km, NN), kv)
            dkall = dkall + _dot(ds, qst, TN)
            dvall = dvall + _dot(eb, dost, TN)
        dsink_ref[0:1, :] += dsink
        prev = pl.multiple_of(jnp.maximum(j - 1, 0) * TB, TB)
        cur = pl.multiple_of(j * TB, TB)
        dk_ref[0:TB, :] += dkall[0:TB]
        dv_ref[0:TB, :] += dvall[0:TB]
        dk_ref[pl.ds(prev, TB), :] += dkall[TB:2 * TB]
        dv_ref[pl.ds(prev, TB), :] += dvall[TB:2 * TB]
        dk_ref[pl.ds(cur, TB), :] += dkall[2 * TB:3 * TB]
        dv_ref[pl.ds(cur, TB), :] += dvall[2 * TB:3 * TB]

    return pl.pallas_call(
        body, name=name, grid=(nb,),
        in_specs=[sp["qr"]] + sp["k"] + sp["v"] + [sp["gate"], sp["sinks"],
                                                   pl.BlockSpec((TB, ATT_W), lambda n: (n, 0)),
                                                   pl.BlockSpec(memory_space=pl.ANY)],
        out_specs=[pl.BlockSpec((TB, ATT_W), lambda n: (n, 0)),
                   pl.BlockSpec((TB, ATT_W), lambda n: (n, 0)),
                   pl.BlockSpec((lp, KV_W), lambda n: (0, 0)),
                   pl.BlockSpec((lp, KV_W), lambda n: (0, 0)),
                   pl.BlockSpec((8, 128), lambda n: (0, 0))],
        out_shape=[jax.ShapeDtypeStruct((lp, ATT_W), F32),
                   jax.ShapeDtypeStruct((lp, ATT_W), BF16),
                   jax.ShapeDtypeStruct((lp, KV_W), F32),
                   jax.ShapeDtypeStruct((lp, KV_W), F32),
                   jax.ShapeDtypeStruct((8, 128), F32)],
        compiler_params=_cp(1),
    )(qr, kr, kr, kr, proj, proj, proj, proj, sinks_row, dycat, dep)


def _attn_assemble(name, dq, dgate, dk, dv, tabs, dproj):
    lp = dq.shape[0]
    tr = _row_tile(lp, 3)

    def body(dq_ref, dg_ref, dk_ref, dv_ref, c_ref, sa_ref, sb_ref, din_ref, o_ref):
        del din_ref
        cidx = pl.program_id(1)
        c, sa, sb = c_ref[...], sa_ref[...], sb_ref[...]

        @pl.when(cidx < 2)
        def _():
            c2 = jnp.concatenate([c, c], axis=1)
            sa2 = jnp.concatenate([sa, sa], axis=1)
            sb2 = jnp.concatenate([sb, sb], axis=1)
            o_ref[...] = (_rot_t(dq_ref[...], c2, sa2, sb2) * ATT_SCALE).astype(BF16)

        @pl.when(cidx == 2)
        def _():
            o_ref[:, 0:KV_W] = _rot_t(dk_ref[...], c, sa, sb).astype(BF16)
            o_ref[:, KV_W:2 * KV_W] = dv_ref[...].astype(BF16)

        @pl.when(cidx > 2)
        def _():
            o_ref[...] = dg_ref[...]

    tab = pl.BlockSpec((tr, KV_W), lambda n, c: (n, 0))
    return pl.pallas_call(
        body, name=name, grid=(lp // tr, 5),
        in_specs=[pl.BlockSpec((tr, 512), lambda n, c: (n, jnp.minimum(c, 1))),
                  pl.BlockSpec((tr, 512), lambda n, c: (n, jnp.clip(c - 3, 0, 1))),
                  tab, tab,
                  tab, tab, tab,
                  pl.BlockSpec(memory_space=pl.ANY)],
        out_specs=pl.BlockSpec((tr, 512), lambda n, c: (n, COL_Q0 + c)),
        out_shape=jax.ShapeDtypeStruct((lp, IN_TOTAL), BF16),
        input_output_aliases={7: 0},
        compiler_params=_cp(2),
    )(dq, dgate, dk, dv, *tabs, dproj)


def _softplus_neg(lam):
    t = jnp.exp(-jnp.abs(lam))
    u = 1.0 + t
    den = jnp.where(u == 1.0, 1.0, u - 1.0)
    l1p = jnp.where(u == 1.0, t, jnp.log(u) * (t / den))
    return jnp.maximum(-lam, 0.0) + l1p


def _lru_chain(j, tb, rx_ref, rxp_ref, wl_ref, vec_ref, wa_ref, wx_ref):
    rx = rx_ref[...]
    rxp = jnp.where(j > 0, rxp_ref[...], 0.0)
    cat = jnp.concatenate([rxp, rx], axis=0)
    views = [cat[8:8 + tb, :]] + [pltpu.roll(cat, s, axis=0)[8:8 + tb, :] for s in range(1, LRU_CONV_K)]
    x1 = jnp.broadcast_to(vec_ref[0:1, :], (tb, LRU_W))
    for k in range(LRU_CONV_K):
        x1 = x1 + wl_ref[k:k + 1, :] * views[LRU_CONV_K - 1 - k]
    x1b = x1.astype(BF16)
    r = _sig(_dot(x1b, wa_ref[...], NN) + vec_ref[1:2, :])
    ig = _sig(_dot(x1b, wx_ref[...], NN) + vec_ref[2:3, :])
    sp = _softplus_neg(vec_ref[3:4, :])
    log_a = -LRU_C * r * sp
    rows = _row_ids((tb, LRU_W), j * tb)
    live = rows >= PAD0
    a = jnp.where(live, jnp.exp(log_a), 0.0)
    y2 = 2.0 * log_a
    em = -jnp.tanh(0.5 * y2) * (jnp.exp(y2) + 1.0)
    mult = jnp.sqrt(em)
    return dict(views=views, x1=x1, x1b=x1b, r=r, ig=ig, sp=sp, a=a, mult=mult, live=live, a_raw=jnp.exp(log_a))


def _lru_specs(jmap, tb):
    return [pl.BlockSpec((tb, 512), lambda n: (jmap(n), COL_RX)),
            pl.BlockSpec((8, 512), lambda n: (jnp.maximum(jmap(n) * (tb // 8) - 1, 0), COL_RX)),
            pl.BlockSpec((tb, 512), lambda n: (jmap(n), COL_RGATE))]


def _lru_param_specs():
    return [pl.BlockSpec((8, LRU_W), lambda n: (0, 0)),
            pl.BlockSpec((8, LRU_W), lambda n: (0, 0)),
            pl.BlockSpec((LRU_W, LRU_W), lambda n: (0, 0)),
            pl.BlockSpec((LRU_W, LRU_W), lambda n: (0, 0))]


def _lru_fwd(name, proj, wl, vec, wa, wx, ycat):
    lp = proj.shape[0]
    tb = _row_tile(lp, 3)
    nb = lp // tb

    def body(rx_ref, rxp_ref, gate_ref, wl_ref, vec_ref, wa_ref, wx_ref, yin_ref, o_ref, h_ref, carry_ref):
        del yin_ref
        j = pl.program_id(0)

        @pl.when(j == 0)
        def _():
            carry_ref[...] = jnp.zeros(carry_ref.shape, F32)

        c = _lru_chain(j, tb, rx_ref, rxp_ref, wl_ref, vec_ref, wa_ref, wx_ref)
        a = c["a"]
        u = jnp.where(c["live"], c["mult"] * (c["ig"] * c["x1"]), 0.0)
        rows = lax.broadcasted_iota(jnp.int32, (tb, LRU_W), 0)
        d = 1
        while d < tb:
            ap = jnp.where(rows >= d, pltpu.roll(a, d, axis=0), 1.0)
            up = jnp.where(rows >= d, pltpu.roll(u, d, axis=0), 0.0)
            u = a * up + u
            a = a * ap
            d *= 2
        h = u + a * carry_ref[0:1, :]
        carry_ref[...] = h[tb - 8:tb, :]
        carry_ref[0:1, :] = h[tb - 1:tb, :]
        h_ref[...] = h
        gate = gate_ref[...]
        o_ref[...] = (h * (gate * _sig(gate))).astype(BF16)

    return pl.pallas_call(
        body, name=name, grid=(nb,),
        in_specs=_lru_specs(lambda n: n, tb) + _lru_param_specs() + [pl.BlockSpec(memory_space=pl.ANY)],
        out_specs=[pl.BlockSpec((tb, 512), lambda n: (n, YC_LRU)),
                   pl.BlockSpec((tb, LRU_W), lambda n: (n, 0))],
        out_shape=[jax.ShapeDtypeStruct((lp, D_MODEL), BF16),
                   jax.ShapeDtypeStruct((lp, LRU_W), F32)],
        input_output_aliases={7: 0},
        scratch_shapes=[pltpu.VMEM((8, LRU_W), F32)],
        compiler_params=_cp(1),
    )(proj, proj, proj, wl, vec, wa, wx, ycat)


def _lru_bwd(name, proj, dycat, hstate, wl, vec, wa, wx, dproj):
    lp = proj.shape[0]
    tb = _row_tile(lp, 3)
    nb = lp // tb

    def body(rx_ref, rxp_ref, gate_ref, dy_ref, h_ref, hp_ref, wl_ref, vec_ref, wa_ref, wx_ref, din_ref,
             dp_ref, dwl_ref, dvec_ref, dwa_ref, dwx_ref, dhc_ref, anx_ref, dxc_ref):
        del din_ref
        n = pl.program_id(0)
        j = nb - 1 - n

        @pl.when(n == 0)
        def _():
            dhc_ref[...] = jnp.zeros(dhc_ref.shape, F32)
            anx_ref[...] = jnp.zeros(anx_ref.shape, F32)
            dxc_ref[...] = jnp.zeros(dxc_ref.shape, F32)
            dwl_ref[...] = jnp.zeros(dwl_ref.shape, F32)
            dvec_ref[...] = jnp.zeros(dvec_ref.shape, F32)
            dwa_ref[...] = jnp.zeros(dwa_ref.shape, F32)
            dwx_ref[...] = jnp.zeros(dwx_ref.shape, F32)

        c = _lru_chain(j, tb, rx_ref, rxp_ref, wl_ref, vec_ref, wa_ref, wx_ref)
        a, mult, r, ig, x1, live = c["a"], c["mult"], c["r"], c["ig"], c["x1"], c["live"]
        h = h_ref[...]
        gate = gate_ref[...]
        sgate = _sig(gate)
        dy = dy_ref[...].astype(F32)
        gsum = dy * (gate * sgate)
        dgate = dy * h * _dsilu(gate, sgate)
        rows = lax.broadcasted_iota(jnp.int32, (tb, LRU_W), 0)
        bb = jnp.where(rows == tb - 1, anx_ref[0:1, :], pltpu.roll(a, tb - 1, axis=0))
        gg = gsum
        d = 1
        while d < tb:
            keep = rows < tb - d
            bn = jnp.where(keep, pltpu.roll(bb, tb - d, axis=0), 1.0)
            gn = jnp.where(keep, pltpu.roll(gg, tb - d, axis=0), 0.0)
            gg = gg + bb * gn
            bb = bb * bn
            d *= 2
        dh = gg + bb * dhc_ref[0:1, :]
        dhc_ref[...] = dh[0:8, :]
        anx_ref[...] = a[0:8, :]
        hprev = jnp.where(rows == 0, jnp.where(j > 0, hp_ref[7:8, :], 0.0), pltpu.roll(h, 1, axis=0))
        du = jnp.where(live, dh, 0.0)
        da = jnp.where(live, dh * hprev, 0.0)
        ar = c["a_raw"]
        dmult = du * (ig * x1)
        di = du * mult * x1
        dx1 = du * mult * ig
        dloga = da * ar - dmult * ar * ar / mult
        dr = dloga * (-LRU_C * c["sp"])
        dvec_ref[3:4, :] += _colsum(dloga * (-LRU_C * r))
        dza = dr * r * (1.0 - r)
        dzx = di * ig * (1.0 - ig)
        dzab, dzxb = dza.astype(BF16), dzx.astype(BF16)
        dvec_ref[1:2, :] += _colsum(dza)
        dvec_ref[2:3, :] += _colsum(dzx)
        dwa_ref[...] += _dot(c["x1b"], dzab, TN)
        dwx_ref[...] += _dot(c["x1b"], dzxb, TN)
        dx1 = dx1 + _dot(dzab, wa_ref[...], NT) + _dot(dzxb, wx_ref[...], NT)
        dvec_ref[0:1, :] += _colsum(dx1)
        for k in range(LRU_CONV_K):
            dwl_ref[k:k + 1, :] += _colsum(dx1 * c["views"][LRU_CONV_K - 1 - k])
        dcat = jnp.concatenate([dx1, dxc_ref[...]], axis=0)
        drx = jnp.zeros((tb, LRU_W), F32)
        for k in range(LRU_CONV_K):
            s = LRU_CONV_K - 1 - k
            view = dcat[0:tb, :] if s == 0 else pltpu.roll(dcat, tb + 8 - s, axis=0)[0:tb, :]
            drx = drx + wl_ref[k:k + 1, :] * view
        dxc_ref[...] = dx1[0:8, :]
        dp_ref[:, 0:512] = drx.astype(BF16)
        dp_ref[:, 512:1024] = dgate.astype(BF16)

        @pl.when(n == nb - 1)
        def _():
            lam = vec_ref[3:4, :]
            dvec_ref[3:4, :] = dvec_ref[3:4, :] * (-_sig(-lam))

    jmap = lambda n: nb - 1 - n
    return pl.pallas_call(
        body, name=name, grid=(nb,),
        in_specs=(_lru_specs(jmap, tb)
                  + [pl.BlockSpec((tb, 512), lambda n: (jmap(n), YC_LRU)),
                     pl.BlockSpec((tb, LRU_W), lambda n: (jmap(n), 0)),
                     pl.BlockSpec((8, LRU_W), lambda n: (jnp.maximum(jmap(n) * (tb // 8) - 1, 0), 0))]
                  + _lru_param_specs() + [pl.BlockSpec(memory_space=pl.ANY)]),
        out_specs=[pl.BlockSpec((tb, 1024), lambda n: (jmap(n), 4)),
                   pl.BlockSpec((8, LRU_W), lambda n: (0, 0)),
                   pl.BlockSpec((8, LRU_W), lambda n: (0, 0)),
                   pl.BlockSpec((LRU_W, LRU_W), lambda n: (0, 0)),
                   pl.BlockSpec((LRU_W, LRU_W), lambda n: (0, 0))],
        out_shape=[jax.ShapeDtypeStruct((lp, IN_TOTAL), BF16),
                   jax.ShapeDtypeStruct((8, LRU_W), F32),
                   jax.ShapeDtypeStruct((8, LRU_W), F32),
                   jax.ShapeDtypeStruct((LRU_W, LRU_W), F32),
                   jax.ShapeDtypeStruct((LRU_W, LRU_W), F32)],
        input_output_aliases={10: 0},
        scratch_shapes=[pltpu.VMEM((8, LRU_W), F32), pltpu.VMEM((8, LRU_W), F32), pltpu.VMEM((8, LRU_W), F32)],
        compiler_params=_cp(1),
    )(proj, proj, proj, dycat, hstate, hstate, wl, vec, wa, wx, dproj)


_HBM = pl.BlockSpec(memory_space=pltpu.HBM)
_SEM = pl.BlockSpec(memory_space=pltpu.SEMAPHORE)
_ANY = pl.BlockSpec(memory_space=pl.ANY)
_EFFECT = pltpu.SideEffectType.DATAFLOW_SIDE_EFFECTING


def _hbm(a):
    return pltpu.with_memory_space_constraint(a, pltpu.HBM)


_ALL_PEERS = tuple(range(1, N_DEV))
_CHIP_PEERS = (1, 2, 4, 6)
_OTHER_CHIPS = (2, 4, 6)


def _spec_peers(mode):
    return {"ici": _CHIP_PEERS, "fwd": _OTHER_CHIPS}.get(mode, _ALL_PEERS)


def _split_descriptors(copies, srcs, lands, send_sems, recv_sems):
    x, y, c = lax.axis_index("x"), lax.axis_index("y"), lax.axis_index("c")
    me = 4 * x + 2 * y + c
    out, sem = [], 0
    for si, mode, li, ll in copies:
        for k in _spec_peers(mode):
            px = 1 - x if k & 4 else x
            py = 1 - y if k & 2 else y
            pc = 1 - c if k & 1 else c
            peer = 4 * px + 2 * py + pc
            if mode == "fwd":
                src = dst = lands[li].at[peer]
                target = (x, y, 1 - c)
            else:
                src = srcs[si].at[peer] if mode is True else srcs[si]
                dst = lands[li].at[me] if ll is None else lands[li].at[me, ll]
                target = (px, py, pc)
            out.append(pltpu.make_async_remote_copy(
                src_ref=src, dst_ref=dst, send_sem=send_sems.at[sem], recv_sem=recv_sems.at[sem],
                device_id=target, device_id_type=pl.DeviceIdType.MESH))
            sem += 1
    return out


def _n_copies(copies):
    return sum(len(_spec_peers(mode)) for _, mode, _, _ in copies)


def _xchg_start(name, groups):
    n_src = [len(g[0]) for g in groups]
    n_land = [len(g[1]) for g in groups]
    srcs = [s for g in groups for s in g[0]]
    lands = [l for g in groups for l in g[1]]
    ns, nl, ng = len(srcs), len(lands), len(groups)

    def body(*refs):
        src_refs, land_refs = refs[:ns], refs[ns:ns + nl]
        sems = refs[ns + nl:ns + nl + 2 * ng]
        token = refs[-1]
        so = lo = 0
        for gi, (_, _, copies) in enumerate(groups):
            for d in _split_descriptors(copies, src_refs[so:so + n_src[gi]], land_refs[lo:lo + n_land[gi]],
                                        sems[2 * gi], sems[2 * gi + 1]):
                d.start()
            so += n_src[gi]
            lo += n_land[gi]
        token[...] = jnp.zeros(token.shape, F32)

    out_shape, out_specs = [], []
    for g in groups:
        n = _n_copies(g[2])
        out_shape += [pltpu.SemaphoreType.DMA((n,)), pltpu.SemaphoreType.DMA((n,))]
        out_specs += [_SEM, _SEM]
    out_shape += [pltpu.HBM(l.shape, l.dtype) for l in lands]
    out_specs += [_HBM] * nl
    out_shape.append(jax.ShapeDtypeStruct((8, 128), F32))
    out_specs.append(pl.BlockSpec(memory_space=pltpu.VMEM))
    outs = pl.pallas_call(
        body, name=name, in_specs=[_HBM] * (ns + nl), out_specs=out_specs, out_shape=out_shape,
        input_output_aliases={ns + i: 2 * ng + i for i in range(nl)},
        compiler_params=pltpu.CompilerParams(has_side_effects=_EFFECT),
    )(*[_hbm(a) for a in srcs + lands])
    res, lo = [], 2 * ng
    for gi in range(ng):
        res.append((outs[2 * gi], outs[2 * gi + 1], list(outs[lo:lo + n_land[gi]])))
        lo += n_land[gi]
    return res, outs[-1]


def _xchg_wait(name, group, started, after):
    srcs, _, copies = group
    send_sems, recv_sems, lands = started
    ns, nl = len(srcs), len(lands)
    after = list(after)

    def body(*refs):
        src_refs, land_refs = refs[:ns], refs[ns:ns + nl]
        send_ref, recv_ref = refs[ns + nl], refs[ns + nl + 1]
        for d in _split_descriptors(copies, src_refs, land_refs, send_ref, recv_ref):
            d.wait_send()
            d.wait_recv()

    outs = pl.pallas_call(
        body, name=name, in_specs=[_HBM] * (ns + nl) + [_SEM, _SEM] + [_ANY] * len(after),
        out_specs=[_HBM] * nl, out_shape=[pltpu.HBM(l.shape, l.dtype) for l in lands],
        input_output_aliases={ns + i: i for i in range(nl)},
        compiler_params=pltpu.CompilerParams(has_side_effects=_EFFECT),
    )(*[_hbm(a) for a in srcs], *lands, send_sems, recv_sems, *after)
    return list(outs)


def _landing(own, me):
    land = lax.empty((N_DEV,) + own.shape, own.dtype)
    return lax.dynamic_update_slice(land, own[None], (me,) + (0,) * own.ndim)


def _adamw(name, w, m, v, recv, row0=0, prev=None):
    cdim = w.shape[1]
    r = recv.shape[1]
    tr = r
    for cand in (512, 256, 128, 64, 32, 16, 8):
        if r % cand == 0 and r > cand:
            tr = cand
            break
    assert row0 % tr == 0
    blk0 = row0 // tr
    n_prev = 0 if prev is None else 4

    def body(w_ref, m_ref, v_ref, r_ref, *rest):
        g_ref, d_ref, mo_ref, vo_ref = rest[n_prev:]
        g = r_ref[0].astype(F32)
        for s in range(1, N_DEV):
            g = g + r_ref[s].astype(F32)
        mn = ADAM_B1 * m_ref[...] + (1.0 - ADAM_B1) * g
        vn = ADAM_B2 * v_ref[...] + (1.0 - ADAM_B2) * (g * g)
        m_hat = mn / (1.0 - ADAM_B1 ** ADAM_STEP)
        v_hat = vn / (1.0 - ADAM_B2 ** ADAM_STEP)
        g_ref[...] = g
        d_ref[...] = -ADAM_LR * (m_hat / (jnp.sqrt(v_hat) + ADAM_EPS) + ADAM_WD * w_ref[...])
        mo_ref[...] = mn
        vo_ref[...] = vn

    blk = pl.BlockSpec((tr, cdim), lambda i: (i + blk0, 0))
    return pl.pallas_call(
        body, name=name, grid=(r // tr,),
        in_specs=[blk, blk, blk, pl.BlockSpec((N_DEV, tr, cdim), lambda i: (0, i, 0))] + [_ANY] * n_prev,
        out_specs=[blk, blk, blk, blk],
        out_shape=[jax.ShapeDtypeStruct(w.shape, F32)] * 4,
        input_output_aliases={4 + i: i for i in range(n_prev)},
        compiler_params=_cp(1),
    )(w, m, v, recv, *(prev or []))


def _pack_rows(arrs, lead=()):
    n = len(lead)
    flat = jnp.concatenate([a.reshape(a.shape[:n] + (-1,)) for a in arrs], axis=-1)
    size = flat.shape[-1]
    padded = -(-size // PACK_QUANTUM) * PACK_QUANTUM
    flat = jnp.pad(flat, [(0, 0)] * n + [(0, padded - size)])
    return flat.reshape(flat.shape[:n] + (padded // 128, 128))


def _unpack_rows(packed, shapes, lead=()):
    n = len(lead)
    flat = packed.reshape(packed.shape[:n] + (-1,))
    out, off = [], 0
    for s in shapes:
        size = int(np.prod(s))
        out.append(flat[..., off:off + size].reshape(packed.shape[:n] + tuple(s)))
        off += size
    return out


def _block_diag(w):
    eye = jnp.eye(LRU_HEADS, dtype=w.dtype)
    return (eye[:, None, :, None] * w[:, :, None, :]).reshape(LRU_W, LRU_W)


def _diag_blocks(dense):
    t = dense.reshape(LRU_HEADS, 64, LRU_HEADS, 64)
    return jnp.stack([t[h, :, h, :] for h in range(LRU_HEADS)], axis=0)


def _cols_to_slots(full):
    lead = full.shape[:-1]
    t = full.reshape(lead + (N_DEV, full.shape[-1] // N_DEV))
    return jnp.moveaxis(t, -2, 0)


def _slots_to_cols(slots):
    t = jnp.moveaxis(slots, 0, -2)
    return t.reshape(t.shape[:-2] + (t.shape[-2] * t.shape[-1],))


def kernel(x, meta_tokens, ln_in_g, ln_in_b, w_in, conv_dw_w, conv_dw_b, conv_ln_g, conv_ln_b, conv_pw_w, conv_pw_b, attn_sinks, lru_conv_w, lru_conv_b, lru_wa, lru_ba, lru_wx, lru_bx, lru_lambda, w_out, ln_post_g, ln_post_b, loss_target, m_meta_tokens, m_ln_in_g, m_ln_in_b, m_w_in, m_conv_dw_w, m_conv_dw_b, m_conv_ln_g, m_conv_ln_b, m_conv_pw_w, m_conv_pw_b, m_attn_sinks, m_lru_conv_w, m_lru_conv_b, m_lru_wa, m_lru_ba, m_lru_wx, m_lru_bx, m_lru_lambda, m_w_out, m_ln_post_g, m_ln_post_b, v_meta_tokens, v_ln_in_g, v_ln_in_b, v_w_in, v_conv_dw_w, v_conv_dw_b, v_conv_ln_g, v_conv_ln_b, v_conv_pw_w, v_conv_pw_b, v_attn_sinks, v_lru_conv_w, v_lru_conv_b, v_lru_wa, v_lru_ba, v_lru_wx, v_lru_bx, v_lru_lambda, v_w_out, v_ln_post_g, v_ln_post_b):
    seq = x.shape[1]
    lp = seq + TB
    row = lambda a: a.reshape(1, -1)
    rep_names = ["ln_in_g", "ln_in_b", "conv_dw_b", "conv_ln_g", "conv_ln_b", "conv_pw_b", "attn_sinks",
                 "lru_conv_b", "lru_wa", "lru_ba", "lru_wx", "lru_bx", "lru_lambda", "ln_post_g", "ln_post_b"]
    shard_small_names = ["conv_dw_w", "lru_conv_w", "meta_tokens"]
    weights = dict(meta_tokens=meta_tokens, ln_in_g=ln_in_g, ln_in_b=ln_in_b, w_in=w_in, conv_dw_w=conv_dw_w,
                   conv_dw_b=conv_dw_b, conv_ln_g=conv_ln_g, conv_ln_b=conv_ln_b, conv_pw_w=conv_pw_w,
                   conv_pw_b=conv_pw_b, attn_sinks=attn_sinks, lru_conv_w=lru_conv_w, lru_conv_b=lru_conv_b,
                   lru_wa=lru_wa, lru_ba=lru_ba, lru_wx=lru_wx, lru_bx=lru_bx, lru_lambda=lru_lambda,
                   w_out=w_out, ln_post_g=ln_post_g, ln_post_b=ln_post_b)
    mom1 = dict(meta_tokens=m_meta_tokens, ln_in_g=m_ln_in_g, ln_in_b=m_ln_in_b, w_in=m_w_in, conv_dw_w=m_conv_dw_w,
                conv_dw_b=m_conv_dw_b, conv_ln_g=m_conv_ln_g, conv_ln_b=m_conv_ln_b, conv_pw_w=m_conv_pw_w,
                conv_pw_b=m_conv_pw_b, attn_sinks=m_attn_sinks, lru_conv_w=m_lru_conv_w, lru_conv_b=m_lru_conv_b,
                lru_wa=m_lru_wa, lru_ba=m_lru_ba, lru_wx=m_lru_wx, lru_bx=m_lru_bx, lru_lambda=m_lru_lambda,
                w_out=m_w_out, ln_post_g=m_ln_post_g, ln_post_b=m_ln_post_b)
    mom2 = dict(meta_tokens=v_meta_tokens, ln_in_g=v_ln_in_g, ln_in_b=v_ln_in_b, w_in=v_w_in, conv_dw_w=v_conv_dw_w,
                conv_dw_b=v_conv_dw_b, conv_ln_g=v_conv_ln_g, conv_ln_b=v_conv_ln_b, conv_pw_w=v_conv_pw_w,
                conv_pw_b=v_conv_pw_b, attn_sinks=v_attn_sinks, lru_conv_w=v_lru_conv_w, lru_conv_b=v_lru_conv_b,
                lru_wa=v_lru_wa, lru_ba=v_lru_ba, lru_wx=v_lru_wx, lru_bx=v_lru_bx, lru_lambda=v_lru_lambda,
                w_out=v_w_out, ln_post_g=v_ln_post_g, ln_post_b=v_ln_post_b)
    shard_wmv = [_pack_rows([d[n] for n in shard_small_names]) for d in (weights, mom1, mom2)]
    rep_wmv = [_pack_rows([d[n] for n in rep_names]) for d in (weights, mom1, mom2)]
    gate_w = [(_block_diag(lru_wa[l]).astype(BF16), _block_diag(lru_wx[l]).astype(BF16)) for l in range(DEPTH)]
    tabs = _rope_tables(lp)
    prepared = shard_wmv + rep_wmv + [w for pair in gate_w for w in pair] + list(tabs)

    small_shard_shapes = [conv_dw_w.shape, lru_conv_w.shape, meta_tokens.shape]
    small_shard = _pack_rows([conv_dw_w, lru_conv_w, meta_tokens])
    me = 4 * lax.axis_index("x") + 2 * lax.axis_index("y") + lax.axis_index("c")
    w_in_b = [w_in[l].astype(BF16) for l in range(DEPTH)]
    w_out_b = [w_out[l].astype(BF16) for l in range(DEPTH)]
    pw_b = conv_pw_w.astype(BF16)
    wgroups = [
        ([small_shard], [_landing(small_shard, me)], [(0, False, 0, None)]),
        ([w_in_b[0]], [_landing(w_in_b[0], me)], [(0, "ici", 0, None)]),
        ([pw_b, w_out_b[0]], [_landing(pw_b, me), _landing(w_out_b[0], me)],
         [(0, False, 0, None), (1, False, 1, None)]),
        ([w_in_b[1], w_out_b[1]], [_landing(w_in_b[1], me), _landing(w_out_b[1], me)],
         [(0, False, 0, None), (1, False, 1, None)]),
    ]
    wstarted, wtoken = _xchg_start("weights_start", wgroups)
    wg_small, = _xchg_wait("weights_wait_s", wgroups[0], wstarted[0], [wtoken])
    g_dw, g_lc, g_meta = _unpack_rows(wg_small, small_shard_shapes, lead=(N_DEV,))
    conv_dw_full = _slots_to_cols(g_dw)
    lru_conv_full = _slots_to_cols(g_lc)
    meta_full = _slots_to_cols(g_meta)
    wg_in = [None, None]
    wg_out = [None, None]
    wg_pw = None

    ln_g = [ln_in_g, ln_post_g[0], ln_post_g[1]]
    ln_b = [ln_in_b, ln_post_b[0], ln_post_b[1]]

    def layer_params(l):
        wdw = jnp.pad(conv_dw_full[l], ((0, 1), (0, 0)))
        cvec = jnp.pad(jnp.stack([conv_dw_b[l], conv_ln_g[l], conv_ln_b[l], conv_pw_b[l]]), ((0, 4), (0, 0)))
        wpw = wg_pw[:, l].reshape(CONV_W, CONV_W)
        sinks = jnp.pad(attn_sinks[l].reshape(1, N_HEADS), ((0, 7), (0, 128 - N_HEADS)))
        wl = jnp.pad(lru_conv_full[l], ((0, 4), (0, 0)))
        lvec = jnp.pad(jnp.stack([lru_conv_b[l], lru_ba[l], lru_bx[l], lru_lambda[l]]), ((0, 4), (0, 0)))
        wa, wx = gate_w[l]
        wo = wg_out[l].reshape(D_MODEL, D_MODEL)
        wout = jnp.concatenate([wo[512:1536], wo[0:512], wo[1536:]], axis=0)
        return dict(wdw=wdw, cvec=cvec, wpw=wpw, sinks=sinks, wl=wl, lvec=lvec, wa=wa, wx=wx, wout=wout)

    params = [None] * DEPTH

    z0, hb = _embed(x, meta_full, row(ln_g[0]), row(ln_b[0]))
    z = [z0]
    saved = []
    for l in range(DEPTH):
        if l == 0:
            part, = _xchg_wait("weights_wait_a", wgroups[1], wstarted[1], [hb] + prepared)
            fwd = ([], [part], [(None, "fwd", 0, None)])
            fstarted, ftoken = _xchg_start("weights_fwd_start", [fwd])
            wg_in[0], = _xchg_wait("weights_fwd_wait", fwd, fstarted[0], [ftoken])
        else:
            wg_in[1], wg_out[1] = _xchg_wait("weights_wait_c", wgroups[3], wstarted[3], [hb])
        proj = _mm_proj(f"proj{l}", hb, wg_in[l])
        if l == 0:
            wg_pw, wg_out[0] = _xchg_wait("weights_wait_b", wgroups[2], wstarted[2], [proj])
        p = params[l] = layer_params(l)
        ycat, c1 = _conv_fwd(f"conv_fwd{l}", proj, p["wdw"], p["cvec"], p["wpw"])
        qr, kr = _rope_fwd(f"rope{l}", proj, tabs)
        ycat = _attn_fwd(f"attn_fwd{l}", qr, kr, proj, p["sinks"], ycat)
        ycat, hstate = _lru_fwd(f"lru_fwd{l}", proj, p["wl"], p["lvec"], p["wa"], p["wx"], ycat)
        saved.append(dict(hb=hb, proj=proj, ycat=ycat, qr=qr, kr=kr, hstate=hstate, c1=c1))
        z_next, hb = _mm_out(f"out{l}", ycat, p["wout"], z[l], row(ln_g[l]), row(ln_b[l]),
                             row(ln_g[l + 1]), row(ln_b[l + 1]))
        z.append(z_next)

    dz, st_post1, loss_blk = _loss_head(z[DEPTH], loss_target, row(ln_g[DEPTH]), row(ln_b[DEPTH]))
    loss = lax.psum(loss_blk[0, 0], ("x", "y", "c"))

    ln_stats = {DEPTH: st_post1}
    g_layers = [None] * DEPTH
    dwin_l, dwout_l = [None] * DEPTH, [None] * DEPTH
    grad_x = gmeta = None
    token = wtoken
    ggroups = [None] * DEPTH
    own = lambda a: lax.dynamic_index_in_dim(a, me, 0, keepdims=False)
    for l in reversed(range(DEPTH)):
        p, s = params[l], saved[l]
        dycat = _mm_dycat(f"dycat{l}", dz, p["wout"], token)
        dwout_l[l] = _mm_dwout(f"dwout{l}", s["ycat"], dz)
        dproj, dwdw, dcvec, dwpw = _conv_bwd(f"conv_bwd{l}", s["proj"], dycat, s["c1"], p["wdw"], p["cvec"], p["wpw"])
        dwo = jnp.concatenate([dwout_l[l][1024:1536], dwout_l[l][0:1024], dwout_l[l][1536:]], axis=0)
        dwo = dwo.reshape(N_DEV, D_MODEL // N_DEV, D_MODEL)
        dpw = dwpw.reshape(N_DEV, CONV_W // N_DEV, CONV_W)
        early = ([dwo, dpw], [_landing(own(dwo), me), _landing(own(dpw), me)],
                 [(0, True, 0, None), (1, True, 1, None)])
        started_early, token = _xchg_start(f"grads_start_out{l}", [early])
        dq, dgate, dk, dv, dsink = _attn_bwd(f"attn_bwd{l}", s["qr"], s["kr"], s["proj"], p["sinks"], dycat, token)
        dproj = _attn_assemble(f"attn_asm{l}", dq, dgate, dk, dv, tabs, dproj)
        dproj, dwl, dlvec, dwa, dwx = _lru_bwd(f"lru_bwd{l}", s["proj"], dycat, s["hstate"],
                                                p["wl"], p["lvec"], p["wa"], p["wx"], dproj)
        dwin_l[l] = _mm_dwin(f"dwin{l}", s["hb"], dproj)
        late = ([dwin_l[l]], [_landing(own(dwin_l[l]), me)], [(0, True, 0, None)])
        started_late, token = _xchg_start(f"grads_start_in{l}", [late])
        ggroups[l] = [(late, started_late[0]), (early, started_early[0])]
        dh = _mm_dh(f"dh{l}", dproj, wg_in[l], dz, token)
        if l > 0:
            dz, ln_stats[l] = _ln_bwd(f"ln_bwd{l}", dh, z[l], row(ln_g[l]))
        else:
            grad_x, gmeta, ln_stats[0] = _ln_bwd_input(dh, z[0], row(ln_g[0]))
        g_layers[l] = dict(dwdw=dwdw[:CONV_K], dcvec=dcvec, dwpw=dwpw, dsink=dsink[0, :N_HEADS],
                           dwl=dwl[:LRU_CONV_K], dlvec=dlvec, dwa=_diag_blocks(dwa), dwx=_diag_blocks(dwx))

    stack = lambda f: jnp.stack([f(g_layers[l]) for l in range(DEPTH)])
    g_local = dict(
        ln_in_g=ln_stats[0][0], ln_in_b=ln_stats[0][1],
        conv_dw_b=stack(lambda g: g["dcvec"][0]), conv_ln_g=stack(lambda g: g["dcvec"][1]),
        conv_ln_b=stack(lambda g: g["dcvec"][2]), conv_pw_b=stack(lambda g: g["dcvec"][3]),
        attn_sinks=stack(lambda g: g["dsink"]),
        lru_conv_b=stack(lambda g: g["dlvec"][0]), lru_wa=stack(lambda g: g["dwa"]),
        lru_ba=stack(lambda g: g["dlvec"][1]), lru_wx=stack(lambda g: g["dwx"]),
        lru_bx=stack(lambda g: g["dlvec"][2]), lru_lambda=stack(lambda g: g["dlvec"][3]),
        ln_post_g=jnp.stack([ln_stats[1][0], ln_stats[2][0]]),
        ln_post_b=jnp.stack([ln_stats[1][1], ln_stats[2][1]]),
    )
    rep_pack = _pack_rows([g_local[n] for n in rep_names])
    g_dw_full = jnp.stack([g_layers[l]["dwdw"] for l in range(DEPTH)])
    g_lc_full = jnp.stack([g_layers[l]["dwl"] for l in range(DEPTH)])
    shard_pack = _pack_rows([_cols_to_slots(g_dw_full), _cols_to_slots(g_lc_full), _cols_to_slots(gmeta)],
                            lead=(N_DEV,))
    sgroup = ([shard_pack, rep_pack], [_landing(own(shard_pack), me), _landing(rep_pack, me)],
              [(0, True, 0, None), (1, False, 1, None)])
    sstarted, token = _xchg_start("small_grads_start", [sgroup])

    res = {}

    def flat2(a, cols):
        return a.reshape(-1, cols)

    big = (("w_in", 0, W_IN_SHARD), ("w_out", 1, D_MODEL), ("conv_pw_w", 2, CONV_W))
    prev = {n: None for n, _, _ in big}
    after = token
    for l in reversed(range(DEPTH)):
        recvs = []
        for gi, (grp, started) in enumerate(ggroups[l]):
            recvs += _xchg_wait(f"grads_wait{l}_{gi}", grp, started, [after])
        for name_, gi, cols in big:
            w_ = weights[name_]
            rows = w_.shape[1]
            prev[name_] = _adamw(f"adamw_{name_}{l}", flat2(w_, cols), flat2(mom1[name_], cols),
                                 flat2(mom2[name_], cols), recvs[gi], row0=l * rows, prev=prev[name_])
        after = prev["w_in"][0]
    for name_, _, _ in big:
        res[name_] = [o.reshape(weights[name_].shape) for o in prev[name_]]

    r_small, r_rep = _xchg_wait("small_grads_wait", sgroup, sstarted[0], [prev[n][0] for n, _, _ in big])
    sshapes = [weights[n].shape for n in shard_small_names]
    outs = _adamw("adamw_small_sharded", *shard_wmv, r_small)
    for k, o in enumerate(outs):
        for n, a in zip(shard_small_names, _unpack_rows(o, sshapes)):
            res.setdefault(n, [None] * 4)[k] = a

    rshapes = [weights[n].shape for n in rep_names]
    outs = _adamw("adamw_replicated", *rep_wmv, r_rep)
    for k, o in enumerate(outs):
        for n, a in zip(rep_names, _unpack_rows(o, rshapes)):
            res.setdefault(n, [None] * 4)[k] = a

    order = ["meta_tokens", "ln_in_g", "ln_in_b", "w_in", "conv_dw_w", "conv_dw_b", "conv_ln_g", "conv_ln_b",
             "conv_pw_w", "conv_pw_b", "attn_sinks", "lru_conv_w", "lru_conv_b", "lru_wa", "lru_ba", "lru_wx",
             "lru_bx", "lru_lambda", "w_out", "ln_post_g", "ln_post_b"]
    return (loss, grad_x,
            *[res[n][0] for n in order], *[res[n][1] for n in order],
            *[res[n][2] for n in order], *[res[n][3] for n in order])
```

```python
import functools
import math

import numpy as np
import jax
import jax.numpy as jnp
from jax import lax
from jax.experimental import pallas as pl
from jax.experimental.pallas import tpu as pltpu

F32 = jnp.float32
BF16 = jnp.bfloat16

D_MODEL = 2048
DEPTH = 2
N_META = 16
TB = 128
PAD0 = TB - N_META
CONV_W = 512
CONV_K = 31
HEAD_DIM = 64
N_HEADS = 16
N_KV = 4
GROUP = 4
ATT_W = 1024
KV_W = 256
ROT_DIM = 16
ROPE_THETA = 500000.0
LRU_W = 512
LRU_HEADS = 8
LRU_CONV_K = 4
LRU_C = 8.0
IN_TOTAL = 5120
N_DEV = 8
W_IN_SHARD = IN_TOTAL // N_DEV
LN_EPS = 1e-5
ALPHA = (2.0 * DEPTH) ** 0.25
NEG_INF = -1e30
ATT_SCALE = HEAD_DIM ** -0.5

ADAM_LR = 0.001
ADAM_B1 = 0.9
ADAM_B2 = 0.999
ADAM_EPS = 1e-08
ADAM_WD = 0.01
ADAM_STEP = 10

VMEM_LIMIT = 56 * 1024 * 1024
PACK_QUANTUM = 256 * 128

COL_CV, COL_CG, COL_CGATE = 0, 1, 2
COL_Q0 = 3
COL_K256 = 10
COL_V256 = 11
COL_AGATE1024 = 3
COL_RX, COL_RGATE = 8, 9
YC_CONV, YC_LRU = 2, 3


def _cp(n_axes, vmem=VMEM_LIMIT):
    return pltpu.CompilerParams(dimension_semantics=("arbitrary",) * n_axes, vmem_limit_bytes=vmem)


def _row_tile(lp, max_blocks):
    nb = lp // TB
    d = max(k for k in range(1, max_blocks + 1) if nb % k == 0)
    return TB * d


def _sig(x):
    return jax.nn.sigmoid(x)


def _dsilu(x, s):
    return s * (1.0 + x * (1.0 - s))


def _ln_core(z):
    mu = jnp.mean(z, axis=-1, keepdims=True)
    zc = z - mu
    var = jnp.mean(zc * zc, axis=-1, keepdims=True)
    rstd = lax.rsqrt(var + LN_EPS)
    return zc * rstd, rstd


def _ln_bwd_core(dy, xh, rstd, g):
    dxh = dy * g
    m1 = jnp.mean(dxh, axis=-1, keepdims=True)
    m2 = jnp.mean(dxh * xh, axis=-1, keepdims=True)
    return rstd * (dxh - m1 - xh * m2)


def _row_ids(shape, base):
    return lax.broadcasted_iota(jnp.int32, shape, 0) + base


def _colsum(x):
    return jnp.sum(x, axis=0, keepdims=True)


def _dot(a, b, dims):
    return lax.dot_general(a, b, (dims, ((), ())), preferred_element_type=F32)


NN = ((1,), (0,))
NT = ((1,), (1,))
TN = ((0,), (0,))


def _embed(x, meta_full, g, b):
    s = x.shape[1]
    lp = s + TB
    nb = lp // TB

    def body(x_ref, m_ref, g_ref, b_ref, o_ref, hb_ref):
        i = pl.program_id(0)

        @pl.when(i == 0)
        def _():
            o_ref[0:PAD0, :] = jnp.zeros((PAD0, D_MODEL), F32)
            o_ref[PAD0:TB, :] = m_ref[...]

        @pl.when(i > 0)
        def _():
            o_ref[...] = x_ref[...]

        xh, _ = _ln_core(o_ref[...])
        h = xh * g_ref[...] + b_ref[...]
        rows = _row_ids(h.shape, i * TB)
        hb_ref[...] = jnp.where(rows >= PAD0, h, 0.0).astype(BF16)

    return pl.pallas_call(
        body, name="embed", grid=(nb,),
        in_specs=[pl.BlockSpec((None, TB, D_MODEL), lambda i: (0, jnp.maximum(i - 1, 0), 0)),
                  pl.BlockSpec((N_META, D_MODEL), lambda i: (0, 0)),
                  pl.BlockSpec((1, D_MODEL), lambda i: (0, 0)),
                  pl.BlockSpec((1, D_MODEL), lambda i: (0, 0))],
        out_specs=[pl.BlockSpec((TB, D_MODEL), lambda i: (i, 0)),
                   pl.BlockSpec((TB, D_MODEL), lambda i: (i, 0))],
        out_shape=[jax.ShapeDtypeStruct((lp, D_MODEL), F32),
                   jax.ShapeDtypeStruct((lp, D_MODEL), BF16)],
        compiler_params=_cp(1),
    )(x, meta_full, g, b)


def _loss_head(z, target, g, b):
    lp = z.shape[0]
    nb = lp // TB

    def body(z_ref, t_ref, g_ref, b_ref, dz_ref, st_ref, loss_ref):
        i = pl.program_id(0)

        @pl.when(i == 0)
        def _():
            st_ref[...] = jnp.zeros(st_ref.shape, F32)
            loss_ref[...] = jnp.zeros(loss_ref.shape, F32)
            dz_ref[...] = jnp.zeros(dz_ref.shape, F32)

        @pl.when(i > 0)
        def _():
            xh, rstd = _ln_core(z_ref[...])
            gg = g_ref[...]
            y = xh * gg + b_ref[...]
            e = y - t_ref[...]
            part = 0.5 * jnp.sum(jnp.mean(e * e, axis=-1, keepdims=True), axis=0, keepdims=True)
            loss_ref[...] += jnp.broadcast_to(part, loss_ref.shape)
            dy = e / float(D_MODEL)
            st_ref[0:1, :] += _colsum(dy * xh)
            st_ref[1:2, :] += _colsum(dy)
            dz_ref[...] = _ln_bwd_core(dy, xh, rstd, gg)

    return pl.pallas_call(
        body, name="loss_head", grid=(nb,),
        in_specs=[pl.BlockSpec((TB, D_MODEL), lambda i: (i, 0)),
                  pl.BlockSpec((None, TB, D_MODEL), lambda i: (0, jnp.maximum(i - 1, 0), 0)),
                  pl.BlockSpec((1, D_MODEL), lambda i: (0, 0)),
                  pl.BlockSpec((1, D_MODEL), lambda i: (0, 0))],
        out_specs=[pl.BlockSpec((TB, D_MODEL), lambda i: (i, 0)),
                   pl.BlockSpec((8, D_MODEL), lambda i: (0, 0)),
                   pl.BlockSpec((8, 128), lambda i: (0, 0))],
        out_shape=[jax.ShapeDtypeStruct((lp, D_MODEL), F32),
                   jax.ShapeDtypeStruct((8, D_MODEL), F32),
                   jax.ShapeDtypeStruct((8, 128), F32)],
        compiler_params=_cp(1),
    )(z, target, g, b)


def _ln_bwd(name, dh, z, g):
    lp = z.shape[0]
    nb = lp // TB

    def body(dh_ref, z_ref, g_ref, dz_ref, st_ref):
        i = pl.program_id(0)

        @pl.when(i == 0)
        def _():
            st_ref[...] = jnp.zeros(st_ref.shape, F32)

        xh, rstd = _ln_core(z_ref[...])
        rows = _row_ids(xh.shape, i * TB)
        dy = jnp.where(rows >= PAD0, dh_ref[...], 0.0)
        st_ref[0:1, :] += _colsum(dy * xh)
        st_ref[1:2, :] += _colsum(dy)
        dz_ref[...] = _ln_bwd_core(dy, xh, rstd, g_ref[...])

    return pl.pallas_call(
        body, name=name, grid=(nb,),
        in_specs=[pl.BlockSpec((TB, D_MODEL), lambda i: (i, 0)),
                  pl.BlockSpec((TB, D_MODEL), lambda i: (i, 0)),
                  pl.BlockSpec((1, D_MODEL), lambda i: (0, 0))],
        out_specs=[pl.BlockSpec((TB, D_MODEL), lambda i: (i, 0)),
                   pl.BlockSpec((8, D_MODEL), lambda i: (0, 0))],
        out_shape=[jax.ShapeDtypeStruct((lp, D_MODEL), F32),
                   jax.ShapeDtypeStruct((8, D_MODEL), F32)],
        compiler_params=_cp(1),
    )(dh, z, g)


def _ln_bwd_input(dh, z, g):
    lp = z.shape[0]
    nb = lp // TB
    s = lp - TB

    def body(dh_ref, z_ref, g_ref, gx_ref, gm_ref, st_ref):
        i = pl.program_id(0)

        @pl.when(i == 0)
        def _():
            st_ref[...] = jnp.zeros(st_ref.shape, F32)

        xh, rstd = _ln_core(z_ref[...])
        rows = _row_ids(xh.shape, i * TB)
        dy = jnp.where(rows >= PAD0, dh_ref[...], 0.0)
        st_ref[0:1, :] += _colsum(dy * xh)
        st_ref[1:2, :] += _colsum(dy)
        dz = _ln_bwd_core(dy, xh, rstd, g_ref[...])
        gx_ref[...] = dz

        @pl.when(i == 0)
        def _():
            gm_ref[...] = dz[PAD0:TB, :]

    return pl.pallas_call(
        body, name="ln_in_bwd", grid=(nb,),
        in_specs=[pl.BlockSpec((TB, D_MODEL), lambda i: (i, 0)),
                  pl.BlockSpec((TB, D_MODEL), lambda i: (i, 0)),
                  pl.BlockSpec((1, D_MODEL), lambda i: (0, 0))],
        out_specs=[pl.BlockSpec((None, TB, D_MODEL), lambda i: (0, jnp.maximum(i - 1, 0), 0)),
                   pl.BlockSpec((N_META, D_MODEL), lambda i: (0, 0)),
                   pl.BlockSpec((8, D_MODEL), lambda i: (0, 0))],
        out_shape=[jax.ShapeDtypeStruct((1, s, D_MODEL), F32),
                   jax.ShapeDtypeStruct((N_META, D_MODEL), F32),
                   jax.ShapeDtypeStruct((8, D_MODEL), F32)],
        compiler_params=_cp(1),
    )(dh, z, g)


def _mm_proj(name, hb, wg_in):
    lp = hb.shape[0]
    tm = lp // 3

    def body(a_ref, b_ref, o_ref):
        b = jnp.concatenate([b_ref[0], b_ref[1]], axis=1)
        o_ref[...] = _dot(a_ref[...], b, NN)

    return pl.pallas_call(
        body, name=name, grid=(3, N_DEV // 2),
        in_specs=[pl.BlockSpec((tm, D_MODEL), lambda i, j: (i, 0)),
                  pl.BlockSpec((2, D_MODEL, W_IN_SHARD), lambda i, j: (j, 0, 0))],
        out_specs=pl.BlockSpec((tm, 2 * W_IN_SHARD), lambda i, j: (i, j)),
        out_shape=jax.ShapeDtypeStruct((lp, IN_TOTAL), F32),
        compiler_params=_cp(2),
    )(hb, wg_in)


def _mm_out(name, ycat, wout, z, g, b, g2, b2):
    lp = ycat.shape[0]
    tm = lp // 6

    def body(a_ref, w_ref, z_ref, g_ref, b_ref, g2_ref, b2_ref, o_ref, hb_ref):
        i = pl.program_id(0)
        xh, _ = _ln_core(z_ref[...])
        h = xh * g_ref[...] + b_ref[...]
        live = _row_ids(h.shape, i * tm) >= PAD0
        h = jnp.where(live, h, 0.0)
        zn = ALPHA * h + _dot(a_ref[...], w_ref[...], NN)
        o_ref[...] = zn
        xh2, _ = _ln_core(zn)
        hb_ref[...] = jnp.where(live, xh2 * g2_ref[...] + b2_ref[...], 0.0).astype(BF16)

    vec = pl.BlockSpec((1, D_MODEL), lambda i: (0, 0))
    return pl.pallas_call(
        body, name=name, grid=(6,),
        in_specs=[pl.BlockSpec((tm, D_MODEL), lambda i: (i, 0)),
                  pl.BlockSpec((D_MODEL, D_MODEL), lambda i: (0, 0), pipeline_mode=pl.Buffered(1)),
                  pl.BlockSpec((tm, D_MODEL), lambda i: (i, 0)),
                  vec, vec, vec, vec],
        out_specs=[pl.BlockSpec((tm, D_MODEL), lambda i: (i, 0)),
                   pl.BlockSpec((tm, D_MODEL), lambda i: (i, 0))],
        out_shape=[jax.ShapeDtypeStruct((lp, D_MODEL), F32),
                   jax.ShapeDtypeStruct((lp, D_MODEL), BF16)],
        compiler_params=_cp(1),
    )(ycat, wout, z, g, b, g2, b2)


def _mm_dycat(name, dz, wout, dep):
    lp = dz.shape[0]
    tm = lp // 6

    def body(a_ref, w_ref, dep_ref, o_ref):
        del dep_ref
        o_ref[...] = _dot(a_ref[...].astype(BF16), w_ref[...], NT).astype(BF16)

    return pl.pallas_call(
        body, name=name, grid=(6,),
        in_specs=[pl.BlockSpec((tm, D_MODEL), lambda i: (i, 0)),
                  pl.BlockSpec((D_MODEL, D_MODEL), lambda i: (0, 0), pipeline_mode=pl.Buffered(1)),
                  pl.BlockSpec(memory_space=pl.ANY)],
        out_specs=pl.BlockSpec((tm, D_MODEL), lambda i: (i, 0)),
        out_shape=jax.ShapeDtypeStruct((lp, D_MODEL), BF16),
        compiler_params=_cp(1),
    )(dz, wout, dep)


def _mm_dwout(name, ycat, dz):
    lp = ycat.shape[0]
    tk = _row_tile(lp, 11)
    nk = lp // tk
    half = D_MODEL // 2

    def body(a_ref, b_ref, o_ref, acc_ref):
        k = pl.program_id(1)

        @pl.when(k == 0)
        def _():
            acc_ref[...] = jnp.zeros(acc_ref.shape, F32)

        acc_ref[...] += _dot(a_ref[...], b_ref[...].astype(BF16), TN)

        @pl.when(k == nk - 1)
        def _():
            o_ref[...] = acc_ref[...].astype(BF16)

    return pl.pallas_call(
        body, name=name, grid=(2, nk),
        in_specs=[pl.BlockSpec((tk, half), lambda h, k: (k, h)),
                  pl.BlockSpec((tk, D_MODEL), lambda h, k: (k, 0))],
        out_specs=pl.BlockSpec((half, D_MODEL), lambda h, k: (h, 0)),
        out_shape=jax.ShapeDtypeStruct((D_MODEL, D_MODEL), BF16),
        scratch_shapes=[pltpu.VMEM((half, D_MODEL), F32)],
        compiler_params=_cp(2),
    )(ycat, dz)


def _mm_dwin(name, hb, dproj):
    lp = hb.shape[0]
    tk = _row_tile(lp, 11)
    nk = lp // tk

    def body(a_ref, b_ref, o_ref, acc_ref):
        k = pl.program_id(1)

        @pl.when(k == 0)
        def _():
            acc_ref[...] = jnp.zeros(acc_ref.shape, F32)

        acc_ref[...] += _dot(a_ref[...], b_ref[...], TN)

        @pl.when(k == nk - 1)
        def _():
            o_ref[0] = acc_ref[:, 0:W_IN_SHARD].astype(BF16)
            o_ref[1] = acc_ref[:, W_IN_SHARD:2 * W_IN_SHARD].astype(BF16)

    return pl.pallas_call(
        body, name=name, grid=(4, nk),
        in_specs=[pl.BlockSpec((tk, D_MODEL), lambda j, k: (k, 0)),
                  pl.BlockSpec((tk, 2 * W_IN_SHARD), lambda j, k: (k, j))],
        out_specs=pl.BlockSpec((2, D_MODEL, W_IN_SHARD), lambda j, k: (j, 0, 0)),
        out_shape=jax.ShapeDtypeStruct((N_DEV, D_MODEL, W_IN_SHARD), BF16),
        scratch_shapes=[pltpu.VMEM((D_MODEL, 2 * W_IN_SHARD), F32)],
        compiler_params=_cp(2),
    )(hb, dproj)


def _mm_dh(name, dproj, wg_in, dz, dep):
    lp = dproj.shape[0]
    tm = lp // 6

    def body(a_ref, w_ref, dz_ref, dep_ref, o_ref, acc_ref):
        del dep_ref
        k = pl.program_id(1)

        @pl.when(k == 0)
        def _():
            acc_ref[...] = jnp.zeros(acc_ref.shape, F32)

        w = jnp.concatenate([w_ref[0], w_ref[1]], axis=1)
        acc_ref[...] += _dot(a_ref[...], w, NT)

        @pl.when(k == N_DEV // 2 - 1)
        def _():
            o_ref[...] = acc_ref[...] + ALPHA * dz_ref[...]

    return pl.pallas_call(
        body, name=name, grid=(6, N_DEV // 2),
        in_specs=[pl.BlockSpec((tm, 2 * W_IN_SHARD), lambda i, k: (i, k)),
                  pl.BlockSpec((2, D_MODEL, W_IN_SHARD), lambda i, k: (k, 0, 0)),
                  pl.BlockSpec((tm, D_MODEL), lambda i, k: (i, 0)),
                  pl.BlockSpec(memory_space=pl.ANY)],
        out_specs=pl.BlockSpec((tm, D_MODEL), lambda i, k: (i, 0)),
        out_shape=jax.ShapeDtypeStruct((lp, D_MODEL), F32),
        scratch_shapes=[pltpu.VMEM((tm, D_MODEL), F32)],
        compiler_params=_cp(2),
    )(dproj, wg_in, dz, dep)


SUB = 128


def _shift_plan(cat, n_shift, base):
    rolled = [cat] + [pltpu.roll(cat, b, axis=0) for b in range(1, 8)]
    return [(rolled[s % 8], base - 8 * (s // 8)) for s in range(n_shift)]


def _tap_sum(w_ref, plan, rows, init=None):
    blocks = []
    for r0 in range(0, rows, SUB):
        row = []
        for c0 in range(0, CONV_W, SUB):
            acc = (jnp.zeros((SUB, SUB), F32) if init is None
                   else jnp.broadcast_to(init[:, c0:c0 + SUB], (SUB, SUB)))
            for k, (arr, off) in enumerate(plan):
                acc = acc + w_ref[k:k + 1, c0:c0 + SUB] * arr[off + r0:off + r0 + SUB, c0:c0 + SUB]
            row.append(acc)
        blocks.append(jnp.concatenate(row, axis=1))
    return jnp.concatenate(blocks, axis=0)


def _tap_grads(dw_ref, dy, plan, rows):
    for c0 in range(0, CONV_W, SUB):
        dys = [dy[r0:r0 + SUB, c0:c0 + SUB] for r0 in range(0, rows, SUB)]
        for k, (arr, off) in enumerate(plan):
            part = None
            for ri, r0 in enumerate(range(0, rows, SUB)):
                prod = dys[ri] * arr[off + r0:off + r0 + SUB, c0:c0 + SUB]
                for i in range(SUB // 8):
                    piece = prod[8 * i:8 * i + 8, :]
                    part = piece if part is None else part + piece
            dw_ref[k:k + 1, c0:c0 + SUB] += jnp.sum(part, axis=0, keepdims=True)


CONV_HALO = 32


def _conv_chain(j, tb, cv_ref, cg_ref, cvp_ref, cgp_ref, wdw_ref, vec_ref, wpw_ref, c1_ref=None):
    cv = cv_ref[...]
    sg = _sig(cg_ref[...])
    c0 = cv * sg
    c0p = jnp.where(j > 0, cvp_ref[...] * _sig(cgp_ref[...]), 0.0)
    cat = jnp.concatenate([c0p, c0], axis=0)
    shifts = _shift_plan(cat, CONV_K, CONV_HALO)
    taps = [shifts[CONV_K - 1 - k] for k in range(CONV_K)]
    if c1_ref is None:
        c1 = _tap_sum(wdw_ref, taps, tb, init=vec_ref[0:1, :])
    else:
        c1 = c1_ref[...]
    xh, rstd = _ln_core(c1)
    c2 = xh * vec_ref[1:2, :] + vec_ref[2:3, :]
    s2 = _sig(c2)
    c3 = c2 * s2
    c4 = _dot(c3.astype(BF16), wpw_ref[...], NN) + vec_ref[3:4, :]
    return dict(cv=cv, sg=sg, taps=taps, c1=c1, xh=xh, rstd=rstd, c2=c2, s2=s2, c3=c3, c4=c4)


def _conv_in_specs(jmap, tb):
    def cur(col):
        return pl.BlockSpec((tb, 512), lambda n: (jmap(n), col))

    def prev(col):
        return pl.BlockSpec((CONV_HALO, 512),
                            lambda n: (jnp.maximum(jmap(n) * (tb // CONV_HALO) - 1, 0), col))

    return [cur(COL_CV), cur(COL_CG), prev(COL_CV), prev(COL_CG), cur(COL_CGATE)]


def _conv_param_specs():
    return [pl.BlockSpec((32, CONV_W), lambda n: (0, 0)),
            pl.BlockSpec((8, CONV_W), lambda n: (0, 0)),
            pl.BlockSpec((CONV_W, CONV_W), lambda n: (0, 0))]


def _conv_fwd(name, proj, wdw, vec, wpw):
    lp = proj.shape[0]
    tb = _row_tile(lp, 3)
    nb = lp // tb

    def body(cv_ref, cg_ref, cvp_ref, cgp_ref, gate_ref, wdw_ref, vec_ref, wpw_ref, o_ref, c1_ref):
        j = pl.program_id(0)
        c = _conv_chain(j, tb, cv_ref, cg_ref, cvp_ref, cgp_ref, wdw_ref, vec_ref, wpw_ref)
        gate = gate_ref[...]
        o_ref[...] = (c["c4"] * (gate * _sig(gate))).astype(BF16)
        c1_ref[...] = c["c1"]

    return pl.pallas_call(
        body, name=name, grid=(nb,),
        in_specs=_conv_in_specs(lambda n: n, tb) + _conv_param_specs(),
        out_specs=[pl.BlockSpec((tb, 512), lambda n: (n, YC_CONV)),
                   pl.BlockSpec((tb, CONV_W), lambda n: (n, 0))],
        out_shape=[jax.ShapeDtypeStruct((lp, D_MODEL), BF16),
                   jax.ShapeDtypeStruct((lp, CONV_W), F32)],
        compiler_params=_cp(1),
    )(proj, proj, proj, proj, proj, wdw, vec, wpw)


def _conv_bwd(name, proj, dycat, c1, wdw, vec, wpw):
    lp = proj.shape[0]
    tb = _row_tile(lp, 3)
    nb = lp // tb
    halo = CONV_HALO

    def body(cv_ref, cg_ref, cvp_ref, cgp_ref, gate_ref, dy_ref, c1_ref, wdw_ref, vec_ref, wpw_ref,
             dp_ref, dwdw_ref, dvec_ref, dwpw_ref, carry_ref):
        n = pl.program_id(0)
        j = nb - 1 - n

        @pl.when(n == 0)
        def _():
            carry_ref[...] = jnp.zeros(carry_ref.shape, F32)
            dwdw_ref[...] = jnp.zeros(dwdw_ref.shape, F32)
            dvec_ref[...] = jnp.zeros(dvec_ref.shape, F32)
            dwpw_ref[...] = jnp.zeros(dwpw_ref.shape, F32)

        c = _conv_chain(j, tb, cv_ref, cg_ref, cvp_ref, cgp_ref, wdw_ref, vec_ref, wpw_ref, c1_ref)
        dy = dy_ref[...].astype(F32)
        gate = gate_ref[...]
        sgate = _sig(gate)
        dc4 = dy * (gate * sgate)
        dgate = dy * c["c4"] * _dsilu(gate, sgate)
        dc4b = dc4.astype(BF16)
        dvec_ref[3:4, :] += _colsum(dc4)
        dwpw_ref[...] += _dot(c["c3"].astype(BF16), dc4b, TN)
        dc3 = _dot(dc4b, wpw_ref[...], NT)
        dc2 = dc3 * _dsilu(c["c2"], c["s2"])
        dvec_ref[1:2, :] += _colsum(dc2 * c["xh"])
        dvec_ref[2:3, :] += _colsum(dc2)
        dc1 = _ln_bwd_core(dc2, c["xh"], c["rstd"], vec_ref[1:2, :])
        dvec_ref[0:1, :] += _colsum(dc1)
        _tap_grads(dwdw_ref, dc1, c["taps"], tb)
        dcat = jnp.concatenate([dc1, carry_ref[...]], axis=0)
        total = tb + halo
        up = [dcat] + [pltpu.roll(dcat, total - b, axis=0) for b in range(1, 8)]
        ahead = [(up[(CONV_K - 1 - k) % 8], 8 * ((CONV_K - 1 - k) // 8)) for k in range(CONV_K)]
        dc0 = _tap_sum(wdw_ref, ahead, tb)
        carry_ref[...] = dc1[0:halo, :]
        sg = c["sg"]
        dcv = dc0 * sg
        dcg = dc0 * c["cv"] * sg * (1.0 - sg)
        dp_ref[:, 0:512] = dcv.astype(BF16)
        dp_ref[:, 512:1024] = dcg.astype(BF16)
        dp_ref[:, 1024:1536] = dgate.astype(BF16)

    jmap = lambda n: nb - 1 - n
    return pl.pallas_call(
        body, name=name, grid=(nb,),
        in_specs=(_conv_in_specs(jmap, tb)
                  + [pl.BlockSpec((tb, 512), lambda n: (jmap(n), YC_CONV)),
                     pl.BlockSpec((tb, CONV_W), lambda n: (jmap(n), 0))]
                  + _conv_param_specs()),
        out_specs=[pl.BlockSpec((tb, 1536), lambda n: (jmap(n), 0)),
                   pl.BlockSpec((32, CONV_W), lambda n: (0, 0)),
                   pl.BlockSpec((8, CONV_W), lambda n: (0, 0)),
                   pl.BlockSpec((CONV_W, CONV_W), lambda n: (0, 0))],
        out_shape=[jax.ShapeDtypeStruct((lp, IN_TOTAL), BF16),
                   jax.ShapeDtypeStruct((32, CONV_W), F32),
                   jax.ShapeDtypeStruct((8, CONV_W), F32),
                   jax.ShapeDtypeStruct((CONV_W, CONV_W), F32)],
        scratch_shapes=[pltpu.VMEM((halo, CONV_W), F32)],
        compiler_params=_cp(1),
    )(proj, proj, proj, proj, proj, dycat, c1, wdw, vec, wpw)


def _rope_tables(lp):
    half = ROT_DIM // 2
    inv_freq = ROPE_THETA ** (-jnp.arange(half, dtype=F32) / half)
    pos = (jnp.arange(lp, dtype=jnp.int32) - PAD0).astype(F32)
    ang = pos[:, None] * inv_freq[None, :]
    cos, sin = jnp.cos(ang), jnp.sin(ang)
    ones = jnp.ones((lp, HEAD_DIM - ROT_DIM), F32)
    zeros = jnp.zeros((lp, HEAD_DIM - ROT_DIM), F32)
    zh = jnp.zeros((lp, half), F32)
    c = jnp.concatenate([cos, cos, ones], axis=1)
    sa = jnp.concatenate([-sin, zh, zeros], axis=1)
    sb = jnp.concatenate([zh, sin, zeros], axis=1)
    tile = lambda t: jnp.tile(t, (1, KV_W // HEAD_DIM))
    return tile(c), tile(sa), tile(sb)


def _rot(x, c, sa, sb):
    w = x.shape[1]
    return x * c + pltpu.roll(x, w - 8, axis=1) * sa + pltpu.roll(x, 8, axis=1) * sb


def _rot_t(dy, c, sa, sb):
    w = dy.shape[1]
    return dy * c + pltpu.roll(dy * sa, 8, axis=1) + pltpu.roll(dy * sb, w - 8, axis=1)


def _rope_fwd(name, proj, tabs):
    lp = proj.shape[0]
    tr = _row_tile(lp, 3)

    def body(q0_ref, q1_ref, k_ref, c_ref, sa_ref, sb_ref, qr_ref, kr_ref):
        c, sa, sb = c_ref[...], sa_ref[...], sb_ref[...]
        c2 = jnp.concatenate([c, c], axis=1)
        sa2 = jnp.concatenate([sa, sa], axis=1)
        sb2 = jnp.concatenate([sb, sb], axis=1)
        qr_ref[:, 0:512] = (_rot(q0_ref[...], c2, sa2, sb2) * ATT_SCALE).astype(BF16)
        qr_ref[:, 512:1024] = (_rot(q1_ref[...], c2, sa2, sb2) * ATT_SCALE).astype(BF16)
        kr_ref[...] = _rot(k_ref[...], c, sa, sb).astype(BF16)

    tab = pl.BlockSpec((tr, KV_W), lambda i: (i, 0))
    return pl.pallas_call(
        body, name=name, grid=(lp // tr,),
        in_specs=[pl.BlockSpec((tr, 512), lambda i: (i, COL_Q0)),
                  pl.BlockSpec((tr, 512), lambda i: (i, COL_Q0 + 1)),
                  pl.BlockSpec((tr, KV_W), lambda i: (i, COL_K256)),
                  tab, tab, tab],
        out_specs=[pl.BlockSpec((tr, ATT_W), lambda i: (i, 0)),
                   pl.BlockSpec((tr, KV_W), lambda i: (i, 0))],
        out_shape=[jax.ShapeDtypeStruct((lp, ATT_W), BF16),
                   jax.ShapeDtypeStruct((lp, KV_W), BF16)],
        compiler_params=_cp(1),
    )(proj, proj, proj, *tabs)


def _attn_mask(j):
    qi = lax.broadcasted_iota(jnp.int32, (GROUP * TB, 3 * TB), 0) & (TB - 1)
    cc = lax.broadcasted_iota(jnp.int32, (GROUP * TB, 3 * TB), 1)
    jj = cc & (TB - 1)
    is_meta = jj >= PAD0
    p0 = (cc < TB) & is_meta & (j >= 1)
    p1 = (cc >= TB) & (cc < 2 * TB) & (jj > qi) & (j >= 2)
    p2 = (cc >= 2 * TB) & (jj <= qi) & ((j >= 1) | is_meta)
    return p0 | p1 | p2


def _lane_group(rows):
    return lax.broadcasted_iota(jnp.int32, (rows, KV_W), 1) // HEAD_DIM


def _stack_heads(x, kv, lgq):
    parts = []
    for g in range(GROUP):
        sh = ((kv - g) % GROUP) * HEAD_DIM
        moved = x if sh == 0 else pltpu.roll(x, sh, axis=1)
        parts.append(jnp.where(lgq == kv, moved, 0.0))
    return jnp.concatenate(parts, axis=0).astype(BF16)


def _unstack_heads(r, kv):
    out = None
    for g in range(GROUP):
        blk = r[g * TB:(g + 1) * TB, :]
        sh = ((g - kv) % GROUP) * HEAD_DIM
        blk = blk if sh == 0 else pltpu.roll(blk, sh, axis=1)
        out = blk if out is None else out + blk
    return out


def _sink_column(sinks, kv):
    lane = lax.broadcasted_iota(jnp.int32, (1, 128), 1)
    cols = []
    for g in range(GROUP):
        sg = jnp.sum(jnp.where(lane == kv * GROUP + g, sinks, 0.0), axis=1, keepdims=True)
        cols.append(jnp.broadcast_to(sg, (TB, 1)))
    return jnp.concatenate(cols, axis=0)


def _attn_kv(kall, vall, lg, kv):
    km = jnp.where(lg == kv, kall, 0.0).astype(BF16)
    vm = jnp.where(lg == kv, vall, 0.0).astype(BF16)
    ones = jnp.where(lg == kv, 1.0, 0.0).astype(BF16)
    return km, vm, ones


def _attn_weights(qst, km, vm, ones, sinkcol, valid, lg4, kv):
    s = jnp.where(valid, _dot(qst, km, NT), NEG_INF)
    m = jnp.maximum(jnp.max(s, axis=-1, keepdims=True), sinkcol)
    eb = jnp.exp(s - m).astype(BF16)
    es = jnp.exp(sinkcol - m)
    r = _dot(eb, vm, NN)
    inv = 1.0 / (_dot(eb, ones, NN) + es)
    out = jnp.where(lg4 == kv, r * inv, 0.0)
    return eb, es, inv, out


def _attn_specs(jmap):
    blk = lambda col: pl.BlockSpec((TB, KV_W), lambda n: (jmap(n), col))
    prv = lambda col: pl.BlockSpec((TB, KV_W), lambda n: (jnp.maximum(jmap(n) - 1, 0), col))
    met = lambda col: pl.BlockSpec((TB, KV_W), lambda n: (0, col))
    return dict(
        qr=pl.BlockSpec((TB, ATT_W), lambda n: (jmap(n), 0)),
        k=[met(0), prv(0), blk(0)],
        v=[met(COL_V256), prv(COL_V256), blk(COL_V256)],
        gate=pl.BlockSpec((TB, ATT_W), lambda n: (jmap(n), COL_AGATE1024)),
        sinks=pl.BlockSpec((8, 128), lambda n: (0, 0)),
    )


def _attn_fwd(name, qr, kr, proj, sinks_row, ycat):
    lp = proj.shape[0]
    nb = lp // TB
    sp = _attn_specs(lambda n: n)

    def body(qr_ref, km_ref, kp_ref, kc_ref, vm_ref, vp_ref, vc_ref, gate_ref, sink_ref, yin_ref, o_ref):
        del yin_ref
        j = pl.program_id(0)
        valid = _attn_mask(j)
        kall = jnp.concatenate([km_ref[...], kp_ref[...], kc_ref[...]], axis=0).astype(F32)
        vall = jnp.concatenate([vm_ref[...], vp_ref[...], vc_ref[...]], axis=0)
        lg = _lane_group(3 * TB)
        lgq = _lane_group(TB)
        lg4 = _lane_group(GROUP * TB)
        sinks = sink_ref[0:1, :]
        for kv in range(N_KV):
            cols = slice(kv * KV_W, (kv + 1) * KV_W)
            km, vm, ones = _attn_kv(kall, vall, lg, kv)
            qst = _stack_heads(qr_ref[:, cols].astype(F32), kv, lgq)
            _, _, _, out = _attn_weights(qst, km, vm, ones, _sink_column(sinks, kv), valid, lg4, kv)
            att = _unstack_heads(out, kv)
            gate = gate_ref[:, cols]
            o_ref[:, cols] = (att * (gate * _sig(gate))).astype(BF16)

    return pl.pallas_call(
        body, name=name, grid=(nb,),
        in_specs=[sp["qr"]] + sp["k"] + sp["v"] + [sp["gate"], sp["sinks"],
                                                   pl.BlockSpec(memory_space=pl.ANY)],
        out_specs=pl.BlockSpec((TB, ATT_W), lambda n: (n, 0)),
        out_shape=jax.ShapeDtypeStruct((lp, D_MODEL), BF16),
        input_output_aliases={9: 0},
        compiler_params=_cp(1),
    )(qr, kr, kr, kr, proj, proj, proj, proj, sinks_row, ycat)


def _attn_bwd(name, qr, kr, proj, sinks_row, dycat, dep):
    lp = proj.shape[0]
    nb = lp // TB
    sp = _attn_specs(lambda n: n)

    def body(qr_ref, km_ref, kp_ref, kc_ref, vm_ref, vp_ref, vc_ref, gate_ref, sink_ref, dy_ref, dep_ref,
             dq_ref, dgate_ref, dk_ref, dv_ref, dsink_ref):
        del dep_ref
        j = pl.program_id(0)

        @pl.when(j == 0)
        def _():
            dk_ref[...] = jnp.zeros(dk_ref.shape, F32)
            dv_ref[...] = jnp.zeros(dv_ref.shape, F32)
            dsink_ref[...] = jnp.zeros(dsink_ref.shape, F32)

        valid = _attn_mask(j)
        kall = jnp.concatenate([km_ref[...], kp_ref[...], kc_ref[...]], axis=0).astype(F32)
        vall = jnp.concatenate([vm_ref[...], vp_ref[...], vc_ref[...]], axis=0)
        lg = _lane_group(3 * TB)
        lgq = _lane_group(TB)
        lg4 = _lane_group(GROUP * TB)
        sinks = sink_ref[0:1, :]
        lane = lax.broadcasted_iota(jnp.int32, (1, 128), 1)
        dkall = jnp.zeros((3 * TB, KV_W), F32)
        dvall = jnp.zeros((3 * TB, KV_W), F32)
        dsink = jnp.zeros((1, 128), F32)
        for kv in range(N_KV):
            cols = slice(kv * KV_W, (kv + 1) * KV_W)
            km, vm, ones = _attn_kv(kall, vall, lg, kv)
            qst = _stack_heads(qr_ref[:, cols].astype(F32), kv, lgq)
            gate = gate_ref[:, cols]
            sgate = _sig(gate)
            dy = dy_ref[:, cols].astype(F32)
            dout = dy * (gate * sgate)
            eb, es, inv, out = _attn_weights(qst, km, vm, ones, _sink_column(sinks, kv), valid, lg4, kv)
            att = _unstack_heads(out, kv)
            dgate_ref[:, cols] = (dy * att * _dsilu(gate, sgate)).astype(BF16)
            dsc = dout * _unstack_heads(jnp.where(lg4 == kv, inv, 0.0), kv)
            dost = _stack_heads(dsc, kv, lgq)
            dd = dsc * att
            dcol = jnp.concatenate(
                [jnp.sum(jnp.where(lgq == g, dd, 0.0), axis=1, keepdims=True) for g in range(GROUP)], axis=0)
            dp = _dot(dost, vm, NT)
            ds = (eb.astype(F32) * (dp - dcol)).astype(BF16)
            pd = es * dcol
            for g in range(GROUP):
                tot = jnp.sum(pd[g * TB:(g + 1) * TB, :], axis=0, keepdims=True)
                dsink = dsink - jnp.where(lane == kv * GROUP + g, tot, 0.0)
            dq_ref[:, cols] = _unstack_heads(_dot(ds, km, NN), kv)
            dkall = dkall + _dot(ds, qst, TN)
            dvall = dvall + _dot(eb, dost, TN)
        dsink_ref[0:1, :] += dsink
        prev = pl.multiple_of(jnp.maximum(j - 1, 0) * TB, TB)
        cur = pl.multiple_of(j * TB, TB)
        dk_ref[0:TB, :] += dkall[0:TB]
        dv_ref[0:TB, :] += dvall[0:TB]
        dk_ref[pl.ds(prev, TB), :] += dkall[TB:2 * TB]
        dv_ref[pl.ds(prev, TB), :] += dvall[TB:2 * TB]
        dk_ref[pl.ds(cur, TB), :] += dkall[2 * TB:3 * TB]
        dv_ref[pl.ds(cur, TB), :] += dvall[2 * TB:3 * TB]

    return pl.pallas_call(
        body, name=name, grid=(nb,),
        in_specs=[sp["qr"]] + sp["k"] + sp["v"] + [sp["gate"], sp["sinks"],
                                                   pl.BlockSpec((TB, ATT_W), lambda n: (n, 0)),
                                                   pl.BlockSpec(memory_space=pl.ANY)],
        out_specs=[pl.BlockSpec((TB, ATT_W), lambda n: (n, 0)),
                   pl.BlockSpec((TB, ATT_W), lambda n: (n, 0)),
                   pl.BlockSpec((lp, KV_W), lambda n: (0, 0)),
                   pl.BlockSpec((lp, KV_W), lambda n: (0, 0)),
                   pl.BlockSpec((8, 128), lambda n: (0, 0))],
        out_shape=[jax.ShapeDtypeStruct((lp, ATT_W), F32),
                   jax.ShapeDtypeStruct((lp, ATT_W), BF16),
                   jax.ShapeDtypeStruct((lp, KV_W), F32),
                   jax.ShapeDtypeStruct((lp, KV_W), F32),
                   jax.ShapeDtypeStruct((8, 128), F32)],
        compiler_params=_cp(1),
    )(qr, kr, kr, kr, proj, proj, proj, proj, sinks_row, dycat, dep)


def _attn_assemble(name, dq, dgate, dk, dv, tabs, dproj):
    lp = dq.shape[0]
    tr = _row_tile(lp, 3)

    def body(dq_ref, dg_ref, dk_ref, dv_ref, c_ref, sa_ref, sb_ref, din_ref, o_ref):
        del din_ref
        cidx = pl.program_id(1)
        c, sa, sb = c_ref[...], sa_ref[...], sb_ref[...]

        @pl.when(cidx < 2)
        def _():
            c2 = jnp.concatenate([c, c], axis=1)
            sa2 = jnp.concatenate([sa, sa], axis=1)
            sb2 = jnp.concatenate([sb, sb], axis=1)
            o_ref[...] = (_rot_t(dq_ref[...], c2, sa2, sb2) * ATT_SCALE).astype(BF16)

        @pl.when(cidx == 2)
        def _():
            o_ref[:, 0:KV_W] = _rot_t(dk_ref[...], c, sa, sb).astype(BF16)
            o_ref[:, KV_W:2 * KV_W] = dv_ref[...].astype(BF16)

        @pl.when(cidx > 2)
        def _():
            o_ref[...] = dg_ref[...]

    tab = pl.BlockSpec((tr, KV_W), lambda n, c: (n, 0))
    return pl.pallas_call(
        body, name=name, grid=(lp // tr, 5),
        in_specs=[pl.BlockSpec((tr, 512), lambda n, c: (n, jnp.minimum(c, 1))),
                  pl.BlockSpec((tr, 512), lambda n, c: (n, jnp.clip(c - 3, 0, 1))),
                  tab, tab,
                  tab, tab, tab,
                  pl.BlockSpec(memory_space=pl.ANY)],
        out_specs=pl.BlockSpec((tr, 512), lambda n, c: (n, COL_Q0 + c)),
        out_shape=jax.ShapeDtypeStruct((lp, IN_TOTAL), BF16),
        input_output_aliases={7: 0},
        compiler_params=_cp(2),
    )(dq, dgate, dk, dv, *tabs, dproj)


def _softplus_neg(lam):
    t = jnp.exp(-jnp.abs(lam))
    u = 1.0 + t
    den = jnp.where(u == 1.0, 1.0, u - 1.0)
    l1p = jnp.where(u == 1.0, t, jnp.log(u) * (t / den))
    return jnp.maximum(-lam, 0.0) + l1p


def _lru_chain(j, tb, rx_ref, rxp_ref, wl_ref, vec_ref, wa_ref, wx_ref):
    rx = rx_ref[...]
    rxp = jnp.where(j > 0, rxp_ref[...], 0.0)
    cat = jnp.concatenate([rxp, rx], axis=0)
    views = [cat[8:8 + tb, :]] + [pltpu.roll(cat, s, axis=0)[8:8 + tb, :] for s in range(1, LRU_CONV_K)]
    x1 = jnp.broadcast_to(vec_ref[0:1, :], (tb, LRU_W))
    for k in range(LRU_CONV_K):
        x1 = x1 + wl_ref[k:k + 1, :] * views[LRU_CONV_K - 1 - k]
    x1b = x1.astype(BF16)
    r = _sig(_dot(x1b, wa_ref[...], NN) + vec_ref[1:2, :])
    ig = _sig(_dot(x1b, wx_ref[...], NN) + vec_ref[2:3, :])
    sp = _softplus_neg(vec_ref[3:4, :])
    log_a = -LRU_C * r * sp
    rows = _row_ids((tb, LRU_W), j * tb)
    live = rows >= PAD0
    a = jnp.where(live, jnp.exp(log_a), 0.0)
    y2 = 2.0 * log_a
    em = -jnp.tanh(0.5 * y2) * (jnp.exp(y2) + 1.0)
    mult = jnp.sqrt(em)
    return dict(views=views, x1=x1, x1b=x1b, r=r, ig=ig, sp=sp, a=a, mult=mult, live=live, a_raw=jnp.exp(log_a))


def _lru_specs(jmap, tb):
    return [pl.BlockSpec((tb, 512), lambda n: (jmap(n), COL_RX)),
            pl.BlockSpec((8, 512), lambda n: (jnp.maximum(jmap(n) * (tb // 8) - 1, 0), COL_RX)),
            pl.BlockSpec((tb, 512), lambda n: (jmap(n), COL_RGATE))]


def _lru_param_specs():
    return [pl.BlockSpec((8, LRU_W), lambda n: (0, 0)),
            pl.BlockSpec((8, LRU_W), lambda n: (0, 0)),
            pl.BlockSpec((LRU_W, LRU_W), lambda n: (0, 0)),
            pl.BlockSpec((LRU_W, LRU_W), lambda n: (0, 0))]


def _lru_fwd(name, proj, wl, vec, wa, wx, ycat):
    lp = proj.shape[0]
    tb = _row_tile(lp, 3)
    nb = lp // tb

    def body(rx_ref, rxp_ref, gate_ref, wl_ref, vec_ref, wa_ref, wx_ref, yin_ref, o_ref, h_ref, carry_ref):
        del yin_ref
        j = pl.program_id(0)

        @pl.when(j == 0)
        def _():
            carry_ref[...] = jnp.zeros(carry_ref.shape, F32)

        c = _lru_chain(j, tb, rx_ref, rxp_ref, wl_ref, vec_ref, wa_ref, wx_ref)
        a = c["a"]
        u = jnp.where(c["live"], c["mult"] * (c["ig"] * c["x1"]), 0.0)
        rows = lax.broadcasted_iota(jnp.int32, (tb, LRU_W), 0)
        d = 1
        while d < tb:
            ap = jnp.where(rows >= d, pltpu.roll(a, d, axis=0), 1.0)
            up = jnp.where(rows >= d, pltpu.roll(u, d, axis=0), 0.0)
            u = a * up + u
            a = a * ap
            d *= 2
        h = u + a * carry_ref[0:1, :]
        carry_ref[...] = h[tb - 8:tb, :]
        carry_ref[0:1, :] = h[tb - 1:tb, :]
        h_ref[...] = h
        gate = gate_ref[...]
        o_ref[...] = (h * (gate * _sig(gate))).astype(BF16)

    return pl.pallas_call(
        body, name=name, grid=(nb,),
        in_specs=_lru_specs(lambda n: n, tb) + _lru_param_specs() + [pl.BlockSpec(memory_space=pl.ANY)],
        out_specs=[pl.BlockSpec((tb, 512), lambda n: (n, YC_LRU)),
                   pl.BlockSpec((tb, LRU_W), lambda n: (n, 0))],
        out_shape=[jax.ShapeDtypeStruct((lp, D_MODEL), BF16),
                   jax.ShapeDtypeStruct((lp, LRU_W), F32)],
        input_output_aliases={7: 0},
        scratch_shapes=[pltpu.VMEM((8, LRU_W), F32)],
        compiler_params=_cp(1),
    )(proj, proj, proj, wl, vec, wa, wx, ycat)


def _lru_bwd(name, proj, dycat, hstate, wl, vec, wa, wx, dproj):
    lp = proj.shape[0]
    tb = _row_tile(lp, 3)
    nb = lp // tb

    def body(rx_ref, rxp_ref, gate_ref, dy_ref, h_ref, hp_ref, wl_ref, vec_ref, wa_ref, wx_ref, din_ref,
             dp_ref, dwl_ref, dvec_ref, dwa_ref, dwx_ref, dhc_ref, anx_ref, dxc_ref):
        del din_ref
        n = pl.program_id(0)
        j = nb - 1 - n

        @pl.when(n == 0)
        def _():
            dhc_ref[...] = jnp.zeros(dhc_ref.shape, F32)
            anx_ref[...] = jnp.zeros(anx_ref.shape, F32)
            dxc_ref[...] = jnp.zeros(dxc_ref.shape, F32)
            dwl_ref[...] = jnp.zeros(dwl_ref.shape, F32)
            dvec_ref[...] = jnp.zeros(dvec_ref.shape, F32)
            dwa_ref[...] = jnp.zeros(dwa_ref.shape, F32)
            dwx_ref[...] = jnp.zeros(dwx_ref.shape, F32)

        c = _lru_chain(j, tb, rx_ref, rxp_ref, wl_ref, vec_ref, wa_ref, wx_ref)
        a, mult, r, ig, x1, live = c["a"], c["mult"], c["r"], c["ig"], c["x1"], c["live"]
        h = h_ref[...]
        gate = gate_ref[...]
        sgate = _sig(gate)
        dy = dy_ref[...].astype(F32)
        gsum = dy * (gate * sgate)
        dgate = dy * h * _dsilu(gate, sgate)
        rows = lax.broadcasted_iota(jnp.int32, (tb, LRU_W), 0)
        bb = jnp.where(rows == tb - 1, anx_ref[0:1, :], pltpu.roll(a, tb - 1, axis=0))
        gg = gsum
        d = 1
        while d < tb:
            keep = rows < tb - d
            bn = jnp.where(keep, pltpu.roll(bb, tb - d, axis=0), 1.0)
            gn = jnp.where(keep, pltpu.roll(gg, tb - d, axis=0), 0.0)
            gg = gg + bb * gn
            bb = bb * bn
            d *= 2
        dh = gg + bb * dhc_ref[0:1, :]
        dhc_ref[...] = dh[0:8, :]
        anx_ref[...] = a[0:8, :]
        hprev = jnp.where(rows == 0, jnp.where(j > 0, hp_ref[7:8, :], 0.0), pltpu.roll(h, 1, axis=0))
        du = jnp.where(live, dh, 0.0)
        da = jnp.where(live, dh * hprev, 0.0)
        ar = c["a_raw"]
        dmult = du * (ig * x1)
        di = du * mult * x1
        dx1 = du * mult * ig
        dloga = da * ar - dmult * ar * ar / mult
        dr = dloga * (-LRU_C * c["sp"])
        dvec_ref[3:4, :] += _colsum(dloga * (-LRU_C * r))
        dza = dr * r * (1.0 - r)
        dzx = di * ig * (1.0 - ig)
        dzab, dzxb = dza.astype(BF16), dzx.astype(BF16)
        dvec_ref[1:2, :] += _colsum(dza)
        dvec_ref[2:3, :] += _colsum(dzx)
        dwa_ref[...] += _dot(c["x1b"], dzab, TN)
        dwx_ref[...] += _dot(c["x1b"], dzxb, TN)
        dx1 = dx1 + _dot(dzab, wa_ref[...], NT) + _dot(dzxb, wx_ref[...], NT)
        dvec_ref[0:1, :] += _colsum(dx1)
        for k in range(LRU_CONV_K):
            dwl_ref[k:k + 1, :] += _colsum(dx1 * c["views"][LRU_CONV_K - 1 - k])
        dcat = jnp.concatenate([dx1, dxc_ref[...]], axis=0)
        drx = jnp.zeros((tb, LRU_W), F32)
        for k in range(LRU_CONV_K):
            s = LRU_CONV_K - 1 - k
            view = dcat[0:tb, :] if s == 0 else pltpu.roll(dcat, tb + 8 - s, axis=0)[0:tb, :]
            drx = drx + wl_ref[k:k + 1, :] * view
        dxc_ref[...] = dx1[0:8, :]
        dp_ref[:, 0:512] = drx.astype(BF16)
        dp_ref[:, 512:1024] = dgate.astype(BF16)

        @pl.when(n == nb - 1)
        def _():
            lam = vec_ref[3:4, :]
            dvec_ref[3:4, :] = dvec_ref[3:4, :] * (-_sig(-lam))

    jmap = lambda n: nb - 1 - n
    return pl.pallas_call(
        body, name=name, grid=(nb,),
        in_specs=(_lru_specs(jmap, tb)
                  + [pl.BlockSpec((tb, 512), lambda n: (jmap(n), YC_LRU)),
                     pl.BlockSpec((tb, LRU_W), lambda n: (jmap(n), 0)),
                     pl.BlockSpec((8, LRU_W), lambda n: (jnp.maximum(jmap(n) * (tb // 8) - 1, 0), 0))]
                  + _lru_param_specs() + [pl.BlockSpec(memory_space=pl.ANY)]),
        out_specs=[pl.BlockSpec((tb, 1024), lambda n: (jmap(n), 4)),
                   pl.BlockSpec((8, LRU_W), lambda n: (0, 0)),
                   pl.BlockSpec((8, LRU_W), lambda n: (0, 0)),
                   pl.BlockSpec((LRU_W, LRU_W), lambda n: (0, 0)),
                   pl.BlockSpec((LRU_W, LRU_W), lambda n: (0, 0))],
        out_shape=[jax.ShapeDtypeStruct((lp, IN_TOTAL), BF16),
                   jax.ShapeDtypeStruct((8, LRU_W), F32),
                   jax.ShapeDtypeStruct((8, LRU_W), F32),
                   jax.ShapeDtypeStruct((LRU_W, LRU_W), F32),
                   jax.ShapeDtypeStruct((LRU_W, LRU_W), F32)],
        input_output_aliases={10: 0},
        scratch_shapes=[pltpu.VMEM((8, LRU_W), F32), pltpu.VMEM((8, LRU_W), F32), pltpu.VMEM((8, LRU_W), F32)],
        compiler_params=_cp(1),
    )(proj, proj, proj, dycat, hstate, hstate, wl, vec, wa, wx, dproj)


_HBM = pl.BlockSpec(memory_space=pltpu.HBM)
_SEM = pl.BlockSpec(memory_space=pltpu.SEMAPHORE)
_ANY = pl.BlockSpec(memory_space=pl.ANY)
_EFFECT = pltpu.SideEffectType.DATAFLOW_SIDE_EFFECTING


def _hbm(a):
    return pltpu.with_memory_space_constraint(a, pltpu.HBM)


_ALL_PEERS = tuple(range(1, N_DEV))
_CHIP_PEERS = (1, 2, 4, 6)
_OTHER_CHIPS = (2, 4, 6)


def _spec_peers(mode):
    return {"ici": _CHIP_PEERS, "fwd": _OTHER_CHIPS}.get(mode, _ALL_PEERS)


def _split_descriptors(copies, srcs, lands, send_sems, recv_sems):
    x, y, c = lax.axis_index("x"), lax.axis_index("y"), lax.axis_index("c")
    me = 4 * x + 2 * y + c
    out, sem = [], 0
    for si, mode, li, ll in copies:
        for k in _spec_peers(mode):
            px = 1 - x if k & 4 else x
            py = 1 - y if k & 2 else y
            pc = 1 - c if k & 1 else c
            peer = 4 * px + 2 * py + pc
            if mode == "fwd":
                src = dst = lands[li].at[peer]
                target = (x, y, 1 - c)
            else:
                src = srcs[si].at[peer] if mode is True else srcs[si]
                dst = lands[li].at[me] if ll is None else lands[li].at[me, ll]
                target = (px, py, pc)
            out.append(pltpu.make_async_remote_copy(
                src_ref=src, dst_ref=dst, send_sem=send_sems.at[sem], recv_sem=recv_sems.at[sem],
                device_id=target, device_id_type=pl.DeviceIdType.MESH))
            sem += 1
    return out


def _n_copies(copies):
    return sum(len(_spec_peers(mode)) for _, mode, _, _ in copies)


def _xchg_start(name, groups):
    n_src = [len(g[0]) for g in groups]
    n_land = [len(g[1]) for g in groups]
    srcs = [s for g in groups for s in g[0]]
    lands = [l for g in groups for l in g[1]]
    ns, nl, ng = len(srcs), len(lands), len(groups)

    def body(*refs):
        src_refs, land_refs = refs[:ns], refs[ns:ns + nl]
        sems = refs[ns + nl:ns + nl + 2 * ng]
        token = refs[-1]
        so = lo = 0
        for gi, (_, _, copies) in enumerate(groups):
            for d in _split_descriptors(copies, src_refs[so:so + n_src[gi]], land_refs[lo:lo + n_land[gi]],
                                        sems[2 * gi], sems[2 * gi + 1]):
                d.start()
            so += n_src[gi]
            lo += n_land[gi]
        token[...] = jnp.zeros(token.shape, F32)

    out_shape, out_specs = [], []
    for g in groups:
        n = _n_copies(g[2])
        out_shape += [pltpu.SemaphoreType.DMA((n,)), pltpu.SemaphoreType.DMA((n,))]
        out_specs += [_SEM, _SEM]
    out_shape += [pltpu.HBM(l.shape, l.dtype) for l in lands]
    out_specs += [_HBM] * nl
    out_shape.append(jax.ShapeDtypeStruct((8, 128), F32))
    out_specs.append(pl.BlockSpec(memory_space=pltpu.VMEM))
    outs = pl.pallas_call(
        body, name=name, in_specs=[_HBM] * (ns + nl), out_specs=out_specs, out_shape=out_shape,
        input_output_aliases={ns + i: 2 * ng + i for i in range(nl)},
        compiler_params=pltpu.CompilerParams(has_side_effects=_EFFECT),
    )(*[_hbm(a) for a in srcs + lands])
    res, lo = [], 2 * ng
    for gi in range(ng):
        res.append((outs[2 * gi], outs[2 * gi + 1], list(outs[lo:lo + n_land[gi]])))
        lo += n_land[gi]
    return res, outs[-1]


def _xchg_wait(name, group, started, after):
    srcs, _, copies = group
    send_sems, recv_sems, lands = started
    ns, nl = len(srcs), len(lands)
    after = list(after)

    def body(*refs):
        src_refs, land_refs = refs[:ns], refs[ns:ns + nl]
        send_ref, recv_ref = refs[ns + nl], refs[ns + nl + 1]
        for d in _split_descriptors(copies, src_refs, land_refs, send_ref, recv_ref):
            d.wait_send()
            d.wait_recv()

    outs = pl.pallas_call(
        body, name=name, in_specs=[_HBM] * (ns + nl) + [_SEM, _SEM] + [_ANY] * len(after),
        out_specs=[_HBM] * nl, out_shape=[pltpu.HBM(l.shape, l.dtype) for l in lands],
        input_output_aliases={ns + i: i for i in range(nl)},
        compiler_params=pltpu.CompilerParams(has_side_effects=_EFFECT),
    )(*[_hbm(a) for a in srcs], *lands, send_sems, recv_sems, *after)
    return list(outs)


def _landing(own, me):
    land = lax.empty((N_DEV,) + own.shape, own.dtype)
    return lax.dynamic_update_slice(land, own[None], (me,) + (0,) * own.ndim)


def _adamw(name, w, m, v, recv, row0=0, prev=None):
    cdim = w.shape[1]
    r = recv.shape[1]
    tr = r
    for cand in (512, 256, 128, 64, 32, 16, 8):
        if r % cand == 0 and r > cand:
            tr = cand
            break
    assert row0 % tr == 0
    blk0 = row0 // tr
    n_prev = 0 if prev is None else 4

    def body(w_ref, m_ref, v_ref, r_ref, *rest):
        g_ref, d_ref, mo_ref, vo_ref = rest[n_prev:]
        g = r_ref[0].astype(F32)
        for s in range(1, N_DEV):
            g = g + r_ref[s].astype(F32)
        mn = ADAM_B1 * m_ref[...] + (1.0 - ADAM_B1) * g
        vn = ADAM_B2 * v_ref[...] + (1.0 - ADAM_B2) * (g * g)
        m_hat = mn / (1.0 - ADAM_B1 ** ADAM_STEP)
        v_hat = vn / (1.0 - ADAM_B2 ** ADAM_STEP)
        g_ref[...] = g
        d_ref[...] = -ADAM_LR * (m_hat / (jnp.sqrt(v_hat) + ADAM_EPS) + ADAM_WD * w_ref[...])
        mo_ref[...] = mn
        vo_ref[...] = vn

    blk = pl.BlockSpec((tr, cdim), lambda i: (i + blk0, 0))
    return pl.pallas_call(
        body, name=name, grid=(r // tr,),
        in_specs=[blk, blk, blk, pl.BlockSpec((N_DEV, tr, cdim), lambda i: (0, i, 0))] + [_ANY] * n_prev,
        out_specs=[blk, blk, blk, blk],
        out_shape=[jax.ShapeDtypeStruct(w.shape, F32)] * 4,
        input_output_aliases={4 + i: i for i in range(n_prev)},
        compiler_params=_cp(1),
    )(w, m, v, recv, *(prev or []))


def _pack_rows(arrs, lead=()):
    n = len(lead)
    flat = jnp.concatenate([a.reshape(a.shape[:n] + (-1,)) for a in arrs], axis=-1)
    size = flat.shape[-1]
    padded = -(-size // PACK_QUANTUM) * PACK_QUANTUM
    flat = jnp.pad(flat, [(0, 0)] * n + [(0, padded - size)])
    return flat.reshape(flat.shape[:n] + (padded // 128, 128))


def _unpack_rows(packed, shapes, lead=()):
    n = len(lead)
    flat = packed.reshape(packed.shape[:n] + (-1,))
    out, off = [], 0
    for s in shapes:
        size = int(np.prod(s))
        out.append(flat[..., off:off + size].reshape(packed.shape[:n] + tuple(s)))
        off += size
    return out


def _block_diag(w):
    eye = jnp.eye(LRU_HEADS, dtype=w.dtype)
    return (eye[:, None, :, None] * w[:, :, None, :]).reshape(LRU_W, LRU_W)


def _diag_blocks(dense):
    t = dense.reshape(LRU_HEADS, 64, LRU_HEADS, 64)
    return jnp.stack([t[h, :, h, :] for h in range(LRU_HEADS)], axis=0)


def _cols_to_slots(full):
    lead = full.shape[:-1]
    t = full.reshape(lead + (N_DEV, full.shape[-1] // N_DEV))
    return jnp.moveaxis(t, -2, 0)


def _slots_to_cols(slots):
    t = jnp.moveaxis(slots, 0, -2)
    return t.reshape(t.shape[:-2] + (t.shape[-2] * t.shape[-1],))


def kernel(x, meta_tokens, ln_in_g, ln_in_b, w_in, conv_dw_w, conv_dw_b, conv_ln_g, conv_ln_b, conv_pw_w, conv_pw_b, attn_sinks, lru_conv_w, lru_conv_b, lru_wa, lru_ba, lru_wx, lru_bx, lru_lambda, w_out, ln_post_g, ln_post_b, loss_target, m_meta_tokens, m_ln_in_g, m_ln_in_b, m_w_in, m_conv_dw_w, m_conv_dw_b, m_conv_ln_g, m_conv_ln_b, m_conv_pw_w, m_conv_pw_b, m_attn_sinks, m_lru_conv_w, m_lru_conv_b, m_lru_wa, m_lru_ba, m_lru_wx, m_lru_bx, m_lru_lambda, m_w_out, m_ln_post_g, m_ln_post_b, v_meta_tokens, v_ln_in_g, v_ln_in_b, v_w_in, v_conv_dw_w, v_conv_dw_b, v_conv_ln_g, v_conv_ln_b, v_conv_pw_w, v_conv_pw_b, v_attn_sinks, v_lru_conv_w, v_lru_conv_b, v_lru_wa, v_lru_ba, v_lru_wx, v_lru_bx, v_lru_lambda, v_w_out, v_ln_post_g, v_ln_post_b):
    seq = x.shape[1]
    lp = seq + TB
    row = lambda a: a.reshape(1, -1)
    rep_names = ["ln_in_g", "ln_in_b", "conv_dw_b", "conv_ln_g", "conv_ln_b", "conv_pw_b", "attn_sinks",
                 "lru_conv_b", "lru_wa", "lru_ba", "lru_wx", "lru_bx", "lru_lambda", "ln_post_g", "ln_post_b"]
    shard_small_names = ["conv_dw_w", "lru_conv_w", "meta_tokens"]
    weights = dict(meta_tokens=meta_tokens, ln_in_g=ln_in_g, ln_in_b=ln_in_b, w_in=w_in, conv_dw_w=conv_dw_w,
                   conv_dw_b=conv_dw_b, conv_ln_g=conv_ln_g, conv_ln_b=conv_ln_b, conv_pw_w=conv_pw_w,
                   conv_pw_b=conv_pw_b, attn_sinks=attn_sinks, lru_conv_w=lru_conv_w, lru_conv_b=lru_conv_b,
                   lru_wa=lru_wa, lru_ba=lru_ba, lru_wx=lru_wx, lru_bx=lru_bx, lru_lambda=lru_lambda,
                   w_out=w_out, ln_post_g=ln_post_g, ln_post_b=ln_post_b)
    mom1 = dict(meta_tokens=m_meta_tokens, ln_in_g=m_ln_in_g, ln_in_b=m_ln_in_b, w_in=m_w_in, conv_dw_w=m_conv_dw_w,
                conv_dw_b=m_conv_dw_b, conv_ln_g=m_conv_ln_g, conv_ln_b=m_conv_ln_b, conv_pw_w=m_conv_pw_w,
                conv_pw_b=m_conv_pw_b, attn_sinks=m_attn_sinks, lru_conv_w=m_lru_conv_w, lru_conv_b=m_lru_conv_b,
                lru_wa=m_lru_wa, lru_ba=m_lru_ba, lru_wx=m_lru_wx, lru_bx=m_lru_bx, lru_lambda=m_lru_lambda,
                w_out=m_w_out, ln_post_g=m_ln_post_g, ln_post_b=m_ln_post_b)
    mom2 = dict(meta_tokens=v_meta_tokens, ln_in_g=v_ln_in_g, ln_in_b=v_ln_in_b, w_in=v_w_in, conv_dw_w=v_conv_dw_w,
                conv_dw_b=v_conv_dw_b, conv_ln_g=v_conv_ln_g, conv_ln_b=v_conv_ln_b, conv_pw_w=v_conv_pw_w,
                conv_pw_b=v_conv_pw_b, attn_sinks=v_attn_sinks, lru_conv_w=v_lru_conv_w, lru_conv_b=v_lru_conv_b,
                lru_wa=v_lru_wa, lru_ba=v_lru_ba, lru_wx=v_lru_wx, lru_bx=v_lru_bx, lru_lambda=v_lru_lambda,
                w_out=v_w_out, ln_post_g=v_ln_post_g, ln_post_b=v_ln_post_b)
    shard_wmv = [_pack_rows([d[n] for n in shard_small_names]) for d in (weights, mom1, mom2)]
    rep_wmv = [_pack_rows([d[n] for n in rep_names]) for d in (weights, mom1, mom2)]
    gate_w = [(_block_diag(lru_wa[l]).astype(BF16), _block_diag(lru_wx[l]).astype(BF16)) for l in range(DEPTH)]
    tabs = _rope_tables(lp)
    prepared = shard_wmv + rep_wmv + [w for pair in gate_w for w in pair] + list(tabs)

    small_shard_shapes = [conv_dw_w.shape, lru_conv_w.shape, meta_tokens.shape]
    small_shard = _pack_rows([conv_dw_w, lru_conv_w, meta_tokens])
    me = 4 * lax.axis_index("x") + 2 * lax.axis_index("y") + lax.axis_index("c")
    w_in_b = [w_in[l].astype(BF16) for l in range(DEPTH)]
    w_out_b = [w_out[l].astype(BF16) for l in range(DEPTH)]
    pw_b = conv_pw_w.astype(BF16)
    wgroups = [
        ([small_shard], [_landing(small_shard, me)], [(0, False, 0, None)]),
        ([w_in_b[0]], [_landing(w_in_b[0], me)], [(0, "ici", 0, None)]),
        ([pw_b, w_out_b[0]], [_landing(pw_b, me), _landing(w_out_b[0], me)],
         [(0, False, 0, None), (1, False, 1, None)]),
        ([w_in_b[1], w_out_b[1]], [_landing(w_in_b[1], me), _landing(w_out_b[1], me)],
         [(0, False, 0, None), (1, False, 1, None)]),
    ]
    wstarted, wtoken = _xchg_start("weights_start", wgroups)
    wg_small, = _xchg_wait("weights_wait_s", wgroups[0], wstarted[0], [wtoken])
    g_dw, g_lc, g_meta = _unpack_rows(wg_small, small_shard_shapes, lead=(N_DEV,))
    conv_dw_full = _slots_to_cols(g_dw)
    lru_conv_full = _slots_to_cols(g_lc)
    meta_full = _slots_to_cols(g_meta)
    wg_in = [None, None]
    wg_out = [None, None]
    wg_pw = None

    ln_g = [ln_in_g, ln_post_g[0], ln_post_g[1]]
    ln_b = [ln_in_b, ln_post_b[0], ln_post_b[1]]

    def layer_params(l):
        wdw = jnp.pad(conv_dw_full[l], ((0, 1), (0, 0)))
        cvec = jnp.pad(jnp.stack([conv_dw_b[l], conv_ln_g[l], conv_ln_b[l], conv_pw_b[l]]), ((0, 4), (0, 0)))
        wpw = wg_pw[:, l].reshape(CONV_W, CONV_W)
        sinks = jnp.pad(attn_sinks[l].reshape(1, N_HEADS), ((0, 7), (0, 128 - N_HEADS)))
        wl = jnp.pad(lru_conv_full[l], ((0, 4), (0, 0)))
        lvec = jnp.pad(jnp.stack([lru_conv_b[l], lru_ba[l], lru_bx[l], lru_lambda[l]]), ((0, 4), (0, 0)))
        wa, wx = gate_w[l]
        wo = wg_out[l].reshape(D_MODEL, D_MODEL)
        wout = jnp.concatenate([wo[512:1536], wo[0:512], wo[1536:]], axis=0)
        return dict(wdw=wdw, cvec=cvec, wpw=wpw, sinks=sinks, wl=wl, lvec=lvec, wa=wa, wx=wx, wout=wout)

    params = [None] * DEPTH

    z0, hb = _embed(x, meta_full, row(ln_g[0]), row(ln_b[0]))
    z = [z0]
    saved = []
    for l in range(DEPTH):
        if l == 0:
            part, = _xchg_wait("weights_wait_a", wgroups[1], wstarted[1], [hb] + prepared)
            fwd = ([], [part], [(None, "fwd", 0, None)])
            fstarted, ftoken = _xchg_start("weights_fwd_start", [fwd])
            wg_in[0], = _xchg_wait("weights_fwd_wait", fwd, fstarted[0], [ftoken])
        else:
            wg_in[1], wg_out[1] = _xchg_wait("weights_wait_c", wgroups[3], wstarted[3], [hb])
        proj = _mm_proj(f"proj{l}", hb, wg_in[l])
        if l == 0:
            wg_pw, wg_out[0] = _xchg_wait("weights_wait_b", wgroups[2], wstarted[2], [proj])
        p = params[l] = layer_params(l)
        ycat, c1 = _conv_fwd(f"conv_fwd{l}", proj, p["wdw"], p["cvec"], p["wpw"])
        qr, kr = _rope_fwd(f"rope{l}", proj, tabs)
        ycat = _attn_fwd(f"attn_fwd{l}", qr, kr, proj, p["sinks"], ycat)
        ycat, hstate = _lru_fwd(f"lru_fwd{l}", proj, p["wl"], p["lvec"], p["wa"], p["wx"], ycat)
        saved.append(dict(hb=hb, proj=proj, ycat=ycat, qr=qr, kr=kr, hstate=hstate, c1=c1))
        z_next, hb = _mm_out(f"out{l}", ycat, p["wout"], z[l], row(ln_g[l]), row(ln_b[l]),
                             row(ln_g[l + 1]), row(ln_b[l + 1]))
        z.append(z_next)

    dz, st_post1, loss_blk = _loss_head(z[DEPTH], loss_target, row(ln_g[DEPTH]), row(ln_b[DEPTH]))
    loss = lax.psum(loss_blk[0, 0], ("x", "y", "c"))

    ln_stats = {DEPTH: st_post1}
    g_layers = [None] * DEPTH
    dwin_l, dwout_l = [None] * DEPTH, [None] * DEPTH
    grad_x = gmeta = None
    token = wtoken
    ggroups = [None] * DEPTH
    own = lambda a: lax.dynamic_index_in_dim(a, me, 0, keepdims=False)
    for l in reversed(range(DEPTH)):
        p, s = params[l], saved[l]
        dycat = _mm_dycat(f"dycat{l}", dz, p["wout"], token)
        dwout_l[l] = _mm_dwout(f"dwout{l}", s["ycat"], dz)
        dproj, dwdw, dcvec, dwpw = _conv_bwd(f"conv_bwd{l}", s["proj"], dycat, s["c1"], p["wdw"], p["cvec"], p["wpw"])
        dwo = jnp.concatenate([dwout_l[l][1024:1536], dwout_l[l][0:1024], dwout_l[l][1536:]], axis=0)
        dwo = dwo.reshape(N_DEV, D_MODEL // N_DEV, D_MODEL)
        dpw = dwpw.reshape(N_DEV, CONV_W // N_DEV, CONV_W)
        early = ([dwo, dpw], [_landing(own(dwo), me), _landing(own(dpw), me)],
                 [(0, True, 0, None), (1, True, 1, None)])
        started_early, token = _xchg_start(f"grads_start_out{l}", [early])
        dq, dgate, dk, dv, dsink = _attn_bwd(f"attn_bwd{l}", s["qr"], s["kr"], s["proj"], p["sinks"], dycat, token)
        dproj = _attn_assemble(f"attn_asm{l}", dq, dgate, dk, dv, tabs, dproj)
        dproj, dwl, dlvec, dwa, dwx = _lru_bwd(f"lru_bwd{l}", s["proj"], dycat, s["hstate"],
                                                p["wl"], p["lvec"], p["wa"], p["wx"], dproj)
        dwin_l[l] = _mm_dwin(f"dwin{l}", s["hb"], dproj)
        late = ([dwin_l[l]], [_landing(own(dwin_l[l]), me)], [(0, True, 0, None)])
        started_late, token = _xchg_start(f"grads_start_in{l}", [late])
        ggroups[l] = [(late, started_late[0]), (early, started_early[0])]
        dh = _mm_dh(f"dh{l}", dproj, wg_in[l], dz, token)
        if l > 0:
            dz, ln_stats[l] = _ln_bwd(f"ln_bwd{l}", dh, z[l], row(ln_g[l]))
        else:
            grad_x, gmeta, ln_stats[0] = _ln_bwd_input(dh, z[0], row(ln_g[0]))
        g_layers[l] = dict(dwdw=dwdw[:CONV_K], dcvec=dcvec, dwpw=dwpw, dsink=dsink[0, :N_HEADS],
                           dwl=dwl[:LRU_CONV_K], dlvec=dlvec, dwa=_diag_blocks(dwa), dwx=_diag_blocks(dwx))

    stack = lambda f: jnp.stack([f(g_layers[l]) for l in range(DEPTH)])
    g_local = dict(
        ln_in_g=ln_stats[0][0], ln_in_b=ln_stats[0][1],
        conv_dw_b=stack(lambda g: g["dcvec"][0]), conv_ln_g=stack(lambda g: g["dcvec"][1]),
        conv_ln_b=stack(lambda g: g["dcvec"][2]), conv_pw_b=stack(lambda g: g["dcvec"][3]),
        attn_sinks=stack(lambda g: g["dsink"]),
        lru_conv_b=stack(lambda g: g["dlvec"][0]), lru_wa=stack(lambda g: g["dwa"]),
        lru_ba=stack(lambda g: g["dlvec"][1]), lru_wx=stack(lambda g: g["dwx"]),
        lru_bx=stack(lambda g: g["dlvec"][2]), lru_lambda=stack(lambda g: g["dlvec"][3]),
        ln_post_g=jnp.stack([ln_stats[1][0], ln_stats[2][0]]),
        ln_post_b=jnp.stack([ln_stats[1][1], ln_stats[2][1]]),
    )
    rep_pack = _pack_rows([g_local[n] for n in rep_names])
    g_dw_full = jnp.stack([g_layers[l]["dwdw"] for l in range(DEPTH)])
    g_lc_full = jnp.stack([g_layers[l]["dwl"] for l in range(DEPTH)])
    shard_pack = _pack_rows([_cols_to_slots(g_dw_full), _cols_to_slots(g_lc_full), _cols_to_slots(gmeta)],
                            lead=(N_DEV,))
    sgroup = ([shard_pack, rep_pack], [_landing(own(shard_pack), me), _landing(rep_pack, me)],
              [(0, True, 0, None), (1, False, 1, None)])
    sstarted, token = _xchg_start("small_grads_start", [sgroup])

    res = {}

    def flat2(a, cols):
        return a.reshape(-1, cols)

    big = (("w_in", 0, W_IN_SHARD), ("w_out", 1, D_MODEL), ("conv_pw_w", 2, CONV_W))
    prev = {n: None for n, _, _ in big}
    after = token
    for l in reversed(range(DEPTH)):
        recvs = []
        for gi, (grp, started) in enumerate(ggroups[l]):
            recvs += _xchg_wait(f"grads_wait{l}_{gi}", grp, started, [after])
        for name_, gi, cols in big:
            w_ = weights[name_]
            rows = w_.shape[1]
            prev[name_] = _adamw(f"adamw_{name_}{l}", flat2(w_, cols), flat2(mom1[name_], cols),
                                 flat2(mom2[name_], cols), recvs[gi], row0=l * rows, prev=prev[name_])
        after = prev["w_in"][0]
    for name_, _, _ in big:
        res[name_] = [o.reshape(weights[name_].shape) for o in prev[name_]]

    r_small, r_rep = _xchg_wait("small_grads_wait", sgroup, sstarted[0], [prev[n][0] for n, _, _ in big])
    sshapes = [weights[n].shape for n in shard_small_names]
    outs = _adamw("adamw_small_sharded", *shard_wmv, r_small)
    for k, o in enumerate(outs):
        for n, a in zip(shard_small_names, _unpack_rows(o, sshapes)):
            res.setdefault(n, [None] * 4)[k] = a

    rshapes = [weights[n].shape for n in rep_names]
    outs = _adamw("adamw_replicated", *rep_wmv, r_rep)
    for k, o in enumerate(outs):
        for n, a in zip(rep_names, _unpack_rows(o, rshapes)):
            res.setdefault(n, [None] * 4)[k] = a

    order = ["meta_tokens", "ln_in_g", "ln_in_b", "w_in", "conv_dw_w", "conv_dw_b", "conv_ln_g", "conv_ln_b",
             "conv_pw_w", "conv_pw_b", "attn_sinks", "lru_conv_w", "lru_conv_b", "lru_wa", "lru_ba", "lru_wx",
             "lru_bx", "lru_lambda", "w_out", "ln_post_g", "ln_post_b"]
    return (loss, grad_x,
            *[res[n][0] for n in order], *[res[n][1] for n in order],
            *[res[n][2] for n in order], *[res[n][3] for n in order])
```

```python
import functools
import math

import numpy as np
import jax
import jax.numpy as jnp
from jax import lax
from jax.experimental import pallas as pl
from jax.experimental.pallas import tpu as pltpu

F32 = jnp.float32
BF16 = jnp.bfloat16

D_MODEL = 2048
DEPTH = 2
N_META = 16
TB = 128
PAD0 = TB - N_META
CONV_W = 512
CONV_K = 31
HEAD_DIM = 64
N_HEADS = 16
N_KV = 4
GROUP = 4
ATT_W = 1024
KV_W = 256
ROT_DIM = 16
ROPE_THETA = 500000.0
LRU_W = 512
LRU_HEADS = 8
LRU_CONV_K = 4
LRU_C = 8.0
IN_TOTAL = 5120
N_DEV = 8
W_IN_SHARD = IN_TOTAL // N_DEV
LN_EPS = 1e-5
ALPHA = (2.0 * DEPTH) ** 0.25
NEG_INF = -1e30
ATT_SCALE = HEAD_DIM ** -0.5

ADAM_LR = 0.001
ADAM_B1 = 0.9
ADAM_B2 = 0.999
ADAM_EPS = 1e-08
ADAM_WD = 0.01
ADAM_STEP = 10

VMEM_LIMIT = 56 * 1024 * 1024
PACK_QUANTUM = 256 * 128

COL_CV, COL_CG, COL_CGATE = 0, 1, 2
COL_Q0 = 3
COL_K256 = 10
COL_V256 = 11
COL_AGATE1024 = 3
COL_RX, COL_RGATE = 8, 9
YC_CONV, YC_LRU = 2, 3


def _cp(n_axes, vmem=VMEM_LIMIT):
    return pltpu.CompilerParams(dimension_semantics=("arbitrary",) * n_axes, vmem_limit_bytes=vmem)


def _row_tile(lp, max_blocks):
    nb = lp // TB
    d = max(k for k in range(1, max_blocks + 1) if nb % k == 0)
    return TB * d


def _sig(x):
    return jax.nn.sigmoid(x)


def _dsilu(x, s):
    return s * (1.0 + x * (1.0 - s))


def _ln_core(z):
    mu = jnp.mean(z, axis=-1, keepdims=True)
    zc = z - mu
    var = jnp.mean(zc * zc, axis=-1, keepdims=True)
    rstd = lax.rsqrt(var + LN_EPS)
    return zc * rstd, rstd


def _ln_bwd_core(dy, xh, rstd, g):
    dxh = dy * g
    m1 = jnp.mean(dxh, axis=-1, keepdims=True)
    m2 = jnp.mean(dxh * xh, axis=-1, keepdims=True)
    return rstd * (dxh - m1 - xh * m2)


def _row_ids(shape, base):
    return lax.broadcasted_iota(jnp.int32, shape, 0) + base


def _colsum(x):
    return jnp.sum(x, axis=0, keepdims=True)


def _dot(a, b, dims):
    return lax.dot_general(a, b, (dims, ((), ())), preferred_element_type=F32)


NN = ((1,), (0,))
NT = ((1,), (1,))
TN = ((0,), (0,))


def _embed(x, meta_full, g, b):
    s = x.shape[1]
    lp = s + TB
    nb = lp // TB

    def body(x_ref, m_ref, g_ref, b_ref, o_ref, hb_ref):
        i = pl.program_id(0)

        @pl.when(i == 0)
        def _():
            o_ref[0:PAD0, :] = jnp.zeros((PAD0, D_MODEL), F32)
            o_ref[PAD0:TB, :] = m_ref[...]

        @pl.when(i > 0)
        def _():
            o_ref[...] = x_ref[...]

        xh, _ = _ln_core(o_ref[...])
        h = xh * g_ref[...] + b_ref[...]
        rows = _row_ids(h.shape, i * TB)
        hb_ref[...] = jnp.where(rows >= PAD0, h, 0.0).astype(BF16)

    return pl.pallas_call(
        body, name="embed", grid=(nb,),
        in_specs=[pl.BlockSpec((None, TB, D_MODEL), lambda i: (0, jnp.maximum(i - 1, 0), 0)),
                  pl.BlockSpec((N_META, D_MODEL), lambda i: (0, 0)),
                  pl.BlockSpec((1, D_MODEL), lambda i: (0, 0)),
                  pl.BlockSpec((1, D_MODEL), lambda i: (0, 0))],
        out_specs=[pl.BlockSpec((TB, D_MODEL), lambda i: (i, 0)),
                   pl.BlockSpec((TB, D_MODEL), lambda i: (i, 0))],
        out_shape=[jax.ShapeDtypeStruct((lp, D_MODEL), F32),
                   jax.ShapeDtypeStruct((lp, D_MODEL), BF16)],
        compiler_params=_cp(1),
    )(x, meta_full, g, b)


def _loss_head(z, target, g, b):
    lp = z.shape[0]
    nb = lp // TB

    def body(z_ref, t_ref, g_ref, b_ref, dz_ref, st_ref, loss_ref):
        i = pl.program_id(0)

        @pl.when(i == 0)
        def _():
            st_ref[...] = jnp.zeros(st_ref.shape, F32)
            loss_ref[...] = jnp.zeros(loss_ref.shape, F32)
            dz_ref[...] = jnp.zeros(dz_ref.shape, F32)

        @pl.when(i > 0)
        def _():
            xh, rstd = _ln_core(z_ref[...])
            gg = g_ref[...]
            y = xh * gg + b_ref[...]
            e = y - t_ref[...]
            part = 0.5 * jnp.sum(jnp.mean(e * e, axis=-1, keepdims=True), axis=0, keepdims=True)
            loss_ref[...] += jnp.broadcast_to(part, loss_ref.shape)
            dy = e / float(D_MODEL)
            st_ref[0:1, :] += _colsum(dy * xh)
            st_ref[1:2, :] += _colsum(dy)
            dz_ref[...] = _ln_bwd_core(dy, xh, rstd, gg)

    return pl.pallas_call(
        body, name="loss_head", grid=(nb,),
        in_specs=[pl.BlockSpec((TB, D_MODEL), lambda i: (i, 0)),
                  pl.BlockSpec((None, TB, D_MODEL), lambda i: (0, jnp.maximum(i - 1, 0), 0)),
                  pl.BlockSpec((1, D_MODEL), lambda i: (0, 0)),
                  pl.BlockSpec((1, D_MODEL), lambda i: (0, 0))],
        out_specs=[pl.BlockSpec((TB, D_MODEL), lambda i: (i, 0)),
                   pl.BlockSpec((8, D_MODEL), lambda i: (0, 0)),
                   pl.BlockSpec((8, 128), lambda i: (0, 0))],
        out_shape=[jax.ShapeDtypeStruct((lp, D_MODEL), F32),
                   jax.ShapeDtypeStruct((8, D_MODEL), F32),
                   jax.ShapeDtypeStruct((8, 128), F32)],
        compiler_params=_cp(1),
    )(z, target, g, b)


def _ln_bwd(name, dh, z, g):
    lp = z.shape[0]
    tr = _row_tile(lp, 3)

    def body(dh_ref, z_ref, g_ref, dz_ref, st_ref):
        i = pl.program_id(0)

        @pl.when(i == 0)
        def _():
            st_ref[...] = jnp.zeros(st_ref.shape, F32)

        xh, rstd = _ln_core(z_ref[...])
        rows = _row_ids(xh.shape, i * tr)
        dy = jnp.where(rows >= PAD0, dh_ref[...], 0.0)
        st_ref[0:1, :] += _colsum(dy * xh)
        st_ref[1:2, :] += _colsum(dy)
        dz_ref[...] = _ln_bwd_core(dy, xh, rstd, g_ref[...])

    return pl.pallas_call(
        body, name=name, grid=(lp // tr,),
        in_specs=[pl.BlockSpec((tr, D_MODEL), lambda i: (i, 0)),
                  pl.BlockSpec((tr, D_MODEL), lambda i: (i, 0)),
                  pl.BlockSpec((1, D_MODEL), lambda i: (0, 0))],
        out_specs=[pl.BlockSpec((tr, D_MODEL), lambda i: (i, 0)),
                   pl.BlockSpec((8, D_MODEL), lambda i: (0, 0))],
        out_shape=[jax.ShapeDtypeStruct((lp, D_MODEL), F32),
                   jax.ShapeDtypeStruct((8, D_MODEL), F32)],
        compiler_params=_cp(1),
    )(dh, z, g)


def _ln_bwd_input(dh, z, g):
    lp = z.shape[0]
    nb = lp // TB
    s = lp - TB

    def body(dh_ref, z_ref, g_ref, gx_ref, gm_ref, st_ref):
        i = pl.program_id(0)

        @pl.when(i == 0)
        def _():
            st_ref[...] = jnp.zeros(st_ref.shape, F32)

        xh, rstd = _ln_core(z_ref[...])
        rows = _row_ids(xh.shape, i * TB)
        dy = jnp.where(rows >= PAD0, dh_ref[...], 0.0)
        st_ref[0:1, :] += _colsum(dy * xh)
        st_ref[1:2, :] += _colsum(dy)
        dz = _ln_bwd_core(dy, xh, rstd, g_ref[...])
        gx_ref[...] = dz

        @pl.when(i == 0)
        def _():
            gm_ref[...] = dz[PAD0:TB, :]

    return pl.pallas_call(
        body, name="ln_in_bwd", grid=(nb,),
        in_specs=[pl.BlockSpec((TB, D_MODEL), lambda i: (i, 0)),
                  pl.BlockSpec((TB, D_MODEL), lambda i: (i, 0)),
                  pl.BlockSpec((1, D_MODEL), lambda i: (0, 0))],
        out_specs=[pl.BlockSpec((None, TB, D_MODEL), lambda i: (0, jnp.maximum(i - 1, 0), 0)),
                   pl.BlockSpec((N_META, D_MODEL), lambda i: (0, 0)),
                   pl.BlockSpec((8, D_MODEL), lambda i: (0, 0))],
        out_shape=[jax.ShapeDtypeStruct((1, s, D_MODEL), F32),
                   jax.ShapeDtypeStruct((N_META, D_MODEL), F32),
                   jax.ShapeDtypeStruct((8, D_MODEL), F32)],
        compiler_params=_cp(1),
    )(dh, z, g)


def _mm_proj(name, hb, wg_in):
    lp = hb.shape[0]
    tm = lp // 3

    def body(a_ref, b_ref, o_ref):
        b = jnp.concatenate([b_ref[0], b_ref[1]], axis=1)
        o_ref[...] = _dot(a_ref[...], b, NN)

    return pl.pallas_call(
        body, name=name, grid=(3, N_DEV // 2),
        in_specs=[pl.BlockSpec((tm, D_MODEL), lambda i, j: (i, 0)),
                  pl.BlockSpec((2, D_MODEL, W_IN_SHARD), lambda i, j: (j, 0, 0))],
        out_specs=pl.BlockSpec((tm, 2 * W_IN_SHARD), lambda i, j: (i, j)),
        out_shape=jax.ShapeDtypeStruct((lp, IN_TOTAL), F32),
        compiler_params=_cp(2),
    )(hb, wg_in)


def _mm_out(name, ycat, wout, z, g, b, g2, b2):
    lp = ycat.shape[0]
    tm = lp // 6

    def body(a_ref, w_ref, z_ref, g_ref, b_ref, g2_ref, b2_ref, o_ref, hb_ref):
        i = pl.program_id(0)
        xh, _ = _ln_core(z_ref[...])
        h = xh * g_ref[...] + b_ref[...]
        live = _row_ids(h.shape, i * tm) >= PAD0
        h = jnp.where(live, h, 0.0)
        zn = ALPHA * h + _dot(a_ref[...], w_ref[...], NN)
        o_ref[...] = zn
        xh2, _ = _ln_core(zn)
        hb_ref[...] = jnp.where(live, xh2 * g2_ref[...] + b2_ref[...], 0.0).astype(BF16)

    vec = pl.BlockSpec((1, D_MODEL), lambda i: (0, 0))
    return pl.pallas_call(
        body, name=name, grid=(6,),
        in_specs=[pl.BlockSpec((tm, D_MODEL), lambda i: (i, 0)),
                  pl.BlockSpec((D_MODEL, D_MODEL), lambda i: (0, 0), pipeline_mode=pl.Buffered(1)),
                  pl.BlockSpec((tm, D_MODEL), lambda i: (i, 0)),
                  vec, vec, vec, vec],
        out_specs=[pl.BlockSpec((tm, D_MODEL), lambda i: (i, 0)),
                   pl.BlockSpec((tm, D_MODEL), lambda i: (i, 0))],
        out_shape=[jax.ShapeDtypeStruct((lp, D_MODEL), F32),
                   jax.ShapeDtypeStruct((lp, D_MODEL), BF16)],
        compiler_params=_cp(1),
    )(ycat, wout, z, g, b, g2, b2)


def _mm_dycat(name, dz, wout, dep):
    lp = dz.shape[0]
    tm = lp // 6

    def body(a_ref, w_ref, dep_ref, o_ref):
        del dep_ref
        o_ref[...] = _dot(a_ref[...].astype(BF16), w_ref[...], NT).astype(BF16)

    return pl.pallas_call(
        body, name=name, grid=(6,),
        in_specs=[pl.BlockSpec((tm, D_MODEL), lambda i: (i, 0)),
                  pl.BlockSpec((D_MODEL, D_MODEL), lambda i: (0, 0), pipeline_mode=pl.Buffered(1)),
                  pl.BlockSpec(memory_space=pl.ANY)],
        out_specs=pl.BlockSpec((tm, D_MODEL), lambda i: (i, 0)),
        out_shape=jax.ShapeDtypeStruct((lp, D_MODEL), BF16),
        compiler_params=_cp(1),
    )(dz, wout, dep)


def _mm_dwout(name, ycat, dz):
    lp = ycat.shape[0]
    tk = _row_tile(lp, 11)
    nk = lp // tk
    half = D_MODEL // 2

    def body(a_ref, b_ref, o_ref, acc_ref):
        k = pl.program_id(1)

        @pl.when(k == 0)
        def _():
            acc_ref[...] = jnp.zeros(acc_ref.shape, F32)

        acc_ref[...] += _dot(a_ref[...], b_ref[...].astype(BF16), TN)

        @pl.when(k == nk - 1)
        def _():
            o_ref[...] = acc_ref[...].astype(BF16)

    return pl.pallas_call(
        body, name=name, grid=(2, nk),
        in_specs=[pl.BlockSpec((tk, half), lambda h, k: (k, h)),
                  pl.BlockSpec((tk, D_MODEL), lambda h, k: (k, 0))],
        out_specs=pl.BlockSpec((half, D_MODEL), lambda h, k: (h, 0)),
        out_shape=jax.ShapeDtypeStruct((D_MODEL, D_MODEL), BF16),
        scratch_shapes=[pltpu.VMEM((half, D_MODEL), F32)],
        compiler_params=_cp(2),
    )(ycat, dz)


def _mm_dwin(name, hb, dproj):
    lp = hb.shape[0]
    tk = _row_tile(lp, 11)
    nk = lp // tk

    def body(a_ref, b_ref, o_ref, acc_ref):
        k = pl.program_id(1)

        @pl.when(k == 0)
        def _():
            acc_ref[...] = jnp.zeros(acc_ref.shape, F32)

        acc_ref[...] += _dot(a_ref[...], b_ref[...], TN)

        @pl.when(k == nk - 1)
        def _():
            o_ref[0] = acc_ref[:, 0:W_IN_SHARD].astype(BF16)
            o_ref[1] = acc_ref[:, W_IN_SHARD:2 * W_IN_SHARD].astype(BF16)

    return pl.pallas_call(
        body, name=name, grid=(4, nk),
        in_specs=[pl.BlockSpec((tk, D_MODEL), lambda j, k: (k, 0)),
                  pl.BlockSpec((tk, 2 * W_IN_SHARD), lambda j, k: (k, j))],
        out_specs=pl.BlockSpec((2, D_MODEL, W_IN_SHARD), lambda j, k: (j, 0, 0)),
        out_shape=jax.ShapeDtypeStruct((N_DEV, D_MODEL, W_IN_SHARD), BF16),
        scratch_shapes=[pltpu.VMEM((D_MODEL, 2 * W_IN_SHARD), F32)],
        compiler_params=_cp(2),
    )(hb, dproj)


def _mm_dh(name, dproj, wg_in, dz, dep):
    lp = dproj.shape[0]
    tm = lp // 6

    def body(a_ref, w_ref, dz_ref, dep_ref, o_ref, acc_ref):
        del dep_ref
        k = pl.program_id(1)

        @pl.when(k == 0)
        def _():
            acc_ref[...] = jnp.zeros(acc_ref.shape, F32)

        w = jnp.concatenate([w_ref[0], w_ref[1]], axis=1)
        acc_ref[...] += _dot(a_ref[...], w, NT)

        @pl.when(k == N_DEV // 2 - 1)
        def _():
            o_ref[...] = acc_ref[...] + ALPHA * dz_ref[...]

    return pl.pallas_call(
        body, name=name, grid=(6, N_DEV // 2),
        in_specs=[pl.BlockSpec((tm, 2 * W_IN_SHARD), lambda i, k: (i, k)),
                  pl.BlockSpec((2, D_MODEL, W_IN_SHARD), lambda i, k: (k, 0, 0)),
                  pl.BlockSpec((tm, D_MODEL), lambda i, k: (i, 0)),
                  pl.BlockSpec(memory_space=pl.ANY)],
        out_specs=pl.BlockSpec((tm, D_MODEL), lambda i, k: (i, 0)),
        out_shape=jax.ShapeDtypeStruct((lp, D_MODEL), F32),
        scratch_shapes=[pltpu.VMEM((tm, D_MODEL), F32)],
        compiler_params=_cp(2),
    )(dproj, wg_in, dz, dep)


SUB = 128


def _shift_plan(cat, n_shift, base):
    rolled = [cat] + [pltpu.roll(cat, b, axis=0) for b in range(1, 8)]
    return [(rolled[s % 8], base - 8 * (s // 8)) for s in range(n_shift)]


def _tap_sum(w_ref, plan, rows, init=None):
    blocks = []
    for r0 in range(0, rows, SUB):
        row = []
        for c0 in range(0, CONV_W, SUB):
            acc = (jnp.zeros((SUB, SUB), F32) if init is None
                   else jnp.broadcast_to(init[:, c0:c0 + SUB], (SUB, SUB)))
            for k, (arr, off) in enumerate(plan):
                acc = acc + w_ref[k:k + 1, c0:c0 + SUB] * arr[off + r0:off + r0 + SUB, c0:c0 + SUB]
            row.append(acc)
        blocks.append(jnp.concatenate(row, axis=1))
    return jnp.concatenate(blocks, axis=0)


def _tap_grads(dw_ref, dy, plan, rows):
    for c0 in range(0, CONV_W, SUB):
        dys = [dy[r0:r0 + SUB, c0:c0 + SUB] for r0 in range(0, rows, SUB)]
        for k, (arr, off) in enumerate(plan):
            part = None
            for ri, r0 in enumerate(range(0, rows, SUB)):
                prod = dys[ri] * arr[off + r0:off + r0 + SUB, c0:c0 + SUB]
                for i in range(SUB // 8):
                    piece = prod[8 * i:8 * i + 8, :]
                    part = piece if part is None else part + piece
            dw_ref[k:k + 1, c0:c0 + SUB] += jnp.sum(part, axis=0, keepdims=True)


CONV_HALO = 32


def _conv_chain(j, tb, cv_ref, cg_ref, cvp_ref, cgp_ref, wdw_ref, vec_ref, wpw_ref, c1_ref=None):
    cv = cv_ref[...]
    sg = _sig(cg_ref[...])
    c0 = cv * sg
    c0p = jnp.where(j > 0, cvp_ref[...] * _sig(cgp_ref[...]), 0.0)
    cat = jnp.concatenate([c0p, c0], axis=0)
    shifts = _shift_plan(cat, CONV_K, CONV_HALO)
    taps = [shifts[CONV_K - 1 - k] for k in range(CONV_K)]
    if c1_ref is None:
        c1 = _tap_sum(wdw_ref, taps, tb, init=vec_ref[0:1, :])
    else:
        c1 = c1_ref[...]
    xh, rstd = _ln_core(c1)
    c2 = xh * vec_ref[1:2, :] + vec_ref[2:3, :]
    s2 = _sig(c2)
    c3 = c2 * s2
    c4 = _dot(c3.astype(BF16), wpw_ref[...], NN) + vec_ref[3:4, :]
    return dict(cv=cv, sg=sg, taps=taps, c1=c1, xh=xh, rstd=rstd, c2=c2, s2=s2, c3=c3, c4=c4)


def _conv_in_specs(jmap, tb):
    def cur(col):
        return pl.BlockSpec((tb, 512), lambda n: (jmap(n), col))

    def prev(col):
        return pl.BlockSpec((CONV_HALO, 512),
                            lambda n: (jnp.maximum(jmap(n) * (tb // CONV_HALO) - 1, 0), col))

    return [cur(COL_CV), cur(COL_CG), prev(COL_CV), prev(COL_CG), cur(COL_CGATE)]


def _conv_param_specs():
    return [pl.BlockSpec((32, CONV_W), lambda n: (0, 0)),
            pl.BlockSpec((8, CONV_W), lambda n: (0, 0)),
            pl.BlockSpec((CONV_W, CONV_W), lambda n: (0, 0))]


def _conv_fwd(name, proj, wdw, vec, wpw):
    lp = proj.shape[0]
    tb = _row_tile(lp, 3)
    nb = lp // tb

    def body(cv_ref, cg_ref, cvp_ref, cgp_ref, gate_ref, wdw_ref, vec_ref, wpw_ref, o_ref, c1_ref):
        j = pl.program_id(0)
        c = _conv_chain(j, tb, cv_ref, cg_ref, cvp_ref, cgp_ref, wdw_ref, vec_ref, wpw_ref)
        gate = gate_ref[...]
        o_ref[...] = (c["c4"] * (gate * _sig(gate))).astype(BF16)
        c1_ref[...] = c["c1"]

    return pl.pallas_call(
        body, name=name, grid=(nb,),
        in_specs=_conv_in_specs(lambda n: n, tb) + _conv_param_specs(),
        out_specs=[pl.BlockSpec((tb, 512), lambda n: (n, YC_CONV)),
                   pl.BlockSpec((tb, CONV_W), lambda n: (n, 0))],
        out_shape=[jax.ShapeDtypeStruct((lp, D_MODEL), BF16),
                   jax.ShapeDtypeStruct((lp, CONV_W), F32)],
        compiler_params=_cp(1),
    )(proj, proj, proj, proj, proj, wdw, vec, wpw)


def _conv_bwd(name, proj, dycat, c1, wdw, vec, wpw):
    lp = proj.shape[0]
    tb = _row_tile(lp, 3)
    nb = lp // tb
    halo = CONV_HALO

    def body(cv_ref, cg_ref, cvp_ref, cgp_ref, gate_ref, dy_ref, c1_ref, wdw_ref, vec_ref, wpw_ref,
             dp_ref, dwdw_ref, dvec_ref, dwpw_ref, carry_ref):
        n = pl.program_id(0)
        j = nb - 1 - n

        @pl.when(n == 0)
        def _():
            carry_ref[...] = jnp.zeros(carry_ref.shape, F32)
            dwdw_ref[...] = jnp.zeros(dwdw_ref.shape, F32)
            dvec_ref[...] = jnp.zeros(dvec_ref.shape, F32)
            dwpw_ref[...] = jnp.zeros(dwpw_ref.shape, F32)

        c = _conv_chain(j, tb, cv_ref, cg_ref, cvp_ref, cgp_ref, wdw_ref, vec_ref, wpw_ref, c1_ref)
        dy = dy_ref[...].astype(F32)
        gate = gate_ref[...]
        sgate = _sig(gate)
        dc4 = dy * (gate * sgate)
        dgate = dy * c["c4"] * _dsilu(gate, sgate)
        dc4b = dc4.astype(BF16)
        dvec_ref[3:4, :] += _colsum(dc4)
        dwpw_ref[...] += _dot(c["c3"].astype(BF16), dc4b, TN)
        dc3 = _dot(dc4b, wpw_ref[...], NT)
        dc2 = dc3 * _dsilu(c["c2"], c["s2"])
        dvec_ref[1:2, :] += _colsum(dc2 * c["xh"])
        dvec_ref[2:3, :] += _colsum(dc2)
        dc1 = _ln_bwd_core(dc2, c["xh"], c["rstd"], vec_ref[1:2, :])
        dvec_ref[0:1, :] += _colsum(dc1)
        _tap_grads(dwdw_ref, dc1, c["taps"], tb)
        dcat = jnp.concatenate([dc1, carry_ref[...]], axis=0)
        total = tb + halo
        up = [dcat] + [pltpu.roll(dcat, total - b, axis=0) for b in range(1, 8)]
        ahead = [(up[(CONV_K - 1 - k) % 8], 8 * ((CONV_K - 1 - k) // 8)) for k in range(CONV_K)]
        dc0 = _tap_sum(wdw_ref, ahead, tb)
        carry_ref[...] = dc1[0:halo, :]
        sg = c["sg"]
        dcv = dc0 * sg
        dcg = dc0 * c["cv"] * sg * (1.0 - sg)
        dp_ref[:, 0:512] = dcv.astype(BF16)
        dp_ref[:, 512:1024] = dcg.astype(BF16)
        dp_ref[:, 1024:1536] = dgate.astype(BF16)

    jmap = lambda n: nb - 1 - n
    return pl.pallas_call(
        body, name=name, grid=(nb,),
        in_specs=(_conv_in_specs(jmap, tb)
                  + [pl.BlockSpec((tb, 512), lambda n: (jmap(n), YC_CONV)),
                     pl.BlockSpec((tb, CONV_W), lambda n: (jmap(n), 0))]
                  + _conv_param_specs()),
        out_specs=[pl.BlockSpec((tb, 1536), lambda n: (jmap(n), 0)),
                   pl.BlockSpec((32, CONV_W), lambda n: (0, 0)),
                   pl.BlockSpec((8, CONV_W), lambda n: (0, 0)),
                   pl.BlockSpec((CONV_W, CONV_W), lambda n: (0, 0))],
        out_shape=[jax.ShapeDtypeStruct((lp, IN_TOTAL), BF16),
                   jax.ShapeDtypeStruct((32, CONV_W), F32),
                   jax.ShapeDtypeStruct((8, CONV_W), F32),
                   jax.ShapeDtypeStruct((CONV_W, CONV_W), F32)],
        scratch_shapes=[pltpu.VMEM((halo, CONV_W), F32)],
        compiler_params=_cp(1),
    )(proj, proj, proj, proj, proj, dycat, c1, wdw, vec, wpw)


def _rope_tables(lp):
    half = ROT_DIM // 2
    inv_freq = ROPE_THETA ** (-jnp.arange(half, dtype=F32) / half)
    pos = (jnp.arange(lp, dtype=jnp.int32) - PAD0).astype(F32)
    ang = pos[:, None] * inv_freq[None, :]
    cos, sin = jnp.cos(ang), jnp.sin(ang)
    ones = jnp.ones((lp, HEAD_DIM - ROT_DIM), F32)
    zeros = jnp.zeros((lp, HEAD_DIM - ROT_DIM), F32)
    zh = jnp.zeros((lp, half), F32)
    c = jnp.concatenate([cos, cos, ones], axis=1)
    sa = jnp.concatenate([-sin, zh, zeros], axis=1)
    sb = jnp.concatenate([zh, sin, zeros], axis=1)
    tile = lambda t: jnp.tile(t, (1, KV_W // HEAD_DIM))
    return tile(c), tile(sa), tile(sb)


def _rot(x, c, sa, sb):
    w = x.shape[1]
    return x * c + pltpu.roll(x, w - 8, axis=1) * sa + pltpu.roll(x, 8, axis=1) * sb


def _rot_t(dy, c, sa, sb):
    w = dy.shape[1]
    return dy * c + pltpu.roll(dy * sa, 8, axis=1) + pltpu.roll(dy * sb, w - 8, axis=1)


def _rope_fwd(name, proj, tabs):
    lp = proj.shape[0]
    tr = _row_tile(lp, 11)

    def body(q0_ref, q1_ref, k_ref, c_ref, sa_ref, sb_ref, qr_ref, kr_ref):
        c, sa, sb = c_ref[...], sa_ref[...], sb_ref[...]
        c2 = jnp.concatenate([c, c], axis=1)
        sa2 = jnp.concatenate([sa, sa], axis=1)
        sb2 = jnp.concatenate([sb, sb], axis=1)
        qr_ref[:, 0:512] = (_rot(q0_ref[...], c2, sa2, sb2) * ATT_SCALE).astype(BF16)
        qr_ref[:, 512:1024] = (_rot(q1_ref[...], c2, sa2, sb2) * ATT_SCALE).astype(BF16)
        kr_ref[...] = _rot(k_ref[...], c, sa, sb).astype(BF16)

    tab = pl.BlockSpec((tr, KV_W), lambda i: (i, 0))
    return pl.pallas_call(
        body, name=name, grid=(lp // tr,),
        in_specs=[pl.BlockSpec((tr, 512), lambda i: (i, COL_Q0)),
                  pl.BlockSpec((tr, 512), lambda i: (i, COL_Q0 + 1)),
                  pl.BlockSpec((tr, KV_W), lambda i: (i, COL_K256)),
                  tab, tab, tab],
        out_specs=[pl.BlockSpec((tr, ATT_W), lambda i: (i, 0)),
                   pl.BlockSpec((tr, KV_W), lambda i: (i, 0))],
        out_shape=[jax.ShapeDtypeStruct((lp, ATT_W), BF16),
                   jax.ShapeDtypeStruct((lp, KV_W), BF16)],
        compiler_params=_cp(1),
    )(proj, proj, proj, *tabs)


def _attn_mask(j):
    qi = lax.broadcasted_iota(jnp.int32, (GROUP * TB, 3 * TB), 0) & (TB - 1)
    cc = lax.broadcasted_iota(jnp.int32, (GROUP * TB, 3 * TB), 1)
    jj = cc & (TB - 1)
    is_meta = jj >= PAD0
    p0 = (cc < TB) & is_meta & (j >= 1)
    p1 = (cc >= TB) & (cc < 2 * TB) & (jj > qi) & (j >= 2)
    p2 = (cc >= 2 * TB) & (jj <= qi) & ((j >= 1) | is_meta)
    return p0 | p1 | p2


def _lane_group(rows):
    return lax.broadcasted_iota(jnp.int32, (rows, KV_W), 1) // HEAD_DIM


def _stack_heads(x, kv, lgq):
    parts = []
    for g in range(GROUP):
        sh = ((kv - g) % GROUP) * HEAD_DIM
        moved = x if sh == 0 else pltpu.roll(x, sh, axis=1)
        parts.append(jnp.where(lgq == kv, moved, 0.0))
    return jnp.concatenate(parts, axis=0).astype(BF16)


def _unstack_heads(r, kv):
    out = None
    for g in range(GROUP):
        blk = r[g * TB:(g + 1) * TB, :]
        sh = ((g - kv) % GROUP) * HEAD_DIM
        blk = blk if sh == 0 else pltpu.roll(blk, sh, axis=1)
        out = blk if out is None else out + blk
    return out


def _sink_column(sinks, kv):
    lane = lax.broadcasted_iota(jnp.int32, (1, 128), 1)
    cols = []
    for g in range(GROUP):
        sg = jnp.sum(jnp.where(lane == kv * GROUP + g, sinks, 0.0), axis=1, keepdims=True)
        cols.append(jnp.broadcast_to(sg, (TB, 1)))
    return jnp.concatenate(cols, axis=0)


def _attn_kv(kall, vall, lg, kv):
    km = jnp.where(lg == kv, kall, 0.0).astype(BF16)
    vm = jnp.where(lg == kv, vall, 0.0).astype(BF16)
    ones = jnp.where(lg == kv, 1.0, 0.0).astype(BF16)
    return km, vm, ones


def _attn_weights(qst, km, vm, ones, sinkcol, valid, lg4, kv):
    s = jnp.where(valid, _dot(qst, km, NT), NEG_INF)
    m = jnp.maximum(jnp.max(s, axis=-1, keepdims=True), sinkcol)
    eb = jnp.exp(s - m).astype(BF16)
    es = jnp.exp(sinkcol - m)
    r = _dot(eb, vm, NN)
    inv = 1.0 / (_dot(eb, ones, NN) + es)
    out = jnp.where(lg4 == kv, r * inv, 0.0)
    return eb, es, inv, out


def _attn_specs(jmap):
    blk = lambda col: pl.BlockSpec((TB, KV_W), lambda n: (jmap(n), col))
    prv = lambda col: pl.BlockSpec((TB, KV_W), lambda n: (jnp.maximum(jmap(n) - 1, 0), col))
    met = lambda col: pl.BlockSpec((TB, KV_W), lambda n: (0, col))
    return dict(
        qr=pl.BlockSpec((TB, ATT_W), lambda n: (jmap(n), 0)),
        k=[met(0), prv(0), blk(0)],
        v=[met(COL_V256), prv(COL_V256), blk(COL_V256)],
        gate=pl.BlockSpec((TB, ATT_W), lambda n: (jmap(n), COL_AGATE1024)),
        sinks=pl.BlockSpec((8, 128), lambda n: (0, 0)),
    )


def _attn_fwd(name, qr, kr, proj, sinks_row, ycat):
    lp = proj.shape[0]
    nb = lp // TB
    sp = _attn_specs(lambda n: n)

    def body(qr_ref, km_ref, kp_ref, kc_ref, vm_ref, vp_ref, vc_ref, gate_ref, sink_ref, yin_ref, o_ref):
        del yin_ref
        j = pl.program_id(0)
        valid = _attn_mask(j)
        kall = jnp.concatenate([km_ref[...], kp_ref[...], kc_ref[...]], axis=0).astype(F32)
        vall = jnp.concatenate([vm_ref[...], vp_ref[...], vc_ref[...]], axis=0)
        lg = _lane_group(3 * TB)
        lgq = _lane_group(TB)
        lg4 = _lane_group(GROUP * TB)
        sinks = sink_ref[0:1, :]
        for kv in range(N_KV):
            cols = slice(kv * KV_W, (kv + 1) * KV_W)
            km, vm, ones = _attn_kv(kall, vall, lg, kv)
            qst = _stack_heads(qr_ref[:, cols].astype(F32), kv, lgq)
            _, _, _, out = _attn_weights(qst, km, vm, ones, _sink_column(sinks, kv), valid, lg4, kv)
            att = _unstack_heads(out, kv)
            gate = gate_ref[:, cols]
            o_ref[:, cols] = (att * (gate * _sig(gate))).astype(BF16)

    return pl.pallas_call(
        body, name=name, grid=(nb,),
        in_specs=[sp["qr"]] + sp["k"] + sp["v"] + [sp["gate"], sp["sinks"],
                                                   pl.BlockSpec(memory_space=pl.ANY)],
        out_specs=pl.BlockSpec((TB, ATT_W), lambda n: (n, 0)),
        out_shape=jax.ShapeDtypeStruct((lp, D_MODEL), BF16),
        input_output_aliases={9: 0},
        compiler_params=_cp(1),
    )(qr, kr, kr, kr, proj, proj, proj, proj, sinks_row, ycat)


def _attn_bwd(name, qr, kr, proj, sinks_row, dycat, dep):
    lp = proj.shape[0]
    nb = lp // TB
    sp = _attn_specs(lambda n: n)

    def body(qr_ref, km_ref, kp_ref, kc_ref, vm_ref, vp_ref, vc_ref, gate_ref, sink_ref, dy_ref, dep_ref,
             dq_ref, dgate_ref, dk_ref, dv_ref, dsink_ref):
        del dep_ref
        j = pl.program_id(0)

        @pl.when(j == 0)
        def _():
            dk_ref[...] = jnp.zeros(dk_ref.shape, F32)
            dv_ref[...] = jnp.zeros(dv_ref.shape, F32)
            dsink_ref[...] = jnp.zeros(dsink_ref.shape, F32)

        valid = _attn_mask(j)
        kall = jnp.concatenate([km_ref[...], kp_ref[...], kc_ref[...]], axis=0).astype(F32)
        vall = jnp.concatenate([vm_ref[...], vp_ref[...], vc_ref[...]], axis=0)
        lg = _lane_group(3 * TB)
        lgq = _lane_group(TB)
        lg4 = _lane_group(GROUP * TB)
        sinks = sink_ref[0:1, :]
        lane = lax.broadcasted_iota(jnp.int32, (1, 128), 1)
        dkall = jnp.zeros((3 * TB, KV_W), F32)
        dvall = jnp.zeros((3 * TB, KV_W), F32)
        dsink = jnp.zeros((1, 128), F32)
        for kv in range(N_KV):
            cols = slice(kv * KV_W, (kv + 1) * KV_W)
            km, vm, ones = _attn_kv(kall, vall, lg, kv)
            qst = _stack_heads(qr_ref[:, cols].astype(F32), kv, lgq)
            gate = gate_ref[:, cols]
            sgate = _sig(gate)
            dy = dy_ref[:, cols].astype(F32)
            dout = dy * (gate * sgate)
            eb, es, inv, out = _attn_weights(qst, km, vm, ones, _sink_column(sinks, kv), valid, lg4, kv)
            att = _unstack_heads(out, kv)
            dgate_ref[:, cols] = (dy * att * _dsilu(gate, sgate)).astype(BF16)
            dsc = dout * _unstack_heads(jnp.where(lg4 == kv, inv, 0.0), kv)
            dost = _stack_heads(dsc, kv, lgq)
            dd = dsc * att
            dcol = jnp.concatenate(
                [jnp.sum(jnp.where(lgq == g, dd, 0.0), axis=1, keepdims=True) for g in range(GROUP)], axis=0)
            dp = _dot(dost, vm, NT)
            ds = (eb.astype(F32) * (dp - dcol)).astype(BF16)
            pd = es * dcol
            for g in range(GROUP):
                tot = jnp.sum(pd[g * TB:(g + 1) * TB, :], axis=0, keepdims=True)
                dsink = dsink - jnp.where(lane == kv * GROUP + g, tot, 0.0)
            dq_ref[:, cols] = _unstack_heads(_dot(ds, km, NN), kv)
            dkall = dkall + _dot(ds, qst, TN)
            dvall = dvall + _dot(eb, dost, TN)
        dsink_ref[0:1, :] += dsink
        prev = pl.multiple_of(jnp.maximum(j - 1, 0) * TB, TB)
        cur = pl.multiple_of(j * TB, TB)
        dk_ref[0:TB, :] += dkall[0:TB]
        dv_ref[0:TB, :] += dvall[0:TB]
        dk_ref[pl.ds(prev, TB), :] += dkall[TB:2 * TB]
        dv_ref[pl.ds(prev, TB), :] += dvall[TB:2 * TB]
        dk_ref[pl.ds(cur, TB), :] += dkall[2 * TB:3 * TB]
        dv_ref[pl.ds(cur, TB), :] += dvall[2 * TB:3 * TB]

    return pl.pallas_call(
        body, name=name, grid=(nb,),
        in_specs=[sp["qr"]] + sp["k"] + sp["v"] + [sp["gate"], sp["sinks"],
                                                   pl.BlockSpec((TB, ATT_W), lambda n: (n, 0)),
                                                   pl.BlockSpec(memory_space=pl.ANY)],
        out_specs=[pl.BlockSpec((TB, ATT_W), lambda n: (n, 0)),
                   pl.BlockSpec((TB, ATT_W), lambda n: (n, 0)),
                   pl.BlockSpec((lp, KV_W), lambda n: (0, 0)),
                   pl.BlockSpec((lp, KV_W), lambda n: (0, 0)),
                   pl.BlockSpec((8, 128), lambda n: (0, 0))],
        out_shape=[jax.ShapeDtypeStruct((lp, ATT_W), F32),
                   jax.ShapeDtypeStruct((lp, ATT_W), BF16),
                   jax.ShapeDtypeStruct((lp, KV_W), F32),
                   jax.ShapeDtypeStruct((lp, KV_W), F32),
                   jax.ShapeDtypeStruct((8, 128), F32)],
        compiler_params=_cp(1),
    )(qr, kr, kr, kr, proj, proj, proj, proj, sinks_row, dycat, dep)


def _attn_assemble(name, dq, dgate, dk, dv, tabs, dproj):
    lp = dq.shape[0]
    tr = _row_tile(lp, 11)

    def body(dq_ref, dg_ref, dk_ref, dv_ref, c_ref, sa_ref, sb_ref, din_ref, o_ref):
        del din_ref
        cidx = pl.program_id(1)
        c, sa, sb = c_ref[...], sa_ref[...], sb_ref[...]

        @pl.when(cidx < 2)
        def _():
            c2 = jnp.concatenate([c, c], axis=1)
            sa2 = jnp.concatenate([sa, sa], axis=1)
            sb2 = jnp.concatenate([sb, sb], axis=1)
            o_ref[...] = (_rot_t(dq_ref[...], c2, sa2, sb2) * ATT_SCALE).astype(BF16)

        @pl.when(cidx == 2)
        def _():
            o_ref[:, 0:KV_W] = _rot_t(dk_ref[...], c, sa, sb).astype(BF16)
            o_ref[:, KV_W:2 * KV_W] = dv_ref[...].astype(BF16)

        @pl.when(cidx > 2)
        def _():
            o_ref[...] = dg_ref[...]

    tab = pl.BlockSpec((tr, KV_W), lambda n, c: (n, 0))
    return pl.pallas_call(
        body, name=name, grid=(lp // tr, 5),
        in_specs=[pl.BlockSpec((tr, 512), lambda n, c: (n, jnp.minimum(c, 1))),
                  pl.BlockSpec((tr, 512), lambda n, c: (n, jnp.clip(c - 3, 0, 1))),
                  tab, tab,
                  tab, tab, tab,
                  pl.BlockSpec(memory_space=pl.ANY)],
        out_specs=pl.BlockSpec((tr, 512), lambda n, c: (n, COL_Q0 + c)),
        out_shape=jax.ShapeDtypeStruct((lp, IN_TOTAL), BF16),
        input_output_aliases={7: 0},
        compiler_params=_cp(2),
    )(dq, dgate, dk, dv, *tabs, dproj)


def _softplus_neg(lam):
    t = jnp.exp(-jnp.abs(lam))
    u = 1.0 + t
    den = jnp.where(u == 1.0, 1.0, u - 1.0)
    l1p = jnp.where(u == 1.0, t, jnp.log(u) * (t / den))
    return jnp.maximum(-lam, 0.0) + l1p


def _lru_chain(j, tb, rx_ref, rxp_ref, wl_ref, vec_ref, wa_ref, wx_ref):
    rx = rx_ref[...]
    rxp = jnp.where(j > 0, rxp_ref[...], 0.0)
    cat = jnp.concatenate([rxp, rx], axis=0)
    views = [cat[8:8 + tb, :]] + [pltpu.roll(cat, s, axis=0)[8:8 + tb, :] for s in range(1, LRU_CONV_K)]
    x1 = jnp.broadcast_to(vec_ref[0:1, :], (tb, LRU_W))
    for k in range(LRU_CONV_K):
        x1 = x1 + wl_ref[k:k + 1, :] * views[LRU_CONV_K - 1 - k]
    x1b = x1.astype(BF16)
    r = _sig(_dot(x1b, wa_ref[...], NN) + vec_ref[1:2, :])
    ig = _sig(_dot(x1b, wx_ref[...], NN) + vec_ref[2:3, :])
    sp = _softplus_neg(vec_ref[3:4, :])
    log_a = -LRU_C * r * sp
    rows = _row_ids((tb, LRU_W), j * tb)
    live = rows >= PAD0
    a = jnp.where(live, jnp.exp(log_a), 0.0)
    y2 = 2.0 * log_a
    em = -jnp.tanh(0.5 * y2) * (jnp.exp(y2) + 1.0)
    mult = jnp.sqrt(em)
    return dict(views=views, x1=x1, x1b=x1b, r=r, ig=ig, sp=sp, a=a, mult=mult, live=live, a_raw=jnp.exp(log_a))


def _lru_specs(jmap, tb):
    return [pl.BlockSpec((tb, 512), lambda n: (jmap(n), COL_RX)),
            pl.BlockSpec((8, 512), lambda n: (jnp.maximum(jmap(n) * (tb // 8) - 1, 0), COL_RX)),
            pl.BlockSpec((tb, 512), lambda n: (jmap(n), COL_RGATE))]


def _lru_param_specs():
    return [pl.BlockSpec((8, LRU_W), lambda n: (0, 0)),
            pl.BlockSpec((8, LRU_W), lambda n: (0, 0)),
            pl.BlockSpec((LRU_W, LRU_W), lambda n: (0, 0)),
            pl.BlockSpec((LRU_W, LRU_W), lambda n: (0, 0))]


def _lru_fwd(name, proj, wl, vec, wa, wx, ycat):
    lp = proj.shape[0]
    tb = _row_tile(lp, 3)
    nb = lp // tb

    def body(rx_ref, rxp_ref, gate_ref, wl_ref, vec_ref, wa_ref, wx_ref, yin_ref, o_ref, h_ref, carry_ref):
        del yin_ref
        j = pl.program_id(0)

        @pl.when(j == 0)
        def _():
            carry_ref[...] = jnp.zeros(carry_ref.shape, F32)

        c = _lru_chain(j, tb, rx_ref, rxp_ref, wl_ref, vec_ref, wa_ref, wx_ref)
        a = c["a"]
        u = jnp.where(c["live"], c["mult"] * (c["ig"] * c["x1"]), 0.0)
        rows = lax.broadcasted_iota(jnp.int32, (tb, LRU_W), 0)
        d = 1
        while d < tb:
            ap = jnp.where(rows >= d, pltpu.roll(a, d, axis=0), 1.0)
            up = jnp.where(rows >= d, pltpu.roll(u, d, axis=0), 0.0)
            u = a * up + u
            a = a * ap
            d *= 2
        h = u + a * carry_ref[0:1, :]
        carry_ref[...] = h[tb - 8:tb, :]
        carry_ref[0:1, :] = h[tb - 1:tb, :]
        h_ref[...] = h
        gate = gate_ref[...]
        o_ref[...] = (h * (gate * _sig(gate))).astype(BF16)

    return pl.pallas_call(
        body, name=name, grid=(nb,),
        in_specs=_lru_specs(lambda n: n, tb) + _lru_param_specs() + [pl.BlockSpec(memory_space=pl.ANY)],
        out_specs=[pl.BlockSpec((tb, 512), lambda n: (n, YC_LRU)),
                   pl.BlockSpec((tb, LRU_W), lambda n: (n, 0))],
        out_shape=[jax.ShapeDtypeStruct((lp, D_MODEL), BF16),
                   jax.ShapeDtypeStruct((lp, LRU_W), F32)],
        input_output_aliases={7: 0},
        scratch_shapes=[pltpu.VMEM((8, LRU_W), F32)],
        compiler_params=_cp(1),
    )(proj, proj, proj, wl, vec, wa, wx, ycat)


def _lru_bwd(name, proj, dycat, hstate, wl, vec, wa, wx, dproj):
    lp = proj.shape[0]
    tb = _row_tile(lp, 3)
    nb = lp // tb

    def body(rx_ref, rxp_ref, gate_ref, dy_ref, h_ref, hp_ref, wl_ref, vec_ref, wa_ref, wx_ref, din_ref,
             dp_ref, dwl_ref, dvec_ref, dwa_ref, dwx_ref, dhc_ref, anx_ref, dxc_ref):
        del din_ref
        n = pl.program_id(0)
        j = nb - 1 - n

        @pl.when(n == 0)
        def _():
            dhc_ref[...] = jnp.zeros(dhc_ref.shape, F32)
            anx_ref[...] = jnp.zeros(anx_ref.shape, F32)
            dxc_ref[...] = jnp.zeros(dxc_ref.shape, F32)
            dwl_ref[...] = jnp.zeros(dwl_ref.shape, F32)
            dvec_ref[...] = jnp.zeros(dvec_ref.shape, F32)
            dwa_ref[...] = jnp.zeros(dwa_ref.shape, F32)
            dwx_ref[...] = jnp.zeros(dwx_ref.shape, F32)

        c = _lru_chain(j, tb, rx_ref, rxp_ref, wl_ref, vec_ref, wa_ref, wx_ref)
        a, mult, r, ig, x1, live = c["a"], c["mult"], c["r"], c["ig"], c["x1"], c["live"]
        h = h_ref[...]
        gate = gate_ref[...]
        sgate = _sig(gate)
        dy = dy_ref[...].astype(F32)
        gsum = dy * (gate * sgate)
        dgate = dy * h * _dsilu(gate, sgate)
        rows = lax.broadcasted_iota(jnp.int32, (tb, LRU_W), 0)
        bb = jnp.where(rows == tb - 1, anx_ref[0:1, :], pltpu.roll(a, tb - 1, axis=0))
        gg = gsum
        d = 1
        while d < tb:
            keep = rows < tb - d
            bn = jnp.where(keep, pltpu.roll(bb, tb - d, axis=0), 1.0)
            gn = jnp.where(keep, pltpu.roll(gg, tb - d, axis=0), 0.0)
            gg = gg + bb * gn
            bb = bb * bn
            d *= 2
        dh = gg + bb * dhc_ref[0:1, :]
        dhc_ref[...] = dh[0:8, :]
        anx_ref[...] = a[0:8, :]
        hprev = jnp.where(rows == 0, jnp.where(j > 0, hp_ref[7:8, :], 0.0), pltpu.roll(h, 1, axis=0))
        du = jnp.where(live, dh, 0.0)
        da = jnp.where(live, dh * hprev, 0.0)
        ar = c["a_raw"]
        dmult = du * (ig * x1)
        di = du * mult * x1
        dx1 = du * mult * ig
        dloga = da * ar - dmult * ar * ar / mult
        dr = dloga * (-LRU_C * c["sp"])
        dvec_ref[3:4, :] += _colsum(dloga * (-LRU_C * r))
        dza = dr * r * (1.0 - r)
        dzx = di * ig * (1.0 - ig)
        dzab, dzxb = dza.astype(BF16), dzx.astype(BF16)
        dvec_ref[1:2, :] += _colsum(dza)
        dvec_ref[2:3, :] += _colsum(dzx)
        dwa_ref[...] += _dot(c["x1b"], dzab, TN)
        dwx_ref[...] += _dot(c["x1b"], dzxb, TN)
        dx1 = dx1 + _dot(dzab, wa_ref[...], NT) + _dot(dzxb, wx_ref[...], NT)
        dvec_ref[0:1, :] += _colsum(dx1)
        for k in range(LRU_CONV_K):
            dwl_ref[k:k + 1, :] += _colsum(dx1 * c["views"][LRU_CONV_K - 1 - k])
        dcat = jnp.concatenate([dx1, dxc_ref[...]], axis=0)
        drx = jnp.zeros((tb, LRU_W), F32)
        for k in range(LRU_CONV_K):
            s = LRU_CONV_K - 1 - k
            view = dcat[0:tb, :] if s == 0 else pltpu.roll(dcat, tb + 8 - s, axis=0)[0:tb, :]
            drx = drx + wl_ref[k:k + 1, :] * view
        dxc_ref[...] = dx1[0:8, :]
        dp_ref[:, 0:512] = drx.astype(BF16)
        dp_ref[:, 512:1024] = dgate.astype(BF16)

        @pl.when(n == nb - 1)
        def _():
            lam = vec_ref[3:4, :]
            dvec_ref[3:4, :] = dvec_ref[3:4, :] * (-_sig(-lam))

    jmap = lambda n: nb - 1 - n
    return pl.pallas_call(
        body, name=name, grid=(nb,),
        in_specs=(_lru_specs(jmap, tb)
                  + [pl.BlockSpec((tb, 512), lambda n: (jmap(n), YC_LRU)),
                     pl.BlockSpec((tb, LRU_W), lambda n: (jmap(n), 0)),
                     pl.BlockSpec((8, LRU_W), lambda n: (jnp.maximum(jmap(n) * (tb // 8) - 1, 0), 0))]
                  + _lru_param_specs() + [pl.BlockSpec(memory_space=pl.ANY)]),
        out_specs=[pl.BlockSpec((tb, 1024), lambda n: (jmap(n), 4)),
                   pl.BlockSpec((8, LRU_W), lambda n: (0, 0)),
                   pl.BlockSpec((8, LRU_W), lambda n: (0, 0)),
                   pl.BlockSpec((LRU_W, LRU_W), lambda n: (0, 0)),
                   pl.BlockSpec((LRU_W, LRU_W), lambda n: (0, 0))],
        out_shape=[jax.ShapeDtypeStruct((lp, IN_TOTAL), BF16),
                   jax.ShapeDtypeStruct((8, LRU_W), F32),
                   jax.ShapeDtypeStruct((8, LRU_W), F32),
                   jax.ShapeDtypeStruct((LRU_W, LRU_W), F32),
                   jax.ShapeDtypeStruct((LRU_W, LRU_W), F32)],
        input_output_aliases={10: 0},
        scratch_shapes=[pltpu.VMEM((8, LRU_W), F32), pltpu.VMEM((8, LRU_W), F32), pltpu.VMEM((8, LRU_W), F32)],
        compiler_params=_cp(1),
    )(proj, proj, proj, dycat, hstate, hstate, wl, vec, wa, wx, dproj)


_HBM = pl.BlockSpec(memory_space=pltpu.HBM)
_SEM = pl.BlockSpec(memory_space=pltpu.SEMAPHORE)
_ANY = pl.BlockSpec(memory_space=pl.ANY)
_EFFECT = pltpu.SideEffectType.DATAFLOW_SIDE_EFFECTING


def _hbm(a):
    return pltpu.with_memory_space_constraint(a, pltpu.HBM)


_ALL_PEERS = tuple(range(1, N_DEV))
_CHIP_PEERS = (1, 2, 4, 6)
_OTHER_CHIPS = (2, 4, 6)


def _spec_peers(mode):
    return {"ici": _CHIP_PEERS, "fwd": _OTHER_CHIPS}.get(mode, _ALL_PEERS)


def _split_descriptors(copies, srcs, lands, send_sems, recv_sems):
    x, y, c = lax.axis_index("x"), lax.axis_index("y"), lax.axis_index("c")
    me = 4 * x + 2 * y + c
    out, sem = [], 0
    for si, mode, li, ll in copies:
        for k in _spec_peers(mode):
            px = 1 - x if k & 4 else x
            py = 1 - y if k & 2 else y
            pc = 1 - c if k & 1 else c
            peer = 4 * px + 2 * py + pc
            if mode == "fwd":
                src = dst = lands[li].at[peer]
                target = (x, y, 1 - c)
            else:
                src = srcs[si].at[peer] if mode is True else srcs[si]
                dst = lands[li].at[me] if ll is None else lands[li].at[me, ll]
                target = (px, py, pc)
            out.append(pltpu.make_async_remote_copy(
                src_ref=src, dst_ref=dst, send_sem=send_sems.at[sem], recv_sem=recv_sems.at[sem],
                device_id=target, device_id_type=pl.DeviceIdType.MESH))
            sem += 1
    return out


def _n_copies(copies):
    return sum(len(_spec_peers(mode)) for _, mode, _, _ in copies)


def _xchg_start(name, groups):
    n_src = [len(g[0]) for g in groups]
    n_land = [len(g[1]) for g in groups]
    srcs = [s for g in groups for s in g[0]]
    lands = [l for g in groups for l in g[1]]
    ns, nl, ng = len(srcs), len(lands), len(groups)

    def body(*refs):
        src_refs, land_refs = refs[:ns], refs[ns:ns + nl]
        sems = refs[ns + nl:ns + nl + 2 * ng]
        token = refs[-1]
        so = lo = 0
        for gi, (_, _, copies) in enumerate(groups):
            for d in _split_descriptors(copies, src_refs[so:so + n_src[gi]], land_refs[lo:lo + n_land[gi]],
                                        sems[2 * gi], sems[2 * gi + 1]):
                d.start()
            so += n_src[gi]
            lo += n_land[gi]
        token[...] = jnp.zeros(token.shape, F32)

    out_shape, out_specs = [], []
    for g in groups:
        n = _n_copies(g[2])
        out_shape += [pltpu.SemaphoreType.DMA((n,)), pltpu.SemaphoreType.DMA((n,))]
        out_specs += [_SEM, _SEM]
    out_shape += [pltpu.HBM(l.shape, l.dtype) for l in lands]
    out_specs += [_HBM] * nl
    out_shape.append(jax.ShapeDtypeStruct((8, 128), F32))
    out_specs.append(pl.BlockSpec(memory_space=pltpu.VMEM))
    outs = pl.pallas_call(
        body, name=name, in_specs=[_HBM] * (ns + nl), out_specs=out_specs, out_shape=out_shape,
        input_output_aliases={ns + i: 2 * ng + i for i in range(nl)},
        compiler_params=pltpu.CompilerParams(has_side_effects=_EFFECT),
    )(*[_hbm(a) for a in srcs + lands])
    res, lo = [], 2 * ng
    for gi in range(ng):
        res.append((outs[2 * gi], outs[2 * gi + 1], list(outs[lo:lo + n_land[gi]])))
        lo += n_land[gi]
    return res, outs[-1]


def _xchg_wait(name, group, started, after):
    srcs, _, copies = group
    send_sems, recv_sems, lands = started
    ns, nl = len(srcs), len(lands)
    after = list(after)

    def body(*refs):
        src_refs, land_refs = refs[:ns], refs[ns:ns + nl]
        send_ref, recv_ref = refs[ns + nl], refs[ns + nl + 1]
        for d in _split_descriptors(copies, src_refs, land_refs, send_ref, recv_ref):
            d.wait_send()
            d.wait_recv()

    outs = pl.pallas_call(
        body, name=name, in_specs=[_HBM] * (ns + nl) + [_SEM, _SEM] + [_ANY] * len(after),
        out_specs=[_HBM] * nl, out_shape=[pltpu.HBM(l.shape, l.dtype) for l in lands],
        input_output_aliases={ns + i: i for i in range(nl)},
        compiler_params=pltpu.CompilerParams(has_side_effects=_EFFECT),
    )(*[_hbm(a) for a in srcs], *lands, send_sems, recv_sems, *after)
    return list(outs)


def _landing(own, me):
    land = lax.empty((N_DEV,) + own.shape, own.dtype)
    return lax.dynamic_update_slice(land, own[None], (me,) + (0,) * own.ndim)


def _adamw(name, w, m, v, recv, row0=0, prev=None):
    cdim = w.shape[1]
    r = recv.shape[1]
    tr = r
    for cand in (512, 256, 128, 64, 32, 16, 8):
        if r % cand == 0 and r > cand:
            tr = cand
            break
    assert row0 % tr == 0
    blk0 = row0 // tr
    n_prev = 0 if prev is None else 4

    def body(w_ref, m_ref, v_ref, r_ref, *rest):
        g_ref, d_ref, mo_ref, vo_ref = rest[n_prev:]
        g = r_ref[0].astype(F32)
        for s in range(1, N_DEV):
            g = g + r_ref[s].astype(F32)
        mn = ADAM_B1 * m_ref[...] + (1.0 - ADAM_B1) * g
        vn = ADAM_B2 * v_ref[...] + (1.0 - ADAM_B2) * (g * g)
        m_hat = mn / (1.0 - ADAM_B1 ** ADAM_STEP)
        v_hat = vn / (1.0 - ADAM_B2 ** ADAM_STEP)
        g_ref[...] = g
        d_ref[...] = -ADAM_LR * (m_hat / (jnp.sqrt(v_hat) + ADAM_EPS) + ADAM_WD * w_ref[...])
        mo_ref[...] = mn
        vo_ref[...] = vn

    blk = pl.BlockSpec((tr, cdim), lambda i: (i + blk0, 0))
    return pl.pallas_call(
        body, name=name, grid=(r // tr,),
        in_specs=[blk, blk, blk, pl.BlockSpec((N_DEV, tr, cdim), lambda i: (0, i, 0))] + [_ANY] * n_prev,
        out_specs=[blk, blk, blk, blk],
        out_shape=[jax.ShapeDtypeStruct(w.shape, F32)] * 4,
        input_output_aliases={4 + i: i for i in range(n_prev)},
        compiler_params=_cp(1),
    )(w, m, v, recv, *(prev or []))


def _pack_rows(arrs, lead=()):
    n = len(lead)
    flat = jnp.concatenate([a.reshape(a.shape[:n] + (-1,)) for a in arrs], axis=-1)
    size = flat.shape[-1]
    padded = -(-size // PACK_QUANTUM) * PACK_QUANTUM
    flat = jnp.pad(flat, [(0, 0)] * n + [(0, padded - size)])
    return flat.reshape(flat.shape[:n] + (padded // 128, 128))


def _unpack_rows(packed, shapes, lead=()):
    n = len(lead)
    flat = packed.reshape(packed.shape[:n] + (-1,))
    out, off = [], 0
    for s in shapes:
        size = int(np.prod(s))
        out.append(flat[..., off:off + size].reshape(packed.shape[:n] + tuple(s)))
        off += size
    return out


def _block_diag(w):
    eye = jnp.eye(LRU_HEADS, dtype=w.dtype)
    return (eye[:, None, :, None] * w[:, :, None, :]).reshape(LRU_W, LRU_W)


def _diag_blocks(dense):
    t = dense.reshape(LRU_HEADS, 64, LRU_HEADS, 64)
    eye = jnp.eye(LRU_HEADS, dtype=dense.dtype)
    return jnp.sum(t * eye[:, None, :, None], axis=2).reshape(LRU_HEADS * 64, 64)


_W512_NAMES = ("conv_dw_b", "conv_ln_g", "conv_ln_b", "conv_pw_b", "lru_conv_b", "lru_ba", "lru_bx", "lru_lambda")
_W512_ROWS = 12


def _pack_small(d):
    sinks = jnp.pad(d["attn_sinks"], ((0, 0), (0, 512 - N_HEADS)))
    t = jnp.stack([d[n] for n in _W512_NAMES] + [sinks], axis=1)
    w512 = jnp.pad(t, ((0, 0), (0, _W512_ROWS - t.shape[1]), (0, 0))).reshape(DEPTH * _W512_ROWS, 512)
    w2048 = jnp.concatenate([d["ln_in_g"][None], d["ln_in_b"][None], d["ln_post_g"], d["ln_post_b"],
                             jnp.zeros((2, D_MODEL), F32)], axis=0)
    w64 = jnp.concatenate([d["lru_wa"].reshape(-1, 64), d["lru_wx"].reshape(-1, 64)], axis=0)
    return [w512, w2048, w64]


def _unpack_small(w512, w2048, w64):
    t = w512.reshape(DEPTH, _W512_ROWS, 512)
    out = {n: t[:, i, :] for i, n in enumerate(_W512_NAMES)}
    out["attn_sinks"] = t[:, len(_W512_NAMES), :N_HEADS]
    out["ln_in_g"], out["ln_in_b"] = w2048[0], w2048[1]
    out["ln_post_g"], out["ln_post_b"] = w2048[2:4], w2048[4:6]
    half = w64.shape[0] // 2
    out["lru_wa"] = w64[:half].reshape(DEPTH, LRU_HEADS, 64, 64)
    out["lru_wx"] = w64[half:].reshape(DEPTH, LRU_HEADS, 64, 64)
    return out


def _cols_to_slots(full):
    lead = full.shape[:-1]
    t = full.reshape(lead + (N_DEV, full.shape[-1] // N_DEV))
    return jnp.moveaxis(t, -2, 0)


def _slots_to_cols(slots):
    t = jnp.moveaxis(slots, 0, -2)
    return t.reshape(t.shape[:-2] + (t.shape[-2] * t.shape[-1],))


def kernel(x, meta_tokens, ln_in_g, ln_in_b, w_in, conv_dw_w, conv_dw_b, conv_ln_g, conv_ln_b, conv_pw_w, conv_pw_b, attn_sinks, lru_conv_w, lru_conv_b, lru_wa, lru_ba, lru_wx, lru_bx, lru_lambda, w_out, ln_post_g, ln_post_b, loss_target, m_meta_tokens, m_ln_in_g, m_ln_in_b, m_w_in, m_conv_dw_w, m_conv_dw_b, m_conv_ln_g, m_conv_ln_b, m_conv_pw_w, m_conv_pw_b, m_attn_sinks, m_lru_conv_w, m_lru_conv_b, m_lru_wa, m_lru_ba, m_lru_wx, m_lru_bx, m_lru_lambda, m_w_out, m_ln_post_g, m_ln_post_b, v_meta_tokens, v_ln_in_g, v_ln_in_b, v_w_in, v_conv_dw_w, v_conv_dw_b, v_conv_ln_g, v_conv_ln_b, v_conv_pw_w, v_conv_pw_b, v_attn_sinks, v_lru_conv_w, v_lru_conv_b, v_lru_wa, v_lru_ba, v_lru_wx, v_lru_bx, v_lru_lambda, v_w_out, v_ln_post_g, v_ln_post_b):
    seq = x.shape[1]
    lp = seq + TB
    row = lambda a: a.reshape(1, -1)
    rep_names = ["ln_in_g", "ln_in_b", "conv_dw_b", "conv_ln_g", "conv_ln_b", "conv_pw_b", "attn_sinks",
                 "lru_conv_b", "lru_wa", "lru_ba", "lru_wx", "lru_bx", "lru_lambda", "ln_post_g", "ln_post_b"]
    shard_small_names = ["conv_dw_w", "lru_conv_w", "meta_tokens"]
    weights = dict(meta_tokens=meta_tokens, ln_in_g=ln_in_g, ln_in_b=ln_in_b, w_in=w_in, conv_dw_w=conv_dw_w,
                   conv_dw_b=conv_dw_b, conv_ln_g=conv_ln_g, conv_ln_b=conv_ln_b, conv_pw_w=conv_pw_w,
                   conv_pw_b=conv_pw_b, attn_sinks=attn_sinks, lru_conv_w=lru_conv_w, lru_conv_b=lru_conv_b,
                   lru_wa=lru_wa, lru_ba=lru_ba, lru_wx=lru_wx, lru_bx=lru_bx, lru_lambda=lru_lambda,
                   w_out=w_out, ln_post_g=ln_post_g, ln_post_b=ln_post_b)
    mom1 = dict(meta_tokens=m_meta_tokens, ln_in_g=m_ln_in_g, ln_in_b=m_ln_in_b, w_in=m_w_in, conv_dw_w=m_conv_dw_w,
                conv_dw_b=m_conv_dw_b, conv_ln_g=m_conv_ln_g, conv_ln_b=m_conv_ln_b, conv_pw_w=m_conv_pw_w,
                conv_pw_b=m_conv_pw_b, attn_sinks=m_attn_sinks, lru_conv_w=m_lru_conv_w, lru_conv_b=m_lru_conv_b,
                lru_wa=m_lru_wa, lru_ba=m_lru_ba, lru_wx=m_lru_wx, lru_bx=m_lru_bx, lru_lambda=m_lru_lambda,
                w_out=m_w_out, ln_post_g=m_ln_post_g, ln_post_b=m_ln_post_b)
    mom2 = dict(meta_tokens=v_meta_tokens, ln_in_g=v_ln_in_g, ln_in_b=v_ln_in_b, w_in=v_w_in, conv_dw_w=v_conv_dw_w,
                conv_dw_b=v_conv_dw_b, conv_ln_g=v_conv_ln_g, conv_ln_b=v_conv_ln_b, conv_pw_w=v_conv_pw_w,
                conv_pw_b=v_conv_pw_b, attn_sinks=v_attn_sinks, lru_conv_w=v_lru_conv_w, lru_conv_b=v_lru_conv_b,
                lru_wa=v_lru_wa, lru_ba=v_lru_ba, lru_wx=v_lru_wx, lru_bx=v_lru_bx, lru_lambda=v_lru_lambda,
                w_out=v_w_out, ln_post_g=v_ln_post_g, ln_post_b=v_ln_post_b)
    shard_wmv = [_pack_rows([d[n] for n in shard_small_names]) for d in (weights, mom1, mom2)]
    rep_wmv = [_pack_small(d) for d in (weights, mom1, mom2)]
    gate_w = [(_block_diag(lru_wa[l]).astype(BF16), _block_diag(lru_wx[l]).astype(BF16)) for l in range(DEPTH)]
    tabs = _rope_tables(lp)
    prepared = (shard_wmv + [a for wmv in rep_wmv for a in wmv]
                + [w for pair in gate_w for w in pair] + list(tabs))

    small_shard_shapes = [conv_dw_w.shape, lru_conv_w.shape, meta_tokens.shape]
    small_shard = _pack_rows([conv_dw_w, lru_conv_w, meta_tokens])
    me = 4 * lax.axis_index("x") + 2 * lax.axis_index("y") + lax.axis_index("c")
    w_in_b = [w_in[l].astype(BF16) for l in range(DEPTH)]
    w_out_b = [w_out[l].astype(BF16) for l in range(DEPTH)]
    pw_b = conv_pw_w.astype(BF16)
    wgroups = [
        ([small_shard], [_landing(small_shard, me)], [(0, False, 0, None)]),
        ([w_in_b[0]], [_landing(w_in_b[0], me)], [(0, "ici", 0, None)]),
        ([pw_b, w_out_b[0]], [_landing(pw_b, me), _landing(w_out_b[0], me)],
         [(0, False, 0, None), (1, False, 1, None)]),
        ([w_in_b[1], w_out_b[1]], [_landing(w_in_b[1], me), _landing(w_out_b[1], me)],
         [(0, False, 0, None), (1, False, 1, None)]),
    ]
    wstarted, wtoken = _xchg_start("weights_start", wgroups)
    wg_small, = _xchg_wait("weights_wait_s", wgroups[0], wstarted[0], [wtoken])
    g_dw, g_lc, g_meta = _unpack_rows(wg_small, small_shard_shapes, lead=(N_DEV,))
    conv_dw_full = _slots_to_cols(g_dw)
    lru_conv_full = _slots_to_cols(g_lc)
    meta_full = _slots_to_cols(g_meta)
    wg_in = [None, None]
    wg_out = [None, None]
    wg_pw = None

    ln_g = [ln_in_g, ln_post_g[0], ln_post_g[1]]
    ln_b = [ln_in_b, ln_post_b[0], ln_post_b[1]]

    def layer_params(l):
        wdw = jnp.pad(conv_dw_full[l], ((0, 1), (0, 0)))
        cvec = jnp.pad(jnp.stack([conv_dw_b[l], conv_ln_g[l], conv_ln_b[l], conv_pw_b[l]]), ((0, 4), (0, 0)))
        wpw = wg_pw[:, l].reshape(CONV_W, CONV_W)
        sinks = jnp.pad(attn_sinks[l].reshape(1, N_HEADS), ((0, 7), (0, 128 - N_HEADS)))
        wl = jnp.pad(lru_conv_full[l], ((0, 4), (0, 0)))
        lvec = jnp.pad(jnp.stack([lru_conv_b[l], lru_ba[l], lru_bx[l], lru_lambda[l]]), ((0, 4), (0, 0)))
        wa, wx = gate_w[l]
        wo = wg_out[l].reshape(D_MODEL, D_MODEL)
        wout = jnp.concatenate([wo[512:1536], wo[0:512], wo[1536:]], axis=0)
        return dict(wdw=wdw, cvec=cvec, wpw=wpw, sinks=sinks, wl=wl, lvec=lvec, wa=wa, wx=wx, wout=wout)

    params = [None] * DEPTH

    z0, hb = _embed(x, meta_full, row(ln_g[0]), row(ln_b[0]))
    z = [z0]
    saved = []
    for l in range(DEPTH):
        if l == 0:
            part, = _xchg_wait("weights_wait_a", wgroups[1], wstarted[1], [hb] + prepared)
            fwd = ([], [part], [(None, "fwd", 0, None)])
            fstarted, ftoken = _xchg_start("weights_fwd_start", [fwd])
            wg_in[0], = _xchg_wait("weights_fwd_wait", fwd, fstarted[0], [ftoken])
        else:
            wg_in[1], wg_out[1] = _xchg_wait("weights_wait_c", wgroups[3], wstarted[3], [hb])
        proj = _mm_proj(f"proj{l}", hb, wg_in[l])
        if l == 0:
            wg_pw, wg_out[0] = _xchg_wait("weights_wait_b", wgroups[2], wstarted[2], [proj])
        p = params[l] = layer_params(l)
        ycat, c1 = _conv_fwd(f"conv_fwd{l}", proj, p["wdw"], p["cvec"], p["wpw"])
        qr, kr = _rope_fwd(f"rope{l}", proj, tabs)
        ycat = _attn_fwd(f"attn_fwd{l}", qr, kr, proj, p["sinks"], ycat)
        ycat, hstate = _lru_fwd(f"lru_fwd{l}", proj, p["wl"], p["lvec"], p["wa"], p["wx"], ycat)
        saved.append(dict(hb=hb, proj=proj, ycat=ycat, qr=qr, kr=kr, hstate=hstate, c1=c1))
        z_next, hb = _mm_out(f"out{l}", ycat, p["wout"], z[l], row(ln_g[l]), row(ln_b[l]),
                             row(ln_g[l + 1]), row(ln_b[l + 1]))
        z.append(z_next)

    dz, st_post1, loss_blk = _loss_head(z[DEPTH], loss_target, row(ln_g[DEPTH]), row(ln_b[DEPTH]))
    loss = lax.psum(loss_blk[0, 0], ("x", "y", "c"))

    ln_stats = {DEPTH: st_post1}
    g_layers = [None] * DEPTH
    dwin_l, dwout_l = [None] * DEPTH, [None] * DEPTH
    grad_x = gmeta = None
    token = wtoken
    ggroups = [None] * DEPTH
    own = lambda a: lax.dynamic_index_in_dim(a, me, 0, keepdims=False)
    for l in reversed(range(DEPTH)):
        p, s = params[l], saved[l]
        dycat = _mm_dycat(f"dycat{l}", dz, p["wout"], token)
        dwout_l[l] = _mm_dwout(f"dwout{l}", s["ycat"], dz)
        dproj, dwdw, dcvec, dwpw = _conv_bwd(f"conv_bwd{l}", s["proj"], dycat, s["c1"], p["wdw"], p["cvec"], p["wpw"])
        dwo = jnp.concatenate([dwout_l[l][1024:1536], dwout_l[l][0:1024], dwout_l[l][1536:]], axis=0)
        dwo = dwo.reshape(N_DEV, D_MODEL // N_DEV, D_MODEL)
        dpw = dwpw.reshape(N_DEV, CONV_W // N_DEV, CONV_W)
        early = ([dwo, dpw], [_landing(own(dwo), me), _landing(own(dpw), me)],
                 [(0, True, 0, None), (1, True, 1, None)])
        started_early, token = _xchg_start(f"grads_start_out{l}", [early])
        dq, dgate, dk, dv, dsink = _attn_bwd(f"attn_bwd{l}", s["qr"], s["kr"], s["proj"], p["sinks"], dycat, token)
        dproj = _attn_assemble(f"attn_asm{l}", dq, dgate, dk, dv, tabs, dproj)
        dproj, dwl, dlvec, dwa, dwx = _lru_bwd(f"lru_bwd{l}", s["proj"], dycat, s["hstate"],
                                                p["wl"], p["lvec"], p["wa"], p["wx"], dproj)
        dwin_l[l] = _mm_dwin(f"dwin{l}", s["hb"], dproj)
        late = ([dwin_l[l]], [_landing(own(dwin_l[l]), me)], [(0, True, 0, None)])
        started_late, token = _xchg_start(f"grads_start_in{l}", [late])
        ggroups[l] = [(late, started_late[0]), (early, started_early[0])]
        dh = _mm_dh(f"dh{l}", dproj, wg_in[l], dz, token)
        if l > 0:
            dz, ln_stats[l] = _ln_bwd(f"ln_bwd{l}", dh, z[l], row(ln_g[l]))
        else:
            grad_x, gmeta, ln_stats[0] = _ln_bwd_input(dh, z[0], row(ln_g[0]))
        g512 = jnp.concatenate([dcvec[0:4], dlvec[0:4], jnp.pad(dsink[0:1], ((0, 0), (0, 512 - 128))),
                                jnp.zeros((_W512_ROWS - 9, 512), F32)], axis=0)
        g_layers[l] = dict(dwdw=dwdw[:CONV_K], dwl=dwl[:LRU_CONV_K], g512=g512,
                           dwa=_diag_blocks(dwa), dwx=_diag_blocks(dwx))

    g512 = jnp.concatenate([g_layers[l]["g512"] for l in range(DEPTH)], axis=0)
    g2048 = jnp.concatenate([ln_stats[0][0:2], ln_stats[1][0:1], ln_stats[2][0:1], ln_stats[1][1:2],
                             ln_stats[2][1:2], jnp.zeros((2, D_MODEL), F32)], axis=0)
    g64 = jnp.concatenate([g_layers[l][k] for k in ("dwa", "dwx") for l in range(DEPTH)], axis=0)
    g_dw_full = jnp.stack([g_layers[l]["dwdw"] for l in range(DEPTH)])
    g_lc_full = jnp.stack([g_layers[l]["dwl"] for l in range(DEPTH)])
    shard_pack = _pack_rows([_cols_to_slots(g_dw_full), _cols_to_slots(g_lc_full), _cols_to_slots(gmeta)],
                            lead=(N_DEV,))
    sgroup = ([shard_pack, g512, g2048, g64],
              [_landing(own(shard_pack), me), _landing(g512, me), _landing(g2048, me), _landing(g64, me)],
              [(0, True, 0, None), (1, False, 1, None), (2, False, 2, None), (3, False, 3, None)])
    sstarted, token = _xchg_start("small_grads_start", [sgroup])

    res = {}

    def flat2(a, cols):
        return a.reshape(-1, cols)

    big = (("w_in", 0, W_IN_SHARD), ("w_out", 1, D_MODEL), ("conv_pw_w", 2, CONV_W))
    prev = {n: None for n, _, _ in big}
    after = token
    for l in reversed(range(DEPTH)):
        recvs = []
        for gi, (grp, started) in enumerate(ggroups[l]):
            recvs += _xchg_wait(f"grads_wait{l}_{gi}", grp, started, [after])
        for name_, gi, cols in big:
            w_ = weights[name_]
            rows = w_.shape[1]
            prev[name_] = _adamw(f"adamw_{name_}{l}", flat2(w_, cols), flat2(mom1[name_], cols),
                                 flat2(mom2[name_], cols), recvs[gi], row0=l * rows, prev=prev[name_])
        after = prev["w_in"][0]
    for name_, _, _ in big:
        res[name_] = [o.reshape(weights[name_].shape) for o in prev[name_]]

    r_small, *r_rep = _xchg_wait("small_grads_wait", sgroup, sstarted[0], [prev[n][0] for n, _, _ in big])
    sshapes = [weights[n].shape for n in shard_small_names]
    outs = _adamw("adamw_small_sharded", *shard_wmv, r_small)
    for k, o in enumerate(outs):
        for n, a in zip(shard_small_names, _unpack_rows(o, sshapes)):
            res.setdefault(n, [None] * 4)[k] = a

    outs = [_adamw(f"adamw_small_w{tag}", rep_wmv[0][ci], rep_wmv[1][ci], rep_wmv[2][ci], r_rep[ci])
            for ci, tag in enumerate(("512", "2048", "64"))]
    for k in range(4):
        for n, a in _unpack_small(outs[0][k], outs[1][k], outs[2][k]).items():
            res.setdefault(n, [None] * 4)[k] = a

    order = ["meta_tokens", "ln_in_g", "ln_in_b", "w_in", "conv_dw_w", "conv_dw_b", "conv_ln_g", "conv_ln_b",
             "conv_pw_w", "conv_pw_b", "attn_sinks", "lru_conv_w", "lru_conv_b", "lru_wa", "lru_ba", "lru_wx",
             "lru_bx", "lru_lambda", "w_out", "ln_post_g", "ln_post_b"]
    return (loss, grad_x,
            *[res[n][0] for n in order], *[res[n][1] for n in order],
            *[res[n][2] for n in order], *[res[n][3] for n in order])
```

```python
import functools
import math

import numpy as np
import jax
import jax.numpy as jnp
from jax import lax
from jax.experimental import pallas as pl
from jax.experimental.pallas import tpu as pltpu

F32 = jnp.float32
BF16 = jnp.bfloat16

D_MODEL = 2048
DEPTH = 2
N_META = 16
TB = 128
PAD0 = TB - N_META
CONV_W = 512
CONV_K = 31
HEAD_DIM = 64
N_HEADS = 16
N_KV = 4
GROUP = 4
ATT_W = 1024
KV_W = 256
ROT_DIM = 16
ROPE_THETA = 500000.0
LRU_W = 512
LRU_HEADS = 8
LRU_CONV_K = 4
LRU_C = 8.0
IN_TOTAL = 5120
N_DEV = 8
W_IN_SHARD = IN_TOTAL // N_DEV
LN_EPS = 1e-5
ALPHA = (2.0 * DEPTH) ** 0.25
NEG_INF = -1e30
ATT_SCALE = HEAD_DIM ** -0.5

ADAM_LR = 0.001
ADAM_B1 = 0.9
ADAM_B2 = 0.999
ADAM_EPS = 1e-08
ADAM_WD = 0.01
ADAM_STEP = 10

VMEM_LIMIT = 56 * 1024 * 1024
PACK_QUANTUM = 256 * 128

COL_CV, COL_CG, COL_CGATE = 0, 1, 2
COL_Q0 = 3
COL_K256 = 10
COL_V256 = 11
COL_AGATE1024 = 3
COL_RX, COL_RGATE = 8, 9
YC_CONV, YC_LRU = 2, 3


def _cp(n_axes, vmem=VMEM_LIMIT):
    return pltpu.CompilerParams(dimension_semantics=("arbitrary",) * n_axes, vmem_limit_bytes=vmem)


def _row_tile(lp, max_blocks):
    nb = lp // TB
    d = max(k for k in range(1, max_blocks + 1) if nb % k == 0)
    return TB * d


def _sig(x):
    return jax.nn.sigmoid(x)


def _dsilu(x, s):
    return s * (1.0 + x * (1.0 - s))


def _ln_core(z):
    mu = jnp.mean(z, axis=-1, keepdims=True)
    zc = z - mu
    var = jnp.mean(zc * zc, axis=-1, keepdims=True)
    rstd = lax.rsqrt(var + LN_EPS)
    return zc * rstd, rstd


def _ln_bwd_core(dy, xh, rstd, g):
    dxh = dy * g
    m1 = jnp.mean(dxh, axis=-1, keepdims=True)
    m2 = jnp.mean(dxh * xh, axis=-1, keepdims=True)
    return rstd * (dxh - m1 - xh * m2)


def _row_ids(shape, base):
    return lax.broadcasted_iota(jnp.int32, shape, 0) + base


def _colsum(x):
    return jnp.sum(x, axis=0, keepdims=True)


def _dot(a, b, dims):
    return lax.dot_general(a, b, (dims, ((), ())), preferred_element_type=F32)


NN = ((1,), (0,))
NT = ((1,), (1,))
TN = ((0,), (0,))


def _embed(x, meta_full, g, b):
    s = x.shape[1]
    lp = s + TB
    nb = lp // TB

    def body(x_ref, m_ref, g_ref, b_ref, o_ref, hb_ref):
        i = pl.program_id(0)

        @pl.when(i == 0)
        def _():
            o_ref[0:PAD0, :] = jnp.zeros((PAD0, D_MODEL), F32)
            o_ref[PAD0:TB, :] = m_ref[...]

        @pl.when(i > 0)
        def _():
            o_ref[...] = x_ref[...]

        xh, _ = _ln_core(o_ref[...])
        h = xh * g_ref[...] + b_ref[...]
        rows = _row_ids(h.shape, i * TB)
        hb_ref[...] = jnp.where(rows >= PAD0, h, 0.0).astype(BF16)

    return pl.pallas_call(
        body, name="embed", grid=(nb,),
        in_specs=[pl.BlockSpec((None, TB, D_MODEL), lambda i: (0, jnp.maximum(i - 1, 0), 0)),
                  pl.BlockSpec((N_META, D_MODEL), lambda i: (0, 0)),
                  pl.BlockSpec((1, D_MODEL), lambda i: (0, 0)),
                  pl.BlockSpec((1, D_MODEL), lambda i: (0, 0))],
        out_specs=[pl.BlockSpec((TB, D_MODEL), lambda i: (i, 0)),
                   pl.BlockSpec((TB, D_MODEL), lambda i: (i, 0))],
        out_shape=[jax.ShapeDtypeStruct((lp, D_MODEL), F32),
                   jax.ShapeDtypeStruct((lp, D_MODEL), BF16)],
        compiler_params=_cp(1),
    )(x, meta_full, g, b)


def _loss_head(z, target, g, b):
    lp = z.shape[0]
    nb = lp // TB

    def body(z_ref, t_ref, g_ref, b_ref, dz_ref, st_ref, loss_ref):
        i = pl.program_id(0)

        @pl.when(i == 0)
        def _():
            st_ref[...] = jnp.zeros(st_ref.shape, F32)
            loss_ref[...] = jnp.zeros(loss_ref.shape, F32)
            dz_ref[...] = jnp.zeros(dz_ref.shape, F32)

        @pl.when(i > 0)
        def _():
            xh, rstd = _ln_core(z_ref[...])
            gg = g_ref[...]
            y = xh * gg + b_ref[...]
            e = y - t_ref[...]
            part = 0.5 * jnp.sum(jnp.mean(e * e, axis=-1, keepdims=True), axis=0, keepdims=True)
            loss_ref[...] += jnp.broadcast_to(part, loss_ref.shape)
            dy = e / float(D_MODEL)
            st_ref[0:1, :] += _colsum(dy * xh)
            st_ref[1:2, :] += _colsum(dy)
            dz_ref[...] = _ln_bwd_core(dy, xh, rstd, gg)

    return pl.pallas_call(
        body, name="loss_head", grid=(nb,),
        in_specs=[pl.BlockSpec((TB, D_MODEL), lambda i: (i, 0)),
                  pl.BlockSpec((None, TB, D_MODEL), lambda i: (0, jnp.maximum(i - 1, 0), 0)),
                  pl.BlockSpec((1, D_MODEL), lambda i: (0, 0)),
                  pl.BlockSpec((1, D_MODEL), lambda i: (0, 0))],
        out_specs=[pl.BlockSpec((TB, D_MODEL), lambda i: (i, 0)),
                   pl.BlockSpec((8, D_MODEL), lambda i: (0, 0)),
                   pl.BlockSpec((8, 128), lambda i: (0, 0))],
        out_shape=[jax.ShapeDtypeStruct((lp, D_MODEL), F32),
                   jax.ShapeDtypeStruct((8, D_MODEL), F32),
                   jax.ShapeDtypeStruct((8, 128), F32)],
        compiler_params=_cp(1),
    )(z, target, g, b)


def _ln_bwd(name, dh, z, g):
    lp = z.shape[0]
    tr = _row_tile(lp, 3)

    def body(dh_ref, z_ref, g_ref, dz_ref, st_ref):
        i = pl.program_id(0)

        @pl.when(i == 0)
        def _():
            st_ref[...] = jnp.zeros(st_ref.shape, F32)

        xh, rstd = _ln_core(z_ref[...])
        rows = _row_ids(xh.shape, i * tr)
        dy = jnp.where(rows >= PAD0, dh_ref[...], 0.0)
        st_ref[0:1, :] += _colsum(dy * xh)
        st_ref[1:2, :] += _colsum(dy)
        dz_ref[...] = _ln_bwd_core(dy, xh, rstd, g_ref[...])

    return pl.pallas_call(
        body, name=name, grid=(lp // tr,),
        in_specs=[pl.BlockSpec((tr, D_MODEL), lambda i: (i, 0)),
                  pl.BlockSpec((tr, D_MODEL), lambda i: (i, 0)),
                  pl.BlockSpec((1, D_MODEL), lambda i: (0, 0))],
        out_specs=[pl.BlockSpec((tr, D_MODEL), lambda i: (i, 0)),
                   pl.BlockSpec((8, D_MODEL), lambda i: (0, 0))],
        out_shape=[jax.ShapeDtypeStruct((lp, D_MODEL), F32),
                   jax.ShapeDtypeStruct((8, D_MODEL), F32)],
        compiler_params=_cp(1),
    )(dh, z, g)


def _ln_bwd_input(dh, z, g):
    lp = z.shape[0]
    nb = lp // TB
    s = lp - TB

    def body(dh_ref, z_ref, g_ref, gx_ref, gm_ref, st_ref):
        i = pl.program_id(0)

        @pl.when(i == 0)
        def _():
            st_ref[...] = jnp.zeros(st_ref.shape, F32)

        xh, rstd = _ln_core(z_ref[...])
        rows = _row_ids(xh.shape, i * TB)
        dy = jnp.where(rows >= PAD0, dh_ref[...], 0.0)
        st_ref[0:1, :] += _colsum(dy * xh)
        st_ref[1:2, :] += _colsum(dy)
        dz = _ln_bwd_core(dy, xh, rstd, g_ref[...])
        gx_ref[...] = dz

        @pl.when(i == 0)
        def _():
            gm_ref[...] = dz[PAD0:TB, :]

    return pl.pallas_call(
        body, name="ln_in_bwd", grid=(nb,),
        in_specs=[pl.BlockSpec((TB, D_MODEL), lambda i: (i, 0)),
                  pl.BlockSpec((TB, D_MODEL), lambda i: (i, 0)),
                  pl.BlockSpec((1, D_MODEL), lambda i: (0, 0))],
        out_specs=[pl.BlockSpec((None, TB, D_MODEL), lambda i: (0, jnp.maximum(i - 1, 0), 0)),
                   pl.BlockSpec((N_META, D_MODEL), lambda i: (0, 0)),
                   pl.BlockSpec((8, D_MODEL), lambda i: (0, 0))],
        out_shape=[jax.ShapeDtypeStruct((1, s, D_MODEL), F32),
                   jax.ShapeDtypeStruct((N_META, D_MODEL), F32),
                   jax.ShapeDtypeStruct((8, D_MODEL), F32)],
        compiler_params=_cp(1),
    )(dh, z, g)


def _mm_proj(name, hb, wg_in):
    lp = hb.shape[0]
    tm = lp // 3

    def body(a_ref, b_ref, o_ref):
        b = jnp.concatenate([b_ref[0], b_ref[1]], axis=1)
        o_ref[...] = _dot(a_ref[...], b, NN)

    return pl.pallas_call(
        body, name=name, grid=(3, N_DEV // 2),
        in_specs=[pl.BlockSpec((tm, D_MODEL), lambda i, j: (i, 0)),
                  pl.BlockSpec((2, D_MODEL, W_IN_SHARD), lambda i, j: (j, 0, 0))],
        out_specs=pl.BlockSpec((tm, 2 * W_IN_SHARD), lambda i, j: (i, j)),
        out_shape=jax.ShapeDtypeStruct((lp, IN_TOTAL), F32),
        compiler_params=_cp(2),
    )(hb, wg_in)


def _mm_out(name, ycat, wout, z, g, b, g2, b2):
    lp = ycat.shape[0]
    tm = lp // 6

    def body(a_ref, w_ref, z_ref, g_ref, b_ref, g2_ref, b2_ref, o_ref, hb_ref):
        i = pl.program_id(0)
        xh, _ = _ln_core(z_ref[...])
        h = xh * g_ref[...] + b_ref[...]
        live = _row_ids(h.shape, i * tm) >= PAD0
        h = jnp.where(live, h, 0.0)
        zn = ALPHA * h + _dot(a_ref[...], w_ref[...], NN)
        o_ref[...] = zn
        xh2, _ = _ln_core(zn)
        hb_ref[...] = jnp.where(live, xh2 * g2_ref[...] + b2_ref[...], 0.0).astype(BF16)

    vec = pl.BlockSpec((1, D_MODEL), lambda i: (0, 0))
    return pl.pallas_call(
        body, name=name, grid=(6,),
        in_specs=[pl.BlockSpec((tm, D_MODEL), lambda i: (i, 0)),
                  pl.BlockSpec((D_MODEL, D_MODEL), lambda i: (0, 0), pipeline_mode=pl.Buffered(1)),
                  pl.BlockSpec((tm, D_MODEL), lambda i: (i, 0)),
                  vec, vec, vec, vec],
        out_specs=[pl.BlockSpec((tm, D_MODEL), lambda i: (i, 0)),
                   pl.BlockSpec((tm, D_MODEL), lambda i: (i, 0))],
        out_shape=[jax.ShapeDtypeStruct((lp, D_MODEL), F32),
                   jax.ShapeDtypeStruct((lp, D_MODEL), BF16)],
        compiler_params=_cp(1),
    )(ycat, wout, z, g, b, g2, b2)


def _mm_dycat(name, dz, wout, dep):
    lp = dz.shape[0]
    tm = lp // 6

    def body(a_ref, w_ref, dep_ref, o_ref):
        del dep_ref
        o_ref[...] = _dot(a_ref[...].astype(BF16), w_ref[...], NT).astype(BF16)

    return pl.pallas_call(
        body, name=name, grid=(6,),
        in_specs=[pl.BlockSpec((tm, D_MODEL), lambda i: (i, 0)),
                  pl.BlockSpec((D_MODEL, D_MODEL), lambda i: (0, 0), pipeline_mode=pl.Buffered(1)),
                  pl.BlockSpec(memory_space=pl.ANY)],
        out_specs=pl.BlockSpec((tm, D_MODEL), lambda i: (i, 0)),
        out_shape=jax.ShapeDtypeStruct((lp, D_MODEL), BF16),
        compiler_params=_cp(1),
    )(dz, wout, dep)


def _mm_dwout(name, ycat, dz):
    lp = ycat.shape[0]
    tk = _row_tile(lp, 11)
    nk = lp // tk
    half = D_MODEL // 2

    def body(a_ref, b_ref, o_ref, acc_ref):
        k = pl.program_id(1)

        @pl.when(k == 0)
        def _():
            acc_ref[...] = jnp.zeros(acc_ref.shape, F32)

        acc_ref[...] += _dot(a_ref[...], b_ref[...].astype(BF16), TN)

        @pl.when(k == nk - 1)
        def _():
            o_ref[...] = acc_ref[...].astype(BF16)

    return pl.pallas_call(
        body, name=name, grid=(2, nk),
        in_specs=[pl.BlockSpec((tk, half), lambda h, k: (k, h)),
                  pl.BlockSpec((tk, D_MODEL), lambda h, k: (k, 0))],
        out_specs=pl.BlockSpec((half, D_MODEL), lambda h, k: (h, 0)),
        out_shape=jax.ShapeDtypeStruct((D_MODEL, D_MODEL), BF16),
        scratch_shapes=[pltpu.VMEM((half, D_MODEL), F32)],
        compiler_params=_cp(2),
    )(ycat, dz)


def _mm_dwin(name, hb, dproj, dep):
    lp = hb.shape[0]
    tk = _row_tile(lp, 11)
    nk = lp // tk

    def body(a_ref, b_ref, dep_ref, o_ref, acc_ref):
        del dep_ref
        k = pl.program_id(1)

        @pl.when(k == 0)
        def _():
            acc_ref[...] = jnp.zeros(acc_ref.shape, F32)

        acc_ref[...] += _dot(a_ref[...], b_ref[...], TN)

        @pl.when(k == nk - 1)
        def _():
            o_ref[0] = acc_ref[:, 0:W_IN_SHARD].astype(BF16)
            o_ref[1] = acc_ref[:, W_IN_SHARD:2 * W_IN_SHARD].astype(BF16)

    return pl.pallas_call(
        body, name=name, grid=(4, nk),
        in_specs=[pl.BlockSpec((tk, D_MODEL), lambda j, k: (k, 0)),
                  pl.BlockSpec((tk, 2 * W_IN_SHARD), lambda j, k: (k, j)),
                  pl.BlockSpec(memory_space=pl.ANY)],
        out_specs=pl.BlockSpec((2, D_MODEL, W_IN_SHARD), lambda j, k: (j, 0, 0)),
        out_shape=jax.ShapeDtypeStruct((N_DEV, D_MODEL, W_IN_SHARD), BF16),
        scratch_shapes=[pltpu.VMEM((D_MODEL, 2 * W_IN_SHARD), F32)],
        compiler_params=_cp(2),
    )(hb, dproj, dep)


def _mm_dh(name, dproj, wg_in, dz, dep):
    lp = dproj.shape[0]
    tm = lp // 6

    def body(a_ref, w_ref, dz_ref, dep_ref, o_ref, acc_ref):
        del dep_ref
        k = pl.program_id(1)

        @pl.when(k == 0)
        def _():
            acc_ref[...] = jnp.zeros(acc_ref.shape, F32)

        w = jnp.concatenate([w_ref[0], w_ref[1]], axis=1)
        acc_ref[...] += _dot(a_ref[...], w, NT)

        @pl.when(k == N_DEV // 2 - 1)
        def _():
            o_ref[...] = acc_ref[...] + ALPHA * dz_ref[...]

    return pl.pallas_call(
        body, name=name, grid=(6, N_DEV // 2),
        in_specs=[pl.BlockSpec((tm, 2 * W_IN_SHARD), lambda i, k: (i, k)),
                  pl.BlockSpec((2, D_MODEL, W_IN_SHARD), lambda i, k: (k, 0, 0)),
                  pl.BlockSpec((tm, D_MODEL), lambda i, k: (i, 0)),
                  pl.BlockSpec(memory_space=pl.ANY)],
        out_specs=pl.BlockSpec((tm, D_MODEL), lambda i, k: (i, 0)),
        out_shape=jax.ShapeDtypeStruct((lp, D_MODEL), F32),
        scratch_shapes=[pltpu.VMEM((tm, D_MODEL), F32)],
        compiler_params=_cp(2),
    )(dproj, wg_in, dz, dep)


SUB = 128


def _shift_plan(cat, n_shift, base):
    rolled = [cat] + [pltpu.roll(cat, b, axis=0) for b in range(1, 8)]
    return [(rolled[s % 8], base - 8 * (s // 8)) for s in range(n_shift)]


def _tap_sum(w_ref, plan, rows, init=None):
    blocks = []
    for r0 in range(0, rows, SUB):
        row = []
        for c0 in range(0, CONV_W, SUB):
            acc = (jnp.zeros((SUB, SUB), F32) if init is None
                   else jnp.broadcast_to(init[:, c0:c0 + SUB], (SUB, SUB)))
            for k, (arr, off) in enumerate(plan):
                acc = acc + w_ref[k:k + 1, c0:c0 + SUB] * arr[off + r0:off + r0 + SUB, c0:c0 + SUB]
            row.append(acc)
        blocks.append(jnp.concatenate(row, axis=1))
    return jnp.concatenate(blocks, axis=0)


def _tap_grads(dw_ref, dy, plan, rows):
    for c0 in range(0, CONV_W, SUB):
        dys = [dy[r0:r0 + SUB, c0:c0 + SUB] for r0 in range(0, rows, SUB)]
        for k, (arr, off) in enumerate(plan):
            part = None
            for ri, r0 in enumerate(range(0, rows, SUB)):
                prod = dys[ri] * arr[off + r0:off + r0 + SUB, c0:c0 + SUB]
                for i in range(SUB // 8):
                    piece = prod[8 * i:8 * i + 8, :]
                    part = piece if part is None else part + piece
            dw_ref[k:k + 1, c0:c0 + SUB] += jnp.sum(part, axis=0, keepdims=True)


CONV_HALO = 32


def _conv_chain(j, tb, cv_ref, cg_ref, cvp_ref, cgp_ref, wdw_ref, vec_ref, wpw_ref, c1_ref=None):
    cv = cv_ref[...]
    sg = _sig(cg_ref[...])
    c0 = cv * sg
    c0p = jnp.where(j > 0, cvp_ref[...] * _sig(cgp_ref[...]), 0.0)
    cat = jnp.concatenate([c0p, c0], axis=0)
    shifts = _shift_plan(cat, CONV_K, CONV_HALO)
    taps = [shifts[CONV_K - 1 - k] for k in range(CONV_K)]
    if c1_ref is None:
        c1 = _tap_sum(wdw_ref, taps, tb, init=vec_ref[0:1, :])
    else:
        c1 = c1_ref[...]
    xh, rstd = _ln_core(c1)
    c2 = xh * vec_ref[1:2, :] + vec_ref[2:3, :]
    s2 = _sig(c2)
    c3 = c2 * s2
    c4 = _dot(c3.astype(BF16), wpw_ref[...], NN) + vec_ref[3:4, :]
    return dict(cv=cv, sg=sg, taps=taps, c1=c1, xh=xh, rstd=rstd, c2=c2, s2=s2, c3=c3, c4=c4)


def _conv_in_specs(jmap, tb):
    def cur(col):
        return pl.BlockSpec((tb, 512), lambda n: (jmap(n), col))

    def prev(col):
        return pl.BlockSpec((CONV_HALO, 512),
                            lambda n: (jnp.maximum(jmap(n) * (tb // CONV_HALO) - 1, 0), col))

    return [cur(COL_CV), cur(COL_CG), prev(COL_CV), prev(COL_CG), cur(COL_CGATE)]


def _conv_param_specs():
    return [pl.BlockSpec((32, CONV_W), lambda n: (0, 0)),
            pl.BlockSpec((8, CONV_W), lambda n: (0, 0)),
            pl.BlockSpec((CONV_W, CONV_W), lambda n: (0, 0))]


def _conv_fwd(name, proj, wdw, vec, wpw):
    lp = proj.shape[0]
    tb = _row_tile(lp, 3)
    nb = lp // tb

    def body(cv_ref, cg_ref, cvp_ref, cgp_ref, gate_ref, wdw_ref, vec_ref, wpw_ref, o_ref, c1_ref):
        j = pl.program_id(0)
        c = _conv_chain(j, tb, cv_ref, cg_ref, cvp_ref, cgp_ref, wdw_ref, vec_ref, wpw_ref)
        gate = gate_ref[...]
        o_ref[...] = (c["c4"] * (gate * _sig(gate))).astype(BF16)
        c1_ref[...] = c["c1"]

    return pl.pallas_call(
        body, name=name, grid=(nb,),
        in_specs=_conv_in_specs(lambda n: n, tb) + _conv_param_specs(),
        out_specs=[pl.BlockSpec((tb, 512), lambda n: (n, YC_CONV)),
                   pl.BlockSpec((tb, CONV_W), lambda n: (n, 0))],
        out_shape=[jax.ShapeDtypeStruct((lp, D_MODEL), BF16),
                   jax.ShapeDtypeStruct((lp, CONV_W), F32)],
        compiler_params=_cp(1),
    )(proj, proj, proj, proj, proj, wdw, vec, wpw)


def _conv_bwd(name, proj, dycat, c1, wdw, vec, wpw):
    lp = proj.shape[0]
    tb = _row_tile(lp, 3)
    nb = lp // tb
    halo = CONV_HALO

    def body(cv_ref, cg_ref, cvp_ref, cgp_ref, gate_ref, dy_ref, c1_ref, wdw_ref, vec_ref, wpw_ref,
             dp_ref, dwdw_ref, dvec_ref, dwpw_ref, carry_ref):
        n = pl.program_id(0)
        j = nb - 1 - n

        @pl.when(n == 0)
        def _():
            carry_ref[...] = jnp.zeros(carry_ref.shape, F32)
            dwdw_ref[...] = jnp.zeros(dwdw_ref.shape, F32)
            dvec_ref[...] = jnp.zeros(dvec_ref.shape, F32)
            dwpw_ref[...] = jnp.zeros(dwpw_ref.shape, F32)

        c = _conv_chain(j, tb, cv_ref, cg_ref, cvp_ref, cgp_ref, wdw_ref, vec_ref, wpw_ref, c1_ref)
        dy = dy_ref[...].astype(F32)
        gate = gate_ref[...]
        sgate = _sig(gate)
        dc4 = dy * (gate * sgate)
        dgate = dy * c["c4"] * _dsilu(gate, sgate)
        dc4b = dc4.astype(BF16)
        dvec_ref[3:4, :] += _colsum(dc4)
        dwpw_ref[...] += _dot(c["c3"].astype(BF16), dc4b, TN)
        dc3 = _dot(dc4b, wpw_ref[...], NT)
        dc2 = dc3 * _dsilu(c["c2"], c["s2"])
        dvec_ref[1:2, :] += _colsum(dc2 * c["xh"])
        dvec_ref[2:3, :] += _colsum(dc2)
        dc1 = _ln_bwd_core(dc2, c["xh"], c["rstd"], vec_ref[1:2, :])
        dvec_ref[0:1, :] += _colsum(dc1)
        _tap_grads(dwdw_ref, dc1, c["taps"], tb)
        dcat = jnp.concatenate([dc1, carry_ref[...]], axis=0)
        total = tb + halo
        up = [dcat] + [pltpu.roll(dcat, total - b, axis=0) for b in range(1, 8)]
        ahead = [(up[(CONV_K - 1 - k) % 8], 8 * ((CONV_K - 1 - k) // 8)) for k in range(CONV_K)]
        dc0 = _tap_sum(wdw_ref, ahead, tb)
        carry_ref[...] = dc1[0:halo, :]
        sg = c["sg"]
        dcv = dc0 * sg
        dcg = dc0 * c["cv"] * sg * (1.0 - sg)
        dp_ref[:, 0:512] = dcv.astype(BF16)
        dp_ref[:, 512:1024] = dcg.astype(BF16)
        dp_ref[:, 1024:1536] = dgate.astype(BF16)

    jmap = lambda n: nb - 1 - n
    return pl.pallas_call(
        body, name=name, grid=(nb,),
        in_specs=(_conv_in_specs(jmap, tb)
                  + [pl.BlockSpec((tb, 512), lambda n: (jmap(n), YC_CONV)),
                     pl.BlockSpec((tb, CONV_W), lambda n: (jmap(n), 0))]
                  + _conv_param_specs()),
        out_specs=[pl.BlockSpec((tb, 1536), lambda n: (jmap(n), 0)),
                   pl.BlockSpec((32, CONV_W), lambda n: (0, 0)),
                   pl.BlockSpec((8, CONV_W), lambda n: (0, 0)),
                   pl.BlockSpec((CONV_W, CONV_W), lambda n: (0, 0))],
        out_shape=[jax.ShapeDtypeStruct((lp, IN_TOTAL), BF16),
                   jax.ShapeDtypeStruct((32, CONV_W), F32),
                   jax.ShapeDtypeStruct((8, CONV_W), F32),
                   jax.ShapeDtypeStruct((CONV_W, CONV_W), F32)],
        scratch_shapes=[pltpu.VMEM((halo, CONV_W), F32)],
        compiler_params=_cp(1),
    )(proj, proj, proj, proj, proj, dycat, c1, wdw, vec, wpw)


def _rope_tables(lp):
    half = ROT_DIM // 2
    inv_freq = ROPE_THETA ** (-jnp.arange(half, dtype=F32) / half)
    pos = (jnp.arange(lp, dtype=jnp.int32) - PAD0).astype(F32)
    ang = pos[:, None] * inv_freq[None, :]
    cos, sin = jnp.cos(ang), jnp.sin(ang)
    ones = jnp.ones((lp, HEAD_DIM - ROT_DIM), F32)
    zeros = jnp.zeros((lp, HEAD_DIM - ROT_DIM), F32)
    zh = jnp.zeros((lp, half), F32)
    c = jnp.concatenate([cos, cos, ones], axis=1)
    sa = jnp.concatenate([-sin, zh, zeros], axis=1)
    sb = jnp.concatenate([zh, sin, zeros], axis=1)
    tile = lambda t: jnp.tile(t, (1, KV_W // HEAD_DIM))
    return tile(c), tile(sa), tile(sb)


def _rot(x, c, sa, sb):
    w = x.shape[1]
    return x * c + pltpu.roll(x, w - 8, axis=1) * sa + pltpu.roll(x, 8, axis=1) * sb


def _rot_t(dy, c, sa, sb):
    w = dy.shape[1]
    return dy * c + pltpu.roll(dy * sa, 8, axis=1) + pltpu.roll(dy * sb, w - 8, axis=1)


def _rope_fwd(name, proj, tabs):
    lp = proj.shape[0]
    tr = _row_tile(lp, 11)

    def body(q0_ref, q1_ref, k_ref, c_ref, sa_ref, sb_ref, qr_ref, kr_ref):
        c, sa, sb = c_ref[...], sa_ref[...], sb_ref[...]
        c2 = jnp.concatenate([c, c], axis=1)
        sa2 = jnp.concatenate([sa, sa], axis=1)
        sb2 = jnp.concatenate([sb, sb], axis=1)
        qr_ref[:, 0:512] = (_rot(q0_ref[...], c2, sa2, sb2) * ATT_SCALE).astype(BF16)
        qr_ref[:, 512:1024] = (_rot(q1_ref[...], c2, sa2, sb2) * ATT_SCALE).astype(BF16)
        kr_ref[...] = _rot(k_ref[...], c, sa, sb).astype(BF16)

    tab = pl.BlockSpec((tr, KV_W), lambda i: (i, 0))
    return pl.pallas_call(
        body, name=name, grid=(lp // tr,),
        in_specs=[pl.BlockSpec((tr, 512), lambda i: (i, COL_Q0)),
                  pl.BlockSpec((tr, 512), lambda i: (i, COL_Q0 + 1)),
                  pl.BlockSpec((tr, KV_W), lambda i: (i, COL_K256)),
                  tab, tab, tab],
        out_specs=[pl.BlockSpec((tr, ATT_W), lambda i: (i, 0)),
                   pl.BlockSpec((tr, KV_W), lambda i: (i, 0))],
        out_shape=[jax.ShapeDtypeStruct((lp, ATT_W), BF16),
                   jax.ShapeDtypeStruct((lp, KV_W), BF16)],
        compiler_params=_cp(1),
    )(proj, proj, proj, *tabs)


def _attn_mask(j):
    qi = lax.broadcasted_iota(jnp.int32, (GROUP * TB, 3 * TB), 0) & (TB - 1)
    cc = lax.broadcasted_iota(jnp.int32, (GROUP * TB, 3 * TB), 1)
    jj = cc & (TB - 1)
    is_meta = jj >= PAD0
    p0 = (cc < TB) & is_meta & (j >= 1)
    p1 = (cc >= TB) & (cc < 2 * TB) & (jj > qi) & (j >= 2)
    p2 = (cc >= 2 * TB) & (jj <= qi) & ((j >= 1) | is_meta)
    return p0 | p1 | p2


def _lane_group(rows):
    return lax.broadcasted_iota(jnp.int32, (rows, KV_W), 1) // HEAD_DIM


def _stack_heads(x, kv, lgq):
    parts = []
    for g in range(GROUP):
        sh = ((kv - g) % GROUP) * HEAD_DIM
        moved = x if sh == 0 else pltpu.roll(x, sh, axis=1)
        parts.append(jnp.where(lgq == kv, moved, 0.0))
    return jnp.concatenate(parts, axis=0).astype(BF16)


def _unstack_heads(r, kv):
    out = None
    for g in range(GROUP):
        blk = r[g * TB:(g + 1) * TB, :]
        sh = ((g - kv) % GROUP) * HEAD_DIM
        blk = blk if sh == 0 else pltpu.roll(blk, sh, axis=1)
        out = blk if out is None else out + blk
    return out


def _sink_column(sinks, kv):
    lane = lax.broadcasted_iota(jnp.int32, (1, 128), 1)
    cols = []
    for g in range(GROUP):
        sg = jnp.sum(jnp.where(lane == kv * GROUP + g, sinks, 0.0), axis=1, keepdims=True)
        cols.append(jnp.broadcast_to(sg, (TB, 1)))
    return jnp.concatenate(cols, axis=0)


def _attn_kv(kall, vall, lg, kv):
    km = jnp.where(lg == kv, kall, 0.0).astype(BF16)
    vm = jnp.where(lg == kv, vall, 0.0).astype(BF16)
    ones = jnp.where(lg == kv, 1.0, 0.0).astype(BF16)
    return km, vm, ones


def _attn_weights(qst, km, vm, ones, sinkcol, valid, lg4, kv):
    s = jnp.where(valid, _dot(qst, km, NT), NEG_INF)
    m = jnp.maximum(jnp.max(s, axis=-1, keepdims=True), sinkcol)
    eb = jnp.exp(s - m).astype(BF16)
    es = jnp.exp(sinkcol - m)
    r = _dot(eb, vm, NN)
    inv = 1.0 / (_dot(eb, ones, NN) + es)
    out = jnp.where(lg4 == kv, r * inv, 0.0)
    return eb, es, inv, out


def _attn_specs(jmap):
    blk = lambda col: pl.BlockSpec((TB, KV_W), lambda n: (jmap(n), col))
    prv = lambda col: pl.BlockSpec((TB, KV_W), lambda n: (jnp.maximum(jmap(n) - 1, 0), col))
    met = lambda col: pl.BlockSpec((TB, KV_W), lambda n: (0, col))
    return dict(
        qr=pl.BlockSpec((TB, ATT_W), lambda n: (jmap(n), 0)),
        k=[met(0), prv(0), blk(0)],
        v=[met(COL_V256), prv(COL_V256), blk(COL_V256)],
        gate=pl.BlockSpec((TB, ATT_W), lambda n: (jmap(n), COL_AGATE1024)),
        sinks=pl.BlockSpec((8, 128), lambda n: (0, 0)),
    )


def _attn_fwd(name, qr, kr, proj, sinks_row, ycat):
    lp = proj.shape[0]
    nb = lp // TB
    sp = _attn_specs(lambda n: n)

    def body(qr_ref, km_ref, kp_ref, kc_ref, vm_ref, vp_ref, vc_ref, gate_ref, sink_ref, yin_ref, o_ref):
        del yin_ref
        j = pl.program_id(0)
        valid = _attn_mask(j)
        kall = jnp.concatenate([km_ref[...], kp_ref[...], kc_ref[...]], axis=0).astype(F32)
        vall = jnp.concatenate([vm_ref[...], vp_ref[...], vc_ref[...]], axis=0)
        lg = _lane_group(3 * TB)
        lgq = _lane_group(TB)
        lg4 = _lane_group(GROUP * TB)
        sinks = sink_ref[0:1, :]
        for kv in range(N_KV):
            cols = slice(kv * KV_W, (kv + 1) * KV_W)
            km, vm, ones = _attn_kv(kall, vall, lg, kv)
            qst = _stack_heads(qr_ref[:, cols].astype(F32), kv, lgq)
            _, _, _, out = _attn_weights(qst, km, vm, ones, _sink_column(sinks, kv), valid, lg4, kv)
            att = _unstack_heads(out, kv)
            gate = gate_ref[:, cols]
            o_ref[:, cols] = (att * (gate * _sig(gate))).astype(BF16)

    return pl.pallas_call(
        body, name=name, grid=(nb,),
        in_specs=[sp["qr"]] + sp["k"] + sp["v"] + [sp["gate"], sp["sinks"],
                                                   pl.BlockSpec(memory_space=pl.ANY)],
        out_specs=pl.BlockSpec((TB, ATT_W), lambda n: (n, 0)),
        out_shape=jax.ShapeDtypeStruct((lp, D_MODEL), BF16),
        input_output_aliases={9: 0},
        compiler_params=_cp(1),
    )(qr, kr, kr, kr, proj, proj, proj, proj, sinks_row, ycat)


def _attn_bwd(name, qr, kr, proj, sinks_row, dycat, dep):
    lp = proj.shape[0]
    nb = lp // TB
    sp = _attn_specs(lambda n: n)

    def body(qr_ref, km_ref, kp_ref, kc_ref, vm_ref, vp_ref, vc_ref, gate_ref, sink_ref, dy_ref, dep_ref,
             dq_ref, dgate_ref, dk_ref, dv_ref, dsink_ref):
        del dep_ref
        j = pl.program_id(0)

        @pl.when(j == 0)
        def _():
            dk_ref[...] = jnp.zeros(dk_ref.shape, F32)
            dv_ref[...] = jnp.zeros(dv_ref.shape, F32)
            dsink_ref[...] = jnp.zeros(dsink_ref.shape, F32)

        valid = _attn_mask(j)
        kall = jnp.concatenate([km_ref[...], kp_ref[...], kc_ref[...]], axis=0).astype(F32)
        vall = jnp.concatenate([vm_ref[...], vp_ref[...], vc_ref[...]], axis=0)
        lg = _lane_group(3 * TB)
        lgq = _lane_group(TB)
        lg4 = _lane_group(GROUP * TB)
        sinks = sink_ref[0:1, :]
        lane = lax.broadcasted_iota(jnp.int32, (1, 128), 1)
        dkall = jnp.zeros((3 * TB, KV_W), F32)
        dvall = jnp.zeros((3 * TB, KV_W), F32)
        dsink = jnp.zeros((1, 128), F32)
        for kv in range(N_KV):
            cols = slice(kv * KV_W, (kv + 1) * KV_W)
            km, vm, ones = _attn_kv(kall, vall, lg, kv)
            qst = _stack_heads(qr_ref[:, cols].astype(F32), kv, lgq)
            gate = gate_ref[:, cols]
            sgate = _sig(gate)
            dy = dy_ref[:, cols].astype(F32)
            dout = dy * (gate * sgate)
            eb, es, inv, out = _attn_weights(qst, km, vm, ones, _sink_column(sinks, kv), valid, lg4, kv)
            att = _unstack_heads(out, kv)
            dgate_ref[:, cols] = (dy * att * _dsilu(gate, sgate)).astype(BF16)
            dsc = dout * _unstack_heads(jnp.where(lg4 == kv, inv, 0.0), kv)
            dost = _stack_heads(dsc, kv, lgq)
            dd = dsc * att
            dcol = jnp.concatenate(
                [jnp.sum(jnp.where(lgq == g, dd, 0.0), axis=1, keepdims=True) for g in range(GROUP)], axis=0)
            dp = _dot(dost, vm, NT)
            ds = (eb.astype(F32) * (dp - dcol)).astype(BF16)
            pd = es * dcol
            for g in range(GROUP):
                tot = jnp.sum(pd[g * TB:(g + 1) * TB, :], axis=0, keepdims=True)
                dsink = dsink - jnp.where(lane == kv * GROUP + g, tot, 0.0)
            dq_ref[:, cols] = _unstack_heads(_dot(ds, km, NN), kv)
            dkall = dkall + _dot(ds, qst, TN)
            dvall = dvall + _dot(eb, dost, TN)
        dsink_ref[0:1, :] += dsink
        prev = pl.multiple_of(jnp.maximum(j - 1, 0) * TB, TB)
        cur = pl.multiple_of(j * TB, TB)
        dk_ref[0:TB, :] += dkall[0:TB]
        dv_ref[0:TB, :] += dvall[0:TB]
        dk_ref[pl.ds(prev, TB), :] += dkall[TB:2 * TB]
        dv_ref[pl.ds(prev, TB), :] += dvall[TB:2 * TB]
        dk_ref[pl.ds(cur, TB), :] += dkall[2 * TB:3 * TB]
        dv_ref[pl.ds(cur, TB), :] += dvall[2 * TB:3 * TB]

    return pl.pallas_call(
        body, name=name, grid=(nb,),
        in_specs=[sp["qr"]] + sp["k"] + sp["v"] + [sp["gate"], sp["sinks"],
                                                   pl.BlockSpec((TB, ATT_W), lambda n: (n, 0)),
                                                   pl.BlockSpec(memory_space=pl.ANY)],
        out_specs=[pl.BlockSpec((TB, ATT_W), lambda n: (n, 0)),
                   pl.BlockSpec((TB, ATT_W), lambda n: (n, 0)),
                   pl.BlockSpec((lp, KV_W), lambda n: (0, 0)),
                   pl.BlockSpec((lp, KV_W), lambda n: (0, 0)),
                   pl.BlockSpec((8, 128), lambda n: (0, 0))],
        out_shape=[jax.ShapeDtypeStruct((lp, ATT_W), F32),
                   jax.ShapeDtypeStruct((lp, ATT_W), BF16),
                   jax.ShapeDtypeStruct((lp, KV_W), F32),
                   jax.ShapeDtypeStruct((lp, KV_W), F32),
                   jax.ShapeDtypeStruct((8, 128), F32)],
        compiler_params=_cp(1),
    )(qr, kr, kr, kr, proj, proj, proj, proj, sinks_row, dycat, dep)


def _attn_assemble(name, dq, dgate, dk, dv, tabs, dproj):
    lp = dq.shape[0]
    tr = _row_tile(lp, 11)

    def body(dq_ref, dg_ref, dk_ref, dv_ref, c_ref, sa_ref, sb_ref, din_ref, o_ref):
        del din_ref
        cidx = pl.program_id(1)
        c, sa, sb = c_ref[...], sa_ref[...], sb_ref[...]

        @pl.when(cidx < 2)
        def _():
            c2 = jnp.concatenate([c, c], axis=1)
            sa2 = jnp.concatenate([sa, sa], axis=1)
            sb2 = jnp.concatenate([sb, sb], axis=1)
            o_ref[...] = (_rot_t(dq_ref[...], c2, sa2, sb2) * ATT_SCALE).astype(BF16)

        @pl.when(cidx == 2)
        def _():
            o_ref[:, 0:KV_W] = _rot_t(dk_ref[...], c, sa, sb).astype(BF16)
            o_ref[:, KV_W:2 * KV_W] = dv_ref[...].astype(BF16)

        @pl.when(cidx > 2)
        def _():
            o_ref[...] = dg_ref[...]

    tab = pl.BlockSpec((tr, KV_W), lambda n, c: (n, 0))
    return pl.pallas_call(
        body, name=name, grid=(lp // tr, 5),
        in_specs=[pl.BlockSpec((tr, 512), lambda n, c: (n, jnp.minimum(c, 1))),
                  pl.BlockSpec((tr, 512), lambda n, c: (n, jnp.clip(c - 3, 0, 1))),
                  tab, tab,
                  tab, tab, tab,
                  pl.BlockSpec(memory_space=pl.ANY)],
        out_specs=pl.BlockSpec((tr, 512), lambda n, c: (n, COL_Q0 + c)),
        out_shape=jax.ShapeDtypeStruct((lp, IN_TOTAL), BF16),
        input_output_aliases={7: 0},
        compiler_params=_cp(2),
    )(dq, dgate, dk, dv, *tabs, dproj)


def _softplus_neg(lam):
    t = jnp.exp(-jnp.abs(lam))
    u = 1.0 + t
    den = jnp.where(u == 1.0, 1.0, u - 1.0)
    l1p = jnp.where(u == 1.0, t, jnp.log(u) * (t / den))
    return jnp.maximum(-lam, 0.0) + l1p


def _lru_chain(j, tb, rx_ref, rxp_ref, wl_ref, vec_ref, wa_ref, wx_ref):
    rx = rx_ref[...]
    rxp = jnp.where(j > 0, rxp_ref[...], 0.0)
    cat = jnp.concatenate([rxp, rx], axis=0)
    views = [cat[8:8 + tb, :]] + [pltpu.roll(cat, s, axis=0)[8:8 + tb, :] for s in range(1, LRU_CONV_K)]
    x1 = jnp.broadcast_to(vec_ref[0:1, :], (tb, LRU_W))
    for k in range(LRU_CONV_K):
        x1 = x1 + wl_ref[k:k + 1, :] * views[LRU_CONV_K - 1 - k]
    x1b = x1.astype(BF16)
    r = _sig(_dot(x1b, wa_ref[...], NN) + vec_ref[1:2, :])
    ig = _sig(_dot(x1b, wx_ref[...], NN) + vec_ref[2:3, :])
    sp = _softplus_neg(vec_ref[3:4, :])
    log_a = -LRU_C * r * sp
    rows = _row_ids((tb, LRU_W), j * tb)
    live = rows >= PAD0
    a = jnp.where(live, jnp.exp(log_a), 0.0)
    y2 = 2.0 * log_a
    em = -jnp.tanh(0.5 * y2) * (jnp.exp(y2) + 1.0)
    mult = jnp.sqrt(em)
    return dict(views=views, x1=x1, x1b=x1b, r=r, ig=ig, sp=sp, a=a, mult=mult, live=live, a_raw=jnp.exp(log_a))


def _lru_specs(jmap, tb):
    return [pl.BlockSpec((tb, 512), lambda n: (jmap(n), COL_RX)),
            pl.BlockSpec((8, 512), lambda n: (jnp.maximum(jmap(n) * (tb // 8) - 1, 0), COL_RX)),
            pl.BlockSpec((tb, 512), lambda n: (jmap(n), COL_RGATE))]


def _lru_param_specs():
    return [pl.BlockSpec((8, LRU_W), lambda n: (0, 0)),
            pl.BlockSpec((8, LRU_W), lambda n: (0, 0)),
            pl.BlockSpec((LRU_W, LRU_W), lambda n: (0, 0)),
            pl.BlockSpec((LRU_W, LRU_W), lambda n: (0, 0))]


def _lru_fwd(name, proj, wl, vec, wa, wx, ycat):
    lp = proj.shape[0]
    tb = _row_tile(lp, 3)
    nb = lp // tb

    def body(rx_ref, rxp_ref, gate_ref, wl_ref, vec_ref, wa_ref, wx_ref, yin_ref, o_ref, h_ref, carry_ref):
        del yin_ref
        j = pl.program_id(0)

        @pl.when(j == 0)
        def _():
            carry_ref[...] = jnp.zeros(carry_ref.shape, F32)

        c = _lru_chain(j, tb, rx_ref, rxp_ref, wl_ref, vec_ref, wa_ref, wx_ref)
        a = c["a"]
        u = jnp.where(c["live"], c["mult"] * (c["ig"] * c["x1"]), 0.0)
        rows = lax.broadcasted_iota(jnp.int32, (tb, LRU_W), 0)
        d = 1
        while d < tb:
            ap = jnp.where(rows >= d, pltpu.roll(a, d, axis=0), 1.0)
            up = jnp.where(rows >= d, pltpu.roll(u, d, axis=0), 0.0)
            u = a * up + u
            a = a * ap
            d *= 2
        h = u + a * carry_ref[0:1, :]
        carry_ref[...] = h[tb - 8:tb, :]
        carry_ref[0:1, :] = h[tb - 1:tb, :]
        h_ref[...] = h
        gate = gate_ref[...]
        o_ref[...] = (h * (gate * _sig(gate))).astype(BF16)

    return pl.pallas_call(
        body, name=name, grid=(nb,),
        in_specs=_lru_specs(lambda n: n, tb) + _lru_param_specs() + [pl.BlockSpec(memory_space=pl.ANY)],
        out_specs=[pl.BlockSpec((tb, 512), lambda n: (n, YC_LRU)),
                   pl.BlockSpec((tb, LRU_W), lambda n: (n, 0))],
        out_shape=[jax.ShapeDtypeStruct((lp, D_MODEL), BF16),
                   jax.ShapeDtypeStruct((lp, LRU_W), F32)],
        input_output_aliases={7: 0},
        scratch_shapes=[pltpu.VMEM((8, LRU_W), F32)],
        compiler_params=_cp(1),
    )(proj, proj, proj, wl, vec, wa, wx, ycat)


def _lru_bwd(name, proj, dycat, hstate, wl, vec, wa, wx, dproj):
    lp = proj.shape[0]
    tb = _row_tile(lp, 3)
    nb = lp // tb

    def body(rx_ref, rxp_ref, gate_ref, dy_ref, h_ref, hp_ref, wl_ref, vec_ref, wa_ref, wx_ref, din_ref,
             dp_ref, dwl_ref, dvec_ref, dwa_ref, dwx_ref, dhc_ref, anx_ref, dxc_ref):
        del din_ref
        n = pl.program_id(0)
        j = nb - 1 - n

        @pl.when(n == 0)
        def _():
            dhc_ref[...] = jnp.zeros(dhc_ref.shape, F32)
            anx_ref[...] = jnp.zeros(anx_ref.shape, F32)
            dxc_ref[...] = jnp.zeros(dxc_ref.shape, F32)
            dwl_ref[...] = jnp.zeros(dwl_ref.shape, F32)
            dvec_ref[...] = jnp.zeros(dvec_ref.shape, F32)
            dwa_ref[...] = jnp.zeros(dwa_ref.shape, F32)
            dwx_ref[...] = jnp.zeros(dwx_ref.shape, F32)

        c = _lru_chain(j, tb, rx_ref, rxp_ref, wl_ref, vec_ref, wa_ref, wx_ref)
        a, mult, r, ig, x1, live = c["a"], c["mult"], c["r"], c["ig"], c["x1"], c["live"]
        h = h_ref[...]
        gate = gate_ref[...]
        sgate = _sig(gate)
        dy = dy_ref[...].astype(F32)
        gsum = dy * (gate * sgate)
        dgate = dy * h * _dsilu(gate, sgate)
        rows = lax.broadcasted_iota(jnp.int32, (tb, LRU_W), 0)
        bb = jnp.where(rows == tb - 1, anx_ref[0:1, :], pltpu.roll(a, tb - 1, axis=0))
        gg = gsum
        d = 1
        while d < tb:
            keep = rows < tb - d
            bn = jnp.where(keep, pltpu.roll(bb, tb - d, axis=0), 1.0)
            gn = jnp.where(keep, pltpu.roll(gg, tb - d, axis=0), 0.0)
            gg = gg + bb * gn
            bb = bb * bn
            d *= 2
        dh = gg + bb * dhc_ref[0:1, :]
        dhc_ref[...] = dh[0:8, :]
        anx_ref[...] = a[0:8, :]
        hprev = jnp.where(rows == 0, jnp.where(j > 0, hp_ref[7:8, :], 0.0), pltpu.roll(h, 1, axis=0))
        du = jnp.where(live, dh, 0.0)
        da = jnp.where(live, dh * hprev, 0.0)
        ar = c["a_raw"]
        dmult = du * (ig * x1)
        di = du * mult * x1
        dx1 = du * mult * ig
        dloga = da * ar - dmult * ar * ar / mult
        dr = dloga * (-LRU_C * c["sp"])
        dvec_ref[3:4, :] += _colsum(dloga * (-LRU_C * r))
        dza = dr * r * (1.0 - r)
        dzx = di * ig * (1.0 - ig)
        dzab, dzxb = dza.astype(BF16), dzx.astype(BF16)
        dvec_ref[1:2, :] += _colsum(dza)
        dvec_ref[2:3, :] += _colsum(dzx)
        dwa_ref[...] += _dot(c["x1b"], dzab, TN)
        dwx_ref[...] += _dot(c["x1b"], dzxb, TN)
        dx1 = dx1 + _dot(dzab, wa_ref[...], NT) + _dot(dzxb, wx_ref[...], NT)
        dvec_ref[0:1, :] += _colsum(dx1)
        for k in range(LRU_CONV_K):
            dwl_ref[k:k + 1, :] += _colsum(dx1 * c["views"][LRU_CONV_K - 1 - k])
        dcat = jnp.concatenate([dx1, dxc_ref[...]], axis=0)
        drx = jnp.zeros((tb, LRU_W), F32)
        for k in range(LRU_CONV_K):
            s = LRU_CONV_K - 1 - k
            view = dcat[0:tb, :] if s == 0 else pltpu.roll(dcat, tb + 8 - s, axis=0)[0:tb, :]
            drx = drx + wl_ref[k:k + 1, :] * view
        dxc_ref[...] = dx1[0:8, :]
        dp_ref[:, 0:512] = drx.astype(BF16)
        dp_ref[:, 512:1024] = dgate.astype(BF16)

        @pl.when(n == nb - 1)
        def _():
            lam = vec_ref[3:4, :]
            dvec_ref[3:4, :] = dvec_ref[3:4, :] * (-_sig(-lam))

    jmap = lambda n: nb - 1 - n
    return pl.pallas_call(
        body, name=name, grid=(nb,),
        in_specs=(_lru_specs(jmap, tb)
                  + [pl.BlockSpec((tb, 512), lambda n: (jmap(n), YC_LRU)),
                     pl.BlockSpec((tb, LRU_W), lambda n: (jmap(n), 0)),
                     pl.BlockSpec((8, LRU_W), lambda n: (jnp.maximum(jmap(n) * (tb // 8) - 1, 0), 0))]
                  + _lru_param_specs() + [pl.BlockSpec(memory_space=pl.ANY)]),
        out_specs=[pl.BlockSpec((tb, 1024), lambda n: (jmap(n), 4)),
                   pl.BlockSpec((8, LRU_W), lambda n: (0, 0)),
                   pl.BlockSpec((8, LRU_W), lambda n: (0, 0)),
                   pl.BlockSpec((LRU_W, LRU_W), lambda n: (0, 0)),
                   pl.BlockSpec((LRU_W, LRU_W), lambda n: (0, 0))],
        out_shape=[jax.ShapeDtypeStruct((lp, IN_TOTAL), BF16),
                   jax.ShapeDtypeStruct((8, LRU_W), F32),
                   jax.ShapeDtypeStruct((8, LRU_W), F32),
                   jax.ShapeDtypeStruct((LRU_W, LRU_W), F32),
                   jax.ShapeDtypeStruct((LRU_W, LRU_W), F32)],
        input_output_aliases={10: 0},
        scratch_shapes=[pltpu.VMEM((8, LRU_W), F32), pltpu.VMEM((8, LRU_W), F32), pltpu.VMEM((8, LRU_W), F32)],
        compiler_params=_cp(1),
    )(proj, proj, proj, dycat, hstate, hstate, wl, vec, wa, wx, dproj)


_HBM = pl.BlockSpec(memory_space=pltpu.HBM)
_SEM = pl.BlockSpec(memory_space=pltpu.SEMAPHORE)
_ANY = pl.BlockSpec(memory_space=pl.ANY)
_EFFECT = pltpu.SideEffectType.DATAFLOW_SIDE_EFFECTING


def _hbm(a):
    return pltpu.with_memory_space_constraint(a, pltpu.HBM)


_ALL_PEERS = tuple(range(1, N_DEV))
_CHIP_PEERS = (1, 2, 4, 6)
_OTHER_CHIPS = (2, 4, 6)


def _spec_peers(mode):
    return {"ici": _CHIP_PEERS, "fwd": _OTHER_CHIPS}.get(mode, _ALL_PEERS)


def _split_descriptors(copies, srcs, lands, send_sems, recv_sems):
    x, y, c = lax.axis_index("x"), lax.axis_index("y"), lax.axis_index("c")
    me = 4 * x + 2 * y + c
    out, sem = [], 0
    for si, mode, li, ll in copies:
        for k in _spec_peers(mode):
            px = 1 - x if k & 4 else x
            py = 1 - y if k & 2 else y
            pc = 1 - c if k & 1 else c
            peer = 4 * px + 2 * py + pc
            if mode == "fwd":
                src = dst = lands[li].at[peer]
                target = (x, y, 1 - c)
            else:
                src = srcs[si].at[peer] if mode is True else srcs[si]
                dst = lands[li].at[me] if ll is None else lands[li].at[me, ll]
                target = (px, py, pc)
            out.append(pltpu.make_async_remote_copy(
                src_ref=src, dst_ref=dst, send_sem=send_sems.at[sem], recv_sem=recv_sems.at[sem],
                device_id=target, device_id_type=pl.DeviceIdType.MESH))
            sem += 1
    return out


def _n_copies(copies):
    return sum(len(_spec_peers(mode)) for _, mode, _, _ in copies)


def _xchg_start(name, groups):
    n_src = [len(g[0]) for g in groups]
    n_land = [len(g[1]) for g in groups]
    srcs = [s for g in groups for s in g[0]]
    lands = [l for g in groups for l in g[1]]
    ns, nl, ng = len(srcs), len(lands), len(groups)

    def body(*refs):
        src_refs, land_refs = refs[:ns], refs[ns:ns + nl]
        sems = refs[ns + nl:ns + nl + 2 * ng]
        token = refs[-1]
        so = lo = 0
        for gi, (_, _, copies) in enumerate(groups):
            for d in _split_descriptors(copies, src_refs[so:so + n_src[gi]], land_refs[lo:lo + n_land[gi]],
                                        sems[2 * gi], sems[2 * gi + 1]):
                d.start()
            so += n_src[gi]
            lo += n_land[gi]
        token[...] = jnp.zeros(token.shape, F32)

    out_shape, out_specs = [], []
    for g in groups:
        n = _n_copies(g[2])
        out_shape += [pltpu.SemaphoreType.DMA((n,)), pltpu.SemaphoreType.DMA((n,))]
        out_specs += [_SEM, _SEM]
    out_shape += [pltpu.HBM(l.shape, l.dtype) for l in lands]
    out_specs += [_HBM] * nl
    out_shape.append(jax.ShapeDtypeStruct((8, 128), F32))
    out_specs.append(pl.BlockSpec(memory_space=pltpu.VMEM))
    outs = pl.pallas_call(
        body, name=name, in_specs=[_HBM] * (ns + nl), out_specs=out_specs, out_shape=out_shape,
        input_output_aliases={ns + i: 2 * ng + i for i in range(nl)},
        compiler_params=pltpu.CompilerParams(has_side_effects=_EFFECT),
    )(*[_hbm(a) for a in srcs + lands])
    res, lo = [], 2 * ng
    for gi in range(ng):
        res.append((outs[2 * gi], outs[2 * gi + 1], list(outs[lo:lo + n_land[gi]])))
        lo += n_land[gi]
    return res, outs[-1]


def _xchg_wait(name, group, started, after):
    srcs, _, copies = group
    send_sems, recv_sems, lands = started
    ns, nl = len(srcs), len(lands)
    after = list(after)

    def body(*refs):
        src_refs, land_refs = refs[:ns], refs[ns:ns + nl]
        send_ref, recv_ref = refs[ns + nl], refs[ns + nl + 1]
        for d in _split_descriptors(copies, src_refs, land_refs, send_ref, recv_ref):
            d.wait_send()
            d.wait_recv()

    outs = pl.pallas_call(
        body, name=name, in_specs=[_HBM] * (ns + nl) + [_SEM, _SEM] + [_ANY] * len(after),
        out_specs=[_HBM] * nl, out_shape=[pltpu.HBM(l.shape, l.dtype) for l in lands],
        input_output_aliases={ns + i: i for i in range(nl)},
        compiler_params=pltpu.CompilerParams(has_side_effects=_EFFECT),
    )(*[_hbm(a) for a in srcs], *lands, send_sems, recv_sems, *after)
    return list(outs)


def _landing(own, me):
    land = lax.empty((N_DEV,) + own.shape, own.dtype)
    return lax.dynamic_update_slice(land, own[None], (me,) + (0,) * own.ndim)


def _adamw(name, w, m, v, recv, row0=0, prev=None):
    cdim = w.shape[1]
    r = recv.shape[1]
    tr = r
    for cand in (512, 256, 128, 64, 32, 16, 8):
        if r % cand == 0 and r > cand:
            tr = cand
            break
    assert row0 % tr == 0
    blk0 = row0 // tr
    n_prev = 0 if prev is None else 4

    def body(w_ref, m_ref, v_ref, r_ref, *rest):
        g_ref, d_ref, mo_ref, vo_ref = rest[n_prev:]
        g = r_ref[0].astype(F32)
        for s in range(1, N_DEV):
            g = g + r_ref[s].astype(F32)
        mn = ADAM_B1 * m_ref[...] + (1.0 - ADAM_B1) * g
        vn = ADAM_B2 * v_ref[...] + (1.0 - ADAM_B2) * (g * g)
        m_hat = mn / (1.0 - ADAM_B1 ** ADAM_STEP)
        v_hat = vn / (1.0 - ADAM_B2 ** ADAM_STEP)
        g_ref[...] = g
        d_ref[...] = -ADAM_LR * (m_hat / (jnp.sqrt(v_hat) + ADAM_EPS) + ADAM_WD * w_ref[...])
        mo_ref[...] = mn
        vo_ref[...] = vn

    blk = pl.BlockSpec((tr, cdim), lambda i: (i + blk0, 0))
    return pl.pallas_call(
        body, name=name, grid=(r // tr,),
        in_specs=[blk, blk, blk, pl.BlockSpec((N_DEV, tr, cdim), lambda i: (0, i, 0))] + [_ANY] * n_prev,
        out_specs=[blk, blk, blk, blk],
        out_shape=[jax.ShapeDtypeStruct(w.shape, F32)] * 4,
        input_output_aliases={4 + i: i for i in range(n_prev)},
        compiler_params=_cp(1),
    )(w, m, v, recv, *(prev or []))


def _pack_rows(arrs, lead=()):
    n = len(lead)
    flat = jnp.concatenate([a.reshape(a.shape[:n] + (-1,)) for a in arrs], axis=-1)
    size = flat.shape[-1]
    padded = -(-size // PACK_QUANTUM) * PACK_QUANTUM
    flat = jnp.pad(flat, [(0, 0)] * n + [(0, padded - size)])
    return flat.reshape(flat.shape[:n] + (padded // 128, 128))


def _unpack_rows(packed, shapes, lead=()):
    n = len(lead)
    flat = packed.reshape(packed.shape[:n] + (-1,))
    out, off = [], 0
    for s in shapes:
        size = int(np.prod(s))
        out.append(flat[..., off:off + size].reshape(packed.shape[:n] + tuple(s)))
        off += size
    return out


def _block_diag(w):
    eye = jnp.eye(LRU_HEADS, dtype=w.dtype)
    return (eye[:, None, :, None] * w[:, :, None, :]).reshape(LRU_W, LRU_W)


def _diag_blocks(dense):
    t = dense.reshape(LRU_HEADS, 64, LRU_HEADS, 64)
    eye = jnp.eye(LRU_HEADS, dtype=dense.dtype)
    return jnp.sum(t * eye[:, None, :, None], axis=2).reshape(LRU_HEADS * 64, 64)


_W512_NAMES = ("conv_dw_b", "conv_ln_g", "conv_ln_b", "conv_pw_b", "lru_conv_b", "lru_ba", "lru_bx", "lru_lambda")
_W512_ROWS = 12


def _pack_small(d):
    sinks = jnp.pad(d["attn_sinks"], ((0, 0), (0, 512 - N_HEADS)))
    t = jnp.stack([d[n] for n in _W512_NAMES] + [sinks], axis=1)
    w512 = jnp.pad(t, ((0, 0), (0, _W512_ROWS - t.shape[1]), (0, 0))).reshape(DEPTH * _W512_ROWS, 512)
    w2048 = jnp.concatenate([d["ln_in_g"][None], d["ln_in_b"][None], d["ln_post_g"], d["ln_post_b"],
                             jnp.zeros((2, D_MODEL), F32)], axis=0)
    w64 = jnp.concatenate([d["lru_wa"].reshape(-1, 64), d["lru_wx"].reshape(-1, 64)], axis=0)
    return [w512, w2048, w64.reshape(-1, 128)]


def _unpack_small(w512, w2048, w64):
    t = w512.reshape(DEPTH, _W512_ROWS, 512)
    out = {n: t[:, i, :] for i, n in enumerate(_W512_NAMES)}
    out["attn_sinks"] = t[:, len(_W512_NAMES), :N_HEADS]
    out["ln_in_g"], out["ln_in_b"] = w2048[0], w2048[1]
    out["ln_post_g"], out["ln_post_b"] = w2048[2:4], w2048[4:6]
    w64 = w64.reshape(-1, 64)
    half = w64.shape[0] // 2
    out["lru_wa"] = w64[:half].reshape(DEPTH, LRU_HEADS, 64, 64)
    out["lru_wx"] = w64[half:].reshape(DEPTH, LRU_HEADS, 64, 64)
    return out


def _cols_to_slots(full):
    lead = full.shape[:-1]
    t = full.reshape(lead + (N_DEV, full.shape[-1] // N_DEV))
    return jnp.moveaxis(t, -2, 0)


def _slots_to_cols(slots):
    t = jnp.moveaxis(slots, 0, -2)
    return t.reshape(t.shape[:-2] + (t.shape[-2] * t.shape[-1],))


def kernel(x, meta_tokens, ln_in_g, ln_in_b, w_in, conv_dw_w, conv_dw_b, conv_ln_g, conv_ln_b, conv_pw_w, conv_pw_b, attn_sinks, lru_conv_w, lru_conv_b, lru_wa, lru_ba, lru_wx, lru_bx, lru_lambda, w_out, ln_post_g, ln_post_b, loss_target, m_meta_tokens, m_ln_in_g, m_ln_in_b, m_w_in, m_conv_dw_w, m_conv_dw_b, m_conv_ln_g, m_conv_ln_b, m_conv_pw_w, m_conv_pw_b, m_attn_sinks, m_lru_conv_w, m_lru_conv_b, m_lru_wa, m_lru_ba, m_lru_wx, m_lru_bx, m_lru_lambda, m_w_out, m_ln_post_g, m_ln_post_b, v_meta_tokens, v_ln_in_g, v_ln_in_b, v_w_in, v_conv_dw_w, v_conv_dw_b, v_conv_ln_g, v_conv_ln_b, v_conv_pw_w, v_conv_pw_b, v_attn_sinks, v_lru_conv_w, v_lru_conv_b, v_lru_wa, v_lru_ba, v_lru_wx, v_lru_bx, v_lru_lambda, v_w_out, v_ln_post_g, v_ln_post_b):
    seq = x.shape[1]
    lp = seq + TB
    row = lambda a: a.reshape(1, -1)
    rep_names = ["ln_in_g", "ln_in_b", "conv_dw_b", "conv_ln_g", "conv_ln_b", "conv_pw_b", "attn_sinks",
                 "lru_conv_b", "lru_wa", "lru_ba", "lru_wx", "lru_bx", "lru_lambda", "ln_post_g", "ln_post_b"]
    shard_small_names = ["conv_dw_w", "lru_conv_w", "meta_tokens"]
    weights = dict(meta_tokens=meta_tokens, ln_in_g=ln_in_g, ln_in_b=ln_in_b, w_in=w_in, conv_dw_w=conv_dw_w,
                   conv_dw_b=conv_dw_b, conv_ln_g=conv_ln_g, conv_ln_b=conv_ln_b, conv_pw_w=conv_pw_w,
                   conv_pw_b=conv_pw_b, attn_sinks=attn_sinks, lru_conv_w=lru_conv_w, lru_conv_b=lru_conv_b,
                   lru_wa=lru_wa, lru_ba=lru_ba, lru_wx=lru_wx, lru_bx=lru_bx, lru_lambda=lru_lambda,
                   w_out=w_out, ln_post_g=ln_post_g, ln_post_b=ln_post_b)
    mom1 = dict(meta_tokens=m_meta_tokens, ln_in_g=m_ln_in_g, ln_in_b=m_ln_in_b, w_in=m_w_in, conv_dw_w=m_conv_dw_w,
                conv_dw_b=m_conv_dw_b, conv_ln_g=m_conv_ln_g, conv_ln_b=m_conv_ln_b, conv_pw_w=m_conv_pw_w,
                conv_pw_b=m_conv_pw_b, attn_sinks=m_attn_sinks, lru_conv_w=m_lru_conv_w, lru_conv_b=m_lru_conv_b,
                lru_wa=m_lru_wa, lru_ba=m_lru_ba, lru_wx=m_lru_wx, lru_bx=m_lru_bx, lru_lambda=m_lru_lambda,
                w_out=m_w_out, ln_post_g=m_ln_post_g, ln_post_b=m_ln_post_b)
    mom2 = dict(meta_tokens=v_meta_tokens, ln_in_g=v_ln_in_g, ln_in_b=v_ln_in_b, w_in=v_w_in, conv_dw_w=v_conv_dw_w,
                conv_dw_b=v_conv_dw_b, conv_ln_g=v_conv_ln_g, conv_ln_b=v_conv_ln_b, conv_pw_w=v_conv_pw_w,
                conv_pw_b=v_conv_pw_b, attn_sinks=v_attn_sinks, lru_conv_w=v_lru_conv_w, lru_conv_b=v_lru_conv_b,
                lru_wa=v_lru_wa, lru_ba=v_lru_ba, lru_wx=v_lru_wx, lru_bx=v_lru_bx, lru_lambda=v_lru_lambda,
                w_out=v_w_out, ln_post_g=v_ln_post_g, ln_post_b=v_ln_post_b)
    shard_wmv = [_pack_rows([d[n] for n in shard_small_names]) for d in (weights, mom1, mom2)]
    rep_wmv = [_pack_small(d) for d in (weights, mom1, mom2)]
    gate_w = [(_block_diag(lru_wa[l]).astype(BF16), _block_diag(lru_wx[l]).astype(BF16)) for l in range(DEPTH)]
    tabs = _rope_tables(lp)
    prepared = (shard_wmv + [a for wmv in rep_wmv for a in wmv]
                + [w for pair in gate_w for w in pair] + list(tabs))

    small_shard_shapes = [conv_dw_w.shape, lru_conv_w.shape, meta_tokens.shape]
    small_shard = _pack_rows([conv_dw_w, lru_conv_w, meta_tokens])
    me = 4 * lax.axis_index("x") + 2 * lax.axis_index("y") + lax.axis_index("c")
    w_in_b = [w_in[l].astype(BF16) for l in range(DEPTH)]
    w_out_b = [w_out[l].astype(BF16) for l in range(DEPTH)]
    pw_b = conv_pw_w.astype(BF16)
    wgroups = [
        ([small_shard], [_landing(small_shard, me)], [(0, False, 0, None)]),
        ([w_in_b[0]], [_landing(w_in_b[0], me)], [(0, "ici", 0, None)]),
        ([pw_b, w_out_b[0]], [_landing(pw_b, me), _landing(w_out_b[0], me)],
         [(0, False, 0, None), (1, False, 1, None)]),
        ([w_in_b[1], w_out_b[1]], [_landing(w_in_b[1], me), _landing(w_out_b[1], me)],
         [(0, False, 0, None), (1, False, 1, None)]),
    ]
    wstarted, wtoken = _xchg_start("weights_start", wgroups)
    wg_small, = _xchg_wait("weights_wait_s", wgroups[0], wstarted[0], [wtoken])
    g_dw, g_lc, g_meta = _unpack_rows(wg_small, small_shard_shapes, lead=(N_DEV,))
    conv_dw_full = _slots_to_cols(g_dw)
    lru_conv_full = _slots_to_cols(g_lc)
    meta_full = _slots_to_cols(g_meta)
    wg_in = [None, None]
    wg_out = [None, None]
    wg_pw = None

    ln_g = [ln_in_g, ln_post_g[0], ln_post_g[1]]
    ln_b = [ln_in_b, ln_post_b[0], ln_post_b[1]]

    def layer_params(l):
        wdw = jnp.pad(conv_dw_full[l], ((0, 1), (0, 0)))
        cvec = jnp.pad(jnp.stack([conv_dw_b[l], conv_ln_g[l], conv_ln_b[l], conv_pw_b[l]]), ((0, 4), (0, 0)))
        wpw = wg_pw[:, l].reshape(CONV_W, CONV_W)
        sinks = jnp.pad(attn_sinks[l].reshape(1, N_HEADS), ((0, 7), (0, 128 - N_HEADS)))
        wl = jnp.pad(lru_conv_full[l], ((0, 4), (0, 0)))
        lvec = jnp.pad(jnp.stack([lru_conv_b[l], lru_ba[l], lru_bx[l], lru_lambda[l]]), ((0, 4), (0, 0)))
        wa, wx = gate_w[l]
        wo = wg_out[l].reshape(D_MODEL, D_MODEL)
        wout = jnp.concatenate([wo[512:1536], wo[0:512], wo[1536:]], axis=0)
        return dict(wdw=wdw, cvec=cvec, wpw=wpw, sinks=sinks, wl=wl, lvec=lvec, wa=wa, wx=wx, wout=wout)

    params = [None] * DEPTH

    z0, hb = _embed(x, meta_full, row(ln_g[0]), row(ln_b[0]))
    z = [z0]
    saved = []
    for l in range(DEPTH):
        if l == 0:
            part, = _xchg_wait("weights_wait_a", wgroups[1], wstarted[1], [hb] + prepared)
            fwd = ([], [part], [(None, "fwd", 0, None)])
            fstarted, ftoken = _xchg_start("weights_fwd_start", [fwd])
            wg_in[0], = _xchg_wait("weights_fwd_wait", fwd, fstarted[0], [ftoken])
        else:
            wg_in[1], wg_out[1] = _xchg_wait("weights_wait_c", wgroups[3], wstarted[3], [hb])
        proj = _mm_proj(f"proj{l}", hb, wg_in[l])
        if l == 0:
            wg_pw, wg_out[0] = _xchg_wait("weights_wait_b", wgroups[2], wstarted[2], [proj])
        p = params[l] = layer_params(l)
        ycat, c1 = _conv_fwd(f"conv_fwd{l}", proj, p["wdw"], p["cvec"], p["wpw"])
        qr, kr = _rope_fwd(f"rope{l}", proj, tabs)
        ycat = _attn_fwd(f"attn_fwd{l}", qr, kr, proj, p["sinks"], ycat)
        ycat, hstate = _lru_fwd(f"lru_fwd{l}", proj, p["wl"], p["lvec"], p["wa"], p["wx"], ycat)
        saved.append(dict(hb=hb, proj=proj, ycat=ycat, qr=qr, kr=kr, hstate=hstate, c1=c1))
        z_next, hb = _mm_out(f"out{l}", ycat, p["wout"], z[l], row(ln_g[l]), row(ln_b[l]),
                             row(ln_g[l + 1]), row(ln_b[l + 1]))
        z.append(z_next)

    dz, st_post1, loss_blk = _loss_head(z[DEPTH], loss_target, row(ln_g[DEPTH]), row(ln_b[DEPTH]))
    loss = lax.psum(loss_blk[0, 0], ("x", "y", "c"))

    ln_stats = {DEPTH: st_post1}
    g_layers = [None] * DEPTH
    dwin_l, dwout_l = [None] * DEPTH, [None] * DEPTH
    grad_x = gmeta = None
    token = wtoken
    ggroups = [None] * DEPTH
    own = lambda a: lax.dynamic_index_in_dim(a, me, 0, keepdims=False)
    for l in reversed(range(DEPTH)):
        p, s = params[l], saved[l]
        dycat = _mm_dycat(f"dycat{l}", dz, p["wout"], token)
        dwout_l[l] = _mm_dwout(f"dwout{l}", s["ycat"], dz)
        dproj, dwdw, dcvec, dwpw = _conv_bwd(f"conv_bwd{l}", s["proj"], dycat, s["c1"], p["wdw"], p["cvec"], p["wpw"])
        dwo = jnp.concatenate([dwout_l[l][1024:1536], dwout_l[l][0:1024], dwout_l[l][1536:]], axis=0)
        dwo = dwo.reshape(N_DEV, D_MODEL // N_DEV, D_MODEL)
        dpw = dwpw.reshape(N_DEV, CONV_W // N_DEV, CONV_W)
        early = ([dwo, dpw], [_landing(own(dwo), me), _landing(own(dpw), me)],
                 [(0, True, 0, None), (1, True, 1, None)])
        started_early, token = _xchg_start(f"grads_start_out{l}", [early])
        dq, dgate, dk, dv, dsink = _attn_bwd(f"attn_bwd{l}", s["qr"], s["kr"], s["proj"], p["sinks"], dycat, token)
        dproj = _attn_assemble(f"attn_asm{l}", dq, dgate, dk, dv, tabs, dproj)
        dproj, dwl, dlvec, dwa, dwx = _lru_bwd(f"lru_bwd{l}", s["proj"], dycat, s["hstate"],
                                                p["wl"], p["lvec"], p["wa"], p["wx"], dproj)
        g512 = jnp.concatenate([dcvec[0:4], dlvec[0:4], jnp.pad(dsink[0:1], ((0, 0), (0, 512 - 128))),
                                jnp.zeros((_W512_ROWS - 9, 512), F32)], axis=0)
        g_layers[l] = dict(dwdw=dwdw[:CONV_K], dwl=dwl[:LRU_CONV_K], g512=g512,
                           dwa=_diag_blocks(dwa), dwx=_diag_blocks(dwx))
        if l == 0:
            g512 = jnp.concatenate([g_layers[i]["g512"] for i in range(DEPTH)], axis=0)
            g64 = jnp.concatenate([g_layers[i][k] for k in ("dwa", "dwx") for i in range(DEPTH)], axis=0)
            g64 = g64.reshape(-1, 128)
            vgroup = ([g512, g64], [_landing(g512, me), _landing(g64, me)],
                      [(0, False, 0, None), (1, False, 1, None)])
            vstarted, token = _xchg_start("vector_grads_start", [vgroup])
        dwin_l[l] = _mm_dwin(f"dwin{l}", s["hb"], dproj, token)
        late = ([dwin_l[l]], [_landing(own(dwin_l[l]), me)], [(0, True, 0, None)])
        started_late, token = _xchg_start(f"grads_start_in{l}", [late])
        ggroups[l] = [(late, started_late[0]), (early, started_early[0])]
        dh = _mm_dh(f"dh{l}", dproj, wg_in[l], dz, token)
        if l > 0:
            dz, ln_stats[l] = _ln_bwd(f"ln_bwd{l}", dh, z[l], row(ln_g[l]))
        else:
            grad_x, gmeta, ln_stats[0] = _ln_bwd_input(dh, z[0], row(ln_g[0]))

    g2048 = jnp.concatenate([ln_stats[0][0:2], ln_stats[1][0:1], ln_stats[2][0:1], ln_stats[1][1:2],
                             ln_stats[2][1:2], jnp.zeros((2, D_MODEL), F32)], axis=0)
    g_dw_full = jnp.stack([g_layers[l]["dwdw"] for l in range(DEPTH)])
    g_lc_full = jnp.stack([g_layers[l]["dwl"] for l in range(DEPTH)])
    shard_pack = _pack_rows([_cols_to_slots(g_dw_full), _cols_to_slots(g_lc_full), _cols_to_slots(gmeta)],
                            lead=(N_DEV,))
    sgroup = ([shard_pack, g2048], [_landing(own(shard_pack), me), _landing(g2048, me)],
              [(0, True, 0, None), (1, False, 1, None)])
    sstarted, token = _xchg_start("small_grads_start", [sgroup])

    res = {}

    def flat2(a, cols):
        return a.reshape(-1, cols)

    big = (("w_in", 0, W_IN_SHARD), ("w_out", 1, D_MODEL), ("conv_pw_w", 2, CONV_W))
    prev = {n: None for n, _, _ in big}
    after = token
    for l in reversed(range(DEPTH)):
        recvs = []
        for gi, (grp, started) in enumerate(ggroups[l]):
            recvs += _xchg_wait(f"grads_wait{l}_{gi}", grp, started, [after])
        for name_, gi, cols in big:
            w_ = weights[name_]
            rows = w_.shape[1]
            prev[name_] = _adamw(f"adamw_{name_}{l}", flat2(w_, cols), flat2(mom1[name_], cols),
                                 flat2(mom2[name_], cols), recvs[gi], row0=l * rows, prev=prev[name_])
        after = prev["w_in"][0]
    for name_, _, _ in big:
        res[name_] = [o.reshape(weights[name_].shape) for o in prev[name_]]

    big_done = [prev[n][0] for n, _, _ in big]
    r_512, r_64 = _xchg_wait("vector_grads_wait", vgroup, vstarted[0], big_done)
    r_small, r_2048 = _xchg_wait("small_grads_wait", sgroup, sstarted[0], big_done)
    r_rep = [r_512, r_2048, r_64]
    sshapes = [weights[n].shape for n in shard_small_names]
    outs = _adamw("adamw_small_sharded", *shard_wmv, r_small)
    for k, o in enumerate(outs):
        for n, a in zip(shard_small_names, _unpack_rows(o, sshapes)):
            res.setdefault(n, [None] * 4)[k] = a

    outs = [_adamw(f"adamw_small_w{tag}", rep_wmv[0][ci], rep_wmv[1][ci], rep_wmv[2][ci], r_rep[ci])
            for ci, tag in enumerate(("512", "2048", "64"))]
    for k in range(4):
        for n, a in _unpack_small(outs[0][k], outs[1][k], outs[2][k]).items():
            res.setdefault(n, [None] * 4)[k] = a

    order = ["meta_tokens", "ln_in_g", "ln_in_b", "w_in", "conv_dw_w", "conv_dw_b", "conv_ln_g", "conv_ln_b",
             "conv_pw_w", "conv_pw_b", "attn_sinks", "lru_conv_w", "lru_conv_b", "lru_wa", "lru_ba", "lru_wx",
             "lru_bx", "lru_lambda", "w_out", "ln_post_g", "ln_post_b"]
    return (loss, grad_x,
            *[res[n][0] for n in order], *[res[n][1] for n in order],
            *[res[n][2] for n in order], *[res[n][3] for n in order])
```

```python
import functools
import math

import numpy as np
import jax
import jax.numpy as jnp
from jax import lax
from jax.experimental import pallas as pl
from jax.experimental.pallas import tpu as pltpu

F32 = jnp.float32
BF16 = jnp.bfloat16

D_MODEL = 2048
DEPTH = 2
N_META = 16
TB = 128
PAD0 = TB - N_META
CONV_W = 512
CONV_K = 31
HEAD_DIM = 64
N_HEADS = 16
N_KV = 4
GROUP = 4
ATT_W = 1024
KV_W = 256
ROT_DIM = 16
ROPE_THETA = 500000.0
LRU_W = 512
LRU_HEADS = 8
LRU_CONV_K = 4
LRU_C = 8.0
IN_TOTAL = 5120
N_DEV = 8
W_IN_SHARD = IN_TOTAL // N_DEV
LN_EPS = 1e-5
ALPHA = (2.0 * DEPTH) ** 0.25
NEG_INF = -1e30
ATT_SCALE = HEAD_DIM ** -0.5

ADAM_LR = 0.001
ADAM_B1 = 0.9
ADAM_B2 = 0.999
ADAM_EPS = 1e-08
ADAM_WD = 0.01
ADAM_STEP = 10

VMEM_LIMIT = 56 * 1024 * 1024
PACK_QUANTUM = 256 * 128

COL_CV, COL_CG, COL_CGATE = 0, 1, 2
COL_Q0 = 3
COL_K256 = 10
COL_V256 = 11
COL_AGATE1024 = 3
COL_RX, COL_RGATE = 8, 9
YC_CONV, YC_LRU = 2, 3


def _cp(n_axes, vmem=VMEM_LIMIT):
    return pltpu.CompilerParams(dimension_semantics=("arbitrary",) * n_axes, vmem_limit_bytes=vmem)


def _row_tile(lp, max_blocks):
    nb = lp // TB
    d = max(k for k in range(1, max_blocks + 1) if nb % k == 0)
    return TB * d


def _sig(x):
    return jax.nn.sigmoid(x)


def _dsilu(x, s):
    return s * (1.0 + x * (1.0 - s))


def _ln_core(z):
    mu = jnp.mean(z, axis=-1, keepdims=True)
    zc = z - mu
    var = jnp.mean(zc * zc, axis=-1, keepdims=True)
    rstd = lax.rsqrt(var + LN_EPS)
    return zc * rstd, rstd


def _ln_bwd_core(dy, xh, rstd, g):
    dxh = dy * g
    m1 = jnp.mean(dxh, axis=-1, keepdims=True)
    m2 = jnp.mean(dxh * xh, axis=-1, keepdims=True)
    return rstd * (dxh - m1 - xh * m2)


def _row_ids(shape, base):
    return lax.broadcasted_iota(jnp.int32, shape, 0) + base


def _colsum(x):
    return jnp.sum(x, axis=0, keepdims=True)


def _dot(a, b, dims):
    return lax.dot_general(a, b, (dims, ((), ())), preferred_element_type=F32)


NN = ((1,), (0,))
NT = ((1,), (1,))
TN = ((0,), (0,))


def _embed(x, meta_full, g, b):
    s = x.shape[1]
    lp = s + TB
    nb = lp // TB

    def body(x_ref, m_ref, g_ref, b_ref, o_ref, hb_ref):
        i = pl.program_id(0)

        @pl.when(i == 0)
        def _():
            o_ref[0:PAD0, :] = jnp.zeros((PAD0, D_MODEL), F32)
            o_ref[PAD0:TB, :] = m_ref[...]

        @pl.when(i > 0)
        def _():
            o_ref[...] = x_ref[...]

        xh, _ = _ln_core(o_ref[...])
        h = xh * g_ref[...] + b_ref[...]
        rows = _row_ids(h.shape, i * TB)
        hb_ref[...] = jnp.where(rows >= PAD0, h, 0.0).astype(BF16)

    return pl.pallas_call(
        body, name="embed", grid=(nb,),
        in_specs=[pl.BlockSpec((None, TB, D_MODEL), lambda i: (0, jnp.maximum(i - 1, 0), 0)),
                  pl.BlockSpec((N_META, D_MODEL), lambda i: (0, 0)),
                  pl.BlockSpec((1, D_MODEL), lambda i: (0, 0)),
                  pl.BlockSpec((1, D_MODEL), lambda i: (0, 0))],
        out_specs=[pl.BlockSpec((TB, D_MODEL), lambda i: (i, 0)),
                   pl.BlockSpec((TB, D_MODEL), lambda i: (i, 0))],
        out_shape=[jax.ShapeDtypeStruct((lp, D_MODEL), F32),
                   jax.ShapeDtypeStruct((lp, D_MODEL), BF16)],
        compiler_params=_cp(1),
    )(x, meta_full, g, b)


def _loss_head(z, target, g, b):
    lp = z.shape[0]
    nb = lp // TB

    def body(z_ref, t_ref, g_ref, b_ref, dz_ref, st_ref, loss_ref):
        i = pl.program_id(0)

        @pl.when(i == 0)
        def _():
            st_ref[...] = jnp.zeros(st_ref.shape, F32)
            loss_ref[...] = jnp.zeros(loss_ref.shape, F32)
            dz_ref[...] = jnp.zeros(dz_ref.shape, F32)

        @pl.when(i > 0)
        def _():
            xh, rstd = _ln_core(z_ref[...])
            gg = g_ref[...]
            y = xh * gg + b_ref[...]
            e = y - t_ref[...]
            part = 0.5 * jnp.sum(jnp.mean(e * e, axis=-1, keepdims=True), axis=0, keepdims=True)
            loss_ref[...] += jnp.broadcast_to(part, loss_ref.shape)
            dy = e / float(D_MODEL)
            st_ref[0:1, :] += _colsum(dy * xh)
            st_ref[1:2, :] += _colsum(dy)
            dz_ref[...] = _ln_bwd_core(dy, xh, rstd, gg)

    return pl.pallas_call(
        body, name="loss_head", grid=(nb,),
        in_specs=[pl.BlockSpec((TB, D_MODEL), lambda i: (i, 0)),
                  pl.BlockSpec((None, TB, D_MODEL), lambda i: (0, jnp.maximum(i - 1, 0), 0)),
                  pl.BlockSpec((1, D_MODEL), lambda i: (0, 0)),
                  pl.BlockSpec((1, D_MODEL), lambda i: (0, 0))],
        out_specs=[pl.BlockSpec((TB, D_MODEL), lambda i: (i, 0)),
                   pl.BlockSpec((8, D_MODEL), lambda i: (0, 0)),
                   pl.BlockSpec((8, 128), lambda i: (0, 0))],
        out_shape=[jax.ShapeDtypeStruct((lp, D_MODEL), F32),
                   jax.ShapeDtypeStruct((8, D_MODEL), F32),
                   jax.ShapeDtypeStruct((8, 128), F32)],
        compiler_params=_cp(1),
    )(z, target, g, b)


def _ln_bwd(name, dh, z, g):
    lp = z.shape[0]
    tr = _row_tile(lp, 3)

    def body(dh_ref, z_ref, g_ref, dz_ref, st_ref):
        i = pl.program_id(0)

        @pl.when(i == 0)
        def _():
            st_ref[...] = jnp.zeros(st_ref.shape, F32)

        xh, rstd = _ln_core(z_ref[...])
        rows = _row_ids(xh.shape, i * tr)
        dy = jnp.where(rows >= PAD0, dh_ref[...], 0.0)
        st_ref[0:1, :] += _colsum(dy * xh)
        st_ref[1:2, :] += _colsum(dy)
        dz_ref[...] = _ln_bwd_core(dy, xh, rstd, g_ref[...])

    return pl.pallas_call(
        body, name=name, grid=(lp // tr,),
        in_specs=[pl.BlockSpec((tr, D_MODEL), lambda i: (i, 0)),
                  pl.BlockSpec((tr, D_MODEL), lambda i: (i, 0)),
                  pl.BlockSpec((1, D_MODEL), lambda i: (0, 0))],
        out_specs=[pl.BlockSpec((tr, D_MODEL), lambda i: (i, 0)),
                   pl.BlockSpec((8, D_MODEL), lambda i: (0, 0))],
        out_shape=[jax.ShapeDtypeStruct((lp, D_MODEL), F32),
                   jax.ShapeDtypeStruct((8, D_MODEL), F32)],
        compiler_params=_cp(1),
    )(dh, z, g)


def _ln_bwd_input(dh, z, g):
    lp = z.shape[0]
    nb = lp // TB
    s = lp - TB

    def body(dh_ref, z_ref, g_ref, gx_ref, gm_ref, st_ref):
        i = pl.program_id(0)

        @pl.when(i == 0)
        def _():
            st_ref[...] = jnp.zeros(st_ref.shape, F32)

        xh, rstd = _ln_core(z_ref[...])
        rows = _row_ids(xh.shape, i * TB)
        dy = jnp.where(rows >= PAD0, dh_ref[...], 0.0)
        st_ref[0:1, :] += _colsum(dy * xh)
        st_ref[1:2, :] += _colsum(dy)
        dz = _ln_bwd_core(dy, xh, rstd, g_ref[...])
        gx_ref[...] = dz

        @pl.when(i == 0)
        def _():
            gm_ref[...] = dz[PAD0:TB, :]

    return pl.pallas_call(
        body, name="ln_in_bwd", grid=(nb,),
        in_specs=[pl.BlockSpec((TB, D_MODEL), lambda i: (i, 0)),
                  pl.BlockSpec((TB, D_MODEL), lambda i: (i, 0)),
                  pl.BlockSpec((1, D_MODEL), lambda i: (0, 0))],
        out_specs=[pl.BlockSpec((None, TB, D_MODEL), lambda i: (0, jnp.maximum(i - 1, 0), 0)),
                   pl.BlockSpec((N_META, D_MODEL), lambda i: (0, 0)),
                   pl.BlockSpec((8, D_MODEL), lambda i: (0, 0))],
        out_shape=[jax.ShapeDtypeStruct((1, s, D_MODEL), F32),
                   jax.ShapeDtypeStruct((N_META, D_MODEL), F32),
                   jax.ShapeDtypeStruct((8, D_MODEL), F32)],
        compiler_params=_cp(1),
    )(dh, z, g)


def _mm_proj(name, hb, wg_in):
    lp = hb.shape[0]
    tm = lp // 3

    def body(a_ref, b_ref, o_ref):
        b = jnp.concatenate([b_ref[0], b_ref[1]], axis=1)
        o_ref[...] = _dot(a_ref[...], b, NN)

    return pl.pallas_call(
        body, name=name, grid=(3, N_DEV // 2),
        in_specs=[pl.BlockSpec((tm, D_MODEL), lambda i, j: (i, 0)),
                  pl.BlockSpec((2, D_MODEL, W_IN_SHARD), lambda i, j: (j, 0, 0))],
        out_specs=pl.BlockSpec((tm, 2 * W_IN_SHARD), lambda i, j: (i, j)),
        out_shape=jax.ShapeDtypeStruct((lp, IN_TOTAL), F32),
        compiler_params=_cp(2),
    )(hb, wg_in)


def _mm_out(name, ycat, wout, z, g, b, g2, b2):
    lp = ycat.shape[0]
    tm = lp // 6

    def body(a_ref, w_ref, z_ref, g_ref, b_ref, g2_ref, b2_ref, o_ref, hb_ref):
        i = pl.program_id(0)
        xh, _ = _ln_core(z_ref[...])
        h = xh * g_ref[...] + b_ref[...]
        live = _row_ids(h.shape, i * tm) >= PAD0
        h = jnp.where(live, h, 0.0)
        zn = ALPHA * h + _dot(a_ref[...], w_ref[...], NN)
        o_ref[...] = zn
        xh2, _ = _ln_core(zn)
        hb_ref[...] = jnp.where(live, xh2 * g2_ref[...] + b2_ref[...], 0.0).astype(BF16)

    vec = pl.BlockSpec((1, D_MODEL), lambda i: (0, 0))
    return pl.pallas_call(
        body, name=name, grid=(6,),
        in_specs=[pl.BlockSpec((tm, D_MODEL), lambda i: (i, 0)),
                  pl.BlockSpec((D_MODEL, D_MODEL), lambda i: (0, 0), pipeline_mode=pl.Buffered(1)),
                  pl.BlockSpec((tm, D_MODEL), lambda i: (i, 0)),
                  vec, vec, vec, vec],
        out_specs=[pl.BlockSpec((tm, D_MODEL), lambda i: (i, 0)),
                   pl.BlockSpec((tm, D_MODEL), lambda i: (i, 0))],
        out_shape=[jax.ShapeDtypeStruct((lp, D_MODEL), F32),
                   jax.ShapeDtypeStruct((lp, D_MODEL), BF16)],
        compiler_params=_cp(1),
    )(ycat, wout, z, g, b, g2, b2)


def _mm_dycat(name, dz, wout, dep):
    lp = dz.shape[0]
    tm = lp // 6

    def body(a_ref, w_ref, dep_ref, o_ref):
        del dep_ref
        o_ref[...] = _dot(a_ref[...].astype(BF16), w_ref[...], NT).astype(BF16)

    return pl.pallas_call(
        body, name=name, grid=(6,),
        in_specs=[pl.BlockSpec((tm, D_MODEL), lambda i: (i, 0)),
                  pl.BlockSpec((D_MODEL, D_MODEL), lambda i: (0, 0), pipeline_mode=pl.Buffered(1)),
                  pl.BlockSpec(memory_space=pl.ANY)],
        out_specs=pl.BlockSpec((tm, D_MODEL), lambda i: (i, 0)),
        out_shape=jax.ShapeDtypeStruct((lp, D_MODEL), BF16),
        compiler_params=_cp(1),
    )(dz, wout, dep)


def _mm_dwout(name, ycat, dz):
    lp = ycat.shape[0]
    tk = _row_tile(lp, 11)
    nk = lp // tk
    half = D_MODEL // 2

    def body(a_ref, b_ref, o_ref, acc_ref):
        k = pl.program_id(1)

        @pl.when(k == 0)
        def _():
            acc_ref[...] = jnp.zeros(acc_ref.shape, F32)

        acc_ref[...] += _dot(a_ref[...], b_ref[...].astype(BF16), TN)

        @pl.when(k == nk - 1)
        def _():
            o_ref[...] = acc_ref[...].astype(BF16)

    return pl.pallas_call(
        body, name=name, grid=(2, nk),
        in_specs=[pl.BlockSpec((tk, half), lambda h, k: (k, h)),
                  pl.BlockSpec((tk, D_MODEL), lambda h, k: (k, 0))],
        out_specs=pl.BlockSpec((half, D_MODEL), lambda h, k: (h, 0)),
        out_shape=jax.ShapeDtypeStruct((D_MODEL, D_MODEL), BF16),
        scratch_shapes=[pltpu.VMEM((half, D_MODEL), F32)],
        compiler_params=_cp(2),
    )(ycat, dz)


def _mm_dwin(name, hb, dproj, dep):
    lp = hb.shape[0]
    tk = _row_tile(lp, 11)
    nk = lp // tk

    def body(a_ref, b_ref, dep_ref, o_ref, acc_ref):
        del dep_ref
        k = pl.program_id(1)

        @pl.when(k == 0)
        def _():
            acc_ref[...] = jnp.zeros(acc_ref.shape, F32)

        acc_ref[...] += _dot(a_ref[...], b_ref[...], TN)

        @pl.when(k == nk - 1)
        def _():
            o_ref[0] = acc_ref[:, 0:W_IN_SHARD].astype(BF16)
            o_ref[1] = acc_ref[:, W_IN_SHARD:2 * W_IN_SHARD].astype(BF16)

    return pl.pallas_call(
        body, name=name, grid=(4, nk),
        in_specs=[pl.BlockSpec((tk, D_MODEL), lambda j, k: (k, 0)),
                  pl.BlockSpec((tk, 2 * W_IN_SHARD), lambda j, k: (k, j)),
                  pl.BlockSpec(memory_space=pl.ANY)],
        out_specs=pl.BlockSpec((2, D_MODEL, W_IN_SHARD), lambda j, k: (j, 0, 0)),
        out_shape=jax.ShapeDtypeStruct((N_DEV, D_MODEL, W_IN_SHARD), BF16),
        scratch_shapes=[pltpu.VMEM((D_MODEL, 2 * W_IN_SHARD), F32)],
        compiler_params=_cp(2),
    )(hb, dproj, dep)


def _mm_dh(name, dproj, wg_in, dz, dep):
    lp = dproj.shape[0]
    tm = lp // 6

    def body(a_ref, w_ref, dz_ref, dep_ref, o_ref, acc_ref):
        del dep_ref
        k = pl.program_id(1)

        @pl.when(k == 0)
        def _():
            acc_ref[...] = jnp.zeros(acc_ref.shape, F32)

        w = jnp.concatenate([w_ref[0], w_ref[1]], axis=1)
        acc_ref[...] += _dot(a_ref[...], w, NT)

        @pl.when(k == N_DEV // 2 - 1)
        def _():
            o_ref[...] = acc_ref[...] + ALPHA * dz_ref[...]

    return pl.pallas_call(
        body, name=name, grid=(6, N_DEV // 2),
        in_specs=[pl.BlockSpec((tm, 2 * W_IN_SHARD), lambda i, k: (i, k)),
                  pl.BlockSpec((2, D_MODEL, W_IN_SHARD), lambda i, k: (k, 0, 0)),
                  pl.BlockSpec((tm, D_MODEL), lambda i, k: (i, 0)),
                  pl.BlockSpec(memory_space=pl.ANY)],
        out_specs=pl.BlockSpec((tm, D_MODEL), lambda i, k: (i, 0)),
        out_shape=jax.ShapeDtypeStruct((lp, D_MODEL), F32),
        scratch_shapes=[pltpu.VMEM((tm, D_MODEL), F32)],
        compiler_params=_cp(2),
    )(dproj, wg_in, dz, dep)


SUB = 128


def _shift_plan(cat, n_shift, base):
    rolled = [cat] + [pltpu.roll(cat, b, axis=0) for b in range(1, 8)]
    return [(rolled[s % 8], base - 8 * (s // 8)) for s in range(n_shift)]


def _tap_sum(w_ref, plan, rows, init=None):
    blocks = []
    for r0 in range(0, rows, SUB):
        row = []
        for c0 in range(0, CONV_W, SUB):
            acc = (jnp.zeros((SUB, SUB), F32) if init is None
                   else jnp.broadcast_to(init[:, c0:c0 + SUB], (SUB, SUB)))
            for k, (arr, off) in enumerate(plan):
                acc = acc + w_ref[k:k + 1, c0:c0 + SUB] * arr[off + r0:off + r0 + SUB, c0:c0 + SUB]
            row.append(acc)
        blocks.append(jnp.concatenate(row, axis=1))
    return jnp.concatenate(blocks, axis=0)


def _tap_grads(dw_ref, dy, plan, rows):
    for c0 in range(0, CONV_W, SUB):
        dys = [dy[r0:r0 + SUB, c0:c0 + SUB] for r0 in range(0, rows, SUB)]
        for k, (arr, off) in enumerate(plan):
            part = None
            for ri, r0 in enumerate(range(0, rows, SUB)):
                prod = dys[ri] * arr[off + r0:off + r0 + SUB, c0:c0 + SUB]
                for i in range(SUB // 8):
                    piece = prod[8 * i:8 * i + 8, :]
                    part = piece if part is None else part + piece
            dw_ref[k:k + 1, c0:c0 + SUB] += jnp.sum(part, axis=0, keepdims=True)


CONV_HALO = 32


def _conv_chain(j, tb, cv_ref, cg_ref, cvp_ref, cgp_ref, wdw_ref, vec_ref, wpw_ref, c1_ref=None):
    cv = cv_ref[...]
    sg = _sig(cg_ref[...])
    c0 = cv * sg
    c0p = jnp.where(j > 0, cvp_ref[...] * _sig(cgp_ref[...]), 0.0)
    cat = jnp.concatenate([c0p, c0], axis=0)
    shifts = _shift_plan(cat, CONV_K, CONV_HALO)
    taps = [shifts[CONV_K - 1 - k] for k in range(CONV_K)]
    if c1_ref is None:
        c1 = _tap_sum(wdw_ref, taps, tb, init=vec_ref[0:1, :])
    else:
        c1 = c1_ref[...]
    xh, rstd = _ln_core(c1)
    c2 = xh * vec_ref[1:2, :] + vec_ref[2:3, :]
    s2 = _sig(c2)
    c3 = c2 * s2
    c4 = _dot(c3.astype(BF16), wpw_ref[...], NN) + vec_ref[3:4, :]
    return dict(cv=cv, sg=sg, taps=taps, c1=c1, xh=xh, rstd=rstd, c2=c2, s2=s2, c3=c3, c4=c4)


def _conv_in_specs(jmap, tb):
    def cur(col):
        return pl.BlockSpec((tb, 512), lambda n: (jmap(n), col))

    def prev(col):
        return pl.BlockSpec((CONV_HALO, 512),
                            lambda n: (jnp.maximum(jmap(n) * (tb // CONV_HALO) - 1, 0), col))

    return [cur(COL_CV), cur(COL_CG), prev(COL_CV), prev(COL_CG), cur(COL_CGATE)]


def _conv_param_specs():
    return [pl.BlockSpec((32, CONV_W), lambda n: (0, 0)),
            pl.BlockSpec((8, CONV_W), lambda n: (0, 0)),
            pl.BlockSpec((CONV_W, CONV_W), lambda n: (0, 0))]


def _conv_fwd(name, proj, wdw, vec, wpw):
    lp = proj.shape[0]
    tb = _row_tile(lp, 3)
    nb = lp // tb

    def body(cv_ref, cg_ref, cvp_ref, cgp_ref, gate_ref, wdw_ref, vec_ref, wpw_ref, o_ref, c1_ref):
        j = pl.program_id(0)
        c = _conv_chain(j, tb, cv_ref, cg_ref, cvp_ref, cgp_ref, wdw_ref, vec_ref, wpw_ref)
        gate = gate_ref[...]
        o_ref[...] = (c["c4"] * (gate * _sig(gate))).astype(BF16)
        c1_ref[...] = c["c1"]

    return pl.pallas_call(
        body, name=name, grid=(nb,),
        in_specs=_conv_in_specs(lambda n: n, tb) + _conv_param_specs(),
        out_specs=[pl.BlockSpec((tb, 512), lambda n: (n, YC_CONV)),
                   pl.BlockSpec((tb, CONV_W), lambda n: (n, 0))],
        out_shape=[jax.ShapeDtypeStruct((lp, D_MODEL), BF16),
                   jax.ShapeDtypeStruct((lp, CONV_W), F32)],
        compiler_params=_cp(1),
    )(proj, proj, proj, proj, proj, wdw, vec, wpw)


def _conv_bwd(name, proj, dycat, c1, wdw, vec, wpw):
    lp = proj.shape[0]
    tb = _row_tile(lp, 3)
    nb = lp // tb
    halo = CONV_HALO

    def body(cv_ref, cg_ref, cvp_ref, cgp_ref, gate_ref, dy_ref, c1_ref, wdw_ref, vec_ref, wpw_ref,
             dp_ref, dwdw_ref, dvec_ref, dwpw_ref, carry_ref):
        n = pl.program_id(0)
        j = nb - 1 - n

        @pl.when(n == 0)
        def _():
            carry_ref[...] = jnp.zeros(carry_ref.shape, F32)
            dwdw_ref[...] = jnp.zeros(dwdw_ref.shape, F32)
            dvec_ref[...] = jnp.zeros(dvec_ref.shape, F32)
            dwpw_ref[...] = jnp.zeros(dwpw_ref.shape, F32)

        c = _conv_chain(j, tb, cv_ref, cg_ref, cvp_ref, cgp_ref, wdw_ref, vec_ref, wpw_ref, c1_ref)
        dy = dy_ref[...].astype(F32)
        gate = gate_ref[...]
        sgate = _sig(gate)
        dc4 = dy * (gate * sgate)
        dgate = dy * c["c4"] * _dsilu(gate, sgate)
        dc4b = dc4.astype(BF16)
        dvec_ref[3:4, :] += _colsum(dc4)
        dwpw_ref[...] += _dot(c["c3"].astype(BF16), dc4b, TN)
        dc3 = _dot(dc4b, wpw_ref[...], NT)
        dc2 = dc3 * _dsilu(c["c2"], c["s2"])
        dvec_ref[1:2, :] += _colsum(dc2 * c["xh"])
        dvec_ref[2:3, :] += _colsum(dc2)
        dc1 = _ln_bwd_core(dc2, c["xh"], c["rstd"], vec_ref[1:2, :])
        dvec_ref[0:1, :] += _colsum(dc1)
        _tap_grads(dwdw_ref, dc1, c["taps"], tb)
        dcat = jnp.concatenate([dc1, carry_ref[...]], axis=0)
        total = tb + halo
        up = [dcat] + [pltpu.roll(dcat, total - b, axis=0) for b in range(1, 8)]
        ahead = [(up[(CONV_K - 1 - k) % 8], 8 * ((CONV_K - 1 - k) // 8)) for k in range(CONV_K)]
        dc0 = _tap_sum(wdw_ref, ahead, tb)
        carry_ref[...] = dc1[0:halo, :]
        sg = c["sg"]
        dcv = dc0 * sg
        dcg = dc0 * c["cv"] * sg * (1.0 - sg)
        dp_ref[:, 0:512] = dcv.astype(BF16)
        dp_ref[:, 512:1024] = dcg.astype(BF16)
        dp_ref[:, 1024:1536] = dgate.astype(BF16)

    jmap = lambda n: nb - 1 - n
    return pl.pallas_call(
        body, name=name, grid=(nb,),
        in_specs=(_conv_in_specs(jmap, tb)
                  + [pl.BlockSpec((tb, 512), lambda n: (jmap(n), YC_CONV)),
                     pl.BlockSpec((tb, CONV_W), lambda n: (jmap(n), 0))]
                  + _conv_param_specs()),
        out_specs=[pl.BlockSpec((tb, 1536), lambda n: (jmap(n), 0)),
                   pl.BlockSpec((32, CONV_W), lambda n: (0, 0)),
                   pl.BlockSpec((8, CONV_W), lambda n: (0, 0)),
                   pl.BlockSpec((CONV_W, CONV_W), lambda n: (0, 0))],
        out_shape=[jax.ShapeDtypeStruct((lp, IN_TOTAL), BF16),
                   jax.ShapeDtypeStruct((32, CONV_W), F32),
                   jax.ShapeDtypeStruct((8, CONV_W), F32),
                   jax.ShapeDtypeStruct((CONV_W, CONV_W), F32)],
        scratch_shapes=[pltpu.VMEM((halo, CONV_W), F32)],
        compiler_params=_cp(1),
    )(proj, proj, proj, proj, proj, dycat, c1, wdw, vec, wpw)


def _rope_tables(lp):
    half = ROT_DIM // 2
    inv_freq = ROPE_THETA ** (-jnp.arange(half, dtype=F32) / half)
    pos = (jnp.arange(lp, dtype=jnp.int32) - PAD0).astype(F32)
    ang = pos[:, None] * inv_freq[None, :]
    cos, sin = jnp.cos(ang), jnp.sin(ang)
    ones = jnp.ones((lp, HEAD_DIM - ROT_DIM), F32)
    zeros = jnp.zeros((lp, HEAD_DIM - ROT_DIM), F32)
    zh = jnp.zeros((lp, half), F32)
    c = jnp.concatenate([cos, cos, ones], axis=1)
    sa = jnp.concatenate([-sin, zh, zeros], axis=1)
    sb = jnp.concatenate([zh, sin, zeros], axis=1)
    tile = lambda t: jnp.tile(t, (1, KV_W // HEAD_DIM))
    return tile(c), tile(sa), tile(sb)


def _rot(x, c, sa, sb):
    w = x.shape[1]
    return x * c + pltpu.roll(x, w - 8, axis=1) * sa + pltpu.roll(x, 8, axis=1) * sb


def _rot_t(dy, c, sa, sb):
    w = dy.shape[1]
    return dy * c + pltpu.roll(dy * sa, 8, axis=1) + pltpu.roll(dy * sb, w - 8, axis=1)


def _rope_fwd(name, proj, tabs):
    lp = proj.shape[0]
    tr = _row_tile(lp, 11)

    def body(q0_ref, q1_ref, k_ref, c_ref, sa_ref, sb_ref, qr_ref, kr_ref):
        c, sa, sb = c_ref[...], sa_ref[...], sb_ref[...]
        c2 = jnp.concatenate([c, c], axis=1)
        sa2 = jnp.concatenate([sa, sa], axis=1)
        sb2 = jnp.concatenate([sb, sb], axis=1)
        qr_ref[:, 0:512] = (_rot(q0_ref[...], c2, sa2, sb2) * ATT_SCALE).astype(BF16)
        qr_ref[:, 512:1024] = (_rot(q1_ref[...], c2, sa2, sb2) * ATT_SCALE).astype(BF16)
        kr_ref[...] = _rot(k_ref[...], c, sa, sb).astype(BF16)

    tab = pl.BlockSpec((tr, KV_W), lambda i: (i, 0))
    return pl.pallas_call(
        body, name=name, grid=(lp // tr,),
        in_specs=[pl.BlockSpec((tr, 512), lambda i: (i, COL_Q0)),
                  pl.BlockSpec((tr, 512), lambda i: (i, COL_Q0 + 1)),
                  pl.BlockSpec((tr, KV_W), lambda i: (i, COL_K256)),
                  tab, tab, tab],
        out_specs=[pl.BlockSpec((tr, ATT_W), lambda i: (i, 0)),
                   pl.BlockSpec((tr, KV_W), lambda i: (i, 0))],
        out_shape=[jax.ShapeDtypeStruct((lp, ATT_W), BF16),
                   jax.ShapeDtypeStruct((lp, KV_W), BF16)],
        compiler_params=_cp(1),
    )(proj, proj, proj, *tabs)


def _attn_mask(j):
    qi = lax.broadcasted_iota(jnp.int32, (GROUP * TB, 3 * TB), 0) & (TB - 1)
    cc = lax.broadcasted_iota(jnp.int32, (GROUP * TB, 3 * TB), 1)
    jj = cc & (TB - 1)
    is_meta = jj >= PAD0
    p0 = (cc < TB) & is_meta & (j >= 1)
    p1 = (cc >= TB) & (cc < 2 * TB) & (jj > qi) & (j >= 2)
    p2 = (cc >= 2 * TB) & (jj <= qi) & ((j >= 1) | is_meta)
    return p0 | p1 | p2


def _lane_group(rows):
    return lax.broadcasted_iota(jnp.int32, (rows, KV_W), 1) // HEAD_DIM


def _stack_heads(x, kv, lgq):
    parts = []
    for g in range(GROUP):
        sh = ((kv - g) % GROUP) * HEAD_DIM
        moved = x if sh == 0 else pltpu.roll(x, sh, axis=1)
        parts.append(jnp.where(lgq == kv, moved, 0.0))
    return jnp.concatenate(parts, axis=0).astype(BF16)


def _unstack_heads(r, kv):
    out = None
    for g in range(GROUP):
        blk = r[g * TB:(g + 1) * TB, :]
        sh = ((g - kv) % GROUP) * HEAD_DIM
        blk = blk if sh == 0 else pltpu.roll(blk, sh, axis=1)
        out = blk if out is None else out + blk
    return out


def _sink_column(sinks, kv):
    lane = lax.broadcasted_iota(jnp.int32, (1, 128), 1)
    cols = []
    for g in range(GROUP):
        sg = jnp.sum(jnp.where(lane == kv * GROUP + g, sinks, 0.0), axis=1, keepdims=True)
        cols.append(jnp.broadcast_to(sg, (TB, 1)))
    return jnp.concatenate(cols, axis=0)


def _attn_kv(kall, vall, lg, kv):
    km = jnp.where(lg == kv, kall, 0.0).astype(BF16)
    vm = jnp.where(lg == kv, vall, 0.0).astype(BF16)
    ones = jnp.where(lg == kv, 1.0, 0.0).astype(BF16)
    return km, vm, ones


def _attn_weights(qst, km, vm, ones, sinkcol, valid, lg4, kv):
    s = jnp.where(valid, _dot(qst, km, NT), NEG_INF)
    m = jnp.maximum(jnp.max(s, axis=-1, keepdims=True), sinkcol)
    eb = jnp.exp(s - m).astype(BF16)
    es = jnp.exp(sinkcol - m)
    r = _dot(eb, vm, NN)
    inv = 1.0 / (_dot(eb, ones, NN) + es)
    out = jnp.where(lg4 == kv, r * inv, 0.0)
    return eb, es, inv, out


def _attn_specs(jmap):
    blk = lambda col: pl.BlockSpec((TB, KV_W), lambda n: (jmap(n), col))
    prv = lambda col: pl.BlockSpec((TB, KV_W), lambda n: (jnp.maximum(jmap(n) - 1, 0), col))
    met = lambda col: pl.BlockSpec((TB, KV_W), lambda n: (0, col))
    return dict(
        qr=pl.BlockSpec((TB, ATT_W), lambda n: (jmap(n), 0)),
        k=[met(0), prv(0), blk(0)],
        v=[met(COL_V256), prv(COL_V256), blk(COL_V256)],
        gate=pl.BlockSpec((TB, ATT_W), lambda n: (jmap(n), COL_AGATE1024)),
        sinks=pl.BlockSpec((8, 128), lambda n: (0, 0)),
    )


def _attn_fwd(name, qr, kr, proj, sinks_row, ycat):
    lp = proj.shape[0]
    nb = lp // TB
    sp = _attn_specs(lambda n: n)

    def body(qr_ref, km_ref, kp_ref, kc_ref, vm_ref, vp_ref, vc_ref, gate_ref, sink_ref, yin_ref, o_ref):
        del yin_ref
        j = pl.program_id(0)
        valid = _attn_mask(j)
        kall = jnp.concatenate([km_ref[...], kp_ref[...], kc_ref[...]], axis=0).astype(F32)
        vall = jnp.concatenate([vm_ref[...], vp_ref[...], vc_ref[...]], axis=0)
        lg = _lane_group(3 * TB)
        lgq = _lane_group(TB)
        lg4 = _lane_group(GROUP * TB)
        sinks = sink_ref[0:1, :]
        for kv in range(N_KV):
            cols = slice(kv * KV_W, (kv + 1) * KV_W)
            km, vm, ones = _attn_kv(kall, vall, lg, kv)
            qst = _stack_heads(qr_ref[:, cols].astype(F32), kv, lgq)
            _, _, _, out = _attn_weights(qst, km, vm, ones, _sink_column(sinks, kv), valid, lg4, kv)
            att = _unstack_heads(out, kv)
            gate = gate_ref[:, cols]
            o_ref[:, cols] = (att * (gate * _sig(gate))).astype(BF16)

    return pl.pallas_call(
        body, name=name, grid=(nb,),
        in_specs=[sp["qr"]] + sp["k"] + sp["v"] + [sp["gate"], sp["sinks"],
                                                   pl.BlockSpec(memory_space=pl.ANY)],
        out_specs=pl.BlockSpec((TB, ATT_W), lambda n: (n, 0)),
        out_shape=jax.ShapeDtypeStruct((lp, D_MODEL), BF16),
        input_output_aliases={9: 0},
        compiler_params=_cp(1),
    )(qr, kr, kr, kr, proj, proj, proj, proj, sinks_row, ycat)


def _attn_bwd(name, qr, kr, proj, sinks_row, dycat, dep):
    lp = proj.shape[0]
    nb = lp // TB
    sp = _attn_specs(lambda n: n)

    def body(qr_ref, km_ref, kp_ref, kc_ref, vm_ref, vp_ref, vc_ref, gate_ref, sink_ref, dy_ref, dep_ref,
             dq_ref, dgate_ref, dk_ref, dv_ref, dsink_ref):
        del dep_ref
        j = pl.program_id(0)

        @pl.when(j == 0)
        def _():
            dk_ref[...] = jnp.zeros(dk_ref.shape, F32)
            dv_ref[...] = jnp.zeros(dv_ref.shape, F32)
            dsink_ref[...] = jnp.zeros(dsink_ref.shape, F32)

        valid = _attn_mask(j)
        kall = jnp.concatenate([km_ref[...], kp_ref[...], kc_ref[...]], axis=0).astype(F32)
        vall = jnp.concatenate([vm_ref[...], vp_ref[...], vc_ref[...]], axis=0)
        lg = _lane_group(3 * TB)
        lgq = _lane_group(TB)
        lg4 = _lane_group(GROUP * TB)
        sinks = sink_ref[0:1, :]
        lane = lax.broadcasted_iota(jnp.int32, (1, 128), 1)
        dkall = jnp.zeros((3 * TB, KV_W), F32)
        dvall = jnp.zeros((3 * TB, KV_W), F32)
        dsink = jnp.zeros((1, 128), F32)
        for kv in range(N_KV):
            cols = slice(kv * KV_W, (kv + 1) * KV_W)
            km, vm, ones = _attn_kv(kall, vall, lg, kv)
            qst = _stack_heads(qr_ref[:, cols].astype(F32), kv, lgq)
            gate = gate_ref[:, cols]
            sgate = _sig(gate)
            dy = dy_ref[:, cols].astype(F32)
            dout = dy * (gate * sgate)
            eb, es, inv, out = _attn_weights(qst, km, vm, ones, _sink_column(sinks, kv), valid, lg4, kv)
            att = _unstack_heads(out, kv)
            dgate_ref[:, cols] = (dy * att * _dsilu(gate, sgate)).astype(BF16)
            dsc = dout * _unstack_heads(jnp.where(lg4 == kv, inv, 0.0), kv)
            dost = _stack_heads(dsc, kv, lgq)
            dd = dsc * att
            dcol = jnp.concatenate(
                [jnp.sum(jnp.where(lgq == g, dd, 0.0), axis=1, keepdims=True) for g in range(GROUP)], axis=0)
            dp = _dot(dost, vm, NT)
            ds = (eb.astype(F32) * (dp - dcol)).astype(BF16)
            pd = es * dcol
            for g in range(GROUP):
                tot = jnp.sum(pd[g * TB:(g + 1) * TB, :], axis=0, keepdims=True)
                dsink = dsink - jnp.where(lane == kv * GROUP + g, tot, 0.0)
            dq_ref[:, cols] = _unstack_heads(_dot(ds, km, NN), kv)
            dkall = dkall + _dot(ds, qst, TN)
            dvall = dvall + _dot(eb, dost, TN)
        dsink_ref[0:1, :] += dsink
        prev = pl.multiple_of(jnp.maximum(j - 1, 0) * TB, TB)
        cur = pl.multiple_of(j * TB, TB)
        dk_ref[0:TB, :] += dkall[0:TB]
        dv_ref[0:TB, :] += dvall[0:TB]
        dk_ref[pl.ds(prev, TB), :] += dkall[TB:2 * TB]
        dv_ref[pl.ds(prev, TB), :] += dvall[TB:2 * TB]
        dk_ref[pl.ds(cur, TB), :] += dkall[2 * TB:3 * TB]
        dv_ref[pl.ds(cur, TB), :] += dvall[2 * TB:3 * TB]

    return pl.pallas_call(
        body, name=name, grid=(nb,),
        in_specs=[sp["qr"]] + sp["k"] + sp["v"] + [sp["gate"], sp["sinks"],
                                                   pl.BlockSpec((TB, ATT_W), lambda n: (n, 0)),
                                                   pl.BlockSpec(memory_space=pl.ANY)],
        out_specs=[pl.BlockSpec((TB, ATT_W), lambda n: (n, 0)),
                   pl.BlockSpec((TB, ATT_W), lambda n: (n, 0)),
                   pl.BlockSpec((lp, KV_W), lambda n: (0, 0)),
                   pl.BlockSpec((lp, KV_W), lambda n: (0, 0)),
                   pl.BlockSpec((8, 128), lambda n: (0, 0))],
        out_shape=[jax.ShapeDtypeStruct((lp, ATT_W), F32),
                   jax.ShapeDtypeStruct((lp, ATT_W), BF16),
                   jax.ShapeDtypeStruct((lp, KV_W), F32),
                   jax.ShapeDtypeStruct((lp, KV_W), F32),
                   jax.ShapeDtypeStruct((8, 128), F32)],
        compiler_params=_cp(1),
    )(qr, kr, kr, kr, proj, proj, proj, proj, sinks_row, dycat, dep)


def _attn_assemble(name, dq, dgate, dk, dv, tabs, dproj):
    lp = dq.shape[0]
    tr = _row_tile(lp, 11)

    def body(dq_ref, dg_ref, dk_ref, dv_ref, c_ref, sa_ref, sb_ref, din_ref, o_ref):
        del din_ref
        cidx = pl.program_id(1)
        c, sa, sb = c_ref[...], sa_ref[...], sb_ref[...]

        @pl.when(cidx < 2)
        def _():
            c2 = jnp.concatenate([c, c], axis=1)
            sa2 = jnp.concatenate([sa, sa], axis=1)
            sb2 = jnp.concatenate([sb, sb], axis=1)
            o_ref[...] = (_rot_t(dq_ref[...], c2, sa2, sb2) * ATT_SCALE).astype(BF16)

        @pl.when(cidx == 2)
        def _():
            o_ref[:, 0:KV_W] = _rot_t(dk_ref[...], c, sa, sb).astype(BF16)
            o_ref[:, KV_W:2 * KV_W] = dv_ref[...].astype(BF16)

        @pl.when(cidx > 2)
        def _():
            o_ref[...] = dg_ref[...]

    tab = pl.BlockSpec((tr, KV_W), lambda n, c: (n, 0))
    return pl.pallas_call(
        body, name=name, grid=(lp // tr, 5),
        in_specs=[pl.BlockSpec((tr, 512), lambda n, c: (n, jnp.minimum(c, 1))),
                  pl.BlockSpec((tr, 512), lambda n, c: (n, jnp.clip(c - 3, 0, 1))),
                  tab, tab,
                  tab, tab, tab,
                  pl.BlockSpec(memory_space=pl.ANY)],
        out_specs=pl.BlockSpec((tr, 512), lambda n, c: (n, COL_Q0 + c)),
        out_shape=jax.ShapeDtypeStruct((lp, IN_TOTAL), BF16),
        input_output_aliases={7: 0},
        compiler_params=_cp(2),
    )(dq, dgate, dk, dv, *tabs, dproj)


def _softplus_neg(lam):
    t = jnp.exp(-jnp.abs(lam))
    u = 1.0 + t
    den = jnp.where(u == 1.0, 1.0, u - 1.0)
    l1p = jnp.where(u == 1.0, t, jnp.log(u) * (t / den))
    return jnp.maximum(-lam, 0.0) + l1p


def _lru_chain(j, tb, rx_ref, rxp_ref, wl_ref, vec_ref, wa_ref, wx_ref):
    rx = rx_ref[...]
    rxp = jnp.where(j > 0, rxp_ref[...], 0.0)
    cat = jnp.concatenate([rxp, rx], axis=0)
    views = [cat[8:8 + tb, :]] + [pltpu.roll(cat, s, axis=0)[8:8 + tb, :] for s in range(1, LRU_CONV_K)]
    x1 = jnp.broadcast_to(vec_ref[0:1, :], (tb, LRU_W))
    for k in range(LRU_CONV_K):
        x1 = x1 + wl_ref[k:k + 1, :] * views[LRU_CONV_K - 1 - k]
    x1b = x1.astype(BF16)
    r = _sig(_dot(x1b, wa_ref[...], NN) + vec_ref[1:2, :])
    ig = _sig(_dot(x1b, wx_ref[...], NN) + vec_ref[2:3, :])
    sp = _softplus_neg(vec_ref[3:4, :])
    log_a = -LRU_C * r * sp
    rows = _row_ids((tb, LRU_W), j * tb)
    live = rows >= PAD0
    a = jnp.where(live, jnp.exp(log_a), 0.0)
    y2 = 2.0 * log_a
    em = -jnp.tanh(0.5 * y2) * (jnp.exp(y2) + 1.0)
    mult = jnp.sqrt(em)
    return dict(views=views, x1=x1, x1b=x1b, r=r, ig=ig, sp=sp, a=a, mult=mult, live=live, a_raw=jnp.exp(log_a))


def _scan_slabs(a, u, forward):
    tb = a.shape[0]
    rows = lax.broadcasted_iota(jnp.int32, (tb, SUB), 0)
    outs_a, outs_u = [], []
    for c0 in range(0, a.shape[1], SUB):
        ac, uc = a[:, c0:c0 + SUB], u[:, c0:c0 + SUB]
        d = 1
        while d < tb:
            if forward:
                keep, sh = rows >= d, d
            else:
                keep, sh = rows < tb - d, tb - d
            an = jnp.where(keep, pltpu.roll(ac, sh, axis=0), 1.0)
            un = jnp.where(keep, pltpu.roll(uc, sh, axis=0), 0.0)
            uc = ac * un + uc
            ac = ac * an
            d *= 2
        outs_a.append(ac)
        outs_u.append(uc)
    return jnp.concatenate(outs_a, axis=1), jnp.concatenate(outs_u, axis=1)


def _lru_specs(jmap, tb):
    return [pl.BlockSpec((tb, 512), lambda n: (jmap(n), COL_RX)),
            pl.BlockSpec((8, 512), lambda n: (jnp.maximum(jmap(n) * (tb // 8) - 1, 0), COL_RX)),
            pl.BlockSpec((tb, 512), lambda n: (jmap(n), COL_RGATE))]


def _lru_param_specs():
    return [pl.BlockSpec((8, LRU_W), lambda n: (0, 0)),
            pl.BlockSpec((8, LRU_W), lambda n: (0, 0)),
            pl.BlockSpec((LRU_W, LRU_W), lambda n: (0, 0)),
            pl.BlockSpec((LRU_W, LRU_W), lambda n: (0, 0))]


def _lru_fwd(name, proj, wl, vec, wa, wx, ycat):
    lp = proj.shape[0]
    tb = _row_tile(lp, 3)
    nb = lp // tb

    def body(rx_ref, rxp_ref, gate_ref, wl_ref, vec_ref, wa_ref, wx_ref, yin_ref, o_ref, h_ref, carry_ref):
        del yin_ref
        j = pl.program_id(0)

        @pl.when(j == 0)
        def _():
            carry_ref[...] = jnp.zeros(carry_ref.shape, F32)

        c = _lru_chain(j, tb, rx_ref, rxp_ref, wl_ref, vec_ref, wa_ref, wx_ref)
        u = jnp.where(c["live"], c["mult"] * (c["ig"] * c["x1"]), 0.0)
        a, u = _scan_slabs(c["a"], u, forward=True)
        h = u + a * carry_ref[0:1, :]
        carry_ref[...] = h[tb - 8:tb, :]
        carry_ref[0:1, :] = h[tb - 1:tb, :]
        h_ref[...] = h
        gate = gate_ref[...]
        o_ref[...] = (h * (gate * _sig(gate))).astype(BF16)

    return pl.pallas_call(
        body, name=name, grid=(nb,),
        in_specs=_lru_specs(lambda n: n, tb) + _lru_param_specs() + [pl.BlockSpec(memory_space=pl.ANY)],
        out_specs=[pl.BlockSpec((tb, 512), lambda n: (n, YC_LRU)),
                   pl.BlockSpec((tb, LRU_W), lambda n: (n, 0))],
        out_shape=[jax.ShapeDtypeStruct((lp, D_MODEL), BF16),
                   jax.ShapeDtypeStruct((lp, LRU_W), F32)],
        input_output_aliases={7: 0},
        scratch_shapes=[pltpu.VMEM((8, LRU_W), F32)],
        compiler_params=_cp(1),
    )(proj, proj, proj, wl, vec, wa, wx, ycat)


def _lru_bwd(name, proj, dycat, hstate, wl, vec, wa, wx, dproj):
    lp = proj.shape[0]
    tb = _row_tile(lp, 3)
    nb = lp // tb

    def body(rx_ref, rxp_ref, gate_ref, dy_ref, h_ref, hp_ref, wl_ref, vec_ref, wa_ref, wx_ref, din_ref,
             dp_ref, dwl_ref, dvec_ref, dwa_ref, dwx_ref, dhc_ref, anx_ref, dxc_ref):
        del din_ref
        n = pl.program_id(0)
        j = nb - 1 - n

        @pl.when(n == 0)
        def _():
            dhc_ref[...] = jnp.zeros(dhc_ref.shape, F32)
            anx_ref[...] = jnp.zeros(anx_ref.shape, F32)
            dxc_ref[...] = jnp.zeros(dxc_ref.shape, F32)
            dwl_ref[...] = jnp.zeros(dwl_ref.shape, F32)
            dvec_ref[...] = jnp.zeros(dvec_ref.shape, F32)
            dwa_ref[...] = jnp.zeros(dwa_ref.shape, F32)
            dwx_ref[...] = jnp.zeros(dwx_ref.shape, F32)

        c = _lru_chain(j, tb, rx_ref, rxp_ref, wl_ref, vec_ref, wa_ref, wx_ref)
        a, mult, r, ig, x1, live = c["a"], c["mult"], c["r"], c["ig"], c["x1"], c["live"]
        h = h_ref[...]
        gate = gate_ref[...]
        sgate = _sig(gate)
        dy = dy_ref[...].astype(F32)
        gsum = dy * (gate * sgate)
        dgate = dy * h * _dsilu(gate, sgate)
        rows = lax.broadcasted_iota(jnp.int32, (tb, LRU_W), 0)
        bb = jnp.where(rows == tb - 1, anx_ref[0:1, :], pltpu.roll(a, tb - 1, axis=0))
        bb, gg = _scan_slabs(bb, gsum, forward=False)
        dh = gg + bb * dhc_ref[0:1, :]
        dhc_ref[...] = dh[0:8, :]
        anx_ref[...] = a[0:8, :]
        hprev = jnp.where(rows == 0, jnp.where(j > 0, hp_ref[7:8, :], 0.0), pltpu.roll(h, 1, axis=0))
        du = jnp.where(live, dh, 0.0)
        da = jnp.where(live, dh * hprev, 0.0)
        ar = c["a_raw"]
        dmult = du * (ig * x1)
        di = du * mult * x1
        dx1 = du * mult * ig
        dloga = da * ar - dmult * ar * ar / mult
        dr = dloga * (-LRU_C * c["sp"])
        dvec_ref[3:4, :] += _colsum(dloga * (-LRU_C * r))
        dza = dr * r * (1.0 - r)
        dzx = di * ig * (1.0 - ig)
        dzab, dzxb = dza.astype(BF16), dzx.astype(BF16)
        dvec_ref[1:2, :] += _colsum(dza)
        dvec_ref[2:3, :] += _colsum(dzx)
        dwa_ref[...] += _dot(c["x1b"], dzab, TN)
        dwx_ref[...] += _dot(c["x1b"], dzxb, TN)
        dx1 = dx1 + _dot(dzab, wa_ref[...], NT) + _dot(dzxb, wx_ref[...], NT)
        dvec_ref[0:1, :] += _colsum(dx1)
        for k in range(LRU_CONV_K):
            dwl_ref[k:k + 1, :] += _colsum(dx1 * c["views"][LRU_CONV_K - 1 - k])
        dcat = jnp.concatenate([dx1, dxc_ref[...]], axis=0)
        drx = jnp.zeros((tb, LRU_W), F32)
        for k in range(LRU_CONV_K):
            s = LRU_CONV_K - 1 - k
            view = dcat[0:tb, :] if s == 0 else pltpu.roll(dcat, tb + 8 - s, axis=0)[0:tb, :]
            drx = drx + wl_ref[k:k + 1, :] * view
        dxc_ref[...] = dx1[0:8, :]
        dp_ref[:, 0:512] = drx.astype(BF16)
        dp_ref[:, 512:1024] = dgate.astype(BF16)

        @pl.when(n == nb - 1)
        def _():
            lam = vec_ref[3:4, :]
            dvec_ref[3:4, :] = dvec_ref[3:4, :] * (-_sig(-lam))

    jmap = lambda n: nb - 1 - n
    return pl.pallas_call(
        body, name=name, grid=(nb,),
        in_specs=(_lru_specs(jmap, tb)
                  + [pl.BlockSpec((tb, 512), lambda n: (jmap(n), YC_LRU)),
                     pl.BlockSpec((tb, LRU_W), lambda n: (jmap(n), 0)),
                     pl.BlockSpec((8, LRU_W), lambda n: (jnp.maximum(jmap(n) * (tb // 8) - 1, 0), 0))]
                  + _lru_param_specs() + [pl.BlockSpec(memory_space=pl.ANY)]),
        out_specs=[pl.BlockSpec((tb, 1024), lambda n: (jmap(n), 4)),
                   pl.BlockSpec((8, LRU_W), lambda n: (0, 0)),
                   pl.BlockSpec((8, LRU_W), lambda n: (0, 0)),
                   pl.BlockSpec((LRU_W, LRU_W), lambda n: (0, 0)),
                   pl.BlockSpec((LRU_W, LRU_W), lambda n: (0, 0))],
        out_shape=[jax.ShapeDtypeStruct((lp, IN_TOTAL), BF16),
                   jax.ShapeDtypeStruct((8, LRU_W), F32),
                   jax.ShapeDtypeStruct((8, LRU_W), F32),
                   jax.ShapeDtypeStruct((LRU_W, LRU_W), F32),
                   jax.ShapeDtypeStruct((LRU_W, LRU_W), F32)],
        input_output_aliases={10: 0},
        scratch_shapes=[pltpu.VMEM((8, LRU_W), F32), pltpu.VMEM((8, LRU_W), F32), pltpu.VMEM((8, LRU_W), F32)],
        compiler_params=_cp(1),
    )(proj, proj, proj, dycat, hstate, hstate, wl, vec, wa, wx, dproj)


_HBM = pl.BlockSpec(memory_space=pltpu.HBM)
_SEM = pl.BlockSpec(memory_space=pltpu.SEMAPHORE)
_ANY = pl.BlockSpec(memory_space=pl.ANY)
_EFFECT = pltpu.SideEffectType.DATAFLOW_SIDE_EFFECTING


def _hbm(a):
    return pltpu.with_memory_space_constraint(a, pltpu.HBM)


_ALL_PEERS = tuple(range(1, N_DEV))
_CHIP_PEERS = (1, 2, 4, 6)
_OTHER_CHIPS = (2, 4, 6)


def _spec_peers(mode):
    return {"ici": _CHIP_PEERS, "fwd": _OTHER_CHIPS}.get(mode, _ALL_PEERS)


def _split_descriptors(copies, srcs, lands, send_sems, recv_sems):
    x, y, c = lax.axis_index("x"), lax.axis_index("y"), lax.axis_index("c")
    me = 4 * x + 2 * y + c
    out, sem = [], 0
    for si, mode, li, ll in copies:
        for k in _spec_peers(mode):
            px = 1 - x if k & 4 else x
            py = 1 - y if k & 2 else y
            pc = 1 - c if k & 1 else c
            peer = 4 * px + 2 * py + pc
            if mode == "fwd":
                src = dst = lands[li].at[peer]
                target = (x, y, 1 - c)
            else:
                src = srcs[si].at[peer] if mode is True else srcs[si]
                dst = lands[li].at[me] if ll is None else lands[li].at[me, ll]
                target = (px, py, pc)
            out.append(pltpu.make_async_remote_copy(
                src_ref=src, dst_ref=dst, send_sem=send_sems.at[sem], recv_sem=recv_sems.at[sem],
                device_id=target, device_id_type=pl.DeviceIdType.MESH))
            sem += 1
    return out


def _n_copies(copies):
    return sum(len(_spec_peers(mode)) for _, mode, _, _ in copies)


def _xchg_start(name, groups):
    n_src = [len(g[0]) for g in groups]
    n_land = [len(g[1]) for g in groups]
    srcs = [s for g in groups for s in g[0]]
    lands = [l for g in groups for l in g[1]]
    ns, nl, ng = len(srcs), len(lands), len(groups)

    def body(*refs):
        src_refs, land_refs = refs[:ns], refs[ns:ns + nl]
        sems = refs[ns + nl:ns + nl + 2 * ng]
        token = refs[-1]
        so = lo = 0
        for gi, (_, _, copies) in enumerate(groups):
            for d in _split_descriptors(copies, src_refs[so:so + n_src[gi]], land_refs[lo:lo + n_land[gi]],
                                        sems[2 * gi], sems[2 * gi + 1]):
                d.start()
            so += n_src[gi]
            lo += n_land[gi]
        token[...] = jnp.zeros(token.shape, F32)

    out_shape, out_specs = [], []
    for g in groups:
        n = _n_copies(g[2])
        out_shape += [pltpu.SemaphoreType.DMA((n,)), pltpu.SemaphoreType.DMA((n,))]
        out_specs += [_SEM, _SEM]
    out_shape += [pltpu.HBM(l.shape, l.dtype) for l in lands]
    out_specs += [_HBM] * nl
    out_shape.append(jax.ShapeDtypeStruct((8, 128), F32))
    out_specs.append(pl.BlockSpec(memory_space=pltpu.VMEM))
    outs = pl.pallas_call(
        body, name=name, in_specs=[_HBM] * (ns + nl), out_specs=out_specs, out_shape=out_shape,
        input_output_aliases={ns + i: 2 * ng + i for i in range(nl)},
        compiler_params=pltpu.CompilerParams(has_side_effects=_EFFECT),
    )(*[_hbm(a) for a in srcs + lands])
    res, lo = [], 2 * ng
    for gi in range(ng):
        res.append((outs[2 * gi], outs[2 * gi + 1], list(outs[lo:lo + n_land[gi]])))
        lo += n_land[gi]
    return res, outs[-1]


def _xchg_wait(name, group, started, after):
    srcs, _, copies = group
    send_sems, recv_sems, lands = started
    ns, nl = len(srcs), len(lands)
    after = list(after)

    def body(*refs):
        src_refs, land_refs = refs[:ns], refs[ns:ns + nl]
        send_ref, recv_ref = refs[ns + nl], refs[ns + nl + 1]
        for d in _split_descriptors(copies, src_refs, land_refs, send_ref, recv_ref):
            d.wait_send()
            d.wait_recv()

    outs = pl.pallas_call(
        body, name=name, in_specs=[_HBM] * (ns + nl) + [_SEM, _SEM] + [_ANY] * len(after),
        out_specs=[_HBM] * nl, out_shape=[pltpu.HBM(l.shape, l.dtype) for l in lands],
        input_output_aliases={ns + i: i for i in range(nl)},
        compiler_params=pltpu.CompilerParams(has_side_effects=_EFFECT),
    )(*[_hbm(a) for a in srcs], *lands, send_sems, recv_sems, *after)
    return list(outs)


def _landing(own, me):
    land = lax.empty((N_DEV,) + own.shape, own.dtype)
    return lax.dynamic_update_slice(land, own[None], (me,) + (0,) * own.ndim)


def _adamw(name, w, m, v, recv, row0=0, prev=None):
    cdim = w.shape[1]
    r = recv.shape[1]
    tr = r
    for cand in (512, 256, 128, 64, 32, 16, 8):
        if r % cand == 0 and r > cand:
            tr = cand
            break
    assert row0 % tr == 0
    blk0 = row0 // tr
    n_prev = 0 if prev is None else 4

    def body(w_ref, m_ref, v_ref, r_ref, *rest):
        g_ref, d_ref, mo_ref, vo_ref = rest[n_prev:]
        g = r_ref[0].astype(F32)
        for s in range(1, N_DEV):
            g = g + r_ref[s].astype(F32)
        mn = ADAM_B1 * m_ref[...] + (1.0 - ADAM_B1) * g
        vn = ADAM_B2 * v_ref[...] + (1.0 - ADAM_B2) * (g * g)
        m_hat = mn / (1.0 - ADAM_B1 ** ADAM_STEP)
        v_hat = vn / (1.0 - ADAM_B2 ** ADAM_STEP)
        g_ref[...] = g
        d_ref[...] = -ADAM_LR * (m_hat / (jnp.sqrt(v_hat) + ADAM_EPS) + ADAM_WD * w_ref[...])
        mo_ref[...] = mn
        vo_ref[...] = vn

    blk = pl.BlockSpec((tr, cdim), lambda i: (i + blk0, 0))
    return pl.pallas_call(
        body, name=name, grid=(r // tr,),
        in_specs=[blk, blk, blk, pl.BlockSpec((N_DEV, tr, cdim), lambda i: (0, i, 0))] + [_ANY] * n_prev,
        out_specs=[blk, blk, blk, blk],
        out_shape=[jax.ShapeDtypeStruct(w.shape, F32)] * 4,
        input_output_aliases={4 + i: i for i in range(n_prev)},
        compiler_params=_cp(1),
    )(w, m, v, recv, *(prev or []))


def _pack_rows(arrs, lead=()):
    n = len(lead)
    flat = jnp.concatenate([a.reshape(a.shape[:n] + (-1,)) for a in arrs], axis=-1)
    size = flat.shape[-1]
    padded = -(-size // PACK_QUANTUM) * PACK_QUANTUM
    flat = jnp.pad(flat, [(0, 0)] * n + [(0, padded - size)])
    return flat.reshape(flat.shape[:n] + (padded // 128, 128))


def _unpack_rows(packed, shapes, lead=()):
    n = len(lead)
    flat = packed.reshape(packed.shape[:n] + (-1,))
    out, off = [], 0
    for s in shapes:
        size = int(np.prod(s))
        out.append(flat[..., off:off + size].reshape(packed.shape[:n] + tuple(s)))
        off += size
    return out


def _block_diag(w):
    eye = jnp.eye(LRU_HEADS, dtype=w.dtype)
    return (eye[:, None, :, None] * w[:, :, None, :]).reshape(LRU_W, LRU_W)


def _diag_blocks(dense):
    t = dense.reshape(LRU_HEADS, 64, LRU_HEADS, 64)
    eye = jnp.eye(LRU_HEADS, dtype=dense.dtype)
    return jnp.sum(t * eye[:, None, :, None], axis=2).reshape(LRU_HEADS * 64, 64)


_W512_NAMES = ("conv_dw_b", "conv_ln_g", "conv_ln_b", "conv_pw_b", "lru_conv_b", "lru_ba", "lru_bx", "lru_lambda")
_W512_ROWS = 12


def _pack_small(d):
    sinks = jnp.pad(d["attn_sinks"], ((0, 0), (0, 512 - N_HEADS)))
    t = jnp.stack([d[n] for n in _W512_NAMES] + [sinks], axis=1)
    w512 = jnp.pad(t, ((0, 0), (0, _W512_ROWS - t.shape[1]), (0, 0))).reshape(DEPTH * _W512_ROWS, 512)
    w2048 = jnp.concatenate([d["ln_in_g"][None], d["ln_in_b"][None], d["ln_post_g"], d["ln_post_b"],
                             jnp.zeros((2, D_MODEL), F32)], axis=0)
    w64 = jnp.concatenate([d["lru_wa"].reshape(-1, 64), d["lru_wx"].reshape(-1, 64)], axis=0)
    return [w512, w2048, w64.reshape(-1, 128)]


def _unpack_small(w512, w2048, w64):
    t = w512.reshape(DEPTH, _W512_ROWS, 512)
    out = {n: t[:, i, :] for i, n in enumerate(_W512_NAMES)}
    out["attn_sinks"] = t[:, len(_W512_NAMES), :N_HEADS]
    out["ln_in_g"], out["ln_in_b"] = w2048[0], w2048[1]
    out["ln_post_g"], out["ln_post_b"] = w2048[2:4], w2048[4:6]
    w64 = w64.reshape(-1, 64)
    half = w64.shape[0] // 2
    out["lru_wa"] = w64[:half].reshape(DEPTH, LRU_HEADS, 64, 64)
    out["lru_wx"] = w64[half:].reshape(DEPTH, LRU_HEADS, 64, 64)
    return out


def _cols_to_slots(full):
    lead = full.shape[:-1]
    t = full.reshape(lead + (N_DEV, full.shape[-1] // N_DEV))
    return jnp.moveaxis(t, -2, 0)


def _slots_to_cols(slots):
    t = jnp.moveaxis(slots, 0, -2)
    return t.reshape(t.shape[:-2] + (t.shape[-2] * t.shape[-1],))


def kernel(x, meta_tokens, ln_in_g, ln_in_b, w_in, conv_dw_w, conv_dw_b, conv_ln_g, conv_ln_b, conv_pw_w, conv_pw_b, attn_sinks, lru_conv_w, lru_conv_b, lru_wa, lru_ba, lru_wx, lru_bx, lru_lambda, w_out, ln_post_g, ln_post_b, loss_target, m_meta_tokens, m_ln_in_g, m_ln_in_b, m_w_in, m_conv_dw_w, m_conv_dw_b, m_conv_ln_g, m_conv_ln_b, m_conv_pw_w, m_conv_pw_b, m_attn_sinks, m_lru_conv_w, m_lru_conv_b, m_lru_wa, m_lru_ba, m_lru_wx, m_lru_bx, m_lru_lambda, m_w_out, m_ln_post_g, m_ln_post_b, v_meta_tokens, v_ln_in_g, v_ln_in_b, v_w_in, v_conv_dw_w, v_conv_dw_b, v_conv_ln_g, v_conv_ln_b, v_conv_pw_w, v_conv_pw_b, v_attn_sinks, v_lru_conv_w, v_lru_conv_b, v_lru_wa, v_lru_ba, v_lru_wx, v_lru_bx, v_lru_lambda, v_w_out, v_ln_post_g, v_ln_post_b):
    seq = x.shape[1]
    lp = seq + TB
    row = lambda a: a.reshape(1, -1)
    rep_names = ["ln_in_g", "ln_in_b", "conv_dw_b", "conv_ln_g", "conv_ln_b", "conv_pw_b", "attn_sinks",
                 "lru_conv_b", "lru_wa", "lru_ba", "lru_wx", "lru_bx", "lru_lambda", "ln_post_g", "ln_post_b"]
    shard_small_names = ["conv_dw_w", "lru_conv_w", "meta_tokens"]
    weights = dict(meta_tokens=meta_tokens, ln_in_g=ln_in_g, ln_in_b=ln_in_b, w_in=w_in, conv_dw_w=conv_dw_w,
                   conv_dw_b=conv_dw_b, conv_ln_g=conv_ln_g, conv_ln_b=conv_ln_b, conv_pw_w=conv_pw_w,
                   conv_pw_b=conv_pw_b, attn_sinks=attn_sinks, lru_conv_w=lru_conv_w, lru_conv_b=lru_conv_b,
                   lru_wa=lru_wa, lru_ba=lru_ba, lru_wx=lru_wx, lru_bx=lru_bx, lru_lambda=lru_lambda,
                   w_out=w_out, ln_post_g=ln_post_g, ln_post_b=ln_post_b)
    mom1 = dict(meta_tokens=m_meta_tokens, ln_in_g=m_ln_in_g, ln_in_b=m_ln_in_b, w_in=m_w_in, conv_dw_w=m_conv_dw_w,
                conv_dw_b=m_conv_dw_b, conv_ln_g=m_conv_ln_g, conv_ln_b=m_conv_ln_b, conv_pw_w=m_conv_pw_w,
                conv_pw_b=m_conv_pw_b, attn_sinks=m_attn_sinks, lru_conv_w=m_lru_conv_w, lru_conv_b=m_lru_conv_b,
                lru_wa=m_lru_wa, lru_ba=m_lru_ba, lru_wx=m_lru_wx, lru_bx=m_lru_bx, lru_lambda=m_lru_lambda,
                w_out=m_w_out, ln_post_g=m_ln_post_g, ln_post_b=m_ln_post_b)
    mom2 = dict(meta_tokens=v_meta_tokens, ln_in_g=v_ln_in_g, ln_in_b=v_ln_in_b, w_in=v_w_in, conv_dw_w=v_conv_dw_w,
                conv_dw_b=v_conv_dw_b, conv_ln_g=v_conv_ln_g, conv_ln_b=v_conv_ln_b, conv_pw_w=v_conv_pw_w,
                conv_pw_b=v_conv_pw_b, attn_sinks=v_attn_sinks, lru_conv_w=v_lru_conv_w, lru_conv_b=v_lru_conv_b,
                lru_wa=v_lru_wa, lru_ba=v_lru_ba, lru_wx=v_lru_wx, lru_bx=v_lru_bx, lru_lambda=v_lru_lambda,
                w_out=v_w_out, ln_post_g=v_ln_post_g, ln_post_b=v_ln_post_b)
    shard_wmv = [_pack_rows([d[n] for n in shard_small_names]) for d in (weights, mom1, mom2)]
    rep_wmv = [_pack_small(d) for d in (weights, mom1, mom2)]
    gate_w = [(_block_diag(lru_wa[l]).astype(BF16), _block_diag(lru_wx[l]).astype(BF16)) for l in range(DEPTH)]
    tabs = _rope_tables(lp)
    prepared = (shard_wmv + [a for wmv in rep_wmv for a in wmv]
                + [w for pair in gate_w for w in pair] + list(tabs))

    small_shard_shapes = [conv_dw_w.shape, lru_conv_w.shape, meta_tokens.shape]
    small_shard = _pack_rows([conv_dw_w, lru_conv_w, meta_tokens])
    me = 4 * lax.axis_index("x") + 2 * lax.axis_index("y") + lax.axis_index("c")
    w_in_b = [w_in[l].astype(BF16) for l in range(DEPTH)]
    w_out_b = [w_out[l].astype(BF16) for l in range(DEPTH)]
    pw_b = conv_pw_w.astype(BF16)
    wgroups = [
        ([small_shard], [_landing(small_shard, me)], [(0, False, 0, None)]),
        ([w_in_b[0]], [_landing(w_in_b[0], me)], [(0, "ici", 0, None)]),
        ([pw_b, w_out_b[0]], [_landing(pw_b, me), _landing(w_out_b[0], me)],
         [(0, False, 0, None), (1, False, 1, None)]),
        ([w_in_b[1], w_out_b[1]], [_landing(w_in_b[1], me), _landing(w_out_b[1], me)],
         [(0, False, 0, None), (1, False, 1, None)]),
    ]
    wstarted, wtoken = _xchg_start("weights_start", wgroups)
    wg_small, = _xchg_wait("weights_wait_s", wgroups[0], wstarted[0], [wtoken])
    g_dw, g_lc, g_meta = _unpack_rows(wg_small, small_shard_shapes, lead=(N_DEV,))
    conv_dw_full = _slots_to_cols(g_dw)
    lru_conv_full = _slots_to_cols(g_lc)
    meta_full = _slots_to_cols(g_meta)
    wg_in = [None, None]
    wg_out = [None, None]
    wg_pw = None

    ln_g = [ln_in_g, ln_post_g[0], ln_post_g[1]]
    ln_b = [ln_in_b, ln_post_b[0], ln_post_b[1]]

    def layer_params(l):
        wdw = jnp.pad(conv_dw_full[l], ((0, 1), (0, 0)))
        cvec = jnp.pad(jnp.stack([conv_dw_b[l], conv_ln_g[l], conv_ln_b[l], conv_pw_b[l]]), ((0, 4), (0, 0)))
        wpw = wg_pw[:, l].reshape(CONV_W, CONV_W)
        sinks = jnp.pad(attn_sinks[l].reshape(1, N_HEADS), ((0, 7), (0, 128 - N_HEADS)))
        wl = jnp.pad(lru_conv_full[l], ((0, 4), (0, 0)))
        lvec = jnp.pad(jnp.stack([lru_conv_b[l], lru_ba[l], lru_bx[l], lru_lambda[l]]), ((0, 4), (0, 0)))
        wa, wx = gate_w[l]
        wo = wg_out[l].reshape(D_MODEL, D_MODEL)
        wout = jnp.concatenate([wo[512:1536], wo[0:512], wo[1536:]], axis=0)
        return dict(wdw=wdw, cvec=cvec, wpw=wpw, sinks=sinks, wl=wl, lvec=lvec, wa=wa, wx=wx, wout=wout)

    params = [None] * DEPTH

    z0, hb = _embed(x, meta_full, row(ln_g[0]), row(ln_b[0]))
    z = [z0]
    saved = []
    for l in range(DEPTH):
        if l == 0:
            part, = _xchg_wait("weights_wait_a", wgroups[1], wstarted[1], [hb] + prepared)
            fwd = ([], [part], [(None, "fwd", 0, None)])
            fstarted, ftoken = _xchg_start("weights_fwd_start", [fwd])
            wg_in[0], = _xchg_wait("weights_fwd_wait", fwd, fstarted[0], [ftoken])
        else:
            wg_in[1], wg_out[1] = _xchg_wait("weights_wait_c", wgroups[3], wstarted[3], [hb])
        proj = _mm_proj(f"proj{l}", hb, wg_in[l])
        if l == 0:
            wg_pw, wg_out[0] = _xchg_wait("weights_wait_b", wgroups[2], wstarted[2], [proj])
        p = params[l] = layer_params(l)
        ycat, c1 = _conv_fwd(f"conv_fwd{l}", proj, p["wdw"], p["cvec"], p["wpw"])
        qr, kr = _rope_fwd(f"rope{l}", proj, tabs)
        ycat = _attn_fwd(f"attn_fwd{l}", qr, kr, proj, p["sinks"], ycat)
        ycat, hstate = _lru_fwd(f"lru_fwd{l}", proj, p["wl"], p["lvec"], p["wa"], p["wx"], ycat)
        saved.append(dict(hb=hb, proj=proj, ycat=ycat, qr=qr, kr=kr, hstate=hstate, c1=c1))
        z_next, hb = _mm_out(f"out{l}", ycat, p["wout"], z[l], row(ln_g[l]), row(ln_b[l]),
                             row(ln_g[l + 1]), row(ln_b[l + 1]))
        z.append(z_next)

    dz, st_post1, loss_blk = _loss_head(z[DEPTH], loss_target, row(ln_g[DEPTH]), row(ln_b[DEPTH]))
    loss = lax.psum(loss_blk[0, 0], ("x", "y", "c"))

    ln_stats = {DEPTH: st_post1}
    g_layers = [None] * DEPTH
    dwin_l, dwout_l = [None] * DEPTH, [None] * DEPTH
    grad_x = gmeta = None
    token = wtoken
    ggroups = [None] * DEPTH
    own = lambda a: lax.dynamic_index_in_dim(a, me, 0, keepdims=False)
    for l in reversed(range(DEPTH)):
        p, s = params[l], saved[l]
        dycat = _mm_dycat(f"dycat{l}", dz, p["wout"], token)
        dwout_l[l] = _mm_dwout(f"dwout{l}", s["ycat"], dz)
        dproj, dwdw, dcvec, dwpw = _conv_bwd(f"conv_bwd{l}", s["proj"], dycat, s["c1"], p["wdw"], p["cvec"], p["wpw"])
        dwo = jnp.concatenate([dwout_l[l][1024:1536], dwout_l[l][0:1024], dwout_l[l][1536:]], axis=0)
        dwo = dwo.reshape(N_DEV, D_MODEL // N_DEV, D_MODEL)
        dpw = dwpw.reshape(N_DEV, CONV_W // N_DEV, CONV_W)
        early = ([dwo, dpw], [_landing(own(dwo), me), _landing(own(dpw), me)],
                 [(0, True, 0, None), (1, True, 1, None)])
        started_early, token = _xchg_start(f"grads_start_out{l}", [early])
        dq, dgate, dk, dv, dsink = _attn_bwd(f"attn_bwd{l}", s["qr"], s["kr"], s["proj"], p["sinks"], dycat, token)
        dproj = _attn_assemble(f"attn_asm{l}", dq, dgate, dk, dv, tabs, dproj)
        dproj, dwl, dlvec, dwa, dwx = _lru_bwd(f"lru_bwd{l}", s["proj"], dycat, s["hstate"],
                                                p["wl"], p["lvec"], p["wa"], p["wx"], dproj)
        g512 = jnp.concatenate([dcvec[0:4], dlvec[0:4], jnp.pad(dsink[0:1], ((0, 0), (0, 512 - 128))),
                                jnp.zeros((_W512_ROWS - 9, 512), F32)], axis=0)
        g_layers[l] = dict(dwdw=dwdw[:CONV_K], dwl=dwl[:LRU_CONV_K], g512=g512,
                           dwa=_diag_blocks(dwa), dwx=_diag_blocks(dwx))
        if l == 0:
            g512 = jnp.concatenate([g_layers[i]["g512"] for i in range(DEPTH)], axis=0)
            g64 = jnp.concatenate([g_layers[i][k] for k in ("dwa", "dwx") for i in range(DEPTH)], axis=0)
            g64 = g64.reshape(-1, 128)
            vgroup = ([g512, g64], [_landing(g512, me), _landing(g64, me)],
                      [(0, False, 0, None), (1, False, 1, None)])
            vstarted, token = _xchg_start("vector_grads_start", [vgroup])
        dwin_l[l] = _mm_dwin(f"dwin{l}", s["hb"], dproj, token)
        late = ([dwin_l[l]], [_landing(own(dwin_l[l]), me)], [(0, True, 0, None)])
        started_late, token = _xchg_start(f"grads_start_in{l}", [late])
        ggroups[l] = [(late, started_late[0]), (early, started_early[0])]
        dh = _mm_dh(f"dh{l}", dproj, wg_in[l], dz, token)
        if l > 0:
            dz, ln_stats[l] = _ln_bwd(f"ln_bwd{l}", dh, z[l], row(ln_g[l]))
        else:
            grad_x, gmeta, ln_stats[0] = _ln_bwd_input(dh, z[0], row(ln_g[0]))

    g2048 = jnp.concatenate([ln_stats[0][0:2], ln_stats[1][0:1], ln_stats[2][0:1], ln_stats[1][1:2],
                             ln_stats[2][1:2], jnp.zeros((2, D_MODEL), F32)], axis=0)
    g_dw_full = jnp.stack([g_layers[l]["dwdw"] for l in range(DEPTH)])
    g_lc_full = jnp.stack([g_layers[l]["dwl"] for l in range(DEPTH)])
    shard_pack = _pack_rows([_cols_to_slots(g_dw_full), _cols_to_slots(g_lc_full), _cols_to_slots(gmeta)],
                            lead=(N_DEV,))
    sgroup = ([shard_pack, g2048], [_landing(own(shard_pack), me), _landing(g2048, me)],
              [(0, True, 0, None), (1, False, 1, None)])
    sstarted, token = _xchg_start("small_grads_start", [sgroup])

    res = {}

    def flat2(a, cols):
        return a.reshape(-1, cols)

    big = (("w_in", 0, W_IN_SHARD), ("w_out", 1, D_MODEL), ("conv_pw_w", 2, CONV_W))
    prev = {n: None for n, _, _ in big}
    def update(name_, cols, recv, l):
        w_ = weights[name_]
        prev[name_] = _adamw(f"adamw_{name_}{l}", flat2(w_, cols), flat2(mom1[name_], cols),
                             flat2(mom2[name_], cols), recv, row0=l * w_.shape[1], prev=prev[name_])

    after = [token]
    for l in reversed(range(DEPTH)):
        late, early = ggroups[l]
        r_out, r_pw = _xchg_wait(f"grads_wait{l}_1", early[0], early[1], after)
        if l > 0:
            r_in, = _xchg_wait(f"grads_wait{l}_0", late[0], late[1], after)
            update("w_in", W_IN_SHARD, r_in, l)
        update("w_out", D_MODEL, r_out, l)
        update("conv_pw_w", CONV_W, r_pw, l)
        after = [prev["w_out"][0], prev["conv_pw_w"][0]]
    late = ggroups[0][0]
    r_in, = _xchg_wait("grads_wait0_0", late[0], late[1], after)
    update("w_in", W_IN_SHARD, r_in, 0)
    for name_, _, _ in big:
        res[name_] = [o.reshape(weights[name_].shape) for o in prev[name_]]

    big_done = [prev[n][0] for n, _, _ in big]
    r_512, r_64 = _xchg_wait("vector_grads_wait", vgroup, vstarted[0], big_done)
    r_small, r_2048 = _xchg_wait("small_grads_wait", sgroup, sstarted[0], big_done)
    r_rep = [r_512, r_2048, r_64]
    sshapes = [weights[n].shape for n in shard_small_names]
    outs = _adamw("adamw_small_sharded", *shard_wmv, r_small)
    for k, o in enumerate(outs):
        for n, a in zip(shard_small_names, _unpack_rows(o, sshapes)):
            res.setdefault(n, [None] * 4)[k] = a

    outs = [_adamw(f"adamw_small_w{tag}", rep_wmv[0][ci], rep_wmv[1][ci], rep_wmv[2][ci], r_rep[ci])
            for ci, tag in enumerate(("512", "2048", "64"))]
    for k in range(4):
        for n, a in _unpack_small(outs[0][k], outs[1][k], outs[2][k]).items():
            res.setdefault(n, [None] * 4)[k] = a

    order = ["meta_tokens", "ln_in_g", "ln_in_b", "w_in", "conv_dw_w", "conv_dw_b", "conv_ln_g", "conv_ln_b",
             "conv_pw_w", "conv_pw_b", "attn_sinks", "lru_conv_w", "lru_conv_b", "lru_wa", "lru_ba", "lru_wx",
             "lru_bx", "lru_lambda", "w_out", "ln_post_g", "ln_post_b"]
    return (loss, grad_x,
            *[res[n][0] for n in order], *[res[n][1] for n in order],
            *[res[n][2] for n in order], *[res[n][3] for n in order])
```

```python
import functools
import math

import numpy as np
import jax
import jax.numpy as jnp
from jax import lax
from jax.experimental import pallas as pl
from jax.experimental.pallas import tpu as pltpu

F32 = jnp.float32
BF16 = jnp.bfloat16

D_MODEL = 2048
DEPTH = 2
N_META = 16
TB = 128
PAD0 = TB - N_META
CONV_W = 512
CONV_K = 31
HEAD_DIM = 64
N_HEADS = 16
N_KV = 4
GROUP = 4
ATT_W = 1024
KV_W = 256
ROT_DIM = 16
ROPE_THETA = 500000.0
LRU_W = 512
LRU_HEADS = 8
LRU_CONV_K = 4
LRU_C = 8.0
IN_TOTAL = 5120
N_DEV = 8
W_IN_SHARD = IN_TOTAL // N_DEV
LN_EPS = 1e-5
ALPHA = (2.0 * DEPTH) ** 0.25
NEG_INF = -1e30
ATT_SCALE = HEAD_DIM ** -0.5

ADAM_LR = 0.001
ADAM_B1 = 0.9
ADAM_B2 = 0.999
ADAM_EPS = 1e-08
ADAM_WD = 0.01
ADAM_STEP = 10

VMEM_LIMIT = 56 * 1024 * 1024
PACK_QUANTUM = 256 * 128

COL_CV, COL_CG, COL_CGATE = 0, 1, 2
COL_Q0 = 3
COL_K256 = 10
COL_V256 = 11
COL_AGATE1024 = 3
COL_RX, COL_RGATE = 8, 9
YC_CONV, YC_LRU = 2, 3


def _cp(n_axes, vmem=VMEM_LIMIT):
    return pltpu.CompilerParams(dimension_semantics=("arbitrary",) * n_axes, vmem_limit_bytes=vmem)


def _row_tile(lp, max_blocks):
    nb = lp // TB
    d = max(k for k in range(1, max_blocks + 1) if nb % k == 0)
    return TB * d


def _sig(x):
    return jax.nn.sigmoid(x)


def _dsilu(x, s):
    return s * (1.0 + x * (1.0 - s))


def _ln_core(z):
    mu = jnp.mean(z, axis=-1, keepdims=True)
    zc = z - mu
    var = jnp.mean(zc * zc, axis=-1, keepdims=True)
    rstd = lax.rsqrt(var + LN_EPS)
    return zc * rstd, rstd


def _ln_bwd_core(dy, xh, rstd, g):
    dxh = dy * g
    m1 = jnp.mean(dxh, axis=-1, keepdims=True)
    m2 = jnp.mean(dxh * xh, axis=-1, keepdims=True)
    return rstd * (dxh - m1 - xh * m2)


def _row_ids(shape, base):
    return lax.broadcasted_iota(jnp.int32, shape, 0) + base


def _colsum(x):
    return jnp.sum(x, axis=0, keepdims=True)


def _dot(a, b, dims):
    return lax.dot_general(a, b, (dims, ((), ())), preferred_element_type=F32)


NN = ((1,), (0,))
NT = ((1,), (1,))
TN = ((0,), (0,))


def _embed(x, meta_full, g, b):
    s = x.shape[1]
    lp = s + TB
    nb = lp // TB

    def body(x_ref, m_ref, g_ref, b_ref, o_ref, hb_ref):
        i = pl.program_id(0)

        @pl.when(i == 0)
        def _():
            o_ref[0:PAD0, :] = jnp.zeros((PAD0, D_MODEL), F32)
            o_ref[PAD0:TB, :] = m_ref[...]

        @pl.when(i > 0)
        def _():
            o_ref[...] = x_ref[...]

        xh, _ = _ln_core(o_ref[...])
        h = xh * g_ref[...] + b_ref[...]
        rows = _row_ids(h.shape, i * TB)
        hb_ref[...] = jnp.where(rows >= PAD0, h, 0.0).astype(BF16)

    return pl.pallas_call(
        body, name="embed", grid=(nb,),
        in_specs=[pl.BlockSpec((None, TB, D_MODEL), lambda i: (0, jnp.maximum(i - 1, 0), 0)),
                  pl.BlockSpec((N_META, D_MODEL), lambda i: (0, 0)),
                  pl.BlockSpec((1, D_MODEL), lambda i: (0, 0)),
                  pl.BlockSpec((1, D_MODEL), lambda i: (0, 0))],
        out_specs=[pl.BlockSpec((TB, D_MODEL), lambda i: (i, 0)),
                   pl.BlockSpec((TB, D_MODEL), lambda i: (i, 0))],
        out_shape=[jax.ShapeDtypeStruct((lp, D_MODEL), F32),
                   jax.ShapeDtypeStruct((lp, D_MODEL), BF16)],
        compiler_params=_cp(1),
    )(x, meta_full, g, b)


def _loss_head(z, target, g, b):
    lp = z.shape[0]
    nb = lp // TB

    def body(z_ref, t_ref, g_ref, b_ref, dz_ref, st_ref, loss_ref):
        i = pl.program_id(0)

        @pl.when(i == 0)
        def _():
            st_ref[...] = jnp.zeros(st_ref.shape, F32)
            loss_ref[...] = jnp.zeros(loss_ref.shape, F32)
            dz_ref[...] = jnp.zeros(dz_ref.shape, F32)

        @pl.when(i > 0)
        def _():
            xh, rstd = _ln_core(z_ref[...])
            gg = g_ref[...]
            y = xh * gg + b_ref[...]
            e = y - t_ref[...]
            part = 0.5 * jnp.sum(jnp.mean(e * e, axis=-1, keepdims=True), axis=0, keepdims=True)
            loss_ref[...] += jnp.broadcast_to(part, loss_ref.shape)
            dy = e / float(D_MODEL)
            st_ref[0:1, :] += _colsum(dy * xh)
            st_ref[1:2, :] += _colsum(dy)
            dz_ref[...] = _ln_bwd_core(dy, xh, rstd, gg)

    return pl.pallas_call(
        body, name="loss_head", grid=(nb,),
        in_specs=[pl.BlockSpec((TB, D_MODEL), lambda i: (i, 0)),
                  pl.BlockSpec((None, TB, D_MODEL), lambda i: (0, jnp.maximum(i - 1, 0), 0)),
                  pl.BlockSpec((1, D_MODEL), lambda i: (0, 0)),
                  pl.BlockSpec((1, D_MODEL), lambda i: (0, 0))],
        out_specs=[pl.BlockSpec((TB, D_MODEL), lambda i: (i, 0)),
                   pl.BlockSpec((8, D_MODEL), lambda i: (0, 0)),
                   pl.BlockSpec((8, 128), lambda i: (0, 0))],
        out_shape=[jax.ShapeDtypeStruct((lp, D_MODEL), F32),
                   jax.ShapeDtypeStruct((8, D_MODEL), F32),
                   jax.ShapeDtypeStruct((8, 128), F32)],
        compiler_params=_cp(1),
    )(z, target, g, b)


def _ln_bwd(name, dh, z, g):
    lp = z.shape[0]
    tr = _row_tile(lp, 3)

    def body(dh_ref, z_ref, g_ref, dz_ref, st_ref):
        i = pl.program_id(0)

        @pl.when(i == 0)
        def _():
            st_ref[...] = jnp.zeros(st_ref.shape, F32)

        xh, rstd = _ln_core(z_ref[...])
        rows = _row_ids(xh.shape, i * tr)
        dy = jnp.where(rows >= PAD0, dh_ref[...], 0.0)
        st_ref[0:1, :] += _colsum(dy * xh)
        st_ref[1:2, :] += _colsum(dy)
        dz_ref[...] = _ln_bwd_core(dy, xh, rstd, g_ref[...])

    return pl.pallas_call(
        body, name=name, grid=(lp // tr,),
        in_specs=[pl.BlockSpec((tr, D_MODEL), lambda i: (i, 0)),
                  pl.BlockSpec((tr, D_MODEL), lambda i: (i, 0)),
                  pl.BlockSpec((1, D_MODEL), lambda i: (0, 0))],
        out_specs=[pl.BlockSpec((tr, D_MODEL), lambda i: (i, 0)),
                   pl.BlockSpec((8, D_MODEL), lambda i: (0, 0))],
        out_shape=[jax.ShapeDtypeStruct((lp, D_MODEL), F32),
                   jax.ShapeDtypeStruct((8, D_MODEL), F32)],
        compiler_params=_cp(1),
    )(dh, z, g)


def _ln_bwd_input(dh, z, g):
    lp = z.shape[0]
    nb = lp // TB
    s = lp - TB

    def body(dh_ref, z_ref, g_ref, gx_ref, gm_ref, st_ref):
        i = pl.program_id(0)

        @pl.when(i == 0)
        def _():
            st_ref[...] = jnp.zeros(st_ref.shape, F32)

        xh, rstd = _ln_core(z_ref[...])
        rows = _row_ids(xh.shape, i * TB)
        dy = jnp.where(rows >= PAD0, dh_ref[...], 0.0)
        st_ref[0:1, :] += _colsum(dy * xh)
        st_ref[1:2, :] += _colsum(dy)
        dz = _ln_bwd_core(dy, xh, rstd, g_ref[...])
        gx_ref[...] = dz

        @pl.when(i == 0)
        def _():
            gm_ref[...] = dz[PAD0:TB, :]

    return pl.pallas_call(
        body, name="ln_in_bwd", grid=(nb,),
        in_specs=[pl.BlockSpec((TB, D_MODEL), lambda i: (i, 0)),
                  pl.BlockSpec((TB, D_MODEL), lambda i: (i, 0)),
                  pl.BlockSpec((1, D_MODEL), lambda i: (0, 0))],
        out_specs=[pl.BlockSpec((None, TB, D_MODEL), lambda i: (0, jnp.maximum(i - 1, 0), 0)),
                   pl.BlockSpec((N_META, D_MODEL), lambda i: (0, 0)),
                   pl.BlockSpec((8, D_MODEL), lambda i: (0, 0))],
        out_shape=[jax.ShapeDtypeStruct((1, s, D_MODEL), F32),
                   jax.ShapeDtypeStruct((N_META, D_MODEL), F32),
                   jax.ShapeDtypeStruct((8, D_MODEL), F32)],
        compiler_params=_cp(1),
    )(dh, z, g)


def _mm_proj(name, hb, wg_in):
    lp = hb.shape[0]
    tm = lp // 3

    def body(a_ref, b_ref, o_ref):
        b = jnp.concatenate([b_ref[0], b_ref[1]], axis=1)
        o_ref[...] = _dot(a_ref[...], b, NN)

    return pl.pallas_call(
        body, name=name, grid=(3, N_DEV // 2),
        in_specs=[pl.BlockSpec((tm, D_MODEL), lambda i, j: (i, 0)),
                  pl.BlockSpec((2, D_MODEL, W_IN_SHARD), lambda i, j: (j, 0, 0))],
        out_specs=pl.BlockSpec((tm, 2 * W_IN_SHARD), lambda i, j: (i, j)),
        out_shape=jax.ShapeDtypeStruct((lp, IN_TOTAL), F32),
        compiler_params=_cp(2),
    )(hb, wg_in)


def _mm_out(name, ycat, wout, z, g, b, g2, b2):
    lp = ycat.shape[0]
    tm = lp // 6

    def body(a_ref, w_ref, z_ref, g_ref, b_ref, g2_ref, b2_ref, o_ref, hb_ref):
        i = pl.program_id(0)
        xh, _ = _ln_core(z_ref[...])
        h = xh * g_ref[...] + b_ref[...]
        live = _row_ids(h.shape, i * tm) >= PAD0
        h = jnp.where(live, h, 0.0)
        zn = ALPHA * h + _dot(a_ref[...], w_ref[...], NN)
        o_ref[...] = zn
        xh2, _ = _ln_core(zn)
        hb_ref[...] = jnp.where(live, xh2 * g2_ref[...] + b2_ref[...], 0.0).astype(BF16)

    vec = pl.BlockSpec((1, D_MODEL), lambda i: (0, 0))
    return pl.pallas_call(
        body, name=name, grid=(6,),
        in_specs=[pl.BlockSpec((tm, D_MODEL), lambda i: (i, 0)),
                  pl.BlockSpec((D_MODEL, D_MODEL), lambda i: (0, 0), pipeline_mode=pl.Buffered(1)),
                  pl.BlockSpec((tm, D_MODEL), lambda i: (i, 0)),
                  vec, vec, vec, vec],
        out_specs=[pl.BlockSpec((tm, D_MODEL), lambda i: (i, 0)),
                   pl.BlockSpec((tm, D_MODEL), lambda i: (i, 0))],
        out_shape=[jax.ShapeDtypeStruct((lp, D_MODEL), F32),
                   jax.ShapeDtypeStruct((lp, D_MODEL), BF16)],
        compiler_params=_cp(1),
    )(ycat, wout, z, g, b, g2, b2)


def _mm_dycat(name, dz, wout, dep):
    lp = dz.shape[0]
    tm = lp // 6

    def body(a_ref, w_ref, dep_ref, o_ref):
        del dep_ref
        o_ref[...] = _dot(a_ref[...].astype(BF16), w_ref[...], NT).astype(BF16)

    return pl.pallas_call(
        body, name=name, grid=(6,),
        in_specs=[pl.BlockSpec((tm, D_MODEL), lambda i: (i, 0)),
                  pl.BlockSpec((D_MODEL, D_MODEL), lambda i: (0, 0), pipeline_mode=pl.Buffered(1)),
                  pl.BlockSpec(memory_space=pl.ANY)],
        out_specs=pl.BlockSpec((tm, D_MODEL), lambda i: (i, 0)),
        out_shape=jax.ShapeDtypeStruct((lp, D_MODEL), BF16),
        compiler_params=_cp(1),
    )(dz, wout, dep)


def _mm_dwout(name, ycat, dz):
    lp = ycat.shape[0]
    tk = _row_tile(lp, 11)
    nk = lp // tk
    half = D_MODEL // 2

    def body(a_ref, b_ref, o_ref, acc_ref):
        k = pl.program_id(1)

        @pl.when(k == 0)
        def _():
            acc_ref[...] = jnp.zeros(acc_ref.shape, F32)

        acc_ref[...] += _dot(a_ref[...], b_ref[...].astype(BF16), TN)

        @pl.when(k == nk - 1)
        def _():
            o_ref[...] = acc_ref[...].astype(BF16)

    return pl.pallas_call(
        body, name=name, grid=(2, nk),
        in_specs=[pl.BlockSpec((tk, half), lambda h, k: (k, h)),
                  pl.BlockSpec((tk, D_MODEL), lambda h, k: (k, 0))],
        out_specs=pl.BlockSpec((half, D_MODEL), lambda h, k: (h, 0)),
        out_shape=jax.ShapeDtypeStruct((D_MODEL, D_MODEL), BF16),
        scratch_shapes=[pltpu.VMEM((half, D_MODEL), F32)],
        compiler_params=_cp(2),
    )(ycat, dz)


def _mm_dwin(name, hb, dproj, dep):
    lp = hb.shape[0]
    tk = _row_tile(lp, 11)
    nk = lp // tk

    def body(a_ref, b_ref, dep_ref, o_ref, acc_ref):
        del dep_ref
        k = pl.program_id(1)

        @pl.when(k == 0)
        def _():
            acc_ref[...] = jnp.zeros(acc_ref.shape, F32)

        acc_ref[...] += _dot(a_ref[...], b_ref[...], TN)

        @pl.when(k == nk - 1)
        def _():
            o_ref[0] = acc_ref[:, 0:W_IN_SHARD].astype(BF16)
            o_ref[1] = acc_ref[:, W_IN_SHARD:2 * W_IN_SHARD].astype(BF16)

    return pl.pallas_call(
        body, name=name, grid=(4, nk),
        in_specs=[pl.BlockSpec((tk, D_MODEL), lambda j, k: (k, 0)),
                  pl.BlockSpec((tk, 2 * W_IN_SHARD), lambda j, k: (k, j)),
                  pl.BlockSpec(memory_space=pl.ANY)],
        out_specs=pl.BlockSpec((2, D_MODEL, W_IN_SHARD), lambda j, k: (j, 0, 0)),
        out_shape=jax.ShapeDtypeStruct((N_DEV, D_MODEL, W_IN_SHARD), BF16),
        scratch_shapes=[pltpu.VMEM((D_MODEL, 2 * W_IN_SHARD), F32)],
        compiler_params=_cp(2),
    )(hb, dproj, dep)


def _mm_dh(name, dproj, wg_in, dz, dep):
    lp = dproj.shape[0]
    tm = lp // 6

    def body(a_ref, w_ref, dz_ref, dep_ref, o_ref, acc_ref):
        del dep_ref
        k = pl.program_id(1)

        @pl.when(k == 0)
        def _():
            acc_ref[...] = jnp.zeros(acc_ref.shape, F32)

        w = jnp.concatenate([w_ref[0], w_ref[1]], axis=1)
        acc_ref[...] += _dot(a_ref[...], w, NT)

        @pl.when(k == N_DEV // 2 - 1)
        def _():
            o_ref[...] = acc_ref[...] + ALPHA * dz_ref[...]

    return pl.pallas_call(
        body, name=name, grid=(6, N_DEV // 2),
        in_specs=[pl.BlockSpec((tm, 2 * W_IN_SHARD), lambda i, k: (i, k)),
                  pl.BlockSpec((2, D_MODEL, W_IN_SHARD), lambda i, k: (k, 0, 0)),
                  pl.BlockSpec((tm, D_MODEL), lambda i, k: (i, 0)),
                  pl.BlockSpec(memory_space=pl.ANY)],
        out_specs=pl.BlockSpec((tm, D_MODEL), lambda i, k: (i, 0)),
        out_shape=jax.ShapeDtypeStruct((lp, D_MODEL), F32),
        scratch_shapes=[pltpu.VMEM((tm, D_MODEL), F32)],
        compiler_params=_cp(2),
    )(dproj, wg_in, dz, dep)


SUB = 128


def _shift_plan(cat, n_shift, base):
    rolled = [cat] + [pltpu.roll(cat, b, axis=0) for b in range(1, 8)]
    return [(rolled[s % 8], base - 8 * (s // 8)) for s in range(n_shift)]


def _tap_sum(w_ref, plan, rows, init=None):
    blocks = []
    for r0 in range(0, rows, SUB):
        row = []
        for c0 in range(0, CONV_W, SUB):
            acc = (jnp.zeros((SUB, SUB), F32) if init is None
                   else jnp.broadcast_to(init[:, c0:c0 + SUB], (SUB, SUB)))
            for k, (arr, off) in enumerate(plan):
                acc = acc + w_ref[k:k + 1, c0:c0 + SUB] * arr[off + r0:off + r0 + SUB, c0:c0 + SUB]
            row.append(acc)
        blocks.append(jnp.concatenate(row, axis=1))
    return jnp.concatenate(blocks, axis=0)


def _tap_grads(dw_ref, dy, plan, rows):
    for c0 in range(0, CONV_W, SUB):
        dys = [dy[r0:r0 + SUB, c0:c0 + SUB] for r0 in range(0, rows, SUB)]
        for k, (arr, off) in enumerate(plan):
            part = None
            for ri, r0 in enumerate(range(0, rows, SUB)):
                prod = dys[ri] * arr[off + r0:off + r0 + SUB, c0:c0 + SUB]
                for i in range(SUB // 8):
                    piece = prod[8 * i:8 * i + 8, :]
                    part = piece if part is None else part + piece
            dw_ref[k:k + 1, c0:c0 + SUB] += jnp.sum(part, axis=0, keepdims=True)


CONV_HALO = 32


def _conv_chain(j, tb, cv_ref, cg_ref, cvp_ref, cgp_ref, wdw_ref, vec_ref, wpw_ref, c1_ref=None):
    cv = cv_ref[...]
    sg = _sig(cg_ref[...])
    c0 = cv * sg
    c0p = jnp.where(j > 0, cvp_ref[...] * _sig(cgp_ref[...]), 0.0)
    cat = jnp.concatenate([c0p, c0], axis=0)
    shifts = _shift_plan(cat, CONV_K, CONV_HALO)
    taps = [shifts[CONV_K - 1 - k] for k in range(CONV_K)]
    if c1_ref is None:
        c1 = _tap_sum(wdw_ref, taps, tb, init=vec_ref[0:1, :])
    else:
        c1 = c1_ref[...]
    xh, rstd = _ln_core(c1)
    c2 = xh * vec_ref[1:2, :] + vec_ref[2:3, :]
    s2 = _sig(c2)
    c3 = c2 * s2
    c4 = _dot(c3.astype(BF16), wpw_ref[...], NN) + vec_ref[3:4, :]
    return dict(cv=cv, sg=sg, taps=taps, c1=c1, xh=xh, rstd=rstd, c2=c2, s2=s2, c3=c3, c4=c4)


def _conv_in_specs(jmap, tb):
    def cur(col):
        return pl.BlockSpec((tb, 512), lambda n: (jmap(n), col))

    def prev(col):
        return pl.BlockSpec((CONV_HALO, 512),
                            lambda n: (jnp.maximum(jmap(n) * (tb // CONV_HALO) - 1, 0), col))

    return [cur(COL_CV), cur(COL_CG), prev(COL_CV), prev(COL_CG), cur(COL_CGATE)]


def _conv_param_specs():
    return [pl.BlockSpec((32, CONV_W), lambda n: (0, 0)),
            pl.BlockSpec((8, CONV_W), lambda n: (0, 0)),
            pl.BlockSpec((CONV_W, CONV_W), lambda n: (0, 0))]


def _conv_fwd(name, proj, wdw, vec, wpw):
    lp = proj.shape[0]
    tb = _row_tile(lp, 3)
    nb = lp // tb

    def body(cv_ref, cg_ref, cvp_ref, cgp_ref, gate_ref, wdw_ref, vec_ref, wpw_ref, o_ref, c1_ref):
        j = pl.program_id(0)
        c = _conv_chain(j, tb, cv_ref, cg_ref, cvp_ref, cgp_ref, wdw_ref, vec_ref, wpw_ref)
        gate = gate_ref[...]
        o_ref[...] = (c["c4"] * (gate * _sig(gate))).astype(BF16)
        c1_ref[...] = c["c1"]

    return pl.pallas_call(
        body, name=name, grid=(nb,),
        in_specs=_conv_in_specs(lambda n: n, tb) + _conv_param_specs(),
        out_specs=[pl.BlockSpec((tb, 512), lambda n: (n, YC_CONV)),
                   pl.BlockSpec((tb, CONV_W), lambda n: (n, 0))],
        out_shape=[jax.ShapeDtypeStruct((lp, D_MODEL), BF16),
                   jax.ShapeDtypeStruct((lp, CONV_W), F32)],
        compiler_params=_cp(1),
    )(proj, proj, proj, proj, proj, wdw, vec, wpw)


def _conv_bwd(name, proj, dycat, c1, wdw, vec, wpw):
    lp = proj.shape[0]
    tb = _row_tile(lp, 3)
    nb = lp // tb
    halo = CONV_HALO

    def body(cv_ref, cg_ref, cvp_ref, cgp_ref, gate_ref, dy_ref, c1_ref, wdw_ref, vec_ref, wpw_ref,
             dp_ref, dwdw_ref, dvec_ref, dwpw_ref, carry_ref):
        n = pl.program_id(0)
        j = nb - 1 - n

        @pl.when(n == 0)
        def _():
            carry_ref[...] = jnp.zeros(carry_ref.shape, F32)
            dwdw_ref[...] = jnp.zeros(dwdw_ref.shape, F32)
            dvec_ref[...] = jnp.zeros(dvec_ref.shape, F32)
            dwpw_ref[...] = jnp.zeros(dwpw_ref.shape, F32)

        c = _conv_chain(j, tb, cv_ref, cg_ref, cvp_ref, cgp_ref, wdw_ref, vec_ref, wpw_ref, c1_ref)
        dy = dy_ref[...].astype(F32)
        gate = gate_ref[...]
        sgate = _sig(gate)
        dc4 = dy * (gate * sgate)
        dgate = dy * c["c4"] * _dsilu(gate, sgate)
        dc4b = dc4.astype(BF16)
        dvec_ref[3:4, :] += _colsum(dc4)
        dwpw_ref[...] += _dot(c["c3"].astype(BF16), dc4b, TN)
        dc3 = _dot(dc4b, wpw_ref[...], NT)
        dc2 = dc3 * _dsilu(c["c2"], c["s2"])
        dvec_ref[1:2, :] += _colsum(dc2 * c["xh"])
        dvec_ref[2:3, :] += _colsum(dc2)
        dc1 = _ln_bwd_core(dc2, c["xh"], c["rstd"], vec_ref[1:2, :])
        dvec_ref[0:1, :] += _colsum(dc1)
        _tap_grads(dwdw_ref, dc1, c["taps"], tb)
        dcat = jnp.concatenate([dc1, carry_ref[...]], axis=0)
        total = tb + halo
        up = [dcat] + [pltpu.roll(dcat, total - b, axis=0) for b in range(1, 8)]
        ahead = [(up[(CONV_K - 1 - k) % 8], 8 * ((CONV_K - 1 - k) // 8)) for k in range(CONV_K)]
        dc0 = _tap_sum(wdw_ref, ahead, tb)
        carry_ref[...] = dc1[0:halo, :]
        sg = c["sg"]
        dcv = dc0 * sg
        dcg = dc0 * c["cv"] * sg * (1.0 - sg)
        dp_ref[:, 0:512] = dcv.astype(BF16)
        dp_ref[:, 512:1024] = dcg.astype(BF16)
        dp_ref[:, 1024:1536] = dgate.astype(BF16)

    jmap = lambda n: nb - 1 - n
    return pl.pallas_call(
        body, name=name, grid=(nb,),
        in_specs=(_conv_in_specs(jmap, tb)
                  + [pl.BlockSpec((tb, 512), lambda n: (jmap(n), YC_CONV)),
                     pl.BlockSpec((tb, CONV_W), lambda n: (jmap(n), 0))]
                  + _conv_param_specs()),
        out_specs=[pl.BlockSpec((tb, 1536), lambda n: (jmap(n), 0)),
                   pl.BlockSpec((32, CONV_W), lambda n: (0, 0)),
                   pl.BlockSpec((8, CONV_W), lambda n: (0, 0)),
                   pl.BlockSpec((CONV_W, CONV_W), lambda n: (0, 0))],
        out_shape=[jax.ShapeDtypeStruct((lp, IN_TOTAL), BF16),
                   jax.ShapeDtypeStruct((32, CONV_W), F32),
                   jax.ShapeDtypeStruct((8, CONV_W), F32),
                   jax.ShapeDtypeStruct((CONV_W, CONV_W), F32)],
        scratch_shapes=[pltpu.VMEM((halo, CONV_W), F32)],
        compiler_params=_cp(1),
    )(proj, proj, proj, proj, proj, dycat, c1, wdw, vec, wpw)


def _rope_tables(lp):
    half = ROT_DIM // 2
    inv_freq = ROPE_THETA ** (-jnp.arange(half, dtype=F32) / half)
    pos = (jnp.arange(lp, dtype=jnp.int32) - PAD0).astype(F32)
    ang = pos[:, None] * inv_freq[None, :]
    cos, sin = jnp.cos(ang), jnp.sin(ang)
    ones = jnp.ones((lp, HEAD_DIM - ROT_DIM), F32)
    zeros = jnp.zeros((lp, HEAD_DIM - ROT_DIM), F32)
    zh = jnp.zeros((lp, half), F32)
    c = jnp.concatenate([cos, cos, ones], axis=1)
    sa = jnp.concatenate([-sin, zh, zeros], axis=1)
    sb = jnp.concatenate([zh, sin, zeros], axis=1)
    tile = lambda t: jnp.tile(t, (1, KV_W // HEAD_DIM))
    return tile(c), tile(sa), tile(sb)


def _rot(x, c, sa, sb):
    w = x.shape[1]
    return x * c + pltpu.roll(x, w - 8, axis=1) * sa + pltpu.roll(x, 8, axis=1) * sb


def _rot_t(dy, c, sa, sb):
    w = dy.shape[1]
    return dy * c + pltpu.roll(dy * sa, 8, axis=1) + pltpu.roll(dy * sb, w - 8, axis=1)


def _rope_fwd(name, proj, tabs):
    lp = proj.shape[0]
    tr = _row_tile(lp, 11)

    def body(q0_ref, q1_ref, k_ref, c_ref, sa_ref, sb_ref, qr_ref, kr_ref):
        c, sa, sb = c_ref[...], sa_ref[...], sb_ref[...]
        c2 = jnp.concatenate([c, c], axis=1)
        sa2 = jnp.concatenate([sa, sa], axis=1)
        sb2 = jnp.concatenate([sb, sb], axis=1)
        qr_ref[:, 0:512] = (_rot(q0_ref[...], c2, sa2, sb2) * ATT_SCALE).astype(BF16)
        qr_ref[:, 512:1024] = (_rot(q1_ref[...], c2, sa2, sb2) * ATT_SCALE).astype(BF16)
        kr_ref[...] = _rot(k_ref[...], c, sa, sb).astype(BF16)

    tab = pl.BlockSpec((tr, KV_W), lambda i: (i, 0))
    return pl.pallas_call(
        body, name=name, grid=(lp // tr,),
        in_specs=[pl.BlockSpec((tr, 512), lambda i: (i, COL_Q0)),
                  pl.BlockSpec((tr, 512), lambda i: (i, COL_Q0 + 1)),
                  pl.BlockSpec((tr, KV_W), lambda i: (i, COL_K256)),
                  tab, tab, tab],
        out_specs=[pl.BlockSpec((tr, ATT_W), lambda i: (i, 0)),
                   pl.BlockSpec((tr, KV_W), lambda i: (i, 0))],
        out_shape=[jax.ShapeDtypeStruct((lp, ATT_W), BF16),
                   jax.ShapeDtypeStruct((lp, KV_W), BF16)],
        compiler_params=_cp(1),
    )(proj, proj, proj, *tabs)


def _attn_mask(j):
    qi = lax.broadcasted_iota(jnp.int32, (GROUP * TB, 3 * TB), 0) & (TB - 1)
    cc = lax.broadcasted_iota(jnp.int32, (GROUP * TB, 3 * TB), 1)
    jj = cc & (TB - 1)
    is_meta = jj >= PAD0
    p0 = (cc < TB) & is_meta & (j >= 1)
    p1 = (cc >= TB) & (cc < 2 * TB) & (jj > qi) & (j >= 2)
    p2 = (cc >= 2 * TB) & (jj <= qi) & ((j >= 1) | is_meta)
    return p0 | p1 | p2


def _lane_group(rows):
    return lax.broadcasted_iota(jnp.int32, (rows, KV_W), 1) // HEAD_DIM


def _stack_heads(x, kv, lgq):
    parts = []
    for g in range(GROUP):
        sh = ((kv - g) % GROUP) * HEAD_DIM
        moved = x if sh == 0 else pltpu.roll(x, sh, axis=1)
        parts.append(jnp.where(lgq == kv, moved, 0.0))
    return jnp.concatenate(parts, axis=0).astype(BF16)


def _unstack_heads(r, kv):
    out = None
    for g in range(GROUP):
        blk = r[g * TB:(g + 1) * TB, :]
        sh = ((g - kv) % GROUP) * HEAD_DIM
        blk = blk if sh == 0 else pltpu.roll(blk, sh, axis=1)
        out = blk if out is None else out + blk
    return out


def _sink_column(sinks, kv):
    lane = lax.broadcasted_iota(jnp.int32, (1, 128), 1)
    cols = []
    for g in range(GROUP):
        sg = jnp.sum(jnp.where(lane == kv * GROUP + g, sinks, 0.0), axis=1, keepdims=True)
        cols.append(jnp.broadcast_to(sg, (TB, 1)))
    return jnp.concatenate(cols, axis=0)


def _attn_kv(kall, vall, lg, kv):
    km = jnp.where(lg == kv, kall, 0.0).astype(BF16)
    vm = jnp.where(lg == kv, vall, 0.0).astype(BF16)
    ones = jnp.where(lg == kv, 1.0, 0.0).astype(BF16)
    return km, vm, ones


def _attn_specs(jmap):
    blk = lambda col: pl.BlockSpec((TB, KV_W), lambda n: (jmap(n), col))
    prv = lambda col: pl.BlockSpec((TB, KV_W), lambda n: (jnp.maximum(jmap(n) - 1, 0), col))
    met = lambda col: pl.BlockSpec((TB, KV_W), lambda n: (0, col))
    return dict(
        qr=pl.BlockSpec((TB, ATT_W), lambda n: (jmap(n), 0)),
        k=[met(0), prv(0), blk(0)],
        v=[met(COL_V256), prv(COL_V256), blk(COL_V256)],
        gate=pl.BlockSpec((TB, ATT_W), lambda n: (jmap(n), COL_AGATE1024)),
        sinks=pl.BlockSpec((8, 128), lambda n: (0, 0)),
    )


def _attn_fwd(name, qr, kr, proj, sinks_row, ycat):
    lp = proj.shape[0]
    nb = lp // TB
    sp = _attn_specs(lambda n: n)

    def body(qr_ref, km_ref, kp_ref, kc_ref, vm_ref, vp_ref, vc_ref, gate_ref, sink_ref, yin_ref, o_ref):
        del yin_ref
        j = pl.program_id(0)
        valid = _attn_mask(j)
        kall = jnp.concatenate([km_ref[...], kp_ref[...], kc_ref[...]], axis=0).astype(F32)
        vall = jnp.concatenate([vm_ref[...], vp_ref[...], vc_ref[...]], axis=0)
        lg = _lane_group(3 * TB)
        lgq = _lane_group(TB)
        lg4 = _lane_group(GROUP * TB)
        sinks = sink_ref[0:1, :]
        heads = range(N_KV)
        cols = [slice(kv * KV_W, (kv + 1) * KV_W) for kv in heads]
        kvo = [_attn_kv(kall, vall, lg, kv) for kv in heads]
        qst = [_stack_heads(qr_ref[:, cols[kv]].astype(F32), kv, lgq) for kv in heads]
        s = [jnp.where(valid, _dot(qst[kv], kvo[kv][0], NT), NEG_INF) for kv in heads]
        eb, es = [], []
        for kv in heads:
            sinkcol = _sink_column(sinks, kv)
            m = jnp.maximum(jnp.max(s[kv], axis=-1, keepdims=True), sinkcol)
            eb.append(jnp.exp(s[kv] - m).astype(BF16))
            es.append(jnp.exp(sinkcol - m))
        r = [_dot(eb[kv], kvo[kv][1], NN) for kv in heads]
        inv = [1.0 / (_dot(eb[kv], kvo[kv][2], NN) + es[kv]) for kv in heads]
        for kv in heads:
            out = jnp.where(lg4 == kv, r[kv] * inv[kv], 0.0)
            gate = gate_ref[:, cols[kv]]
            o_ref[:, cols[kv]] = (_unstack_heads(out, kv) * (gate * _sig(gate))).astype(BF16)

    return pl.pallas_call(
        body, name=name, grid=(nb,),
        in_specs=[sp["qr"]] + sp["k"] + sp["v"] + [sp["gate"], sp["sinks"],
                                                   pl.BlockSpec(memory_space=pl.ANY)],
        out_specs=pl.BlockSpec((TB, ATT_W), lambda n: (n, 0)),
        out_shape=jax.ShapeDtypeStruct((lp, D_MODEL), BF16),
        input_output_aliases={9: 0},
        compiler_params=_cp(1),
    )(qr, kr, kr, kr, proj, proj, proj, proj, sinks_row, ycat)


def _attn_bwd(name, qr, kr, proj, sinks_row, dycat, dep):
    lp = proj.shape[0]
    nb = lp // TB
    sp = _attn_specs(lambda n: n)

    def body(qr_ref, km_ref, kp_ref, kc_ref, vm_ref, vp_ref, vc_ref, gate_ref, sink_ref, dy_ref, dep_ref,
             dq_ref, dgate_ref, dk_ref, dv_ref, dsink_ref):
        del dep_ref
        j = pl.program_id(0)

        @pl.when(j == 0)
        def _():
            dk_ref[...] = jnp.zeros(dk_ref.shape, F32)
            dv_ref[...] = jnp.zeros(dv_ref.shape, F32)
            dsink_ref[...] = jnp.zeros(dsink_ref.shape, F32)

        valid = _attn_mask(j)
        kall = jnp.concatenate([km_ref[...], kp_ref[...], kc_ref[...]], axis=0).astype(F32)
        vall = jnp.concatenate([vm_ref[...], vp_ref[...], vc_ref[...]], axis=0)
        lg = _lane_group(3 * TB)
        lgq = _lane_group(TB)
        lg4 = _lane_group(GROUP * TB)
        sinks = sink_ref[0:1, :]
        lane = lax.broadcasted_iota(jnp.int32, (1, 128), 1)
        dsink = jnp.zeros((1, 128), F32)
        heads = range(N_KV)
        cols = [slice(kv * KV_W, (kv + 1) * KV_W) for kv in heads]
        kvo = [_attn_kv(kall, vall, lg, kv) for kv in heads]
        qst = [_stack_heads(qr_ref[:, cols[kv]].astype(F32), kv, lgq) for kv in heads]
        s = [jnp.where(valid, _dot(qst[kv], kvo[kv][0], NT), NEG_INF) for kv in heads]
        eb, es = [], []
        for kv in heads:
            sinkcol = _sink_column(sinks, kv)
            m = jnp.maximum(jnp.max(s[kv], axis=-1, keepdims=True), sinkcol)
            eb.append(jnp.exp(s[kv] - m).astype(BF16))
            es.append(jnp.exp(sinkcol - m))
        r = [_dot(eb[kv], kvo[kv][1], NN) for kv in heads]
        inv = [1.0 / (_dot(eb[kv], kvo[kv][2], NN) + es[kv]) for kv in heads]
        dost, dcol = [], []
        for kv in heads:
            att = _unstack_heads(jnp.where(lg4 == kv, r[kv] * inv[kv], 0.0), kv)
            gate = gate_ref[:, cols[kv]]
            sgate = _sig(gate)
            dy = dy_ref[:, cols[kv]].astype(F32)
            dgate_ref[:, cols[kv]] = (dy * att * _dsilu(gate, sgate)).astype(BF16)
            dsc = dy * (gate * sgate) * _unstack_heads(jnp.where(lg4 == kv, inv[kv], 0.0), kv)
            dost.append(_stack_heads(dsc, kv, lgq))
            dd = dsc * att
            dcol.append(jnp.concatenate(
                [jnp.sum(jnp.where(lgq == g, dd, 0.0), axis=1, keepdims=True) for g in range(GROUP)], axis=0))
        dp = [_dot(dost[kv], kvo[kv][1], NT) for kv in heads]
        ds = []
        for kv in heads:
            ds.append((eb[kv].astype(F32) * (dp[kv] - dcol[kv])).astype(BF16))
            pd = es[kv] * dcol[kv]
            for g in range(GROUP):
                tot = jnp.sum(pd[g * TB:(g + 1) * TB, :], axis=0, keepdims=True)
                dsink = dsink - jnp.where(lane == kv * GROUP + g, tot, 0.0)
        dqs = [_dot(ds[kv], kvo[kv][0], NN) for kv in heads]
        dks = [_dot(ds[kv], qst[kv], TN) for kv in heads]
        dvs = [_dot(eb[kv], dost[kv], TN) for kv in heads]
        for kv in heads:
            dq_ref[:, cols[kv]] = _unstack_heads(dqs[kv], kv)
        dkall = (dks[0] + dks[1]) + (dks[2] + dks[3])
        dvall = (dvs[0] + dvs[1]) + (dvs[2] + dvs[3])
        dsink_ref[0:1, :] += dsink
        prev = pl.multiple_of(jnp.maximum(j - 1, 0) * TB, TB)
        cur = pl.multiple_of(j * TB, TB)
        dk_ref[0:TB, :] += dkall[0:TB]
        dv_ref[0:TB, :] += dvall[0:TB]
        dk_ref[pl.ds(prev, TB), :] += dkall[TB:2 * TB]
        dv_ref[pl.ds(prev, TB), :] += dvall[TB:2 * TB]
        dk_ref[pl.ds(cur, TB), :] += dkall[2 * TB:3 * TB]
        dv_ref[pl.ds(cur, TB), :] += dvall[2 * TB:3 * TB]

    return pl.pallas_call(
        body, name=name, grid=(nb,),
        in_specs=[sp["qr"]] + sp["k"] + sp["v"] + [sp["gate"], sp["sinks"],
                                                   pl.BlockSpec((TB, ATT_W), lambda n: (n, 0)),
                                                   pl.BlockSpec(memory_space=pl.ANY)],
        out_specs=[pl.BlockSpec((TB, ATT_W), lambda n: (n, 0)),
                   pl.BlockSpec((TB, ATT_W), lambda n: (n, 0)),
                   pl.BlockSpec((lp, KV_W), lambda n: (0, 0)),
                   pl.BlockSpec((lp, KV_W), lambda n: (0, 0)),
                   pl.BlockSpec((8, 128), lambda n: (0, 0))],
        out_shape=[jax.ShapeDtypeStruct((lp, ATT_W), F32),
                   jax.ShapeDtypeStruct((lp, ATT_W), BF16),
                   jax.ShapeDtypeStruct((lp, KV_W), F32),
                   jax.ShapeDtypeStruct((lp, KV_W), F32),
                   jax.ShapeDtypeStruct((8, 128), F32)],
        compiler_params=_cp(1),
    )(qr, kr, kr, kr, proj, proj, proj, proj, sinks_row, dycat, dep)


def _attn_assemble(name, dq, dgate, dk, dv, tabs, dproj):
    lp = dq.shape[0]
    tr = _row_tile(lp, 11)

    def body(dq_ref, dg_ref, dk_ref, dv_ref, c_ref, sa_ref, sb_ref, din_ref, o_ref):
        del din_ref
        cidx = pl.program_id(1)
        c, sa, sb = c_ref[...], sa_ref[...], sb_ref[...]

        @pl.when(cidx < 2)
        def _():
            c2 = jnp.concatenate([c, c], axis=1)
            sa2 = jnp.concatenate([sa, sa], axis=1)
            sb2 = jnp.concatenate([sb, sb], axis=1)
            o_ref[...] = (_rot_t(dq_ref[...], c2, sa2, sb2) * ATT_SCALE).astype(BF16)

        @pl.when(cidx == 2)
        def _():
            o_ref[:, 0:KV_W] = _rot_t(dk_ref[...], c, sa, sb).astype(BF16)
            o_ref[:, KV_W:2 * KV_W] = dv_ref[...].astype(BF16)

        @pl.when(cidx > 2)
        def _():
            o_ref[...] = dg_ref[...]

    tab = pl.BlockSpec((tr, KV_W), lambda n, c: (n, 0))
    return pl.pallas_call(
        body, name=name, grid=(lp // tr, 5),
        in_specs=[pl.BlockSpec((tr, 512), lambda n, c: (n, jnp.minimum(c, 1))),
                  pl.BlockSpec((tr, 512), lambda n, c: (n, jnp.clip(c - 3, 0, 1))),
                  tab, tab,
                  tab, tab, tab,
                  pl.BlockSpec(memory_space=pl.ANY)],
        out_specs=pl.BlockSpec((tr, 512), lambda n, c: (n, COL_Q0 + c)),
        out_shape=jax.ShapeDtypeStruct((lp, IN_TOTAL), BF16),
        input_output_aliases={7: 0},
        compiler_params=_cp(2),
    )(dq, dgate, dk, dv, *tabs, dproj)


def _softplus_neg(lam):
    t = jnp.exp(-jnp.abs(lam))
    u = 1.0 + t
    den = jnp.where(u == 1.0, 1.0, u - 1.0)
    l1p = jnp.where(u == 1.0, t, jnp.log(u) * (t / den))
    return jnp.maximum(-lam, 0.0) + l1p


def _lru_chain(j, tb, rx_ref, rxp_ref, wl_ref, vec_ref, wa_ref, wx_ref):
    rx = rx_ref[...]
    rxp = jnp.where(j > 0, rxp_ref[...], 0.0)
    cat = jnp.concatenate([rxp, rx], axis=0)
    views = [cat[8:8 + tb, :]] + [pltpu.roll(cat, s, axis=0)[8:8 + tb, :] for s in range(1, LRU_CONV_K)]
    x1 = jnp.broadcast_to(vec_ref[0:1, :], (tb, LRU_W))
    for k in range(LRU_CONV_K):
        x1 = x1 + wl_ref[k:k + 1, :] * views[LRU_CONV_K - 1 - k]
    x1b = x1.astype(BF16)
    r = _sig(_dot(x1b, wa_ref[...], NN) + vec_ref[1:2, :])
    ig = _sig(_dot(x1b, wx_ref[...], NN) + vec_ref[2:3, :])
    sp = _softplus_neg(vec_ref[3:4, :])
    log_a = -LRU_C * r * sp
    rows = _row_ids((tb, LRU_W), j * tb)
    live = rows >= PAD0
    a = jnp.where(live, jnp.exp(log_a), 0.0)
    y2 = 2.0 * log_a
    em = -jnp.tanh(0.5 * y2) * (jnp.exp(y2) + 1.0)
    mult = jnp.sqrt(em)
    return dict(views=views, x1=x1, x1b=x1b, r=r, ig=ig, sp=sp, a=a, mult=mult, live=live, a_raw=jnp.exp(log_a))


def _scan_slabs(a, u, forward):
    tb = a.shape[0]
    rows = lax.broadcasted_iota(jnp.int32, (tb, SUB), 0)
    outs_a, outs_u = [], []
    for c0 in range(0, a.shape[1], SUB):
        ac, uc = a[:, c0:c0 + SUB], u[:, c0:c0 + SUB]
        d = 1
        while d < tb:
            if forward:
                keep, sh = rows >= d, d
            else:
                keep, sh = rows < tb - d, tb - d
            an = jnp.where(keep, pltpu.roll(ac, sh, axis=0), 1.0)
            un = jnp.where(keep, pltpu.roll(uc, sh, axis=0), 0.0)
            uc = ac * un + uc
            ac = ac * an
            d *= 2
        outs_a.append(ac)
        outs_u.append(uc)
    return jnp.concatenate(outs_a, axis=1), jnp.concatenate(outs_u, axis=1)


def _lru_specs(jmap, tb):
    return [pl.BlockSpec((tb, 512), lambda n: (jmap(n), COL_RX)),
            pl.BlockSpec((8, 512), lambda n: (jnp.maximum(jmap(n) * (tb // 8) - 1, 0), COL_RX)),
            pl.BlockSpec((tb, 512), lambda n: (jmap(n), COL_RGATE))]


def _lru_param_specs():
    return [pl.BlockSpec((8, LRU_W), lambda n: (0, 0)),
            pl.BlockSpec((8, LRU_W), lambda n: (0, 0)),
            pl.BlockSpec((LRU_W, LRU_W), lambda n: (0, 0)),
            pl.BlockSpec((LRU_W, LRU_W), lambda n: (0, 0))]


def _lru_fwd(name, proj, wl, vec, wa, wx, ycat):
    lp = proj.shape[0]
    tb = _row_tile(lp, 3)
    nb = lp // tb

    def body(rx_ref, rxp_ref, gate_ref, wl_ref, vec_ref, wa_ref, wx_ref, yin_ref, o_ref, h_ref, carry_ref):
        del yin_ref
        j = pl.program_id(0)

        @pl.when(j == 0)
        def _():
            carry_ref[...] = jnp.zeros(carry_ref.shape, F32)

        c = _lru_chain(j, tb, rx_ref, rxp_ref, wl_ref, vec_ref, wa_ref, wx_ref)
        u = jnp.where(c["live"], c["mult"] * (c["ig"] * c["x1"]), 0.0)
        a, u = _scan_slabs(c["a"], u, forward=True)
        h = u + a * carry_ref[0:1, :]
        carry_ref[...] = h[tb - 8:tb, :]
        carry_ref[0:1, :] = h[tb - 1:tb, :]
        h_ref[...] = h
        gate = gate_ref[...]
        o_ref[...] = (h * (gate * _sig(gate))).astype(BF16)

    return pl.pallas_call(
        body, name=name, grid=(nb,),
        in_specs=_lru_specs(lambda n: n, tb) + _lru_param_specs() + [pl.BlockSpec(memory_space=pl.ANY)],
        out_specs=[pl.BlockSpec((tb, 512), lambda n: (n, YC_LRU)),
                   pl.BlockSpec((tb, LRU_W), lambda n: (n, 0))],
        out_shape=[jax.ShapeDtypeStruct((lp, D_MODEL), BF16),
                   jax.ShapeDtypeStruct((lp, LRU_W), F32)],
        input_output_aliases={7: 0},
        scratch_shapes=[pltpu.VMEM((8, LRU_W), F32)],
        compiler_params=_cp(1),
    )(proj, proj, proj, wl, vec, wa, wx, ycat)


def _lru_bwd(name, proj, dycat, hstate, wl, vec, wa, wx, dproj):
    lp = proj.shape[0]
    tb = _row_tile(lp, 3)
    nb = lp // tb

    def body(rx_ref, rxp_ref, gate_ref, dy_ref, h_ref, hp_ref, wl_ref, vec_ref, wa_ref, wx_ref, din_ref,
             dp_ref, dwl_ref, dvec_ref, dwa_ref, dwx_ref, dhc_ref, anx_ref, dxc_ref):
        del din_ref
        n = pl.program_id(0)
        j = nb - 1 - n

        @pl.when(n == 0)
        def _():
            dhc_ref[...] = jnp.zeros(dhc_ref.shape, F32)
            anx_ref[...] = jnp.zeros(anx_ref.shape, F32)
            dxc_ref[...] = jnp.zeros(dxc_ref.shape, F32)
            dwl_ref[...] = jnp.zeros(dwl_ref.shape, F32)
            dvec_ref[...] = jnp.zeros(dvec_ref.shape, F32)
            dwa_ref[...] = jnp.zeros(dwa_ref.shape, F32)
            dwx_ref[...] = jnp.zeros(dwx_ref.shape, F32)

        c = _lru_chain(j, tb, rx_ref, rxp_ref, wl_ref, vec_ref, wa_ref, wx_ref)
        a, mult, r, ig, x1, live = c["a"], c["mult"], c["r"], c["ig"], c["x1"], c["live"]
        h = h_ref[...]
        gate = gate_ref[...]
        sgate = _sig(gate)
        dy = dy_ref[...].astype(F32)
        gsum = dy * (gate * sgate)
        dgate = dy * h * _dsilu(gate, sgate)
        rows = lax.broadcasted_iota(jnp.int32, (tb, LRU_W), 0)
        bb = jnp.where(rows == tb - 1, anx_ref[0:1, :], pltpu.roll(a, tb - 1, axis=0))
        bb, gg = _scan_slabs(bb, gsum, forward=False)
        dh = gg + bb * dhc_ref[0:1, :]
        dhc_ref[...] = dh[0:8, :]
        anx_ref[...] = a[0:8, :]
        hprev = jnp.where(rows == 0, jnp.where(j > 0, hp_ref[7:8, :], 0.0), pltpu.roll(h, 1, axis=0))
        du = jnp.where(live, dh, 0.0)
        da = jnp.where(live, dh * hprev, 0.0)
        ar = c["a_raw"]
        dmult = du * (ig * x1)
        di = du * mult * x1
        dx1 = du * mult * ig
        dloga = da * ar - dmult * ar * ar / mult
        dr = dloga * (-LRU_C * c["sp"])
        dvec_ref[3:4, :] += _colsum(dloga * (-LRU_C * r))
        dza = dr * r * (1.0 - r)
        dzx = di * ig * (1.0 - ig)
        dzab, dzxb = dza.astype(BF16), dzx.astype(BF16)
        dvec_ref[1:2, :] += _colsum(dza)
        dvec_ref[2:3, :] += _colsum(dzx)
        dwa_ref[...] += _dot(c["x1b"], dzab, TN)
        dwx_ref[...] += _dot(c["x1b"], dzxb, TN)
        dx1 = dx1 + _dot(dzab, wa_ref[...], NT) + _dot(dzxb, wx_ref[...], NT)
        dvec_ref[0:1, :] += _colsum(dx1)
        for k in range(LRU_CONV_K):
            dwl_ref[k:k + 1, :] += _colsum(dx1 * c["views"][LRU_CONV_K - 1 - k])
        dcat = jnp.concatenate([dx1, dxc_ref[...]], axis=0)
        drx = jnp.zeros((tb, LRU_W), F32)
        for k in range(LRU_CONV_K):
            s = LRU_CONV_K - 1 - k
            view = dcat[0:tb, :] if s == 0 else pltpu.roll(dcat, tb + 8 - s, axis=0)[0:tb, :]
            drx = drx + wl_ref[k:k + 1, :] * view
        dxc_ref[...] = dx1[0:8, :]
        dp_ref[:, 0:512] = drx.astype(BF16)
        dp_ref[:, 512:1024] = dgate.astype(BF16)

        @pl.when(n == nb - 1)
        def _():
            lam = vec_ref[3:4, :]
            dvec_ref[3:4, :] = dvec_ref[3:4, :] * (-_sig(-lam))

    jmap = lambda n: nb - 1 - n
    return pl.pallas_call(
        body, name=name, grid=(nb,),
        in_specs=(_lru_specs(jmap, tb)
                  + [pl.BlockSpec((tb, 512), lambda n: (jmap(n), YC_LRU)),
                     pl.BlockSpec((tb, LRU_W), lambda n: (jmap(n), 0)),
                     pl.BlockSpec((8, LRU_W), lambda n: (jnp.maximum(jmap(n) * (tb // 8) - 1, 0), 0))]
                  + _lru_param_specs() + [pl.BlockSpec(memory_space=pl.ANY)]),
        out_specs=[pl.BlockSpec((tb, 1024), lambda n: (jmap(n), 4)),
                   pl.BlockSpec((8, LRU_W), lambda n: (0, 0)),
                   pl.BlockSpec((8, LRU_W), lambda n: (0, 0)),
                   pl.BlockSpec((LRU_W, LRU_W), lambda n: (0, 0)),
                   pl.BlockSpec((LRU_W, LRU_W), lambda n: (0, 0))],
        out_shape=[jax.ShapeDtypeStruct((lp, IN_TOTAL), BF16),
                   jax.ShapeDtypeStruct((8, LRU_W), F32),
                   jax.ShapeDtypeStruct((8, LRU_W), F32),
                   jax.ShapeDtypeStruct((LRU_W, LRU_W), F32),
                   jax.ShapeDtypeStruct((LRU_W, LRU_W), F32)],
        input_output_aliases={10: 0},
        scratch_shapes=[pltpu.VMEM((8, LRU_W), F32), pltpu.VMEM((8, LRU_W), F32), pltpu.VMEM((8, LRU_W), F32)],
        compiler_params=_cp(1),
    )(proj, proj, proj, dycat, hstate, hstate, wl, vec, wa, wx, dproj)


_HBM = pl.BlockSpec(memory_space=pltpu.HBM)
_SEM = pl.BlockSpec(memory_space=pltpu.SEMAPHORE)
_ANY = pl.BlockSpec(memory_space=pl.ANY)
_EFFECT = pltpu.SideEffectType.DATAFLOW_SIDE_EFFECTING


def _hbm(a):
    return pltpu.with_memory_space_constraint(a, pltpu.HBM)


_ALL_PEERS = tuple(range(1, N_DEV))
_CHIP_PEERS = (1, 2, 4, 6)
_OTHER_CHIPS = (2, 4, 6)


def _spec_peers(mode):
    return {"ici": _CHIP_PEERS, "fwd": _OTHER_CHIPS}.get(mode, _ALL_PEERS)


def _split_descriptors(copies, srcs, lands, send_sems, recv_sems):
    x, y, c = lax.axis_index("x"), lax.axis_index("y"), lax.axis_index("c")
    me = 4 * x + 2 * y + c
    out, sem = [], 0
    for si, mode, li, ll in copies:
        for k in _spec_peers(mode):
            px = 1 - x if k & 4 else x
            py = 1 - y if k & 2 else y
            pc = 1 - c if k & 1 else c
            peer = 4 * px + 2 * py + pc
            if mode == "fwd":
                src = dst = lands[li].at[peer]
                target = (x, y, 1 - c)
            else:
                src = srcs[si].at[peer] if mode is True else srcs[si]
                dst = lands[li].at[me] if ll is None else lands[li].at[me, ll]
                target = (px, py, pc)
            out.append(pltpu.make_async_remote_copy(
                src_ref=src, dst_ref=dst, send_sem=send_sems.at[sem], recv_sem=recv_sems.at[sem],
                device_id=target, device_id_type=pl.DeviceIdType.MESH))
            sem += 1
    return out


def _n_copies(copies):
    return sum(len(_spec_peers(mode)) for _, mode, _, _ in copies)


def _xchg_start(name, groups):
    n_src = [len(g[0]) for g in groups]
    n_land = [len(g[1]) for g in groups]
    srcs = [s for g in groups for s in g[0]]
    lands = [l for g in groups for l in g[1]]
    ns, nl, ng = len(srcs), len(lands), len(groups)

    def body(*refs):
        src_refs, land_refs = refs[:ns], refs[ns:ns + nl]
        sems = refs[ns + nl:ns + nl + 2 * ng]
        token = refs[-1]
        so = lo = 0
        for gi, (_, _, copies) in enumerate(groups):
            for d in _split_descriptors(copies, src_refs[so:so + n_src[gi]], land_refs[lo:lo + n_land[gi]],
                                        sems[2 * gi], sems[2 * gi + 1]):
                d.start()
            so += n_src[gi]
            lo += n_land[gi]
        token[...] = jnp.zeros(token.shape, F32)

    out_shape, out_specs = [], []
    for g in groups:
        n = _n_copies(g[2])
        out_shape += [pltpu.SemaphoreType.DMA((n,)), pltpu.SemaphoreType.DMA((n,))]
        out_specs += [_SEM, _SEM]
    out_shape += [pltpu.HBM(l.shape, l.dtype) for l in lands]
    out_specs += [_HBM] * nl
    out_shape.append(jax.ShapeDtypeStruct((8, 128), F32))
    out_specs.append(pl.BlockSpec(memory_space=pltpu.VMEM))
    outs = pl.pallas_call(
        body, name=name, in_specs=[_HBM] * (ns + nl), out_specs=out_specs, out_shape=out_shape,
        input_output_aliases={ns + i: 2 * ng + i for i in range(nl)},
        compiler_params=pltpu.CompilerParams(has_side_effects=_EFFECT),
    )(*[_hbm(a) for a in srcs + lands])
    res, lo = [], 2 * ng
    for gi in range(ng):
        res.append((outs[2 * gi], outs[2 * gi + 1], list(outs[lo:lo + n_land[gi]])))
        lo += n_land[gi]
    return res, outs[-1]


def _xchg_wait(name, group, started, after):
    srcs, _, copies = group
    send_sems, recv_sems, lands = started
    ns, nl = len(srcs), len(lands)
    after = list(after)

    def body(*refs):
        src_refs, land_refs = refs[:ns], refs[ns:ns + nl]
        send_ref, recv_ref = refs[ns + nl], refs[ns + nl + 1]
        for d in _split_descriptors(copies, src_refs, land_refs, send_ref, recv_ref):
            d.wait_send()
            d.wait_recv()

    outs = pl.pallas_call(
        body, name=name, in_specs=[_HBM] * (ns + nl) + [_SEM, _SEM] + [_ANY] * len(after),
        out_specs=[_HBM] * nl, out_shape=[pltpu.HBM(l.shape, l.dtype) for l in lands],
        input_output_aliases={ns + i: i for i in range(nl)},
        compiler_params=pltpu.CompilerParams(has_side_effects=_EFFECT),
    )(*[_hbm(a) for a in srcs], *lands, send_sems, recv_sems, *after)
    return list(outs)


def _landing(own, me):
    land = lax.empty((N_DEV,) + own.shape, own.dtype)
    return lax.dynamic_update_slice(land, own[None], (me,) + (0,) * own.ndim)


def _adamw(name, w, m, v, recv, row0=0, prev=None):
    cdim = w.shape[1]
    r = recv.shape[1]
    tr = r
    for cand in (512, 256, 128, 64, 32, 16, 8):
        if r % cand == 0 and r > cand:
            tr = cand
            break
    assert row0 % tr == 0
    blk0 = row0 // tr
    n_prev = 0 if prev is None else 4

    def body(w_ref, m_ref, v_ref, r_ref, *rest):
        g_ref, d_ref, mo_ref, vo_ref = rest[n_prev:]
        g = r_ref[0].astype(F32)
        for s in range(1, N_DEV):
            g = g + r_ref[s].astype(F32)
        mn = ADAM_B1 * m_ref[...] + (1.0 - ADAM_B1) * g
        vn = ADAM_B2 * v_ref[...] + (1.0 - ADAM_B2) * (g * g)
        m_hat = mn / (1.0 - ADAM_B1 ** ADAM_STEP)
        v_hat = vn / (1.0 - ADAM_B2 ** ADAM_STEP)
        g_ref[...] = g
        d_ref[...] = -ADAM_LR * (m_hat / (jnp.sqrt(v_hat) + ADAM_EPS) + ADAM_WD * w_ref[...])
        mo_ref[...] = mn
        vo_ref[...] = vn

    blk = pl.BlockSpec((tr, cdim), lambda i: (i + blk0, 0))
    return pl.pallas_call(
        body, name=name, grid=(r // tr,),
        in_specs=[blk, blk, blk, pl.BlockSpec((N_DEV, tr, cdim), lambda i: (0, i, 0))] + [_ANY] * n_prev,
        out_specs=[blk, blk, blk, blk],
        out_shape=[jax.ShapeDtypeStruct(w.shape, F32)] * 4,
        input_output_aliases={4 + i: i for i in range(n_prev)},
        compiler_params=_cp(1),
    )(w, m, v, recv, *(prev or []))


def _pack_rows(arrs, lead=()):
    n = len(lead)
    flat = jnp.concatenate([a.reshape(a.shape[:n] + (-1,)) for a in arrs], axis=-1)
    size = flat.shape[-1]
    padded = -(-size // PACK_QUANTUM) * PACK_QUANTUM
    flat = jnp.pad(flat, [(0, 0)] * n + [(0, padded - size)])
    return flat.reshape(flat.shape[:n] + (padded // 128, 128))


def _unpack_rows(packed, shapes, lead=()):
    n = len(lead)
    flat = packed.reshape(packed.shape[:n] + (-1,))
    out, off = [], 0
    for s in shapes:
        size = int(np.prod(s))
        out.append(flat[..., off:off + size].reshape(packed.shape[:n] + tuple(s)))
        off += size
    return out


def _block_diag(w):
    eye = jnp.eye(LRU_HEADS, dtype=w.dtype)
    return (eye[:, None, :, None] * w[:, :, None, :]).reshape(LRU_W, LRU_W)


def _diag_blocks(dense):
    t = dense.reshape(LRU_HEADS, 64, LRU_HEADS, 64)
    eye = jnp.eye(LRU_HEADS, dtype=dense.dtype)
    return jnp.sum(t * eye[:, None, :, None], axis=2).reshape(LRU_HEADS * 64, 64)


_W512_NAMES = ("conv_dw_b", "conv_ln_g", "conv_ln_b", "conv_pw_b", "lru_conv_b", "lru_ba", "lru_bx", "lru_lambda")
_W512_ROWS = 12


def _pack_small(d):
    sinks = jnp.pad(d["attn_sinks"], ((0, 0), (0, 512 - N_HEADS)))
    t = jnp.stack([d[n] for n in _W512_NAMES] + [sinks], axis=1)
    w512 = jnp.pad(t, ((0, 0), (0, _W512_ROWS - t.shape[1]), (0, 0))).reshape(DEPTH * _W512_ROWS, 512)
    w2048 = jnp.concatenate([d["ln_in_g"][None], d["ln_in_b"][None], d["ln_post_g"], d["ln_post_b"],
                             jnp.zeros((2, D_MODEL), F32)], axis=0)
    w64 = jnp.concatenate([d["lru_wa"].reshape(-1, 64), d["lru_wx"].reshape(-1, 64)], axis=0)
    return [w512, w2048, w64.reshape(-1, 128)]


def _unpack_small(w512, w2048, w64):
    t = w512.reshape(DEPTH, _W512_ROWS, 512)
    out = {n: t[:, i, :] for i, n in enumerate(_W512_NAMES)}
    out["attn_sinks"] = t[:, len(_W512_NAMES), :N_HEADS]
    out["ln_in_g"], out["ln_in_b"] = w2048[0], w2048[1]
    out["ln_post_g"], out["ln_post_b"] = w2048[2:4], w2048[4:6]
    w64 = w64.reshape(-1, 64)
    half = w64.shape[0] // 2
    out["lru_wa"] = w64[:half].reshape(DEPTH, LRU_HEADS, 64, 64)
    out["lru_wx"] = w64[half:].reshape(DEPTH, LRU_HEADS, 64, 64)
    return out


def _cols_to_slots(full):
    lead = full.shape[:-1]
    t = full.reshape(lead + (N_DEV, full.shape[-1] // N_DEV))
    return jnp.moveaxis(t, -2, 0)


def _slots_to_cols(slots):
    t = jnp.moveaxis(slots, 0, -2)
    return t.reshape(t.shape[:-2] + (t.shape[-2] * t.shape[-1],))


def kernel(x, meta_tokens, ln_in_g, ln_in_b, w_in, conv_dw_w, conv_dw_b, conv_ln_g, conv_ln_b, conv_pw_w, conv_pw_b, attn_sinks, lru_conv_w, lru_conv_b, lru_wa, lru_ba, lru_wx, lru_bx, lru_lambda, w_out, ln_post_g, ln_post_b, loss_target, m_meta_tokens, m_ln_in_g, m_ln_in_b, m_w_in, m_conv_dw_w, m_conv_dw_b, m_conv_ln_g, m_conv_ln_b, m_conv_pw_w, m_conv_pw_b, m_attn_sinks, m_lru_conv_w, m_lru_conv_b, m_lru_wa, m_lru_ba, m_lru_wx, m_lru_bx, m_lru_lambda, m_w_out, m_ln_post_g, m_ln_post_b, v_meta_tokens, v_ln_in_g, v_ln_in_b, v_w_in, v_conv_dw_w, v_conv_dw_b, v_conv_ln_g, v_conv_ln_b, v_conv_pw_w, v_conv_pw_b, v_attn_sinks, v_lru_conv_w, v_lru_conv_b, v_lru_wa, v_lru_ba, v_lru_wx, v_lru_bx, v_lru_lambda, v_w_out, v_ln_post_g, v_ln_post_b):
    seq = x.shape[1]
    lp = seq + TB
    row = lambda a: a.reshape(1, -1)
    rep_names = ["ln_in_g", "ln_in_b", "conv_dw_b", "conv_ln_g", "conv_ln_b", "conv_pw_b", "attn_sinks",
                 "lru_conv_b", "lru_wa", "lru_ba", "lru_wx", "lru_bx", "lru_lambda", "ln_post_g", "ln_post_b"]
    shard_small_names = ["conv_dw_w", "lru_conv_w", "meta_tokens"]
    weights = dict(meta_tokens=meta_tokens, ln_in_g=ln_in_g, ln_in_b=ln_in_b, w_in=w_in, conv_dw_w=conv_dw_w,
                   conv_dw_b=conv_dw_b, conv_ln_g=conv_ln_g, conv_ln_b=conv_ln_b, conv_pw_w=conv_pw_w,
                   conv_pw_b=conv_pw_b, attn_sinks=attn_sinks, lru_conv_w=lru_conv_w, lru_conv_b=lru_conv_b,
                   lru_wa=lru_wa, lru_ba=lru_ba, lru_wx=lru_wx, lru_bx=lru_bx, lru_lambda=lru_lambda,
                   w_out=w_out, ln_post_g=ln_post_g, ln_post_b=ln_post_b)
    mom1 = dict(meta_tokens=m_meta_tokens, ln_in_g=m_ln_in_g, ln_in_b=m_ln_in_b, w_in=m_w_in, conv_dw_w=m_conv_dw_w,
                conv_dw_b=m_conv_dw_b, conv_ln_g=m_conv_ln_g, conv_ln_b=m_conv_ln_b, conv_pw_w=m_conv_pw_w,
                conv_pw_b=m_conv_pw_b, attn_sinks=m_attn_sinks, lru_conv_w=m_lru_conv_w, lru_conv_b=m_lru_conv_b,
                lru_wa=m_lru_wa, lru_ba=m_lru_ba, lru_wx=m_lru_wx, lru_bx=m_lru_bx, lru_lambda=m_lru_lambda,
                w_out=m_w_out, ln_post_g=m_ln_post_g, ln_post_b=m_ln_post_b)
    mom2 = dict(meta_tokens=v_meta_tokens, ln_in_g=v_ln_in_g, ln_in_b=v_ln_in_b, w_in=v_w_in, conv_dw_w=v_conv_dw_w,
                conv_dw_b=v_conv_dw_b, conv_ln_g=v_conv_ln_g, conv_ln_b=v_conv_ln_b, conv_pw_w=v_conv_pw_w,
                conv_pw_b=v_conv_pw_b, attn_sinks=v_attn_sinks, lru_conv_w=v_lru_conv_w, lru_conv_b=v_lru_conv_b,
                lru_wa=v_lru_wa, lru_ba=v_lru_ba, lru_wx=v_lru_wx, lru_bx=v_lru_bx, lru_lambda=v_lru_lambda,
                w_out=v_w_out, ln_post_g=v_ln_post_g, ln_post_b=v_ln_post_b)
    shard_wmv = [_pack_rows([d[n] for n in shard_small_names]) for d in (weights, mom1, mom2)]
    rep_wmv = [_pack_small(d) for d in (weights, mom1, mom2)]
    gate_w = [(_block_diag(lru_wa[l]).astype(BF16), _block_diag(lru_wx[l]).astype(BF16)) for l in range(DEPTH)]
    tabs = _rope_tables(lp)
    prepared = (shard_wmv + [a for wmv in rep_wmv for a in wmv]
                + [w for pair in gate_w for w in pair] + list(tabs))

    small_shard_shapes = [conv_dw_w.shape, lru_conv_w.shape, meta_tokens.shape]
    small_shard = _pack_rows([conv_dw_w, lru_conv_w, meta_tokens])
    me = 4 * lax.axis_index("x") + 2 * lax.axis_index("y") + lax.axis_index("c")
    w_in_b = [w_in[l].astype(BF16) for l in range(DEPTH)]
    w_out_b = [w_out[l].astype(BF16) for l in range(DEPTH)]
    pw_b = conv_pw_w.astype(BF16)
    wgroups = [
        ([small_shard], [_landing(small_shard, me)], [(0, False, 0, None)]),
        ([w_in_b[0]], [_landing(w_in_b[0], me)], [(0, "ici", 0, None)]),
        ([pw_b, w_out_b[0]], [_landing(pw_b, me), _landing(w_out_b[0], me)],
         [(0, False, 0, None), (1, False, 1, None)]),
        ([w_in_b[1], w_out_b[1]], [_landing(w_in_b[1], me), _landing(w_out_b[1], me)],
         [(0, False, 0, None), (1, False, 1, None)]),
    ]
    wstarted, wtoken = _xchg_start("weights_start", wgroups)
    wg_small, = _xchg_wait("weights_wait_s", wgroups[0], wstarted[0], [wtoken])
    g_dw, g_lc, g_meta = _unpack_rows(wg_small, small_shard_shapes, lead=(N_DEV,))
    conv_dw_full = _slots_to_cols(g_dw)
    lru_conv_full = _slots_to_cols(g_lc)
    meta_full = _slots_to_cols(g_meta)
    wg_in = [None, None]
    wg_out = [None, None]
    wg_pw = None

    ln_g = [ln_in_g, ln_post_g[0], ln_post_g[1]]
    ln_b = [ln_in_b, ln_post_b[0], ln_post_b[1]]

    def layer_params(l):
        wdw = jnp.pad(conv_dw_full[l], ((0, 1), (0, 0)))
        cvec = jnp.pad(jnp.stack([conv_dw_b[l], conv_ln_g[l], conv_ln_b[l], conv_pw_b[l]]), ((0, 4), (0, 0)))
        wpw = wg_pw[:, l].reshape(CONV_W, CONV_W)
        sinks = jnp.pad(attn_sinks[l].reshape(1, N_HEADS), ((0, 7), (0, 128 - N_HEADS)))
        wl = jnp.pad(lru_conv_full[l], ((0, 4), (0, 0)))
        lvec = jnp.pad(jnp.stack([lru_conv_b[l], lru_ba[l], lru_bx[l], lru_lambda[l]]), ((0, 4), (0, 0)))
        wa, wx = gate_w[l]
        wo = wg_out[l].reshape(D_MODEL, D_MODEL)
        wout = jnp.concatenate([wo[512:1536], wo[0:512], wo[1536:]], axis=0)
        return dict(wdw=wdw, cvec=cvec, wpw=wpw, sinks=sinks, wl=wl, lvec=lvec, wa=wa, wx=wx, wout=wout)

    params = [None] * DEPTH

    z0, hb = _embed(x, meta_full, row(ln_g[0]), row(ln_b[0]))
    z = [z0]
    saved = []
    for l in range(DEPTH):
        if l == 0:
            part, = _xchg_wait("weights_wait_a", wgroups[1], wstarted[1], [hb] + prepared)
            fwd = ([], [part], [(None, "fwd", 0, None)])
            fstarted, ftoken = _xchg_start("weights_fwd_start", [fwd])
            wg_in[0], = _xchg_wait("weights_fwd_wait", fwd, fstarted[0], [ftoken])
        else:
            wg_in[1], wg_out[1] = _xchg_wait("weights_wait_c", wgroups[3], wstarted[3], [hb])
        proj = _mm_proj(f"proj{l}", hb, wg_in[l])
        if l == 0:
            wg_pw, wg_out[0] = _xchg_wait("weights_wait_b", wgroups[2], wstarted[2], [proj])
        p = params[l] = layer_params(l)
        ycat, c1 = _conv_fwd(f"conv_fwd{l}", proj, p["wdw"], p["cvec"], p["wpw"])
        qr, kr = _rope_fwd(f"rope{l}", proj, tabs)
        ycat = _attn_fwd(f"attn_fwd{l}", qr, kr, proj, p["sinks"], ycat)
        ycat, hstate = _lru_fwd(f"lru_fwd{l}", proj, p["wl"], p["lvec"], p["wa"], p["wx"], ycat)
        saved.append(dict(hb=hb, proj=proj, ycat=ycat, qr=qr, kr=kr, hstate=hstate, c1=c1))
        z_next, hb = _mm_out(f"out{l}", ycat, p["wout"], z[l], row(ln_g[l]), row(ln_b[l]),
                             row(ln_g[l + 1]), row(ln_b[l + 1]))
        z.append(z_next)

    dz, st_post1, loss_blk = _loss_head(z[DEPTH], loss_target, row(ln_g[DEPTH]), row(ln_b[DEPTH]))
    loss = lax.psum(loss_blk[0, 0], ("x", "y", "c"))

    ln_stats = {DEPTH: st_post1}
    g_layers = [None] * DEPTH
    dwin_l, dwout_l = [None] * DEPTH, [None] * DEPTH
    grad_x = gmeta = None
    token = wtoken
    ggroups = [None] * DEPTH
    own = lambda a: lax.dynamic_index_in_dim(a, me, 0, keepdims=False)
    for l in reversed(range(DEPTH)):
        p, s = params[l], saved[l]
        dycat = _mm_dycat(f"dycat{l}", dz, p["wout"], token)
        dwout_l[l] = _mm_dwout(f"dwout{l}", s["ycat"], dz)
        dproj, dwdw, dcvec, dwpw = _conv_bwd(f"conv_bwd{l}", s["proj"], dycat, s["c1"], p["wdw"], p["cvec"], p["wpw"])
        dwo = jnp.concatenate([dwout_l[l][1024:1536], dwout_l[l][0:1024], dwout_l[l][1536:]], axis=0)
        dwo = dwo.reshape(N_DEV, D_MODEL // N_DEV, D_MODEL)
        dpw = dwpw.reshape(N_DEV, CONV_W // N_DEV, CONV_W)
        early = ([dwo, dpw], [_landing(own(dwo), me), _landing(own(dpw), me)],
                 [(0, True, 0, None), (1, True, 1, None)])
        started_early, token = _xchg_start(f"grads_start_out{l}", [early])
        dq, dgate, dk, dv, dsink = _attn_bwd(f"attn_bwd{l}", s["qr"], s["kr"], s["proj"], p["sinks"], dycat, token)
        dproj = _attn_assemble(f"attn_asm{l}", dq, dgate, dk, dv, tabs, dproj)
        dproj, dwl, dlvec, dwa, dwx = _lru_bwd(f"lru_bwd{l}", s["proj"], dycat, s["hstate"],
                                                p["wl"], p["lvec"], p["wa"], p["wx"], dproj)
        g512 = jnp.concatenate([dcvec[0:4], dlvec[0:4], jnp.pad(dsink[0:1], ((0, 0), (0, 512 - 128))),
                                jnp.zeros((_W512_ROWS - 9, 512), F32)], axis=0)
        g_layers[l] = dict(dwdw=dwdw[:CONV_K], dwl=dwl[:LRU_CONV_K], g512=g512,
                           dwa=_diag_blocks(dwa), dwx=_diag_blocks(dwx))
        if l == 0:
            g512 = jnp.concatenate([g_layers[i]["g512"] for i in range(DEPTH)], axis=0)
            g64 = jnp.concatenate([g_layers[i][k] for k in ("dwa", "dwx") for i in range(DEPTH)], axis=0)
            g64 = g64.reshape(-1, 128)
            vgroup = ([g512, g64], [_landing(g512, me), _landing(g64, me)],
                      [(0, False, 0, None), (1, False, 1, None)])
            vstarted, token = _xchg_start("vector_grads_start", [vgroup])
        dwin_l[l] = _mm_dwin(f"dwin{l}", s["hb"], dproj, token)
        late = ([dwin_l[l]], [_landing(own(dwin_l[l]), me)], [(0, True, 0, None)])
        started_late, token = _xchg_start(f"grads_start_in{l}", [late])
        ggroups[l] = [(late, started_late[0]), (early, started_early[0])]
        dh = _mm_dh(f"dh{l}", dproj, wg_in[l], dz, token)
        if l > 0:
            dz, ln_stats[l] = _ln_bwd(f"ln_bwd{l}", dh, z[l], row(ln_g[l]))
        else:
            grad_x, gmeta, ln_stats[0] = _ln_bwd_input(dh, z[0], row(ln_g[0]))

    g2048 = jnp.concatenate([ln_stats[0][0:2], ln_stats[1][0:1], ln_stats[2][0:1], ln_stats[1][1:2],
                             ln_stats[2][1:2], jnp.zeros((2, D_MODEL), F32)], axis=0)
    g_dw_full = jnp.stack([g_layers[l]["dwdw"] for l in range(DEPTH)])
    g_lc_full = jnp.stack([g_layers[l]["dwl"] for l in range(DEPTH)])
    shard_pack = _pack_rows([_cols_to_slots(g_dw_full), _cols_to_slots(g_lc_full), _cols_to_slots(gmeta)],
                            lead=(N_DEV,))
    sgroup = ([shard_pack, g2048], [_landing(own(shard_pack), me), _landing(g2048, me)],
              [(0, True, 0, None), (1, False, 1, None)])
    sstarted, token = _xchg_start("small_grads_start", [sgroup])

    res = {}

    def flat2(a, cols):
        return a.reshape(-1, cols)

    big = (("w_in", 0, W_IN_SHARD), ("w_out", 1, D_MODEL), ("conv_pw_w", 2, CONV_W))
    prev = {n: None for n, _, _ in big}
    def update(name_, cols, recv, l):
        w_ = weights[name_]
        prev[name_] = _adamw(f"adamw_{name_}{l}", flat2(w_, cols), flat2(mom1[name_], cols),
                             flat2(mom2[name_], cols), recv, row0=l * w_.shape[1], prev=prev[name_])

    after = [token]
    for l in reversed(range(DEPTH)):
        late, early = ggroups[l]
        r_out, r_pw = _xchg_wait(f"grads_wait{l}_1", early[0], early[1], after)
        if l > 0:
            r_in, = _xchg_wait(f"grads_wait{l}_0", late[0], late[1], after)
            update("w_in", W_IN_SHARD, r_in, l)
        update("w_out", D_MODEL, r_out, l)
        update("conv_pw_w", CONV_W, r_pw, l)
        after = [prev["w_out"][0], prev["conv_pw_w"][0]]
    late = ggroups[0][0]
    r_in, = _xchg_wait("grads_wait0_0", late[0], late[1], after)
    update("w_in", W_IN_SHARD, r_in, 0)
    for name_, _, _ in big:
        res[name_] = [o.reshape(weights[name_].shape) for o in prev[name_]]

    big_done = [prev[n][0] for n, _, _ in big]
    r_512, r_64 = _xchg_wait("vector_grads_wait", vgroup, vstarted[0], big_done)
    r_small, r_2048 = _xchg_wait("small_grads_wait", sgroup, sstarted[0], big_done)
    r_rep = [r_512, r_2048, r_64]
    sshapes = [weights[n].shape for n in shard_small_names]
    outs = _adamw("adamw_small_sharded", *shard_wmv, r_small)
    for k, o in enumerate(outs):
        for n, a in zip(shard_small_names, _unpack_rows(o, sshapes)):
            res.setdefault(n, [None] * 4)[k] = a

    outs = [_adamw(f"adamw_small_w{tag}", rep_wmv[0][ci], rep_wmv[1][ci], rep_wmv[2][ci], r_rep[ci])
            for ci, tag in enumerate(("512", "2048", "64"))]
    for k in range(4):
        for n, a in _unpack_small(outs[0][k], outs[1][k], outs[2][k]).items():
            res.setdefault(n, [None] * 4)[k] = a

    order = ["meta_tokens", "ln_in_g", "ln_in_b", "w_in", "conv_dw_w", "conv_dw_b", "conv_ln_g", "conv_ln_b",
             "conv_pw_w", "conv_pw_b", "attn_sinks", "lru_conv_w", "lru_conv_b", "lru_wa", "lru_ba", "lru_wx",
             "lru_bx", "lru_lambda", "w_out", "ln_post_g", "ln_post_b"]
    return (loss, grad_x,
            *[res[n][0] for n in order], *[res[n][1] for n in order],
            *[res[n][2] for n in order], *[res[n][3] for n in order])
```

```python
import functools
import math

import numpy as np
import jax
import jax.numpy as jnp
from jax import lax
from jax.experimental import pallas as pl
from jax.experimental.pallas import tpu as pltpu

F32 = jnp.float32
BF16 = jnp.bfloat16

D_MODEL = 2048
DEPTH = 2
N_META = 16
TB = 128
PAD0 = TB - N_META
CONV_W = 512
CONV_K = 31
HEAD_DIM = 64
N_HEADS = 16
N_KV = 4
GROUP = 4
ATT_W = 1024
KV_W = 256
ROT_DIM = 16
ROPE_THETA = 500000.0
LRU_W = 512
LRU_HEADS = 8
LRU_CONV_K = 4
LRU_C = 8.0
IN_TOTAL = 5120
N_DEV = 8
W_IN_SHARD = IN_TOTAL // N_DEV
LN_EPS = 1e-5
ALPHA = (2.0 * DEPTH) ** 0.25
NEG_INF = -1e30
ATT_SCALE = HEAD_DIM ** -0.5

ADAM_LR = 0.001
ADAM_B1 = 0.9
ADAM_B2 = 0.999
ADAM_EPS = 1e-08
ADAM_WD = 0.01
ADAM_STEP = 10

VMEM_LIMIT = 56 * 1024 * 1024
PACK_QUANTUM = 256 * 128

COL_CV, COL_CG, COL_CGATE = 0, 1, 2
COL_Q0 = 3
COL_K256 = 10
COL_V256 = 11
COL_AGATE1024 = 3
COL_RX, COL_RGATE = 8, 9
YC_CONV, YC_LRU = 2, 3


def _cp(n_axes, vmem=VMEM_LIMIT):
    return pltpu.CompilerParams(dimension_semantics=("arbitrary",) * n_axes, vmem_limit_bytes=vmem)


def _row_tile(lp, max_blocks):
    nb = lp // TB
    d = max(k for k in range(1, max_blocks + 1) if nb % k == 0)
    return TB * d


def _sig(x):
    return jax.nn.sigmoid(x)


def _dsilu(x, s):
    return s * (1.0 + x * (1.0 - s))


def _ln_core(z):
    mu = jnp.mean(z, axis=-1, keepdims=True)
    zc = z - mu
    var = jnp.mean(zc * zc, axis=-1, keepdims=True)
    rstd = lax.rsqrt(var + LN_EPS)
    return zc * rstd, rstd


def _ln_bwd_core(dy, xh, rstd, g):
    dxh = dy * g
    m1 = jnp.mean(dxh, axis=-1, keepdims=True)
    m2 = jnp.mean(dxh * xh, axis=-1, keepdims=True)
    return rstd * (dxh - m1 - xh * m2)


def _row_ids(shape, base):
    return lax.broadcasted_iota(jnp.int32, shape, 0) + base


def _colsum(x):
    return jnp.sum(x, axis=0, keepdims=True)


def _dot(a, b, dims):
    return lax.dot_general(a, b, (dims, ((), ())), preferred_element_type=F32)


NN = ((1,), (0,))
NT = ((1,), (1,))
TN = ((0,), (0,))


def _embed(x, meta_full, g, b):
    s = x.shape[1]
    lp = s + TB
    nb = lp // TB

    def body(x_ref, m_ref, g_ref, b_ref, o_ref, hb_ref):
        i = pl.program_id(0)

        @pl.when(i == 0)
        def _():
            o_ref[0:PAD0, :] = jnp.zeros((PAD0, D_MODEL), F32)
            o_ref[PAD0:TB, :] = m_ref[...]

        @pl.when(i > 0)
        def _():
            o_ref[...] = x_ref[...]

        xh, _ = _ln_core(o_ref[...])
        h = xh * g_ref[...] + b_ref[...]
        rows = _row_ids(h.shape, i * TB)
        hb_ref[...] = jnp.where(rows >= PAD0, h, 0.0).astype(BF16)

    return pl.pallas_call(
        body, name="embed", grid=(nb,),
        in_specs=[pl.BlockSpec((None, TB, D_MODEL), lambda i: (0, jnp.maximum(i - 1, 0), 0)),
                  pl.BlockSpec((N_META, D_MODEL), lambda i: (0, 0)),
                  pl.BlockSpec((1, D_MODEL), lambda i: (0, 0)),
                  pl.BlockSpec((1, D_MODEL), lambda i: (0, 0))],
        out_specs=[pl.BlockSpec((TB, D_MODEL), lambda i: (i, 0)),
                   pl.BlockSpec((TB, D_MODEL), lambda i: (i, 0))],
        out_shape=[jax.ShapeDtypeStruct((lp, D_MODEL), F32),
                   jax.ShapeDtypeStruct((lp, D_MODEL), BF16)],
        compiler_params=_cp(1),
    )(x, meta_full, g, b)


def _loss_head(z, target, g, b):
    lp = z.shape[0]
    nb = lp // TB

    def body(z_ref, t_ref, g_ref, b_ref, dz_ref, st_ref, loss_ref):
        i = pl.program_id(0)

        @pl.when(i == 0)
        def _():
            st_ref[...] = jnp.zeros(st_ref.shape, F32)
            loss_ref[...] = jnp.zeros(loss_ref.shape, F32)
            dz_ref[...] = jnp.zeros(dz_ref.shape, F32)

        @pl.when(i > 0)
        def _():
            xh, rstd = _ln_core(z_ref[...])
            gg = g_ref[...]
            y = xh * gg + b_ref[...]
            e = y - t_ref[...]
            part = 0.5 * jnp.sum(jnp.mean(e * e, axis=-1, keepdims=True), axis=0, keepdims=True)
            loss_ref[...] += jnp.broadcast_to(part, loss_ref.shape)
            dy = e / float(D_MODEL)
            st_ref[0:1, :] += _colsum(dy * xh)
            st_ref[1:2, :] += _colsum(dy)
            dz_ref[...] = _ln_bwd_core(dy, xh, rstd, gg)

    return pl.pallas_call(
        body, name="loss_head", grid=(nb,),
        in_specs=[pl.BlockSpec((TB, D_MODEL), lambda i: (i, 0)),
                  pl.BlockSpec((None, TB, D_MODEL), lambda i: (0, jnp.maximum(i - 1, 0), 0)),
                  pl.BlockSpec((1, D_MODEL), lambda i: (0, 0)),
                  pl.BlockSpec((1, D_MODEL), lambda i: (0, 0))],
        out_specs=[pl.BlockSpec((TB, D_MODEL), lambda i: (i, 0)),
                   pl.BlockSpec((8, D_MODEL), lambda i: (0, 0)),
                   pl.BlockSpec((8, 128), lambda i: (0, 0))],
        out_shape=[jax.ShapeDtypeStruct((lp, D_MODEL), F32),
                   jax.ShapeDtypeStruct((8, D_MODEL), F32),
                   jax.ShapeDtypeStruct((8, 128), F32)],
        compiler_params=_cp(1),
    )(z, target, g, b)


def _ln_bwd(name, dh, z, g):
    lp = z.shape[0]
    tr = _row_tile(lp, 3)

    def body(dh_ref, z_ref, g_ref, dz_ref, st_ref):
        i = pl.program_id(0)

        @pl.when(i == 0)
        def _():
            st_ref[...] = jnp.zeros(st_ref.shape, F32)

        xh, rstd = _ln_core(z_ref[...])
        rows = _row_ids(xh.shape, i * tr)
        dy = jnp.where(rows >= PAD0, dh_ref[...], 0.0)
        st_ref[0:1, :] += _colsum(dy * xh)
        st_ref[1:2, :] += _colsum(dy)
        dz_ref[...] = _ln_bwd_core(dy, xh, rstd, g_ref[...])

    return pl.pallas_call(
        body, name=name, grid=(lp // tr,),
        in_specs=[pl.BlockSpec((tr, D_MODEL), lambda i: (i, 0)),
                  pl.BlockSpec((tr, D_MODEL), lambda i: (i, 0)),
                  pl.BlockSpec((1, D_MODEL), lambda i: (0, 0))],
        out_specs=[pl.BlockSpec((tr, D_MODEL), lambda i: (i, 0)),
                   pl.BlockSpec((8, D_MODEL), lambda i: (0, 0))],
        out_shape=[jax.ShapeDtypeStruct((lp, D_MODEL), F32),
                   jax.ShapeDtypeStruct((8, D_MODEL), F32)],
        compiler_params=_cp(1),
    )(dh, z, g)


def _ln_bwd_input(dh, z, g):
    lp = z.shape[0]
    nb = lp // TB
    s = lp - TB

    def body(dh_ref, z_ref, g_ref, gx_ref, gm_ref, st_ref):
        i = pl.program_id(0)

        @pl.when(i == 0)
        def _():
            st_ref[...] = jnp.zeros(st_ref.shape, F32)

        xh, rstd = _ln_core(z_ref[...])
        rows = _row_ids(xh.shape, i * TB)
        dy = jnp.where(rows >= PAD0, dh_ref[...], 0.0)
        st_ref[0:1, :] += _colsum(dy * xh)
        st_ref[1:2, :] += _colsum(dy)
        dz = _ln_bwd_core(dy, xh, rstd, g_ref[...])
        gx_ref[...] = dz

        @pl.when(i == 0)
        def _():
            gm_ref[...] = dz[PAD0:TB, :]

    return pl.pallas_call(
        body, name="ln_in_bwd", grid=(nb,),
        in_specs=[pl.BlockSpec((TB, D_MODEL), lambda i: (i, 0)),
                  pl.BlockSpec((TB, D_MODEL), lambda i: (i, 0)),
                  pl.BlockSpec((1, D_MODEL), lambda i: (0, 0))],
        out_specs=[pl.BlockSpec((None, TB, D_MODEL), lambda i: (0, jnp.maximum(i - 1, 0), 0)),
                   pl.BlockSpec((N_META, D_MODEL), lambda i: (0, 0)),
                   pl.BlockSpec((8, D_MODEL), lambda i: (0, 0))],
        out_shape=[jax.ShapeDtypeStruct((1, s, D_MODEL), F32),
                   jax.ShapeDtypeStruct((N_META, D_MODEL), F32),
                   jax.ShapeDtypeStruct((8, D_MODEL), F32)],
        compiler_params=_cp(1),
    )(dh, z, g)


def _mm_proj(name, hb, wg_in):
    lp = hb.shape[0]
    tm = lp // 3

    def body(a_ref, b_ref, o_ref):
        b = jnp.concatenate([b_ref[0], b_ref[1]], axis=1)
        o_ref[...] = _dot(a_ref[...], b, NN)

    return pl.pallas_call(
        body, name=name, grid=(3, N_DEV // 2),
        in_specs=[pl.BlockSpec((tm, D_MODEL), lambda i, j: (i, 0)),
                  pl.BlockSpec((2, D_MODEL, W_IN_SHARD), lambda i, j: (j, 0, 0))],
        out_specs=pl.BlockSpec((tm, 2 * W_IN_SHARD), lambda i, j: (i, j)),
        out_shape=jax.ShapeDtypeStruct((lp, IN_TOTAL), F32),
        compiler_params=_cp(2),
    )(hb, wg_in)


def _mm_out(name, ycat, wout, z, g, b, g2, b2):
    lp = ycat.shape[0]
    tm = lp // 6

    def body(a_ref, w_ref, z_ref, g_ref, b_ref, g2_ref, b2_ref, o_ref, hb_ref):
        i = pl.program_id(0)
        xh, _ = _ln_core(z_ref[...])
        h = xh * g_ref[...] + b_ref[...]
        live = _row_ids(h.shape, i * tm) >= PAD0
        h = jnp.where(live, h, 0.0)
        zn = ALPHA * h + _dot(a_ref[...], w_ref[...], NN)
        o_ref[...] = zn
        xh2, _ = _ln_core(zn)
        hb_ref[...] = jnp.where(live, xh2 * g2_ref[...] + b2_ref[...], 0.0).astype(BF16)

    vec = pl.BlockSpec((1, D_MODEL), lambda i: (0, 0))
    return pl.pallas_call(
        body, name=name, grid=(6,),
        in_specs=[pl.BlockSpec((tm, D_MODEL), lambda i: (i, 0)),
                  pl.BlockSpec((D_MODEL, D_MODEL), lambda i: (0, 0), pipeline_mode=pl.Buffered(1)),
                  pl.BlockSpec((tm, D_MODEL), lambda i: (i, 0)),
                  vec, vec, vec, vec],
        out_specs=[pl.BlockSpec((tm, D_MODEL), lambda i: (i, 0)),
                   pl.BlockSpec((tm, D_MODEL), lambda i: (i, 0))],
        out_shape=[jax.ShapeDtypeStruct((lp, D_MODEL), F32),
                   jax.ShapeDtypeStruct((lp, D_MODEL), BF16)],
        compiler_params=_cp(1),
    )(ycat, wout, z, g, b, g2, b2)


def _mm_dycat(name, dz, wout, dep):
    lp = dz.shape[0]
    tm = lp // 6

    def body(a_ref, w_ref, dep_ref, o_ref):
        del dep_ref
        o_ref[...] = _dot(a_ref[...].astype(BF16), w_ref[...], NT).astype(BF16)

    return pl.pallas_call(
        body, name=name, grid=(6,),
        in_specs=[pl.BlockSpec((tm, D_MODEL), lambda i: (i, 0)),
                  pl.BlockSpec((D_MODEL, D_MODEL), lambda i: (0, 0), pipeline_mode=pl.Buffered(1)),
                  pl.BlockSpec(memory_space=pl.ANY)],
        out_specs=pl.BlockSpec((tm, D_MODEL), lambda i: (i, 0)),
        out_shape=jax.ShapeDtypeStruct((lp, D_MODEL), BF16),
        compiler_params=_cp(1),
    )(dz, wout, dep)


def _mm_dwout(name, ycat, dz):
    lp = ycat.shape[0]
    tk = _row_tile(lp, 11)
    nk = lp // tk
    half = D_MODEL // 2

    def body(a_ref, b_ref, o_ref, acc_ref):
        k = pl.program_id(1)

        @pl.when(k == 0)
        def _():
            acc_ref[...] = jnp.zeros(acc_ref.shape, F32)

        acc_ref[...] += _dot(a_ref[...], b_ref[...].astype(BF16), TN)

        @pl.when(k == nk - 1)
        def _():
            o_ref[...] = acc_ref[...].astype(BF16)

    return pl.pallas_call(
        body, name=name, grid=(2, nk),
        in_specs=[pl.BlockSpec((tk, half), lambda h, k: (k, h)),
                  pl.BlockSpec((tk, D_MODEL), lambda h, k: (k, 0))],
        out_specs=pl.BlockSpec((half, D_MODEL), lambda h, k: (h, 0)),
        out_shape=jax.ShapeDtypeStruct((D_MODEL, D_MODEL), BF16),
        scratch_shapes=[pltpu.VMEM((half, D_MODEL), F32)],
        compiler_params=_cp(2),
    )(ycat, dz)


def _mm_dwin(name, hb, dproj, dep):
    lp = hb.shape[0]
    tk = _row_tile(lp, 11)
    nk = lp // tk

    def body(a_ref, b_ref, dep_ref, o_ref, acc_ref):
        del dep_ref
        k = pl.program_id(1)

        @pl.when(k == 0)
        def _():
            acc_ref[...] = jnp.zeros(acc_ref.shape, F32)

        acc_ref[...] += _dot(a_ref[...], b_ref[...], TN)

        @pl.when(k == nk - 1)
        def _():
            o_ref[0] = acc_ref[:, 0:W_IN_SHARD].astype(BF16)
            o_ref[1] = acc_ref[:, W_IN_SHARD:2 * W_IN_SHARD].astype(BF16)

    return pl.pallas_call(
        body, name=name, grid=(4, nk),
        in_specs=[pl.BlockSpec((tk, D_MODEL), lambda j, k: (k, 0)),
                  pl.BlockSpec((tk, 2 * W_IN_SHARD), lambda j, k: (k, j)),
                  pl.BlockSpec(memory_space=pl.ANY)],
        out_specs=pl.BlockSpec((2, D_MODEL, W_IN_SHARD), lambda j, k: (j, 0, 0)),
        out_shape=jax.ShapeDtypeStruct((N_DEV, D_MODEL, W_IN_SHARD), BF16),
        scratch_shapes=[pltpu.VMEM((D_MODEL, 2 * W_IN_SHARD), F32)],
        compiler_params=_cp(2),
    )(hb, dproj, dep)


def _mm_dh(name, dproj, wg_in, dz, dep):
    lp = dproj.shape[0]
    tm = lp // 6

    def body(a_ref, w_ref, dz_ref, dep_ref, o_ref, acc_ref):
        del dep_ref
        k = pl.program_id(1)

        @pl.when(k == 0)
        def _():
            acc_ref[...] = jnp.zeros(acc_ref.shape, F32)

        w = jnp.concatenate([w_ref[0], w_ref[1]], axis=1)
        acc_ref[...] += _dot(a_ref[...], w, NT)

        @pl.when(k == N_DEV // 2 - 1)
        def _():
            o_ref[...] = acc_ref[...] + ALPHA * dz_ref[...]

    return pl.pallas_call(
        body, name=name, grid=(6, N_DEV // 2),
        in_specs=[pl.BlockSpec((tm, 2 * W_IN_SHARD), lambda i, k: (i, k)),
                  pl.BlockSpec((2, D_MODEL, W_IN_SHARD), lambda i, k: (k, 0, 0)),
                  pl.BlockSpec((tm, D_MODEL), lambda i, k: (i, 0)),
                  pl.BlockSpec(memory_space=pl.ANY)],
        out_specs=pl.BlockSpec((tm, D_MODEL), lambda i, k: (i, 0)),
        out_shape=jax.ShapeDtypeStruct((lp, D_MODEL), F32),
        scratch_shapes=[pltpu.VMEM((tm, D_MODEL), F32)],
        compiler_params=_cp(2),
    )(dproj, wg_in, dz, dep)


SUB = 128


def _shift_plan(cat, n_shift, base):
    rolled = [cat] + [pltpu.roll(cat, b, axis=0) for b in range(1, 8)]
    return [(rolled[s % 8], base - 8 * (s // 8)) for s in range(n_shift)]


def _tap_sum(w_ref, plan, rows, init=None):
    blocks = []
    for r0 in range(0, rows, SUB):
        row = []
        for c0 in range(0, CONV_W, SUB):
            acc = (jnp.zeros((SUB, SUB), F32) if init is None
                   else jnp.broadcast_to(init[:, c0:c0 + SUB], (SUB, SUB)))
            for k, (arr, off) in enumerate(plan):
                acc = acc + w_ref[k:k + 1, c0:c0 + SUB] * arr[off + r0:off + r0 + SUB, c0:c0 + SUB]
            row.append(acc)
        blocks.append(jnp.concatenate(row, axis=1))
    return jnp.concatenate(blocks, axis=0)


def _tap_grads(dw_ref, dy, plan, rows):
    for c0 in range(0, CONV_W, SUB):
        dys = [dy[r0:r0 + SUB, c0:c0 + SUB] for r0 in range(0, rows, SUB)]
        for k, (arr, off) in enumerate(plan):
            part = None
            for ri, r0 in enumerate(range(0, rows, SUB)):
                prod = dys[ri] * arr[off + r0:off + r0 + SUB, c0:c0 + SUB]
                for i in range(SUB // 8):
                    piece = prod[8 * i:8 * i + 8, :]
                    part = piece if part is None else part + piece
            dw_ref[k:k + 1, c0:c0 + SUB] += jnp.sum(part, axis=0, keepdims=True)


CONV_HALO = 32


def _conv_chain(j, tb, cv_ref, cg_ref, cvp_ref, cgp_ref, wdw_ref, vec_ref, wpw_ref, c1_ref=None):
    cv = cv_ref[...]
    sg = _sig(cg_ref[...])
    c0 = cv * sg
    c0p = jnp.where(j > 0, cvp_ref[...] * _sig(cgp_ref[...]), 0.0)
    cat = jnp.concatenate([c0p, c0], axis=0)
    shifts = _shift_plan(cat, CONV_K, CONV_HALO)
    taps = [shifts[CONV_K - 1 - k] for k in range(CONV_K)]
    if c1_ref is None:
        c1 = _tap_sum(wdw_ref, taps, tb, init=vec_ref[0:1, :])
    else:
        c1 = c1_ref[...]
    xh, rstd = _ln_core(c1)
    c2 = xh * vec_ref[1:2, :] + vec_ref[2:3, :]
    s2 = _sig(c2)
    c3 = c2 * s2
    c4 = _dot(c3.astype(BF16), wpw_ref[...], NN) + vec_ref[3:4, :]
    return dict(cv=cv, sg=sg, taps=taps, c1=c1, xh=xh, rstd=rstd, c2=c2, s2=s2, c3=c3, c4=c4)


def _conv_in_specs(jmap, tb):
    def cur(col):
        return pl.BlockSpec((tb, 512), lambda n: (jmap(n), col))

    def prev(col):
        return pl.BlockSpec((CONV_HALO, 512),
                            lambda n: (jnp.maximum(jmap(n) * (tb // CONV_HALO) - 1, 0), col))

    return [cur(COL_CV), cur(COL_CG), prev(COL_CV), prev(COL_CG), cur(COL_CGATE)]


def _conv_param_specs():
    return [pl.BlockSpec((32, CONV_W), lambda n: (0, 0)),
            pl.BlockSpec((8, CONV_W), lambda n: (0, 0)),
            pl.BlockSpec((CONV_W, CONV_W), lambda n: (0, 0))]


def _conv_fwd(name, proj, wdw, vec, wpw):
    lp = proj.shape[0]
    tb = _row_tile(lp, 3)
    nb = lp // tb

    def body(cv_ref, cg_ref, cvp_ref, cgp_ref, gate_ref, wdw_ref, vec_ref, wpw_ref, o_ref, c1_ref):
        j = pl.program_id(0)
        c = _conv_chain(j, tb, cv_ref, cg_ref, cvp_ref, cgp_ref, wdw_ref, vec_ref, wpw_ref)
        gate = gate_ref[...]
        o_ref[...] = (c["c4"] * (gate * _sig(gate))).astype(BF16)
        c1_ref[...] = c["c1"]

    return pl.pallas_call(
        body, name=name, grid=(nb,),
        in_specs=_conv_in_specs(lambda n: n, tb) + _conv_param_specs(),
        out_specs=[pl.BlockSpec((tb, 512), lambda n: (n, YC_CONV)),
                   pl.BlockSpec((tb, CONV_W), lambda n: (n, 0))],
        out_shape=[jax.ShapeDtypeStruct((lp, D_MODEL), BF16),
                   jax.ShapeDtypeStruct((lp, CONV_W), F32)],
        compiler_params=_cp(1),
    )(proj, proj, proj, proj, proj, wdw, vec, wpw)


def _conv_bwd(name, proj, dycat, c1, wdw, vec, wpw):
    lp = proj.shape[0]
    tb = _row_tile(lp, 3)
    nb = lp // tb
    halo = CONV_HALO

    def body(cv_ref, cg_ref, cvp_ref, cgp_ref, gate_ref, dy_ref, c1_ref, wdw_ref, vec_ref, wpw_ref,
             dp_ref, dwdw_ref, dvec_ref, dwpw_ref, carry_ref):
        n = pl.program_id(0)
        j = nb - 1 - n

        @pl.when(n == 0)
        def _():
            carry_ref[...] = jnp.zeros(carry_ref.shape, F32)
            dwdw_ref[...] = jnp.zeros(dwdw_ref.shape, F32)
            dvec_ref[...] = jnp.zeros(dvec_ref.shape, F32)
            dwpw_ref[...] = jnp.zeros(dwpw_ref.shape, F32)

        c = _conv_chain(j, tb, cv_ref, cg_ref, cvp_ref, cgp_ref, wdw_ref, vec_ref, wpw_ref, c1_ref)
        dy = dy_ref[...].astype(F32)
        gate = gate_ref[...]
        sgate = _sig(gate)
        dc4 = dy * (gate * sgate)
        dgate = dy * c["c4"] * _dsilu(gate, sgate)
        dc4b = dc4.astype(BF16)
        dvec_ref[3:4, :] += _colsum(dc4)
        dwpw_ref[...] += _dot(c["c3"].astype(BF16), dc4b, TN)
        dc3 = _dot(dc4b, wpw_ref[...], NT)
        dc2 = dc3 * _dsilu(c["c2"], c["s2"])
        dvec_ref[1:2, :] += _colsum(dc2 * c["xh"])
        dvec_ref[2:3, :] += _colsum(dc2)
        dc1 = _ln_bwd_core(dc2, c["xh"], c["rstd"], vec_ref[1:2, :])
        dvec_ref[0:1, :] += _colsum(dc1)
        _tap_grads(dwdw_ref, dc1, c["taps"], tb)
        dcat = jnp.concatenate([dc1, carry_ref[...]], axis=0)
        total = tb + halo
        up = [dcat] + [pltpu.roll(dcat, total - b, axis=0) for b in range(1, 8)]
        ahead = [(up[(CONV_K - 1 - k) % 8], 8 * ((CONV_K - 1 - k) // 8)) for k in range(CONV_K)]
        dc0 = _tap_sum(wdw_ref, ahead, tb)
        carry_ref[...] = dc1[0:halo, :]
        sg = c["sg"]
        dcv = dc0 * sg
        dcg = dc0 * c["cv"] * sg * (1.0 - sg)
        dp_ref[:, 0:512] = dcv.astype(BF16)
        dp_ref[:, 512:1024] = dcg.astype(BF16)
        dp_ref[:, 1024:1536] = dgate.astype(BF16)

    jmap = lambda n: nb - 1 - n
    return pl.pallas_call(
        body, name=name, grid=(nb,),
        in_specs=(_conv_in_specs(jmap, tb)
                  + [pl.BlockSpec((tb, 512), lambda n: (jmap(n), YC_CONV)),
                     pl.BlockSpec((tb, CONV_W), lambda n: (jmap(n), 0))]
                  + _conv_param_specs()),
        out_specs=[pl.BlockSpec((tb, 1536), lambda n: (jmap(n), 0)),
                   pl.BlockSpec((32, CONV_W), lambda n: (0, 0)),
                   pl.BlockSpec((8, CONV_W), lambda n: (0, 0)),
                   pl.BlockSpec((CONV_W, CONV_W), lambda n: (0, 0))],
        out_shape=[jax.ShapeDtypeStruct((lp, IN_TOTAL), BF16),
                   jax.ShapeDtypeStruct((32, CONV_W), F32),
                   jax.ShapeDtypeStruct((8, CONV_W), F32),
                   jax.ShapeDtypeStruct((CONV_W, CONV_W), F32)],
        scratch_shapes=[pltpu.VMEM((halo, CONV_W), F32)],
        compiler_params=_cp(1),
    )(proj, proj, proj, proj, proj, dycat, c1, wdw, vec, wpw)


def _rope_tables(lp):
    half = ROT_DIM // 2
    inv_freq = ROPE_THETA ** (-jnp.arange(half, dtype=F32) / half)
    pos = (jnp.arange(lp, dtype=jnp.int32) - PAD0).astype(F32)
    ang = pos[:, None] * inv_freq[None, :]
    cos, sin = jnp.cos(ang), jnp.sin(ang)
    ones = jnp.ones((lp, HEAD_DIM - ROT_DIM), F32)
    zeros = jnp.zeros((lp, HEAD_DIM - ROT_DIM), F32)
    zh = jnp.zeros((lp, half), F32)
    c = jnp.concatenate([cos, cos, ones], axis=1)
    sa = jnp.concatenate([-sin, zh, zeros], axis=1)
    sb = jnp.concatenate([zh, sin, zeros], axis=1)
    tile = lambda t: jnp.tile(t, (1, KV_W // HEAD_DIM))
    return tile(c), tile(sa), tile(sb)


def _rot(x, c, sa, sb):
    w = x.shape[1]
    return x * c + pltpu.roll(x, w - 8, axis=1) * sa + pltpu.roll(x, 8, axis=1) * sb


def _rot_t(dy, c, sa, sb):
    w = dy.shape[1]
    return dy * c + pltpu.roll(dy * sa, 8, axis=1) + pltpu.roll(dy * sb, w - 8, axis=1)


def _rope_fwd(name, proj, tabs):
    lp = proj.shape[0]
    tr = _row_tile(lp, 11)

    def body(q0_ref, q1_ref, k_ref, c_ref, sa_ref, sb_ref, qr_ref, kr_ref):
        c, sa, sb = c_ref[...], sa_ref[...], sb_ref[...]
        c2 = jnp.concatenate([c, c], axis=1)
        sa2 = jnp.concatenate([sa, sa], axis=1)
        sb2 = jnp.concatenate([sb, sb], axis=1)
        qr_ref[:, 0:512] = (_rot(q0_ref[...], c2, sa2, sb2) * ATT_SCALE).astype(BF16)
        qr_ref[:, 512:1024] = (_rot(q1_ref[...], c2, sa2, sb2) * ATT_SCALE).astype(BF16)
        kr_ref[...] = _rot(k_ref[...], c, sa, sb).astype(BF16)

    tab = pl.BlockSpec((tr, KV_W), lambda i: (i, 0))
    return pl.pallas_call(
        body, name=name, grid=(lp // tr,),
        in_specs=[pl.BlockSpec((tr, 512), lambda i: (i, COL_Q0)),
                  pl.BlockSpec((tr, 512), lambda i: (i, COL_Q0 + 1)),
                  pl.BlockSpec((tr, KV_W), lambda i: (i, COL_K256)),
                  tab, tab, tab],
        out_specs=[pl.BlockSpec((tr, ATT_W), lambda i: (i, 0)),
                   pl.BlockSpec((tr, KV_W), lambda i: (i, 0))],
        out_shape=[jax.ShapeDtypeStruct((lp, ATT_W), BF16),
                   jax.ShapeDtypeStruct((lp, KV_W), BF16)],
        compiler_params=_cp(1),
    )(proj, proj, proj, *tabs)


def _attn_mask(j):
    qi = lax.broadcasted_iota(jnp.int32, (GROUP * TB, 3 * TB), 0) & (TB - 1)
    cc = lax.broadcasted_iota(jnp.int32, (GROUP * TB, 3 * TB), 1)
    jj = cc & (TB - 1)
    is_meta = jj >= PAD0
    p0 = (cc < TB) & is_meta & (j >= 1)
    p1 = (cc >= TB) & (cc < 2 * TB) & (jj > qi) & (j >= 2)
    p2 = (cc >= 2 * TB) & (jj <= qi) & ((j >= 1) | is_meta)
    return p0 | p1 | p2


def _lane_group(rows):
    return lax.broadcasted_iota(jnp.int32, (rows, KV_W), 1) // HEAD_DIM


def _stack_heads(x, kv, lgq):
    parts = []
    for g in range(GROUP):
        sh = ((kv - g) % GROUP) * HEAD_DIM
        moved = x if sh == 0 else pltpu.roll(x, sh, axis=1)
        parts.append(jnp.where(lgq == kv, moved, 0.0))
    return jnp.concatenate(parts, axis=0).astype(BF16)


def _unstack_heads(r, kv):
    out = None
    for g in range(GROUP):
        blk = r[g * TB:(g + 1) * TB, :]
        sh = ((g - kv) % GROUP) * HEAD_DIM
        blk = blk if sh == 0 else pltpu.roll(blk, sh, axis=1)
        out = blk if out is None else out + blk
    return out


def _sink_column(sinks, kv):
    lane = lax.broadcasted_iota(jnp.int32, (1, 128), 1)
    cols = []
    for g in range(GROUP):
        sg = jnp.sum(jnp.where(lane == kv * GROUP + g, sinks, 0.0), axis=1, keepdims=True)
        cols.append(jnp.broadcast_to(sg, (TB, 1)))
    return jnp.concatenate(cols, axis=0)


def _attn_kv(kall, vall, lg, kv):
    km = jnp.where(lg == kv, kall, 0.0).astype(BF16)
    vm = jnp.where(lg == kv, vall, 0.0).astype(BF16)
    ones = jnp.where(lg == kv, 1.0, 0.0).astype(BF16)
    return km, vm, ones


def _attn_specs(jmap):
    blk = lambda col: pl.BlockSpec((TB, KV_W), lambda n: (jmap(n), col))
    prv = lambda col: pl.BlockSpec((TB, KV_W), lambda n: (jnp.maximum(jmap(n) - 1, 0), col))
    met = lambda col: pl.BlockSpec((TB, KV_W), lambda n: (0, col))
    return dict(
        qr=pl.BlockSpec((TB, ATT_W), lambda n: (jmap(n), 0)),
        k=[met(0), prv(0), blk(0)],
        v=[met(COL_V256), prv(COL_V256), blk(COL_V256)],
        gate=pl.BlockSpec((TB, ATT_W), lambda n: (jmap(n), COL_AGATE1024)),
        sinks=pl.BlockSpec((8, 128), lambda n: (0, 0)),
    )


ATTN_BWD_HEAD_SETS = ((0, 1, 2, 3),)


def _attn_fwd(name, qr, kr, proj, sinks_row, ycat):
    lp = proj.shape[0]
    nb = lp // TB
    sp = _attn_specs(lambda n: n)

    def body(qr_ref, km_ref, kp_ref, kc_ref, vm_ref, vp_ref, vc_ref, gate_ref, sink_ref, yin_ref, o_ref):
        del yin_ref
        j = pl.program_id(0)
        valid = _attn_mask(j)
        kall = jnp.concatenate([km_ref[...], kp_ref[...], kc_ref[...]], axis=0).astype(F32)
        vall = jnp.concatenate([vm_ref[...], vp_ref[...], vc_ref[...]], axis=0)
        lg = _lane_group(3 * TB)
        lgq = _lane_group(TB)
        lg4 = _lane_group(GROUP * TB)
        sinks = sink_ref[0:1, :]
        heads = range(N_KV)
        cols = [slice(kv * KV_W, (kv + 1) * KV_W) for kv in heads]
        kvo = [_attn_kv(kall, vall, lg, kv) for kv in heads]
        qst = [_stack_heads(qr_ref[:, cols[kv]].astype(F32), kv, lgq) for kv in heads]
        s = [jnp.where(valid, _dot(qst[kv], kvo[kv][0], NT), NEG_INF) for kv in heads]
        eb, es = [], []
        for kv in heads:
            sinkcol = _sink_column(sinks, kv)
            m = jnp.maximum(jnp.max(s[kv], axis=-1, keepdims=True), sinkcol)
            eb.append(jnp.exp(s[kv] - m).astype(BF16))
            es.append(jnp.exp(sinkcol - m))
        r = [_dot(eb[kv], kvo[kv][1], NN) for kv in heads]
        inv = [1.0 / (_dot(eb[kv], kvo[kv][2], NN) + es[kv]) for kv in heads]
        for kv in heads:
            out = jnp.where(lg4 == kv, r[kv] * inv[kv], 0.0)
            gate = gate_ref[:, cols[kv]]
            o_ref[:, cols[kv]] = (_unstack_heads(out, kv) * (gate * _sig(gate))).astype(BF16)

    return pl.pallas_call(
        body, name=name, grid=(nb,),
        in_specs=[sp["qr"]] + sp["k"] + sp["v"] + [sp["gate"], sp["sinks"],
                                                   pl.BlockSpec(memory_space=pl.ANY)],
        out_specs=pl.BlockSpec((TB, ATT_W), lambda n: (n, 0)),
        out_shape=jax.ShapeDtypeStruct((lp, D_MODEL), BF16),
        input_output_aliases={9: 0},
        compiler_params=_cp(1),
    )(qr, kr, kr, kr, proj, proj, proj, proj, sinks_row, ycat)


def _attn_bwd(name, qr, kr, proj, sinks_row, dycat, dep):
    lp = proj.shape[0]
    nb = lp // TB
    sp = _attn_specs(lambda n: n)

    def body(qr_ref, km_ref, kp_ref, kc_ref, vm_ref, vp_ref, vc_ref, gate_ref, sink_ref, dy_ref, dep_ref,
             dq_ref, dgate_ref, dk_ref, dv_ref, dsink_ref):
        del dep_ref
        j = pl.program_id(0)

        @pl.when(j == 0)
        def _():
            dk_ref[...] = jnp.zeros(dk_ref.shape, F32)
            dv_ref[...] = jnp.zeros(dv_ref.shape, F32)
            dsink_ref[...] = jnp.zeros(dsink_ref.shape, F32)

        valid = _attn_mask(j)
        kall = jnp.concatenate([km_ref[...], kp_ref[...], kc_ref[...]], axis=0).astype(F32)
        vall = jnp.concatenate([vm_ref[...], vp_ref[...], vc_ref[...]], axis=0)
        lg = _lane_group(3 * TB)
        lgq = _lane_group(TB)
        lg4 = _lane_group(GROUP * TB)
        sinks = sink_ref[0:1, :]
        lane = lax.broadcasted_iota(jnp.int32, (1, 128), 1)
        def stages(heads):
            dsink = jnp.zeros((1, 128), F32)
            cols = {kv: slice(kv * KV_W, (kv + 1) * KV_W) for kv in heads}
            kvo = {kv: _attn_kv(kall, vall, lg, kv) for kv in heads}
            qst = {kv: _stack_heads(qr_ref[:, cols[kv]].astype(F32), kv, lgq) for kv in heads}
            s = {kv: jnp.where(valid, _dot(qst[kv], kvo[kv][0], NT), NEG_INF) for kv in heads}
            eb, es = {}, {}
            for kv in heads:
                sinkcol = _sink_column(sinks, kv)
                m = jnp.maximum(jnp.max(s[kv], axis=-1, keepdims=True), sinkcol)
                eb[kv] = jnp.exp(s[kv] - m).astype(BF16)
                es[kv] = jnp.exp(sinkcol - m)
            r = {kv: _dot(eb[kv], kvo[kv][1], NN) for kv in heads}
            inv = {kv: 1.0 / (_dot(eb[kv], kvo[kv][2], NN) + es[kv]) for kv in heads}
            dost, dcol = {}, {}
            for kv in heads:
                att = _unstack_heads(jnp.where(lg4 == kv, r[kv] * inv[kv], 0.0), kv)
                gate = gate_ref[:, cols[kv]]
                sgate = _sig(gate)
                dy = dy_ref[:, cols[kv]].astype(F32)
                dgate_ref[:, cols[kv]] = (dy * att * _dsilu(gate, sgate)).astype(BF16)
                dsc = dy * (gate * sgate) * _unstack_heads(jnp.where(lg4 == kv, inv[kv], 0.0), kv)
                dost[kv] = _stack_heads(dsc, kv, lgq)
                dd = dsc * att
                dcol[kv] = jnp.concatenate(
                    [jnp.sum(jnp.where(lgq == g, dd, 0.0), axis=1, keepdims=True) for g in range(GROUP)], axis=0)
            dp = {kv: _dot(dost[kv], kvo[kv][1], NT) for kv in heads}
            ds = {}
            for kv in heads:
                ds[kv] = (eb[kv].astype(F32) * (dp[kv] - dcol[kv])).astype(BF16)
                pd = es[kv] * dcol[kv]
                for g in range(GROUP):
                    tot = jnp.sum(pd[g * TB:(g + 1) * TB, :], axis=0, keepdims=True)
                    dsink = dsink - jnp.where(lane == kv * GROUP + g, tot, 0.0)
            dqs = {kv: _dot(ds[kv], kvo[kv][0], NN) for kv in heads}
            dks = [_dot(ds[kv], qst[kv], TN) for kv in heads]
            dvs = [_dot(eb[kv], dost[kv], TN) for kv in heads]
            for kv in heads:
                dq_ref[:, cols[kv]] = _unstack_heads(dqs[kv], kv)
            return sum(dks[1:], dks[0]), sum(dvs[1:], dvs[0]), dsink

        parts = [stages(hs) for hs in ATTN_BWD_HEAD_SETS]
        dkall = sum([p[0] for p in parts[1:]], parts[0][0])
        dvall = sum([p[1] for p in parts[1:]], parts[0][1])
        dsink = sum([p[2] for p in parts[1:]], parts[0][2])
        dsink_ref[0:1, :] += dsink
        prev = pl.multiple_of(jnp.maximum(j - 1, 0) * TB, TB)
        cur = pl.multiple_of(j * TB, TB)
        dk_ref[0:TB, :] += dkall[0:TB]
        dv_ref[0:TB, :] += dvall[0:TB]
        dk_ref[pl.ds(prev, TB), :] += dkall[TB:2 * TB]
        dv_ref[pl.ds(prev, TB), :] += dvall[TB:2 * TB]
        dk_ref[pl.ds(cur, TB), :] += dkall[2 * TB:3 * TB]
        dv_ref[pl.ds(cur, TB), :] += dvall[2 * TB:3 * TB]

    return pl.pallas_call(
        body, name=name, grid=(nb,),
        in_specs=[sp["qr"]] + sp["k"] + sp["v"] + [sp["gate"], sp["sinks"],
                                                   pl.BlockSpec((TB, ATT_W), lambda n: (n, 0)),
                                                   pl.BlockSpec(memory_space=pl.ANY)],
        out_specs=[pl.BlockSpec((TB, ATT_W), lambda n: (n, 0)),
                   pl.BlockSpec((TB, ATT_W), lambda n: (n, 0)),
                   pl.BlockSpec((lp, KV_W), lambda n: (0, 0)),
                   pl.BlockSpec((lp, KV_W), lambda n: (0, 0)),
                   pl.BlockSpec((8, 128), lambda n: (0, 0))],
        out_shape=[jax.ShapeDtypeStruct((lp, ATT_W), F32),
                   jax.ShapeDtypeStruct((lp, ATT_W), BF16),
                   jax.ShapeDtypeStruct((lp, KV_W), F32),
                   jax.ShapeDtypeStruct((lp, KV_W), F32),
                   jax.ShapeDtypeStruct((8, 128), F32)],
        compiler_params=_cp(1),
    )(qr, kr, kr, kr, proj, proj, proj, proj, sinks_row, dycat, dep)


def _attn_assemble(name, dq, dgate, dk, dv, tabs, dproj):
    lp = dq.shape[0]
    tr = _row_tile(lp, 11)

    def body(dq_ref, dg_ref, dk_ref, dv_ref, c_ref, sa_ref, sb_ref, din_ref, o_ref):
        del din_ref
        cidx = pl.program_id(1)
        c, sa, sb = c_ref[...], sa_ref[...], sb_ref[...]

        @pl.when(cidx < 2)
        def _():
            c2 = jnp.concatenate([c, c], axis=1)
            sa2 = jnp.concatenate([sa, sa], axis=1)
            sb2 = jnp.concatenate([sb, sb], axis=1)
            o_ref[...] = (_rot_t(dq_ref[...], c2, sa2, sb2) * ATT_SCALE).astype(BF16)

        @pl.when(cidx == 2)
        def _():
            o_ref[:, 0:KV_W] = _rot_t(dk_ref[...], c, sa, sb).astype(BF16)
            o_ref[:, KV_W:2 * KV_W] = dv_ref[...].astype(BF16)

        @pl.when(cidx > 2)
        def _():
            o_ref[...] = dg_ref[...]

    tab = pl.BlockSpec((tr, KV_W), lambda n, c: (n, 0))
    return pl.pallas_call(
        body, name=name, grid=(lp // tr, 5),
        in_specs=[pl.BlockSpec((tr, 512), lambda n, c: (n, jnp.minimum(c, 1))),
                  pl.BlockSpec((tr, 512), lambda n, c: (n, jnp.clip(c - 3, 0, 1))),
                  tab, tab,
                  tab, tab, tab,
                  pl.BlockSpec(memory_space=pl.ANY)],
        out_specs=pl.BlockSpec((tr, 512), lambda n, c: (n, COL_Q0 + c)),
        out_shape=jax.ShapeDtypeStruct((lp, IN_TOTAL), BF16),
        input_output_aliases={7: 0},
        compiler_params=_cp(2),
    )(dq, dgate, dk, dv, *tabs, dproj)


def _softplus_neg(lam):
    t = jnp.exp(-jnp.abs(lam))
    u = 1.0 + t
    den = jnp.where(u == 1.0, 1.0, u - 1.0)
    l1p = jnp.where(u == 1.0, t, jnp.log(u) * (t / den))
    return jnp.maximum(-lam, 0.0) + l1p


def _lru_chain(j, tb, rx_ref, rxp_ref, wl_ref, vec_ref, wa_ref, wx_ref):
    rx = rx_ref[...]
    rxp = jnp.where(j > 0, rxp_ref[...], 0.0)
    cat = jnp.concatenate([rxp, rx], axis=0)
    views = [cat[8:8 + tb, :]] + [pltpu.roll(cat, s, axis=0)[8:8 + tb, :] for s in range(1, LRU_CONV_K)]
    x1 = jnp.broadcast_to(vec_ref[0:1, :], (tb, LRU_W))
    for k in range(LRU_CONV_K):
        x1 = x1 + wl_ref[k:k + 1, :] * views[LRU_CONV_K - 1 - k]
    x1b = x1.astype(BF16)
    r = _sig(_dot(x1b, wa_ref[...], NN) + vec_ref[1:2, :])
    ig = _sig(_dot(x1b, wx_ref[...], NN) + vec_ref[2:3, :])
    sp = _softplus_neg(vec_ref[3:4, :])
    log_a = -LRU_C * r * sp
    rows = _row_ids((tb, LRU_W), j * tb)
    live = rows >= PAD0
    a = jnp.where(live, jnp.exp(log_a), 0.0)
    y2 = 2.0 * log_a
    em = -jnp.tanh(0.5 * y2) * (jnp.exp(y2) + 1.0)
    mult = jnp.sqrt(em)
    return dict(views=views, x1=x1, x1b=x1b, r=r, ig=ig, sp=sp, a=a, mult=mult, live=live, a_raw=jnp.exp(log_a))


def _scan_slabs(a, u, forward):
    tb = a.shape[0]
    rows = lax.broadcasted_iota(jnp.int32, (tb, SUB), 0)
    outs_a, outs_u = [], []
    for c0 in range(0, a.shape[1], SUB):
        ac, uc = a[:, c0:c0 + SUB], u[:, c0:c0 + SUB]
        d = 1
        while d < tb:
            if forward:
                keep, sh = rows >= d, d
            else:
                keep, sh = rows < tb - d, tb - d
            an = jnp.where(keep, pltpu.roll(ac, sh, axis=0), 1.0)
            un = jnp.where(keep, pltpu.roll(uc, sh, axis=0), 0.0)
            uc = ac * un + uc
            ac = ac * an
            d *= 2
        outs_a.append(ac)
        outs_u.append(uc)
    return jnp.concatenate(outs_a, axis=1), jnp.concatenate(outs_u, axis=1)


def _lru_specs(jmap, tb):
    return [pl.BlockSpec((tb, 512), lambda n: (jmap(n), COL_RX)),
            pl.BlockSpec((8, 512), lambda n: (jnp.maximum(jmap(n) * (tb // 8) - 1, 0), COL_RX)),
            pl.BlockSpec((tb, 512), lambda n: (jmap(n), COL_RGATE))]


def _lru_param_specs():
    return [pl.BlockSpec((8, LRU_W), lambda n: (0, 0)),
            pl.BlockSpec((8, LRU_W), lambda n: (0, 0)),
            pl.BlockSpec((LRU_W, LRU_W), lambda n: (0, 0)),
            pl.BlockSpec((LRU_W, LRU_W), lambda n: (0, 0))]


def _lru_fwd(name, proj, wl, vec, wa, wx, ycat):
    lp = proj.shape[0]
    tb = _row_tile(lp, 3)
    nb = lp // tb

    def body(rx_ref, rxp_ref, gate_ref, wl_ref, vec_ref, wa_ref, wx_ref, yin_ref, o_ref, h_ref, carry_ref):
        del yin_ref
        j = pl.program_id(0)

        @pl.when(j == 0)
        def _():
            carry_ref[...] = jnp.zeros(carry_ref.shape, F32)

        c = _lru_chain(j, tb, rx_ref, rxp_ref, wl_ref, vec_ref, wa_ref, wx_ref)
        u = jnp.where(c["live"], c["mult"] * (c["ig"] * c["x1"]), 0.0)
        a, u = _scan_slabs(c["a"], u, forward=True)
        h = u + a * carry_ref[0:1, :]
        carry_ref[...] = h[tb - 8:tb, :]
        carry_ref[0:1, :] = h[tb - 1:tb, :]
        h_ref[...] = h
        gate = gate_ref[...]
        o_ref[...] = (h * (gate * _sig(gate))).astype(BF16)

    return pl.pallas_call(
        body, name=name, grid=(nb,),
        in_specs=_lru_specs(lambda n: n, tb) + _lru_param_specs() + [pl.BlockSpec(memory_space=pl.ANY)],
        out_specs=[pl.BlockSpec((tb, 512), lambda n: (n, YC_LRU)),
                   pl.BlockSpec((tb, LRU_W), lambda n: (n, 0))],
        out_shape=[jax.ShapeDtypeStruct((lp, D_MODEL), BF16),
                   jax.ShapeDtypeStruct((lp, LRU_W), F32)],
        input_output_aliases={7: 0},
        scratch_shapes=[pltpu.VMEM((8, LRU_W), F32)],
        compiler_params=_cp(1),
    )(proj, proj, proj, wl, vec, wa, wx, ycat)


def _lru_bwd(name, proj, dycat, hstate, wl, vec, wa, wx, dproj):
    lp = proj.shape[0]
    tb = _row_tile(lp, 3)
    nb = lp // tb

    def body(rx_ref, rxp_ref, gate_ref, dy_ref, h_ref, hp_ref, wl_ref, vec_ref, wa_ref, wx_ref, din_ref,
             dp_ref, dwl_ref, dvec_ref, dwa_ref, dwx_ref, dhc_ref, anx_ref, dxc_ref):
        del din_ref
        n = pl.program_id(0)
        j = nb - 1 - n

        @pl.when(n == 0)
        def _():
            dhc_ref[...] = jnp.zeros(dhc_ref.shape, F32)
            anx_ref[...] = jnp.zeros(anx_ref.shape, F32)
            dxc_ref[...] = jnp.zeros(dxc_ref.shape, F32)
            dwl_ref[...] = jnp.zeros(dwl_ref.shape, F32)
            dvec_ref[...] = jnp.zeros(dvec_ref.shape, F32)
            dwa_ref[...] = jnp.zeros(dwa_ref.shape, F32)
            dwx_ref[...] = jnp.zeros(dwx_ref.shape, F32)

        c = _lru_chain(j, tb, rx_ref, rxp_ref, wl_ref, vec_ref, wa_ref, wx_ref)
        a, mult, r, ig, x1, live = c["a"], c["mult"], c["r"], c["ig"], c["x1"], c["live"]
        h = h_ref[...]
        gate = gate_ref[...]
        sgate = _sig(gate)
        dy = dy_ref[...].astype(F32)
        gsum = dy * (gate * sgate)
        dgate = dy * h * _dsilu(gate, sgate)
        rows = lax.broadcasted_iota(jnp.int32, (tb, LRU_W), 0)
        bb = jnp.where(rows == tb - 1, anx_ref[0:1, :], pltpu.roll(a, tb - 1, axis=0))
        bb, gg = _scan_slabs(bb, gsum, forward=False)
        dh = gg + bb * dhc_ref[0:1, :]
        dhc_ref[...] = dh[0:8, :]
        anx_ref[...] = a[0:8, :]
        hprev = jnp.where(rows == 0, jnp.where(j > 0, hp_ref[7:8, :], 0.0), pltpu.roll(h, 1, axis=0))
        du = jnp.where(live, dh, 0.0)
        da = jnp.where(live, dh * hprev, 0.0)
        ar = c["a_raw"]
        dmult = du * (ig * x1)
        di = du * mult * x1
        dx1 = du * mult * ig
        dloga = da * ar - dmult * ar * ar / mult
        dr = dloga * (-LRU_C * c["sp"])
        dvec_ref[3:4, :] += _colsum(dloga * (-LRU_C * r))
        dza = dr * r * (1.0 - r)
        dzx = di * ig * (1.0 - ig)
        dzab, dzxb = dza.astype(BF16), dzx.astype(BF16)
        dvec_ref[1:2, :] += _colsum(dza)
        dvec_ref[2:3, :] += _colsum(dzx)
        dwa_ref[...] += _dot(c["x1b"], dzab, TN)
        dwx_ref[...] += _dot(c["x1b"], dzxb, TN)
        dx1 = dx1 + _dot(dzab, wa_ref[...], NT) + _dot(dzxb, wx_ref[...], NT)
        dvec_ref[0:1, :] += _colsum(dx1)
        for k in range(LRU_CONV_K):
            dwl_ref[k:k + 1, :] += _colsum(dx1 * c["views"][LRU_CONV_K - 1 - k])
        dcat = jnp.concatenate([dx1, dxc_ref[...]], axis=0)
        drx = jnp.zeros((tb, LRU_W), F32)
        for k in range(LRU_CONV_K):
            s = LRU_CONV_K - 1 - k
            view = dcat[0:tb, :] if s == 0 else pltpu.roll(dcat, tb + 8 - s, axis=0)[0:tb, :]
            drx = drx + wl_ref[k:k + 1, :] * view
        dxc_ref[...] = dx1[0:8, :]
        dp_ref[:, 0:512] = drx.astype(BF16)
        dp_ref[:, 512:1024] = dgate.astype(BF16)

        @pl.when(n == nb - 1)
        def _():
            lam = vec_ref[3:4, :]
            dvec_ref[3:4, :] = dvec_ref[3:4, :] * (-_sig(-lam))

    jmap = lambda n: nb - 1 - n
    return pl.pallas_call(
        body, name=name, grid=(nb,),
        in_specs=(_lru_specs(jmap, tb)
                  + [pl.BlockSpec((tb, 512), lambda n: (jmap(n), YC_LRU)),
                     pl.BlockSpec((tb, LRU_W), lambda n: (jmap(n), 0)),
                     pl.BlockSpec((8, LRU_W), lambda n: (jnp.maximum(jmap(n) * (tb // 8) - 1, 0), 0))]
                  + _lru_param_specs() + [pl.BlockSpec(memory_space=pl.ANY)]),
        out_specs=[pl.BlockSpec((tb, 1024), lambda n: (jmap(n), 4)),
                   pl.BlockSpec((8, LRU_W), lambda n: (0, 0)),
                   pl.BlockSpec((8, LRU_W), lambda n: (0, 0)),
                   pl.BlockSpec((LRU_W, LRU_W), lambda n: (0, 0)),
                   pl.BlockSpec((LRU_W, LRU_W), lambda n: (0, 0))],
        out_shape=[jax.ShapeDtypeStruct((lp, IN_TOTAL), BF16),
                   jax.ShapeDtypeStruct((8, LRU_W), F32),
                   jax.ShapeDtypeStruct((8, LRU_W), F32),
                   jax.ShapeDtypeStruct((LRU_W, LRU_W), F32),
                   jax.ShapeDtypeStruct((LRU_W, LRU_W), F32)],
        input_output_aliases={10: 0},
        scratch_shapes=[pltpu.VMEM((8, LRU_W), F32), pltpu.VMEM((8, LRU_W), F32), pltpu.VMEM((8, LRU_W), F32)],
        compiler_params=_cp(1),
    )(proj, proj, proj, dycat, hstate, hstate, wl, vec, wa, wx, dproj)


_HBM = pl.BlockSpec(memory_space=pltpu.HBM)
_SEM = pl.BlockSpec(memory_space=pltpu.SEMAPHORE)
_ANY = pl.BlockSpec(memory_space=pl.ANY)
_EFFECT = pltpu.SideEffectType.DATAFLOW_SIDE_EFFECTING


def _hbm(a):
    return pltpu.with_memory_space_constraint(a, pltpu.HBM)


_ALL_PEERS = tuple(range(1, N_DEV))
_CHIP_PEERS = (1, 2, 4, 6)
_OTHER_CHIPS = (2, 4, 6)


def _spec_peers(mode):
    return {"ici": _CHIP_PEERS, "fwd": _OTHER_CHIPS}.get(mode, _ALL_PEERS)


def _split_descriptors(copies, srcs, lands, send_sems, recv_sems):
    x, y, c = lax.axis_index("x"), lax.axis_index("y"), lax.axis_index("c")
    me = 4 * x + 2 * y + c
    out, sem = [], 0
    for si, mode, li, ll in copies:
        for k in _spec_peers(mode):
            px = 1 - x if k & 4 else x
            py = 1 - y if k & 2 else y
            pc = 1 - c if k & 1 else c
            peer = 4 * px + 2 * py + pc
            if mode == "fwd":
                src = dst = lands[li].at[peer]
                target = (x, y, 1 - c)
            else:
                src = srcs[si].at[peer] if mode is True else srcs[si]
                dst = lands[li].at[me] if ll is None else lands[li].at[me, ll]
                target = (px, py, pc)
            out.append(pltpu.make_async_remote_copy(
                src_ref=src, dst_ref=dst, send_sem=send_sems.at[sem], recv_sem=recv_sems.at[sem],
                device_id=target, device_id_type=pl.DeviceIdType.MESH))
            sem += 1
    return out


def _n_copies(copies):
    return sum(len(_spec_peers(mode)) for _, mode, _, _ in copies)


def _xchg_start(name, groups):
    n_src = [len(g[0]) for g in groups]
    n_land = [len(g[1]) for g in groups]
    srcs = [s for g in groups for s in g[0]]
    lands = [l for g in groups for l in g[1]]
    ns, nl, ng = len(srcs), len(lands), len(groups)

    def body(*refs):
        src_refs, land_refs = refs[:ns], refs[ns:ns + nl]
        sems = refs[ns + nl:ns + nl + 2 * ng]
        token = refs[-1]
        so = lo = 0
        for gi, (_, _, copies) in enumerate(groups):
            for d in _split_descriptors(copies, src_refs[so:so + n_src[gi]], land_refs[lo:lo + n_land[gi]],
                                        sems[2 * gi], sems[2 * gi + 1]):
                d.start()
            so += n_src[gi]
            lo += n_land[gi]
        token[...] = jnp.zeros(token.shape, F32)

    out_shape, out_specs = [], []
    for g in groups:
        n = _n_copies(g[2])
        out_shape += [pltpu.SemaphoreType.DMA((n,)), pltpu.SemaphoreType.DMA((n,))]
        out_specs += [_SEM, _SEM]
    out_shape += [pltpu.HBM(l.shape, l.dtype) for l in lands]
    out_specs += [_HBM] * nl
    out_shape.append(jax.ShapeDtypeStruct((8, 128), F32))
    out_specs.append(pl.BlockSpec(memory_space=pltpu.VMEM))
    outs = pl.pallas_call(
        body, name=name, in_specs=[_HBM] * (ns + nl), out_specs=out_specs, out_shape=out_shape,
        input_output_aliases={ns + i: 2 * ng + i for i in range(nl)},
        compiler_params=pltpu.CompilerParams(has_side_effects=_EFFECT),
    )(*[_hbm(a) for a in srcs + lands])
    res, lo = [], 2 * ng
    for gi in range(ng):
        res.append((outs[2 * gi], outs[2 * gi + 1], list(outs[lo:lo + n_land[gi]])))
        lo += n_land[gi]
    return res, outs[-1]


def _xchg_wait(name, group, started, after):
    srcs, _, copies = group
    send_sems, recv_sems, lands = started
    ns, nl = len(srcs), len(lands)
    after = list(after)

    def body(*refs):
        src_refs, land_refs = refs[:ns], refs[ns:ns + nl]
        send_ref, recv_ref = refs[ns + nl], refs[ns + nl + 1]
        for d in _split_descriptors(copies, src_refs, land_refs, send_ref, recv_ref):
            d.wait_send()
            d.wait_recv()

    outs = pl.pallas_call(
        body, name=name, in_specs=[_HBM] * (ns + nl) + [_SEM, _SEM] + [_ANY] * len(after),
        out_specs=[_HBM] * nl, out_shape=[pltpu.HBM(l.shape, l.dtype) for l in lands],
        input_output_aliases={ns + i: i for i in range(nl)},
        compiler_params=pltpu.CompilerParams(has_side_effects=_EFFECT),
    )(*[_hbm(a) for a in srcs], *lands, send_sems, recv_sems, *after)
    return list(outs)


def _landing(own, me):
    land = lax.empty((N_DEV,) + own.shape, own.dtype)
    return lax.dynamic_update_slice(land, own[None], (me,) + (0,) * own.ndim)


def _adamw(name, w, m, v, recv, row0=0, prev=None):
    cdim = w.shape[1]
    r = recv.shape[1]
    tr = r
    for cand in (512, 256, 128, 64, 32, 16, 8):
        if r % cand == 0 and r > cand:
            tr = cand
            break
    assert row0 % tr == 0
    blk0 = row0 // tr
    n_prev = 0 if prev is None else 4

    def body(w_ref, m_ref, v_ref, r_ref, *rest):
        g_ref, d_ref, mo_ref, vo_ref = rest[n_prev:]
        g = r_ref[0].astype(F32)
        for s in range(1, N_DEV):
            g = g + r_ref[s].astype(F32)
        mn = ADAM_B1 * m_ref[...] + (1.0 - ADAM_B1) * g
        vn = ADAM_B2 * v_ref[...] + (1.0 - ADAM_B2) * (g * g)
        m_hat = mn / (1.0 - ADAM_B1 ** ADAM_STEP)
        v_hat = vn / (1.0 - ADAM_B2 ** ADAM_STEP)
        g_ref[...] = g
        d_ref[...] = -ADAM_LR * (m_hat / (jnp.sqrt(v_hat) + ADAM_EPS) + ADAM_WD * w_ref[...])
        mo_ref[...] = mn
        vo_ref[...] = vn

    blk = pl.BlockSpec((tr, cdim), lambda i: (i + blk0, 0))
    return pl.pallas_call(
        body, name=name, grid=(r // tr,),
        in_specs=[blk, blk, blk, pl.BlockSpec((N_DEV, tr, cdim), lambda i: (0, i, 0))] + [_ANY] * n_prev,
        out_specs=[blk, blk, blk, blk],
        out_shape=[jax.ShapeDtypeStruct(w.shape, F32)] * 4,
        input_output_aliases={4 + i: i for i in range(n_prev)},
        compiler_params=_cp(1),
    )(w, m, v, recv, *(prev or []))


def _pack_rows(arrs, lead=()):
    n = len(lead)
    flat = jnp.concatenate([a.reshape(a.shape[:n] + (-1,)) for a in arrs], axis=-1)
    size = flat.shape[-1]
    padded = -(-size // PACK_QUANTUM) * PACK_QUANTUM
    flat = jnp.pad(flat, [(0, 0)] * n + [(0, padded - size)])
    return flat.reshape(flat.shape[:n] + (padded // 128, 128))


def _unpack_rows(packed, shapes, lead=()):
    n = len(lead)
    flat = packed.reshape(packed.shape[:n] + (-1,))
    out, off = [], 0
    for s in shapes:
        size = int(np.prod(s))
        out.append(flat[..., off:off + size].reshape(packed.shape[:n] + tuple(s)))
        off += size
    return out


def _block_diag(w):
    eye = jnp.eye(LRU_HEADS, dtype=w.dtype)
    return (eye[:, None, :, None] * w[:, :, None, :]).reshape(LRU_W, LRU_W)


def _diag_blocks(dense):
    t = dense.reshape(LRU_HEADS, 64, LRU_HEADS, 64)
    eye = jnp.eye(LRU_HEADS, dtype=dense.dtype)
    return jnp.sum(t * eye[:, None, :, None], axis=2).reshape(LRU_HEADS * 64, 64)


_W512_NAMES = ("conv_dw_b", "conv_ln_g", "conv_ln_b", "conv_pw_b", "lru_conv_b", "lru_ba", "lru_bx", "lru_lambda")
_W512_ROWS = 12


def _pack_small(d):
    sinks = jnp.pad(d["attn_sinks"], ((0, 0), (0, 512 - N_HEADS)))
    t = jnp.stack([d[n] for n in _W512_NAMES] + [sinks], axis=1)
    w512 = jnp.pad(t, ((0, 0), (0, _W512_ROWS - t.shape[1]), (0, 0))).reshape(DEPTH * _W512_ROWS, 512)
    w2048 = jnp.concatenate([d["ln_in_g"][None], d["ln_in_b"][None], d["ln_post_g"], d["ln_post_b"],
                             jnp.zeros((2, D_MODEL), F32)], axis=0)
    w64 = jnp.concatenate([d["lru_wa"].reshape(-1, 64), d["lru_wx"].reshape(-1, 64)], axis=0)
    return [w512, w2048, w64.reshape(-1, 128)]


def _unpack_small(w512, w2048, w64):
    t = w512.reshape(DEPTH, _W512_ROWS, 512)
    out = {n: t[:, i, :] for i, n in enumerate(_W512_NAMES)}
    out["attn_sinks"] = t[:, len(_W512_NAMES), :N_HEADS]
    out["ln_in_g"], out["ln_in_b"] = w2048[0], w2048[1]
    out["ln_post_g"], out["ln_post_b"] = w2048[2:4], w2048[4:6]
    w64 = w64.reshape(-1, 64)
    half = w64.shape[0] // 2
    out["lru_wa"] = w64[:half].reshape(DEPTH, LRU_HEADS, 64, 64)
    out["lru_wx"] = w64[half:].reshape(DEPTH, LRU_HEADS, 64, 64)
    return out


def _cols_to_slots(full):
    lead = full.shape[:-1]
    t = full.reshape(lead + (N_DEV, full.shape[-1] // N_DEV))
    return jnp.moveaxis(t, -2, 0)


def _slots_to_cols(slots):
    t = jnp.moveaxis(slots, 0, -2)
    return t.reshape(t.shape[:-2] + (t.shape[-2] * t.shape[-1],))


def kernel(x, meta_tokens, ln_in_g, ln_in_b, w_in, conv_dw_w, conv_dw_b, conv_ln_g, conv_ln_b, conv_pw_w, conv_pw_b, attn_sinks, lru_conv_w, lru_conv_b, lru_wa, lru_ba, lru_wx, lru_bx, lru_lambda, w_out, ln_post_g, ln_post_b, loss_target, m_meta_tokens, m_ln_in_g, m_ln_in_b, m_w_in, m_conv_dw_w, m_conv_dw_b, m_conv_ln_g, m_conv_ln_b, m_conv_pw_w, m_conv_pw_b, m_attn_sinks, m_lru_conv_w, m_lru_conv_b, m_lru_wa, m_lru_ba, m_lru_wx, m_lru_bx, m_lru_lambda, m_w_out, m_ln_post_g, m_ln_post_b, v_meta_tokens, v_ln_in_g, v_ln_in_b, v_w_in, v_conv_dw_w, v_conv_dw_b, v_conv_ln_g, v_conv_ln_b, v_conv_pw_w, v_conv_pw_b, v_attn_sinks, v_lru_conv_w, v_lru_conv_b, v_lru_wa, v_lru_ba, v_lru_wx, v_lru_bx, v_lru_lambda, v_w_out, v_ln_post_g, v_ln_post_b):
    seq = x.shape[1]
    lp = seq + TB
    row = lambda a: a.reshape(1, -1)
    rep_names = ["ln_in_g", "ln_in_b", "conv_dw_b", "conv_ln_g", "conv_ln_b", "conv_pw_b", "attn_sinks",
                 "lru_conv_b", "lru_wa", "lru_ba", "lru_wx", "lru_bx", "lru_lambda", "ln_post_g", "ln_post_b"]
    shard_small_names = ["conv_dw_w", "lru_conv_w", "meta_tokens"]
    weights = dict(meta_tokens=meta_tokens, ln_in_g=ln_in_g, ln_in_b=ln_in_b, w_in=w_in, conv_dw_w=conv_dw_w,
                   conv_dw_b=conv_dw_b, conv_ln_g=conv_ln_g, conv_ln_b=conv_ln_b, conv_pw_w=conv_pw_w,
                   conv_pw_b=conv_pw_b, attn_sinks=attn_sinks, lru_conv_w=lru_conv_w, lru_conv_b=lru_conv_b,
                   lru_wa=lru_wa, lru_ba=lru_ba, lru_wx=lru_wx, lru_bx=lru_bx, lru_lambda=lru_lambda,
                   w_out=w_out, ln_post_g=ln_post_g, ln_post_b=ln_post_b)
    mom1 = dict(meta_tokens=m_meta_tokens, ln_in_g=m_ln_in_g, ln_in_b=m_ln_in_b, w_in=m_w_in, conv_dw_w=m_conv_dw_w,
                conv_dw_b=m_conv_dw_b, conv_ln_g=m_conv_ln_g, conv_ln_b=m_conv_ln_b, conv_pw_w=m_conv_pw_w,
                conv_pw_b=m_conv_pw_b, attn_sinks=m_attn_sinks, lru_conv_w=m_lru_conv_w, lru_conv_b=m_lru_conv_b,
                lru_wa=m_lru_wa, lru_ba=m_lru_ba, lru_wx=m_lru_wx, lru_bx=m_lru_bx, lru_lambda=m_lru_lambda,
                w_out=m_w_out, ln_post_g=m_ln_post_g, ln_post_b=m_ln_post_b)
    mom2 = dict(meta_tokens=v_meta_tokens, ln_in_g=v_ln_in_g, ln_in_b=v_ln_in_b, w_in=v_w_in, conv_dw_w=v_conv_dw_w,
                conv_dw_b=v_conv_dw_b, conv_ln_g=v_conv_ln_g, conv_ln_b=v_conv_ln_b, conv_pw_w=v_conv_pw_w,
                conv_pw_b=v_conv_pw_b, attn_sinks=v_attn_sinks, lru_conv_w=v_lru_conv_w, lru_conv_b=v_lru_conv_b,
                lru_wa=v_lru_wa, lru_ba=v_lru_ba, lru_wx=v_lru_wx, lru_bx=v_lru_bx, lru_lambda=v_lru_lambda,
                w_out=v_w_out, ln_post_g=v_ln_post_g, ln_post_b=v_ln_post_b)
    shard_wmv = [_pack_rows([d[n] for n in shard_small_names]) for d in (weights, mom1, mom2)]
    rep_wmv = [_pack_small(d) for d in (weights, mom1, mom2)]
    gate_w = [(_block_diag(lru_wa[l]).astype(BF16), _block_diag(lru_wx[l]).astype(BF16)) for l in range(DEPTH)]
    tabs = _rope_tables(lp)
    prepared = (shard_wmv + [a for wmv in rep_wmv for a in wmv]
                + [w for pair in gate_w for w in pair] + list(tabs))

    small_shard_shapes = [conv_dw_w.shape, lru_conv_w.shape, meta_tokens.shape]
    small_shard = _pack_rows([conv_dw_w, lru_conv_w, meta_tokens])
    me = 4 * lax.axis_index("x") + 2 * lax.axis_index("y") + lax.axis_index("c")
    w_in_b = [w_in[l].astype(BF16) for l in range(DEPTH)]
    w_out_b = [w_out[l].astype(BF16) for l in range(DEPTH)]
    pw_b = conv_pw_w.astype(BF16)
    wgroups = [
        ([small_shard], [_landing(small_shard, me)], [(0, False, 0, None)]),
        ([w_in_b[0]], [_landing(w_in_b[0], me)], [(0, "ici", 0, None)]),
        ([pw_b, w_out_b[0]], [_landing(pw_b, me), _landing(w_out_b[0], me)],
         [(0, "ici", 0, None), (1, "ici", 1, None)]),
        ([w_in_b[1], w_out_b[1]], [_landing(w_in_b[1], me), _landing(w_out_b[1], me)],
         [(0, "ici", 0, None), (1, "ici", 1, None)]),
    ]
    wstarted, wtoken = _xchg_start("weights_start", wgroups)

    def pass_on(tag, parts):
        fwd = ([], list(parts), [(None, "fwd", i, None) for i in range(len(parts))])
        fstarted, _ = _xchg_start(f"weights_fwd_start_{tag}", [fwd])
        return fwd, fstarted[0]
    wg_small, = _xchg_wait("weights_wait_s", wgroups[0], wstarted[0], [wtoken])
    g_dw, g_lc, g_meta = _unpack_rows(wg_small, small_shard_shapes, lead=(N_DEV,))
    conv_dw_full = _slots_to_cols(g_dw)
    lru_conv_full = _slots_to_cols(g_lc)
    meta_full = _slots_to_cols(g_meta)
    wg_in = [None, None]
    wg_out = [None, None]
    wg_pw = None

    ln_g = [ln_in_g, ln_post_g[0], ln_post_g[1]]
    ln_b = [ln_in_b, ln_post_b[0], ln_post_b[1]]

    def layer_params(l):
        wdw = jnp.pad(conv_dw_full[l], ((0, 1), (0, 0)))
        cvec = jnp.pad(jnp.stack([conv_dw_b[l], conv_ln_g[l], conv_ln_b[l], conv_pw_b[l]]), ((0, 4), (0, 0)))
        wpw = wg_pw[:, l].reshape(CONV_W, CONV_W)
        sinks = jnp.pad(attn_sinks[l].reshape(1, N_HEADS), ((0, 7), (0, 128 - N_HEADS)))
        wl = jnp.pad(lru_conv_full[l], ((0, 4), (0, 0)))
        lvec = jnp.pad(jnp.stack([lru_conv_b[l], lru_ba[l], lru_bx[l], lru_lambda[l]]), ((0, 4), (0, 0)))
        wa, wx = gate_w[l]
        wo = wg_out[l].reshape(D_MODEL, D_MODEL)
        wout = jnp.concatenate([wo[512:1536], wo[0:512], wo[1536:]], axis=0)
        return dict(wdw=wdw, cvec=cvec, wpw=wpw, sinks=sinks, wl=wl, lvec=lvec, wa=wa, wx=wx, wout=wout)

    params = [None] * DEPTH

    z0, hb = _embed(x, meta_full, row(ln_g[0]), row(ln_b[0]))
    z = [z0]
    saved = []
    for l in range(DEPTH):
        if l == 0:
            parts = _xchg_wait("weights_wait_a", wgroups[1], wstarted[1], [hb] + prepared)
            pending = pass_on("a", parts)
            wg_in[0], = _xchg_wait("weights_fwd_wait_a", *pending, [hb])
        else:
            wg_in[1], wg_out[1] = _xchg_wait("weights_fwd_wait_c", *pending_c, [hb])
        proj = _mm_proj(f"proj{l}", hb, wg_in[l])
        if l == 0:
            pending = pass_on("b", _xchg_wait("weights_wait_b", wgroups[2], wstarted[2], [proj]))
        qr, kr = _rope_fwd(f"rope{l}", proj, tabs)
        if l == 0:
            wg_pw, wg_out[0] = _xchg_wait("weights_fwd_wait_b", *pending, [qr])
        p = params[l] = layer_params(l)
        ycat, c1 = _conv_fwd(f"conv_fwd{l}", proj, p["wdw"], p["cvec"], p["wpw"])
        ycat = _attn_fwd(f"attn_fwd{l}", qr, kr, proj, p["sinks"], ycat)
        if l == 0:
            pending_c = pass_on("c", _xchg_wait("weights_wait_c", wgroups[3], wstarted[3], [ycat]))
        ycat, hstate = _lru_fwd(f"lru_fwd{l}", proj, p["wl"], p["lvec"], p["wa"], p["wx"], ycat)
        saved.append(dict(hb=hb, proj=proj, ycat=ycat, qr=qr, kr=kr, hstate=hstate, c1=c1))
        z_next, hb = _mm_out(f"out{l}", ycat, p["wout"], z[l], row(ln_g[l]), row(ln_b[l]),
                             row(ln_g[l + 1]), row(ln_b[l + 1]))
        z.append(z_next)

    dz, st_post1, loss_blk = _loss_head(z[DEPTH], loss_target, row(ln_g[DEPTH]), row(ln_b[DEPTH]))

    ln_stats = {DEPTH: st_post1}
    g_layers = [None] * DEPTH
    dwin_l, dwout_l = [None] * DEPTH, [None] * DEPTH
    grad_x = gmeta = None
    token = wtoken
    ggroups = [None] * DEPTH
    own = lambda a: lax.dynamic_index_in_dim(a, me, 0, keepdims=False)
    for l in reversed(range(DEPTH)):
        p, s = params[l], saved[l]
        dycat = _mm_dycat(f"dycat{l}", dz, p["wout"], token)
        dwout_l[l] = _mm_dwout(f"dwout{l}", s["ycat"], dz)
        dproj, dwdw, dcvec, dwpw = _conv_bwd(f"conv_bwd{l}", s["proj"], dycat, s["c1"], p["wdw"], p["cvec"], p["wpw"])
        dwo = jnp.concatenate([dwout_l[l][1024:1536], dwout_l[l][0:1024], dwout_l[l][1536:]], axis=0)
        dwo = dwo.reshape(N_DEV, D_MODEL // N_DEV, D_MODEL)
        dpw = dwpw.reshape(N_DEV, CONV_W // N_DEV, CONV_W)
        early = ([dwo, dpw], [_landing(own(dwo), me), _landing(own(dpw), me)],
                 [(0, True, 0, None), (1, True, 1, None)])
        started_early, token = _xchg_start(f"grads_start_out{l}", [early])
        dq, dgate, dk, dv, dsink = _attn_bwd(f"attn_bwd{l}", s["qr"], s["kr"], s["proj"], p["sinks"], dycat, token)
        dproj = _attn_assemble(f"attn_asm{l}", dq, dgate, dk, dv, tabs, dproj)
        dproj, dwl, dlvec, dwa, dwx = _lru_bwd(f"lru_bwd{l}", s["proj"], dycat, s["hstate"],
                                                p["wl"], p["lvec"], p["wa"], p["wx"], dproj)
        g512 = jnp.concatenate([dcvec[0:4], dlvec[0:4], jnp.pad(dsink[0:1], ((0, 0), (0, 512 - 128))),
                                jnp.zeros((_W512_ROWS - 9, 512), F32)], axis=0)
        g_layers[l] = dict(dwdw=dwdw[:CONV_K], dwl=dwl[:LRU_CONV_K], g512=g512,
                           dwa=_diag_blocks(dwa), dwx=_diag_blocks(dwx))
        if l == 0:
            g512 = jnp.concatenate([g_layers[i]["g512"] for i in range(DEPTH)], axis=0)
            g64 = jnp.concatenate([g_layers[i][k] for k in ("dwa", "dwx") for i in range(DEPTH)], axis=0)
            g64 = g64.reshape(-1, 128)
            vgroup = ([g512, g64], [_landing(g512, me), _landing(g64, me)],
                      [(0, False, 0, None), (1, False, 1, None)])
            vstarted, token = _xchg_start("vector_grads_start", [vgroup])
        dwin_l[l] = _mm_dwin(f"dwin{l}", s["hb"], dproj, token)
        late = ([dwin_l[l]], [_landing(own(dwin_l[l]), me)], [(0, True, 0, None)])
        started_late, token = _xchg_start(f"grads_start_in{l}", [late])
        ggroups[l] = [(late, started_late[0]), (early, started_early[0])]
        dh = _mm_dh(f"dh{l}", dproj, wg_in[l], dz, token)
        if l > 0:
            dz, ln_stats[l] = _ln_bwd(f"ln_bwd{l}", dh, z[l], row(ln_g[l]))
        else:
            grad_x, gmeta, ln_stats[0] = _ln_bwd_input(dh, z[0], row(ln_g[0]))

    loss_row = jnp.pad(loss_blk[0:1, :], ((0, 0), (0, D_MODEL - 128)))
    g2048 = jnp.concatenate([ln_stats[0][0:2], ln_stats[1][0:1], ln_stats[2][0:1], ln_stats[1][1:2],
                             ln_stats[2][1:2], loss_row, jnp.zeros((1, D_MODEL), F32)], axis=0)
    g_dw_full = jnp.stack([g_layers[l]["dwdw"] for l in range(DEPTH)])
    g_lc_full = jnp.stack([g_layers[l]["dwl"] for l in range(DEPTH)])
    shard_pack = _pack_rows([_cols_to_slots(g_dw_full), _cols_to_slots(g_lc_full), _cols_to_slots(gmeta)],
                            lead=(N_DEV,))
    sgroup = ([shard_pack, g2048], [_landing(own(shard_pack), me), _landing(g2048, me)],
              [(0, True, 0, None), (1, False, 1, None)])
    sstarted, token = _xchg_start("small_grads_start", [sgroup])

    res = {}

    def flat2(a, cols):
        return a.reshape(-1, cols)

    big = (("w_in", 0, W_IN_SHARD), ("w_out", 1, D_MODEL), ("conv_pw_w", 2, CONV_W))
    prev = {n: None for n, _, _ in big}
    def update(name_, cols, recv, l):
        w_ = weights[name_]
        prev[name_] = _adamw(f"adamw_{name_}{l}", flat2(w_, cols), flat2(mom1[name_], cols),
                             flat2(mom2[name_], cols), recv, row0=l * w_.shape[1], prev=prev[name_])

    after = [token]
    for l in reversed(range(DEPTH)):
        late, early = ggroups[l]
        r_out, r_pw = _xchg_wait(f"grads_wait{l}_1", early[0], early[1], after)
        if l > 0:
            r_in, = _xchg_wait(f"grads_wait{l}_0", late[0], late[1], after)
            update("w_in", W_IN_SHARD, r_in, l)
        update("w_out", D_MODEL, r_out, l)
        update("conv_pw_w", CONV_W, r_pw, l)
        after = [prev["w_out"][0], prev["conv_pw_w"][0]]
    late = ggroups[0][0]
    r_in, = _xchg_wait("grads_wait0_0", late[0], late[1], after)
    update("w_in", W_IN_SHARD, r_in, 0)
    for name_, _, _ in big:
        res[name_] = [o.reshape(weights[name_].shape) for o in prev[name_]]

    big_done = [prev[n][0] for n, _, _ in big]
    r_512, r_64 = _xchg_wait("vector_grads_wait", vgroup, vstarted[0], big_done)
    r_small, r_2048 = _xchg_wait("small_grads_wait", sgroup, sstarted[0], big_done)
    r_rep = [r_512, r_2048, r_64]
    sshapes = [weights[n].shape for n in shard_small_names]
    outs = _adamw("adamw_small_sharded", *shard_wmv, r_small)
    for k, o in enumerate(outs):
        for n, a in zip(shard_small_names, _unpack_rows(o, sshapes)):
            res.setdefault(n, [None] * 4)[k] = a

    outs = [_adamw(f"adamw_small_w{tag}", rep_wmv[0][ci], rep_wmv[1][ci], rep_wmv[2][ci], r_rep[ci])
            for ci, tag in enumerate(("512", "2048", "64"))]
    for k in range(4):
        for n, a in _unpack_small(outs[0][k], outs[1][k], outs[2][k]).items():
            res.setdefault(n, [None] * 4)[k] = a
    loss = outs[1][0][6, 0]

    order = ["meta_tokens", "ln_in_g", "ln_in_b", "w_in", "conv_dw_w", "conv_dw_b", "conv_ln_g", "conv_ln_b",
             "conv_pw_w", "conv_pw_b", "attn_sinks", "lru_conv_w", "lru_conv_b", "lru_wa", "lru_ba", "lru_wx",
             "lru_bx", "lru_lambda", "w_out", "ln_post_g", "ln_post_b"]
    return (loss, grad_x,
            *[res[n][0] for n in order], *[res[n][1] for n in order],
            *[res[n][2] for n in order], *[res[n][3] for n in order])
```

```python
import functools
import math

import numpy as np
import jax
import jax.numpy as jnp
from jax import lax
from jax.experimental import pallas as pl
from jax.experimental.pallas import tpu as pltpu

F32 = jnp.float32
BF16 = jnp.bfloat16

D_MODEL = 2048
DEPTH = 2
N_META = 16
TB = 128
PAD0 = TB - N_META
CONV_W = 512
CONV_K = 31
HEAD_DIM = 64
N_HEADS = 16
N_KV = 4
GROUP = 4
ATT_W = 1024
KV_W = 256
ROT_DIM = 16
ROPE_THETA = 500000.0
LRU_W = 512
LRU_HEADS = 8
LRU_CONV_K = 4
LRU_C = 8.0
IN_TOTAL = 5120
N_DEV = 8
W_IN_SHARD = IN_TOTAL // N_DEV
LN_EPS = 1e-5
ALPHA = (2.0 * DEPTH) ** 0.25
NEG_INF = -1e30
ATT_SCALE = HEAD_DIM ** -0.5

ADAM_LR = 0.001
ADAM_B1 = 0.9
ADAM_B2 = 0.999
ADAM_EPS = 1e-08
ADAM_WD = 0.01
ADAM_STEP = 10

VMEM_LIMIT = 56 * 1024 * 1024
PACK_QUANTUM = 256 * 128

COL_CV, COL_CG, COL_CGATE = 0, 1, 2
COL_Q0 = 3
COL_K256 = 10
COL_V256 = 11
COL_AGATE1024 = 3
COL_RX, COL_RGATE = 8, 9
YC_CONV, YC_LRU = 2, 3


def _cp(n_axes, vmem=VMEM_LIMIT):
    return pltpu.CompilerParams(dimension_semantics=("arbitrary",) * n_axes, vmem_limit_bytes=vmem)


def _row_tile(lp, max_blocks):
    nb = lp // TB
    d = max(k for k in range(1, max_blocks + 1) if nb % k == 0)
    return TB * d


def _sig(x):
    return jax.nn.sigmoid(x)


def _dsilu(x, s):
    return s * (1.0 + x * (1.0 - s))


def _ln_core(z):
    mu = jnp.mean(z, axis=-1, keepdims=True)
    zc = z - mu
    var = jnp.mean(zc * zc, axis=-1, keepdims=True)
    rstd = lax.rsqrt(var + LN_EPS)
    return zc * rstd, rstd


def _ln_bwd_core(dy, xh, rstd, g):
    dxh = dy * g
    m1 = jnp.mean(dxh, axis=-1, keepdims=True)
    m2 = jnp.mean(dxh * xh, axis=-1, keepdims=True)
    return rstd * (dxh - m1 - xh * m2)


def _row_ids(shape, base):
    return lax.broadcasted_iota(jnp.int32, shape, 0) + base


def _colsum(x):
    return jnp.sum(x, axis=0, keepdims=True)


def _dot(a, b, dims):
    return lax.dot_general(a, b, (dims, ((), ())), preferred_element_type=F32)


NN = ((1,), (0,))
NT = ((1,), (1,))
TN = ((0,), (0,))


def _embed(x, meta_full, g, b):
    s = x.shape[1]
    lp = s + TB
    nb = lp // TB

    def body(x_ref, m_ref, g_ref, b_ref, o_ref, hb_ref):
        i = pl.program_id(0)

        @pl.when(i == 0)
        def _():
            o_ref[0:PAD0, :] = jnp.zeros((PAD0, D_MODEL), F32)
            o_ref[PAD0:TB, :] = m_ref[...]

        @pl.when(i > 0)
        def _():
            o_ref[...] = x_ref[...]

        xh, _ = _ln_core(o_ref[...])
        h = xh * g_ref[...] + b_ref[...]
        rows = _row_ids(h.shape, i * TB)
        hb_ref[...] = jnp.where(rows >= PAD0, h, 0.0).astype(BF16)

    return pl.pallas_call(
        body, name="embed", grid=(nb,),
        in_specs=[pl.BlockSpec((None, TB, D_MODEL), lambda i: (0, jnp.maximum(i - 1, 0), 0)),
                  pl.BlockSpec((N_META, D_MODEL), lambda i: (0, 0)),
                  pl.BlockSpec((1, D_MODEL), lambda i: (0, 0)),
                  pl.BlockSpec((1, D_MODEL), lambda i: (0, 0))],
        out_specs=[pl.BlockSpec((TB, D_MODEL), lambda i: (i, 0)),
                   pl.BlockSpec((TB, D_MODEL), lambda i: (i, 0))],
        out_shape=[jax.ShapeDtypeStruct((lp, D_MODEL), F32),
                   jax.ShapeDtypeStruct((lp, D_MODEL), BF16)],
        compiler_params=_cp(1),
    )(x, meta_full, g, b)


def _loss_head(z, target, g, b):
    lp = z.shape[0]
    nb = lp // TB

    def body(z_ref, t_ref, g_ref, b_ref, dz_ref, st_ref, loss_ref):
        i = pl.program_id(0)

        @pl.when(i == 0)
        def _():
            st_ref[...] = jnp.zeros(st_ref.shape, F32)
            loss_ref[...] = jnp.zeros(loss_ref.shape, F32)
            dz_ref[...] = jnp.zeros(dz_ref.shape, F32)

        @pl.when(i > 0)
        def _():
            xh, rstd = _ln_core(z_ref[...])
            gg = g_ref[...]
            y = xh * gg + b_ref[...]
            e = y - t_ref[...]
            part = 0.5 * jnp.sum(jnp.mean(e * e, axis=-1, keepdims=True), axis=0, keepdims=True)
            loss_ref[...] += jnp.broadcast_to(part, loss_ref.shape)
            dy = e / float(D_MODEL)
            st_ref[0:1, :] += _colsum(dy * xh)
            st_ref[1:2, :] += _colsum(dy)
            dz_ref[...] = _ln_bwd_core(dy, xh, rstd, gg)

    return pl.pallas_call(
        body, name="loss_head", grid=(nb,),
        in_specs=[pl.BlockSpec((TB, D_MODEL), lambda i: (i, 0)),
                  pl.BlockSpec((None, TB, D_MODEL), lambda i: (0, jnp.maximum(i - 1, 0), 0)),
                  pl.BlockSpec((1, D_MODEL), lambda i: (0, 0)),
                  pl.BlockSpec((1, D_MODEL), lambda i: (0, 0))],
        out_specs=[pl.BlockSpec((TB, D_MODEL), lambda i: (i, 0)),
                   pl.BlockSpec((8, D_MODEL), lambda i: (0, 0)),
                   pl.BlockSpec((8, 128), lambda i: (0, 0))],
        out_shape=[jax.ShapeDtypeStruct((lp, D_MODEL), F32),
                   jax.ShapeDtypeStruct((8, D_MODEL), F32),
                   jax.ShapeDtypeStruct((8, 128), F32)],
        compiler_params=_cp(1),
    )(z, target, g, b)


def _ln_bwd(name, dh, z, g):
    lp = z.shape[0]
    tr = _row_tile(lp, 3)

    def body(dh_ref, z_ref, g_ref, dz_ref, st_ref):
        i = pl.program_id(0)

        @pl.when(i == 0)
        def _():
            st_ref[...] = jnp.zeros(st_ref.shape, F32)

        xh, rstd = _ln_core(z_ref[...])
        rows = _row_ids(xh.shape, i * tr)
        dy = jnp.where(rows >= PAD0, dh_ref[...], 0.0)
        st_ref[0:1, :] += _colsum(dy * xh)
        st_ref[1:2, :] += _colsum(dy)
        dz_ref[...] = _ln_bwd_core(dy, xh, rstd, g_ref[...])

    return pl.pallas_call(
        body, name=name, grid=(lp // tr,),
        in_specs=[pl.BlockSpec((tr, D_MODEL), lambda i: (i, 0)),
                  pl.BlockSpec((tr, D_MODEL), lambda i: (i, 0)),
                  pl.BlockSpec((1, D_MODEL), lambda i: (0, 0))],
        out_specs=[pl.BlockSpec((tr, D_MODEL), lambda i: (i, 0)),
                   pl.BlockSpec((8, D_MODEL), lambda i: (0, 0))],
        out_shape=[jax.ShapeDtypeStruct((lp, D_MODEL), F32),
                   jax.ShapeDtypeStruct((8, D_MODEL), F32)],
        compiler_params=_cp(1),
    )(dh, z, g)


def _ln_bwd_input(dh, z, g):
    lp = z.shape[0]
    nb = lp // TB
    s = lp - TB

    def body(dh_ref, z_ref, g_ref, gx_ref, gm_ref, st_ref):
        i = pl.program_id(0)

        @pl.when(i == 0)
        def _():
            st_ref[...] = jnp.zeros(st_ref.shape, F32)

        xh, rstd = _ln_core(z_ref[...])
        rows = _row_ids(xh.shape, i * TB)
        dy = jnp.where(rows >= PAD0, dh_ref[...], 0.0)
        st_ref[0:1, :] += _colsum(dy * xh)
        st_ref[1:2, :] += _colsum(dy)
        dz = _ln_bwd_core(dy, xh, rstd, g_ref[...])
        gx_ref[...] = dz

        @pl.when(i == 0)
        def _():
            gm_ref[...] = dz[PAD0:TB, :]

    return pl.pallas_call(
        body, name="ln_in_bwd", grid=(nb,),
        in_specs=[pl.BlockSpec((TB, D_MODEL), lambda i: (i, 0)),
                  pl.BlockSpec((TB, D_MODEL), lambda i: (i, 0)),
                  pl.BlockSpec((1, D_MODEL), lambda i: (0, 0))],
        out_specs=[pl.BlockSpec((None, TB, D_MODEL), lambda i: (0, jnp.maximum(i - 1, 0), 0)),
                   pl.BlockSpec((N_META, D_MODEL), lambda i: (0, 0)),
                   pl.BlockSpec((8, D_MODEL), lambda i: (0, 0))],
        out_shape=[jax.ShapeDtypeStruct((1, s, D_MODEL), F32),
                   jax.ShapeDtypeStruct((N_META, D_MODEL), F32),
                   jax.ShapeDtypeStruct((8, D_MODEL), F32)],
        compiler_params=_cp(1),
    )(dh, z, g)


def _mm_proj(name, hb, wg_in):
    lp = hb.shape[0]
    tm = lp // 3

    def body(a_ref, b_ref, o_ref):
        b = jnp.concatenate([b_ref[0], b_ref[1]], axis=1)
        o_ref[...] = _dot(a_ref[...], b, NN)

    return pl.pallas_call(
        body, name=name, grid=(3, N_DEV // 2),
        in_specs=[pl.BlockSpec((tm, D_MODEL), lambda i, j: (i, 0)),
                  pl.BlockSpec((2, D_MODEL, W_IN_SHARD), lambda i, j: (j, 0, 0))],
        out_specs=pl.BlockSpec((tm, 2 * W_IN_SHARD), lambda i, j: (i, j)),
        out_shape=jax.ShapeDtypeStruct((lp, IN_TOTAL), F32),
        compiler_params=_cp(2),
    )(hb, wg_in)


def _mm_out(name, ycat, wout, z, g, b, g2, b2, dep):
    lp = ycat.shape[0]
    tm = lp // 6

    def body(a_ref, w_ref, z_ref, g_ref, b_ref, g2_ref, b2_ref, dep_ref, o_ref, hb_ref):
        del dep_ref
        i = pl.program_id(0)
        xh, _ = _ln_core(z_ref[...])
        h = xh * g_ref[...] + b_ref[...]
        live = _row_ids(h.shape, i * tm) >= PAD0
        h = jnp.where(live, h, 0.0)
        zn = ALPHA * h + _dot(a_ref[...], w_ref[...], NN)
        o_ref[...] = zn
        xh2, _ = _ln_core(zn)
        hb_ref[...] = jnp.where(live, xh2 * g2_ref[...] + b2_ref[...], 0.0).astype(BF16)

    vec = pl.BlockSpec((1, D_MODEL), lambda i: (0, 0))
    return pl.pallas_call(
        body, name=name, grid=(6,),
        in_specs=[pl.BlockSpec((tm, D_MODEL), lambda i: (i, 0)),
                  pl.BlockSpec((D_MODEL, D_MODEL), lambda i: (0, 0), pipeline_mode=pl.Buffered(1)),
                  pl.BlockSpec((tm, D_MODEL), lambda i: (i, 0)),
                  vec, vec, vec, vec, pl.BlockSpec(memory_space=pl.ANY)],
        out_specs=[pl.BlockSpec((tm, D_MODEL), lambda i: (i, 0)),
                   pl.BlockSpec((tm, D_MODEL), lambda i: (i, 0))],
        out_shape=[jax.ShapeDtypeStruct((lp, D_MODEL), F32),
                   jax.ShapeDtypeStruct((lp, D_MODEL), BF16)],
        compiler_params=_cp(1),
    )(ycat, wout, z, g, b, g2, b2, dep)


def _mm_dycat(name, dz, wout, dep):
    lp = dz.shape[0]
    tm = lp // 6

    def body(a_ref, w_ref, dep_ref, o_ref):
        del dep_ref
        o_ref[...] = _dot(a_ref[...].astype(BF16), w_ref[...], NT).astype(BF16)

    return pl.pallas_call(
        body, name=name, grid=(6,),
        in_specs=[pl.BlockSpec((tm, D_MODEL), lambda i: (i, 0)),
                  pl.BlockSpec((D_MODEL, D_MODEL), lambda i: (0, 0), pipeline_mode=pl.Buffered(1)),
                  pl.BlockSpec(memory_space=pl.ANY)],
        out_specs=pl.BlockSpec((tm, D_MODEL), lambda i: (i, 0)),
        out_shape=jax.ShapeDtypeStruct((lp, D_MODEL), BF16),
        compiler_params=_cp(1),
    )(dz, wout, dep)


def _mm_dwout(name, ycat, dz):
    lp = ycat.shape[0]
    tk = _row_tile(lp, 11)
    nk = lp // tk
    half = D_MODEL // 2

    def body(a_ref, b_ref, o_ref, acc_ref):
        k = pl.program_id(1)

        @pl.when(k == 0)
        def _():
            acc_ref[...] = jnp.zeros(acc_ref.shape, F32)

        acc_ref[...] += _dot(a_ref[...], b_ref[...].astype(BF16), TN)

        @pl.when(k == nk - 1)
        def _():
            o_ref[...] = acc_ref[...].astype(BF16)

    return pl.pallas_call(
        body, name=name, grid=(2, nk),
        in_specs=[pl.BlockSpec((tk, half), lambda h, k: (k, h)),
                  pl.BlockSpec((tk, D_MODEL), lambda h, k: (k, 0))],
        out_specs=pl.BlockSpec((half, D_MODEL), lambda h, k: (h, 0)),
        out_shape=jax.ShapeDtypeStruct((D_MODEL, D_MODEL), BF16),
        scratch_shapes=[pltpu.VMEM((half, D_MODEL), F32)],
        compiler_params=_cp(2),
    )(ycat, dz)


def _mm_dwin(name, hb, dproj, dep):
    lp = hb.shape[0]
    tk = _row_tile(lp, 11)
    nk = lp // tk

    def body(a_ref, b_ref, dep_ref, o_ref, acc_ref):
        del dep_ref
        k = pl.program_id(1)

        @pl.when(k == 0)
        def _():
            acc_ref[...] = jnp.zeros(acc_ref.shape, F32)

        acc_ref[...] += _dot(a_ref[...], b_ref[...], TN)

        @pl.when(k == nk - 1)
        def _():
            o_ref[0] = acc_ref[:, 0:W_IN_SHARD].astype(BF16)
            o_ref[1] = acc_ref[:, W_IN_SHARD:2 * W_IN_SHARD].astype(BF16)

    return pl.pallas_call(
        body, name=name, grid=(4, nk),
        in_specs=[pl.BlockSpec((tk, D_MODEL), lambda j, k: (k, 0)),
                  pl.BlockSpec((tk, 2 * W_IN_SHARD), lambda j, k: (k, j)),
                  pl.BlockSpec(memory_space=pl.ANY)],
        out_specs=pl.BlockSpec((2, D_MODEL, W_IN_SHARD), lambda j, k: (j, 0, 0)),
        out_shape=jax.ShapeDtypeStruct((N_DEV, D_MODEL, W_IN_SHARD), BF16),
        scratch_shapes=[pltpu.VMEM((D_MODEL, 2 * W_IN_SHARD), F32)],
        compiler_params=_cp(2),
    )(hb, dproj, dep)


def _mm_dh(name, dproj, wg_in, dz, dep):
    lp = dproj.shape[0]
    tm = lp // 6

    def body(a_ref, w_ref, dz_ref, dep_ref, o_ref, acc_ref):
        del dep_ref
        k = pl.program_id(1)

        @pl.when(k == 0)
        def _():
            acc_ref[...] = jnp.zeros(acc_ref.shape, F32)

        w = jnp.concatenate([w_ref[0], w_ref[1]], axis=1)
        acc_ref[...] += _dot(a_ref[...], w, NT)

        @pl.when(k == N_DEV // 2 - 1)
        def _():
            o_ref[...] = acc_ref[...] + ALPHA * dz_ref[...]

    return pl.pallas_call(
        body, name=name, grid=(6, N_DEV // 2),
        in_specs=[pl.BlockSpec((tm, 2 * W_IN_SHARD), lambda i, k: (i, k)),
                  pl.BlockSpec((2, D_MODEL, W_IN_SHARD), lambda i, k: (k, 0, 0)),
                  pl.BlockSpec((tm, D_MODEL), lambda i, k: (i, 0)),
                  pl.BlockSpec(memory_space=pl.ANY)],
        out_specs=pl.BlockSpec((tm, D_MODEL), lambda i, k: (i, 0)),
        out_shape=jax.ShapeDtypeStruct((lp, D_MODEL), F32),
        scratch_shapes=[pltpu.VMEM((tm, D_MODEL), F32)],
        compiler_params=_cp(2),
    )(dproj, wg_in, dz, dep)


SUB = 128


def _shift_plan(cat, n_shift, base):
    rolled = [cat] + [pltpu.roll(cat, b, axis=0) for b in range(1, 8)]
    return [(rolled[s % 8], base - 8 * (s // 8)) for s in range(n_shift)]


def _tap_sum(w_ref, plan, rows, init=None):
    blocks = []
    for r0 in range(0, rows, SUB):
        row = []
        for c0 in range(0, CONV_W, SUB):
            acc = (jnp.zeros((SUB, SUB), F32) if init is None
                   else jnp.broadcast_to(init[:, c0:c0 + SUB], (SUB, SUB)))
            for k, (arr, off) in enumerate(plan):
                acc = acc + w_ref[k:k + 1, c0:c0 + SUB] * arr[off + r0:off + r0 + SUB, c0:c0 + SUB]
            row.append(acc)
        blocks.append(jnp.concatenate(row, axis=1))
    return jnp.concatenate(blocks, axis=0)


def _tap_grads(dw_ref, dy, plan, rows):
    for c0 in range(0, CONV_W, SUB):
        dys = [dy[r0:r0 + SUB, c0:c0 + SUB] for r0 in range(0, rows, SUB)]
        for k, (arr, off) in enumerate(plan):
            part = None
            for ri, r0 in enumerate(range(0, rows, SUB)):
                prod = dys[ri] * arr[off + r0:off + r0 + SUB, c0:c0 + SUB]
                for i in range(SUB // 8):
                    piece = prod[8 * i:8 * i + 8, :]
                    part = piece if part is None else part + piece
            dw_ref[k:k + 1, c0:c0 + SUB] += jnp.sum(part, axis=0, keepdims=True)


CONV_HALO = 32


def _conv_chain(j, tb, cv_ref, cg_ref, cvp_ref, cgp_ref, wdw_ref, vec_ref, wpw_ref, c1_ref=None):
    cv = cv_ref[...]
    sg = _sig(cg_ref[...])
    c0 = cv * sg
    c0p = jnp.where(j > 0, cvp_ref[...] * _sig(cgp_ref[...]), 0.0)
    cat = jnp.concatenate([c0p, c0], axis=0)
    shifts = _shift_plan(cat, CONV_K, CONV_HALO)
    taps = [shifts[CONV_K - 1 - k] for k in range(CONV_K)]
    if c1_ref is None:
        c1 = _tap_sum(wdw_ref, taps, tb, init=vec_ref[0:1, :])
    else:
        c1 = c1_ref[...]
    xh, rstd = _ln_core(c1)
    c2 = xh * vec_ref[1:2, :] + vec_ref[2:3, :]
    s2 = _sig(c2)
    c3 = c2 * s2
    c4 = _dot(c3.astype(BF16), wpw_ref[...], NN) + vec_ref[3:4, :]
    return dict(cv=cv, sg=sg, taps=taps, c1=c1, xh=xh, rstd=rstd, c2=c2, s2=s2, c3=c3, c4=c4)


def _conv_in_specs(jmap, tb):
    def cur(col):
        return pl.BlockSpec((tb, 512), lambda n: (jmap(n), col))

    def prev(col):
        return pl.BlockSpec((CONV_HALO, 512),
                            lambda n: (jnp.maximum(jmap(n) * (tb // CONV_HALO) - 1, 0), col))

    return [cur(COL_CV), cur(COL_CG), prev(COL_CV), prev(COL_CG), cur(COL_CGATE)]


def _conv_param_specs():
    return [pl.BlockSpec((32, CONV_W), lambda n: (0, 0)),
            pl.BlockSpec((8, CONV_W), lambda n: (0, 0)),
            pl.BlockSpec((CONV_W, CONV_W), lambda n: (0, 0))]


def _conv_fwd(name, proj, wdw, vec, wpw):
    lp = proj.shape[0]
    tb = _row_tile(lp, 3)
    nb = lp // tb

    def body(cv_ref, cg_ref, cvp_ref, cgp_ref, gate_ref, wdw_ref, vec_ref, wpw_ref, o_ref, c1_ref):
        j = pl.program_id(0)
        c = _conv_chain(j, tb, cv_ref, cg_ref, cvp_ref, cgp_ref, wdw_ref, vec_ref, wpw_ref)
        gate = gate_ref[...]
        o_ref[...] = (c["c4"] * (gate * _sig(gate))).astype(BF16)
        c1_ref[...] = c["c1"]

    return pl.pallas_call(
        body, name=name, grid=(nb,),
        in_specs=_conv_in_specs(lambda n: n, tb) + _conv_param_specs(),
        out_specs=[pl.BlockSpec((tb, 512), lambda n: (n, YC_CONV)),
                   pl.BlockSpec((tb, CONV_W), lambda n: (n, 0))],
        out_shape=[jax.ShapeDtypeStruct((lp, D_MODEL), BF16),
                   jax.ShapeDtypeStruct((lp, CONV_W), F32)],
        compiler_params=_cp(1),
    )(proj, proj, proj, proj, proj, wdw, vec, wpw)


def _conv_bwd(name, proj, dycat, c1, wdw, vec, wpw):
    lp = proj.shape[0]
    tb = _row_tile(lp, 3)
    nb = lp // tb
    halo = CONV_HALO

    def body(cv_ref, cg_ref, cvp_ref, cgp_ref, gate_ref, dy_ref, c1_ref, wdw_ref, vec_ref, wpw_ref,
             dp_ref, dwdw_ref, dvec_ref, dwpw_ref, carry_ref):
        n = pl.program_id(0)
        j = nb - 1 - n

        @pl.when(n == 0)
        def _():
            carry_ref[...] = jnp.zeros(carry_ref.shape, F32)
            dwdw_ref[...] = jnp.zeros(dwdw_ref.shape, F32)
            dvec_ref[...] = jnp.zeros(dvec_ref.shape, F32)
            dwpw_ref[...] = jnp.zeros(dwpw_ref.shape, F32)

        c = _conv_chain(j, tb, cv_ref, cg_ref, cvp_ref, cgp_ref, wdw_ref, vec_ref, wpw_ref, c1_ref)
        dy = dy_ref[...].astype(F32)
        gate = gate_ref[...]
        sgate = _sig(gate)
        dc4 = dy * (gate * sgate)
        dgate = dy * c["c4"] * _dsilu(gate, sgate)
        dc4b = dc4.astype(BF16)
        dvec_ref[3:4, :] += _colsum(dc4)
        dwpw_ref[...] += _dot(c["c3"].astype(BF16), dc4b, TN)
        dc3 = _dot(dc4b, wpw_ref[...], NT)
        dc2 = dc3 * _dsilu(c["c2"], c["s2"])
        dvec_ref[1:2, :] += _colsum(dc2 * c["xh"])
        dvec_ref[2:3, :] += _colsum(dc2)
        dc1 = _ln_bwd_core(dc2, c["xh"], c["rstd"], vec_ref[1:2, :])
        dvec_ref[0:1, :] += _colsum(dc1)
        _tap_grads(dwdw_ref, dc1, c["taps"], tb)
        dcat = jnp.concatenate([dc1, carry_ref[...]], axis=0)
        total = tb + halo
        up = [dcat] + [pltpu.roll(dcat, total - b, axis=0) for b in range(1, 8)]
        ahead = [(up[(CONV_K - 1 - k) % 8], 8 * ((CONV_K - 1 - k) // 8)) for k in range(CONV_K)]
        dc0 = _tap_sum(wdw_ref, ahead, tb)
        carry_ref[...] = dc1[0:halo, :]
        sg = c["sg"]
        dcv = dc0 * sg
        dcg = dc0 * c["cv"] * sg * (1.0 - sg)
        dp_ref[:, 0:512] = dcv.astype(BF16)
        dp_ref[:, 512:1024] = dcg.astype(BF16)
        dp_ref[:, 1024:1536] = dgate.astype(BF16)

    jmap = lambda n: nb - 1 - n
    return pl.pallas_call(
        body, name=name, grid=(nb,),
        in_specs=(_conv_in_specs(jmap, tb)
                  + [pl.BlockSpec((tb, 512), lambda n: (jmap(n), YC_CONV)),
                     pl.BlockSpec((tb, CONV_W), lambda n: (jmap(n), 0))]
                  + _conv_param_specs()),
        out_specs=[pl.BlockSpec((tb, 1536), lambda n: (jmap(n), 0)),
                   pl.BlockSpec((32, CONV_W), lambda n: (0, 0)),
                   pl.BlockSpec((8, CONV_W), lambda n: (0, 0)),
                   pl.BlockSpec((CONV_W, CONV_W), lambda n: (0, 0))],
        out_shape=[jax.ShapeDtypeStruct((lp, IN_TOTAL), BF16),
                   jax.ShapeDtypeStruct((32, CONV_W), F32),
                   jax.ShapeDtypeStruct((8, CONV_W), F32),
                   jax.ShapeDtypeStruct((CONV_W, CONV_W), F32)],
        scratch_shapes=[pltpu.VMEM((halo, CONV_W), F32)],
        compiler_params=_cp(1),
    )(proj, proj, proj, proj, proj, dycat, c1, wdw, vec, wpw)


def _rope_tables(lp):
    half = ROT_DIM // 2
    inv_freq = ROPE_THETA ** (-jnp.arange(half, dtype=F32) / half)
    pos = (jnp.arange(lp, dtype=jnp.int32) - PAD0).astype(F32)
    ang = pos[:, None] * inv_freq[None, :]
    cos, sin = jnp.cos(ang), jnp.sin(ang)
    ones = jnp.ones((lp, HEAD_DIM - ROT_DIM), F32)
    zeros = jnp.zeros((lp, HEAD_DIM - ROT_DIM), F32)
    zh = jnp.zeros((lp, half), F32)
    c = jnp.concatenate([cos, cos, ones], axis=1)
    sa = jnp.concatenate([-sin, zh, zeros], axis=1)
    sb = jnp.concatenate([zh, sin, zeros], axis=1)
    tile = lambda t: jnp.tile(t, (1, KV_W // HEAD_DIM))
    return tile(c), tile(sa), tile(sb)


def _rot(x, c, sa, sb):
    w = x.shape[1]
    return x * c + pltpu.roll(x, w - 8, axis=1) * sa + pltpu.roll(x, 8, axis=1) * sb


def _rot_t(dy, c, sa, sb):
    w = dy.shape[1]
    return dy * c + pltpu.roll(dy * sa, 8, axis=1) + pltpu.roll(dy * sb, w - 8, axis=1)


def _rope_fwd(name, proj, tabs):
    lp = proj.shape[0]
    tr = _row_tile(lp, 11)

    def body(q0_ref, q1_ref, k_ref, c_ref, sa_ref, sb_ref, qr_ref, kr_ref):
        c, sa, sb = c_ref[...], sa_ref[...], sb_ref[...]
        c2 = jnp.concatenate([c, c], axis=1)
        sa2 = jnp.concatenate([sa, sa], axis=1)
        sb2 = jnp.concatenate([sb, sb], axis=1)
        qr_ref[:, 0:512] = (_rot(q0_ref[...], c2, sa2, sb2) * ATT_SCALE).astype(BF16)
        qr_ref[:, 512:1024] = (_rot(q1_ref[...], c2, sa2, sb2) * ATT_SCALE).astype(BF16)
        kr_ref[...] = _rot(k_ref[...], c, sa, sb).astype(BF16)

    tab = pl.BlockSpec((tr, KV_W), lambda i: (i, 0))
    return pl.pallas_call(
        body, name=name, grid=(lp // tr,),
        in_specs=[pl.BlockSpec((tr, 512), lambda i: (i, COL_Q0)),
                  pl.BlockSpec((tr, 512), lambda i: (i, COL_Q0 + 1)),
                  pl.BlockSpec((tr, KV_W), lambda i: (i, COL_K256)),
                  tab, tab, tab],
        out_specs=[pl.BlockSpec((tr, ATT_W), lambda i: (i, 0)),
                   pl.BlockSpec((tr, KV_W), lambda i: (i, 0))],
        out_shape=[jax.ShapeDtypeStruct((lp, ATT_W), BF16),
                   jax.ShapeDtypeStruct((lp, KV_W), BF16)],
        compiler_params=_cp(1),
    )(proj, proj, proj, *tabs)


def _attn_mask(j):
    qi = lax.broadcasted_iota(jnp.int32, (GROUP * TB, 3 * TB), 0) & (TB - 1)
    cc = lax.broadcasted_iota(jnp.int32, (GROUP * TB, 3 * TB), 1)
    jj = cc & (TB - 1)
    is_meta = jj >= PAD0
    p0 = (cc < TB) & is_meta & (j >= 1)
    p1 = (cc >= TB) & (cc < 2 * TB) & (jj > qi) & (j >= 2)
    p2 = (cc >= 2 * TB) & (jj <= qi) & ((j >= 1) | is_meta)
    return p0 | p1 | p2


def _lane_group(rows):
    return lax.broadcasted_iota(jnp.int32, (rows, KV_W), 1) // HEAD_DIM


def _stack_heads(x, kv, lgq):
    parts = []
    for g in range(GROUP):
        sh = ((kv - g) % GROUP) * HEAD_DIM
        moved = x if sh == 0 else pltpu.roll(x, sh, axis=1)
        parts.append(jnp.where(lgq == kv, moved, 0.0))
    return jnp.concatenate(parts, axis=0).astype(BF16)


def _unstack_heads(r, kv):
    out = None
    for g in range(GROUP):
        blk = r[g * TB:(g + 1) * TB, :]
        sh = ((g - kv) % GROUP) * HEAD_DIM
        blk = blk if sh == 0 else pltpu.roll(blk, sh, axis=1)
        out = blk if out is None else out + blk
    return out


def _sink_column(sinks, kv):
    lane = lax.broadcasted_iota(jnp.int32, (1, 128), 1)
    cols = []
    for g in range(GROUP):
        sg = jnp.sum(jnp.where(lane == kv * GROUP + g, sinks, 0.0), axis=1, keepdims=True)
        cols.append(jnp.broadcast_to(sg, (TB, 1)))
    return jnp.concatenate(cols, axis=0)


def _attn_kv(kall, vall, lg, kv):
    km = jnp.where(lg == kv, kall, 0.0).astype(BF16)
    vm = jnp.where(lg == kv, vall, 0.0).astype(BF16)
    ones = jnp.where(lg == kv, 1.0, 0.0).astype(BF16)
    return km, vm, ones


def _attn_specs(jmap):
    blk = lambda col: pl.BlockSpec((TB, KV_W), lambda n: (jmap(n), col))
    prv = lambda col: pl.BlockSpec((TB, KV_W), lambda n: (jnp.maximum(jmap(n) - 1, 0), col))
    met = lambda col: pl.BlockSpec((TB, KV_W), lambda n: (0, col))
    return dict(
        qr=pl.BlockSpec((TB, ATT_W), lambda n: (jmap(n), 0)),
        k=[met(0), prv(0), blk(0)],
        v=[met(COL_V256), prv(COL_V256), blk(COL_V256)],
        gate=pl.BlockSpec((TB, ATT_W), lambda n: (jmap(n), COL_AGATE1024)),
        sinks=pl.BlockSpec((8, 128), lambda n: (0, 0)),
    )


ATTN_BWD_HEAD_SETS = ((0, 1, 2, 3),)


def _attn_fwd(name, qr, kr, proj, sinks_row, ycat):
    lp = proj.shape[0]
    nb = lp // TB
    sp = _attn_specs(lambda n: n)

    def body(qr_ref, km_ref, kp_ref, kc_ref, vm_ref, vp_ref, vc_ref, gate_ref, sink_ref, yin_ref, o_ref):
        del yin_ref
        j = pl.program_id(0)
        valid = _attn_mask(j)
        kall = jnp.concatenate([km_ref[...], kp_ref[...], kc_ref[...]], axis=0).astype(F32)
        vall = jnp.concatenate([vm_ref[...], vp_ref[...], vc_ref[...]], axis=0)
        lg = _lane_group(3 * TB)
        lgq = _lane_group(TB)
        lg4 = _lane_group(GROUP * TB)
        sinks = sink_ref[0:1, :]
        heads = range(N_KV)
        cols = [slice(kv * KV_W, (kv + 1) * KV_W) for kv in heads]
        kvo = [_attn_kv(kall, vall, lg, kv) for kv in heads]
        qst = [_stack_heads(qr_ref[:, cols[kv]].astype(F32), kv, lgq) for kv in heads]
        s = [jnp.where(valid, _dot(qst[kv], kvo[kv][0], NT), NEG_INF) for kv in heads]
        eb, es = [], []
        for kv in heads:
            sinkcol = _sink_column(sinks, kv)
            m = jnp.maximum(jnp.max(s[kv], axis=-1, keepdims=True), sinkcol)
            eb.append(jnp.exp(s[kv] - m).astype(BF16))
            es.append(jnp.exp(sinkcol - m))
        r = [_dot(eb[kv], kvo[kv][1], NN) for kv in heads]
        inv = [1.0 / (_dot(eb[kv], kvo[kv][2], NN) + es[kv]) for kv in heads]
        for kv in heads:
            out = jnp.where(lg4 == kv, r[kv] * inv[kv], 0.0)
            gate = gate_ref[:, cols[kv]]
            o_ref[:, cols[kv]] = (_unstack_heads(out, kv) * (gate * _sig(gate))).astype(BF16)

    return pl.pallas_call(
        body, name=name, grid=(nb,),
        in_specs=[sp["qr"]] + sp["k"] + sp["v"] + [sp["gate"], sp["sinks"],
                                                   pl.BlockSpec(memory_space=pl.ANY)],
        out_specs=pl.BlockSpec((TB, ATT_W), lambda n: (n, 0)),
        out_shape=jax.ShapeDtypeStruct((lp, D_MODEL), BF16),
        input_output_aliases={9: 0},
        compiler_params=_cp(1),
    )(qr, kr, kr, kr, proj, proj, proj, proj, sinks_row, ycat)


def _attn_bwd(name, qr, kr, proj, sinks_row, dycat, dep):
    lp = proj.shape[0]
    nb = lp // TB
    sp = _attn_specs(lambda n: n)

    def body(qr_ref, km_ref, kp_ref, kc_ref, vm_ref, vp_ref, vc_ref, gate_ref, sink_ref, dy_ref, dep_ref,
             dq_ref, dgate_ref, dk_ref, dv_ref, dsink_ref):
        del dep_ref
        j = pl.program_id(0)

        @pl.when(j == 0)
        def _():
            dk_ref[...] = jnp.zeros(dk_ref.shape, F32)
            dv_ref[...] = jnp.zeros(dv_ref.shape, F32)
            dsink_ref[...] = jnp.zeros(dsink_ref.shape, F32)

        valid = _attn_mask(j)
        kall = jnp.concatenate([km_ref[...], kp_ref[...], kc_ref[...]], axis=0).astype(F32)
        vall = jnp.concatenate([vm_ref[...], vp_ref[...], vc_ref[...]], axis=0)
        lg = _lane_group(3 * TB)
        lgq = _lane_group(TB)
        lg4 = _lane_group(GROUP * TB)
        sinks = sink_ref[0:1, :]
        lane = lax.broadcasted_iota(jnp.int32, (1, 128), 1)
        def stages(heads):
            dsink = jnp.zeros((1, 128), F32)
            cols = {kv: slice(kv * KV_W, (kv + 1) * KV_W) for kv in heads}
            kvo = {kv: _attn_kv(kall, vall, lg, kv) for kv in heads}
            qst = {kv: _stack_heads(qr_ref[:, cols[kv]].astype(F32), kv, lgq) for kv in heads}
            s = {kv: jnp.where(valid, _dot(qst[kv], kvo[kv][0], NT), NEG_INF) for kv in heads}
            eb, es = {}, {}
            for kv in heads:
                sinkcol = _sink_column(sinks, kv)
                m = jnp.maximum(jnp.max(s[kv], axis=-1, keepdims=True), sinkcol)
                eb[kv] = jnp.exp(s[kv] - m).astype(BF16)
                es[kv] = jnp.exp(sinkcol - m)
            r = {kv: _dot(eb[kv], kvo[kv][1], NN) for kv in heads}
            inv = {kv: 1.0 / (_dot(eb[kv], kvo[kv][2], NN) + es[kv]) for kv in heads}
            dost, dcol = {}, {}
            for kv in heads:
                att = _unstack_heads(jnp.where(lg4 == kv, r[kv] * inv[kv], 0.0), kv)
                gate = gate_ref[:, cols[kv]]
                sgate = _sig(gate)
                dy = dy_ref[:, cols[kv]].astype(F32)
                dgate_ref[:, cols[kv]] = (dy * att * _dsilu(gate, sgate)).astype(BF16)
                dsc = dy * (gate * sgate) * _unstack_heads(jnp.where(lg4 == kv, inv[kv], 0.0), kv)
                dost[kv] = _stack_heads(dsc, kv, lgq)
                dd = dsc * att
                dcol[kv] = jnp.concatenate(
                    [jnp.sum(jnp.where(lgq == g, dd, 0.0), axis=1, keepdims=True) for g in range(GROUP)], axis=0)
            dp = {kv: _dot(dost[kv], kvo[kv][1], NT) for kv in heads}
            ds = {}
            for kv in heads:
                ds[kv] = (eb[kv].astype(F32) * (dp[kv] - dcol[kv])).astype(BF16)
                pd = es[kv] * dcol[kv]
                for g in range(GROUP):
                    tot = jnp.sum(pd[g * TB:(g + 1) * TB, :], axis=0, keepdims=True)
                    dsink = dsink - jnp.where(lane == kv * GROUP + g, tot, 0.0)
            dqs = {kv: _dot(ds[kv], kvo[kv][0], NN) for kv in heads}
            dks = [_dot(ds[kv], qst[kv], TN) for kv in heads]
            dvs = [_dot(eb[kv], dost[kv], TN) for kv in heads]
            for kv in heads:
                dq_ref[:, cols[kv]] = _unstack_heads(dqs[kv], kv)
            return sum(dks[1:], dks[0]), sum(dvs[1:], dvs[0]), dsink

        parts = [stages(hs) for hs in ATTN_BWD_HEAD_SETS]
        dkall = sum([p[0] for p in parts[1:]], parts[0][0])
        dvall = sum([p[1] for p in parts[1:]], parts[0][1])
        dsink = sum([p[2] for p in parts[1:]], parts[0][2])
        dsink_ref[0:1, :] += dsink
        prev = pl.multiple_of(jnp.maximum(j - 1, 0) * TB, TB)
        cur = pl.multiple_of(j * TB, TB)
        dk_ref[0:TB, :] += dkall[0:TB]
        dv_ref[0:TB, :] += dvall[0:TB]
        dk_ref[pl.ds(prev, TB), :] += dkall[TB:2 * TB]
        dv_ref[pl.ds(prev, TB), :] += dvall[TB:2 * TB]
        dk_ref[pl.ds(cur, TB), :] += dkall[2 * TB:3 * TB]
        dv_ref[pl.ds(cur, TB), :] += dvall[2 * TB:3 * TB]

    return pl.pallas_call(
        body, name=name, grid=(nb,),
        in_specs=[sp["qr"]] + sp["k"] + sp["v"] + [sp["gate"], sp["sinks"],
                                                   pl.BlockSpec((TB, ATT_W), lambda n: (n, 0)),
                                                   pl.BlockSpec(memory_space=pl.ANY)],
        out_specs=[pl.BlockSpec((TB, ATT_W), lambda n: (n, 0)),
                   pl.BlockSpec((TB, ATT_W), lambda n: (n, 0)),
                   pl.BlockSpec((lp, KV_W), lambda n: (0, 0)),
                   pl.BlockSpec((lp, KV_W), lambda n: (0, 0)),
                   pl.BlockSpec((8, 128), lambda n: (0, 0))],
        out_shape=[jax.ShapeDtypeStruct((lp, ATT_W), F32),
                   jax.ShapeDtypeStruct((lp, ATT_W), BF16),
                   jax.ShapeDtypeStruct((lp, KV_W), F32),
                   jax.ShapeDtypeStruct((lp, KV_W), F32),
                   jax.ShapeDtypeStruct((8, 128), F32)],
        compiler_params=_cp(1),
    )(qr, kr, kr, kr, proj, proj, proj, proj, sinks_row, dycat, dep)


def _attn_assemble(name, dq, dgate, dk, dv, tabs, dproj):
    lp = dq.shape[0]
    tr = _row_tile(lp, 11)

    def body(dq_ref, dg_ref, dk_ref, dv_ref, c_ref, sa_ref, sb_ref, din_ref, o_ref):
        del din_ref
        cidx = pl.program_id(1)
        c, sa, sb = c_ref[...], sa_ref[...], sb_ref[...]

        @pl.when(cidx < 2)
        def _():
            c2 = jnp.concatenate([c, c], axis=1)
            sa2 = jnp.concatenate([sa, sa], axis=1)
            sb2 = jnp.concatenate([sb, sb], axis=1)
            o_ref[...] = (_rot_t(dq_ref[...], c2, sa2, sb2) * ATT_SCALE).astype(BF16)

        @pl.when(cidx == 2)
        def _():
            o_ref[:, 0:KV_W] = _rot_t(dk_ref[...], c, sa, sb).astype(BF16)
            o_ref[:, KV_W:2 * KV_W] = dv_ref[...].astype(BF16)

        @pl.when(cidx > 2)
        def _():
            o_ref[...] = dg_ref[...]

    tab = pl.BlockSpec((tr, KV_W), lambda n, c: (n, 0))
    return pl.pallas_call(
        body, name=name, grid=(lp // tr, 5),
        in_specs=[pl.BlockSpec((tr, 512), lambda n, c: (n, jnp.minimum(c, 1))),
                  pl.BlockSpec((tr, 512), lambda n, c: (n, jnp.clip(c - 3, 0, 1))),
                  tab, tab,
                  tab, tab, tab,
                  pl.BlockSpec(memory_space=pl.ANY)],
        out_specs=pl.BlockSpec((tr, 512), lambda n, c: (n, COL_Q0 + c)),
        out_shape=jax.ShapeDtypeStruct((lp, IN_TOTAL), BF16),
        input_output_aliases={7: 0},
        compiler_params=_cp(2),
    )(dq, dgate, dk, dv, *tabs, dproj)


def _softplus_neg(lam):
    t = jnp.exp(-jnp.abs(lam))
    u = 1.0 + t
    den = jnp.where(u == 1.0, 1.0, u - 1.0)
    l1p = jnp.where(u == 1.0, t, jnp.log(u) * (t / den))
    return jnp.maximum(-lam, 0.0) + l1p


def _lru_chain(j, tb, rx_ref, rxp_ref, wl_ref, vec_ref, wa_ref, wx_ref):
    rx = rx_ref[...]
    rxp = jnp.where(j > 0, rxp_ref[...], 0.0)
    cat = jnp.concatenate([rxp, rx], axis=0)
    views = [cat[8:8 + tb, :]] + [pltpu.roll(cat, s, axis=0)[8:8 + tb, :] for s in range(1, LRU_CONV_K)]
    x1 = jnp.broadcast_to(vec_ref[0:1, :], (tb, LRU_W))
    for k in range(LRU_CONV_K):
        x1 = x1 + wl_ref[k:k + 1, :] * views[LRU_CONV_K - 1 - k]
    x1b = x1.astype(BF16)
    r = _sig(_dot(x1b, wa_ref[...], NN) + vec_ref[1:2, :])
    ig = _sig(_dot(x1b, wx_ref[...], NN) + vec_ref[2:3, :])
    sp = _softplus_neg(vec_ref[3:4, :])
    log_a = -LRU_C * r * sp
    rows = _row_ids((tb, LRU_W), j * tb)
    live = rows >= PAD0
    a = jnp.where(live, jnp.exp(log_a), 0.0)
    y2 = 2.0 * log_a
    em = -jnp.tanh(0.5 * y2) * (jnp.exp(y2) + 1.0)
    mult = jnp.sqrt(em)
    return dict(views=views, x1=x1, x1b=x1b, r=r, ig=ig, sp=sp, a=a, mult=mult, live=live, a_raw=jnp.exp(log_a))


def _scan_slabs(a, u, forward):
    tb = a.shape[0]
    rows = lax.broadcasted_iota(jnp.int32, (tb, SUB), 0)
    outs_a, outs_u = [], []
    for c0 in range(0, a.shape[1], SUB):
        ac, uc = a[:, c0:c0 + SUB], u[:, c0:c0 + SUB]
        d = 1
        while d < tb:
            if forward:
                keep, sh = rows >= d, d
            else:
                keep, sh = rows < tb - d, tb - d
            an = jnp.where(keep, pltpu.roll(ac, sh, axis=0), 1.0)
            un = jnp.where(keep, pltpu.roll(uc, sh, axis=0), 0.0)
            uc = ac * un + uc
            ac = ac * an
            d *= 2
        outs_a.append(ac)
        outs_u.append(uc)
    return jnp.concatenate(outs_a, axis=1), jnp.concatenate(outs_u, axis=1)


def _lru_specs(jmap, tb):
    return [pl.BlockSpec((tb, 512), lambda n: (jmap(n), COL_RX)),
            pl.BlockSpec((8, 512), lambda n: (jnp.maximum(jmap(n) * (tb // 8) - 1, 0), COL_RX)),
            pl.BlockSpec((tb, 512), lambda n: (jmap(n), COL_RGATE))]


def _lru_param_specs():
    return [pl.BlockSpec((8, LRU_W), lambda n: (0, 0)),
            pl.BlockSpec((8, LRU_W), lambda n: (0, 0)),
            pl.BlockSpec((LRU_W, LRU_W), lambda n: (0, 0)),
            pl.BlockSpec((LRU_W, LRU_W), lambda n: (0, 0))]


def _lru_fwd(name, proj, wl, vec, wa, wx, ycat):
    lp = proj.shape[0]
    tb = _row_tile(lp, 3)
    nb = lp // tb

    def body(rx_ref, rxp_ref, gate_ref, wl_ref, vec_ref, wa_ref, wx_ref, yin_ref, o_ref, h_ref, carry_ref):
        del yin_ref
        j = pl.program_id(0)

        @pl.when(j == 0)
        def _():
            carry_ref[...] = jnp.zeros(carry_ref.shape, F32)

        c = _lru_chain(j, tb, rx_ref, rxp_ref, wl_ref, vec_ref, wa_ref, wx_ref)
        u = jnp.where(c["live"], c["mult"] * (c["ig"] * c["x1"]), 0.0)
        a, u = _scan_slabs(c["a"], u, forward=True)
        h = u + a * carry_ref[0:1, :]
        carry_ref[...] = h[tb - 8:tb, :]
        carry_ref[0:1, :] = h[tb - 1:tb, :]
        h_ref[...] = h
        gate = gate_ref[...]
        o_ref[...] = (h * (gate * _sig(gate))).astype(BF16)

    return pl.pallas_call(
        body, name=name, grid=(nb,),
        in_specs=_lru_specs(lambda n: n, tb) + _lru_param_specs() + [pl.BlockSpec(memory_space=pl.ANY)],
        out_specs=[pl.BlockSpec((tb, 512), lambda n: (n, YC_LRU)),
                   pl.BlockSpec((tb, LRU_W), lambda n: (n, 0))],
        out_shape=[jax.ShapeDtypeStruct((lp, D_MODEL), BF16),
                   jax.ShapeDtypeStruct((lp, LRU_W), F32)],
        input_output_aliases={7: 0},
        scratch_shapes=[pltpu.VMEM((8, LRU_W), F32)],
        compiler_params=_cp(1),
    )(proj, proj, proj, wl, vec, wa, wx, ycat)


def _lru_bwd(name, proj, dycat, hstate, wl, vec, wa, wx, dproj):
    lp = proj.shape[0]
    tb = _row_tile(lp, 3)
    nb = lp // tb

    def body(rx_ref, rxp_ref, gate_ref, dy_ref, h_ref, hp_ref, wl_ref, vec_ref, wa_ref, wx_ref, din_ref,
             dp_ref, dwl_ref, dvec_ref, dwa_ref, dwx_ref, dhc_ref, anx_ref, dxc_ref):
        del din_ref
        n = pl.program_id(0)
        j = nb - 1 - n

        @pl.when(n == 0)
        def _():
            dhc_ref[...] = jnp.zeros(dhc_ref.shape, F32)
            anx_ref[...] = jnp.zeros(anx_ref.shape, F32)
            dxc_ref[...] = jnp.zeros(dxc_ref.shape, F32)
            dwl_ref[...] = jnp.zeros(dwl_ref.shape, F32)
            dvec_ref[...] = jnp.zeros(dvec_ref.shape, F32)
            dwa_ref[...] = jnp.zeros(dwa_ref.shape, F32)
            dwx_ref[...] = jnp.zeros(dwx_ref.shape, F32)

        c = _lru_chain(j, tb, rx_ref, rxp_ref, wl_ref, vec_ref, wa_ref, wx_ref)
        a, mult, r, ig, x1, live = c["a"], c["mult"], c["r"], c["ig"], c["x1"], c["live"]
        h = h_ref[...]
        gate = gate_ref[...]
        sgate = _sig(gate)
        dy = dy_ref[...].astype(F32)
        gsum = dy * (gate * sgate)
        dgate = dy * h * _dsilu(gate, sgate)
        rows = lax.broadcasted_iota(jnp.int32, (tb, LRU_W), 0)
        bb = jnp.where(rows == tb - 1, anx_ref[0:1, :], pltpu.roll(a, tb - 1, axis=0))
        bb, gg = _scan_slabs(bb, gsum, forward=False)
        dh = gg + bb * dhc_ref[0:1, :]
        dhc_ref[...] = dh[0:8, :]
        anx_ref[...] = a[0:8, :]
        hprev = jnp.where(rows == 0, jnp.where(j > 0, hp_ref[7:8, :], 0.0), pltpu.roll(h, 1, axis=0))
        du = jnp.where(live, dh, 0.0)
        da = jnp.where(live, dh * hprev, 0.0)
        ar = c["a_raw"]
        dmult = du * (ig * x1)
        di = du * mult * x1
        dx1 = du * mult * ig
        dloga = da * ar - dmult * ar * ar / mult
        dr = dloga * (-LRU_C * c["sp"])
        dvec_ref[3:4, :] += _colsum(dloga * (-LRU_C * r))
        dza = dr * r * (1.0 - r)
        dzx = di * ig * (1.0 - ig)
        dzab, dzxb = dza.astype(BF16), dzx.astype(BF16)
        dvec_ref[1:2, :] += _colsum(dza)
        dvec_ref[2:3, :] += _colsum(dzx)
        dwa_ref[...] += _dot(c["x1b"], dzab, TN)
        dwx_ref[...] += _dot(c["x1b"], dzxb, TN)
        dx1 = dx1 + _dot(dzab, wa_ref[...], NT) + _dot(dzxb, wx_ref[...], NT)
        dvec_ref[0:1, :] += _colsum(dx1)
        for k in range(LRU_CONV_K):
            dwl_ref[k:k + 1, :] += _colsum(dx1 * c["views"][LRU_CONV_K - 1 - k])
        dcat = jnp.concatenate([dx1, dxc_ref[...]], axis=0)
        drx = jnp.zeros((tb, LRU_W), F32)
        for k in range(LRU_CONV_K):
            s = LRU_CONV_K - 1 - k
            view = dcat[0:tb, :] if s == 0 else pltpu.roll(dcat, tb + 8 - s, axis=0)[0:tb, :]
            drx = drx + wl_ref[k:k + 1, :] * view
        dxc_ref[...] = dx1[0:8, :]
        dp_ref[:, 0:512] = drx.astype(BF16)
        dp_ref[:, 512:1024] = dgate.astype(BF16)

        @pl.when(n == nb - 1)
        def _():
            lam = vec_ref[3:4, :]
            dvec_ref[3:4, :] = dvec_ref[3:4, :] * (-_sig(-lam))

    jmap = lambda n: nb - 1 - n
    return pl.pallas_call(
        body, name=name, grid=(nb,),
        in_specs=(_lru_specs(jmap, tb)
                  + [pl.BlockSpec((tb, 512), lambda n: (jmap(n), YC_LRU)),
                     pl.BlockSpec((tb, LRU_W), lambda n: (jmap(n), 0)),
                     pl.BlockSpec((8, LRU_W), lambda n: (jnp.maximum(jmap(n) * (tb // 8) - 1, 0), 0))]
                  + _lru_param_specs() + [pl.BlockSpec(memory_space=pl.ANY)]),
        out_specs=[pl.BlockSpec((tb, 1024), lambda n: (jmap(n), 4)),
                   pl.BlockSpec((8, LRU_W), lambda n: (0, 0)),
                   pl.BlockSpec((8, LRU_W), lambda n: (0, 0)),
                   pl.BlockSpec((LRU_W, LRU_W), lambda n: (0, 0)),
                   pl.BlockSpec((LRU_W, LRU_W), lambda n: (0, 0))],
        out_shape=[jax.ShapeDtypeStruct((lp, IN_TOTAL), BF16),
                   jax.ShapeDtypeStruct((8, LRU_W), F32),
                   jax.ShapeDtypeStruct((8, LRU_W), F32),
                   jax.ShapeDtypeStruct((LRU_W, LRU_W), F32),
                   jax.ShapeDtypeStruct((LRU_W, LRU_W), F32)],
        input_output_aliases={10: 0},
        scratch_shapes=[pltpu.VMEM((8, LRU_W), F32), pltpu.VMEM((8, LRU_W), F32), pltpu.VMEM((8, LRU_W), F32)],
        compiler_params=_cp(1),
    )(proj, proj, proj, dycat, hstate, hstate, wl, vec, wa, wx, dproj)


_HBM = pl.BlockSpec(memory_space=pltpu.HBM)
_SEM = pl.BlockSpec(memory_space=pltpu.SEMAPHORE)
_ANY = pl.BlockSpec(memory_space=pl.ANY)
_EFFECT = pltpu.SideEffectType.DATAFLOW_SIDE_EFFECTING


def _hbm(a):
    return pltpu.with_memory_space_constraint(a, pltpu.HBM)


_ALL_PEERS = tuple(range(1, N_DEV))
_CHIP_PEERS = (1, 2, 4, 6)
_OTHER_CHIPS = (2, 4, 6)


def _spec_peers(mode):
    return {"ici": _CHIP_PEERS, "fwd": _OTHER_CHIPS}.get(mode, _ALL_PEERS)


def _split_descriptors(copies, srcs, lands, send_sems, recv_sems):
    x, y, c = lax.axis_index("x"), lax.axis_index("y"), lax.axis_index("c")
    me = 4 * x + 2 * y + c
    out, sem = [], 0
    for si, mode, li, ll in copies:
        for k in _spec_peers(mode):
            px = 1 - x if k & 4 else x
            py = 1 - y if k & 2 else y
            pc = 1 - c if k & 1 else c
            peer = 4 * px + 2 * py + pc
            if mode == "fwd":
                src = dst = lands[li].at[peer]
                target = (x, y, 1 - c)
            else:
                src = srcs[si].at[peer] if mode is True else srcs[si]
                dst = lands[li].at[me] if ll is None else lands[li].at[me, ll]
                target = (px, py, pc)
            out.append(pltpu.make_async_remote_copy(
                src_ref=src, dst_ref=dst, send_sem=send_sems.at[sem], recv_sem=recv_sems.at[sem],
                device_id=target, device_id_type=pl.DeviceIdType.MESH))
            sem += 1
    return out


def _n_copies(copies):
    return sum(len(_spec_peers(mode)) for _, mode, _, _ in copies)


def _xchg_start(name, groups):
    n_src = [len(g[0]) for g in groups]
    n_land = [len(g[1]) for g in groups]
    srcs = [s for g in groups for s in g[0]]
    lands = [l for g in groups for l in g[1]]
    ns, nl, ng = len(srcs), len(lands), len(groups)

    def body(*refs):
        src_refs, land_refs = refs[:ns], refs[ns:ns + nl]
        sems = refs[ns + nl:ns + nl + 2 * ng]
        token = refs[-1]
        so = lo = 0
        for gi, (_, _, copies) in enumerate(groups):
            for d in _split_descriptors(copies, src_refs[so:so + n_src[gi]], land_refs[lo:lo + n_land[gi]],
                                        sems[2 * gi], sems[2 * gi + 1]):
                d.start()
            so += n_src[gi]
            lo += n_land[gi]
        token[...] = jnp.zeros(token.shape, F32)

    out_shape, out_specs = [], []
    for g in groups:
        n = _n_copies(g[2])
        out_shape += [pltpu.SemaphoreType.DMA((n,)), pltpu.SemaphoreType.DMA((n,))]
        out_specs += [_SEM, _SEM]
    out_shape += [pltpu.HBM(l.shape, l.dtype) for l in lands]
    out_specs += [_HBM] * nl
    out_shape.append(jax.ShapeDtypeStruct((8, 128), F32))
    out_specs.append(pl.BlockSpec(memory_space=pltpu.VMEM))
    outs = pl.pallas_call(
        body, name=name, in_specs=[_HBM] * (ns + nl), out_specs=out_specs, out_shape=out_shape,
        input_output_aliases={ns + i: 2 * ng + i for i in range(nl)},
        compiler_params=pltpu.CompilerParams(has_side_effects=_EFFECT),
    )(*[_hbm(a) for a in srcs + lands])
    res, lo = [], 2 * ng
    for gi in range(ng):
        res.append((outs[2 * gi], outs[2 * gi + 1], list(outs[lo:lo + n_land[gi]])))
        lo += n_land[gi]
    return res, outs[-1]


def _xchg_wait(name, group, started, after):
    srcs, _, copies = group
    send_sems, recv_sems, lands = started
    ns, nl = len(srcs), len(lands)
    after = list(after)

    def body(*refs):
        src_refs, land_refs = refs[:ns], refs[ns:ns + nl]
        send_ref, recv_ref = refs[ns + nl], refs[ns + nl + 1]
        for d in _split_descriptors(copies, src_refs, land_refs, send_ref, recv_ref):
            d.wait_send()
            d.wait_recv()

    outs = pl.pallas_call(
        body, name=name, in_specs=[_HBM] * (ns + nl) + [_SEM, _SEM] + [_ANY] * len(after),
        out_specs=[_HBM] * nl, out_shape=[pltpu.HBM(l.shape, l.dtype) for l in lands],
        input_output_aliases={ns + i: i for i in range(nl)},
        compiler_params=pltpu.CompilerParams(has_side_effects=_EFFECT),
    )(*[_hbm(a) for a in srcs], *lands, send_sems, recv_sems, *after)
    return list(outs)


def _landing(own, me):
    land = lax.empty((N_DEV,) + own.shape, own.dtype)
    return lax.dynamic_update_slice(land, own[None], (me,) + (0,) * own.ndim)


def _adamw(name, w, m, v, recv, row0=0, prev=None):
    cdim = w.shape[1]
    r = recv.shape[1]
    tr = r
    for cand in (512, 256, 128, 64, 32, 16, 8):
        if r % cand == 0 and r > cand:
            tr = cand
            break
    assert row0 % tr == 0
    blk0 = row0 // tr
    n_prev = 0 if prev is None else 4

    def body(w_ref, m_ref, v_ref, r_ref, *rest):
        g_ref, d_ref, mo_ref, vo_ref = rest[n_prev:]
        g = r_ref[0].astype(F32)
        for s in range(1, N_DEV):
            g = g + r_ref[s].astype(F32)
        mn = ADAM_B1 * m_ref[...] + (1.0 - ADAM_B1) * g
        vn = ADAM_B2 * v_ref[...] + (1.0 - ADAM_B2) * (g * g)
        m_hat = mn / (1.0 - ADAM_B1 ** ADAM_STEP)
        v_hat = vn / (1.0 - ADAM_B2 ** ADAM_STEP)
        g_ref[...] = g
        d_ref[...] = -ADAM_LR * (m_hat / (jnp.sqrt(v_hat) + ADAM_EPS) + ADAM_WD * w_ref[...])
        mo_ref[...] = mn
        vo_ref[...] = vn

    blk = pl.BlockSpec((tr, cdim), lambda i: (i + blk0, 0))
    return pl.pallas_call(
        body, name=name, grid=(r // tr,),
        in_specs=[blk, blk, blk, pl.BlockSpec((N_DEV, tr, cdim), lambda i: (0, i, 0))] + [_ANY] * n_prev,
        out_specs=[blk, blk, blk, blk],
        out_shape=[jax.ShapeDtypeStruct(w.shape, F32)] * 4,
        input_output_aliases={4 + i: i for i in range(n_prev)},
        compiler_params=_cp(1),
    )(w, m, v, recv, *(prev or []))


def _pack_rows(arrs, lead=()):
    n = len(lead)
    flat = jnp.concatenate([a.reshape(a.shape[:n] + (-1,)) for a in arrs], axis=-1)
    size = flat.shape[-1]
    padded = -(-size // PACK_QUANTUM) * PACK_QUANTUM
    flat = jnp.pad(flat, [(0, 0)] * n + [(0, padded - size)])
    return flat.reshape(flat.shape[:n] + (padded // 128, 128))


def _unpack_rows(packed, shapes, lead=()):
    n = len(lead)
    flat = packed.reshape(packed.shape[:n] + (-1,))
    out, off = [], 0
    for s in shapes:
        size = int(np.prod(s))
        out.append(flat[..., off:off + size].reshape(packed.shape[:n] + tuple(s)))
        off += size
    return out


def _block_diag(w):
    eye = jnp.eye(LRU_HEADS, dtype=w.dtype)
    return (eye[:, None, :, None] * w[:, :, None, :]).reshape(LRU_W, LRU_W)


def _diag_blocks(dense):
    t = dense.reshape(LRU_HEADS, 64, LRU_HEADS, 64)
    eye = jnp.eye(LRU_HEADS, dtype=dense.dtype)
    return jnp.sum(t * eye[:, None, :, None], axis=2).reshape(LRU_HEADS * 64, 64)


_W512_NAMES = ("conv_dw_b", "conv_ln_g", "conv_ln_b", "conv_pw_b", "lru_conv_b", "lru_ba", "lru_bx", "lru_lambda")
_W512_ROWS = 12


def _pack_small(d):
    sinks = jnp.pad(d["attn_sinks"], ((0, 0), (0, 512 - N_HEADS)))
    t = jnp.stack([d[n] for n in _W512_NAMES] + [sinks], axis=1)
    w512 = jnp.pad(t, ((0, 0), (0, _W512_ROWS - t.shape[1]), (0, 0))).reshape(DEPTH * _W512_ROWS, 512)
    w2048 = jnp.concatenate([d["ln_in_g"][None], d["ln_in_b"][None], d["ln_post_g"], d["ln_post_b"],
                             jnp.zeros((2, D_MODEL), F32)], axis=0)
    w64 = jnp.concatenate([d["lru_wa"].reshape(-1, 64), d["lru_wx"].reshape(-1, 64)], axis=0)
    return [w512, w2048, w64.reshape(-1, 128)]


def _unpack_small(w512, w2048, w64):
    t = w512.reshape(DEPTH, _W512_ROWS, 512)
    out = {n: t[:, i, :] for i, n in enumerate(_W512_NAMES)}
    out["attn_sinks"] = t[:, len(_W512_NAMES), :N_HEADS]
    out["ln_in_g"], out["ln_in_b"] = w2048[0], w2048[1]
    out["ln_post_g"], out["ln_post_b"] = w2048[2:4], w2048[4:6]
    w64 = w64.reshape(-1, 64)
    half = w64.shape[0] // 2
    out["lru_wa"] = w64[:half].reshape(DEPTH, LRU_HEADS, 64, 64)
    out["lru_wx"] = w64[half:].reshape(DEPTH, LRU_HEADS, 64, 64)
    return out


def _cols_to_slots(full):
    lead = full.shape[:-1]
    t = full.reshape(lead + (N_DEV, full.shape[-1] // N_DEV))
    return jnp.moveaxis(t, -2, 0)


def _slots_to_cols(slots):
    t = jnp.moveaxis(slots, 0, -2)
    return t.reshape(t.shape[:-2] + (t.shape[-2] * t.shape[-1],))


def kernel(x, meta_tokens, ln_in_g, ln_in_b, w_in, conv_dw_w, conv_dw_b, conv_ln_g, conv_ln_b, conv_pw_w, conv_pw_b, attn_sinks, lru_conv_w, lru_conv_b, lru_wa, lru_ba, lru_wx, lru_bx, lru_lambda, w_out, ln_post_g, ln_post_b, loss_target, m_meta_tokens, m_ln_in_g, m_ln_in_b, m_w_in, m_conv_dw_w, m_conv_dw_b, m_conv_ln_g, m_conv_ln_b, m_conv_pw_w, m_conv_pw_b, m_attn_sinks, m_lru_conv_w, m_lru_conv_b, m_lru_wa, m_lru_ba, m_lru_wx, m_lru_bx, m_lru_lambda, m_w_out, m_ln_post_g, m_ln_post_b, v_meta_tokens, v_ln_in_g, v_ln_in_b, v_w_in, v_conv_dw_w, v_conv_dw_b, v_conv_ln_g, v_conv_ln_b, v_conv_pw_w, v_conv_pw_b, v_attn_sinks, v_lru_conv_w, v_lru_conv_b, v_lru_wa, v_lru_ba, v_lru_wx, v_lru_bx, v_lru_lambda, v_w_out, v_ln_post_g, v_ln_post_b):
    seq = x.shape[1]
    lp = seq + TB
    row = lambda a: a.reshape(1, -1)
    rep_names = ["ln_in_g", "ln_in_b", "conv_dw_b", "conv_ln_g", "conv_ln_b", "conv_pw_b", "attn_sinks",
                 "lru_conv_b", "lru_wa", "lru_ba", "lru_wx", "lru_bx", "lru_lambda", "ln_post_g", "ln_post_b"]
    shard_small_names = ["conv_dw_w", "lru_conv_w", "meta_tokens"]
    weights = dict(meta_tokens=meta_tokens, ln_in_g=ln_in_g, ln_in_b=ln_in_b, w_in=w_in, conv_dw_w=conv_dw_w,
                   conv_dw_b=conv_dw_b, conv_ln_g=conv_ln_g, conv_ln_b=conv_ln_b, conv_pw_w=conv_pw_w,
                   conv_pw_b=conv_pw_b, attn_sinks=attn_sinks, lru_conv_w=lru_conv_w, lru_conv_b=lru_conv_b,
                   lru_wa=lru_wa, lru_ba=lru_ba, lru_wx=lru_wx, lru_bx=lru_bx, lru_lambda=lru_lambda,
                   w_out=w_out, ln_post_g=ln_post_g, ln_post_b=ln_post_b)
    mom1 = dict(meta_tokens=m_meta_tokens, ln_in_g=m_ln_in_g, ln_in_b=m_ln_in_b, w_in=m_w_in, conv_dw_w=m_conv_dw_w,
                conv_dw_b=m_conv_dw_b, conv_ln_g=m_conv_ln_g, conv_ln_b=m_conv_ln_b, conv_pw_w=m_conv_pw_w,
                conv_pw_b=m_conv_pw_b, attn_sinks=m_attn_sinks, lru_conv_w=m_lru_conv_w, lru_conv_b=m_lru_conv_b,
                lru_wa=m_lru_wa, lru_ba=m_lru_ba, lru_wx=m_lru_wx, lru_bx=m_lru_bx, lru_lambda=m_lru_lambda,
                w_out=m_w_out, ln_post_g=m_ln_post_g, ln_post_b=m_ln_post_b)
    mom2 = dict(meta_tokens=v_meta_tokens, ln_in_g=v_ln_in_g, ln_in_b=v_ln_in_b, w_in=v_w_in, conv_dw_w=v_conv_dw_w,
                conv_dw_b=v_conv_dw_b, conv_ln_g=v_conv_ln_g, conv_ln_b=v_conv_ln_b, conv_pw_w=v_conv_pw_w,
                conv_pw_b=v_conv_pw_b, attn_sinks=v_attn_sinks, lru_conv_w=v_lru_conv_w, lru_conv_b=v_lru_conv_b,
                lru_wa=v_lru_wa, lru_ba=v_lru_ba, lru_wx=v_lru_wx, lru_bx=v_lru_bx, lru_lambda=v_lru_lambda,
                w_out=v_w_out, ln_post_g=v_ln_post_g, ln_post_b=v_ln_post_b)
    shard_wmv = [_pack_rows([d[n] for n in shard_small_names]) for d in (weights, mom1, mom2)]
    rep_wmv = [_pack_small(d) for d in (weights, mom1, mom2)]
    gate_w = [(_block_diag(lru_wa[l]).astype(BF16), _block_diag(lru_wx[l]).astype(BF16)) for l in range(DEPTH)]
    tabs = _rope_tables(lp)
    prepared = (shard_wmv + [a for wmv in rep_wmv for a in wmv]
                + [w for pair in gate_w for w in pair] + list(tabs))

    small_shard_shapes = [conv_dw_w.shape, lru_conv_w.shape, meta_tokens.shape]
    small_shard = _pack_rows([conv_dw_w, lru_conv_w, meta_tokens])
    me = 4 * lax.axis_index("x") + 2 * lax.axis_index("y") + lax.axis_index("c")
    w_in_b = [w_in[l].astype(BF16) for l in range(DEPTH)]
    w_out_b = [w_out[l].astype(BF16) for l in range(DEPTH)]
    pw_b = conv_pw_w.astype(BF16)
    wgroups = [
        ([small_shard], [_landing(small_shard, me)], [(0, False, 0, None)]),
        ([w_in_b[0]], [_landing(w_in_b[0], me)], [(0, "ici", 0, None)]),
        ([pw_b, w_out_b[0]], [_landing(pw_b, me), _landing(w_out_b[0], me)],
         [(0, False, 0, None), (1, False, 1, None)]),
        ([w_in_b[1], w_out_b[1]], [_landing(w_in_b[1], me), _landing(w_out_b[1], me)],
         [(0, "ici", 0, None), (1, "ici", 1, None)]),
    ]
    wstarted, wtoken = _xchg_start("weights_start", wgroups)

    def pass_on(tag, parts):
        fwd = ([], list(parts), [(None, "fwd", i, None) for i in range(len(parts))])
        fstarted, ftoken = _xchg_start(f"weights_fwd_start_{tag}", [fwd])
        return (fwd, fstarted[0]), ftoken
    wg_small, = _xchg_wait("weights_wait_s", wgroups[0], wstarted[0], [wtoken])
    g_dw, g_lc, g_meta = _unpack_rows(wg_small, small_shard_shapes, lead=(N_DEV,))
    conv_dw_full = _slots_to_cols(g_dw)
    lru_conv_full = _slots_to_cols(g_lc)
    meta_full = _slots_to_cols(g_meta)
    wg_in = [None, None]
    wg_out = [None, None]
    wg_pw = None

    ln_g = [ln_in_g, ln_post_g[0], ln_post_g[1]]
    ln_b = [ln_in_b, ln_post_b[0], ln_post_b[1]]

    def layer_params(l):
        wdw = jnp.pad(conv_dw_full[l], ((0, 1), (0, 0)))
        cvec = jnp.pad(jnp.stack([conv_dw_b[l], conv_ln_g[l], conv_ln_b[l], conv_pw_b[l]]), ((0, 4), (0, 0)))
        wpw = wg_pw[:, l].reshape(CONV_W, CONV_W)
        sinks = jnp.pad(attn_sinks[l].reshape(1, N_HEADS), ((0, 7), (0, 128 - N_HEADS)))
        wl = jnp.pad(lru_conv_full[l], ((0, 4), (0, 0)))
        lvec = jnp.pad(jnp.stack([lru_conv_b[l], lru_ba[l], lru_bx[l], lru_lambda[l]]), ((0, 4), (0, 0)))
        wa, wx = gate_w[l]
        wo = wg_out[l].reshape(D_MODEL, D_MODEL)
        wout = jnp.concatenate([wo[512:1536], wo[0:512], wo[1536:]], axis=0)
        return dict(wdw=wdw, cvec=cvec, wpw=wpw, sinks=sinks, wl=wl, lvec=lvec, wa=wa, wx=wx, wout=wout)

    params = [None] * DEPTH

    z0, hb = _embed(x, meta_full, row(ln_g[0]), row(ln_b[0]))
    z = [z0]
    saved = []
    for l in range(DEPTH):
        if l == 0:
            parts = _xchg_wait("weights_wait_a", wgroups[1], wstarted[1], [hb] + prepared)
            pending, ftoken = pass_on("a", parts)
            wg_in[0], = _xchg_wait("weights_fwd_wait_a", *pending, [ftoken])
        else:
            wg_in[1], wg_out[1] = _xchg_wait("weights_fwd_wait_c", *pending_c, [hb])
        proj = _mm_proj(f"proj{l}", hb, wg_in[l])
        if l == 0:
            wg_pw, wg_out[0] = _xchg_wait("weights_wait_b", wgroups[2], wstarted[2], [proj])
        p = params[l] = layer_params(l)
        ycat, c1 = _conv_fwd(f"conv_fwd{l}", proj, p["wdw"], p["cvec"], p["wpw"])
        qr, kr = _rope_fwd(f"rope{l}", proj, tabs)
        ycat = _attn_fwd(f"attn_fwd{l}", qr, kr, proj, p["sinks"], ycat)
        ycat, hstate = _lru_fwd(f"lru_fwd{l}", proj, p["wl"], p["lvec"], p["wa"], p["wx"], ycat)
        if l == 0:
            pending_c, ftoken = pass_on("c", _xchg_wait("weights_wait_c", wgroups[3], wstarted[3], [ycat]))
        saved.append(dict(hb=hb, proj=proj, ycat=ycat, qr=qr, kr=kr, hstate=hstate, c1=c1))
        z_next, hb = _mm_out(f"out{l}", ycat, p["wout"], z[l], row(ln_g[l]), row(ln_b[l]),
                             row(ln_g[l + 1]), row(ln_b[l + 1]), ftoken)
        z.append(z_next)

    dz, st_post1, loss_blk = _loss_head(z[DEPTH], loss_target, row(ln_g[DEPTH]), row(ln_b[DEPTH]))

    ln_stats = {DEPTH: st_post1}
    g_layers = [None] * DEPTH
    dwin_l, dwout_l = [None] * DEPTH, [None] * DEPTH
    grad_x = gmeta = None
    token = wtoken
    ggroups = [None] * DEPTH
    own = lambda a: lax.dynamic_index_in_dim(a, me, 0, keepdims=False)
    for l in reversed(range(DEPTH)):
        p, s = params[l], saved[l]
        dycat = _mm_dycat(f"dycat{l}", dz, p["wout"], token)
        dwout_l[l] = _mm_dwout(f"dwout{l}", s["ycat"], dz)
        dproj, dwdw, dcvec, dwpw = _conv_bwd(f"conv_bwd{l}", s["proj"], dycat, s["c1"], p["wdw"], p["cvec"], p["wpw"])
        dwo = jnp.concatenate([dwout_l[l][1024:1536], dwout_l[l][0:1024], dwout_l[l][1536:]], axis=0)
        dwo = dwo.reshape(N_DEV, D_MODEL // N_DEV, D_MODEL)
        dpw = dwpw.reshape(N_DEV, CONV_W // N_DEV, CONV_W)
        early = ([dwo, dpw], [_landing(own(dwo), me), _landing(own(dpw), me)],
                 [(0, True, 0, None), (1, True, 1, None)])
        started_early, token = _xchg_start(f"grads_start_out{l}", [early])
        dq, dgate, dk, dv, dsink = _attn_bwd(f"attn_bwd{l}", s["qr"], s["kr"], s["proj"], p["sinks"], dycat, token)
        dproj = _attn_assemble(f"attn_asm{l}", dq, dgate, dk, dv, tabs, dproj)
        dproj, dwl, dlvec, dwa, dwx = _lru_bwd(f"lru_bwd{l}", s["proj"], dycat, s["hstate"],
                                                p["wl"], p["lvec"], p["wa"], p["wx"], dproj)
        g512 = jnp.concatenate([dcvec[0:4], dlvec[0:4], jnp.pad(dsink[0:1], ((0, 0), (0, 512 - 128))),
                                jnp.zeros((_W512_ROWS - 9, 512), F32)], axis=0)
        g_layers[l] = dict(dwdw=dwdw[:CONV_K], dwl=dwl[:LRU_CONV_K], g512=g512,
                           dwa=_diag_blocks(dwa), dwx=_diag_blocks(dwx))
        if l == 0:
            g512 = jnp.concatenate([g_layers[i]["g512"] for i in range(DEPTH)], axis=0)
            g64 = jnp.concatenate([g_layers[i][k] for k in ("dwa", "dwx") for i in range(DEPTH)], axis=0)
            g64 = g64.reshape(-1, 128)
            vgroup = ([g512, g64], [_landing(g512, me), _landing(g64, me)],
                      [(0, False, 0, None), (1, False, 1, None)])
            vstarted, token = _xchg_start("vector_grads_start", [vgroup])
        dwin_l[l] = _mm_dwin(f"dwin{l}", s["hb"], dproj, token)
        late = ([dwin_l[l]], [_landing(own(dwin_l[l]), me)], [(0, True, 0, None)])
        started_late, token = _xchg_start(f"grads_start_in{l}", [late])
        ggroups[l] = [(late, started_late[0]), (early, started_early[0])]
        dh = _mm_dh(f"dh{l}", dproj, wg_in[l], dz, token)
        if l > 0:
            dz, ln_stats[l] = _ln_bwd(f"ln_bwd{l}", dh, z[l], row(ln_g[l]))
        else:
            grad_x, gmeta, ln_stats[0] = _ln_bwd_input(dh, z[0], row(ln_g[0]))

    loss_row = jnp.pad(loss_blk[0:1, :], ((0, 0), (0, D_MODEL - 128)))
    g2048 = jnp.concatenate([ln_stats[0][0:2], ln_stats[1][0:1], ln_stats[2][0:1], ln_stats[1][1:2],
                             ln_stats[2][1:2], loss_row, jnp.zeros((1, D_MODEL), F32)], axis=0)
    g_dw_full = jnp.stack([g_layers[l]["dwdw"] for l in range(DEPTH)])
    g_lc_full = jnp.stack([g_layers[l]["dwl"] for l in range(DEPTH)])
    shard_pack = _pack_rows([_cols_to_slots(g_dw_full), _cols_to_slots(g_lc_full), _cols_to_slots(gmeta)],
                            lead=(N_DEV,))
    sgroup = ([shard_pack, g2048], [_landing(own(shard_pack), me), _landing(g2048, me)],
              [(0, True, 0, None), (1, False, 1, None)])
    sstarted, token = _xchg_start("small_grads_start", [sgroup])

    res = {}

    def flat2(a, cols):
        return a.reshape(-1, cols)

    big = (("w_in", 0, W_IN_SHARD), ("w_out", 1, D_MODEL), ("conv_pw_w", 2, CONV_W))
    prev = {n: None for n, _, _ in big}
    def update(name_, cols, recv, l):
        w_ = weights[name_]
        prev[name_] = _adamw(f"adamw_{name_}{l}", flat2(w_, cols), flat2(mom1[name_], cols),
                             flat2(mom2[name_], cols), recv, row0=l * w_.shape[1], prev=prev[name_])

    after = [token]
    for l in reversed(range(DEPTH)):
        late, early = ggroups[l]
        r_out, r_pw = _xchg_wait(f"grads_wait{l}_1", early[0], early[1], after)
        if l > 0:
            r_in, = _xchg_wait(f"grads_wait{l}_0", late[0], late[1], after)
            update("w_in", W_IN_SHARD, r_in, l)
        update("w_out", D_MODEL, r_out, l)
        update("conv_pw_w", CONV_W, r_pw, l)
        after = [prev["w_out"][0], prev["conv_pw_w"][0]]
    late = ggroups[0][0]
    r_in, = _xchg_wait("grads_wait0_0", late[0], late[1], after)
    update("w_in", W_IN_SHARD, r_in, 0)
    for name_, _, _ in big:
        res[name_] = [o.reshape(weights[name_].shape) for o in prev[name_]]

    big_done = [prev[n][0] for n, _, _ in big]
    r_512, r_64 = _xchg_wait("vector_grads_wait", vgroup, vstarted[0], big_done)
    r_small, r_2048 = _xchg_wait("small_grads_wait", sgroup, sstarted[0], big_done)
    r_rep = [r_512, r_2048, r_64]
    sshapes = [weights[n].shape for n in shard_small_names]
    outs = _adamw("adamw_small_sharded", *shard_wmv, r_small)
    for k, o in enumerate(outs):
        for n, a in zip(shard_small_names, _unpack_rows(o, sshapes)):
            res.setdefault(n, [None] * 4)[k] = a

    outs = [_adamw(f"adamw_small_w{tag}", rep_wmv[0][ci], rep_wmv[1][ci], rep_wmv[2][ci], r_rep[ci])
            for ci, tag in enumerate(("512", "2048", "64"))]
    for k in range(4):
        for n, a in _unpack_small(outs[0][k], outs[1][k], outs[2][k]).items():
            res.setdefault(n, [None] * 4)[k] = a
    loss = outs[1][0][6, 0]

    order = ["meta_tokens", "ln_in_g", "ln_in_b", "w_in", "conv_dw_w", "conv_dw_b", "conv_ln_g", "conv_ln_b",
             "conv_pw_w", "conv_pw_b", "attn_sinks", "lru_conv_w", "lru_conv_b", "lru_wa", "lru_ba", "lru_wx",
             "lru_bx", "lru_lambda", "w_out", "ln_post_g", "ln_post_b"]
    return (loss, grad_x,
            *[res[n][0] for n in order], *[res[n][1] for n in order],
            *[res[n][2] for n in order], *[res[n][3] for n in order])
```

```python
import functools
import math

import numpy as np
import jax
import jax.numpy as jnp
from jax import lax
from jax.experimental import pallas as pl
from jax.experimental.pallas import tpu as pltpu

F32 = jnp.float32
BF16 = jnp.bfloat16

D_MODEL = 2048
DEPTH = 2
N_META = 16
TB = 128
PAD0 = TB - N_META
CONV_W = 512
CONV_K = 31
HEAD_DIM = 64
N_HEADS = 16
N_KV = 4
GROUP = 4
ATT_W = 1024
KV_W = 256
ROT_DIM = 16
ROPE_THETA = 500000.0
LRU_W = 512
LRU_HEADS = 8
LRU_CONV_K = 4
LRU_C = 8.0
IN_TOTAL = 5120
N_DEV = 8
W_IN_SHARD = IN_TOTAL // N_DEV
LN_EPS = 1e-5
ALPHA = (2.0 * DEPTH) ** 0.25
NEG_INF = -1e30
ATT_SCALE = HEAD_DIM ** -0.5

ADAM_LR = 0.001
ADAM_B1 = 0.9
ADAM_B2 = 0.999
ADAM_EPS = 1e-08
ADAM_WD = 0.01
ADAM_STEP = 10

VMEM_LIMIT = 56 * 1024 * 1024
PACK_QUANTUM = 256 * 128

COL_CV, COL_CG, COL_CGATE = 0, 1, 2
COL_Q0 = 3
COL_K256 = 10
COL_V256 = 11
COL_AGATE1024 = 3
COL_RX, COL_RGATE = 8, 9
YC_CONV, YC_LRU = 2, 3


def _cp(n_axes, vmem=VMEM_LIMIT):
    return pltpu.CompilerParams(dimension_semantics=("arbitrary",) * n_axes, vmem_limit_bytes=vmem)


def _row_tile(lp, max_blocks):
    nb = lp // TB
    d = max(k for k in range(1, max_blocks + 1) if nb % k == 0)
    return TB * d


def _sig(x):
    return jax.nn.sigmoid(x)


def _dsilu(x, s):
    return s * (1.0 + x * (1.0 - s))


def _ln_core(z):
    mu = jnp.mean(z, axis=-1, keepdims=True)
    zc = z - mu
    var = jnp.mean(zc * zc, axis=-1, keepdims=True)
    rstd = lax.rsqrt(var + LN_EPS)
    return zc * rstd, rstd


def _ln_bwd_core(dy, xh, rstd, g):
    dxh = dy * g
    m1 = jnp.mean(dxh, axis=-1, keepdims=True)
    m2 = jnp.mean(dxh * xh, axis=-1, keepdims=True)
    return rstd * (dxh - m1 - xh * m2)


def _row_ids(shape, base):
    return lax.broadcasted_iota(jnp.int32, shape, 0) + base


def _colsum(x):
    return jnp.sum(x, axis=0, keepdims=True)


def _dot(a, b, dims):
    return lax.dot_general(a, b, (dims, ((), ())), preferred_element_type=F32)


NN = ((1,), (0,))
NT = ((1,), (1,))
TN = ((0,), (0,))


def _embed(x, meta_full, g, b):
    s = x.shape[1]
    lp = s + TB
    nb = lp // TB

    def body(x_ref, m_ref, g_ref, b_ref, o_ref, hb_ref):
        i = pl.program_id(0)

        @pl.when(i == 0)
        def _():
            o_ref[0:PAD0, :] = jnp.zeros((PAD0, D_MODEL), F32)
            o_ref[PAD0:TB, :] = m_ref[...]

        @pl.when(i > 0)
        def _():
            o_ref[...] = x_ref[...]

        xh, _ = _ln_core(o_ref[...])
        h = xh * g_ref[...] + b_ref[...]
        rows = _row_ids(h.shape, i * TB)
        hb_ref[...] = jnp.where(rows >= PAD0, h, 0.0).astype(BF16)

    return pl.pallas_call(
        body, name="embed", grid=(nb,),
        in_specs=[pl.BlockSpec((None, TB, D_MODEL), lambda i: (0, jnp.maximum(i - 1, 0), 0)),
                  pl.BlockSpec((N_META, D_MODEL), lambda i: (0, 0)),
                  pl.BlockSpec((1, D_MODEL), lambda i: (0, 0)),
                  pl.BlockSpec((1, D_MODEL), lambda i: (0, 0))],
        out_specs=[pl.BlockSpec((TB, D_MODEL), lambda i: (i, 0)),
                   pl.BlockSpec((TB, D_MODEL), lambda i: (i, 0))],
        out_shape=[jax.ShapeDtypeStruct((lp, D_MODEL), F32),
                   jax.ShapeDtypeStruct((lp, D_MODEL), BF16)],
        compiler_params=_cp(1),
    )(x, meta_full, g, b)


def _loss_head(z, target, g, b):
    lp = z.shape[0]
    nb = lp // TB

    def body(z_ref, t_ref, g_ref, b_ref, dz_ref, st_ref, loss_ref):
        i = pl.program_id(0)

        @pl.when(i == 0)
        def _():
            st_ref[...] = jnp.zeros(st_ref.shape, F32)
            loss_ref[...] = jnp.zeros(loss_ref.shape, F32)
            dz_ref[...] = jnp.zeros(dz_ref.shape, F32)

        @pl.when(i > 0)
        def _():
            xh, rstd = _ln_core(z_ref[...])
            gg = g_ref[...]
            y = xh * gg + b_ref[...]
            e = y - t_ref[...]
            part = 0.5 * jnp.sum(jnp.mean(e * e, axis=-1, keepdims=True), axis=0, keepdims=True)
            loss_ref[...] += jnp.broadcast_to(part, loss_ref.shape)
            dy = e / float(D_MODEL)
            st_ref[0:1, :] += _colsum(dy * xh)
            st_ref[1:2, :] += _colsum(dy)
            dz_ref[...] = _ln_bwd_core(dy, xh, rstd, gg)

    return pl.pallas_call(
        body, name="loss_head", grid=(nb,),
        in_specs=[pl.BlockSpec((TB, D_MODEL), lambda i: (i, 0)),
                  pl.BlockSpec((None, TB, D_MODEL), lambda i: (0, jnp.maximum(i - 1, 0), 0)),
                  pl.BlockSpec((1, D_MODEL), lambda i: (0, 0)),
                  pl.BlockSpec((1, D_MODEL), lambda i: (0, 0))],
        out_specs=[pl.BlockSpec((TB, D_MODEL), lambda i: (i, 0)),
                   pl.BlockSpec((8, D_MODEL), lambda i: (0, 0)),
                   pl.BlockSpec((8, 128), lambda i: (0, 0))],
        out_shape=[jax.ShapeDtypeStruct((lp, D_MODEL), F32),
                   jax.ShapeDtypeStruct((8, D_MODEL), F32),
                   jax.ShapeDtypeStruct((8, 128), F32)],
        compiler_params=_cp(1),
    )(z, target, g, b)


def _ln_bwd(name, dh, z, g):
    lp = z.shape[0]
    tr = _row_tile(lp, 3)

    def body(dh_ref, z_ref, g_ref, dz_ref, st_ref):
        i = pl.program_id(0)

        @pl.when(i == 0)
        def _():
            st_ref[...] = jnp.zeros(st_ref.shape, F32)

        xh, rstd = _ln_core(z_ref[...])
        rows = _row_ids(xh.shape, i * tr)
        dy = jnp.where(rows >= PAD0, dh_ref[...], 0.0)
        st_ref[0:1, :] += _colsum(dy * xh)
        st_ref[1:2, :] += _colsum(dy)
        dz_ref[...] = _ln_bwd_core(dy, xh, rstd, g_ref[...])

    return pl.pallas_call(
        body, name=name, grid=(lp // tr,),
        in_specs=[pl.BlockSpec((tr, D_MODEL), lambda i: (i, 0)),
                  pl.BlockSpec((tr, D_MODEL), lambda i: (i, 0)),
                  pl.BlockSpec((1, D_MODEL), lambda i: (0, 0))],
        out_specs=[pl.BlockSpec((tr, D_MODEL), lambda i: (i, 0)),
                   pl.BlockSpec((8, D_MODEL), lambda i: (0, 0))],
        out_shape=[jax.ShapeDtypeStruct((lp, D_MODEL), F32),
                   jax.ShapeDtypeStruct((8, D_MODEL), F32)],
        compiler_params=_cp(1),
    )(dh, z, g)


def _ln_bwd_input(dh, z, g):
    lp = z.shape[0]
    nb = lp // TB
    s = lp - TB

    def body(dh_ref, z_ref, g_ref, gx_ref, gm_ref, st_ref):
        i = pl.program_id(0)

        @pl.when(i == 0)
        def _():
            st_ref[...] = jnp.zeros(st_ref.shape, F32)

        xh, rstd = _ln_core(z_ref[...])
        rows = _row_ids(xh.shape, i * TB)
        dy = jnp.where(rows >= PAD0, dh_ref[...], 0.0)
        st_ref[0:1, :] += _colsum(dy * xh)
        st_ref[1:2, :] += _colsum(dy)
        dz = _ln_bwd_core(dy, xh, rstd, g_ref[...])
        gx_ref[...] = dz

        @pl.when(i == 0)
        def _():
            gm_ref[...] = dz[PAD0:TB, :]

    return pl.pallas_call(
        body, name="ln_in_bwd", grid=(nb,),
        in_specs=[pl.BlockSpec((TB, D_MODEL), lambda i: (i, 0)),
                  pl.BlockSpec((TB, D_MODEL), lambda i: (i, 0)),
                  pl.BlockSpec((1, D_MODEL), lambda i: (0, 0))],
        out_specs=[pl.BlockSpec((None, TB, D_MODEL), lambda i: (0, jnp.maximum(i - 1, 0), 0)),
                   pl.BlockSpec((N_META, D_MODEL), lambda i: (0, 0)),
                   pl.BlockSpec((8, D_MODEL), lambda i: (0, 0))],
        out_shape=[jax.ShapeDtypeStruct((1, s, D_MODEL), F32),
                   jax.ShapeDtypeStruct((N_META, D_MODEL), F32),
                   jax.ShapeDtypeStruct((8, D_MODEL), F32)],
        compiler_params=_cp(1),
    )(dh, z, g)


def _mm_proj(name, hb, wg_in):
    lp = hb.shape[0]
    tm = lp // 3

    def body(a_ref, b_ref, o_ref):
        b = jnp.concatenate([b_ref[0], b_ref[1]], axis=1)
        o_ref[...] = _dot(a_ref[...], b, NN)

    return pl.pallas_call(
        body, name=name, grid=(3, N_DEV // 2),
        in_specs=[pl.BlockSpec((tm, D_MODEL), lambda i, j: (i, 0)),
                  pl.BlockSpec((2, D_MODEL, W_IN_SHARD), lambda i, j: (j, 0, 0))],
        out_specs=pl.BlockSpec((tm, 2 * W_IN_SHARD), lambda i, j: (i, j)),
        out_shape=jax.ShapeDtypeStruct((lp, IN_TOTAL), F32),
        compiler_params=_cp(2),
    )(hb, wg_in)


def _mm_out(name, ycat, wout, z, g, b, g2, b2, dep):
    lp = ycat.shape[0]
    tm = lp // 6

    def body(a_ref, w_ref, z_ref, g_ref, b_ref, g2_ref, b2_ref, dep_ref, o_ref, hb_ref):
        del dep_ref
        i = pl.program_id(0)
        xh, _ = _ln_core(z_ref[...])
        h = xh * g_ref[...] + b_ref[...]
        live = _row_ids(h.shape, i * tm) >= PAD0
        h = jnp.where(live, h, 0.0)
        zn = ALPHA * h + _dot(a_ref[...], w_ref[...], NN)
        o_ref[...] = zn
        xh2, _ = _ln_core(zn)
        hb_ref[...] = jnp.where(live, xh2 * g2_ref[...] + b2_ref[...], 0.0).astype(BF16)

    vec = pl.BlockSpec((1, D_MODEL), lambda i: (0, 0))
    return pl.pallas_call(
        body, name=name, grid=(6,),
        in_specs=[pl.BlockSpec((tm, D_MODEL), lambda i: (i, 0)),
                  pl.BlockSpec((D_MODEL, D_MODEL), lambda i: (0, 0), pipeline_mode=pl.Buffered(1)),
                  pl.BlockSpec((tm, D_MODEL), lambda i: (i, 0)),
                  vec, vec, vec, vec, pl.BlockSpec(memory_space=pl.ANY)],
        out_specs=[pl.BlockSpec((tm, D_MODEL), lambda i: (i, 0)),
                   pl.BlockSpec((tm, D_MODEL), lambda i: (i, 0))],
        out_shape=[jax.ShapeDtypeStruct((lp, D_MODEL), F32),
                   jax.ShapeDtypeStruct((lp, D_MODEL), BF16)],
        compiler_params=_cp(1),
    )(ycat, wout, z, g, b, g2, b2, dep)


def _mm_dycat(name, dz, wout, dep):
    lp = dz.shape[0]
    tm = lp // 6

    def body(a_ref, w_ref, dep_ref, o_ref):
        del dep_ref
        o_ref[...] = _dot(a_ref[...].astype(BF16), w_ref[...], NT).astype(BF16)

    return pl.pallas_call(
        body, name=name, grid=(6,),
        in_specs=[pl.BlockSpec((tm, D_MODEL), lambda i: (i, 0)),
                  pl.BlockSpec((D_MODEL, D_MODEL), lambda i: (0, 0), pipeline_mode=pl.Buffered(1)),
                  pl.BlockSpec(memory_space=pl.ANY)],
        out_specs=pl.BlockSpec((tm, D_MODEL), lambda i: (i, 0)),
        out_shape=jax.ShapeDtypeStruct((lp, D_MODEL), BF16),
        compiler_params=_cp(1),
    )(dz, wout, dep)


def _mm_dwout(name, ycat, dz):
    lp = ycat.shape[0]
    tk = _row_tile(lp, 11)
    nk = lp // tk
    half = D_MODEL // 2

    def body(a_ref, b_ref, o_ref, acc_ref):
        k = pl.program_id(1)

        @pl.when(k == 0)
        def _():
            acc_ref[...] = jnp.zeros(acc_ref.shape, F32)

        acc_ref[...] += _dot(a_ref[...], b_ref[...].astype(BF16), TN)

        @pl.when(k == nk - 1)
        def _():
            o_ref[...] = acc_ref[...].astype(BF16)

    return pl.pallas_call(
        body, name=name, grid=(2, nk),
        in_specs=[pl.BlockSpec((tk, half), lambda h, k: (k, h)),
                  pl.BlockSpec((tk, D_MODEL), lambda h, k: (k, 0))],
        out_specs=pl.BlockSpec((half, D_MODEL), lambda h, k: (h, 0)),
        out_shape=jax.ShapeDtypeStruct((D_MODEL, D_MODEL), BF16),
        scratch_shapes=[pltpu.VMEM((half, D_MODEL), F32)],
        compiler_params=_cp(2),
    )(ycat, dz)


def _mm_dwin(name, hb, dproj, dep):
    lp = hb.shape[0]
    tk = _row_tile(lp, 11)
    nk = lp // tk

    def body(a_ref, b_ref, dep_ref, o_ref, acc_ref):
        del dep_ref
        k = pl.program_id(1)

        @pl.when(k == 0)
        def _():
            acc_ref[...] = jnp.zeros(acc_ref.shape, F32)

        acc_ref[...] += _dot(a_ref[...], b_ref[...], TN)

        @pl.when(k == nk - 1)
        def _():
            o_ref[0] = acc_ref[:, 0:W_IN_SHARD].astype(BF16)
            o_ref[1] = acc_ref[:, W_IN_SHARD:2 * W_IN_SHARD].astype(BF16)

    return pl.pallas_call(
        body, name=name, grid=(4, nk),
        in_specs=[pl.BlockSpec((tk, D_MODEL), lambda j, k: (k, 0)),
                  pl.BlockSpec((tk, 2 * W_IN_SHARD), lambda j, k: (k, j)),
                  pl.BlockSpec(memory_space=pl.ANY)],
        out_specs=pl.BlockSpec((2, D_MODEL, W_IN_SHARD), lambda j, k: (j, 0, 0)),
        out_shape=jax.ShapeDtypeStruct((N_DEV, D_MODEL, W_IN_SHARD), BF16),
        scratch_shapes=[pltpu.VMEM((D_MODEL, 2 * W_IN_SHARD), F32)],
        compiler_params=_cp(2),
    )(hb, dproj, dep)


def _mm_dh(name, dproj, wg_in, dz, dep):
    lp = dproj.shape[0]
    tm = lp // 6

    def body(a_ref, w_ref, dz_ref, dep_ref, o_ref, acc_ref):
        del dep_ref
        k = pl.program_id(1)

        @pl.when(k == 0)
        def _():
            acc_ref[...] = jnp.zeros(acc_ref.shape, F32)

        w = jnp.concatenate([w_ref[0], w_ref[1]], axis=1)
        acc_ref[...] += _dot(a_ref[...], w, NT)

        @pl.when(k == N_DEV // 2 - 1)
        def _():
            o_ref[...] = acc_ref[...] + ALPHA * dz_ref[...]

    return pl.pallas_call(
        body, name=name, grid=(6, N_DEV // 2),
        in_specs=[pl.BlockSpec((tm, 2 * W_IN_SHARD), lambda i, k: (i, k)),
                  pl.BlockSpec((2, D_MODEL, W_IN_SHARD), lambda i, k: (k, 0, 0)),
                  pl.BlockSpec((tm, D_MODEL), lambda i, k: (i, 0)),
                  pl.BlockSpec(memory_space=pl.ANY)],
        out_specs=pl.BlockSpec((tm, D_MODEL), lambda i, k: (i, 0)),
        out_shape=jax.ShapeDtypeStruct((lp, D_MODEL), F32),
        scratch_shapes=[pltpu.VMEM((tm, D_MODEL), F32)],
        compiler_params=_cp(2),
    )(dproj, wg_in, dz, dep)


SUB = 128


def _shift_plan(cat, n_shift, base):
    rolled = [cat] + [pltpu.roll(cat, b, axis=0) for b in range(1, 8)]
    return [(rolled[s % 8], base - 8 * (s // 8)) for s in range(n_shift)]


def _tap_sum(w_ref, plan, rows, init=None):
    blocks = []
    for r0 in range(0, rows, SUB):
        row = []
        for c0 in range(0, CONV_W, SUB):
            acc = (jnp.zeros((SUB, SUB), F32) if init is None
                   else jnp.broadcast_to(init[:, c0:c0 + SUB], (SUB, SUB)))
            for k, (arr, off) in enumerate(plan):
                acc = acc + w_ref[k:k + 1, c0:c0 + SUB] * arr[off + r0:off + r0 + SUB, c0:c0 + SUB]
            row.append(acc)
        blocks.append(jnp.concatenate(row, axis=1))
    return jnp.concatenate(blocks, axis=0)


def _tap_grads(dw_ref, dy, plan, rows):
    for c0 in range(0, CONV_W, SUB):
        dys = [dy[r0:r0 + SUB, c0:c0 + SUB] for r0 in range(0, rows, SUB)]
        for k, (arr, off) in enumerate(plan):
            part = None
            for ri, r0 in enumerate(range(0, rows, SUB)):
                prod = dys[ri] * arr[off + r0:off + r0 + SUB, c0:c0 + SUB]
                for i in range(SUB // 8):
                    piece = prod[8 * i:8 * i + 8, :]
                    part = piece if part is None else part + piece
            dw_ref[k:k + 1, c0:c0 + SUB] += jnp.sum(part, axis=0, keepdims=True)


CONV_HALO = 32


def _conv_chain(j, tb, cv_ref, cg_ref, cvp_ref, cgp_ref, wdw_ref, vec_ref, wpw_ref, c1_ref=None):
    cv = cv_ref[...]
    sg = _sig(cg_ref[...])
    c0 = cv * sg
    c0p = jnp.where(j > 0, cvp_ref[...] * _sig(cgp_ref[...]), 0.0)
    cat = jnp.concatenate([c0p, c0], axis=0)
    shifts = _shift_plan(cat, CONV_K, CONV_HALO)
    taps = [shifts[CONV_K - 1 - k] for k in range(CONV_K)]
    if c1_ref is None:
        c1 = _tap_sum(wdw_ref, taps, tb, init=vec_ref[0:1, :])
    else:
        c1 = c1_ref[...]
    xh, rstd = _ln_core(c1)
    c2 = xh * vec_ref[1:2, :] + vec_ref[2:3, :]
    s2 = _sig(c2)
    c3 = c2 * s2
    c4 = _dot(c3.astype(BF16), wpw_ref[...], NN) + vec_ref[3:4, :]
    return dict(cv=cv, sg=sg, taps=taps, c1=c1, xh=xh, rstd=rstd, c2=c2, s2=s2, c3=c3, c4=c4)


def _conv_in_specs(jmap, tb):
    def cur(col):
        return pl.BlockSpec((tb, 512), lambda n: (jmap(n), col))

    def prev(col):
        return pl.BlockSpec((CONV_HALO, 512),
                            lambda n: (jnp.maximum(jmap(n) * (tb // CONV_HALO) - 1, 0), col))

    return [cur(COL_CV), cur(COL_CG), prev(COL_CV), prev(COL_CG), cur(COL_CGATE)]


def _conv_param_specs():
    return [pl.BlockSpec((32, CONV_W), lambda n: (0, 0)),
            pl.BlockSpec((8, CONV_W), lambda n: (0, 0)),
            pl.BlockSpec((CONV_W, CONV_W), lambda n: (0, 0))]


def _conv_fwd(name, proj, wdw, vec, wpw):
    lp = proj.shape[0]
    tb = _row_tile(lp, 3)
    nb = lp // tb

    def body(cv_ref, cg_ref, cvp_ref, cgp_ref, gate_ref, wdw_ref, vec_ref, wpw_ref, o_ref, c1_ref):
        j = pl.program_id(0)
        c = _conv_chain(j, tb, cv_ref, cg_ref, cvp_ref, cgp_ref, wdw_ref, vec_ref, wpw_ref)
        gate = gate_ref[...]
        o_ref[...] = (c["c4"] * (gate * _sig(gate))).astype(BF16)
        c1_ref[...] = c["c1"]

    return pl.pallas_call(
        body, name=name, grid=(nb,),
        in_specs=_conv_in_specs(lambda n: n, tb) + _conv_param_specs(),
        out_specs=[pl.BlockSpec((tb, 512), lambda n: (n, YC_CONV)),
                   pl.BlockSpec((tb, CONV_W), lambda n: (n, 0))],
        out_shape=[jax.ShapeDtypeStruct((lp, D_MODEL), BF16),
                   jax.ShapeDtypeStruct((lp, CONV_W), F32)],
        compiler_params=_cp(1),
    )(proj, proj, proj, proj, proj, wdw, vec, wpw)


def _conv_bwd(name, proj, dycat, c1, wdw, vec, wpw):
    lp = proj.shape[0]
    tb = _row_tile(lp, 3)
    nb = lp // tb
    halo = CONV_HALO

    def body(cv_ref, cg_ref, cvp_ref, cgp_ref, gate_ref, dy_ref, c1_ref, wdw_ref, vec_ref, wpw_ref,
             dp_ref, dwdw_ref, dvec_ref, dwpw_ref, carry_ref):
        n = pl.program_id(0)
        j = nb - 1 - n

        @pl.when(n == 0)
        def _():
            carry_ref[...] = jnp.zeros(carry_ref.shape, F32)
            dwdw_ref[...] = jnp.zeros(dwdw_ref.shape, F32)
            dvec_ref[...] = jnp.zeros(dvec_ref.shape, F32)
            dwpw_ref[...] = jnp.zeros(dwpw_ref.shape, F32)

        c = _conv_chain(j, tb, cv_ref, cg_ref, cvp_ref, cgp_ref, wdw_ref, vec_ref, wpw_ref, c1_ref)
        dy = dy_ref[...].astype(F32)
        gate = gate_ref[...]
        sgate = _sig(gate)
        dc4 = dy * (gate * sgate)
        dgate = dy * c["c4"] * _dsilu(gate, sgate)
        dc4b = dc4.astype(BF16)
        dvec_ref[3:4, :] += _colsum(dc4)
        dwpw_ref[...] += _dot(c["c3"].astype(BF16), dc4b, TN)
        dc3 = _dot(dc4b, wpw_ref[...], NT)
        dc2 = dc3 * _dsilu(c["c2"], c["s2"])
        dvec_ref[1:2, :] += _colsum(dc2 * c["xh"])
        dvec_ref[2:3, :] += _colsum(dc2)
        dc1 = _ln_bwd_core(dc2, c["xh"], c["rstd"], vec_ref[1:2, :])
        dvec_ref[0:1, :] += _colsum(dc1)
        _tap_grads(dwdw_ref, dc1, c["taps"], tb)
        dcat = jnp.concatenate([dc1, carry_ref[...]], axis=0)
        total = tb + halo
        up = [dcat] + [pltpu.roll(dcat, total - b, axis=0) for b in range(1, 8)]
        ahead = [(up[(CONV_K - 1 - k) % 8], 8 * ((CONV_K - 1 - k) // 8)) for k in range(CONV_K)]
        dc0 = _tap_sum(wdw_ref, ahead, tb)
        carry_ref[...] = dc1[0:halo, :]
        sg = c["sg"]
        dcv = dc0 * sg
        dcg = dc0 * c["cv"] * sg * (1.0 - sg)
        dp_ref[:, 0:512] = dcv.astype(BF16)
        dp_ref[:, 512:1024] = dcg.astype(BF16)
        dp_ref[:, 1024:1536] = dgate.astype(BF16)

    jmap = lambda n: nb - 1 - n
    return pl.pallas_call(
        body, name=name, grid=(nb,),
        in_specs=(_conv_in_specs(jmap, tb)
                  + [pl.BlockSpec((tb, 512), lambda n: (jmap(n), YC_CONV)),
                     pl.BlockSpec((tb, CONV_W), lambda n: (jmap(n), 0))]
                  + _conv_param_specs()),
        out_specs=[pl.BlockSpec((tb, 1536), lambda n: (jmap(n), 0)),
                   pl.BlockSpec((32, CONV_W), lambda n: (0, 0)),
                   pl.BlockSpec((8, CONV_W), lambda n: (0, 0)),
                   pl.BlockSpec((CONV_W, CONV_W), lambda n: (0, 0))],
        out_shape=[jax.ShapeDtypeStruct((lp, IN_TOTAL), BF16),
                   jax.ShapeDtypeStruct((32, CONV_W), F32),
                   jax.ShapeDtypeStruct((8, CONV_W), F32),
                   jax.ShapeDtypeStruct((CONV_W, CONV_W), F32)],
        scratch_shapes=[pltpu.VMEM((halo, CONV_W), F32)],
        compiler_params=_cp(1),
    )(proj, proj, proj, proj, proj, dycat, c1, wdw, vec, wpw)


def _rope_tables(lp):
    half = ROT_DIM // 2
    inv_freq = ROPE_THETA ** (-jnp.arange(half, dtype=F32) / half)
    pos = (jnp.arange(lp, dtype=jnp.int32) - PAD0).astype(F32)
    ang = pos[:, None] * inv_freq[None, :]
    cos, sin = jnp.cos(ang), jnp.sin(ang)
    ones = jnp.ones((lp, HEAD_DIM - ROT_DIM), F32)
    zeros = jnp.zeros((lp, HEAD_DIM - ROT_DIM), F32)
    zh = jnp.zeros((lp, half), F32)
    c = jnp.concatenate([cos, cos, ones], axis=1)
    sa = jnp.concatenate([-sin, zh, zeros], axis=1)
    sb = jnp.concatenate([zh, sin, zeros], axis=1)
    tile = lambda t: jnp.tile(t, (1, KV_W // HEAD_DIM))
    return tile(c), tile(sa), tile(sb)


def _rot(x, c, sa, sb):
    w = x.shape[1]
    return x * c + pltpu.roll(x, w - 8, axis=1) * sa + pltpu.roll(x, 8, axis=1) * sb


def _rot_t(dy, c, sa, sb):
    w = dy.shape[1]
    return dy * c + pltpu.roll(dy * sa, 8, axis=1) + pltpu.roll(dy * sb, w - 8, axis=1)


def _rope_fwd(name, proj, tabs):
    lp = proj.shape[0]
    tr = _row_tile(lp, 11)

    def body(q0_ref, q1_ref, k_ref, c_ref, sa_ref, sb_ref, qr_ref, kr_ref):
        c, sa, sb = c_ref[...], sa_ref[...], sb_ref[...]
        c2 = jnp.concatenate([c, c], axis=1)
        sa2 = jnp.concatenate([sa, sa], axis=1)
        sb2 = jnp.concatenate([sb, sb], axis=1)
        qr_ref[:, 0:512] = (_rot(q0_ref[...], c2, sa2, sb2) * ATT_SCALE).astype(BF16)
        qr_ref[:, 512:1024] = (_rot(q1_ref[...], c2, sa2, sb2) * ATT_SCALE).astype(BF16)
        kr_ref[...] = _rot(k_ref[...], c, sa, sb).astype(BF16)

    tab = pl.BlockSpec((tr, KV_W), lambda i: (i, 0))
    return pl.pallas_call(
        body, name=name, grid=(lp // tr,),
        in_specs=[pl.BlockSpec((tr, 512), lambda i: (i, COL_Q0)),
                  pl.BlockSpec((tr, 512), lambda i: (i, COL_Q0 + 1)),
                  pl.BlockSpec((tr, KV_W), lambda i: (i, COL_K256)),
                  tab, tab, tab],
        out_specs=[pl.BlockSpec((tr, ATT_W), lambda i: (i, 0)),
                   pl.BlockSpec((tr, KV_W), lambda i: (i, 0))],
        out_shape=[jax.ShapeDtypeStruct((lp, ATT_W), BF16),
                   jax.ShapeDtypeStruct((lp, KV_W), BF16)],
        compiler_params=_cp(1),
    )(proj, proj, proj, *tabs)


def _attn_mask(j):
    qi = lax.broadcasted_iota(jnp.int32, (GROUP * TB, 3 * TB), 0) & (TB - 1)
    cc = lax.broadcasted_iota(jnp.int32, (GROUP * TB, 3 * TB), 1)
    jj = cc & (TB - 1)
    is_meta = jj >= PAD0
    p0 = (cc < TB) & is_meta & (j >= 1)
    p1 = (cc >= TB) & (cc < 2 * TB) & (jj > qi) & (j >= 2)
    p2 = (cc >= 2 * TB) & (jj <= qi) & ((j >= 1) | is_meta)
    return p0 | p1 | p2


def _lane_group(rows):
    return lax.broadcasted_iota(jnp.int32, (rows, KV_W), 1) // HEAD_DIM


def _stack_heads(x, kv, lgq):
    parts = []
    for g in range(GROUP):
        sh = ((kv - g) % GROUP) * HEAD_DIM
        moved = x if sh == 0 else pltpu.roll(x, sh, axis=1)
        parts.append(jnp.where(lgq == kv, moved, 0.0))
    return jnp.concatenate(parts, axis=0).astype(BF16)


def _unstack_heads(r, kv):
    out = None
    for g in range(GROUP):
        blk = r[g * TB:(g + 1) * TB, :]
        sh = ((g - kv) % GROUP) * HEAD_DIM
        blk = blk if sh == 0 else pltpu.roll(blk, sh, axis=1)
        out = blk if out is None else out + blk
    return out


def _sink_column(sinks, kv):
    lane = lax.broadcasted_iota(jnp.int32, (1, 128), 1)
    cols = []
    for g in range(GROUP):
        sg = jnp.sum(jnp.where(lane == kv * GROUP + g, sinks, 0.0), axis=1, keepdims=True)
        cols.append(jnp.broadcast_to(sg, (TB, 1)))
    return jnp.concatenate(cols, axis=0)


def _attn_kv(kall, vall, lg, kv):
    km = jnp.where(lg == kv, kall, 0.0).astype(BF16)
    vm = jnp.where(lg == kv, vall, 0.0).astype(BF16)
    ones = jnp.where(lg == kv, 1.0, 0.0).astype(BF16)
    return km, vm, ones


def _attn_specs(jmap):
    blk = lambda col: pl.BlockSpec((TB, KV_W), lambda n: (jmap(n), col))
    prv = lambda col: pl.BlockSpec((TB, KV_W), lambda n: (jnp.maximum(jmap(n) - 1, 0), col))
    met = lambda col: pl.BlockSpec((TB, KV_W), lambda n: (0, col))
    return dict(
        qr=pl.BlockSpec((TB, ATT_W), lambda n: (jmap(n), 0)),
        k=[met(0), prv(0), blk(0)],
        v=[met(COL_V256), prv(COL_V256), blk(COL_V256)],
        gate=pl.BlockSpec((TB, ATT_W), lambda n: (jmap(n), COL_AGATE1024)),
        sinks=pl.BlockSpec((8, 128), lambda n: (0, 0)),
    )


ATTN_BWD_HEAD_SETS = ((0, 1, 2, 3),)


def _attn_fwd(name, qr, kr, proj, sinks_row, ycat):
    lp = proj.shape[0]
    nb = lp // TB
    sp = _attn_specs(lambda n: n)

    def body(qr_ref, km_ref, kp_ref, kc_ref, vm_ref, vp_ref, vc_ref, gate_ref, sink_ref, yin_ref, o_ref):
        del yin_ref
        j = pl.program_id(0)
        valid = _attn_mask(j)
        kall = jnp.concatenate([km_ref[...], kp_ref[...], kc_ref[...]], axis=0).astype(F32)
        vall = jnp.concatenate([vm_ref[...], vp_ref[...], vc_ref[...]], axis=0)
        lg = _lane_group(3 * TB)
        lgq = _lane_group(TB)
        lg4 = _lane_group(GROUP * TB)
        sinks = sink_ref[0:1, :]
        heads = range(N_KV)
        cols = [slice(kv * KV_W, (kv + 1) * KV_W) for kv in heads]
        kvo = [_attn_kv(kall, vall, lg, kv) for kv in heads]
        qst = [_stack_heads(qr_ref[:, cols[kv]].astype(F32), kv, lgq) for kv in heads]
        s = [jnp.where(valid, _dot(qst[kv], kvo[kv][0], NT), NEG_INF) for kv in heads]
        eb, es = [], []
        for kv in heads:
            sinkcol = _sink_column(sinks, kv)
            m = jnp.maximum(jnp.max(s[kv], axis=-1, keepdims=True), sinkcol)
            eb.append(jnp.exp(s[kv] - m).astype(BF16))
            es.append(jnp.exp(sinkcol - m))
        r = [_dot(eb[kv], kvo[kv][1], NN) for kv in heads]
        inv = [1.0 / (_dot(eb[kv], kvo[kv][2], NN) + es[kv]) for kv in heads]
        for kv in heads:
            out = jnp.where(lg4 == kv, r[kv] * inv[kv], 0.0)
            gate = gate_ref[:, cols[kv]]
            o_ref[:, cols[kv]] = (_unstack_heads(out, kv) * (gate * _sig(gate))).astype(BF16)

    return pl.pallas_call(
        body, name=name, grid=(nb,),
        in_specs=[sp["qr"]] + sp["k"] + sp["v"] + [sp["gate"], sp["sinks"],
                                                   pl.BlockSpec(memory_space=pl.ANY)],
        out_specs=pl.BlockSpec((TB, ATT_W), lambda n: (n, 0)),
        out_shape=jax.ShapeDtypeStruct((lp, D_MODEL), BF16),
        input_output_aliases={9: 0},
        compiler_params=_cp(1),
    )(qr, kr, kr, kr, proj, proj, proj, proj, sinks_row, ycat)


def _attn_bwd(name, qr, kr, proj, sinks_row, dycat, dep):
    lp = proj.shape[0]
    nb = lp // TB
    sp = _attn_specs(lambda n: n)

    def body(qr_ref, km_ref, kp_ref, kc_ref, vm_ref, vp_ref, vc_ref, gate_ref, sink_ref, dy_ref, dep_ref,
             dq_ref, dgate_ref, dk_ref, dv_ref, dsink_ref):
        del dep_ref
        j = pl.program_id(0)

        @pl.when(j == 0)
        def _():
            dk_ref[...] = jnp.zeros(dk_ref.shape, F32)
            dv_ref[...] = jnp.zeros(dv_ref.shape, F32)
            dsink_ref[...] = jnp.zeros(dsink_ref.shape, F32)

        valid = _attn_mask(j)
        kall = jnp.concatenate([km_ref[...], kp_ref[...], kc_ref[...]], axis=0).astype(F32)
        vall = jnp.concatenate([vm_ref[...], vp_ref[...], vc_ref[...]], axis=0)
        lg = _lane_group(3 * TB)
        lgq = _lane_group(TB)
        lg4 = _lane_group(GROUP * TB)
        sinks = sink_ref[0:1, :]
        lane = lax.broadcasted_iota(jnp.int32, (1, 128), 1)
        def stages(heads):
            dsink = jnp.zeros((1, 128), F32)
            cols = {kv: slice(kv * KV_W, (kv + 1) * KV_W) for kv in heads}
            kvo = {kv: _attn_kv(kall, vall, lg, kv) for kv in heads}
            qst = {kv: _stack_heads(qr_ref[:, cols[kv]].astype(F32), kv, lgq) for kv in heads}
            s = {kv: jnp.where(valid, _dot(qst[kv], kvo[kv][0], NT), NEG_INF) for kv in heads}
            eb, es = {}, {}
            for kv in heads:
                sinkcol = _sink_column(sinks, kv)
                m = jnp.maximum(jnp.max(s[kv], axis=-1, keepdims=True), sinkcol)
                eb[kv] = jnp.exp(s[kv] - m).astype(BF16)
                es[kv] = jnp.exp(sinkcol - m)
            r = {kv: _dot(eb[kv], kvo[kv][1], NN) for kv in heads}
            inv = {kv: 1.0 / (_dot(eb[kv], kvo[kv][2], NN) + es[kv]) for kv in heads}
            dost, dcol = {}, {}
            for kv in heads:
                att = _unstack_heads(jnp.where(lg4 == kv, r[kv] * inv[kv], 0.0), kv)
                gate = gate_ref[:, cols[kv]]
                sgate = _sig(gate)
                dy = dy_ref[:, cols[kv]].astype(F32)
                dgate_ref[:, cols[kv]] = (dy * att * _dsilu(gate, sgate)).astype(BF16)
                dsc = dy * (gate * sgate) * _unstack_heads(jnp.where(lg4 == kv, inv[kv], 0.0), kv)
                dost[kv] = _stack_heads(dsc, kv, lgq)
                dd = dsc * att
                dcol[kv] = jnp.concatenate(
                    [jnp.sum(jnp.where(lgq == g, dd, 0.0), axis=1, keepdims=True) for g in range(GROUP)], axis=0)
            dp = {kv: _dot(dost[kv], kvo[kv][1], NT) for kv in heads}
            ds = {}
            for kv in heads:
                ds[kv] = (eb[kv].astype(F32) * (dp[kv] - dcol[kv])).astype(BF16)
                pd = es[kv] * dcol[kv]
                for g in range(GROUP):
                    tot = jnp.sum(pd[g * TB:(g + 1) * TB, :], axis=0, keepdims=True)
                    dsink = dsink - jnp.where(lane == kv * GROUP + g, tot, 0.0)
            dqs = {kv: _dot(ds[kv], kvo[kv][0], NN) for kv in heads}
            dks = [_dot(ds[kv], qst[kv], TN) for kv in heads]
            dvs = [_dot(eb[kv], dost[kv], TN) for kv in heads]
            for kv in heads:
                dq_ref[:, cols[kv]] = _unstack_heads(dqs[kv], kv)
            return sum(dks[1:], dks[0]), sum(dvs[1:], dvs[0]), dsink

        parts = [stages(hs) for hs in ATTN_BWD_HEAD_SETS]
        dkall = sum([p[0] for p in parts[1:]], parts[0][0])
        dvall = sum([p[1] for p in parts[1:]], parts[0][1])
        dsink = sum([p[2] for p in parts[1:]], parts[0][2])
        dsink_ref[0:1, :] += dsink
        prev = pl.multiple_of(jnp.maximum(j - 1, 0) * TB, TB)
        cur = pl.multiple_of(j * TB, TB)
        dk_ref[0:TB, :] += dkall[0:TB]
        dv_ref[0:TB, :] += dvall[0:TB]
        dk_ref[pl.ds(prev, TB), :] += dkall[TB:2 * TB]
        dv_ref[pl.ds(prev, TB), :] += dvall[TB:2 * TB]
        dk_ref[pl.ds(cur, TB), :] += dkall[2 * TB:3 * TB]
        dv_ref[pl.ds(cur, TB), :] += dvall[2 * TB:3 * TB]

    return pl.pallas_call(
        body, name=name, grid=(nb,),
        in_specs=[sp["qr"]] + sp["k"] + sp["v"] + [sp["gate"], sp["sinks"],
                                                   pl.BlockSpec((TB, ATT_W), lambda n: (n, 0)),
                                                   pl.BlockSpec(memory_space=pl.ANY)],
        out_specs=[pl.BlockSpec((TB, ATT_W), lambda n: (n, 0)),
                   pl.BlockSpec((TB, ATT_W), lambda n: (n, 0)),
                   pl.BlockSpec((lp, KV_W), lambda n: (0, 0)),
                   pl.BlockSpec((lp, KV_W), lambda n: (0, 0)),
                   pl.BlockSpec((8, 128), lambda n: (0, 0))],
        out_shape=[jax.ShapeDtypeStruct((lp, ATT_W), F32),
                   jax.ShapeDtypeStruct((lp, ATT_W), BF16),
                   jax.ShapeDtypeStruct((lp, KV_W), F32),
                   jax.ShapeDtypeStruct((lp, KV_W), F32),
                   jax.ShapeDtypeStruct((8, 128), F32)],
        compiler_params=_cp(1),
    )(qr, kr, kr, kr, proj, proj, proj, proj, sinks_row, dycat, dep)


def _attn_assemble(name, dq, dgate, dk, dv, tabs, dproj):
    lp = dq.shape[0]
    tr = _row_tile(lp, 11)

    def body(dq_ref, dg_ref, dk_ref, dv_ref, c_ref, sa_ref, sb_ref, din_ref, o_ref):
        del din_ref
        cidx = pl.program_id(1)
        c, sa, sb = c_ref[...], sa_ref[...], sb_ref[...]

        @pl.when(cidx < 2)
        def _():
            c2 = jnp.concatenate([c, c], axis=1)
            sa2 = jnp.concatenate([sa, sa], axis=1)
            sb2 = jnp.concatenate([sb, sb], axis=1)
            o_ref[...] = (_rot_t(dq_ref[...], c2, sa2, sb2) * ATT_SCALE).astype(BF16)

        @pl.when(cidx == 2)
        def _():
            o_ref[:, 0:KV_W] = _rot_t(dk_ref[...], c, sa, sb).astype(BF16)
            o_ref[:, KV_W:2 * KV_W] = dv_ref[...].astype(BF16)

        @pl.when(cidx > 2)
        def _():
            o_ref[...] = dg_ref[...]

    tab = pl.BlockSpec((tr, KV_W), lambda n, c: (n, 0))
    return pl.pallas_call(
        body, name=name, grid=(lp // tr, 5),
        in_specs=[pl.BlockSpec((tr, 512), lambda n, c: (n, jnp.minimum(c, 1))),
                  pl.BlockSpec((tr, 512), lambda n, c: (n, jnp.clip(c - 3, 0, 1))),
                  tab, tab,
                  tab, tab, tab,
                  pl.BlockSpec(memory_space=pl.ANY)],
        out_specs=pl.BlockSpec((tr, 512), lambda n, c: (n, COL_Q0 + c)),
        out_shape=jax.ShapeDtypeStruct((lp, IN_TOTAL), BF16),
        input_output_aliases={7: 0},
        compiler_params=_cp(2),
    )(dq, dgate, dk, dv, *tabs, dproj)


def _softplus_neg(lam):
    t = jnp.exp(-jnp.abs(lam))
    u = 1.0 + t
    den = jnp.where(u == 1.0, 1.0, u - 1.0)
    l1p = jnp.where(u == 1.0, t, jnp.log(u) * (t / den))
    return jnp.maximum(-lam, 0.0) + l1p


def _lru_chain(j, tb, rx_ref, rxp_ref, wl_ref, vec_ref, wa_ref, wx_ref):
    rx = rx_ref[...]
    rxp = jnp.where(j > 0, rxp_ref[...], 0.0)
    cat = jnp.concatenate([rxp, rx], axis=0)
    views = [cat[8:8 + tb, :]] + [pltpu.roll(cat, s, axis=0)[8:8 + tb, :] for s in range(1, LRU_CONV_K)]
    x1 = jnp.broadcast_to(vec_ref[0:1, :], (tb, LRU_W))
    for k in range(LRU_CONV_K):
        x1 = x1 + wl_ref[k:k + 1, :] * views[LRU_CONV_K - 1 - k]
    x1b = x1.astype(BF16)
    r = _sig(_dot(x1b, wa_ref[...], NN) + vec_ref[1:2, :])
    ig = _sig(_dot(x1b, wx_ref[...], NN) + vec_ref[2:3, :])
    sp = _softplus_neg(vec_ref[3:4, :])
    log_a = -LRU_C * r * sp
    rows = _row_ids((tb, LRU_W), j * tb)
    live = rows >= PAD0
    a = jnp.where(live, jnp.exp(log_a), 0.0)
    y2 = 2.0 * log_a
    em = -jnp.tanh(0.5 * y2) * (jnp.exp(y2) + 1.0)
    mult = jnp.sqrt(em)
    return dict(views=views, x1=x1, x1b=x1b, r=r, ig=ig, sp=sp, a=a, mult=mult, live=live, a_raw=jnp.exp(log_a))


def _scan_slabs(a, u, forward):
    tb = a.shape[0]
    rows = lax.broadcasted_iota(jnp.int32, (tb, SUB), 0)
    outs_a, outs_u = [], []
    for c0 in range(0, a.shape[1], SUB):
        ac, uc = a[:, c0:c0 + SUB], u[:, c0:c0 + SUB]
        d = 1
        while d < tb:
            if forward:
                keep, sh = rows >= d, d
            else:
                keep, sh = rows < tb - d, tb - d
            an = jnp.where(keep, pltpu.roll(ac, sh, axis=0), 1.0)
            un = jnp.where(keep, pltpu.roll(uc, sh, axis=0), 0.0)
            uc = ac * un + uc
            ac = ac * an
            d *= 2
        outs_a.append(ac)
        outs_u.append(uc)
    return jnp.concatenate(outs_a, axis=1), jnp.concatenate(outs_u, axis=1)


def _lru_specs(jmap, tb):
    return [pl.BlockSpec((tb, 512), lambda n: (jmap(n), COL_RX)),
            pl.BlockSpec((8, 512), lambda n: (jnp.maximum(jmap(n) * (tb // 8) - 1, 0), COL_RX)),
            pl.BlockSpec((tb, 512), lambda n: (jmap(n), COL_RGATE))]


def _lru_param_specs():
    return [pl.BlockSpec((8, LRU_W), lambda n: (0, 0)),
            pl.BlockSpec((8, LRU_W), lambda n: (0, 0)),
            pl.BlockSpec((LRU_W, LRU_W), lambda n: (0, 0)),
            pl.BlockSpec((LRU_W, LRU_W), lambda n: (0, 0))]


def _lru_fwd(name, proj, wl, vec, wa, wx, ycat):
    lp = proj.shape[0]
    tb = _row_tile(lp, 3)
    nb = lp // tb

    def body(rx_ref, rxp_ref, gate_ref, wl_ref, vec_ref, wa_ref, wx_ref, yin_ref, o_ref, h_ref, carry_ref):
        del yin_ref
        j = pl.program_id(0)

        @pl.when(j == 0)
        def _():
            carry_ref[...] = jnp.zeros(carry_ref.shape, F32)

        c = _lru_chain(j, tb, rx_ref, rxp_ref, wl_ref, vec_ref, wa_ref, wx_ref)
        u = jnp.where(c["live"], c["mult"] * (c["ig"] * c["x1"]), 0.0)
        a, u = _scan_slabs(c["a"], u, forward=True)
        h = u + a * carry_ref[0:1, :]
        carry_ref[...] = h[tb - 8:tb, :]
        carry_ref[0:1, :] = h[tb - 1:tb, :]
        h_ref[...] = h
        gate = gate_ref[...]
        o_ref[...] = (h * (gate * _sig(gate))).astype(BF16)

    return pl.pallas_call(
        body, name=name, grid=(nb,),
        in_specs=_lru_specs(lambda n: n, tb) + _lru_param_specs() + [pl.BlockSpec(memory_space=pl.ANY)],
        out_specs=[pl.BlockSpec((tb, 512), lambda n: (n, YC_LRU)),
                   pl.BlockSpec((tb, LRU_W), lambda n: (n, 0))],
        out_shape=[jax.ShapeDtypeStruct((lp, D_MODEL), BF16),
                   jax.ShapeDtypeStruct((lp, LRU_W), F32)],
        input_output_aliases={7: 0},
        scratch_shapes=[pltpu.VMEM((8, LRU_W), F32)],
        compiler_params=_cp(1),
    )(proj, proj, proj, wl, vec, wa, wx, ycat)


def _lru_bwd(name, proj, dycat, hstate, wl, vec, wa, wx, dproj):
    lp = proj.shape[0]
    tb = _row_tile(lp, 3)
    nb = lp // tb

    def body(rx_ref, rxp_ref, gate_ref, dy_ref, h_ref, hp_ref, wl_ref, vec_ref, wa_ref, wx_ref, din_ref,
             dp_ref, dwl_ref, dvec_ref, dwa_ref, dwx_ref, dhc_ref, anx_ref, dxc_ref):
        del din_ref
        n = pl.program_id(0)
        j = nb - 1 - n

        @pl.when(n == 0)
        def _():
            dhc_ref[...] = jnp.zeros(dhc_ref.shape, F32)
            anx_ref[...] = jnp.zeros(anx_ref.shape, F32)
            dxc_ref[...] = jnp.zeros(dxc_ref.shape, F32)
            dwl_ref[...] = jnp.zeros(dwl_ref.shape, F32)
            dvec_ref[...] = jnp.zeros(dvec_ref.shape, F32)
            dwa_ref[...] = jnp.zeros(dwa_ref.shape, F32)
            dwx_ref[...] = jnp.zeros(dwx_ref.shape, F32)

        c = _lru_chain(j, tb, rx_ref, rxp_ref, wl_ref, vec_ref, wa_ref, wx_ref)
        a, mult, r, ig, x1, live = c["a"], c["mult"], c["r"], c["ig"], c["x1"], c["live"]
        h = h_ref[...]
        gate = gate_ref[...]
        sgate = _sig(gate)
        dy = dy_ref[...].astype(F32)
        gsum = dy * (gate * sgate)
        dgate = dy * h * _dsilu(gate, sgate)
        rows = lax.broadcasted_iota(jnp.int32, (tb, LRU_W), 0)
        bb = jnp.where(rows == tb - 1, anx_ref[0:1, :], pltpu.roll(a, tb - 1, axis=0))
        bb, gg = _scan_slabs(bb, gsum, forward=False)
        dh = gg + bb * dhc_ref[0:1, :]
        dhc_ref[...] = dh[0:8, :]
        anx_ref[...] = a[0:8, :]
        hprev = jnp.where(rows == 0, jnp.where(j > 0, hp_ref[7:8, :], 0.0), pltpu.roll(h, 1, axis=0))
        du = jnp.where(live, dh, 0.0)
        da = jnp.where(live, dh * hprev, 0.0)
        ar = c["a_raw"]
        dmult = du * (ig * x1)
        di = du * mult * x1
        dx1 = du * mult * ig
        dloga = da * ar - dmult * ar * ar / mult
        dr = dloga * (-LRU_C * c["sp"])
        dvec_ref[3:4, :] += _colsum(dloga * (-LRU_C * r))
        dza = dr * r * (1.0 - r)
        dzx = di * ig * (1.0 - ig)
        dzab, dzxb = dza.astype(BF16), dzx.astype(BF16)
        dvec_ref[1:2, :] += _colsum(dza)
        dvec_ref[2:3, :] += _colsum(dzx)
        dwa_ref[...] += _dot(c["x1b"], dzab, TN)
        dwx_ref[...] += _dot(c["x1b"], dzxb, TN)
        dx1 = dx1 + _dot(dzab, wa_ref[...], NT) + _dot(dzxb, wx_ref[...], NT)
        dvec_ref[0:1, :] += _colsum(dx1)
        for k in range(LRU_CONV_K):
            dwl_ref[k:k + 1, :] += _colsum(dx1 * c["views"][LRU_CONV_K - 1 - k])
        dcat = jnp.concatenate([dx1, dxc_ref[...]], axis=0)
        drx = jnp.zeros((tb, LRU_W), F32)
        for k in range(LRU_CONV_K):
            s = LRU_CONV_K - 1 - k
            view = dcat[0:tb, :] if s == 0 else pltpu.roll(dcat, tb + 8 - s, axis=0)[0:tb, :]
            drx = drx + wl_ref[k:k + 1, :] * view
        dxc_ref[...] = dx1[0:8, :]
        dp_ref[:, 0:512] = drx.astype(BF16)
        dp_ref[:, 512:1024] = dgate.astype(BF16)

        @pl.when(n == nb - 1)
        def _():
            lam = vec_ref[3:4, :]
            dvec_ref[3:4, :] = dvec_ref[3:4, :] * (-_sig(-lam))

    jmap = lambda n: nb - 1 - n
    return pl.pallas_call(
        body, name=name, grid=(nb,),
        in_specs=(_lru_specs(jmap, tb)
                  + [pl.BlockSpec((tb, 512), lambda n: (jmap(n), YC_LRU)),
                     pl.BlockSpec((tb, LRU_W), lambda n: (jmap(n), 0)),
                     pl.BlockSpec((8, LRU_W), lambda n: (jnp.maximum(jmap(n) * (tb // 8) - 1, 0), 0))]
                  + _lru_param_specs() + [pl.BlockSpec(memory_space=pl.ANY)]),
        out_specs=[pl.BlockSpec((tb, 1024), lambda n: (jmap(n), 4)),
                   pl.BlockSpec((8, LRU_W), lambda n: (0, 0)),
                   pl.BlockSpec((8, LRU_W), lambda n: (0, 0)),
                   pl.BlockSpec((LRU_W, LRU_W), lambda n: (0, 0)),
                   pl.BlockSpec((LRU_W, LRU_W), lambda n: (0, 0))],
        out_shape=[jax.ShapeDtypeStruct((lp, IN_TOTAL), BF16),
                   jax.ShapeDtypeStruct((8, LRU_W), F32),
                   jax.ShapeDtypeStruct((8, LRU_W), F32),
                   jax.ShapeDtypeStruct((LRU_W, LRU_W), F32),
                   jax.ShapeDtypeStruct((LRU_W, LRU_W), F32)],
        input_output_aliases={10: 0},
        scratch_shapes=[pltpu.VMEM((8, LRU_W), F32), pltpu.VMEM((8, LRU_W), F32), pltpu.VMEM((8, LRU_W), F32)],
        compiler_params=_cp(1),
    )(proj, proj, proj, dycat, hstate, hstate, wl, vec, wa, wx, dproj)


_HBM = pl.BlockSpec(memory_space=pltpu.HBM)
_SEM = pl.BlockSpec(memory_space=pltpu.SEMAPHORE)
_ANY = pl.BlockSpec(memory_space=pl.ANY)
_EFFECT = pltpu.SideEffectType.DATAFLOW_SIDE_EFFECTING


def _hbm(a):
    return pltpu.with_memory_space_constraint(a, pltpu.HBM)


_ALL_PEERS = tuple(range(1, N_DEV))
_CHIP_PEERS = (1, 2, 4, 6)
_OTHER_CHIPS = (2, 4, 6)


def _spec_peers(mode):
    return {"ici": _CHIP_PEERS, "fwd": _OTHER_CHIPS}.get(mode, _ALL_PEERS)


def _split_descriptors(copies, srcs, lands, send_sems, recv_sems):
    x, y, c = lax.axis_index("x"), lax.axis_index("y"), lax.axis_index("c")
    me = 4 * x + 2 * y + c
    out, sem = [], 0
    for si, mode, li, ll in copies:
        for k in _spec_peers(mode):
            px = 1 - x if k & 4 else x
            py = 1 - y if k & 2 else y
            pc = 1 - c if k & 1 else c
            peer = 4 * px + 2 * py + pc
            if mode == "fwd":
                src = dst = lands[li].at[peer]
                target = (x, y, 1 - c)
            else:
                src = srcs[si].at[peer] if mode is True else srcs[si]
                dst = lands[li].at[me] if ll is None else lands[li].at[me, ll]
                target = (px, py, pc)
            out.append(pltpu.make_async_remote_copy(
                src_ref=src, dst_ref=dst, send_sem=send_sems.at[sem], recv_sem=recv_sems.at[sem],
                device_id=target, device_id_type=pl.DeviceIdType.MESH))
            sem += 1
    return out


def _n_copies(copies):
    return sum(len(_spec_peers(mode)) for _, mode, _, _ in copies)


def _xchg_start(name, groups):
    n_src = [len(g[0]) for g in groups]
    n_land = [len(g[1]) for g in groups]
    srcs = [s for g in groups for s in g[0]]
    lands = [l for g in groups for l in g[1]]
    ns, nl, ng = len(srcs), len(lands), len(groups)

    def body(*refs):
        src_refs, land_refs = refs[:ns], refs[ns:ns + nl]
        sems = refs[ns + nl:ns + nl + 2 * ng]
        token = refs[-1]
        so = lo = 0
        for gi, (_, _, copies) in enumerate(groups):
            for d in _split_descriptors(copies, src_refs[so:so + n_src[gi]], land_refs[lo:lo + n_land[gi]],
                                        sems[2 * gi], sems[2 * gi + 1]):
                d.start()
            so += n_src[gi]
            lo += n_land[gi]
        token[...] = jnp.zeros(token.shape, F32)

    out_shape, out_specs = [], []
    for g in groups:
        n = _n_copies(g[2])
        out_shape += [pltpu.SemaphoreType.DMA((n,)), pltpu.SemaphoreType.DMA((n,))]
        out_specs += [_SEM, _SEM]
    out_shape += [pltpu.HBM(l.shape, l.dtype) for l in lands]
    out_specs += [_HBM] * nl
    out_shape.append(jax.ShapeDtypeStruct((8, 128), F32))
    out_specs.append(pl.BlockSpec(memory_space=pltpu.VMEM))
    outs = pl.pallas_call(
        body, name=name, in_specs=[_HBM] * (ns + nl), out_specs=out_specs, out_shape=out_shape,
        input_output_aliases={ns + i: 2 * ng + i for i in range(nl)},
        compiler_params=pltpu.CompilerParams(has_side_effects=_EFFECT),
    )(*[_hbm(a) for a in srcs + lands])
    res, lo = [], 2 * ng
    for gi in range(ng):
        res.append((outs[2 * gi], outs[2 * gi + 1], list(outs[lo:lo + n_land[gi]])))
        lo += n_land[gi]
    return res, outs[-1]


def _xchg_wait(name, group, started, after):
    srcs, _, copies = group
    send_sems, recv_sems, lands = started
    ns, nl = len(srcs), len(lands)
    after = list(after)

    def body(*refs):
        src_refs, land_refs = refs[:ns], refs[ns:ns + nl]
        send_ref, recv_ref = refs[ns + nl], refs[ns + nl + 1]
        for d in _split_descriptors(copies, src_refs, land_refs, send_ref, recv_ref):
            d.wait_send()
            d.wait_recv()

    outs = pl.pallas_call(
        body, name=name, in_specs=[_HBM] * (ns + nl) + [_SEM, _SEM] + [_ANY] * len(after),
        out_specs=[_HBM] * nl, out_shape=[pltpu.HBM(l.shape, l.dtype) for l in lands],
        input_output_aliases={ns + i: i for i in range(nl)},
        compiler_params=pltpu.CompilerParams(has_side_effects=_EFFECT),
    )(*[_hbm(a) for a in srcs], *lands, send_sems, recv_sems, *after)
    return list(outs)


def _landing(own, me):
    land = lax.empty((N_DEV,) + own.shape, own.dtype)
    return lax.dynamic_update_slice(land, own[None], (me,) + (0,) * own.ndim)


def _adamw(name, w, m, v, recv, row0=0, prev=None):
    cdim = w.shape[1]
    r = recv.shape[1]
    tr = r
    for cand in (512, 256, 128, 64, 32, 16, 8):
        if r % cand == 0 and r > cand:
            tr = cand
            break
    assert row0 % tr == 0
    blk0 = row0 // tr
    n_prev = 0 if prev is None else 4

    def body(w_ref, m_ref, v_ref, r_ref, *rest):
        g_ref, d_ref, mo_ref, vo_ref = rest[n_prev:]
        g = r_ref[0].astype(F32)
        for s in range(1, N_DEV):
            g = g + r_ref[s].astype(F32)
        mn = ADAM_B1 * m_ref[...] + (1.0 - ADAM_B1) * g
        vn = ADAM_B2 * v_ref[...] + (1.0 - ADAM_B2) * (g * g)
        m_hat = mn / (1.0 - ADAM_B1 ** ADAM_STEP)
        v_hat = vn / (1.0 - ADAM_B2 ** ADAM_STEP)
        g_ref[...] = g
        d_ref[...] = -ADAM_LR * (m_hat / (jnp.sqrt(v_hat) + ADAM_EPS) + ADAM_WD * w_ref[...])
        mo_ref[...] = mn
        vo_ref[...] = vn

    blk = pl.BlockSpec((tr, cdim), lambda i: (i + blk0, 0))
    return pl.pallas_call(
        body, name=name, grid=(r // tr,),
        in_specs=[blk, blk, blk, pl.BlockSpec((N_DEV, tr, cdim), lambda i: (0, i, 0))] + [_ANY] * n_prev,
        out_specs=[blk, blk, blk, blk],
        out_shape=[jax.ShapeDtypeStruct(w.shape, F32)] * 4,
        input_output_aliases={4 + i: i for i in range(n_prev)},
        compiler_params=_cp(1),
    )(w, m, v, recv, *(prev or []))


def _pack_rows(arrs, lead=()):
    n = len(lead)
    flat = jnp.concatenate([a.reshape(a.shape[:n] + (-1,)) for a in arrs], axis=-1)
    size = flat.shape[-1]
    padded = -(-size // PACK_QUANTUM) * PACK_QUANTUM
    flat = jnp.pad(flat, [(0, 0)] * n + [(0, padded - size)])
    return flat.reshape(flat.shape[:n] + (padded // 128, 128))


def _unpack_rows(packed, shapes, lead=()):
    n = len(lead)
    flat = packed.reshape(packed.shape[:n] + (-1,))
    out, off = [], 0
    for s in shapes:
        size = int(np.prod(s))
        out.append(flat[..., off:off + size].reshape(packed.shape[:n] + tuple(s)))
        off += size
    return out


def _block_diag(w):
    eye = jnp.eye(LRU_HEADS, dtype=w.dtype)
    return (eye[:, None, :, None] * w[:, :, None, :]).reshape(LRU_W, LRU_W)


def _diag_blocks(dense):
    t = dense.reshape(LRU_HEADS, 64, LRU_HEADS, 64)
    eye = jnp.eye(LRU_HEADS, dtype=dense.dtype)
    return jnp.sum(t * eye[:, None, :, None], axis=2).reshape(LRU_HEADS * 64, 64)


_W512_NAMES = ("conv_dw_b", "conv_ln_g", "conv_ln_b", "conv_pw_b", "lru_conv_b", "lru_ba", "lru_bx", "lru_lambda")
_W512_ROWS = 12


def _pack_small(d):
    sinks = jnp.pad(d["attn_sinks"], ((0, 0), (0, 512 - N_HEADS)))
    t = jnp.stack([d[n] for n in _W512_NAMES] + [sinks], axis=1)
    w512 = jnp.pad(t, ((0, 0), (0, _W512_ROWS - t.shape[1]), (0, 0))).reshape(DEPTH * _W512_ROWS, 512)
    w2048 = jnp.concatenate([d["ln_in_g"][None], d["ln_in_b"][None], d["ln_post_g"], d["ln_post_b"],
                             jnp.zeros((2, D_MODEL), F32)], axis=0)
    w64 = jnp.concatenate([d["lru_wa"].reshape(-1, 64), d["lru_wx"].reshape(-1, 64)], axis=0)
    return [w512, w2048, w64.reshape(-1, 128)]


def _unpack_small(w512=None, w2048=None, w64=None):
    out = {}
    if w512 is not None:
        t = w512.reshape(DEPTH, _W512_ROWS, 512)
        out.update({n: t[:, i, :] for i, n in enumerate(_W512_NAMES)})
        out["attn_sinks"] = t[:, len(_W512_NAMES), :N_HEADS]
    if w2048 is not None:
        out["ln_in_g"], out["ln_in_b"] = w2048[0], w2048[1]
        out["ln_post_g"], out["ln_post_b"] = w2048[2:4], w2048[4:6]
    if w64 is not None:
        w64 = w64.reshape(-1, 64)
        half = w64.shape[0] // 2
        out["lru_wa"] = w64[:half].reshape(DEPTH, LRU_HEADS, 64, 64)
        out["lru_wx"] = w64[half:].reshape(DEPTH, LRU_HEADS, 64, 64)
    return out


def _cols_to_slots(full):
    lead = full.shape[:-1]
    t = full.reshape(lead + (N_DEV, full.shape[-1] // N_DEV))
    return jnp.moveaxis(t, -2, 0)


def _slots_to_cols(slots):
    t = jnp.moveaxis(slots, 0, -2)
    return t.reshape(t.shape[:-2] + (t.shape[-2] * t.shape[-1],))


def kernel(x, meta_tokens, ln_in_g, ln_in_b, w_in, conv_dw_w, conv_dw_b, conv_ln_g, conv_ln_b, conv_pw_w, conv_pw_b, attn_sinks, lru_conv_w, lru_conv_b, lru_wa, lru_ba, lru_wx, lru_bx, lru_lambda, w_out, ln_post_g, ln_post_b, loss_target, m_meta_tokens, m_ln_in_g, m_ln_in_b, m_w_in, m_conv_dw_w, m_conv_dw_b, m_conv_ln_g, m_conv_ln_b, m_conv_pw_w, m_conv_pw_b, m_attn_sinks, m_lru_conv_w, m_lru_conv_b, m_lru_wa, m_lru_ba, m_lru_wx, m_lru_bx, m_lru_lambda, m_w_out, m_ln_post_g, m_ln_post_b, v_meta_tokens, v_ln_in_g, v_ln_in_b, v_w_in, v_conv_dw_w, v_conv_dw_b, v_conv_ln_g, v_conv_ln_b, v_conv_pw_w, v_conv_pw_b, v_attn_sinks, v_lru_conv_w, v_lru_conv_b, v_lru_wa, v_lru_ba, v_lru_wx, v_lru_bx, v_lru_lambda, v_w_out, v_ln_post_g, v_ln_post_b):
    seq = x.shape[1]
    lp = seq + TB
    row = lambda a: a.reshape(1, -1)
    rep_names = ["ln_in_g", "ln_in_b", "conv_dw_b", "conv_ln_g", "conv_ln_b", "conv_pw_b", "attn_sinks",
                 "lru_conv_b", "lru_wa", "lru_ba", "lru_wx", "lru_bx", "lru_lambda", "ln_post_g", "ln_post_b"]
    shard_small_names = ["conv_dw_w", "lru_conv_w", "meta_tokens"]
    weights = dict(meta_tokens=meta_tokens, ln_in_g=ln_in_g, ln_in_b=ln_in_b, w_in=w_in, conv_dw_w=conv_dw_w,
                   conv_dw_b=conv_dw_b, conv_ln_g=conv_ln_g, conv_ln_b=conv_ln_b, conv_pw_w=conv_pw_w,
                   conv_pw_b=conv_pw_b, attn_sinks=attn_sinks, lru_conv_w=lru_conv_w, lru_conv_b=lru_conv_b,
                   lru_wa=lru_wa, lru_ba=lru_ba, lru_wx=lru_wx, lru_bx=lru_bx, lru_lambda=lru_lambda,
                   w_out=w_out, ln_post_g=ln_post_g, ln_post_b=ln_post_b)
    mom1 = dict(meta_tokens=m_meta_tokens, ln_in_g=m_ln_in_g, ln_in_b=m_ln_in_b, w_in=m_w_in, conv_dw_w=m_conv_dw_w,
                conv_dw_b=m_conv_dw_b, conv_ln_g=m_conv_ln_g, conv_ln_b=m_conv_ln_b, conv_pw_w=m_conv_pw_w,
                conv_pw_b=m_conv_pw_b, attn_sinks=m_attn_sinks, lru_conv_w=m_lru_conv_w, lru_conv_b=m_lru_conv_b,
                lru_wa=m_lru_wa, lru_ba=m_lru_ba, lru_wx=m_lru_wx, lru_bx=m_lru_bx, lru_lambda=m_lru_lambda,
                w_out=m_w_out, ln_post_g=m_ln_post_g, ln_post_b=m_ln_post_b)
    mom2 = dict(meta_tokens=v_meta_tokens, ln_in_g=v_ln_in_g, ln_in_b=v_ln_in_b, w_in=v_w_in, conv_dw_w=v_conv_dw_w,
                conv_dw_b=v_conv_dw_b, conv_ln_g=v_conv_ln_g, conv_ln_b=v_conv_ln_b, conv_pw_w=v_conv_pw_w,
                conv_pw_b=v_conv_pw_b, attn_sinks=v_attn_sinks, lru_conv_w=v_lru_conv_w, lru_conv_b=v_lru_conv_b,
                lru_wa=v_lru_wa, lru_ba=v_lru_ba, lru_wx=v_lru_wx, lru_bx=v_lru_bx, lru_lambda=v_lru_lambda,
                w_out=v_w_out, ln_post_g=v_ln_post_g, ln_post_b=v_ln_post_b)
    shard_wmv = [_pack_rows([d[n] for n in shard_small_names]) for d in (weights, mom1, mom2)]
    rep_wmv = [_pack_small(d) for d in (weights, mom1, mom2)]
    gate_w = [(_block_diag(lru_wa[l]).astype(BF16), _block_diag(lru_wx[l]).astype(BF16)) for l in range(DEPTH)]
    tabs = _rope_tables(lp)
    prepared = (shard_wmv + [a for wmv in rep_wmv for a in wmv]
                + [w for pair in gate_w for w in pair] + list(tabs))

    small_shard_shapes = [conv_dw_w.shape, lru_conv_w.shape, meta_tokens.shape]
    small_shard = _pack_rows([conv_dw_w, lru_conv_w, meta_tokens])
    me = 4 * lax.axis_index("x") + 2 * lax.axis_index("y") + lax.axis_index("c")
    w_in_b = [w_in[l].astype(BF16) for l in range(DEPTH)]
    w_out_b = [w_out[l].astype(BF16) for l in range(DEPTH)]
    pw_b = conv_pw_w.astype(BF16)
    wgroups = [
        ([small_shard], [_landing(small_shard, me)], [(0, False, 0, None)]),
        ([w_in_b[0]], [_landing(w_in_b[0], me)], [(0, "ici", 0, None)]),
        ([pw_b, w_out_b[0]], [_landing(pw_b, me), _landing(w_out_b[0], me)],
         [(0, False, 0, None), (1, False, 1, None)]),
        ([w_in_b[1], w_out_b[1]], [_landing(w_in_b[1], me), _landing(w_out_b[1], me)],
         [(0, "ici", 0, None), (1, "ici", 1, None)]),
    ]
    wstarted, wtoken = _xchg_start("weights_start", wgroups)

    def pass_on(tag, parts):
        fwd = ([], list(parts), [(None, "fwd", i, None) for i in range(len(parts))])
        fstarted, ftoken = _xchg_start(f"weights_fwd_start_{tag}", [fwd])
        return (fwd, fstarted[0]), ftoken
    wg_small, = _xchg_wait("weights_wait_s", wgroups[0], wstarted[0], [wtoken])
    g_dw, g_lc, g_meta = _unpack_rows(wg_small, small_shard_shapes, lead=(N_DEV,))
    conv_dw_full = _slots_to_cols(g_dw)
    lru_conv_full = _slots_to_cols(g_lc)
    meta_full = _slots_to_cols(g_meta)
    wg_in = [None, None]
    wg_out = [None, None]
    wg_pw = None

    ln_g = [ln_in_g, ln_post_g[0], ln_post_g[1]]
    ln_b = [ln_in_b, ln_post_b[0], ln_post_b[1]]

    def layer_params(l):
        wdw = jnp.pad(conv_dw_full[l], ((0, 1), (0, 0)))
        cvec = jnp.pad(jnp.stack([conv_dw_b[l], conv_ln_g[l], conv_ln_b[l], conv_pw_b[l]]), ((0, 4), (0, 0)))
        wpw = wg_pw[:, l].reshape(CONV_W, CONV_W)
        sinks = jnp.pad(attn_sinks[l].reshape(1, N_HEADS), ((0, 7), (0, 128 - N_HEADS)))
        wl = jnp.pad(lru_conv_full[l], ((0, 4), (0, 0)))
        lvec = jnp.pad(jnp.stack([lru_conv_b[l], lru_ba[l], lru_bx[l], lru_lambda[l]]), ((0, 4), (0, 0)))
        wa, wx = gate_w[l]
        wo = wg_out[l].reshape(D_MODEL, D_MODEL)
        wout = jnp.concatenate([wo[512:1536], wo[0:512], wo[1536:]], axis=0)
        return dict(wdw=wdw, cvec=cvec, wpw=wpw, sinks=sinks, wl=wl, lvec=lvec, wa=wa, wx=wx, wout=wout)

    params = [None] * DEPTH

    z0, hb = _embed(x, meta_full, row(ln_g[0]), row(ln_b[0]))
    z = [z0]
    saved = []
    for l in range(DEPTH):
        if l == 0:
            parts = _xchg_wait("weights_wait_a", wgroups[1], wstarted[1], [hb] + prepared)
            pending, ftoken = pass_on("a", parts)
            wg_in[0], = _xchg_wait("weights_fwd_wait_a", *pending, [ftoken])
        else:
            wg_in[1], wg_out[1] = _xchg_wait("weights_fwd_wait_c", *pending_c, [hb])
        proj = _mm_proj(f"proj{l}", hb, wg_in[l])
        if l == 0:
            wg_pw, wg_out[0] = _xchg_wait("weights_wait_b", wgroups[2], wstarted[2], [proj])
        p = params[l] = layer_params(l)
        ycat, c1 = _conv_fwd(f"conv_fwd{l}", proj, p["wdw"], p["cvec"], p["wpw"])
        qr, kr = _rope_fwd(f"rope{l}", proj, tabs)
        ycat = _attn_fwd(f"attn_fwd{l}", qr, kr, proj, p["sinks"], ycat)
        ycat, hstate = _lru_fwd(f"lru_fwd{l}", proj, p["wl"], p["lvec"], p["wa"], p["wx"], ycat)
        if l == 0:
            pending_c, ftoken = pass_on("c", _xchg_wait("weights_wait_c", wgroups[3], wstarted[3], [ycat]))
        saved.append(dict(hb=hb, proj=proj, ycat=ycat, qr=qr, kr=kr, hstate=hstate, c1=c1))
        z_next, hb = _mm_out(f"out{l}", ycat, p["wout"], z[l], row(ln_g[l]), row(ln_b[l]),
                             row(ln_g[l + 1]), row(ln_b[l + 1]), ftoken)
        z.append(z_next)

    dz, st_post1, loss_blk = _loss_head(z[DEPTH], loss_target, row(ln_g[DEPTH]), row(ln_b[DEPTH]))

    ln_stats = {DEPTH: st_post1}
    g_layers = [None] * DEPTH
    dwin_l, dwout_l = [None] * DEPTH, [None] * DEPTH
    grad_x = gmeta = None
    token = wtoken
    ggroups = [None] * DEPTH
    own = lambda a: lax.dynamic_index_in_dim(a, me, 0, keepdims=False)
    for l in reversed(range(DEPTH)):
        p, s = params[l], saved[l]
        dycat = _mm_dycat(f"dycat{l}", dz, p["wout"], token)
        dwout_l[l] = _mm_dwout(f"dwout{l}", s["ycat"], dz)
        dproj, dwdw, dcvec, dwpw = _conv_bwd(f"conv_bwd{l}", s["proj"], dycat, s["c1"], p["wdw"], p["cvec"], p["wpw"])
        dwo = jnp.concatenate([dwout_l[l][1024:1536], dwout_l[l][0:1024], dwout_l[l][1536:]], axis=0)
        dwo = dwo.reshape(N_DEV, D_MODEL // N_DEV, D_MODEL)
        dpw = dwpw.reshape(N_DEV, CONV_W // N_DEV, CONV_W)
        early = ([dwo, dpw], [_landing(own(dwo), me), _landing(own(dpw), me)],
                 [(0, True, 0, None), (1, True, 1, None)])
        started_early, token = _xchg_start(f"grads_start_out{l}", [early])
        dq, dgate, dk, dv, dsink = _attn_bwd(f"attn_bwd{l}", s["qr"], s["kr"], s["proj"], p["sinks"], dycat, token)
        dproj = _attn_assemble(f"attn_asm{l}", dq, dgate, dk, dv, tabs, dproj)
        dproj, dwl, dlvec, dwa, dwx = _lru_bwd(f"lru_bwd{l}", s["proj"], dycat, s["hstate"],
                                                p["wl"], p["lvec"], p["wa"], p["wx"], dproj)
        g512 = jnp.concatenate([dcvec[0:4], dlvec[0:4], jnp.pad(dsink[0:1], ((0, 0), (0, 512 - 128))),
                                jnp.zeros((_W512_ROWS - 9, 512), F32)], axis=0)
        g_layers[l] = dict(dwdw=dwdw[:CONV_K], dwl=dwl[:LRU_CONV_K], g512=g512,
                           dwa=_diag_blocks(dwa), dwx=_diag_blocks(dwx))
        if l == 0:
            g512 = jnp.concatenate([g_layers[i]["g512"] for i in range(DEPTH)], axis=0)
            g64 = jnp.concatenate([g_layers[i][k] for k in ("dwa", "dwx") for i in range(DEPTH)], axis=0)
            g64 = g64.reshape(-1, 128)
            vgroup = ([g512, g64], [_landing(g512, me), _landing(g64, me)],
                      [(0, False, 0, None), (1, False, 1, None)])
            vstarted, token = _xchg_start("vector_grads_start", [vgroup])
        dwin_l[l] = _mm_dwin(f"dwin{l}", s["hb"], dproj, token)
        late = ([dwin_l[l]], [_landing(own(dwin_l[l]), me)], [(0, True, 0, None)])
        started_late, token = _xchg_start(f"grads_start_in{l}", [late])
        ggroups[l] = [(late, started_late[0]), (early, started_early[0])]
        dh = _mm_dh(f"dh{l}", dproj, wg_in[l], dz, token)
        if l > 0:
            dz, ln_stats[l] = _ln_bwd(f"ln_bwd{l}", dh, z[l], row(ln_g[l]))
        else:
            grad_x, gmeta, ln_stats[0] = _ln_bwd_input(dh, z[0], row(ln_g[0]))

    loss_row = jnp.pad(loss_blk[0:1, :], ((0, 0), (0, D_MODEL - 128)))
    g2048 = jnp.concatenate([ln_stats[0][0:2], ln_stats[1][0:1], ln_stats[2][0:1], ln_stats[1][1:2],
                             ln_stats[2][1:2], loss_row, jnp.zeros((1, D_MODEL), F32)], axis=0)
    g_dw_full = jnp.stack([g_layers[l]["dwdw"] for l in range(DEPTH)])
    g_lc_full = jnp.stack([g_layers[l]["dwl"] for l in range(DEPTH)])
    shard_pack = _pack_rows([_cols_to_slots(g_dw_full), _cols_to_slots(g_lc_full), _cols_to_slots(gmeta)],
                            lead=(N_DEV,))
    sgroup = ([shard_pack, g2048], [_landing(own(shard_pack), me), _landing(g2048, me)],
              [(0, True, 0, None), (1, False, 1, None)])
    sstarted, token = _xchg_start("small_grads_start", [sgroup])

    res = {}

    def flat2(a, cols):
        return a.reshape(-1, cols)

    big = (("w_in", 0, W_IN_SHARD), ("w_out", 1, D_MODEL), ("conv_pw_w", 2, CONV_W))
    prev = {n: None for n, _, _ in big}
    def update(name_, cols, recv, l):
        w_ = weights[name_]
        prev[name_] = _adamw(f"adamw_{name_}{l}", flat2(w_, cols), flat2(mom1[name_], cols),
                             flat2(mom2[name_], cols), recv, row0=l * w_.shape[1], prev=prev[name_])

    def small_update(tag, ci, recv):
        return _adamw(f"adamw_small_w{tag}", rep_wmv[0][ci], rep_wmv[1][ci], rep_wmv[2][ci], recv)

    def keep(unpacked, k):
        for n, a in unpacked.items():
            res.setdefault(n, [None] * 4)[k] = a

    r_512, r_64 = _xchg_wait("vector_grads_wait", vgroup, vstarted[0], [token])
    o512, o64 = small_update("512", 0, r_512), small_update("64", 2, r_64)
    early_done = []
    for k in range(4):
        unpacked = _unpack_small(w512=o512[k], w64=o64[k])
        keep(unpacked, k)
        early_done += list(unpacked.values())
    after = [token]
    for l in reversed(range(DEPTH)):
        late, early = ggroups[l]
        r_out, r_pw = _xchg_wait(f"grads_wait{l}_1", early[0], early[1], after)
        if l > 0:
            r_in, = _xchg_wait(f"grads_wait{l}_0", late[0], late[1], after)
            update("w_in", W_IN_SHARD, r_in, l)
        update("w_out", D_MODEL, r_out, l)
        update("conv_pw_w", CONV_W, r_pw, l)
        after = [prev["w_out"][0], prev["conv_pw_w"][0], prev["w_in"][0]]
    late = ggroups[0][0]
    r_in, = _xchg_wait("grads_wait0_0", late[0], late[1], after + early_done)
    update("w_in", W_IN_SHARD, r_in, 0)
    for name_, _, _ in big:
        res[name_] = [o.reshape(weights[name_].shape) for o in prev[name_]]

    r_small, r_2048 = _xchg_wait("small_grads_wait", sgroup, sstarted[0], [prev[n][0] for n, _, _ in big])
    sshapes = [weights[n].shape for n in shard_small_names]
    outs = _adamw("adamw_small_sharded", *shard_wmv, r_small)
    for k, o in enumerate(outs):
        for n, a in zip(shard_small_names, _unpack_rows(o, sshapes)):
            res.setdefault(n, [None] * 4)[k] = a
    o2048 = small_update("2048", 1, r_2048)
    for k in range(4):
        keep(_unpack_small(w2048=o2048[k]), k)
    loss = o2048[0][6, 0]

    order = ["meta_tokens", "ln_in_g", "ln_in_b", "w_in", "conv_dw_w", "conv_dw_b", "conv_ln_g", "conv_ln_b",
             "conv_pw_w", "conv_pw_b", "attn_sinks", "lru_conv_w", "lru_conv_b", "lru_wa", "lru_ba", "lru_wx",
             "lru_bx", "lru_lambda", "w_out", "ln_post_g", "ln_post_b"]
    return (loss, grad_x,
            *[res[n][0] for n in order], *[res[n][1] for n in order],
            *[res[n][2] for n in order], *[res[n][3] for n in order])
```

```python
import numpy as np
import jax
import jax.numpy as jnp
from jax import lax
from jax.experimental import pallas as pl
from jax.experimental.pallas import tpu as pltpu

F32 = jnp.float32
BF16 = jnp.bfloat16

D_MODEL = 2048
DEPTH = 2
N_META = 16
TB = 128
PAD0 = TB - N_META
CONV_W = 512
CONV_K = 31
HEAD_DIM = 64
N_HEADS = 16
N_KV = 4
GROUP = 4
ATT_W = 1024
KV_W = 256
ROT_DIM = 16
ROPE_THETA = 500000.0
LRU_W = 512
LRU_HEADS = 8
LRU_CONV_K = 4
LRU_C = 8.0
IN_TOTAL = 5120
N_DEV = 8
W_IN_SHARD = IN_TOTAL // N_DEV
LN_EPS = 1e-5
ALPHA = (2.0 * DEPTH) ** 0.25
NEG_INF = -1e30
ATT_SCALE = HEAD_DIM ** -0.5

ADAM_LR = 0.001
ADAM_B1 = 0.9
ADAM_B2 = 0.999
ADAM_EPS = 1e-08
ADAM_WD = 0.01
ADAM_STEP = 10

VMEM_LIMIT = 56 * 1024 * 1024
PACK_QUANTUM = 256 * 128

COL_CV, COL_CG, COL_CGATE = 0, 1, 2
COL_Q0 = 3
COL_K256 = 10
COL_V256 = 11
COL_AGATE1024 = 3
COL_RX, COL_RGATE = 8, 9
YC_CONV, YC_LRU = 2, 3


def _cp(n_axes, vmem=VMEM_LIMIT):
    return pltpu.CompilerParams(dimension_semantics=("arbitrary",) * n_axes, vmem_limit_bytes=vmem)


def _row_tile(lp, max_blocks):
    nb = lp // TB
    d = max(k for k in range(1, max_blocks + 1) if nb % k == 0)
    return TB * d


def _sig(x):
    return jax.nn.sigmoid(x)


def _dsilu(x, s):
    return s * (1.0 + x * (1.0 - s))


def _ln_core(z):
    mu = jnp.mean(z, axis=-1, keepdims=True)
    zc = z - mu
    var = jnp.mean(zc * zc, axis=-1, keepdims=True)
    rstd = lax.rsqrt(var + LN_EPS)
    return zc * rstd, rstd


def _ln_bwd_core(dy, xh, rstd, g):
    dxh = dy * g
    m1 = jnp.mean(dxh, axis=-1, keepdims=True)
    m2 = jnp.mean(dxh * xh, axis=-1, keepdims=True)
    return rstd * (dxh - m1 - xh * m2)


def _row_ids(shape, base):
    return lax.broadcasted_iota(jnp.int32, shape, 0) + base


def _colsum(x):
    return jnp.sum(x, axis=0, keepdims=True)


def _dot(a, b, dims):
    return lax.dot_general(a, b, (dims, ((), ())), preferred_element_type=F32)


NN = ((1,), (0,))
NT = ((1,), (1,))
TN = ((0,), (0,))


def _embed(x, meta_full, g, b):
    s = x.shape[1]
    lp = s + TB
    nb = lp // TB

    def body(x_ref, m_ref, g_ref, b_ref, o_ref, hb_ref):
        i = pl.program_id(0)

        @pl.when(i == 0)
        def _():
            o_ref[0:PAD0, :] = jnp.zeros((PAD0, D_MODEL), F32)
            o_ref[PAD0:TB, :] = m_ref[...]

        @pl.when(i > 0)
        def _():
            o_ref[...] = x_ref[...]

        xh, _ = _ln_core(o_ref[...])
        h = xh * g_ref[...] + b_ref[...]
        rows = _row_ids(h.shape, i * TB)
        hb_ref[...] = jnp.where(rows >= PAD0, h, 0.0).astype(BF16)

    return pl.pallas_call(
        body, name="embed", grid=(nb,),
        in_specs=[pl.BlockSpec((None, TB, D_MODEL), lambda i: (0, jnp.maximum(i - 1, 0), 0)),
                  pl.BlockSpec((N_META, D_MODEL), lambda i: (0, 0)),
                  pl.BlockSpec((1, D_MODEL), lambda i: (0, 0)),
                  pl.BlockSpec((1, D_MODEL), lambda i: (0, 0))],
        out_specs=[pl.BlockSpec((TB, D_MODEL), lambda i: (i, 0)),
                   pl.BlockSpec((TB, D_MODEL), lambda i: (i, 0))],
        out_shape=[jax.ShapeDtypeStruct((lp, D_MODEL), F32),
                   jax.ShapeDtypeStruct((lp, D_MODEL), BF16)],
        compiler_params=_cp(1),
    )(x, meta_full, g, b)


def _loss_head(z, target, g, b):
    lp = z.shape[0]
    nb = lp // TB

    def body(z_ref, t_ref, g_ref, b_ref, dz_ref, st_ref, loss_ref):
        i = pl.program_id(0)

        @pl.when(i == 0)
        def _():
            st_ref[...] = jnp.zeros(st_ref.shape, F32)
            loss_ref[...] = jnp.zeros(loss_ref.shape, F32)
            dz_ref[...] = jnp.zeros(dz_ref.shape, F32)

        @pl.when(i > 0)
        def _():
            xh, rstd = _ln_core(z_ref[...])
            gg = g_ref[...]
            y = xh * gg + b_ref[...]
            e = y - t_ref[...]
            part = 0.5 * jnp.sum(jnp.mean(e * e, axis=-1, keepdims=True), axis=0, keepdims=True)
            loss_ref[...] += jnp.broadcast_to(part, loss_ref.shape)
            dy = e / float(D_MODEL)
            st_ref[0:1, :] += _colsum(dy * xh)
            st_ref[1:2, :] += _colsum(dy)
            dz_ref[...] = _ln_bwd_core(dy, xh, rstd, gg)

    return pl.pallas_call(
        body, name="loss_head", grid=(nb,),
        in_specs=[pl.BlockSpec((TB, D_MODEL), lambda i: (i, 0)),
                  pl.BlockSpec((None, TB, D_MODEL), lambda i: (0, jnp.maximum(i - 1, 0), 0)),
                  pl.BlockSpec((1, D_MODEL), lambda i: (0, 0)),
                  pl.BlockSpec((1, D_MODEL), lambda i: (0, 0))],
        out_specs=[pl.BlockSpec((TB, D_MODEL), lambda i: (i, 0)),
                   pl.BlockSpec((8, D_MODEL), lambda i: (0, 0)),
                   pl.BlockSpec((8, 128), lambda i: (0, 0))],
        out_shape=[jax.ShapeDtypeStruct((lp, D_MODEL), F32),
                   jax.ShapeDtypeStruct((8, D_MODEL), F32),
                   jax.ShapeDtypeStruct((8, 128), F32)],
        compiler_params=_cp(1),
    )(z, target, g, b)


def _ln_bwd(name, dh, z, g):
    lp = z.shape[0]
    tr = _row_tile(lp, 3)

    def body(dh_ref, z_ref, g_ref, dz_ref, st_ref):
        i = pl.program_id(0)

        @pl.when(i == 0)
        def _():
            st_ref[...] = jnp.zeros(st_ref.shape, F32)

        xh, rstd = _ln_core(z_ref[...])
        rows = _row_ids(xh.shape, i * tr)
        dy = jnp.where(rows >= PAD0, dh_ref[...], 0.0)
        st_ref[0:1, :] += _colsum(dy * xh)
        st_ref[1:2, :] += _colsum(dy)
        dz_ref[...] = _ln_bwd_core(dy, xh, rstd, g_ref[...])

    return pl.pallas_call(
        body, name=name, grid=(lp // tr,),
        in_specs=[pl.BlockSpec((tr, D_MODEL), lambda i: (i, 0)),
                  pl.BlockSpec((tr, D_MODEL), lambda i: (i, 0)),
                  pl.BlockSpec((1, D_MODEL), lambda i: (0, 0))],
        out_specs=[pl.BlockSpec((tr, D_MODEL), lambda i: (i, 0)),
                   pl.BlockSpec((8, D_MODEL), lambda i: (0, 0))],
        out_shape=[jax.ShapeDtypeStruct((lp, D_MODEL), F32),
                   jax.ShapeDtypeStruct((8, D_MODEL), F32)],
        compiler_params=_cp(1),
    )(dh, z, g)


def _ln_bwd_input(dh, z, g):
    lp = z.shape[0]
    nb = lp // TB
    s = lp - TB

    def body(dh_ref, z_ref, g_ref, gx_ref, gm_ref, st_ref):
        i = pl.program_id(0)

        @pl.when(i == 0)
        def _():
            st_ref[...] = jnp.zeros(st_ref.shape, F32)

        xh, rstd = _ln_core(z_ref[...])
        rows = _row_ids(xh.shape, i * TB)
        dy = jnp.where(rows >= PAD0, dh_ref[...], 0.0)
        st_ref[0:1, :] += _colsum(dy * xh)
        st_ref[1:2, :] += _colsum(dy)
        dz = _ln_bwd_core(dy, xh, rstd, g_ref[...])
        gx_ref[...] = dz

        @pl.when(i == 0)
        def _():
            gm_ref[...] = dz[PAD0:TB, :]

    return pl.pallas_call(
        body, name="ln_in_bwd", grid=(nb,),
        in_specs=[pl.BlockSpec((TB, D_MODEL), lambda i: (i, 0)),
                  pl.BlockSpec((TB, D_MODEL), lambda i: (i, 0)),
                  pl.BlockSpec((1, D_MODEL), lambda i: (0, 0))],
        out_specs=[pl.BlockSpec((None, TB, D_MODEL), lambda i: (0, jnp.maximum(i - 1, 0), 0)),
                   pl.BlockSpec((N_META, D_MODEL), lambda i: (0, 0)),
                   pl.BlockSpec((8, D_MODEL), lambda i: (0, 0))],
        out_shape=[jax.ShapeDtypeStruct((1, s, D_MODEL), F32),
                   jax.ShapeDtypeStruct((N_META, D_MODEL), F32),
                   jax.ShapeDtypeStruct((8, D_MODEL), F32)],
        compiler_params=_cp(1),
    )(dh, z, g)


def _mm_proj(name, hb, wg_in):
    lp = hb.shape[0]
    tm = lp // 3

    def body(a_ref, b_ref, o_ref):
        b = jnp.concatenate([b_ref[0], b_ref[1]], axis=1)
        o_ref[...] = _dot(a_ref[...], b, NN)

    return pl.pallas_call(
        body, name=name, grid=(3, N_DEV // 2),
        in_specs=[pl.BlockSpec((tm, D_MODEL), lambda i, j: (i, 0)),
                  pl.BlockSpec((2, D_MODEL, W_IN_SHARD), lambda i, j: (j, 0, 0))],
        out_specs=pl.BlockSpec((tm, 2 * W_IN_SHARD), lambda i, j: (i, j)),
        out_shape=jax.ShapeDtypeStruct((lp, IN_TOTAL), F32),
        compiler_params=_cp(2),
    )(hb, wg_in)


def _mm_out(name, ycat, wout, z, g, b, g2, b2, dep):
    lp = ycat.shape[0]
    tm = lp // 6
    with_next = g2 is not None

    def body(a_ref, w_ref, z_ref, g_ref, b_ref, *rest):
        i = pl.program_id(0)
        xh, _ = _ln_core(z_ref[...])
        h = xh * g_ref[...] + b_ref[...]
        live = _row_ids(h.shape, i * tm) >= PAD0
        h = jnp.where(live, h, 0.0)
        zn = ALPHA * h + _dot(a_ref[...], w_ref[...], NN)
        if with_next:
            g2_ref, b2_ref, _, o_ref, hb_ref = rest
            xh2, _ = _ln_core(zn)
            hb_ref[...] = jnp.where(live, xh2 * g2_ref[...] + b2_ref[...], 0.0).astype(BF16)
        else:
            _, o_ref = rest
        o_ref[...] = zn

    vec = pl.BlockSpec((1, D_MODEL), lambda i: (0, 0))
    row_blk = pl.BlockSpec((tm, D_MODEL), lambda i: (i, 0))
    outs = pl.pallas_call(
        body, name=name, grid=(6,),
        in_specs=([row_blk, pl.BlockSpec((D_MODEL, D_MODEL), lambda i: (0, 0), pipeline_mode=pl.Buffered(1)),
                   row_blk, vec, vec] + ([vec, vec] if with_next else [])
                  + [pl.BlockSpec(memory_space=pl.ANY)]),
        out_specs=[row_blk, row_blk] if with_next else [row_blk],
        out_shape=([jax.ShapeDtypeStruct((lp, D_MODEL), F32)]
                   + ([jax.ShapeDtypeStruct((lp, D_MODEL), BF16)] if with_next else [])),
        compiler_params=_cp(1),
    )(ycat, wout, z, g, b, *((g2, b2) if with_next else ()), dep)
    return (outs[0], outs[1]) if with_next else (outs[0], None)


def _mm_dycat(name, dz, wout, dep):
    lp = dz.shape[0]
    tm = lp // 6

    def body(a_ref, w_ref, dep_ref, o_ref):
        del dep_ref
        o_ref[...] = _dot(a_ref[...].astype(BF16), w_ref[...], NT).astype(BF16)

    return pl.pallas_call(
        body, name=name, grid=(6,),
        in_specs=[pl.BlockSpec((tm, D_MODEL), lambda i: (i, 0)),
                  pl.BlockSpec((D_MODEL, D_MODEL), lambda i: (0, 0), pipeline_mode=pl.Buffered(1)),
                  pl.BlockSpec(memory_space=pl.ANY)],
        out_specs=pl.BlockSpec((tm, D_MODEL), lambda i: (i, 0)),
        out_shape=jax.ShapeDtypeStruct((lp, D_MODEL), BF16),
        compiler_params=_cp(1),
    )(dz, wout, dep)


def _mm_dwout(name, ycat, dz):
    lp = ycat.shape[0]
    tk = _row_tile(lp, 11)
    nk = lp // tk
    half = D_MODEL // 2

    def body(a_ref, b_ref, o_ref, acc_ref):
        k = pl.program_id(1)

        @pl.when(k == 0)
        def _():
            acc_ref[...] = jnp.zeros(acc_ref.shape, F32)

        acc_ref[...] += _dot(a_ref[...], b_ref[...].astype(BF16), TN)

        @pl.when(k == nk - 1)
        def _():
            o_ref[...] = acc_ref[...].astype(BF16)

    return pl.pallas_call(
        body, name=name, grid=(2, nk),
        in_specs=[pl.BlockSpec((tk, half), lambda h, k: (k, h)),
                  pl.BlockSpec((tk, D_MODEL), lambda h, k: (k, 0))],
        out_specs=pl.BlockSpec((half, D_MODEL), lambda h, k: (h, 0)),
        out_shape=jax.ShapeDtypeStruct((D_MODEL, D_MODEL), BF16),
        scratch_shapes=[pltpu.VMEM((half, D_MODEL), F32)],
        compiler_params=_cp(2),
    )(ycat, dz)


def _mm_dwin(name, hb, dproj, dep):
    lp = hb.shape[0]
    tk = _row_tile(lp, 11)
    nk = lp // tk

    def body(a_ref, b_ref, dep_ref, o_ref, acc_ref):
        del dep_ref
        k = pl.program_id(1)

        @pl.when(k == 0)
        def _():
            acc_ref[...] = jnp.zeros(acc_ref.shape, F32)

        acc_ref[...] += _dot(a_ref[...], b_ref[...], TN)

        @pl.when(k == nk - 1)
        def _():
            o_ref[0] = acc_ref[:, 0:W_IN_SHARD].astype(BF16)
            o_ref[1] = acc_ref[:, W_IN_SHARD:2 * W_IN_SHARD].astype(BF16)

    return pl.pallas_call(
        body, name=name, grid=(4, nk),
        in_specs=[pl.BlockSpec((tk, D_MODEL), lambda j, k: (k, 0)),
                  pl.BlockSpec((tk, 2 * W_IN_SHARD), lambda j, k: (k, j)),
                  pl.BlockSpec(memory_space=pl.ANY)],
        out_specs=pl.BlockSpec((2, D_MODEL, W_IN_SHARD), lambda j, k: (j, 0, 0)),
        out_shape=jax.ShapeDtypeStruct((N_DEV, D_MODEL, W_IN_SHARD), BF16),
        scratch_shapes=[pltpu.VMEM((D_MODEL, 2 * W_IN_SHARD), F32)],
        compiler_params=_cp(2),
    )(hb, dproj, dep)


def _mm_dh(name, dproj, wg_in, dz, dep):
    lp = dproj.shape[0]
    tm = lp // 6

    def body(a_ref, w_ref, dz_ref, dep_ref, o_ref, acc_ref):
        del dep_ref
        k = pl.program_id(1)

        @pl.when(k == 0)
        def _():
            acc_ref[...] = jnp.zeros(acc_ref.shape, F32)

        w = jnp.concatenate([w_ref[0], w_ref[1]], axis=1)
        acc_ref[...] += _dot(a_ref[...], w, NT)

        @pl.when(k == N_DEV // 2 - 1)
        def _():
            o_ref[...] = acc_ref[...] + ALPHA * dz_ref[...]

    return pl.pallas_call(
        body, name=name, grid=(6, N_DEV // 2),
        in_specs=[pl.BlockSpec((tm, 2 * W_IN_SHARD), lambda i, k: (i, k)),
                  pl.BlockSpec((2, D_MODEL, W_IN_SHARD), lambda i, k: (k, 0, 0)),
                  pl.BlockSpec((tm, D_MODEL), lambda i, k: (i, 0)),
                  pl.BlockSpec(memory_space=pl.ANY)],
        out_specs=pl.BlockSpec((tm, D_MODEL), lambda i, k: (i, 0)),
        out_shape=jax.ShapeDtypeStruct((lp, D_MODEL), F32),
        scratch_shapes=[pltpu.VMEM((tm, D_MODEL), F32)],
        compiler_params=_cp(2),
    )(dproj, wg_in, dz, dep)


SUB = 128


def _shift_plan(cat, n_shift, base):
    rolled = [cat] + [pltpu.roll(cat, b, axis=0) for b in range(1, 8)]
    return [(rolled[s % 8], base - 8 * (s // 8)) for s in range(n_shift)]


def _tap_sum(w_ref, plan, rows, init=None):
    blocks = []
    for r0 in range(0, rows, SUB):
        row = []
        for c0 in range(0, CONV_W, SUB):
            acc = (jnp.zeros((SUB, SUB), F32) if init is None
                   else jnp.broadcast_to(init[:, c0:c0 + SUB], (SUB, SUB)))
            for k, (arr, off) in enumerate(plan):
                acc = acc + w_ref[k:k + 1, c0:c0 + SUB] * arr[off + r0:off + r0 + SUB, c0:c0 + SUB]
            row.append(acc)
        blocks.append(jnp.concatenate(row, axis=1))
    return jnp.concatenate(blocks, axis=0)


def _tap_grads(dw_ref, dy, plan, rows):
    for c0 in range(0, CONV_W, SUB):
        dys = [dy[r0:r0 + SUB, c0:c0 + SUB] for r0 in range(0, rows, SUB)]
        for k, (arr, off) in enumerate(plan):
            part = None
            for ri, r0 in enumerate(range(0, rows, SUB)):
                prod = dys[ri] * arr[off + r0:off + r0 + SUB, c0:c0 + SUB]
                for i in range(SUB // 8):
                    piece = prod[8 * i:8 * i + 8, :]
                    part = piece if part is None else part + piece
            dw_ref[k:k + 1, c0:c0 + SUB] += jnp.sum(part, axis=0, keepdims=True)


CONV_HALO = 32


def _conv_chain(j, tb, cv_ref, cg_ref, cvp_ref, cgp_ref, wdw_ref, vec_ref, wpw_ref, c1_ref=None):
    cv = cv_ref[...]
    sg = _sig(cg_ref[...])
    c0 = cv * sg
    c0p = jnp.where(j > 0, cvp_ref[...] * _sig(cgp_ref[...]), 0.0)
    cat = jnp.concatenate([c0p, c0], axis=0)
    shifts = _shift_plan(cat, CONV_K, CONV_HALO)
    taps = [shifts[CONV_K - 1 - k] for k in range(CONV_K)]
    if c1_ref is None:
        c1 = _tap_sum(wdw_ref, taps, tb, init=vec_ref[0:1, :])
    else:
        c1 = c1_ref[...]
    xh, rstd = _ln_core(c1)
    c2 = xh * vec_ref[1:2, :] + vec_ref[2:3, :]
    s2 = _sig(c2)
    c3 = c2 * s2
    c4 = _dot(c3.astype(BF16), wpw_ref[...], NN) + vec_ref[3:4, :]
    return dict(cv=cv, sg=sg, taps=taps, c1=c1, xh=xh, rstd=rstd, c2=c2, s2=s2, c3=c3, c4=c4)


def _conv_in_specs(jmap, tb):
    def cur(col):
        return pl.BlockSpec((tb, 512), lambda n: (jmap(n), col))

    def prev(col):
        return pl.BlockSpec((CONV_HALO, 512),
                            lambda n: (jnp.maximum(jmap(n) * (tb // CONV_HALO) - 1, 0), col))

    return [cur(COL_CV), cur(COL_CG), prev(COL_CV), prev(COL_CG), cur(COL_CGATE)]


def _conv_param_specs():
    return [pl.BlockSpec((32, CONV_W), lambda n: (0, 0)),
            pl.BlockSpec((8, CONV_W), lambda n: (0, 0)),
            pl.BlockSpec((CONV_W, CONV_W), lambda n: (0, 0))]


def _conv_fwd(name, proj, wdw, vec, wpw):
    lp = proj.shape[0]
    tb = _row_tile(lp, 3)
    nb = lp // tb

    def body(cv_ref, cg_ref, cvp_ref, cgp_ref, gate_ref, wdw_ref, vec_ref, wpw_ref, o_ref, c1_ref):
        j = pl.program_id(0)
        c = _conv_chain(j, tb, cv_ref, cg_ref, cvp_ref, cgp_ref, wdw_ref, vec_ref, wpw_ref)
        gate = gate_ref[...]
        o_ref[...] = (c["c4"] * (gate * _sig(gate))).astype(BF16)
        c1_ref[...] = c["c1"]

    return pl.pallas_call(
        body, name=name, grid=(nb,),
        in_specs=_conv_in_specs(lambda n: n, tb) + _conv_param_specs(),
        out_specs=[pl.BlockSpec((tb, 512), lambda n: (n, YC_CONV)),
                   pl.BlockSpec((tb, CONV_W), lambda n: (n, 0))],
        out_shape=[jax.ShapeDtypeStruct((lp, D_MODEL), BF16),
                   jax.ShapeDtypeStruct((lp, CONV_W), F32)],
        compiler_params=_cp(1),
    )(proj, proj, proj, proj, proj, wdw, vec, wpw)


def _conv_bwd(name, proj, dycat, c1, wdw, vec, wpw):
    lp = proj.shape[0]
    tb = _row_tile(lp, 3)
    nb = lp // tb
    halo = CONV_HALO

    def body(cv_ref, cg_ref, cvp_ref, cgp_ref, gate_ref, dy_ref, c1_ref, wdw_ref, vec_ref, wpw_ref,
             dp_ref, dwdw_ref, dvec_ref, dwpw_ref, carry_ref):
        n = pl.program_id(0)
        j = nb - 1 - n

        @pl.when(n == 0)
        def _():
            carry_ref[...] = jnp.zeros(carry_ref.shape, F32)
            dwdw_ref[...] = jnp.zeros(dwdw_ref.shape, F32)
            dvec_ref[...] = jnp.zeros(dvec_ref.shape, F32)
            dwpw_ref[...] = jnp.zeros(dwpw_ref.shape, F32)

        c = _conv_chain(j, tb, cv_ref, cg_ref, cvp_ref, cgp_ref, wdw_ref, vec_ref, wpw_ref, c1_ref)
        dy = dy_ref[...].astype(F32)
        gate = gate_ref[...]
        sgate = _sig(gate)
        dc4 = dy * (gate * sgate)
        dgate = dy * c["c4"] * _dsilu(gate, sgate)
        dc4b = dc4.astype(BF16)
        dvec_ref[3:4, :] += _colsum(dc4)
        dwpw_ref[...] += _dot(c["c3"].astype(BF16), dc4b, TN)
        dc3 = _dot(dc4b, wpw_ref[...], NT)
        dc2 = dc3 * _dsilu(c["c2"], c["s2"])
        dvec_ref[1:2, :] += _colsum(dc2 * c["xh"])
        dvec_ref[2:3, :] += _colsum(dc2)
        dc1 = _ln_bwd_core(dc2, c["xh"], c["rstd"], vec_ref[1:2, :])
        dvec_ref[0:1, :] += _colsum(dc1)
        _tap_grads(dwdw_ref, dc1, c["taps"], tb)
        dcat = jnp.concatenate([dc1, carry_ref[...]], axis=0)
        total = tb + halo
        up = [dcat] + [pltpu.roll(dcat, total - b, axis=0) for b in range(1, 8)]
        ahead = [(up[(CONV_K - 1 - k) % 8], 8 * ((CONV_K - 1 - k) // 8)) for k in range(CONV_K)]
        dc0 = _tap_sum(wdw_ref, ahead, tb)
        carry_ref[...] = dc1[0:halo, :]
        sg = c["sg"]
        dcv = dc0 * sg
        dcg = dc0 * c["cv"] * sg * (1.0 - sg)
        dp_ref[:, 0:512] = dcv.astype(BF16)
        dp_ref[:, 512:1024] = dcg.astype(BF16)
        dp_ref[:, 1024:1536] = dgate.astype(BF16)

    jmap = lambda n: nb - 1 - n
    return pl.pallas_call(
        body, name=name, grid=(nb,),
        in_specs=(_conv_in_specs(jmap, tb)
                  + [pl.BlockSpec((tb, 512), lambda n: (jmap(n), YC_CONV)),
                     pl.BlockSpec((tb, CONV_W), lambda n: (jmap(n), 0))]
                  + _conv_param_specs()),
        out_specs=[pl.BlockSpec((tb, 1536), lambda n: (jmap(n), 0)),
                   pl.BlockSpec((32, CONV_W), lambda n: (0, 0)),
                   pl.BlockSpec((8, CONV_W), lambda n: (0, 0)),
                   pl.BlockSpec((CONV_W, CONV_W), lambda n: (0, 0))],
        out_shape=[jax.ShapeDtypeStruct((lp, IN_TOTAL), BF16),
                   jax.ShapeDtypeStruct((32, CONV_W), F32),
                   jax.ShapeDtypeStruct((8, CONV_W), F32),
                   jax.ShapeDtypeStruct((CONV_W, CONV_W), F32)],
        scratch_shapes=[pltpu.VMEM((halo, CONV_W), F32)],
        compiler_params=_cp(1),
    )(proj, proj, proj, proj, proj, dycat, c1, wdw, vec, wpw)


def _rope_tables(lp):
    half = ROT_DIM // 2
    inv_freq = ROPE_THETA ** (-jnp.arange(half, dtype=F32) / half)
    pos = (jnp.arange(lp, dtype=jnp.int32) - PAD0).astype(F32)
    ang = pos[:, None] * inv_freq[None, :]
    cos, sin = jnp.cos(ang), jnp.sin(ang)
    ones = jnp.ones((lp, HEAD_DIM - ROT_DIM), F32)
    zeros = jnp.zeros((lp, HEAD_DIM - ROT_DIM), F32)
    zh = jnp.zeros((lp, half), F32)
    c = jnp.concatenate([cos, cos, ones], axis=1)
    sa = jnp.concatenate([-sin, zh, zeros], axis=1)
    sb = jnp.concatenate([zh, sin, zeros], axis=1)
    tile = lambda t: jnp.tile(t, (1, KV_W // HEAD_DIM))
    return tile(c), tile(sa), tile(sb)


def _rot(x, c, sa, sb):
    w = x.shape[1]
    return x * c + pltpu.roll(x, w - 8, axis=1) * sa + pltpu.roll(x, 8, axis=1) * sb


def _rot_t(dy, c, sa, sb):
    w = dy.shape[1]
    return dy * c + pltpu.roll(dy * sa, 8, axis=1) + pltpu.roll(dy * sb, w - 8, axis=1)


def _rope_fwd(name, proj, tabs):
    lp = proj.shape[0]
    tr = _row_tile(lp, 11)

    def body(q0_ref, q1_ref, k_ref, c_ref, sa_ref, sb_ref, qr_ref, kr_ref):
        c, sa, sb = c_ref[...], sa_ref[...], sb_ref[...]
        c2 = jnp.concatenate([c, c], axis=1)
        sa2 = jnp.concatenate([sa, sa], axis=1)
        sb2 = jnp.concatenate([sb, sb], axis=1)
        qr_ref[:, 0:512] = (_rot(q0_ref[...], c2, sa2, sb2) * ATT_SCALE).astype(BF16)
        qr_ref[:, 512:1024] = (_rot(q1_ref[...], c2, sa2, sb2) * ATT_SCALE).astype(BF16)
        kr_ref[...] = _rot(k_ref[...], c, sa, sb).astype(BF16)

    tab = pl.BlockSpec((tr, KV_W), lambda i: (i, 0))
    return pl.pallas_call(
        body, name=name, grid=(lp // tr,),
        in_specs=[pl.BlockSpec((tr, 512), lambda i: (i, COL_Q0)),
                  pl.BlockSpec((tr, 512), lambda i: (i, COL_Q0 + 1)),
                  pl.BlockSpec((tr, KV_W), lambda i: (i, COL_K256)),
                  tab, tab, tab],
        out_specs=[pl.BlockSpec((tr, ATT_W), lambda i: (i, 0)),
                   pl.BlockSpec((tr, KV_W), lambda i: (i, 0))],
        out_shape=[jax.ShapeDtypeStruct((lp, ATT_W), BF16),
                   jax.ShapeDtypeStruct((lp, KV_W), BF16)],
        compiler_params=_cp(1),
    )(proj, proj, proj, *tabs)


def _attn_mask(j):
    qi = lax.broadcasted_iota(jnp.int32, (GROUP * TB, 3 * TB), 0) & (TB - 1)
    cc = lax.broadcasted_iota(jnp.int32, (GROUP * TB, 3 * TB), 1)
    jj = cc & (TB - 1)
    is_meta = jj >= PAD0
    p0 = (cc < TB) & is_meta & (j >= 1)
    p1 = (cc >= TB) & (cc < 2 * TB) & (jj > qi) & (j >= 2)
    p2 = (cc >= 2 * TB) & (jj <= qi) & ((j >= 1) | is_meta)
    return p0 | p1 | p2


def _lane_group(rows):
    return lax.broadcasted_iota(jnp.int32, (rows, KV_W), 1) // HEAD_DIM


def _stack_heads(x, kv, lgq):
    parts = []
    for g in range(GROUP):
        sh = ((kv - g) % GROUP) * HEAD_DIM
        moved = x if sh == 0 else pltpu.roll(x, sh, axis=1)
        parts.append(jnp.where(lgq == kv, moved, 0.0))
    return jnp.concatenate(parts, axis=0).astype(BF16)


def _unstack_heads(r, kv):
    out = None
    for g in range(GROUP):
        blk = r[g * TB:(g + 1) * TB, :]
        sh = ((g - kv) % GROUP) * HEAD_DIM
        blk = blk if sh == 0 else pltpu.roll(blk, sh, axis=1)
        out = blk if out is None else out + blk
    return out


def _sink_column(sinks, kv):
    lane = lax.broadcasted_iota(jnp.int32, (1, 128), 1)
    cols = []
    for g in range(GROUP):
        sg = jnp.sum(jnp.where(lane == kv * GROUP + g, sinks, 0.0), axis=1, keepdims=True)
        cols.append(jnp.broadcast_to(sg, (TB, 1)))
    return jnp.concatenate(cols, axis=0)


def _attn_kv(kall, vall, lg, kv):
    km = jnp.where(lg == kv, kall, 0.0).astype(BF16)
    vm = jnp.where(lg == kv, vall, 0.0).astype(BF16)
    ones = jnp.where(lg == kv, 1.0, 0.0).astype(BF16)
    return km, vm, ones


def _attn_specs(jmap):
    blk = lambda col: pl.BlockSpec((TB, KV_W), lambda n: (jmap(n), col))
    prv = lambda col: pl.BlockSpec((TB, KV_W), lambda n: (jnp.maximum(jmap(n) - 1, 0), col))
    met = lambda col: pl.BlockSpec((TB, KV_W), lambda n: (0, col))
    return dict(
        qr=pl.BlockSpec((TB, ATT_W), lambda n: (jmap(n), 0)),
        k=[met(0), prv(0), blk(0)],
        v=[met(COL_V256), prv(COL_V256), blk(COL_V256)],
        gate=pl.BlockSpec((TB, ATT_W), lambda n: (jmap(n), COL_AGATE1024)),
        sinks=pl.BlockSpec((8, 128), lambda n: (0, 0)),
    )


ATTN_BWD_HEAD_SETS = ((0, 1, 2, 3),)


def _attn_fwd(name, qr, kr, proj, sinks_row, ycat):
    lp = proj.shape[0]
    nb = lp // TB
    sp = _attn_specs(lambda n: n)

    def body(qr_ref, km_ref, kp_ref, kc_ref, vm_ref, vp_ref, vc_ref, gate_ref, sink_ref, yin_ref, o_ref):
        del yin_ref
        j = pl.program_id(0)
        valid = _attn_mask(j)
        kall = jnp.concatenate([km_ref[...], kp_ref[...], kc_ref[...]], axis=0).astype(F32)
        vall = jnp.concatenate([vm_ref[...], vp_ref[...], vc_ref[...]], axis=0)
        lg = _lane_group(3 * TB)
        lgq = _lane_group(TB)
        lg4 = _lane_group(GROUP * TB)
        sinks = sink_ref[0:1, :]
        heads = range(N_KV)
        cols = [slice(kv * KV_W, (kv + 1) * KV_W) for kv in heads]
        kvo = [_attn_kv(kall, vall, lg, kv) for kv in heads]
        qst = [_stack_heads(qr_ref[:, cols[kv]].astype(F32), kv, lgq) for kv in heads]
        s = [jnp.where(valid, _dot(qst[kv], kvo[kv][0], NT), NEG_INF) for kv in heads]
        eb, es = [], []
        for kv in heads:
            sinkcol = _sink_column(sinks, kv)
            m = jnp.maximum(jnp.max(s[kv], axis=-1, keepdims=True), sinkcol)
            eb.append(jnp.exp(s[kv] - m).astype(BF16))
            es.append(jnp.exp(sinkcol - m))
        r = [_dot(eb[kv], kvo[kv][1], NN) for kv in heads]
        inv = [1.0 / (_dot(eb[kv], kvo[kv][2], NN) + es[kv]) for kv in heads]
        for kv in heads:
            out = jnp.where(lg4 == kv, r[kv] * inv[kv], 0.0)
            gate = gate_ref[:, cols[kv]]
            o_ref[:, cols[kv]] = (_unstack_heads(out, kv) * (gate * _sig(gate))).astype(BF16)

    return pl.pallas_call(
        body, name=name, grid=(nb,),
        in_specs=[sp["qr"]] + sp["k"] + sp["v"] + [sp["gate"], sp["sinks"],
                                                   pl.BlockSpec(memory_space=pl.ANY)],
        out_specs=pl.BlockSpec((TB, ATT_W), lambda n: (n, 0)),
        out_shape=jax.ShapeDtypeStruct((lp, D_MODEL), BF16),
        input_output_aliases={9: 0},
        compiler_params=_cp(1),
    )(qr, kr, kr, kr, proj, proj, proj, proj, sinks_row, ycat)


def _attn_bwd(name, qr, kr, proj, sinks_row, dycat, dep):
    lp = proj.shape[0]
    nb = lp // TB
    sp = _attn_specs(lambda n: n)

    def body(qr_ref, km_ref, kp_ref, kc_ref, vm_ref, vp_ref, vc_ref, gate_ref, sink_ref, dy_ref, dep_ref,
             dq_ref, dgate_ref, dk_ref, dv_ref, dsink_ref):
        del dep_ref
        j = pl.program_id(0)

        @pl.when(j == 0)
        def _():
            dk_ref[...] = jnp.zeros(dk_ref.shape, F32)
            dv_ref[...] = jnp.zeros(dv_ref.shape, F32)
            dsink_ref[...] = jnp.zeros(dsink_ref.shape, F32)

        valid = _attn_mask(j)
        kall = jnp.concatenate([km_ref[...], kp_ref[...], kc_ref[...]], axis=0).astype(F32)
        vall = jnp.concatenate([vm_ref[...], vp_ref[...], vc_ref[...]], axis=0)
        lg = _lane_group(3 * TB)
        lgq = _lane_group(TB)
        lg4 = _lane_group(GROUP * TB)
        sinks = sink_ref[0:1, :]
        lane = lax.broadcasted_iota(jnp.int32, (1, 128), 1)
        def stages(heads):
            dsink = jnp.zeros((1, 128), F32)
            cols = {kv: slice(kv * KV_W, (kv + 1) * KV_W) for kv in heads}
            kvo = {kv: _attn_kv(kall, vall, lg, kv) for kv in heads}
            qst = {kv: _stack_heads(qr_ref[:, cols[kv]].astype(F32), kv, lgq) for kv in heads}
            s = {kv: jnp.where(valid, _dot(qst[kv], kvo[kv][0], NT), NEG_INF) for kv in heads}
            eb, es = {}, {}
            for kv in heads:
                sinkcol = _sink_column(sinks, kv)
                m = jnp.maximum(jnp.max(s[kv], axis=-1, keepdims=True), sinkcol)
                eb[kv] = jnp.exp(s[kv] - m).astype(BF16)
                es[kv] = jnp.exp(sinkcol - m)
            r = {kv: _dot(eb[kv], kvo[kv][1], NN) for kv in heads}
            inv = {kv: 1.0 / (_dot(eb[kv], kvo[kv][2], NN) + es[kv]) for kv in heads}
            dost, dcol = {}, {}
            for kv in heads:
                att = _unstack_heads(jnp.where(lg4 == kv, r[kv] * inv[kv], 0.0), kv)
                gate = gate_ref[:, cols[kv]]
                sgate = _sig(gate)
                dy = dy_ref[:, cols[kv]].astype(F32)
                dgate_ref[:, cols[kv]] = (dy * att * _dsilu(gate, sgate)).astype(BF16)
                dsc = dy * (gate * sgate) * _unstack_heads(jnp.where(lg4 == kv, inv[kv], 0.0), kv)
                dost[kv] = _stack_heads(dsc, kv, lgq)
                dd = dsc * att
                dcol[kv] = jnp.concatenate(
                    [jnp.sum(jnp.where(lgq == g, dd, 0.0), axis=1, keepdims=True) for g in range(GROUP)], axis=0)
            dp = {kv: _dot(dost[kv], kvo[kv][1], NT) for kv in heads}
            ds = {}
            for kv in heads:
                ds[kv] = (eb[kv].astype(F32) * (dp[kv] - dcol[kv])).astype(BF16)
                pd = es[kv] * dcol[kv]
                for g in range(GROUP):
                    tot = jnp.sum(pd[g * TB:(g + 1) * TB, :], axis=0, keepdims=True)
                    dsink = dsink - jnp.where(lane == kv * GROUP + g, tot, 0.0)
            dqs = {kv: _dot(ds[kv], kvo[kv][0], NN) for kv in heads}
            dks = [_dot(ds[kv], qst[kv], TN) for kv in heads]
            dvs = [_dot(eb[kv], dost[kv], TN) for kv in heads]
            for kv in heads:
                dq_ref[:, cols[kv]] = _unstack_heads(dqs[kv], kv)
            return sum(dks[1:], dks[0]), sum(dvs[1:], dvs[0]), dsink

        parts = [stages(hs) for hs in ATTN_BWD_HEAD_SETS]
        dkall = sum([p[0] for p in parts[1:]], parts[0][0])
        dvall = sum([p[1] for p in parts[1:]], parts[0][1])
        dsink = sum([p[2] for p in parts[1:]], parts[0][2])
        dsink_ref[0:1, :] += dsink
        prev = pl.multiple_of(jnp.maximum(j - 1, 0) * TB, TB)
        cur = pl.multiple_of(j * TB, TB)
        dk_ref[0:TB, :] += dkall[0:TB]
        dv_ref[0:TB, :] += dvall[0:TB]
        dk_ref[pl.ds(prev, TB), :] += dkall[TB:2 * TB]
        dv_ref[pl.ds(prev, TB), :] += dvall[TB:2 * TB]
        dk_ref[pl.ds(cur, TB), :] += dkall[2 * TB:3 * TB]
        dv_ref[pl.ds(cur, TB), :] += dvall[2 * TB:3 * TB]

    return pl.pallas_call(
        body, name=name, grid=(nb,),
        in_specs=[sp["qr"]] + sp["k"] + sp["v"] + [sp["gate"], sp["sinks"],
                                                   pl.BlockSpec((TB, ATT_W), lambda n: (n, 0)),
                                                   pl.BlockSpec(memory_space=pl.ANY)],
        out_specs=[pl.BlockSpec((TB, ATT_W), lambda n: (n, 0)),
                   pl.BlockSpec((TB, ATT_W), lambda n: (n, 0)),
                   pl.BlockSpec((lp, KV_W), lambda n: (0, 0)),
                   pl.BlockSpec((lp, KV_W), lambda n: (0, 0)),
                   pl.BlockSpec((8, 128), lambda n: (0, 0))],
        out_shape=[jax.ShapeDtypeStruct((lp, ATT_W), F32),
                   jax.ShapeDtypeStruct((lp, ATT_W), BF16),
                   jax.ShapeDtypeStruct((lp, KV_W), F32),
                   jax.ShapeDtypeStruct((lp, KV_W), F32),
                   jax.ShapeDtypeStruct((8, 128), F32)],
        compiler_params=_cp(1),
    )(qr, kr, kr, kr, proj, proj, proj, proj, sinks_row, dycat, dep)


def _attn_assemble(name, dq, dgate, dk, dv, tabs, dproj):
    lp = dq.shape[0]
    tr = _row_tile(lp, 11)

    def body(dq_ref, dg_ref, dk_ref, dv_ref, c_ref, sa_ref, sb_ref, din_ref, o_ref):
        del din_ref
        cidx = pl.program_id(1)
        c, sa, sb = c_ref[...], sa_ref[...], sb_ref[...]

        @pl.when(cidx < 2)
        def _():
            c2 = jnp.concatenate([c, c], axis=1)
            sa2 = jnp.concatenate([sa, sa], axis=1)
            sb2 = jnp.concatenate([sb, sb], axis=1)
            o_ref[...] = (_rot_t(dq_ref[...], c2, sa2, sb2) * ATT_SCALE).astype(BF16)

        @pl.when(cidx == 2)
        def _():
            o_ref[:, 0:KV_W] = _rot_t(dk_ref[...], c, sa, sb).astype(BF16)
            o_ref[:, KV_W:2 * KV_W] = dv_ref[...].astype(BF16)

        @pl.when(cidx > 2)
        def _():
            o_ref[...] = dg_ref[...]

    tab = pl.BlockSpec((tr, KV_W), lambda n, c: (n, 0))
    return pl.pallas_call(
        body, name=name, grid=(lp // tr, 5),
        in_specs=[pl.BlockSpec((tr, 512), lambda n, c: (n, jnp.minimum(c, 1))),
                  pl.BlockSpec((tr, 512), lambda n, c: (n, jnp.clip(c - 3, 0, 1))),
                  tab, tab,
                  tab, tab, tab,
                  pl.BlockSpec(memory_space=pl.ANY)],
        out_specs=pl.BlockSpec((tr, 512), lambda n, c: (n, COL_Q0 + c)),
        out_shape=jax.ShapeDtypeStruct((lp, IN_TOTAL), BF16),
        input_output_aliases={7: 0},
        compiler_params=_cp(2),
    )(dq, dgate, dk, dv, *tabs, dproj)


def _softplus_neg(lam):
    t = jnp.exp(-jnp.abs(lam))
    u = 1.0 + t
    den = jnp.where(u == 1.0, 1.0, u - 1.0)
    l1p = jnp.where(u == 1.0, t, jnp.log(u) * (t / den))
    return jnp.maximum(-lam, 0.0) + l1p


def _lru_chain(j, tb, rx_ref, rxp_ref, wl_ref, vec_ref, wa_ref, wx_ref):
    rx = rx_ref[...]
    rxp = jnp.where(j > 0, rxp_ref[...], 0.0)
    cat = jnp.concatenate([rxp, rx], axis=0)
    views = [cat[8:8 + tb, :]] + [pltpu.roll(cat, s, axis=0)[8:8 + tb, :] for s in range(1, LRU_CONV_K)]
    x1 = jnp.broadcast_to(vec_ref[0:1, :], (tb, LRU_W))
    for k in range(LRU_CONV_K):
        x1 = x1 + wl_ref[k:k + 1, :] * views[LRU_CONV_K - 1 - k]
    x1b = x1.astype(BF16)
    r = _sig(_dot(x1b, wa_ref[...], NN) + vec_ref[1:2, :])
    ig = _sig(_dot(x1b, wx_ref[...], NN) + vec_ref[2:3, :])
    sp = _softplus_neg(vec_ref[3:4, :])
    log_a = -LRU_C * r * sp
    rows = _row_ids((tb, LRU_W), j * tb)
    live = rows >= PAD0
    a = jnp.where(live, jnp.exp(log_a), 0.0)
    y2 = 2.0 * log_a
    em = -jnp.tanh(0.5 * y2) * (jnp.exp(y2) + 1.0)
    mult = jnp.sqrt(em)
    return dict(views=views, x1=x1, x1b=x1b, r=r, ig=ig, sp=sp, a=a, mult=mult, live=live, a_raw=jnp.exp(log_a))


def _scan_slabs(a, u, forward):
    tb = a.shape[0]
    rows = lax.broadcasted_iota(jnp.int32, (tb, SUB), 0)
    outs_a, outs_u = [], []
    for c0 in range(0, a.shape[1], SUB):
        ac, uc = a[:, c0:c0 + SUB], u[:, c0:c0 + SUB]
        d = 1
        while d < tb:
            if forward:
                keep, sh = rows >= d, d
            else:
                keep, sh = rows < tb - d, tb - d
            an = jnp.where(keep, pltpu.roll(ac, sh, axis=0), 1.0)
            un = jnp.where(keep, pltpu.roll(uc, sh, axis=0), 0.0)
            uc = ac * un + uc
            ac = ac * an
            d *= 2
        outs_a.append(ac)
        outs_u.append(uc)
    return jnp.concatenate(outs_a, axis=1), jnp.concatenate(outs_u, axis=1)


def _lru_specs(jmap, tb):
    return [pl.BlockSpec((tb, 512), lambda n: (jmap(n), COL_RX)),
            pl.BlockSpec((8, 512), lambda n: (jnp.maximum(jmap(n) * (tb // 8) - 1, 0), COL_RX)),
            pl.BlockSpec((tb, 512), lambda n: (jmap(n), COL_RGATE))]


def _lru_param_specs():
    return [pl.BlockSpec((8, LRU_W), lambda n: (0, 0)),
            pl.BlockSpec((8, LRU_W), lambda n: (0, 0)),
            pl.BlockSpec((LRU_W, LRU_W), lambda n: (0, 0)),
            pl.BlockSpec((LRU_W, LRU_W), lambda n: (0, 0))]


def _lru_fwd(name, proj, wl, vec, wa, wx, ycat):
    lp = proj.shape[0]
    tb = _row_tile(lp, 3)
    nb = lp // tb

    def body(rx_ref, rxp_ref, gate_ref, wl_ref, vec_ref, wa_ref, wx_ref, yin_ref, o_ref, h_ref, carry_ref):
        del yin_ref
        j = pl.program_id(0)

        @pl.when(j == 0)
        def _():
            carry_ref[...] = jnp.zeros(carry_ref.shape, F32)

        c = _lru_chain(j, tb, rx_ref, rxp_ref, wl_ref, vec_ref, wa_ref, wx_ref)
        u = jnp.where(c["live"], c["mult"] * (c["ig"] * c["x1"]), 0.0)
        a, u = _scan_slabs(c["a"], u, forward=True)
        h = u + a * carry_ref[0:1, :]
        carry_ref[...] = h[tb - 8:tb, :]
        carry_ref[0:1, :] = h[tb - 1:tb, :]
        h_ref[...] = h
        gate = gate_ref[...]
        o_ref[...] = (h * (gate * _sig(gate))).astype(BF16)

    return pl.pallas_call(
        body, name=name, grid=(nb,),
        in_specs=_lru_specs(lambda n: n, tb) + _lru_param_specs() + [pl.BlockSpec(memory_space=pl.ANY)],
        out_specs=[pl.BlockSpec((tb, 512), lambda n: (n, YC_LRU)),
                   pl.BlockSpec((tb, LRU_W), lambda n: (n, 0))],
        out_shape=[jax.ShapeDtypeStruct((lp, D_MODEL), BF16),
                   jax.ShapeDtypeStruct((lp, LRU_W), F32)],
        input_output_aliases={7: 0},
        scratch_shapes=[pltpu.VMEM((8, LRU_W), F32)],
        compiler_params=_cp(1),
    )(proj, proj, proj, wl, vec, wa, wx, ycat)


def _lru_bwd(name, proj, dycat, hstate, wl, vec, wa, wx, dproj):
    lp = proj.shape[0]
    tb = _row_tile(lp, 3)
    nb = lp // tb

    def body(rx_ref, rxp_ref, gate_ref, dy_ref, h_ref, hp_ref, wl_ref, vec_ref, wa_ref, wx_ref, din_ref,
             dp_ref, dwl_ref, dvec_ref, dwa_ref, dwx_ref, dhc_ref, anx_ref, dxc_ref):
        del din_ref
        n = pl.program_id(0)
        j = nb - 1 - n

        @pl.when(n == 0)
        def _():
            dhc_ref[...] = jnp.zeros(dhc_ref.shape, F32)
            anx_ref[...] = jnp.zeros(anx_ref.shape, F32)
            dxc_ref[...] = jnp.zeros(dxc_ref.shape, F32)
            dwl_ref[...] = jnp.zeros(dwl_ref.shape, F32)
            dvec_ref[...] = jnp.zeros(dvec_ref.shape, F32)
            dwa_ref[...] = jnp.zeros(dwa_ref.shape, F32)
            dwx_ref[...] = jnp.zeros(dwx_ref.shape, F32)

        c = _lru_chain(j, tb, rx_ref, rxp_ref, wl_ref, vec_ref, wa_ref, wx_ref)
        a, mult, r, ig, x1, live = c["a"], c["mult"], c["r"], c["ig"], c["x1"], c["live"]
        h = h_ref[...]
        gate = gate_ref[...]
        sgate = _sig(gate)
        dy = dy_ref[...].astype(F32)
        gsum = dy * (gate * sgate)
        dgate = dy * h * _dsilu(gate, sgate)
        rows = lax.broadcasted_iota(jnp.int32, (tb, LRU_W), 0)
        bb = jnp.where(rows == tb - 1, anx_ref[0:1, :], pltpu.roll(a, tb - 1, axis=0))
        bb, gg = _scan_slabs(bb, gsum, forward=False)
        dh = gg + bb * dhc_ref[0:1, :]
        dhc_ref[...] = dh[0:8, :]
        anx_ref[...] = a[0:8, :]
        hprev = jnp.where(rows == 0, jnp.where(j > 0, hp_ref[7:8, :], 0.0), pltpu.roll(h, 1, axis=0))
        du = jnp.where(live, dh, 0.0)
        da = jnp.where(live, dh * hprev, 0.0)
        ar = c["a_raw"]
        dmult = du * (ig * x1)
        di = du * mult * x1
        dx1 = du * mult * ig
        dloga = da * ar - dmult * ar * ar / mult
        dr = dloga * (-LRU_C * c["sp"])
        dvec_ref[3:4, :] += _colsum(dloga * (-LRU_C * r))
        dza = dr * r * (1.0 - r)
        dzx = di * ig * (1.0 - ig)
        dzab, dzxb = dza.astype(BF16), dzx.astype(BF16)
        dvec_ref[1:2, :] += _colsum(dza)
        dvec_ref[2:3, :] += _colsum(dzx)
        dwa_ref[...] += _dot(c["x1b"], dzab, TN)
        dwx_ref[...] += _dot(c["x1b"], dzxb, TN)
        dx1 = dx1 + _dot(dzab, wa_ref[...], NT) + _dot(dzxb, wx_ref[...], NT)
        dvec_ref[0:1, :] += _colsum(dx1)
        for k in range(LRU_CONV_K):
            dwl_ref[k:k + 1, :] += _colsum(dx1 * c["views"][LRU_CONV_K - 1 - k])
        dcat = jnp.concatenate([dx1, dxc_ref[...]], axis=0)
        drx = jnp.zeros((tb, LRU_W), F32)
        for k in range(LRU_CONV_K):
            s = LRU_CONV_K - 1 - k
            view = dcat[0:tb, :] if s == 0 else pltpu.roll(dcat, tb + 8 - s, axis=0)[0:tb, :]
            drx = drx + wl_ref[k:k + 1, :] * view
        dxc_ref[...] = dx1[0:8, :]
        dp_ref[:, 0:512] = drx.astype(BF16)
        dp_ref[:, 512:1024] = dgate.astype(BF16)

        @pl.when(n == nb - 1)
        def _():
            lam = vec_ref[3:4, :]
            dvec_ref[3:4, :] = dvec_ref[3:4, :] * (-_sig(-lam))

    jmap = lambda n: nb - 1 - n
    return pl.pallas_call(
        body, name=name, grid=(nb,),
        in_specs=(_lru_specs(jmap, tb)
                  + [pl.BlockSpec((tb, 512), lambda n: (jmap(n), YC_LRU)),
                     pl.BlockSpec((tb, LRU_W), lambda n: (jmap(n), 0)),
                     pl.BlockSpec((8, LRU_W), lambda n: (jnp.maximum(jmap(n) * (tb // 8) - 1, 0), 0))]
                  + _lru_param_specs() + [pl.BlockSpec(memory_space=pl.ANY)]),
        out_specs=[pl.BlockSpec((tb, 1024), lambda n: (jmap(n), 4)),
                   pl.BlockSpec((8, LRU_W), lambda n: (0, 0)),
                   pl.BlockSpec((8, LRU_W), lambda n: (0, 0)),
                   pl.BlockSpec((LRU_W, LRU_W), lambda n: (0, 0)),
                   pl.BlockSpec((LRU_W, LRU_W), lambda n: (0, 0))],
        out_shape=[jax.ShapeDtypeStruct((lp, IN_TOTAL), BF16),
                   jax.ShapeDtypeStruct((8, LRU_W), F32),
                   jax.ShapeDtypeStruct((8, LRU_W), F32),
                   jax.ShapeDtypeStruct((LRU_W, LRU_W), F32),
                   jax.ShapeDtypeStruct((LRU_W, LRU_W), F32)],
        input_output_aliases={10: 0},
        scratch_shapes=[pltpu.VMEM((8, LRU_W), F32), pltpu.VMEM((8, LRU_W), F32), pltpu.VMEM((8, LRU_W), F32)],
        compiler_params=_cp(1),
    )(proj, proj, proj, dycat, hstate, hstate, wl, vec, wa, wx, dproj)


_HBM = pl.BlockSpec(memory_space=pltpu.HBM)
_SEM = pl.BlockSpec(memory_space=pltpu.SEMAPHORE)
_ANY = pl.BlockSpec(memory_space=pl.ANY)
_EFFECT = pltpu.SideEffectType.DATAFLOW_SIDE_EFFECTING


def _hbm(a):
    return pltpu.with_memory_space_constraint(a, pltpu.HBM)


_ALL_PEERS = tuple(range(1, N_DEV))
_CHIP_PEERS = (1, 2, 4, 6)
_OTHER_CHIPS = (2, 4, 6)


def _spec_peers(mode):
    return {"ici": _CHIP_PEERS, "fwd": _OTHER_CHIPS}.get(mode, _ALL_PEERS)


def _split_descriptors(copies, srcs, lands, send_sems, recv_sems):
    x, y, c = lax.axis_index("x"), lax.axis_index("y"), lax.axis_index("c")
    me = 4 * x + 2 * y + c
    out, sem = [], 0
    for si, mode, li, ll in copies:
        for k in _spec_peers(mode):
            px = 1 - x if k & 4 else x
            py = 1 - y if k & 2 else y
            pc = 1 - c if k & 1 else c
            peer = 4 * px + 2 * py + pc
            if mode == "fwd":
                src = dst = lands[li].at[peer]
                target = (x, y, 1 - c)
            else:
                src = srcs[si].at[peer] if mode is True else srcs[si]
                dst = lands[li].at[me] if ll is None else lands[li].at[me, ll]
                target = (px, py, pc)
            out.append(pltpu.make_async_remote_copy(
                src_ref=src, dst_ref=dst, send_sem=send_sems.at[sem], recv_sem=recv_sems.at[sem],
                device_id=target, device_id_type=pl.DeviceIdType.MESH))
            sem += 1
    return out


def _n_copies(copies):
    return sum(len(_spec_peers(mode)) for _, mode, _, _ in copies)


def _xchg_start(name, groups):
    n_src = [len(g[0]) for g in groups]
    n_land = [len(g[1]) for g in groups]
    srcs = [s for g in groups for s in g[0]]
    lands = [l for g in groups for l in g[1]]
    ns, nl, ng = len(srcs), len(lands), len(groups)

    def body(*refs):
        src_refs, land_refs = refs[:ns], refs[ns:ns + nl]
        sems = refs[ns + nl:ns + nl + 2 * ng]
        token = refs[-1]
        so = lo = 0
        for gi, (_, _, copies) in enumerate(groups):
            for d in _split_descriptors(copies, src_refs[so:so + n_src[gi]], land_refs[lo:lo + n_land[gi]],
                                        sems[2 * gi], sems[2 * gi + 1]):
                d.start()
            so += n_src[gi]
            lo += n_land[gi]
        token[...] = jnp.zeros(token.shape, F32)

    out_shape, out_specs = [], []
    for g in groups:
        n = _n_copies(g[2])
        out_shape += [pltpu.SemaphoreType.DMA((n,)), pltpu.SemaphoreType.DMA((n,))]
        out_specs += [_SEM, _SEM]
    out_shape += [pltpu.HBM(l.shape, l.dtype) for l in lands]
    out_specs += [_HBM] * nl
    out_shape.append(jax.ShapeDtypeStruct((8, 128), F32))
    out_specs.append(pl.BlockSpec(memory_space=pltpu.VMEM))
    outs = pl.pallas_call(
        body, name=name, in_specs=[_HBM] * (ns + nl), out_specs=out_specs, out_shape=out_shape,
        input_output_aliases={ns + i: 2 * ng + i for i in range(nl)},
        compiler_params=pltpu.CompilerParams(has_side_effects=_EFFECT),
    )(*[_hbm(a) for a in srcs + lands])
    res, lo = [], 2 * ng
    for gi in range(ng):
        res.append((outs[2 * gi], outs[2 * gi + 1], list(outs[lo:lo + n_land[gi]])))
        lo += n_land[gi]
    return res, outs[-1]


def _xchg_wait(name, group, started, after):
    srcs, _, copies = group
    send_sems, recv_sems, lands = started
    ns, nl = len(srcs), len(lands)
    after = list(after)

    def body(*refs):
        src_refs, land_refs = refs[:ns], refs[ns:ns + nl]
        send_ref, recv_ref = refs[ns + nl], refs[ns + nl + 1]
        for d in _split_descriptors(copies, src_refs, land_refs, send_ref, recv_ref):
            d.wait_send()
            d.wait_recv()

    outs = pl.pallas_call(
        body, name=name, in_specs=[_HBM] * (ns + nl) + [_SEM, _SEM] + [_ANY] * len(after),
        out_specs=[_HBM] * nl, out_shape=[pltpu.HBM(l.shape, l.dtype) for l in lands],
        input_output_aliases={ns + i: i for i in range(nl)},
        compiler_params=pltpu.CompilerParams(has_side_effects=_EFFECT),
    )(*[_hbm(a) for a in srcs], *lands, send_sems, recv_sems, *after)
    return list(outs)


def _landing(own, me):
    land = lax.empty((N_DEV,) + own.shape, own.dtype)
    return lax.dynamic_update_slice(land, own[None], (me,) + (0,) * own.ndim)


def _adamw(name, w, m, v, recv, row0=0, prev=None):
    cdim = w.shape[1]
    r = recv.shape[1]
    tr = r
    for cand in (512, 256, 128, 64, 32, 16, 8):
        if r % cand == 0 and r > cand:
            tr = cand
            break
    assert row0 % tr == 0
    blk0 = row0 // tr
    n_prev = 0 if prev is None else 4

    def body(w_ref, m_ref, v_ref, r_ref, *rest):
        g_ref, d_ref, mo_ref, vo_ref = rest[n_prev:]
        g = r_ref[0].astype(F32)
        for s in range(1, N_DEV):
            g = g + r_ref[s].astype(F32)
        mn = ADAM_B1 * m_ref[...] + (1.0 - ADAM_B1) * g
        vn = ADAM_B2 * v_ref[...] + (1.0 - ADAM_B2) * (g * g)
        m_hat = mn / (1.0 - ADAM_B1 ** ADAM_STEP)
        v_hat = vn / (1.0 - ADAM_B2 ** ADAM_STEP)
        g_ref[...] = g
        d_ref[...] = -ADAM_LR * (m_hat / (jnp.sqrt(v_hat) + ADAM_EPS) + ADAM_WD * w_ref[...])
        mo_ref[...] = mn
        vo_ref[...] = vn

    blk = pl.BlockSpec((tr, cdim), lambda i: (i + blk0, 0))
    return pl.pallas_call(
        body, name=name, grid=(r // tr,),
        in_specs=[blk, blk, blk, pl.BlockSpec((N_DEV, tr, cdim), lambda i: (0, i, 0))] + [_ANY] * n_prev,
        out_specs=[blk, blk, blk, blk],
        out_shape=[jax.ShapeDtypeStruct(w.shape, F32)] * 4,
        input_output_aliases={4 + i: i for i in range(n_prev)},
        compiler_params=_cp(1),
    )(w, m, v, recv, *(prev or []))


def _pack_rows(arrs, lead=()):
    n = len(lead)
    flat = jnp.concatenate([a.reshape(a.shape[:n] + (-1,)) for a in arrs], axis=-1)
    size = flat.shape[-1]
    padded = -(-size // PACK_QUANTUM) * PACK_QUANTUM
    flat = jnp.pad(flat, [(0, 0)] * n + [(0, padded - size)])
    return flat.reshape(flat.shape[:n] + (padded // 128, 128))


def _unpack_rows(packed, shapes, lead=()):
    n = len(lead)
    flat = packed.reshape(packed.shape[:n] + (-1,))
    out, off = [], 0
    for s in shapes:
        size = int(np.prod(s))
        out.append(flat[..., off:off + size].reshape(packed.shape[:n] + tuple(s)))
        off += size
    return out


def _block_diag(w):
    eye = jnp.eye(LRU_HEADS, dtype=w.dtype)
    return (eye[:, None, :, None] * w[:, :, None, :]).reshape(LRU_W, LRU_W)


def _diag_blocks(dense):
    t = dense.reshape(LRU_HEADS, 64, LRU_HEADS, 64)
    eye = jnp.eye(LRU_HEADS, dtype=dense.dtype)
    return jnp.sum(t * eye[:, None, :, None], axis=2).reshape(LRU_HEADS * 64, 64)


_W512_NAMES = ("conv_dw_b", "conv_ln_g", "conv_ln_b", "conv_pw_b", "lru_conv_b", "lru_ba", "lru_bx", "lru_lambda")
_W512_ROWS = 12


def _pack_small(d):
    sinks = jnp.pad(d["attn_sinks"], ((0, 0), (0, 512 - N_HEADS)))
    t = jnp.stack([d[n] for n in _W512_NAMES] + [sinks], axis=1)
    w512 = jnp.pad(t, ((0, 0), (0, _W512_ROWS - t.shape[1]), (0, 0))).reshape(DEPTH * _W512_ROWS, 512)
    w2048 = jnp.concatenate([d["ln_in_g"][None], d["ln_in_b"][None], d["ln_post_g"], d["ln_post_b"],
                             jnp.zeros((2, D_MODEL), F32)], axis=0)
    w64 = jnp.concatenate([d["lru_wa"].reshape(-1, 64), d["lru_wx"].reshape(-1, 64)], axis=0)
    return [w512, w2048, w64.reshape(-1, 128)]


def _unpack_small(w512=None, w2048=None, w64=None):
    out = {}
    if w512 is not None:
        t = w512.reshape(DEPTH, _W512_ROWS, 512)
        out.update({n: t[:, i, :] for i, n in enumerate(_W512_NAMES)})
        out["attn_sinks"] = t[:, len(_W512_NAMES), :N_HEADS]
    if w2048 is not None:
        out["ln_in_g"], out["ln_in_b"] = w2048[0], w2048[1]
        out["ln_post_g"], out["ln_post_b"] = w2048[2:4], w2048[4:6]
    if w64 is not None:
        w64 = w64.reshape(-1, 64)
        half = w64.shape[0] // 2
        out["lru_wa"] = w64[:half].reshape(DEPTH, LRU_HEADS, 64, 64)
        out["lru_wx"] = w64[half:].reshape(DEPTH, LRU_HEADS, 64, 64)
    return out


def _cols_to_slots(full):
    lead = full.shape[:-1]
    t = full.reshape(lead + (N_DEV, full.shape[-1] // N_DEV))
    return jnp.moveaxis(t, -2, 0)


def _slots_to_cols(slots):
    t = jnp.moveaxis(slots, 0, -2)
    return t.reshape(t.shape[:-2] + (t.shape[-2] * t.shape[-1],))


def kernel(x, meta_tokens, ln_in_g, ln_in_b, w_in, conv_dw_w, conv_dw_b, conv_ln_g, conv_ln_b, conv_pw_w, conv_pw_b, attn_sinks, lru_conv_w, lru_conv_b, lru_wa, lru_ba, lru_wx, lru_bx, lru_lambda, w_out, ln_post_g, ln_post_b, loss_target, m_meta_tokens, m_ln_in_g, m_ln_in_b, m_w_in, m_conv_dw_w, m_conv_dw_b, m_conv_ln_g, m_conv_ln_b, m_conv_pw_w, m_conv_pw_b, m_attn_sinks, m_lru_conv_w, m_lru_conv_b, m_lru_wa, m_lru_ba, m_lru_wx, m_lru_bx, m_lru_lambda, m_w_out, m_ln_post_g, m_ln_post_b, v_meta_tokens, v_ln_in_g, v_ln_in_b, v_w_in, v_conv_dw_w, v_conv_dw_b, v_conv_ln_g, v_conv_ln_b, v_conv_pw_w, v_conv_pw_b, v_attn_sinks, v_lru_conv_w, v_lru_conv_b, v_lru_wa, v_lru_ba, v_lru_wx, v_lru_bx, v_lru_lambda, v_w_out, v_ln_post_g, v_ln_post_b):
    seq = x.shape[1]
    lp = seq + TB
    row = lambda a: a.reshape(1, -1)
    shard_small_names = ["conv_dw_w", "lru_conv_w", "meta_tokens"]
    weights = dict(meta_tokens=meta_tokens, ln_in_g=ln_in_g, ln_in_b=ln_in_b, w_in=w_in, conv_dw_w=conv_dw_w,
                   conv_dw_b=conv_dw_b, conv_ln_g=conv_ln_g, conv_ln_b=conv_ln_b, conv_pw_w=conv_pw_w,
                   conv_pw_b=conv_pw_b, attn_sinks=attn_sinks, lru_conv_w=lru_conv_w, lru_conv_b=lru_conv_b,
                   lru_wa=lru_wa, lru_ba=lru_ba, lru_wx=lru_wx, lru_bx=lru_bx, lru_lambda=lru_lambda,
                   w_out=w_out, ln_post_g=ln_post_g, ln_post_b=ln_post_b)
    mom1 = dict(meta_tokens=m_meta_tokens, ln_in_g=m_ln_in_g, ln_in_b=m_ln_in_b, w_in=m_w_in, conv_dw_w=m_conv_dw_w,
                conv_dw_b=m_conv_dw_b, conv_ln_g=m_conv_ln_g, conv_ln_b=m_conv_ln_b, conv_pw_w=m_conv_pw_w,
                conv_pw_b=m_conv_pw_b, attn_sinks=m_attn_sinks, lru_conv_w=m_lru_conv_w, lru_conv_b=m_lru_conv_b,
                lru_wa=m_lru_wa, lru_ba=m_lru_ba, lru_wx=m_lru_wx, lru_bx=m_lru_bx, lru_lambda=m_lru_lambda,
                w_out=m_w_out, ln_post_g=m_ln_post_g, ln_post_b=m_ln_post_b)
    mom2 = dict(meta_tokens=v_meta_tokens, ln_in_g=v_ln_in_g, ln_in_b=v_ln_in_b, w_in=v_w_in, conv_dw_w=v_conv_dw_w,
                conv_dw_b=v_conv_dw_b, conv_ln_g=v_conv_ln_g, conv_ln_b=v_conv_ln_b, conv_pw_w=v_conv_pw_w,
                conv_pw_b=v_conv_pw_b, attn_sinks=v_attn_sinks, lru_conv_w=v_lru_conv_w, lru_conv_b=v_lru_conv_b,
                lru_wa=v_lru_wa, lru_ba=v_lru_ba, lru_wx=v_lru_wx, lru_bx=v_lru_bx, lru_lambda=v_lru_lambda,
                w_out=v_w_out, ln_post_g=v_ln_post_g, ln_post_b=v_ln_post_b)
    shard_wmv = [_pack_rows([d[n] for n in shard_small_names]) for d in (weights, mom1, mom2)]
    rep_wmv = [_pack_small(d) for d in (weights, mom1, mom2)]
    gate_w = [(_block_diag(lru_wa[l]).astype(BF16), _block_diag(lru_wx[l]).astype(BF16)) for l in range(DEPTH)]
    tabs = _rope_tables(lp)
    prepared = (shard_wmv + [a for wmv in rep_wmv for a in wmv]
                + [w for pair in gate_w for w in pair] + list(tabs))

    small_shard_shapes = [conv_dw_w.shape, lru_conv_w.shape, meta_tokens.shape]
    small_shard = _pack_rows([conv_dw_w, lru_conv_w, meta_tokens])
    me = 4 * lax.axis_index("x") + 2 * lax.axis_index("y") + lax.axis_index("c")
    w_in_b = [w_in[l].astype(BF16) for l in range(DEPTH)]
    w_out_b = [w_out[l].astype(BF16) for l in range(DEPTH)]
    pw_b = conv_pw_w.astype(BF16)
    wgroups = [
        ([small_shard], [_landing(small_shard, me)], [(0, False, 0, None)]),
        ([w_in_b[0]], [_landing(w_in_b[0], me)], [(0, "ici", 0, None)]),
        ([pw_b, w_out_b[0]], [_landing(pw_b, me), _landing(w_out_b[0], me)],
         [(0, False, 0, None), (1, False, 1, None)]),
        ([w_in_b[1], w_out_b[1]], [_landing(w_in_b[1], me), _landing(w_out_b[1], me)],
         [(0, "ici", 0, None), (1, "ici", 1, None)]),
    ]
    wstarted, wtoken = _xchg_start("weights_start", wgroups)

    def pass_on(tag, parts):
        fwd = ([], list(parts), [(None, "fwd", i, None) for i in range(len(parts))])
        fstarted, ftoken = _xchg_start(f"weights_fwd_start_{tag}", [fwd])
        return (fwd, fstarted[0]), ftoken
    wg_small, = _xchg_wait("weights_wait_s", wgroups[0], wstarted[0], [wtoken])
    g_dw, g_lc, g_meta = _unpack_rows(wg_small, small_shard_shapes, lead=(N_DEV,))
    conv_dw_full = _slots_to_cols(g_dw)
    lru_conv_full = _slots_to_cols(g_lc)
    meta_full = _slots_to_cols(g_meta)
    wg_in = [None, None]
    wg_out = [None, None]
    wg_pw = None

    ln_g = [ln_in_g, ln_post_g[0], ln_post_g[1]]
    ln_b = [ln_in_b, ln_post_b[0], ln_post_b[1]]

    def layer_params(l):
        wdw = jnp.pad(conv_dw_full[l], ((0, 1), (0, 0)))
        cvec = jnp.pad(jnp.stack([conv_dw_b[l], conv_ln_g[l], conv_ln_b[l], conv_pw_b[l]]), ((0, 4), (0, 0)))
        wpw = wg_pw[:, l].reshape(CONV_W, CONV_W)
        sinks = jnp.pad(attn_sinks[l].reshape(1, N_HEADS), ((0, 7), (0, 128 - N_HEADS)))
        wl = jnp.pad(lru_conv_full[l], ((0, 4), (0, 0)))
        lvec = jnp.pad(jnp.stack([lru_conv_b[l], lru_ba[l], lru_bx[l], lru_lambda[l]]), ((0, 4), (0, 0)))
        wa, wx = gate_w[l]
        wo = wg_out[l].reshape(D_MODEL, D_MODEL)
        wout = jnp.concatenate([wo[512:1536], wo[0:512], wo[1536:]], axis=0)
        return dict(wdw=wdw, cvec=cvec, wpw=wpw, sinks=sinks, wl=wl, lvec=lvec, wa=wa, wx=wx, wout=wout)

    params = [None] * DEPTH

    z0, hb = _embed(x, meta_full, row(ln_g[0]), row(ln_b[0]))
    z = [z0]
    saved = []
    for l in range(DEPTH):
        if l == 0:
            parts = _xchg_wait("weights_wait_a", wgroups[1], wstarted[1], [hb] + prepared)
            pending, ftoken = pass_on("a", parts)
            wg_in[0], = _xchg_wait("weights_fwd_wait_a", *pending, [ftoken])
        else:
            wg_in[1], wg_out[1] = _xchg_wait("weights_fwd_wait_c", *pending_c, [hb])
        proj = _mm_proj(f"proj{l}", hb, wg_in[l])
        if l == 0:
            wg_pw, wg_out[0] = _xchg_wait("weights_wait_b", wgroups[2], wstarted[2], [proj])
        p = params[l] = layer_params(l)
        ycat, c1 = _conv_fwd(f"conv_fwd{l}", proj, p["wdw"], p["cvec"], p["wpw"])
        qr, kr = _rope_fwd(f"rope{l}", proj, tabs)
        ycat = _attn_fwd(f"attn_fwd{l}", qr, kr, proj, p["sinks"], ycat)
        ycat, hstate = _lru_fwd(f"lru_fwd{l}", proj, p["wl"], p["lvec"], p["wa"], p["wx"], ycat)
        if l == 0:
            pending_c, ftoken = pass_on("c", _xchg_wait("weights_wait_c", wgroups[3], wstarted[3], [ycat]))
        saved.append(dict(hb=hb, proj=proj, ycat=ycat, qr=qr, kr=kr, hstate=hstate, c1=c1))
        last = l == DEPTH - 1
        z_next, hb = _mm_out(f"out{l}", ycat, p["wout"], z[l], row(ln_g[l]), row(ln_b[l]),
                             None if last else row(ln_g[l + 1]), None if last else row(ln_b[l + 1]), ftoken)
        z.append(z_next)

    dz, st_post1, loss_blk = _loss_head(z[DEPTH], loss_target, row(ln_g[DEPTH]), row(ln_b[DEPTH]))

    ln_stats = {DEPTH: st_post1}
    g_layers = [None] * DEPTH
    dwin_l, dwout_l = [None] * DEPTH, [None] * DEPTH
    grad_x = gmeta = None
    token = wtoken
    ggroups = [None] * DEPTH
    own = lambda a: lax.dynamic_index_in_dim(a, me, 0, keepdims=False)
    for l in reversed(range(DEPTH)):
        p, s = params[l], saved[l]
        dycat = _mm_dycat(f"dycat{l}", dz, p["wout"], token)
        dwout_l[l] = _mm_dwout(f"dwout{l}", s["ycat"], dz)
        dproj, dwdw, dcvec, dwpw = _conv_bwd(f"conv_bwd{l}", s["proj"], dycat, s["c1"], p["wdw"], p["cvec"], p["wpw"])
        dwo = jnp.concatenate([dwout_l[l][1024:1536], dwout_l[l][0:1024], dwout_l[l][1536:]], axis=0)
        dwo = dwo.reshape(N_DEV, D_MODEL // N_DEV, D_MODEL)
        dpw = dwpw.reshape(N_DEV, CONV_W // N_DEV, CONV_W)
        early = ([dwo, dpw], [_landing(own(dwo), me), _landing(own(dpw), me)],
                 [(0, True, 0, None), (1, True, 1, None)])
        started_early, token = _xchg_start(f"grads_start_out{l}", [early])
        dq, dgate, dk, dv, dsink = _attn_bwd(f"attn_bwd{l}", s["qr"], s["kr"], s["proj"], p["sinks"], dycat, token)
        dproj = _attn_assemble(f"attn_asm{l}", dq, dgate, dk, dv, tabs, dproj)
        dproj, dwl, dlvec, dwa, dwx = _lru_bwd(f"lru_bwd{l}", s["proj"], dycat, s["hstate"],
                                                p["wl"], p["lvec"], p["wa"], p["wx"], dproj)
        g512 = jnp.concatenate([dcvec[0:4], dlvec[0:4], jnp.pad(dsink[0:1], ((0, 0), (0, 512 - 128))),
                                jnp.zeros((_W512_ROWS - 9, 512), F32)], axis=0)
        g_layers[l] = dict(dwdw=dwdw[:CONV_K], dwl=dwl[:LRU_CONV_K], g512=g512,
                           dwa=_diag_blocks(dwa), dwx=_diag_blocks(dwx))
        if l == 0:
            g512 = jnp.concatenate([g_layers[i]["g512"] for i in range(DEPTH)], axis=0)
            g64 = jnp.concatenate([g_layers[i][k] for k in ("dwa", "dwx") for i in range(DEPTH)], axis=0)
            g64 = g64.reshape(-1, 128)
            vgroup = ([g512, g64], [_landing(g512, me), _landing(g64, me)],
                      [(0, False, 0, None), (1, False, 1, None)])
            vstarted, token = _xchg_start("vector_grads_start", [vgroup])
        dwin_l[l] = _mm_dwin(f"dwin{l}", s["hb"], dproj, token)
        late = ([dwin_l[l]], [_landing(own(dwin_l[l]), me)], [(0, True, 0, None)])
        started_late, token = _xchg_start(f"grads_start_in{l}", [late])
        ggroups[l] = [(late, started_late[0]), (early, started_early[0])]
        dh = _mm_dh(f"dh{l}", dproj, wg_in[l], dz, token)
        if l > 0:
            dz, ln_stats[l] = _ln_bwd(f"ln_bwd{l}", dh, z[l], row(ln_g[l]))
        else:
            grad_x, gmeta, ln_stats[0] = _ln_bwd_input(dh, z[0], row(ln_g[0]))

    loss_row = jnp.pad(loss_blk[0:1, :], ((0, 0), (0, D_MODEL - 128)))
    g2048 = jnp.concatenate([ln_stats[0][0:2], ln_stats[1][0:1], ln_stats[2][0:1], ln_stats[1][1:2],
                             ln_stats[2][1:2], loss_row, jnp.zeros((1, D_MODEL), F32)], axis=0)
    g_dw_full = jnp.stack([g_layers[l]["dwdw"] for l in range(DEPTH)])
    g_lc_full = jnp.stack([g_layers[l]["dwl"] for l in range(DEPTH)])
    shard_pack = _pack_rows([_cols_to_slots(g_dw_full), _cols_to_slots(g_lc_full), _cols_to_slots(gmeta)],
                            lead=(N_DEV,))
    sgroup = ([shard_pack, g2048], [_landing(own(shard_pack), me), _landing(g2048, me)],
              [(0, True, 0, None), (1, False, 1, None)])
    sstarted, token = _xchg_start("small_grads_start", [sgroup])

    res = {}

    def flat2(a, cols):
        return a.reshape(-1, cols)

    big = (("w_in", 0, W_IN_SHARD), ("w_out", 1, D_MODEL), ("conv_pw_w", 2, CONV_W))
    prev = {n: None for n, _, _ in big}
    def update(name_, cols, recv, l):
        w_ = weights[name_]
        prev[name_] = _adamw(f"adamw_{name_}{l}", flat2(w_, cols), flat2(mom1[name_], cols),
                             flat2(mom2[name_], cols), recv, row0=l * w_.shape[1], prev=prev[name_])

    def small_update(tag, ci, recv):
        return _adamw(f"adamw_small_w{tag}", rep_wmv[0][ci], rep_wmv[1][ci], rep_wmv[2][ci], recv)

    def keep(unpacked, k):
        for n, a in unpacked.items():
            res.setdefault(n, [None] * 4)[k] = a

    r_512, r_64 = _xchg_wait("vector_grads_wait", vgroup, vstarted[0], [token])
    o512, o64 = small_update("512", 0, r_512), small_update("64", 2, r_64)
    early_done = []
    for k in range(4):
        unpacked = _unpack_small(w512=o512[k], w64=o64[k])
        keep(unpacked, k)
        early_done += list(unpacked.values())
    after = [token]
    for l in reversed(range(DEPTH)):
        late, early = ggroups[l]
        r_out, r_pw = _xchg_wait(f"grads_wait{l}_1", early[0], early[1], after)
        if l > 0:
            r_in, = _xchg_wait(f"grads_wait{l}_0", late[0], late[1], after)
            update("w_in", W_IN_SHARD, r_in, l)
        update("w_out", D_MODEL, r_out, l)
        update("conv_pw_w", CONV_W, r_pw, l)
        after = [prev["w_out"][0], prev["conv_pw_w"][0], prev["w_in"][0]]
    late = ggroups[0][0]
    r_in, = _xchg_wait("grads_wait0_0", late[0], late[1], after + early_done)
    update("w_in", W_IN_SHARD, r_in, 0)
    for name_, _, _ in big:
        res[name_] = [o.reshape(weights[name_].shape) for o in prev[name_]]

    r_small, r_2048 = _xchg_wait("small_grads_wait", sgroup, sstarted[0], [prev[n][0] for n, _, _ in big])
    sshapes = [weights[n].shape for n in shard_small_names]
    outs = _adamw("adamw_small_sharded", *shard_wmv, r_small)
    for k, o in enumerate(outs):
        for n, a in zip(shard_small_names, _unpack_rows(o, sshapes)):
            res.setdefault(n, [None] * 4)[k] = a
    o2048 = small_update("2048", 1, r_2048)
    for k in range(4):
        keep(_unpack_small(w2048=o2048[k]), k)
    loss = o2048[0][6, 0]

    order = ["meta_tokens", "ln_in_g", "ln_in_b", "w_in", "conv_dw_w", "conv_dw_b", "conv_ln_g", "conv_ln_b",
             "conv_pw_w", "conv_pw_b", "attn_sinks", "lru_conv_w", "lru_conv_b", "lru_wa", "lru_ba", "lru_wx",
             "lru_bx", "lru_lambda", "w_out", "ln_post_g", "ln_post_b"]
    return (loss, grad_x,
            *[res[n][0] for n in order], *[res[n][1] for n in order],
            *[res[n][2] for n in order], *[res[n][3] for n in order])
```

```python
import numpy as np
import jax
import jax.numpy as jnp
from jax import lax
from jax.experimental import pallas as pl
from jax.experimental.pallas import tpu as pltpu

F32 = jnp.float32
BF16 = jnp.bfloat16

D_MODEL = 2048
DEPTH = 2
N_META = 16
TB = 128
PAD0 = TB - N_META
CONV_W = 512
CONV_K = 31
HEAD_DIM = 64
N_HEADS = 16
N_KV = 4
GROUP = 4
ATT_W = 1024
KV_W = 256
ROT_DIM = 16
ROPE_THETA = 500000.0
LRU_W = 512
LRU_HEADS = 8
LRU_CONV_K = 4
LRU_C = 8.0
IN_TOTAL = 5120
N_DEV = 8
W_IN_SHARD = IN_TOTAL // N_DEV
LN_EPS = 1e-5
ALPHA = (2.0 * DEPTH) ** 0.25
NEG_INF = -1e30
ATT_SCALE = HEAD_DIM ** -0.5

ADAM_LR = 0.001
ADAM_B1 = 0.9
ADAM_B2 = 0.999
ADAM_EPS = 1e-08
ADAM_WD = 0.01
ADAM_STEP = 10

VMEM_LIMIT = 56 * 1024 * 1024
PACK_QUANTUM = 256 * 128

COL_CV, COL_CG, COL_CGATE = 0, 1, 2
COL_Q0 = 3
COL_K256 = 10
COL_V256 = 11
COL_AGATE1024 = 3
COL_RX, COL_RGATE = 8, 9
YC_CONV, YC_LRU = 2, 3


def _cp(n_axes, vmem=VMEM_LIMIT):
    return pltpu.CompilerParams(dimension_semantics=("arbitrary",) * n_axes, vmem_limit_bytes=vmem)


def _row_tile(lp, max_blocks):
    nb = lp // TB
    d = max(k for k in range(1, max_blocks + 1) if nb % k == 0)
    return TB * d


def _sig(x):
    return jax.nn.sigmoid(x)


def _dsilu(x, s):
    return s * (1.0 + x * (1.0 - s))


def _ln_core(z):
    mu = jnp.mean(z, axis=-1, keepdims=True)
    zc = z - mu
    var = jnp.mean(zc * zc, axis=-1, keepdims=True)
    rstd = lax.rsqrt(var + LN_EPS)
    return zc * rstd, rstd


def _ln_bwd_core(dy, xh, rstd, g):
    dxh = dy * g
    m1 = jnp.mean(dxh, axis=-1, keepdims=True)
    m2 = jnp.mean(dxh * xh, axis=-1, keepdims=True)
    return rstd * (dxh - m1 - xh * m2)


def _row_ids(shape, base):
    return lax.broadcasted_iota(jnp.int32, shape, 0) + base


def _colsum(x):
    return jnp.sum(x, axis=0, keepdims=True)


def _dot(a, b, dims):
    return lax.dot_general(a, b, (dims, ((), ())), preferred_element_type=F32)


NN = ((1,), (0,))
NT = ((1,), (1,))
TN = ((0,), (0,))


def _embed(x, meta_full, g, b):
    s = x.shape[1]
    lp = s + TB
    nb = lp // TB

    def body(x_ref, m_ref, g_ref, b_ref, o_ref, hb_ref):
        i = pl.program_id(0)

        @pl.when(i == 0)
        def _():
            o_ref[0:PAD0, :] = jnp.zeros((PAD0, D_MODEL), F32)
            o_ref[PAD0:TB, :] = m_ref[...]

        @pl.when(i > 0)
        def _():
            o_ref[...] = x_ref[...]

        xh, _ = _ln_core(o_ref[...])
        h = xh * g_ref[...] + b_ref[...]
        rows = _row_ids(h.shape, i * TB)
        hb_ref[...] = jnp.where(rows >= PAD0, h, 0.0).astype(BF16)

    return pl.pallas_call(
        body, name="embed", grid=(nb,),
        in_specs=[pl.BlockSpec((None, TB, D_MODEL), lambda i: (0, jnp.maximum(i - 1, 0), 0)),
                  pl.BlockSpec((N_META, D_MODEL), lambda i: (0, 0)),
                  pl.BlockSpec((1, D_MODEL), lambda i: (0, 0)),
                  pl.BlockSpec((1, D_MODEL), lambda i: (0, 0))],
        out_specs=[pl.BlockSpec((TB, D_MODEL), lambda i: (i, 0)),
                   pl.BlockSpec((TB, D_MODEL), lambda i: (i, 0))],
        out_shape=[jax.ShapeDtypeStruct((lp, D_MODEL), F32),
                   jax.ShapeDtypeStruct((lp, D_MODEL), BF16)],
        compiler_params=_cp(1),
    )(x, meta_full, g, b)


def _loss_head(z, target, g, b):
    lp = z.shape[0]
    nb = lp // TB

    def body(z_ref, t_ref, g_ref, b_ref, dz_ref, st_ref, loss_ref):
        i = pl.program_id(0)

        @pl.when(i == 0)
        def _():
            st_ref[...] = jnp.zeros(st_ref.shape, F32)
            loss_ref[...] = jnp.zeros(loss_ref.shape, F32)
            dz_ref[...] = jnp.zeros(dz_ref.shape, F32)

        @pl.when(i > 0)
        def _():
            xh, rstd = _ln_core(z_ref[...])
            gg = g_ref[...]
            y = xh * gg + b_ref[...]
            e = y - t_ref[...]
            part = 0.5 * jnp.sum(jnp.mean(e * e, axis=-1, keepdims=True), axis=0, keepdims=True)
            loss_ref[...] += jnp.broadcast_to(part, loss_ref.shape)
            dy = e / float(D_MODEL)
            st_ref[0:1, :] += _colsum(dy * xh)
            st_ref[1:2, :] += _colsum(dy)
            dz_ref[...] = _ln_bwd_core(dy, xh, rstd, gg)

    return pl.pallas_call(
        body, name="loss_head", grid=(nb,),
        in_specs=[pl.BlockSpec((TB, D_MODEL), lambda i: (i, 0)),
                  pl.BlockSpec((None, TB, D_MODEL), lambda i: (0, jnp.maximum(i - 1, 0), 0)),
                  pl.BlockSpec((1, D_MODEL), lambda i: (0, 0)),
                  pl.BlockSpec((1, D_MODEL), lambda i: (0, 0))],
        out_specs=[pl.BlockSpec((TB, D_MODEL), lambda i: (i, 0)),
                   pl.BlockSpec((8, D_MODEL), lambda i: (0, 0)),
                   pl.BlockSpec((8, 128), lambda i: (0, 0))],
        out_shape=[jax.ShapeDtypeStruct((lp, D_MODEL), F32),
                   jax.ShapeDtypeStruct((8, D_MODEL), F32),
                   jax.ShapeDtypeStruct((8, 128), F32)],
        compiler_params=_cp(1),
    )(z, target, g, b)


def _ln_bwd(name, dh, z, g):
    lp = z.shape[0]
    tr = _row_tile(lp, 3)

    def body(dh_ref, z_ref, g_ref, dz_ref, st_ref):
        i = pl.program_id(0)

        @pl.when(i == 0)
        def _():
            st_ref[...] = jnp.zeros(st_ref.shape, F32)

        xh, rstd = _ln_core(z_ref[...])
        rows = _row_ids(xh.shape, i * tr)
        dy = jnp.where(rows >= PAD0, dh_ref[...], 0.0)
        st_ref[0:1, :] += _colsum(dy * xh)
        st_ref[1:2, :] += _colsum(dy)
        dz_ref[...] = _ln_bwd_core(dy, xh, rstd, g_ref[...])

    return pl.pallas_call(
        body, name=name, grid=(lp // tr,),
        in_specs=[pl.BlockSpec((tr, D_MODEL), lambda i: (i, 0)),
                  pl.BlockSpec((tr, D_MODEL), lambda i: (i, 0)),
                  pl.BlockSpec((1, D_MODEL), lambda i: (0, 0))],
        out_specs=[pl.BlockSpec((tr, D_MODEL), lambda i: (i, 0)),
                   pl.BlockSpec((8, D_MODEL), lambda i: (0, 0))],
        out_shape=[jax.ShapeDtypeStruct((lp, D_MODEL), F32),
                   jax.ShapeDtypeStruct((8, D_MODEL), F32)],
        compiler_params=_cp(1),
    )(dh, z, g)


def _ln_bwd_input(dh, z, g):
    lp = z.shape[0]
    nb = lp // TB
    s = lp - TB

    def body(dh_ref, z_ref, g_ref, gx_ref, gm_ref, st_ref):
        i = pl.program_id(0)

        @pl.when(i == 0)
        def _():
            st_ref[...] = jnp.zeros(st_ref.shape, F32)

        xh, rstd = _ln_core(z_ref[...])
        rows = _row_ids(xh.shape, i * TB)
        dy = jnp.where(rows >= PAD0, dh_ref[...], 0.0)
        st_ref[0:1, :] += _colsum(dy * xh)
        st_ref[1:2, :] += _colsum(dy)
        dz = _ln_bwd_core(dy, xh, rstd, g_ref[...])
        gx_ref[...] = dz

        @pl.when(i == 0)
        def _():
            gm_ref[...] = dz[PAD0:TB, :]

    return pl.pallas_call(
        body, name="ln_in_bwd", grid=(nb,),
        in_specs=[pl.BlockSpec((TB, D_MODEL), lambda i: (i, 0)),
                  pl.BlockSpec((TB, D_MODEL), lambda i: (i, 0)),
                  pl.BlockSpec((1, D_MODEL), lambda i: (0, 0))],
        out_specs=[pl.BlockSpec((None, TB, D_MODEL), lambda i: (0, jnp.maximum(i - 1, 0), 0)),
                   pl.BlockSpec((N_META, D_MODEL), lambda i: (0, 0)),
                   pl.BlockSpec((8, D_MODEL), lambda i: (0, 0))],
        out_shape=[jax.ShapeDtypeStruct((1, s, D_MODEL), F32),
                   jax.ShapeDtypeStruct((N_META, D_MODEL), F32),
                   jax.ShapeDtypeStruct((8, D_MODEL), F32)],
        compiler_params=_cp(1),
    )(dh, z, g)


def _mm_proj(name, hb, wg_in):
    lp = hb.shape[0]
    tm = lp // 3

    def body(a_ref, b_ref, o_ref):
        b = jnp.concatenate([b_ref[0], b_ref[1]], axis=1)
        o_ref[...] = _dot(a_ref[...], b, NN)

    return pl.pallas_call(
        body, name=name, grid=(3, N_DEV // 2),
        in_specs=[pl.BlockSpec((tm, D_MODEL), lambda i, j: (i, 0)),
                  pl.BlockSpec((2, D_MODEL, W_IN_SHARD), lambda i, j: (j, 0, 0))],
        out_specs=pl.BlockSpec((tm, 2 * W_IN_SHARD), lambda i, j: (i, j)),
        out_shape=jax.ShapeDtypeStruct((lp, IN_TOTAL), F32),
        compiler_params=_cp(2),
    )(hb, wg_in)


def _mm_out(name, ycat, wout, z, g, b, g2, b2, dep):
    lp = ycat.shape[0]
    tm = lp // 6
    with_next = g2 is not None

    def body(a_ref, w_ref, z_ref, g_ref, b_ref, *rest):
        i = pl.program_id(0)
        xh, _ = _ln_core(z_ref[...])
        h = xh * g_ref[...] + b_ref[...]
        live = _row_ids(h.shape, i * tm) >= PAD0
        h = jnp.where(live, h, 0.0)
        zn = ALPHA * h + _dot(a_ref[...], w_ref[...], NN)
        if with_next:
            g2_ref, b2_ref, _, o_ref, hb_ref = rest
            xh2, _ = _ln_core(zn)
            hb_ref[...] = jnp.where(live, xh2 * g2_ref[...] + b2_ref[...], 0.0).astype(BF16)
        else:
            _, o_ref = rest
        o_ref[...] = zn

    vec = pl.BlockSpec((1, D_MODEL), lambda i: (0, 0))
    row_blk = pl.BlockSpec((tm, D_MODEL), lambda i: (i, 0))
    outs = pl.pallas_call(
        body, name=name, grid=(6,),
        in_specs=([row_blk, pl.BlockSpec((D_MODEL, D_MODEL), lambda i: (0, 0), pipeline_mode=pl.Buffered(1)),
                   row_blk, vec, vec] + ([vec, vec] if with_next else [])
                  + [pl.BlockSpec(memory_space=pl.ANY)]),
        out_specs=[row_blk, row_blk] if with_next else [row_blk],
        out_shape=([jax.ShapeDtypeStruct((lp, D_MODEL), F32)]
                   + ([jax.ShapeDtypeStruct((lp, D_MODEL), BF16)] if with_next else [])),
        compiler_params=_cp(1),
    )(ycat, wout, z, g, b, *((g2, b2) if with_next else ()), dep)
    return (outs[0], outs[1]) if with_next else (outs[0], None)


def _mm_dycat(name, dz, wout, dep):
    lp = dz.shape[0]
    tm = lp // 6

    def body(a_ref, w_ref, dep_ref, o_ref):
        del dep_ref
        o_ref[...] = _dot(a_ref[...].astype(BF16), w_ref[...], NT).astype(BF16)

    return pl.pallas_call(
        body, name=name, grid=(6,),
        in_specs=[pl.BlockSpec((tm, D_MODEL), lambda i: (i, 0)),
                  pl.BlockSpec((D_MODEL, D_MODEL), lambda i: (0, 0), pipeline_mode=pl.Buffered(1)),
                  pl.BlockSpec(memory_space=pl.ANY)],
        out_specs=pl.BlockSpec((tm, D_MODEL), lambda i: (i, 0)),
        out_shape=jax.ShapeDtypeStruct((lp, D_MODEL), BF16),
        compiler_params=_cp(1),
    )(dz, wout, dep)


def _mm_dwout(name, ycat, dz):
    lp = ycat.shape[0]
    tk = _row_tile(lp, 11)
    nk = lp // tk
    half = D_MODEL // 2

    def body(a_ref, b_ref, o_ref, acc_ref):
        k = pl.program_id(1)

        @pl.when(k == 0)
        def _():
            acc_ref[...] = jnp.zeros(acc_ref.shape, F32)

        acc_ref[...] += _dot(a_ref[...], b_ref[...].astype(BF16), TN)

        @pl.when(k == nk - 1)
        def _():
            o_ref[...] = acc_ref[...].astype(BF16)

    return pl.pallas_call(
        body, name=name, grid=(2, nk),
        in_specs=[pl.BlockSpec((tk, half), lambda h, k: (k, h)),
                  pl.BlockSpec((tk, D_MODEL), lambda h, k: (k, 0))],
        out_specs=pl.BlockSpec((half, D_MODEL), lambda h, k: (h, 0)),
        out_shape=jax.ShapeDtypeStruct((D_MODEL, D_MODEL), BF16),
        scratch_shapes=[pltpu.VMEM((half, D_MODEL), F32)],
        compiler_params=_cp(2),
    )(ycat, dz)


def _mm_dwin(name, hb, dproj, dep):
    lp = hb.shape[0]
    tk = _row_tile(lp, 11)
    nk = lp // tk

    def body(a_ref, b_ref, dep_ref, o_ref, acc_ref):
        del dep_ref
        k = pl.program_id(1)

        @pl.when(k == 0)
        def _():
            acc_ref[...] = jnp.zeros(acc_ref.shape, F32)

        acc_ref[...] += _dot(a_ref[...], b_ref[...], TN)

        @pl.when(k == nk - 1)
        def _():
            o_ref[0] = acc_ref[:, 0:W_IN_SHARD].astype(BF16)
            o_ref[1] = acc_ref[:, W_IN_SHARD:2 * W_IN_SHARD].astype(BF16)

    return pl.pallas_call(
        body, name=name, grid=(4, nk),
        in_specs=[pl.BlockSpec((tk, D_MODEL), lambda j, k: (k, 0)),
                  pl.BlockSpec((tk, 2 * W_IN_SHARD), lambda j, k: (k, j)),
                  pl.BlockSpec(memory_space=pl.ANY)],
        out_specs=pl.BlockSpec((2, D_MODEL, W_IN_SHARD), lambda j, k: (j, 0, 0)),
        out_shape=jax.ShapeDtypeStruct((N_DEV, D_MODEL, W_IN_SHARD), BF16),
        scratch_shapes=[pltpu.VMEM((D_MODEL, 2 * W_IN_SHARD), F32)],
        compiler_params=_cp(2),
    )(hb, dproj, dep)


def _mm_dh(name, dproj, wg_in, dz, dep):
    lp = dproj.shape[0]
    tm = lp // 6

    def body(a_ref, w_ref, dz_ref, dep_ref, o_ref, acc_ref):
        del dep_ref
        k = pl.program_id(1)

        @pl.when(k == 0)
        def _():
            acc_ref[...] = jnp.zeros(acc_ref.shape, F32)

        w = jnp.concatenate([w_ref[0], w_ref[1]], axis=1)
        acc_ref[...] += _dot(a_ref[...], w, NT)

        @pl.when(k == N_DEV // 2 - 1)
        def _():
            o_ref[...] = acc_ref[...] + ALPHA * dz_ref[...]

    return pl.pallas_call(
        body, name=name, grid=(6, N_DEV // 2),
        in_specs=[pl.BlockSpec((tm, 2 * W_IN_SHARD), lambda i, k: (i, k)),
                  pl.BlockSpec((2, D_MODEL, W_IN_SHARD), lambda i, k: (k, 0, 0)),
                  pl.BlockSpec((tm, D_MODEL), lambda i, k: (i, 0)),
                  pl.BlockSpec(memory_space=pl.ANY)],
        out_specs=pl.BlockSpec((tm, D_MODEL), lambda i, k: (i, 0)),
        out_shape=jax.ShapeDtypeStruct((lp, D_MODEL), F32),
        scratch_shapes=[pltpu.VMEM((tm, D_MODEL), F32)],
        compiler_params=_cp(2),
    )(dproj, wg_in, dz, dep)


SUB = 128


def _shift_plan(cat, n_shift, base):
    rolled = [cat] + [pltpu.roll(cat, b, axis=0) for b in range(1, 8)]
    return [(rolled[s % 8], base - 8 * (s // 8)) for s in range(n_shift)]


def _tap_sum(w_ref, plan, rows, init=None):
    blocks = []
    for r0 in range(0, rows, SUB):
        row = []
        for c0 in range(0, CONV_W, SUB):
            acc = (jnp.zeros((SUB, SUB), F32) if init is None
                   else jnp.broadcast_to(init[:, c0:c0 + SUB], (SUB, SUB)))
            for k, (arr, off) in enumerate(plan):
                acc = acc + w_ref[k:k + 1, c0:c0 + SUB] * arr[off + r0:off + r0 + SUB, c0:c0 + SUB]
            row.append(acc)
        blocks.append(jnp.concatenate(row, axis=1))
    return jnp.concatenate(blocks, axis=0)


def _tap_grads(dw_ref, dy, plan, rows):
    for c0 in range(0, CONV_W, SUB):
        dys = [dy[r0:r0 + SUB, c0:c0 + SUB] for r0 in range(0, rows, SUB)]
        for k, (arr, off) in enumerate(plan):
            part = None
            for ri, r0 in enumerate(range(0, rows, SUB)):
                prod = dys[ri] * arr[off + r0:off + r0 + SUB, c0:c0 + SUB]
                for i in range(SUB // 8):
                    piece = prod[8 * i:8 * i + 8, :]
                    part = piece if part is None else part + piece
            dw_ref[k:k + 1, c0:c0 + SUB] += jnp.sum(part, axis=0, keepdims=True)


CONV_HALO = 32


def _conv_chain(j, tb, cv_ref, cg_ref, cvp_ref, cgp_ref, wdw_ref, vec_ref, wpw_ref, c1_ref=None):
    cv = cv_ref[...]
    sg = _sig(cg_ref[...])
    c0 = cv * sg
    c0p = jnp.where(j > 0, cvp_ref[...] * _sig(cgp_ref[...]), 0.0)
    cat = jnp.concatenate([c0p, c0], axis=0)
    shifts = _shift_plan(cat, CONV_K, CONV_HALO)
    taps = [shifts[CONV_K - 1 - k] for k in range(CONV_K)]
    if c1_ref is None:
        c1 = _tap_sum(wdw_ref, taps, tb, init=vec_ref[0:1, :])
    else:
        c1 = c1_ref[...]
    xh, rstd = _ln_core(c1)
    c2 = xh * vec_ref[1:2, :] + vec_ref[2:3, :]
    s2 = _sig(c2)
    c3 = c2 * s2
    c4 = _dot(c3.astype(BF16), wpw_ref[...], NN) + vec_ref[3:4, :]
    return dict(cv=cv, sg=sg, taps=taps, c1=c1, xh=xh, rstd=rstd, c2=c2, s2=s2, c3=c3, c4=c4)


def _conv_in_specs(jmap, tb):
    def cur(col):
        return pl.BlockSpec((tb, 512), lambda n: (jmap(n), col))

    def prev(col):
        return pl.BlockSpec((CONV_HALO, 512),
                            lambda n: (jnp.maximum(jmap(n) * (tb // CONV_HALO) - 1, 0), col))

    return [cur(COL_CV), cur(COL_CG), prev(COL_CV), prev(COL_CG), cur(COL_CGATE)]


def _conv_param_specs():
    return [pl.BlockSpec((32, CONV_W), lambda n: (0, 0)),
            pl.BlockSpec((8, CONV_W), lambda n: (0, 0)),
            pl.BlockSpec((CONV_W, CONV_W), lambda n: (0, 0))]


def _conv_fwd(name, proj, wdw, vec, wpw):
    lp = proj.shape[0]
    tb = _row_tile(lp, 3)
    nb = lp // tb

    def body(cv_ref, cg_ref, cvp_ref, cgp_ref, gate_ref, wdw_ref, vec_ref, wpw_ref, o_ref, c1_ref):
        j = pl.program_id(0)
        c = _conv_chain(j, tb, cv_ref, cg_ref, cvp_ref, cgp_ref, wdw_ref, vec_ref, wpw_ref)
        gate = gate_ref[...]
        o_ref[...] = (c["c4"] * (gate * _sig(gate))).astype(BF16)
        c1_ref[...] = c["c1"]

    return pl.pallas_call(
        body, name=name, grid=(nb,),
        in_specs=_conv_in_specs(lambda n: n, tb) + _conv_param_specs(),
        out_specs=[pl.BlockSpec((tb, 512), lambda n: (n, YC_CONV)),
                   pl.BlockSpec((tb, CONV_W), lambda n: (n, 0))],
        out_shape=[jax.ShapeDtypeStruct((lp, D_MODEL), BF16),
                   jax.ShapeDtypeStruct((lp, CONV_W), F32)],
        compiler_params=_cp(1),
    )(proj, proj, proj, proj, proj, wdw, vec, wpw)


def _conv_bwd(name, proj, dycat, c1, wdw, vec, wpw):
    lp = proj.shape[0]
    tb = _row_tile(lp, 3)
    nb = lp // tb
    halo = CONV_HALO

    def body(cv_ref, cg_ref, cvp_ref, cgp_ref, gate_ref, dy_ref, c1_ref, wdw_ref, vec_ref, wpw_ref,
             dp_ref, dwdw_ref, dvec_ref, dwpw_ref, carry_ref):
        n = pl.program_id(0)
        j = nb - 1 - n

        @pl.when(n == 0)
        def _():
            carry_ref[...] = jnp.zeros(carry_ref.shape, F32)
            dwdw_ref[...] = jnp.zeros(dwdw_ref.shape, F32)
            dvec_ref[...] = jnp.zeros(dvec_ref.shape, F32)
            dwpw_ref[...] = jnp.zeros(dwpw_ref.shape, F32)

        c = _conv_chain(j, tb, cv_ref, cg_ref, cvp_ref, cgp_ref, wdw_ref, vec_ref, wpw_ref, c1_ref)
        dy = dy_ref[...].astype(F32)
        gate = gate_ref[...]
        sgate = _sig(gate)
        dc4 = dy * (gate * sgate)
        dgate = dy * c["c4"] * _dsilu(gate, sgate)
        dc4b = dc4.astype(BF16)
        dvec_ref[3:4, :] += _colsum(dc4)
        dwpw_ref[...] += _dot(c["c3"].astype(BF16), dc4b, TN)
        dc3 = _dot(dc4b, wpw_ref[...], NT)
        dc2 = dc3 * _dsilu(c["c2"], c["s2"])
        dvec_ref[1:2, :] += _colsum(dc2 * c["xh"])
        dvec_ref[2:3, :] += _colsum(dc2)
        dc1 = _ln_bwd_core(dc2, c["xh"], c["rstd"], vec_ref[1:2, :])
        dvec_ref[0:1, :] += _colsum(dc1)
        _tap_grads(dwdw_ref, dc1, c["taps"], tb)
        dcat = jnp.concatenate([dc1, carry_ref[...]], axis=0)
        total = tb + halo
        up = [dcat] + [pltpu.roll(dcat, total - b, axis=0) for b in range(1, 8)]
        ahead = [(up[(CONV_K - 1 - k) % 8], 8 * ((CONV_K - 1 - k) // 8)) for k in range(CONV_K)]
        dc0 = _tap_sum(wdw_ref, ahead, tb)
        carry_ref[...] = dc1[0:halo, :]
        sg = c["sg"]
        dcv = dc0 * sg
        dcg = dc0 * c["cv"] * sg * (1.0 - sg)
        dp_ref[:, 0:512] = dcv.astype(BF16)
        dp_ref[:, 512:1024] = dcg.astype(BF16)
        dp_ref[:, 1024:1536] = dgate.astype(BF16)

    jmap = lambda n: nb - 1 - n
    return pl.pallas_call(
        body, name=name, grid=(nb,),
        in_specs=(_conv_in_specs(jmap, tb)
                  + [pl.BlockSpec((tb, 512), lambda n: (jmap(n), YC_CONV)),
                     pl.BlockSpec((tb, CONV_W), lambda n: (jmap(n), 0))]
                  + _conv_param_specs()),
        out_specs=[pl.BlockSpec((tb, 1536), lambda n: (jmap(n), 0)),
                   pl.BlockSpec((32, CONV_W), lambda n: (0, 0)),
                   pl.BlockSpec((8, CONV_W), lambda n: (0, 0)),
                   pl.BlockSpec((CONV_W, CONV_W), lambda n: (0, 0))],
        out_shape=[jax.ShapeDtypeStruct((lp, IN_TOTAL), BF16),
                   jax.ShapeDtypeStruct((32, CONV_W), F32),
                   jax.ShapeDtypeStruct((8, CONV_W), F32),
                   jax.ShapeDtypeStruct((CONV_W, CONV_W), F32)],
        scratch_shapes=[pltpu.VMEM((halo, CONV_W), F32)],
        compiler_params=_cp(1),
    )(proj, proj, proj, proj, proj, dycat, c1, wdw, vec, wpw)


def _rope_tables(lp):
    half = ROT_DIM // 2
    inv_freq = ROPE_THETA ** (-jnp.arange(half, dtype=F32) / half)
    pos = (jnp.arange(lp, dtype=jnp.int32) - PAD0).astype(F32)
    ang = pos[:, None] * inv_freq[None, :]
    cos, sin = jnp.cos(ang), jnp.sin(ang)
    ones = jnp.ones((lp, HEAD_DIM - ROT_DIM), F32)
    zeros = jnp.zeros((lp, HEAD_DIM - ROT_DIM), F32)
    zh = jnp.zeros((lp, half), F32)
    c = jnp.concatenate([cos, cos, ones], axis=1)
    sa = jnp.concatenate([-sin, zh, zeros], axis=1)
    sb = jnp.concatenate([zh, sin, zeros], axis=1)
    tile = lambda t: jnp.tile(t, (1, KV_W // HEAD_DIM))
    return tile(c), tile(sa), tile(sb)


def _rot(x, c, sa, sb):
    w = x.shape[1]
    return x * c + pltpu.roll(x, w - 8, axis=1) * sa + pltpu.roll(x, 8, axis=1) * sb


def _rot_t(dy, c, sa, sb):
    w = dy.shape[1]
    return dy * c + pltpu.roll(dy * sa, 8, axis=1) + pltpu.roll(dy * sb, w - 8, axis=1)


def _rope_fwd(name, proj, tabs):
    lp = proj.shape[0]
    tr = _row_tile(lp, 11)

    def body(q0_ref, q1_ref, k_ref, c_ref, sa_ref, sb_ref, qr_ref, kr_ref):
        c, sa, sb = c_ref[...], sa_ref[...], sb_ref[...]
        c2 = jnp.concatenate([c, c], axis=1)
        sa2 = jnp.concatenate([sa, sa], axis=1)
        sb2 = jnp.concatenate([sb, sb], axis=1)
        qr_ref[:, 0:512] = (_rot(q0_ref[...], c2, sa2, sb2) * ATT_SCALE).astype(BF16)
        qr_ref[:, 512:1024] = (_rot(q1_ref[...], c2, sa2, sb2) * ATT_SCALE).astype(BF16)
        kr_ref[...] = _rot(k_ref[...], c, sa, sb).astype(BF16)

    tab = pl.BlockSpec((tr, KV_W), lambda i: (i, 0))
    return pl.pallas_call(
        body, name=name, grid=(lp // tr,),
        in_specs=[pl.BlockSpec((tr, 512), lambda i: (i, COL_Q0)),
                  pl.BlockSpec((tr, 512), lambda i: (i, COL_Q0 + 1)),
                  pl.BlockSpec((tr, KV_W), lambda i: (i, COL_K256)),
                  tab, tab, tab],
        out_specs=[pl.BlockSpec((tr, ATT_W), lambda i: (i, 0)),
                   pl.BlockSpec((tr, KV_W), lambda i: (i, 0))],
        out_shape=[jax.ShapeDtypeStruct((lp, ATT_W), BF16),
                   jax.ShapeDtypeStruct((lp, KV_W), BF16)],
        compiler_params=_cp(1),
    )(proj, proj, proj, *tabs)


def _attn_mask(j):
    qi = lax.broadcasted_iota(jnp.int32, (GROUP * TB, 3 * TB), 0) & (TB - 1)
    cc = lax.broadcasted_iota(jnp.int32, (GROUP * TB, 3 * TB), 1)
    jj = cc & (TB - 1)
    is_meta = jj >= PAD0
    p0 = (cc < TB) & is_meta & (j >= 1)
    p1 = (cc >= TB) & (cc < 2 * TB) & (jj > qi) & (j >= 2)
    p2 = (cc >= 2 * TB) & (jj <= qi) & ((j >= 1) | is_meta)
    return p0 | p1 | p2


def _lane_group(rows):
    return lax.broadcasted_iota(jnp.int32, (rows, KV_W), 1) // HEAD_DIM


def _stack_heads(x, kv, lgq):
    parts = []
    for g in range(GROUP):
        sh = ((kv - g) % GROUP) * HEAD_DIM
        moved = x if sh == 0 else pltpu.roll(x, sh, axis=1)
        parts.append(jnp.where(lgq == kv, moved, 0.0))
    return jnp.concatenate(parts, axis=0).astype(BF16)


def _unstack_heads(r, kv):
    out = None
    for g in range(GROUP):
        blk = r[g * TB:(g + 1) * TB, :]
        sh = ((g - kv) % GROUP) * HEAD_DIM
        blk = blk if sh == 0 else pltpu.roll(blk, sh, axis=1)
        out = blk if out is None else out + blk
    return out


def _sink_column(sinks, kv):
    lane = lax.broadcasted_iota(jnp.int32, (1, 128), 1)
    cols = []
    for g in range(GROUP):
        sg = jnp.sum(jnp.where(lane == kv * GROUP + g, sinks, 0.0), axis=1, keepdims=True)
        cols.append(jnp.broadcast_to(sg, (TB, 1)))
    return jnp.concatenate(cols, axis=0)


def _attn_kv(kall, vall, lg, kv):
    km = jnp.where(lg == kv, kall, 0.0).astype(BF16)
    vm = jnp.where(lg == kv, vall, 0.0).astype(BF16)
    ones = jnp.where(lg == kv, 1.0, 0.0).astype(BF16)
    return km, vm, ones


def _attn_specs(jmap):
    blk = lambda col: pl.BlockSpec((TB, KV_W), lambda n: (jmap(n), col))
    prv = lambda col: pl.BlockSpec((TB, KV_W), lambda n: (jnp.maximum(jmap(n) - 1, 0), col))
    met = lambda col: pl.BlockSpec((TB, KV_W), lambda n: (0, col))
    return dict(
        qr=pl.BlockSpec((TB, ATT_W), lambda n: (jmap(n), 0)),
        k=[met(0), prv(0), blk(0)],
        v=[met(COL_V256), prv(COL_V256), blk(COL_V256)],
        gate=pl.BlockSpec((TB, ATT_W), lambda n: (jmap(n), COL_AGATE1024)),
        sinks=pl.BlockSpec((8, 128), lambda n: (0, 0)),
    )


ATTN_BWD_HEAD_SETS = ((0, 1, 2, 3),)


def _attn_fwd(name, qr, kr, proj, sinks_row, ycat):
    lp = proj.shape[0]
    nb = lp // TB
    sp = _attn_specs(lambda n: n)

    def body(qr_ref, km_ref, kp_ref, kc_ref, vm_ref, vp_ref, vc_ref, gate_ref, sink_ref, yin_ref, o_ref):
        del yin_ref
        j = pl.program_id(0)
        valid = _attn_mask(j)
        kall = jnp.concatenate([km_ref[...], kp_ref[...], kc_ref[...]], axis=0).astype(F32)
        vall = jnp.concatenate([vm_ref[...], vp_ref[...], vc_ref[...]], axis=0)
        lg = _lane_group(3 * TB)
        lgq = _lane_group(TB)
        lg4 = _lane_group(GROUP * TB)
        sinks = sink_ref[0:1, :]
        heads = range(N_KV)
        cols = [slice(kv * KV_W, (kv + 1) * KV_W) for kv in heads]
        kvo = [_attn_kv(kall, vall, lg, kv) for kv in heads]
        qst = [_stack_heads(qr_ref[:, cols[kv]].astype(F32), kv, lgq) for kv in heads]
        s = [jnp.where(valid, _dot(qst[kv], kvo[kv][0], NT), NEG_INF) for kv in heads]
        eb, es = [], []
        for kv in heads:
            sinkcol = _sink_column(sinks, kv)
            m = jnp.maximum(jnp.max(s[kv], axis=-1, keepdims=True), sinkcol)
            eb.append(jnp.exp(s[kv] - m).astype(BF16))
            es.append(jnp.exp(sinkcol - m))
        r = [_dot(eb[kv], kvo[kv][1], NN) for kv in heads]
        inv = [1.0 / (_dot(eb[kv], kvo[kv][2], NN) + es[kv]) for kv in heads]
        for kv in heads:
            out = jnp.where(lg4 == kv, r[kv] * inv[kv], 0.0)
            gate = gate_ref[:, cols[kv]]
            o_ref[:, cols[kv]] = (_unstack_heads(out, kv) * (gate * _sig(gate))).astype(BF16)

    return pl.pallas_call(
        body, name=name, grid=(nb,),
        in_specs=[sp["qr"]] + sp["k"] + sp["v"] + [sp["gate"], sp["sinks"],
                                                   pl.BlockSpec(memory_space=pl.ANY)],
        out_specs=pl.BlockSpec((TB, ATT_W), lambda n: (n, 0)),
        out_shape=jax.ShapeDtypeStruct((lp, D_MODEL), BF16),
        input_output_aliases={9: 0},
        compiler_params=_cp(1),
    )(qr, kr, kr, kr, proj, proj, proj, proj, sinks_row, ycat)


def _attn_bwd(name, qr, kr, proj, sinks_row, dycat, dep):
    lp = proj.shape[0]
    nb = lp // TB
    sp = _attn_specs(lambda n: n)

    def body(qr_ref, km_ref, kp_ref, kc_ref, vm_ref, vp_ref, vc_ref, gate_ref, sink_ref, dy_ref, dep_ref,
             dq_ref, dgate_ref, dk_ref, dv_ref, dsink_ref):
        del dep_ref
        j = pl.program_id(0)

        @pl.when(j == 0)
        def _():
            dk_ref[...] = jnp.zeros(dk_ref.shape, F32)
            dv_ref[...] = jnp.zeros(dv_ref.shape, F32)
            dsink_ref[...] = jnp.zeros(dsink_ref.shape, F32)

        valid = _attn_mask(j)
        kall = jnp.concatenate([km_ref[...], kp_ref[...], kc_ref[...]], axis=0).astype(F32)
        vall = jnp.concatenate([vm_ref[...], vp_ref[...], vc_ref[...]], axis=0)
        lg = _lane_group(3 * TB)
        lgq = _lane_group(TB)
        lg4 = _lane_group(GROUP * TB)
        sinks = sink_ref[0:1, :]
        lane = lax.broadcasted_iota(jnp.int32, (1, 128), 1)
        def stages(heads):
            dsink = jnp.zeros((1, 128), F32)
            cols = {kv: slice(kv * KV_W, (kv + 1) * KV_W) for kv in heads}
            kvo = {kv: _attn_kv(kall, vall, lg, kv) for kv in heads}
            qst = {kv: _stack_heads(qr_ref[:, cols[kv]].astype(F32), kv, lgq) for kv in heads}
            s = {kv: jnp.where(valid, _dot(qst[kv], kvo[kv][0], NT), NEG_INF) for kv in heads}
            eb, es = {}, {}
            for kv in heads:
                sinkcol = _sink_column(sinks, kv)
                m = jnp.maximum(jnp.max(s[kv], axis=-1, keepdims=True), sinkcol)
                eb[kv] = jnp.exp(s[kv] - m).astype(BF16)
                es[kv] = jnp.exp(sinkcol - m)
            r = {kv: _dot(eb[kv], kvo[kv][1], NN) for kv in heads}
            inv = {kv: 1.0 / (_dot(eb[kv], kvo[kv][2], NN) + es[kv]) for kv in heads}
            dost, dcol = {}, {}
            for kv in heads:
                att = _unstack_heads(jnp.where(lg4 == kv, r[kv] * inv[kv], 0.0), kv)
                gate = gate_ref[:, cols[kv]]
                sgate = _sig(gate)
                dy = dy_ref[:, cols[kv]].astype(F32)
                dgate_ref[:, cols[kv]] = (dy * att * _dsilu(gate, sgate)).astype(BF16)
                dsc = dy * (gate * sgate) * _unstack_heads(jnp.where(lg4 == kv, inv[kv], 0.0), kv)
                dost[kv] = _stack_heads(dsc, kv, lgq)
                dd = dsc * att
                dcol[kv] = jnp.concatenate(
                    [jnp.sum(jnp.where(lgq == g, dd, 0.0), axis=1, keepdims=True) for g in range(GROUP)], axis=0)
            dp = {kv: _dot(dost[kv], kvo[kv][1], NT) for kv in heads}
            ds = {}
            for kv in heads:
                ds[kv] = (eb[kv].astype(F32) * (dp[kv] - dcol[kv])).astype(BF16)
                pd = es[kv] * dcol[kv]
                for g in range(GROUP):
                    tot = jnp.sum(pd[g * TB:(g + 1) * TB, :], axis=0, keepdims=True)
                    dsink = dsink - jnp.where(lane == kv * GROUP + g, tot, 0.0)
            dqs = {kv: _dot(ds[kv], kvo[kv][0], NN) for kv in heads}
            dks = [_dot(ds[kv], qst[kv], TN) for kv in heads]
            dvs = [_dot(eb[kv], dost[kv], TN) for kv in heads]
            for kv in heads:
                dq_ref[:, cols[kv]] = _unstack_heads(dqs[kv], kv)
            return sum(dks[1:], dks[0]), sum(dvs[1:], dvs[0]), dsink

        parts = [stages(hs) for hs in ATTN_BWD_HEAD_SETS]
        dkall = sum([p[0] for p in parts[1:]], parts[0][0])
        dvall = sum([p[1] for p in parts[1:]], parts[0][1])
        dsink = sum([p[2] for p in parts[1:]], parts[0][2])
        dsink_ref[0:1, :] += dsink
        prev = pl.multiple_of(jnp.maximum(j - 1, 0) * TB, TB)
        cur = pl.multiple_of(j * TB, TB)
        dk_ref[0:TB, :] += dkall[0:TB]
        dv_ref[0:TB, :] += dvall[0:TB]
        dk_ref[pl.ds(prev, TB), :] += dkall[TB:2 * TB]
        dv_ref[pl.ds(prev, TB), :] += dvall[TB:2 * TB]
        dk_ref[pl.ds(cur, TB), :] += dkall[2 * TB:3 * TB]
        dv_ref[pl.ds(cur, TB), :] += dvall[2 * TB:3 * TB]

    return pl.pallas_call(
        body, name=name, grid=(nb,),
        in_specs=[sp["qr"]] + sp["k"] + sp["v"] + [sp["gate"], sp["sinks"],
                                                   pl.BlockSpec((TB, ATT_W), lambda n: (n, 0)),
                                                   pl.BlockSpec(memory_space=pl.ANY)],
        out_specs=[pl.BlockSpec((TB, ATT_W), lambda n: (n, 0)),
                   pl.BlockSpec((TB, ATT_W), lambda n: (n, 0)),
                   pl.BlockSpec((lp, KV_W), lambda n: (0, 0)),
                   pl.BlockSpec((lp, KV_W), lambda n: (0, 0)),
                   pl.BlockSpec((8, 128), lambda n: (0, 0))],
        out_shape=[jax.ShapeDtypeStruct((lp, ATT_W), F32),
                   jax.ShapeDtypeStruct((lp, ATT_W), BF16),
                   jax.ShapeDtypeStruct((lp, KV_W), F32),
                   jax.ShapeDtypeStruct((lp, KV_W), F32),
                   jax.ShapeDtypeStruct((8, 128), F32)],
        compiler_params=_cp(1),
    )(qr, kr, kr, kr, proj, proj, proj, proj, sinks_row, dycat, dep)


def _attn_assemble(name, dq, dgate, dk, dv, tabs, dproj):
    lp = dq.shape[0]
    tr = _row_tile(lp, 11)

    def body(dq_ref, dg_ref, dk_ref, dv_ref, c_ref, sa_ref, sb_ref, din_ref, o_ref):
        del din_ref
        cidx = pl.program_id(1)
        c, sa, sb = c_ref[...], sa_ref[...], sb_ref[...]

        @pl.when(cidx < 2)
        def _():
            c2 = jnp.concatenate([c, c], axis=1)
            sa2 = jnp.concatenate([sa, sa], axis=1)
            sb2 = jnp.concatenate([sb, sb], axis=1)
            o_ref[...] = (_rot_t(dq_ref[...], c2, sa2, sb2) * ATT_SCALE).astype(BF16)

        @pl.when(cidx == 2)
        def _():
            o_ref[:, 0:KV_W] = _rot_t(dk_ref[...], c, sa, sb).astype(BF16)
            o_ref[:, KV_W:2 * KV_W] = dv_ref[...].astype(BF16)

        @pl.when(cidx > 2)
        def _():
            o_ref[...] = dg_ref[...]

    tab = pl.BlockSpec((tr, KV_W), lambda n, c: (n, 0))
    return pl.pallas_call(
        body, name=name, grid=(lp // tr, 5),
        in_specs=[pl.BlockSpec((tr, 512), lambda n, c: (n, jnp.minimum(c, 1))),
                  pl.BlockSpec((tr, 512), lambda n, c: (n, jnp.clip(c - 3, 0, 1))),
                  tab, tab,
                  tab, tab, tab,
                  pl.BlockSpec(memory_space=pl.ANY)],
        out_specs=pl.BlockSpec((tr, 512), lambda n, c: (n, COL_Q0 + c)),
        out_shape=jax.ShapeDtypeStruct((lp, IN_TOTAL), BF16),
        input_output_aliases={7: 0},
        compiler_params=_cp(2),
    )(dq, dgate, dk, dv, *tabs, dproj)


def _softplus_neg(lam):
    t = jnp.exp(-jnp.abs(lam))
    u = 1.0 + t
    den = jnp.where(u == 1.0, 1.0, u - 1.0)
    l1p = jnp.where(u == 1.0, t, jnp.log(u) * (t / den))
    return jnp.maximum(-lam, 0.0) + l1p


def _lru_chain(j, tb, rx_ref, rxp_ref, wl_ref, vec_ref, wa_ref, wx_ref):
    rx = rx_ref[...]
    rxp = jnp.where(j > 0, rxp_ref[...], 0.0)
    cat = jnp.concatenate([rxp, rx], axis=0)
    views = [cat[8:8 + tb, :]] + [pltpu.roll(cat, s, axis=0)[8:8 + tb, :] for s in range(1, LRU_CONV_K)]
    x1 = jnp.broadcast_to(vec_ref[0:1, :], (tb, LRU_W))
    for k in range(LRU_CONV_K):
        x1 = x1 + wl_ref[k:k + 1, :] * views[LRU_CONV_K - 1 - k]
    x1b = x1.astype(BF16)
    r = _sig(_dot(x1b, wa_ref[...], NN) + vec_ref[1:2, :])
    ig = _sig(_dot(x1b, wx_ref[...], NN) + vec_ref[2:3, :])
    sp = _softplus_neg(vec_ref[3:4, :])
    log_a = -LRU_C * r * sp
    rows = _row_ids((tb, LRU_W), j * tb)
    live = rows >= PAD0
    a = jnp.where(live, jnp.exp(log_a), 0.0)
    y2 = 2.0 * log_a
    em = -jnp.tanh(0.5 * y2) * (jnp.exp(y2) + 1.0)
    mult = jnp.sqrt(em)
    return dict(views=views, x1=x1, x1b=x1b, r=r, ig=ig, sp=sp, a=a, mult=mult, live=live, a_raw=jnp.exp(log_a))


def _scan_slabs(a, u, forward):
    tb = a.shape[0]
    rows = lax.broadcasted_iota(jnp.int32, (tb, SUB), 0)
    outs_a, outs_u = [], []
    for c0 in range(0, a.shape[1], SUB):
        ac, uc = a[:, c0:c0 + SUB], u[:, c0:c0 + SUB]
        d = 1
        while d < tb:
            if forward:
                keep, sh = rows >= d, d
            else:
                keep, sh = rows < tb - d, tb - d
            an = jnp.where(keep, pltpu.roll(ac, sh, axis=0), 1.0)
            un = jnp.where(keep, pltpu.roll(uc, sh, axis=0), 0.0)
            uc = ac * un + uc
            ac = ac * an
            d *= 2
        outs_a.append(ac)
        outs_u.append(uc)
    return jnp.concatenate(outs_a, axis=1), jnp.concatenate(outs_u, axis=1)


def _lru_specs(jmap, tb):
    return [pl.BlockSpec((tb, 512), lambda n: (jmap(n), COL_RX)),
            pl.BlockSpec((8, 512), lambda n: (jnp.maximum(jmap(n) * (tb // 8) - 1, 0), COL_RX)),
            pl.BlockSpec((tb, 512), lambda n: (jmap(n), COL_RGATE))]


def _lru_param_specs():
    return [pl.BlockSpec((8, LRU_W), lambda n: (0, 0)),
            pl.BlockSpec((8, LRU_W), lambda n: (0, 0)),
            pl.BlockSpec((LRU_W, LRU_W), lambda n: (0, 0)),
            pl.BlockSpec((LRU_W, LRU_W), lambda n: (0, 0))]


def _lru_fwd(name, proj, wl, vec, wa, wx, ycat):
    lp = proj.shape[0]
    tb = _row_tile(lp, 3)
    nb = lp // tb

    def body(rx_ref, rxp_ref, gate_ref, wl_ref, vec_ref, wa_ref, wx_ref, yin_ref, o_ref, h_ref, carry_ref):
        del yin_ref
        j = pl.program_id(0)

        @pl.when(j == 0)
        def _():
            carry_ref[...] = jnp.zeros(carry_ref.shape, F32)

        c = _lru_chain(j, tb, rx_ref, rxp_ref, wl_ref, vec_ref, wa_ref, wx_ref)
        u = jnp.where(c["live"], c["mult"] * (c["ig"] * c["x1"]), 0.0)
        a, u = _scan_slabs(c["a"], u, forward=True)
        h = u + a * carry_ref[0:1, :]
        carry_ref[...] = h[tb - 8:tb, :]
        carry_ref[0:1, :] = h[tb - 1:tb, :]
        h_ref[...] = h
        gate = gate_ref[...]
        o_ref[...] = (h * (gate * _sig(gate))).astype(BF16)

    return pl.pallas_call(
        body, name=name, grid=(nb,),
        in_specs=_lru_specs(lambda n: n, tb) + _lru_param_specs() + [pl.BlockSpec(memory_space=pl.ANY)],
        out_specs=[pl.BlockSpec((tb, 512), lambda n: (n, YC_LRU)),
                   pl.BlockSpec((tb, LRU_W), lambda n: (n, 0))],
        out_shape=[jax.ShapeDtypeStruct((lp, D_MODEL), BF16),
                   jax.ShapeDtypeStruct((lp, LRU_W), F32)],
        input_output_aliases={7: 0},
        scratch_shapes=[pltpu.VMEM((8, LRU_W), F32)],
        compiler_params=_cp(1),
    )(proj, proj, proj, wl, vec, wa, wx, ycat)


def _lru_bwd(name, proj, dycat, hstate, wl, vec, wa, wx, dproj):
    lp = proj.shape[0]
    tb = _row_tile(lp, 3)
    nb = lp // tb

    def body(rx_ref, rxp_ref, gate_ref, dy_ref, h_ref, hp_ref, wl_ref, vec_ref, wa_ref, wx_ref, din_ref,
             dp_ref, dwl_ref, dvec_ref, dwa_ref, dwx_ref, dhc_ref, anx_ref, dxc_ref):
        del din_ref
        n = pl.program_id(0)
        j = nb - 1 - n

        @pl.when(n == 0)
        def _():
            dhc_ref[...] = jnp.zeros(dhc_ref.shape, F32)
            anx_ref[...] = jnp.zeros(anx_ref.shape, F32)
            dxc_ref[...] = jnp.zeros(dxc_ref.shape, F32)
            dwl_ref[...] = jnp.zeros(dwl_ref.shape, F32)
            dvec_ref[...] = jnp.zeros(dvec_ref.shape, F32)
            dwa_ref[...] = jnp.zeros(dwa_ref.shape, F32)
            dwx_ref[...] = jnp.zeros(dwx_ref.shape, F32)

        c = _lru_chain(j, tb, rx_ref, rxp_ref, wl_ref, vec_ref, wa_ref, wx_ref)
        a, mult, r, ig, x1, live = c["a"], c["mult"], c["r"], c["ig"], c["x1"], c["live"]
        h = h_ref[...]
        gate = gate_ref[...]
        sgate = _sig(gate)
        dy = dy_ref[...].astype(F32)
        gsum = dy * (gate * sgate)
        dgate = dy * h * _dsilu(gate, sgate)
        rows = lax.broadcasted_iota(jnp.int32, (tb, LRU_W), 0)
        bb = jnp.where(rows == tb - 1, anx_ref[0:1, :], pltpu.roll(a, tb - 1, axis=0))
        bb, gg = _scan_slabs(bb, gsum, forward=False)
        dh = gg + bb * dhc_ref[0:1, :]
        dhc_ref[...] = dh[0:8, :]
        anx_ref[...] = a[0:8, :]
        hprev = jnp.where(rows == 0, jnp.where(j > 0, hp_ref[7:8, :], 0.0), pltpu.roll(h, 1, axis=0))
        du = jnp.where(live, dh, 0.0)
        da = jnp.where(live, dh * hprev, 0.0)
        ar = c["a_raw"]
        dmult = du * (ig * x1)
        di = du * mult * x1
        dx1 = du * mult * ig
        dloga = da * ar - dmult * ar * ar / mult
        dr = dloga * (-LRU_C * c["sp"])
        dvec_ref[3:4, :] += _colsum(dloga * (-LRU_C * r))
        dza = dr * r * (1.0 - r)
        dzx = di * ig * (1.0 - ig)
        dzab, dzxb = dza.astype(BF16), dzx.astype(BF16)
        dvec_ref[1:2, :] += _colsum(dza)
        dvec_ref[2:3, :] += _colsum(dzx)
        dwa_ref[...] += _dot(c["x1b"], dzab, TN)
        dwx_ref[...] += _dot(c["x1b"], dzxb, TN)
        dx1 = dx1 + _dot(dzab, wa_ref[...], NT) + _dot(dzxb, wx_ref[...], NT)
        dvec_ref[0:1, :] += _colsum(dx1)
        for k in range(LRU_CONV_K):
            dwl_ref[k:k + 1, :] += _colsum(dx1 * c["views"][LRU_CONV_K - 1 - k])
        dcat = jnp.concatenate([dx1, dxc_ref[...]], axis=0)
        drx = jnp.zeros((tb, LRU_W), F32)
        for k in range(LRU_CONV_K):
            s = LRU_CONV_K - 1 - k
            view = dcat[0:tb, :] if s == 0 else pltpu.roll(dcat, tb + 8 - s, axis=0)[0:tb, :]
            drx = drx + wl_ref[k:k + 1, :] * view
        dxc_ref[...] = dx1[0:8, :]
        dp_ref[:, 0:512] = drx.astype(BF16)
        dp_ref[:, 512:1024] = dgate.astype(BF16)

        @pl.when(n == nb - 1)
        def _():
            lam = vec_ref[3:4, :]
            dvec_ref[3:4, :] = dvec_ref[3:4, :] * (-_sig(-lam))

    jmap = lambda n: nb - 1 - n
    return pl.pallas_call(
        body, name=name, grid=(nb,),
        in_specs=(_lru_specs(jmap, tb)
                  + [pl.BlockSpec((tb, 512), lambda n: (jmap(n), YC_LRU)),
                     pl.BlockSpec((tb, LRU_W), lambda n: (jmap(n), 0)),
                     pl.BlockSpec((8, LRU_W), lambda n: (jnp.maximum(jmap(n) * (tb // 8) - 1, 0), 0))]
                  + _lru_param_specs() + [pl.BlockSpec(memory_space=pl.ANY)]),
        out_specs=[pl.BlockSpec((tb, 1024), lambda n: (jmap(n), 4)),
                   pl.BlockSpec((8, LRU_W), lambda n: (0, 0)),
                   pl.BlockSpec((8, LRU_W), lambda n: (0, 0)),
                   pl.BlockSpec((LRU_W, LRU_W), lambda n: (0, 0)),
                   pl.BlockSpec((LRU_W, LRU_W), lambda n: (0, 0))],
        out_shape=[jax.ShapeDtypeStruct((lp, IN_TOTAL), BF16),
                   jax.ShapeDtypeStruct((8, LRU_W), F32),
                   jax.ShapeDtypeStruct((8, LRU_W), F32),
                   jax.ShapeDtypeStruct((LRU_W, LRU_W), F32),
                   jax.ShapeDtypeStruct((LRU_W, LRU_W), F32)],
        input_output_aliases={10: 0},
        scratch_shapes=[pltpu.VMEM((8, LRU_W), F32), pltpu.VMEM((8, LRU_W), F32), pltpu.VMEM((8, LRU_W), F32)],
        compiler_params=_cp(1),
    )(proj, proj, proj, dycat, hstate, hstate, wl, vec, wa, wx, dproj)


_HBM = pl.BlockSpec(memory_space=pltpu.HBM)
_SEM = pl.BlockSpec(memory_space=pltpu.SEMAPHORE)
_ANY = pl.BlockSpec(memory_space=pl.ANY)
_EFFECT = pltpu.SideEffectType.DATAFLOW_SIDE_EFFECTING


def _hbm(a):
    return pltpu.with_memory_space_constraint(a, pltpu.HBM)


_ALL_PEERS = tuple(range(1, N_DEV))
_CHIP_PEERS = (1, 2, 4, 6)
_OTHER_CHIPS = (2, 4, 6)


def _spec_peers(mode):
    return {"ici": _CHIP_PEERS, "fwd": _OTHER_CHIPS}.get(mode, _ALL_PEERS)


def _split_descriptors(copies, srcs, lands, send_sems, recv_sems):
    x, y, c = lax.axis_index("x"), lax.axis_index("y"), lax.axis_index("c")
    me = 4 * x + 2 * y + c
    out, sem = [], 0
    for si, mode, li, ll in copies:
        for k in _spec_peers(mode):
            px = 1 - x if k & 4 else x
            py = 1 - y if k & 2 else y
            pc = 1 - c if k & 1 else c
            peer = 4 * px + 2 * py + pc
            if mode == "fwd":
                src = dst = lands[li].at[peer]
                target = (x, y, 1 - c)
            else:
                src = srcs[si].at[peer] if mode is True else srcs[si]
                dst = lands[li].at[me] if ll is None else lands[li].at[me, ll]
                target = (px, py, pc)
            out.append(pltpu.make_async_remote_copy(
                src_ref=src, dst_ref=dst, send_sem=send_sems.at[sem], recv_sem=recv_sems.at[sem],
                device_id=target, device_id_type=pl.DeviceIdType.MESH))
            sem += 1
    return out


def _n_copies(copies):
    return sum(len(_spec_peers(mode)) for _, mode, _, _ in copies)


def _xchg_start(name, groups):
    n_src = [len(g[0]) for g in groups]
    n_land = [len(g[1]) for g in groups]
    srcs = [s for g in groups for s in g[0]]
    lands = [l for g in groups for l in g[1]]
    ns, nl, ng = len(srcs), len(lands), len(groups)

    def body(*refs):
        src_refs, land_refs = refs[:ns], refs[ns:ns + nl]
        sems = refs[ns + nl:ns + nl + 2 * ng]
        token = refs[-1]
        so = lo = 0
        for gi, (_, _, copies) in enumerate(groups):
            for d in _split_descriptors(copies, src_refs[so:so + n_src[gi]], land_refs[lo:lo + n_land[gi]],
                                        sems[2 * gi], sems[2 * gi + 1]):
                d.start()
            so += n_src[gi]
            lo += n_land[gi]
        token[...] = jnp.zeros(token.shape, F32)

    out_shape, out_specs = [], []
    for g in groups:
        n = _n_copies(g[2])
        out_shape += [pltpu.SemaphoreType.DMA((n,)), pltpu.SemaphoreType.DMA((n,))]
        out_specs += [_SEM, _SEM]
    out_shape += [pltpu.HBM(l.shape, l.dtype) for l in lands]
    out_specs += [_HBM] * nl
    out_shape.append(jax.ShapeDtypeStruct((8, 128), F32))
    out_specs.append(pl.BlockSpec(memory_space=pltpu.VMEM))
    outs = pl.pallas_call(
        body, name=name, in_specs=[_HBM] * (ns + nl), out_specs=out_specs, out_shape=out_shape,
        input_output_aliases={ns + i: 2 * ng + i for i in range(nl)},
        compiler_params=pltpu.CompilerParams(has_side_effects=_EFFECT),
    )(*[_hbm(a) for a in srcs + lands])
    res, lo = [], 2 * ng
    for gi in range(ng):
        res.append((outs[2 * gi], outs[2 * gi + 1], list(outs[lo:lo + n_land[gi]])))
        lo += n_land[gi]
    return res, outs[-1]


def _xchg_wait(name, group, started, after):
    srcs, _, copies = group
    send_sems, recv_sems, lands = started
    ns, nl = len(srcs), len(lands)
    after = list(after)

    def body(*refs):
        src_refs, land_refs = refs[:ns], refs[ns:ns + nl]
        send_ref, recv_ref = refs[ns + nl], refs[ns + nl + 1]
        for d in _split_descriptors(copies, src_refs, land_refs, send_ref, recv_ref):
            d.wait_send()
            d.wait_recv()

    outs = pl.pallas_call(
        body, name=name, in_specs=[_HBM] * (ns + nl) + [_SEM, _SEM] + [_ANY] * len(after),
        out_specs=[_HBM] * nl, out_shape=[pltpu.HBM(l.shape, l.dtype) for l in lands],
        input_output_aliases={ns + i: i for i in range(nl)},
        compiler_params=pltpu.CompilerParams(has_side_effects=_EFFECT),
    )(*[_hbm(a) for a in srcs], *lands, send_sems, recv_sems, *after)
    return list(outs)


def _landing(own, me):
    land = lax.empty((N_DEV,) + own.shape, own.dtype)
    return lax.dynamic_update_slice(land, own[None], (me,) + (0,) * own.ndim)


def _adamw(name, w, m, v, recv, row0=0, prev=None):
    cdim = w.shape[1]
    r = recv.shape[1]
    tr = r
    for cand in (512, 256, 128, 64, 32, 16, 8):
        if r % cand == 0 and r > cand:
            tr = cand
            break
    assert row0 % tr == 0
    blk0 = row0 // tr
    n_prev = 0 if prev is None else 4

    def body(w_ref, m_ref, v_ref, r_ref, *rest):
        g_ref, d_ref, mo_ref, vo_ref = rest[n_prev:]
        g = r_ref[0].astype(F32)
        for s in range(1, N_DEV):
            g = g + r_ref[s].astype(F32)
        mn = ADAM_B1 * m_ref[...] + (1.0 - ADAM_B1) * g
        vn = ADAM_B2 * v_ref[...] + (1.0 - ADAM_B2) * (g * g)
        m_hat = mn / (1.0 - ADAM_B1 ** ADAM_STEP)
        v_hat = vn / (1.0 - ADAM_B2 ** ADAM_STEP)
        g_ref[...] = g
        d_ref[...] = -ADAM_LR * (m_hat / (jnp.sqrt(v_hat) + ADAM_EPS) + ADAM_WD * w_ref[...])
        mo_ref[...] = mn
        vo_ref[...] = vn

    blk = pl.BlockSpec((tr, cdim), lambda i: (i + blk0, 0))
    return pl.pallas_call(
        body, name=name, grid=(r // tr,),
        in_specs=[blk, blk, blk, pl.BlockSpec((N_DEV, tr, cdim), lambda i: (0, i, 0))] + [_ANY] * n_prev,
        out_specs=[blk, blk, blk, blk],
        out_shape=[jax.ShapeDtypeStruct(w.shape, F32)] * 4,
        input_output_aliases={4 + i: i for i in range(n_prev)},
        compiler_params=_cp(1),
    )(w, m, v, recv, *(prev or []))


def _pack_rows(arrs, lead=()):
    n = len(lead)
    flat = jnp.concatenate([a.reshape(a.shape[:n] + (-1,)) for a in arrs], axis=-1)
    size = flat.shape[-1]
    padded = -(-size // PACK_QUANTUM) * PACK_QUANTUM
    flat = jnp.pad(flat, [(0, 0)] * n + [(0, padded - size)])
    return flat.reshape(flat.shape[:n] + (padded // 128, 128))


def _unpack_rows(packed, shapes, lead=()):
    n = len(lead)
    flat = packed.reshape(packed.shape[:n] + (-1,))
    out, off = [], 0
    for s in shapes:
        size = int(np.prod(s))
        out.append(flat[..., off:off + size].reshape(packed.shape[:n] + tuple(s)))
        off += size
    return out


def _block_diag(w):
    eye = jnp.eye(LRU_HEADS, dtype=w.dtype)
    return (eye[:, None, :, None] * w[:, :, None, :]).reshape(LRU_W, LRU_W)


def _diag_blocks(dense):
    t = dense.reshape(LRU_HEADS, 64, LRU_HEADS, 64)
    eye = jnp.eye(LRU_HEADS, dtype=dense.dtype)
    return jnp.sum(t * eye[:, None, :, None], axis=2).reshape(LRU_HEADS * 64, 64)


_W512_NAMES = ("conv_dw_b", "conv_ln_g", "conv_ln_b", "conv_pw_b", "lru_conv_b", "lru_ba", "lru_bx", "lru_lambda")
_W512_ROWS = 12


def _pack_small(d):
    sinks = jnp.pad(d["attn_sinks"], ((0, 0), (0, 512 - N_HEADS)))
    t = jnp.stack([d[n] for n in _W512_NAMES] + [sinks], axis=1)
    w512 = jnp.pad(t, ((0, 0), (0, _W512_ROWS - t.shape[1]), (0, 0))).reshape(DEPTH * _W512_ROWS, 512)
    w2048 = jnp.concatenate([d["ln_in_g"][None], d["ln_in_b"][None], d["ln_post_g"], d["ln_post_b"],
                             jnp.zeros((2, D_MODEL), F32)], axis=0)
    w64 = jnp.concatenate([d["lru_wa"].reshape(-1, 64), d["lru_wx"].reshape(-1, 64)], axis=0)
    return [w512, w2048, w64.reshape(-1, 128)]


def _unpack_small(w512=None, w2048=None, w64=None):
    out = {}
    if w512 is not None:
        t = w512.reshape(DEPTH, _W512_ROWS, 512)
        out.update({n: t[:, i, :] for i, n in enumerate(_W512_NAMES)})
        out["attn_sinks"] = t[:, len(_W512_NAMES), :N_HEADS]
    if w2048 is not None:
        out["ln_in_g"], out["ln_in_b"] = w2048[0], w2048[1]
        out["ln_post_g"], out["ln_post_b"] = w2048[2:4], w2048[4:6]
    if w64 is not None:
        w64 = w64.reshape(-1, 64)
        half = w64.shape[0] // 2
        out["lru_wa"] = w64[:half].reshape(DEPTH, LRU_HEADS, 64, 64)
        out["lru_wx"] = w64[half:].reshape(DEPTH, LRU_HEADS, 64, 64)
    return out


def _cols_to_slots(full):
    lead = full.shape[:-1]
    t = full.reshape(lead + (N_DEV, full.shape[-1] // N_DEV))
    return jnp.moveaxis(t, -2, 0)


def _slots_to_cols(slots):
    t = jnp.moveaxis(slots, 0, -2)
    return t.reshape(t.shape[:-2] + (t.shape[-2] * t.shape[-1],))


def kernel(x, meta_tokens, ln_in_g, ln_in_b, w_in, conv_dw_w, conv_dw_b, conv_ln_g, conv_ln_b, conv_pw_w, conv_pw_b, attn_sinks, lru_conv_w, lru_conv_b, lru_wa, lru_ba, lru_wx, lru_bx, lru_lambda, w_out, ln_post_g, ln_post_b, loss_target, m_meta_tokens, m_ln_in_g, m_ln_in_b, m_w_in, m_conv_dw_w, m_conv_dw_b, m_conv_ln_g, m_conv_ln_b, m_conv_pw_w, m_conv_pw_b, m_attn_sinks, m_lru_conv_w, m_lru_conv_b, m_lru_wa, m_lru_ba, m_lru_wx, m_lru_bx, m_lru_lambda, m_w_out, m_ln_post_g, m_ln_post_b, v_meta_tokens, v_ln_in_g, v_ln_in_b, v_w_in, v_conv_dw_w, v_conv_dw_b, v_conv_ln_g, v_conv_ln_b, v_conv_pw_w, v_conv_pw_b, v_attn_sinks, v_lru_conv_w, v_lru_conv_b, v_lru_wa, v_lru_ba, v_lru_wx, v_lru_bx, v_lru_lambda, v_w_out, v_ln_post_g, v_ln_post_b):
    seq = x.shape[1]
    lp = seq + TB
    row = lambda a: a.reshape(1, -1)
    shard_small_names = ["conv_dw_w", "lru_conv_w", "meta_tokens"]
    weights = dict(meta_tokens=meta_tokens, ln_in_g=ln_in_g, ln_in_b=ln_in_b, w_in=w_in, conv_dw_w=conv_dw_w,
                   conv_dw_b=conv_dw_b, conv_ln_g=conv_ln_g, conv_ln_b=conv_ln_b, conv_pw_w=conv_pw_w,
                   conv_pw_b=conv_pw_b, attn_sinks=attn_sinks, lru_conv_w=lru_conv_w, lru_conv_b=lru_conv_b,
                   lru_wa=lru_wa, lru_ba=lru_ba, lru_wx=lru_wx, lru_bx=lru_bx, lru_lambda=lru_lambda,
                   w_out=w_out, ln_post_g=ln_post_g, ln_post_b=ln_post_b)
    mom1 = dict(meta_tokens=m_meta_tokens, ln_in_g=m_ln_in_g, ln_in_b=m_ln_in_b, w_in=m_w_in, conv_dw_w=m_conv_dw_w,
                conv_dw_b=m_conv_dw_b, conv_ln_g=m_conv_ln_g, conv_ln_b=m_conv_ln_b, conv_pw_w=m_conv_pw_w,
                conv_pw_b=m_conv_pw_b, attn_sinks=m_attn_sinks, lru_conv_w=m_lru_conv_w, lru_conv_b=m_lru_conv_b,
                lru_wa=m_lru_wa, lru_ba=m_lru_ba, lru_wx=m_lru_wx, lru_bx=m_lru_bx, lru_lambda=m_lru_lambda,
                w_out=m_w_out, ln_post_g=m_ln_post_g, ln_post_b=m_ln_post_b)
    mom2 = dict(meta_tokens=v_meta_tokens, ln_in_g=v_ln_in_g, ln_in_b=v_ln_in_b, w_in=v_w_in, conv_dw_w=v_conv_dw_w,
                conv_dw_b=v_conv_dw_b, conv_ln_g=v_conv_ln_g, conv_ln_b=v_conv_ln_b, conv_pw_w=v_conv_pw_w,
                conv_pw_b=v_conv_pw_b, attn_sinks=v_attn_sinks, lru_conv_w=v_lru_conv_w, lru_conv_b=v_lru_conv_b,
                lru_wa=v_lru_wa, lru_ba=v_lru_ba, lru_wx=v_lru_wx, lru_bx=v_lru_bx, lru_lambda=v_lru_lambda,
                w_out=v_w_out, ln_post_g=v_ln_post_g, ln_post_b=v_ln_post_b)
    shard_wmv = [_pack_rows([d[n] for n in shard_small_names]) for d in (weights, mom1, mom2)]
    rep_wmv = [_pack_small(d) for d in (weights, mom1, mom2)]
    gate_w = [(_block_diag(lru_wa[l]).astype(BF16), _block_diag(lru_wx[l]).astype(BF16)) for l in range(DEPTH)]
    tabs = _rope_tables(lp)
    prepared = (shard_wmv + [a for wmv in rep_wmv for a in wmv]
                + [w for pair in gate_w for w in pair] + list(tabs))

    small_shard_shapes = [conv_dw_w.shape, lru_conv_w.shape, meta_tokens.shape]
    small_shard = _pack_rows([conv_dw_w, lru_conv_w, meta_tokens])
    me = 4 * lax.axis_index("x") + 2 * lax.axis_index("y") + lax.axis_index("c")
    w_in_b = [w_in[l].astype(BF16) for l in range(DEPTH)]
    w_out_b = [w_out[l].astype(BF16) for l in range(DEPTH)]
    pw_b = conv_pw_w.astype(BF16)
    wgroups = [
        ([small_shard], [_landing(small_shard, me)], [(0, False, 0, None)]),
        ([w_in_b[0]], [_landing(w_in_b[0], me)], [(0, "ici", 0, None)]),
        ([pw_b, w_out_b[0]], [_landing(pw_b, me), _landing(w_out_b[0], me)],
         [(0, False, 0, None), (1, False, 1, None)]),
        ([w_in_b[1], w_out_b[1]], [_landing(w_in_b[1], me), _landing(w_out_b[1], me)],
         [(0, "ici", 0, None), (1, "ici", 1, None)]),
    ]
    wstarted, wtoken = _xchg_start("weights_start", wgroups)

    def pass_on(tag, parts):
        fwd = ([], list(parts), [(None, "fwd", i, None) for i in range(len(parts))])
        fstarted, ftoken = _xchg_start(f"weights_fwd_start_{tag}", [fwd])
        return (fwd, fstarted[0]), ftoken
    wg_small, = _xchg_wait("weights_wait_s", wgroups[0], wstarted[0], [wtoken])
    g_dw, g_lc, g_meta = _unpack_rows(wg_small, small_shard_shapes, lead=(N_DEV,))
    conv_dw_full = _slots_to_cols(g_dw)
    lru_conv_full = _slots_to_cols(g_lc)
    meta_full = _slots_to_cols(g_meta)
    wg_in = [None, None]
    wg_out = [None, None]
    wg_pw = None

    ln_g = [ln_in_g, ln_post_g[0], ln_post_g[1]]
    ln_b = [ln_in_b, ln_post_b[0], ln_post_b[1]]

    def layer_params(l):
        wdw = jnp.pad(conv_dw_full[l], ((0, 1), (0, 0)))
        cvec = jnp.pad(jnp.stack([conv_dw_b[l], conv_ln_g[l], conv_ln_b[l], conv_pw_b[l]]), ((0, 4), (0, 0)))
        wpw = wg_pw[:, l].reshape(CONV_W, CONV_W)
        sinks = jnp.pad(attn_sinks[l].reshape(1, N_HEADS), ((0, 7), (0, 128 - N_HEADS)))
        wl = jnp.pad(lru_conv_full[l], ((0, 4), (0, 0)))
        lvec = jnp.pad(jnp.stack([lru_conv_b[l], lru_ba[l], lru_bx[l], lru_lambda[l]]), ((0, 4), (0, 0)))
        wa, wx = gate_w[l]
        wo = wg_out[l].reshape(D_MODEL, D_MODEL)
        wout = jnp.concatenate([wo[512:1536], wo[0:512], wo[1536:]], axis=0)
        return dict(wdw=wdw, cvec=cvec, wpw=wpw, sinks=sinks, wl=wl, lvec=lvec, wa=wa, wx=wx, wout=wout)

    params = [None] * DEPTH

    z0, hb = _embed(x, meta_full, row(ln_g[0]), row(ln_b[0]))
    z = [z0]
    saved = []
    for l in range(DEPTH):
        if l == 0:
            parts = _xchg_wait("weights_wait_a", wgroups[1], wstarted[1], [hb] + prepared)
            pending, ftoken = pass_on("a", parts)
            wg_in[0], = _xchg_wait("weights_fwd_wait_a", *pending, [ftoken])
        else:
            wg_in[1], wg_out[1] = _xchg_wait("weights_fwd_wait_c", *pending_c, [hb])
        proj = _mm_proj(f"proj{l}", hb, wg_in[l])
        if l == 0:
            wg_pw, wg_out[0] = _xchg_wait("weights_wait_b", wgroups[2], wstarted[2], [proj])
        p = params[l] = layer_params(l)
        ycat, c1 = _conv_fwd(f"conv_fwd{l}", proj, p["wdw"], p["cvec"], p["wpw"])
        qr, kr = _rope_fwd(f"rope{l}", proj, tabs)
        ycat = _attn_fwd(f"attn_fwd{l}", qr, kr, proj, p["sinks"], ycat)
        ycat, hstate = _lru_fwd(f"lru_fwd{l}", proj, p["wl"], p["lvec"], p["wa"], p["wx"], ycat)
        if l == 0:
            pending_c, ftoken = pass_on("c", _xchg_wait("weights_wait_c", wgroups[3], wstarted[3], [ycat]))
        saved.append(dict(hb=hb, proj=proj, ycat=ycat, qr=qr, kr=kr, hstate=hstate, c1=c1))
        last = l == DEPTH - 1
        z_next, hb = _mm_out(f"out{l}", ycat, p["wout"], z[l], row(ln_g[l]), row(ln_b[l]),
                             None if last else row(ln_g[l + 1]), None if last else row(ln_b[l + 1]), ftoken)
        z.append(z_next)

    dz, st_post1, loss_blk = _loss_head(z[DEPTH], loss_target, row(ln_g[DEPTH]), row(ln_b[DEPTH]))

    ln_stats = {DEPTH: st_post1}
    g_layers = [None] * DEPTH
    dwin_l, dwout_l = [None] * DEPTH, [None] * DEPTH
    grad_x = gmeta = None
    token = wtoken
    ggroups = [None] * DEPTH
    own = lambda a: lax.dynamic_index_in_dim(a, me, 0, keepdims=False)
    for l in reversed(range(DEPTH)):
        p, s = params[l], saved[l]
        dycat = _mm_dycat(f"dycat{l}", dz, p["wout"], token)
        dwout_l[l] = _mm_dwout(f"dwout{l}", s["ycat"], dz)
        dproj, dwdw, dcvec, dwpw = _conv_bwd(f"conv_bwd{l}", s["proj"], dycat, s["c1"], p["wdw"], p["cvec"], p["wpw"])
        dwo = jnp.concatenate([dwout_l[l][1024:1536], dwout_l[l][0:1024], dwout_l[l][1536:]], axis=0)
        dwo = dwo.reshape(N_DEV, D_MODEL // N_DEV, D_MODEL)
        dpw = dwpw.reshape(N_DEV, CONV_W // N_DEV, CONV_W)
        early = ([dwo, dpw], [_landing(own(dwo), me), _landing(own(dpw), me)],
                 [(0, True, 0, None), (1, True, 1, None)])
        if l == 0:
            (early_started,), token = _xchg_start("grads_start_out0", [early])
        dq, dgate, dk, dv, dsink = _attn_bwd(f"attn_bwd{l}", s["qr"], s["kr"], s["proj"], p["sinks"], dycat, token)
        dproj = _attn_assemble(f"attn_asm{l}", dq, dgate, dk, dv, tabs, dproj)
        dproj, dwl, dlvec, dwa, dwx = _lru_bwd(f"lru_bwd{l}", s["proj"], dycat, s["hstate"],
                                                p["wl"], p["lvec"], p["wa"], p["wx"], dproj)
        g512 = jnp.concatenate([dcvec[0:4], dlvec[0:4], jnp.pad(dsink[0:1], ((0, 0), (0, 512 - 128))),
                                jnp.zeros((_W512_ROWS - 9, 512), F32)], axis=0)
        g_layers[l] = dict(dwdw=dwdw[:CONV_K], dwl=dwl[:LRU_CONV_K], g512=g512,
                           dwa=_diag_blocks(dwa), dwx=_diag_blocks(dwx))
        dwin_l[l] = _mm_dwin(f"dwin{l}", s["hb"], dproj, token)
        late = ([dwin_l[l]], [_landing(own(dwin_l[l]), me)], [(0, True, 0, None)])
        if l == 0:
            g512 = jnp.concatenate([g_layers[i]["g512"] for i in range(DEPTH)], axis=0)
            g64 = jnp.concatenate([g_layers[i][k] for k in ("dwa", "dwx") for i in range(DEPTH)], axis=0)
            g64 = g64.reshape(-1, 128)
            vgroup = ([g512, g64], [_landing(g512, me), _landing(g64, me)],
                      [(0, False, 0, None), (1, False, 1, None)])
            (vstarted, late_started), token = _xchg_start("grads_start_in0", [vgroup, late])
        else:
            (early_started, late_started), token = _xchg_start(f"grads_start_in{l}", [early, late])
        ggroups[l] = [(late, late_started), (early, early_started)]
        dh = _mm_dh(f"dh{l}", dproj, wg_in[l], dz, token)
        if l > 0:
            dz, ln_stats[l] = _ln_bwd(f"ln_bwd{l}", dh, z[l], row(ln_g[l]))
        else:
            grad_x, gmeta, ln_stats[0] = _ln_bwd_input(dh, z[0], row(ln_g[0]))

    loss_row = jnp.pad(loss_blk[0:1, :], ((0, 0), (0, D_MODEL - 128)))
    g2048 = jnp.concatenate([ln_stats[0][0:2], ln_stats[1][0:1], ln_stats[2][0:1], ln_stats[1][1:2],
                             ln_stats[2][1:2], loss_row, jnp.zeros((1, D_MODEL), F32)], axis=0)
    g_dw_full = jnp.stack([g_layers[l]["dwdw"] for l in range(DEPTH)])
    g_lc_full = jnp.stack([g_layers[l]["dwl"] for l in range(DEPTH)])
    shard_pack = _pack_rows([_cols_to_slots(g_dw_full), _cols_to_slots(g_lc_full), _cols_to_slots(gmeta)],
                            lead=(N_DEV,))
    sgroup = ([shard_pack, g2048], [_landing(own(shard_pack), me), _landing(g2048, me)],
              [(0, True, 0, None), (1, False, 1, None)])
    sstarted, token = _xchg_start("small_grads_start", [sgroup])

    res = {}

    def flat2(a, cols):
        return a.reshape(-1, cols)

    big = (("w_in", 0, W_IN_SHARD), ("w_out", 1, D_MODEL), ("conv_pw_w", 2, CONV_W))
    prev = {n: None for n, _, _ in big}
    def update(name_, cols, recv, l):
        w_ = weights[name_]
        prev[name_] = _adamw(f"adamw_{name_}{l}", flat2(w_, cols), flat2(mom1[name_], cols),
                             flat2(mom2[name_], cols), recv, row0=l * w_.shape[1], prev=prev[name_])

    def small_update(tag, ci, recv):
        return _adamw(f"adamw_small_w{tag}", rep_wmv[0][ci], rep_wmv[1][ci], rep_wmv[2][ci], recv)

    def keep(unpacked, k):
        for n, a in unpacked.items():
            res.setdefault(n, [None] * 4)[k] = a

    r_512, r_64 = _xchg_wait("vector_grads_wait", vgroup, vstarted, [token])
    o512, o64 = small_update("512", 0, r_512), small_update("64", 2, r_64)
    early_done = []
    for k in range(4):
        unpacked = _unpack_small(w512=o512[k], w64=o64[k])
        keep(unpacked, k)
        early_done += list(unpacked.values())
    after = [token]
    for l in reversed(range(DEPTH)):
        late, early = ggroups[l]
        r_out, r_pw = _xchg_wait(f"grads_wait{l}_1", early[0], early[1], after)
        if l > 0:
            r_in, = _xchg_wait(f"grads_wait{l}_0", late[0], late[1], after)
            update("w_in", W_IN_SHARD, r_in, l)
        update("w_out", D_MODEL, r_out, l)
        update("conv_pw_w", CONV_W, r_pw, l)
        after = [prev["w_out"][0], prev["conv_pw_w"][0], prev["w_in"][0]]
    late = ggroups[0][0]
    r_in, = _xchg_wait("grads_wait0_0", late[0], late[1], after + early_done)
    update("w_in", W_IN_SHARD, r_in, 0)
    for name_, _, _ in big:
        res[name_] = [o.reshape(weights[name_].shape) for o in prev[name_]]

    r_small, r_2048 = _xchg_wait("small_grads_wait", sgroup, sstarted[0], [prev[n][0] for n, _, _ in big])
    sshapes = [weights[n].shape for n in shard_small_names]
    outs = _adamw("adamw_small_sharded", *shard_wmv, r_small)
    for k, o in enumerate(outs):
        for n, a in zip(shard_small_names, _unpack_rows(o, sshapes)):
            res.setdefault(n, [None] * 4)[k] = a
    o2048 = small_update("2048", 1, r_2048)
    for k in range(4):
        keep(_unpack_small(w2048=o2048[k]), k)
    loss = o2048[0][6, 0]

    order = ["meta_tokens", "ln_in_g", "ln_in_b", "w_in", "conv_dw_w", "conv_dw_b", "conv_ln_g", "conv_ln_b",
             "conv_pw_w", "conv_pw_b", "attn_sinks", "lru_conv_w", "lru_conv_b", "lru_wa", "lru_ba", "lru_wx",
             "lru_bx", "lru_lambda", "w_out", "ln_post_g", "ln_post_b"]
    return (loss, grad_x,
            *[res[n][0] for n in order], *[res[n][1] for n in order],
            *[res[n][2] for n in order], *[res[n][3] for n in order])
```

```python
import numpy as np
import jax
import jax.numpy as jnp
from jax import lax
from jax.experimental import pallas as pl
from jax.experimental.pallas import tpu as pltpu

F32 = jnp.float32
BF16 = jnp.bfloat16

D_MODEL = 2048
DEPTH = 2
N_META = 16
TB = 128
PAD0 = TB - N_META
CONV_W = 512
CONV_K = 31
HEAD_DIM = 64
N_HEADS = 16
N_KV = 4
GROUP = 4
ATT_W = 1024
KV_W = 256
ROT_DIM = 16
ROPE_THETA = 500000.0
LRU_W = 512
LRU_HEADS = 8
LRU_CONV_K = 4
LRU_C = 8.0
IN_TOTAL = 5120
N_DEV = 8
W_IN_SHARD = IN_TOTAL // N_DEV
LN_EPS = 1e-5
ALPHA = (2.0 * DEPTH) ** 0.25
NEG_INF = -1e30
ATT_SCALE = HEAD_DIM ** -0.5

ADAM_LR = 0.001
ADAM_B1 = 0.9
ADAM_B2 = 0.999
ADAM_EPS = 1e-08
ADAM_WD = 0.01
ADAM_STEP = 10

VMEM_LIMIT = 56 * 1024 * 1024
PACK_QUANTUM = 256 * 128

COL_CV, COL_CG, COL_CGATE = 0, 1, 2
COL_Q0 = 3
COL_K256 = 10
COL_V256 = 11
COL_AGATE1024 = 3
COL_RX, COL_RGATE = 8, 9
YC_CONV, YC_LRU = 2, 3


def _cp(n_axes, vmem=VMEM_LIMIT):
    return pltpu.CompilerParams(dimension_semantics=("arbitrary",) * n_axes, vmem_limit_bytes=vmem)


def _row_tile(lp, max_blocks):
    nb = lp // TB
    d = max(k for k in range(1, max_blocks + 1) if nb % k == 0)
    return TB * d


def _sig(x):
    return jax.nn.sigmoid(x)


def _dsilu(x, s):
    return s * (1.0 + x * (1.0 - s))


def _ln_core(z):
    mu = jnp.mean(z, axis=-1, keepdims=True)
    zc = z - mu
    var = jnp.mean(zc * zc, axis=-1, keepdims=True)
    rstd = lax.rsqrt(var + LN_EPS)
    return zc * rstd, rstd


def _ln_bwd_core(dy, xh, rstd, g):
    dxh = dy * g
    m1 = jnp.mean(dxh, axis=-1, keepdims=True)
    m2 = jnp.mean(dxh * xh, axis=-1, keepdims=True)
    return rstd * (dxh - m1 - xh * m2)


def _row_ids(shape, base):
    return lax.broadcasted_iota(jnp.int32, shape, 0) + base


def _colsum(x):
    return jnp.sum(x, axis=0, keepdims=True)


def _dot(a, b, dims):
    return lax.dot_general(a, b, (dims, ((), ())), preferred_element_type=F32)


NN = ((1,), (0,))
NT = ((1,), (1,))
TN = ((0,), (0,))


def _embed(x, meta_full, g, b):
    s = x.shape[1]
    lp = s + TB
    nb = lp // TB

    def body(x_ref, m_ref, g_ref, b_ref, x_hbm, z_hbm, hb_ref, head_ref, sems):
        i = pl.program_id(0)
        head_copy = pltpu.make_async_copy(head_ref, z_hbm.at[pl.ds(0, TB), :], sems.at[0])
        body_copy = pltpu.make_async_copy(x_hbm.at[0], z_hbm.at[pl.ds(TB, s), :], sems.at[1])

        def normed(t):
            xh, _ = _ln_core(t)
            h = xh * g_ref[...] + b_ref[...]
            return jnp.where(_row_ids(h.shape, i * TB) >= PAD0, h, 0.0).astype(BF16)

        @pl.when(i == 0)
        def _():
            head_ref[0:PAD0, :] = jnp.zeros((PAD0, D_MODEL), F32)
            head_ref[PAD0:TB, :] = m_ref[...]
            head_copy.start()
            body_copy.start()
            hb_ref[...] = normed(head_ref[...])

        @pl.when(i > 0)
        def _():
            hb_ref[...] = normed(x_ref[...])

        @pl.when(i == nb - 1)
        def _():
            head_copy.wait()
            body_copy.wait()

    return pl.pallas_call(
        body, name="embed", grid=(nb,),
        in_specs=[pl.BlockSpec((None, TB, D_MODEL), lambda i: (0, jnp.maximum(i - 1, 0), 0)),
                  pl.BlockSpec((N_META, D_MODEL), lambda i: (0, 0)),
                  pl.BlockSpec((1, D_MODEL), lambda i: (0, 0)),
                  pl.BlockSpec((1, D_MODEL), lambda i: (0, 0)),
                  pl.BlockSpec(memory_space=pl.ANY)],
        out_specs=[pl.BlockSpec(memory_space=pl.ANY),
                   pl.BlockSpec((TB, D_MODEL), lambda i: (i, 0))],
        out_shape=[jax.ShapeDtypeStruct((lp, D_MODEL), F32),
                   jax.ShapeDtypeStruct((lp, D_MODEL), BF16)],
        scratch_shapes=[pltpu.VMEM((TB, D_MODEL), F32), pltpu.SemaphoreType.DMA((2,))],
        compiler_params=_cp(1),
    )(x, meta_full, g, b, x)


def _loss_head(z, target, g, b):
    lp = z.shape[0]
    nb = lp // TB

    def body(z_ref, t_ref, g_ref, b_ref, dz_ref, st_ref, loss_ref):
        i = pl.program_id(0)

        @pl.when(i == 0)
        def _():
            st_ref[...] = jnp.zeros(st_ref.shape, F32)
            loss_ref[...] = jnp.zeros(loss_ref.shape, F32)
            dz_ref[...] = jnp.zeros(dz_ref.shape, F32)

        @pl.when(i > 0)
        def _():
            xh, rstd = _ln_core(z_ref[...])
            gg = g_ref[...]
            y = xh * gg + b_ref[...]
            e = y - t_ref[...]
            part = 0.5 * jnp.sum(jnp.mean(e * e, axis=-1, keepdims=True), axis=0, keepdims=True)
            loss_ref[...] += jnp.broadcast_to(part, loss_ref.shape)
            dy = e / float(D_MODEL)
            st_ref[0:1, :] += _colsum(dy * xh)
            st_ref[1:2, :] += _colsum(dy)
            dz_ref[...] = _ln_bwd_core(dy, xh, rstd, gg)

    return pl.pallas_call(
        body, name="loss_head", grid=(nb,),
        in_specs=[pl.BlockSpec((TB, D_MODEL), lambda i: (i, 0)),
                  pl.BlockSpec((None, TB, D_MODEL), lambda i: (0, jnp.maximum(i - 1, 0), 0)),
                  pl.BlockSpec((1, D_MODEL), lambda i: (0, 0)),
                  pl.BlockSpec((1, D_MODEL), lambda i: (0, 0))],
        out_specs=[pl.BlockSpec((TB, D_MODEL), lambda i: (i, 0)),
                   pl.BlockSpec((8, D_MODEL), lambda i: (0, 0)),
                   pl.BlockSpec((8, 128), lambda i: (0, 0))],
        out_shape=[jax.ShapeDtypeStruct((lp, D_MODEL), F32),
                   jax.ShapeDtypeStruct((8, D_MODEL), F32),
                   jax.ShapeDtypeStruct((8, 128), F32)],
        compiler_params=_cp(1),
    )(z, target, g, b)


def _ln_bwd(name, dh, z, g):
    lp = z.shape[0]
    tr = _row_tile(lp, 3)

    def body(dh_ref, z_ref, g_ref, dz_ref, st_ref):
        i = pl.program_id(0)

        @pl.when(i == 0)
        def _():
            st_ref[...] = jnp.zeros(st_ref.shape, F32)

        xh, rstd = _ln_core(z_ref[...])
        rows = _row_ids(xh.shape, i * tr)
        dy = jnp.where(rows >= PAD0, dh_ref[...], 0.0)
        st_ref[0:1, :] += _colsum(dy * xh)
        st_ref[1:2, :] += _colsum(dy)
        dz_ref[...] = _ln_bwd_core(dy, xh, rstd, g_ref[...])

    return pl.pallas_call(
        body, name=name, grid=(lp // tr,),
        in_specs=[pl.BlockSpec((tr, D_MODEL), lambda i: (i, 0)),
                  pl.BlockSpec((tr, D_MODEL), lambda i: (i, 0)),
                  pl.BlockSpec((1, D_MODEL), lambda i: (0, 0))],
        out_specs=[pl.BlockSpec((tr, D_MODEL), lambda i: (i, 0)),
                   pl.BlockSpec((8, D_MODEL), lambda i: (0, 0))],
        out_shape=[jax.ShapeDtypeStruct((lp, D_MODEL), F32),
                   jax.ShapeDtypeStruct((8, D_MODEL), F32)],
        compiler_params=_cp(1),
    )(dh, z, g)


def _ln_bwd_input(dh, z, g):
    lp = z.shape[0]
    nb = lp // TB
    s = lp - TB

    def body(dh_ref, z_ref, g_ref, gx_ref, gm_ref, st_ref):
        i = pl.program_id(0)

        @pl.when(i == 0)
        def _():
            st_ref[...] = jnp.zeros(st_ref.shape, F32)

        xh, rstd = _ln_core(z_ref[...])
        rows = _row_ids(xh.shape, i * TB)
        dy = jnp.where(rows >= PAD0, dh_ref[...], 0.0)
        st_ref[0:1, :] += _colsum(dy * xh)
        st_ref[1:2, :] += _colsum(dy)
        dz = _ln_bwd_core(dy, xh, rstd, g_ref[...])
        gx_ref[...] = dz

        @pl.when(i == 0)
        def _():
            gm_ref[...] = dz[PAD0:TB, :]

    return pl.pallas_call(
        body, name="ln_in_bwd", grid=(nb,),
        in_specs=[pl.BlockSpec((TB, D_MODEL), lambda i: (i, 0)),
                  pl.BlockSpec((TB, D_MODEL), lambda i: (i, 0)),
                  pl.BlockSpec((1, D_MODEL), lambda i: (0, 0))],
        out_specs=[pl.BlockSpec((None, TB, D_MODEL), lambda i: (0, jnp.maximum(i - 1, 0), 0)),
                   pl.BlockSpec((N_META, D_MODEL), lambda i: (0, 0)),
                   pl.BlockSpec((8, D_MODEL), lambda i: (0, 0))],
        out_shape=[jax.ShapeDtypeStruct((1, s, D_MODEL), F32),
                   jax.ShapeDtypeStruct((N_META, D_MODEL), F32),
                   jax.ShapeDtypeStruct((8, D_MODEL), F32)],
        compiler_params=_cp(1),
    )(dh, z, g)


def _mm_proj(name, hb, wg_in):
    lp = hb.shape[0]
    tm = lp // 3

    def body(a_ref, b_ref, o_ref):
        b = jnp.concatenate([b_ref[0], b_ref[1]], axis=1)
        o_ref[...] = _dot(a_ref[...], b, NN)

    return pl.pallas_call(
        body, name=name, grid=(3, N_DEV // 2),
        in_specs=[pl.BlockSpec((tm, D_MODEL), lambda i, j: (i, 0)),
                  pl.BlockSpec((2, D_MODEL, W_IN_SHARD), lambda i, j: (j, 0, 0))],
        out_specs=pl.BlockSpec((tm, 2 * W_IN_SHARD), lambda i, j: (i, j)),
        out_shape=jax.ShapeDtypeStruct((lp, IN_TOTAL), F32),
        compiler_params=_cp(2),
    )(hb, wg_in)


def _mm_out(name, ycat, wout, z, g, b, g2, b2, dep):
    lp = ycat.shape[0]
    tm = lp // 6
    with_next = g2 is not None

    def body(a_ref, w_ref, z_ref, g_ref, b_ref, *rest):
        i = pl.program_id(0)
        xh, _ = _ln_core(z_ref[...])
        h = xh * g_ref[...] + b_ref[...]
        live = _row_ids(h.shape, i * tm) >= PAD0
        h = jnp.where(live, h, 0.0)
        zn = ALPHA * h + _dot(a_ref[...], w_ref[...], NN)
        if with_next:
            g2_ref, b2_ref, _, o_ref, hb_ref = rest
            xh2, _ = _ln_core(zn)
            hb_ref[...] = jnp.where(live, xh2 * g2_ref[...] + b2_ref[...], 0.0).astype(BF16)
        else:
            _, o_ref = rest
        o_ref[...] = zn

    vec = pl.BlockSpec((1, D_MODEL), lambda i: (0, 0))
    row_blk = pl.BlockSpec((tm, D_MODEL), lambda i: (i, 0))
    outs = pl.pallas_call(
        body, name=name, grid=(6,),
        in_specs=([row_blk, pl.BlockSpec((D_MODEL, D_MODEL), lambda i: (0, 0), pipeline_mode=pl.Buffered(1)),
                   row_blk, vec, vec] + ([vec, vec] if with_next else [])
                  + [pl.BlockSpec(memory_space=pl.ANY)]),
        out_specs=[row_blk, row_blk] if with_next else [row_blk],
        out_shape=([jax.ShapeDtypeStruct((lp, D_MODEL), F32)]
                   + ([jax.ShapeDtypeStruct((lp, D_MODEL), BF16)] if with_next else [])),
        compiler_params=_cp(1),
    )(ycat, wout, z, g, b, *((g2, b2) if with_next else ()), dep)
    return (outs[0], outs[1]) if with_next else (outs[0], None)


def _mm_dycat(name, dz, wout, dep):
    lp = dz.shape[0]
    tm = lp // 6

    def body(a_ref, w_ref, dep_ref, o_ref):
        del dep_ref
        o_ref[...] = _dot(a_ref[...].astype(BF16), w_ref[...], NT).astype(BF16)

    return pl.pallas_call(
        body, name=name, grid=(6,),
        in_specs=[pl.BlockSpec((tm, D_MODEL), lambda i: (i, 0)),
                  pl.BlockSpec((D_MODEL, D_MODEL), lambda i: (0, 0), pipeline_mode=pl.Buffered(1)),
                  pl.BlockSpec(memory_space=pl.ANY)],
        out_specs=pl.BlockSpec((tm, D_MODEL), lambda i: (i, 0)),
        out_shape=jax.ShapeDtypeStruct((lp, D_MODEL), BF16),
        compiler_params=_cp(1),
    )(dz, wout, dep)


def _mm_dwout(name, ycat, dz):
    lp = ycat.shape[0]
    tk = _row_tile(lp, 11)
    nk = lp // tk
    half = D_MODEL // 2

    def body(a_ref, b_ref, o_ref, acc_ref):
        k = pl.program_id(1)

        @pl.when(k == 0)
        def _():
            acc_ref[...] = jnp.zeros(acc_ref.shape, F32)

        acc_ref[...] += _dot(a_ref[...], b_ref[...].astype(BF16), TN)

        @pl.when(k == nk - 1)
        def _():
            o_ref[...] = acc_ref[...].astype(BF16)

    return pl.pallas_call(
        body, name=name, grid=(2, nk),
        in_specs=[pl.BlockSpec((tk, half), lambda h, k: (k, h)),
                  pl.BlockSpec((tk, D_MODEL), lambda h, k: (k, 0))],
        out_specs=pl.BlockSpec((half, D_MODEL), lambda h, k: (h, 0)),
        out_shape=jax.ShapeDtypeStruct((D_MODEL, D_MODEL), BF16),
        scratch_shapes=[pltpu.VMEM((half, D_MODEL), F32)],
        compiler_params=_cp(2),
    )(ycat, dz)


def _mm_dwin(name, hb, dproj, dep):
    lp = hb.shape[0]
    tk = _row_tile(lp, 11)
    nk = lp // tk

    def body(a_ref, b_ref, dep_ref, o_ref, acc_ref):
        del dep_ref
        k = pl.program_id(1)

        @pl.when(k == 0)
        def _():
            acc_ref[...] = jnp.zeros(acc_ref.shape, F32)

        acc_ref[...] += _dot(a_ref[...], b_ref[...], TN)

        @pl.when(k == nk - 1)
        def _():
            o_ref[0] = acc_ref[:, 0:W_IN_SHARD].astype(BF16)
            o_ref[1] = acc_ref[:, W_IN_SHARD:2 * W_IN_SHARD].astype(BF16)

    return pl.pallas_call(
        body, name=name, grid=(4, nk),
        in_specs=[pl.BlockSpec((tk, D_MODEL), lambda j, k: (k, 0)),
                  pl.BlockSpec((tk, 2 * W_IN_SHARD), lambda j, k: (k, j)),
                  pl.BlockSpec(memory_space=pl.ANY)],
        out_specs=pl.BlockSpec((2, D_MODEL, W_IN_SHARD), lambda j, k: (j, 0, 0)),
        out_shape=jax.ShapeDtypeStruct((N_DEV, D_MODEL, W_IN_SHARD), BF16),
        scratch_shapes=[pltpu.VMEM((D_MODEL, 2 * W_IN_SHARD), F32)],
        compiler_params=_cp(2),
    )(hb, dproj, dep)


def _mm_dh(name, dproj, wg_in, dz, dep):
    lp = dproj.shape[0]
    tm = lp // 6

    def body(a_ref, w_ref, dz_ref, dep_ref, o_ref, acc_ref):
        del dep_ref
        k = pl.program_id(1)

        @pl.when(k == 0)
        def _():
            acc_ref[...] = jnp.zeros(acc_ref.shape, F32)

        w = jnp.concatenate([w_ref[0], w_ref[1]], axis=1)
        acc_ref[...] += _dot(a_ref[...], w, NT)

        @pl.when(k == N_DEV // 2 - 1)
        def _():
            o_ref[...] = acc_ref[...] + ALPHA * dz_ref[...]

    return pl.pallas_call(
        body, name=name, grid=(6, N_DEV // 2),
        in_specs=[pl.BlockSpec((tm, 2 * W_IN_SHARD), lambda i, k: (i, k)),
                  pl.BlockSpec((2, D_MODEL, W_IN_SHARD), lambda i, k: (k, 0, 0)),
                  pl.BlockSpec((tm, D_MODEL), lambda i, k: (i, 0)),
                  pl.BlockSpec(memory_space=pl.ANY)],
        out_specs=pl.BlockSpec((tm, D_MODEL), lambda i, k: (i, 0)),
        out_shape=jax.ShapeDtypeStruct((lp, D_MODEL), F32),
        scratch_shapes=[pltpu.VMEM((tm, D_MODEL), F32)],
        compiler_params=_cp(2),
    )(dproj, wg_in, dz, dep)


SUB = 128


def _shift_plan(cat, n_shift, base):
    rolled = [cat] + [pltpu.roll(cat, b, axis=0) for b in range(1, 8)]
    return [(rolled[s % 8], base - 8 * (s // 8)) for s in range(n_shift)]


def _tap_sum(w_ref, plan, rows, init=None):
    blocks = []
    for r0 in range(0, rows, SUB):
        row = []
        for c0 in range(0, CONV_W, SUB):
            acc = (jnp.zeros((SUB, SUB), F32) if init is None
                   else jnp.broadcast_to(init[:, c0:c0 + SUB], (SUB, SUB)))
            for k, (arr, off) in enumerate(plan):
                acc = acc + w_ref[k:k + 1, c0:c0 + SUB] * arr[off + r0:off + r0 + SUB, c0:c0 + SUB]
            row.append(acc)
        blocks.append(jnp.concatenate(row, axis=1))
    return jnp.concatenate(blocks, axis=0)


def _tap_grads(dw_ref, dy, plan, rows):
    for c0 in range(0, CONV_W, SUB):
        dys = [dy[r0:r0 + SUB, c0:c0 + SUB] for r0 in range(0, rows, SUB)]
        for k, (arr, off) in enumerate(plan):
            part = None
            for ri, r0 in enumerate(range(0, rows, SUB)):
                prod = dys[ri] * arr[off + r0:off + r0 + SUB, c0:c0 + SUB]
                for i in range(SUB // 8):
                    piece = prod[8 * i:8 * i + 8, :]
                    part = piece if part is None else part + piece
            dw_ref[k:k + 1, c0:c0 + SUB] += jnp.sum(part, axis=0, keepdims=True)


CONV_HALO = 32


def _conv_chain(j, tb, cv_ref, cg_ref, cvp_ref, cgp_ref, wdw_ref, vec_ref, wpw_ref, c1_ref=None):
    cv = cv_ref[...]
    sg = _sig(cg_ref[...])
    c0 = cv * sg
    c0p = jnp.where(j > 0, cvp_ref[...] * _sig(cgp_ref[...]), 0.0)
    cat = jnp.concatenate([c0p, c0], axis=0)
    shifts = _shift_plan(cat, CONV_K, CONV_HALO)
    taps = [shifts[CONV_K - 1 - k] for k in range(CONV_K)]
    if c1_ref is None:
        c1 = _tap_sum(wdw_ref, taps, tb, init=vec_ref[0:1, :])
    else:
        c1 = c1_ref[...]
    xh, rstd = _ln_core(c1)
    c2 = xh * vec_ref[1:2, :] + vec_ref[2:3, :]
    s2 = _sig(c2)
    c3 = c2 * s2
    c4 = _dot(c3.astype(BF16), wpw_ref[...], NN) + vec_ref[3:4, :]
    return dict(cv=cv, sg=sg, taps=taps, c1=c1, xh=xh, rstd=rstd, c2=c2, s2=s2, c3=c3, c4=c4)


def _conv_in_specs(jmap, tb):
    def cur(col):
        return pl.BlockSpec((tb, 512), lambda n: (jmap(n), col))

    def prev(col):
        return pl.BlockSpec((CONV_HALO, 512),
                            lambda n: (jnp.maximum(jmap(n) * (tb // CONV_HALO) - 1, 0), col))

    return [cur(COL_CV), cur(COL_CG), prev(COL_CV), prev(COL_CG), cur(COL_CGATE)]


def _conv_param_specs():
    return [pl.BlockSpec((32, CONV_W), lambda n: (0, 0)),
            pl.BlockSpec((8, CONV_W), lambda n: (0, 0)),
            pl.BlockSpec((CONV_W, CONV_W), lambda n: (0, 0))]


def _conv_fwd(name, proj, wdw, vec, wpw):
    lp = proj.shape[0]
    tb = _row_tile(lp, 3)
    nb = lp // tb

    def body(cv_ref, cg_ref, cvp_ref, cgp_ref, gate_ref, wdw_ref, vec_ref, wpw_ref, o_ref, c1_ref):
        j = pl.program_id(0)
        c = _conv_chain(j, tb, cv_ref, cg_ref, cvp_ref, cgp_ref, wdw_ref, vec_ref, wpw_ref)
        gate = gate_ref[...]
        o_ref[...] = (c["c4"] * (gate * _sig(gate))).astype(BF16)
        c1_ref[...] = c["c1"]

    return pl.pallas_call(
        body, name=name, grid=(nb,),
        in_specs=_conv_in_specs(lambda n: n, tb) + _conv_param_specs(),
        out_specs=[pl.BlockSpec((tb, 512), lambda n: (n, YC_CONV)),
                   pl.BlockSpec((tb, CONV_W), lambda n: (n, 0))],
        out_shape=[jax.ShapeDtypeStruct((lp, D_MODEL), BF16),
                   jax.ShapeDtypeStruct((lp, CONV_W), F32)],
        compiler_params=_cp(1),
    )(proj, proj, proj, proj, proj, wdw, vec, wpw)


def _conv_bwd(name, proj, dycat, c1, wdw, vec, wpw):
    lp = proj.shape[0]
    tb = _row_tile(lp, 3)
    nb = lp // tb
    halo = CONV_HALO

    def body(cv_ref, cg_ref, cvp_ref, cgp_ref, gate_ref, dy_ref, c1_ref, wdw_ref, vec_ref, wpw_ref,
             dp_ref, dwdw_ref, dvec_ref, dwpw_ref, carry_ref):
        n = pl.program_id(0)
        j = nb - 1 - n

        @pl.when(n == 0)
        def _():
            carry_ref[...] = jnp.zeros(carry_ref.shape, F32)
            dwdw_ref[...] = jnp.zeros(dwdw_ref.shape, F32)
            dvec_ref[...] = jnp.zeros(dvec_ref.shape, F32)
            dwpw_ref[...] = jnp.zeros(dwpw_ref.shape, F32)

        c = _conv_chain(j, tb, cv_ref, cg_ref, cvp_ref, cgp_ref, wdw_ref, vec_ref, wpw_ref, c1_ref)
        dy = dy_ref[...].astype(F32)
        gate = gate_ref[...]
        sgate = _sig(gate)
        dc4 = dy * (gate * sgate)
        dgate = dy * c["c4"] * _dsilu(gate, sgate)
        dc4b = dc4.astype(BF16)
        dvec_ref[3:4, :] += _colsum(dc4)
        dwpw_ref[...] += _dot(c["c3"].astype(BF16), dc4b, TN)
        dc3 = _dot(dc4b, wpw_ref[...], NT)
        dc2 = dc3 * _dsilu(c["c2"], c["s2"])
        dvec_ref[1:2, :] += _colsum(dc2 * c["xh"])
        dvec_ref[2:3, :] += _colsum(dc2)
        dc1 = _ln_bwd_core(dc2, c["xh"], c["rstd"], vec_ref[1:2, :])
        dvec_ref[0:1, :] += _colsum(dc1)
        _tap_grads(dwdw_ref, dc1, c["taps"], tb)
        dcat = jnp.concatenate([dc1, carry_ref[...]], axis=0)
        total = tb + halo
        up = [dcat] + [pltpu.roll(dcat, total - b, axis=0) for b in range(1, 8)]
        ahead = [(up[(CONV_K - 1 - k) % 8], 8 * ((CONV_K - 1 - k) // 8)) for k in range(CONV_K)]
        dc0 = _tap_sum(wdw_ref, ahead, tb)
        carry_ref[...] = dc1[0:halo, :]
        sg = c["sg"]
        dcv = dc0 * sg
        dcg = dc0 * c["cv"] * sg * (1.0 - sg)
        dp_ref[:, 0:512] = dcv.astype(BF16)
        dp_ref[:, 512:1024] = dcg.astype(BF16)
        dp_ref[:, 1024:1536] = dgate.astype(BF16)

    jmap = lambda n: nb - 1 - n
    return pl.pallas_call(
        body, name=name, grid=(nb,),
        in_specs=(_conv_in_specs(jmap, tb)
                  + [pl.BlockSpec((tb, 512), lambda n: (jmap(n), YC_CONV)),
                     pl.BlockSpec((tb, CONV_W), lambda n: (jmap(n), 0))]
                  + _conv_param_specs()),
        out_specs=[pl.BlockSpec((tb, 1536), lambda n: (jmap(n), 0)),
                   pl.BlockSpec((32, CONV_W), lambda n: (0, 0)),
                   pl.BlockSpec((8, CONV_W), lambda n: (0, 0)),
                   pl.BlockSpec((CONV_W, CONV_W), lambda n: (0, 0))],
        out_shape=[jax.ShapeDtypeStruct((lp, IN_TOTAL), BF16),
                   jax.ShapeDtypeStruct((32, CONV_W), F32),
                   jax.ShapeDtypeStruct((8, CONV_W), F32),
                   jax.ShapeDtypeStruct((CONV_W, CONV_W), F32)],
        scratch_shapes=[pltpu.VMEM((halo, CONV_W), F32)],
        compiler_params=_cp(1),
    )(proj, proj, proj, proj, proj, dycat, c1, wdw, vec, wpw)


def _rope_tables(lp):
    half = ROT_DIM // 2
    inv_freq = ROPE_THETA ** (-jnp.arange(half, dtype=F32) / half)
    pos = (jnp.arange(lp, dtype=jnp.int32) - PAD0).astype(F32)
    ang = pos[:, None] * inv_freq[None, :]
    cos, sin = jnp.cos(ang), jnp.sin(ang)
    ones = jnp.ones((lp, HEAD_DIM - ROT_DIM), F32)
    zeros = jnp.zeros((lp, HEAD_DIM - ROT_DIM), F32)
    zh = jnp.zeros((lp, half), F32)
    c = jnp.concatenate([cos, cos, ones], axis=1)
    sa = jnp.concatenate([-sin, zh, zeros], axis=1)
    sb = jnp.concatenate([zh, sin, zeros], axis=1)
    tile = lambda t: jnp.tile(t, (1, KV_W // HEAD_DIM))
    return tile(c), tile(sa), tile(sb)


def _rot(x, c, sa, sb):
    w = x.shape[1]
    return x * c + pltpu.roll(x, w - 8, axis=1) * sa + pltpu.roll(x, 8, axis=1) * sb


def _rot_t(dy, c, sa, sb):
    w = dy.shape[1]
    return dy * c + pltpu.roll(dy * sa, 8, axis=1) + pltpu.roll(dy * sb, w - 8, axis=1)


def _rope_fwd(name, proj, tabs):
    lp = proj.shape[0]
    tr = _row_tile(lp, 11)

    def body(q0_ref, q1_ref, k_ref, c_ref, sa_ref, sb_ref, qr_ref, kr_ref):
        c, sa, sb = c_ref[...], sa_ref[...], sb_ref[...]
        c2 = jnp.concatenate([c, c], axis=1)
        sa2 = jnp.concatenate([sa, sa], axis=1)
        sb2 = jnp.concatenate([sb, sb], axis=1)
        qr_ref[:, 0:512] = (_rot(q0_ref[...], c2, sa2, sb2) * ATT_SCALE).astype(BF16)
        qr_ref[:, 512:1024] = (_rot(q1_ref[...], c2, sa2, sb2) * ATT_SCALE).astype(BF16)
        kr_ref[...] = _rot(k_ref[...], c, sa, sb).astype(BF16)

    tab = pl.BlockSpec((tr, KV_W), lambda i: (i, 0))
    return pl.pallas_call(
        body, name=name, grid=(lp // tr,),
        in_specs=[pl.BlockSpec((tr, 512), lambda i: (i, COL_Q0)),
                  pl.BlockSpec((tr, 512), lambda i: (i, COL_Q0 + 1)),
                  pl.BlockSpec((tr, KV_W), lambda i: (i, COL_K256)),
                  tab, tab, tab],
        out_specs=[pl.BlockSpec((tr, ATT_W), lambda i: (i, 0)),
                   pl.BlockSpec((tr, KV_W), lambda i: (i, 0))],
        out_shape=[jax.ShapeDtypeStruct((lp, ATT_W), BF16),
                   jax.ShapeDtypeStruct((lp, KV_W), BF16)],
        compiler_params=_cp(1),
    )(proj, proj, proj, *tabs)


def _attn_mask(j):
    qi = lax.broadcasted_iota(jnp.int32, (GROUP * TB, 3 * TB), 0) & (TB - 1)
    cc = lax.broadcasted_iota(jnp.int32, (GROUP * TB, 3 * TB), 1)
    jj = cc & (TB - 1)
    is_meta = jj >= PAD0
    p0 = (cc < TB) & is_meta & (j >= 1)
    p1 = (cc >= TB) & (cc < 2 * TB) & (jj > qi) & (j >= 2)
    p2 = (cc >= 2 * TB) & (jj <= qi) & ((j >= 1) | is_meta)
    return p0 | p1 | p2


def _lane_group(rows):
    return lax.broadcasted_iota(jnp.int32, (rows, KV_W), 1) // HEAD_DIM


def _stack_heads(x, kv, lgq):
    parts = []
    for g in range(GROUP):
        sh = ((kv - g) % GROUP) * HEAD_DIM
        moved = x if sh == 0 else pltpu.roll(x, sh, axis=1)
        parts.append(jnp.where(lgq == kv, moved, 0.0))
    return jnp.concatenate(parts, axis=0).astype(BF16)


def _unstack_heads(r, kv):
    out = None
    for g in range(GROUP):
        blk = r[g * TB:(g + 1) * TB, :]
        sh = ((g - kv) % GROUP) * HEAD_DIM
        blk = blk if sh == 0 else pltpu.roll(blk, sh, axis=1)
        out = blk if out is None else out + blk
    return out


def _sink_column(sinks, kv):
    lane = lax.broadcasted_iota(jnp.int32, (1, 128), 1)
    cols = []
    for g in range(GROUP):
        sg = jnp.sum(jnp.where(lane == kv * GROUP + g, sinks, 0.0), axis=1, keepdims=True)
        cols.append(jnp.broadcast_to(sg, (TB, 1)))
    return jnp.concatenate(cols, axis=0)


def _attn_kv(kall, vall, lg, kv):
    km = jnp.where(lg == kv, kall, 0.0).astype(BF16)
    vm = jnp.where(lg == kv, vall, 0.0).astype(BF16)
    ones = jnp.where(lg == kv, 1.0, 0.0).astype(BF16)
    return km, vm, ones


def _attn_specs(jmap):
    blk = lambda col: pl.BlockSpec((TB, KV_W), lambda n: (jmap(n), col))
    prv = lambda col: pl.BlockSpec((TB, KV_W), lambda n: (jnp.maximum(jmap(n) - 1, 0), col))
    met = lambda col: pl.BlockSpec((TB, KV_W), lambda n: (0, col))
    return dict(
        qr=pl.BlockSpec((TB, ATT_W), lambda n: (jmap(n), 0)),
        k=[met(0), prv(0), blk(0)],
        v=[met(COL_V256), prv(COL_V256), blk(COL_V256)],
        gate=pl.BlockSpec((TB, ATT_W), lambda n: (jmap(n), COL_AGATE1024)),
        sinks=pl.BlockSpec((8, 128), lambda n: (0, 0)),
    )


ATTN_BWD_HEAD_SETS = ((0, 1, 2, 3),)


def _attn_fwd(name, qr, kr, proj, sinks_row, ycat):
    lp = proj.shape[0]
    nb = lp // TB
    sp = _attn_specs(lambda n: n)

    def body(qr_ref, km_ref, kp_ref, kc_ref, vm_ref, vp_ref, vc_ref, gate_ref, sink_ref, yin_ref, o_ref):
        del yin_ref
        j = pl.program_id(0)
        valid = _attn_mask(j)
        kall = jnp.concatenate([km_ref[...], kp_ref[...], kc_ref[...]], axis=0).astype(F32)
        vall = jnp.concatenate([vm_ref[...], vp_ref[...], vc_ref[...]], axis=0)
        lg = _lane_group(3 * TB)
        lgq = _lane_group(TB)
        lg4 = _lane_group(GROUP * TB)
        sinks = sink_ref[0:1, :]
        heads = range(N_KV)
        cols = [slice(kv * KV_W, (kv + 1) * KV_W) for kv in heads]
        kvo = [_attn_kv(kall, vall, lg, kv) for kv in heads]
        qst = [_stack_heads(qr_ref[:, cols[kv]].astype(F32), kv, lgq) for kv in heads]
        s = [jnp.where(valid, _dot(qst[kv], kvo[kv][0], NT), NEG_INF) for kv in heads]
        eb, es = [], []
        for kv in heads:
            sinkcol = _sink_column(sinks, kv)
            m = jnp.maximum(jnp.max(s[kv], axis=-1, keepdims=True), sinkcol)
            eb.append(jnp.exp(s[kv] - m).astype(BF16))
            es.append(jnp.exp(sinkcol - m))
        r = [_dot(eb[kv], kvo[kv][1], NN) for kv in heads]
        inv = [1.0 / (_dot(eb[kv], kvo[kv][2], NN) + es[kv]) for kv in heads]
        for kv in heads:
            out = jnp.where(lg4 == kv, r[kv] * inv[kv], 0.0)
            gate = gate_ref[:, cols[kv]]
            o_ref[:, cols[kv]] = (_unstack_heads(out, kv) * (gate * _sig(gate))).astype(BF16)

    return pl.pallas_call(
        body, name=name, grid=(nb,),
        in_specs=[sp["qr"]] + sp["k"] + sp["v"] + [sp["gate"], sp["sinks"],
                                                   pl.BlockSpec(memory_space=pl.ANY)],
        out_specs=pl.BlockSpec((TB, ATT_W), lambda n: (n, 0)),
        out_shape=jax.ShapeDtypeStruct((lp, D_MODEL), BF16),
        input_output_aliases={9: 0},
        compiler_params=_cp(1),
    )(qr, kr, kr, kr, proj, proj, proj, proj, sinks_row, ycat)


def _attn_bwd(name, qr, kr, proj, sinks_row, dycat, dep):
    lp = proj.shape[0]
    nb = lp // TB
    sp = _attn_specs(lambda n: n)

    def body(qr_ref, km_ref, kp_ref, kc_ref, vm_ref, vp_ref, vc_ref, gate_ref, sink_ref, dy_ref, dep_ref,
             dq_ref, dgate_ref, dk_ref, dv_ref, dsink_ref):
        del dep_ref
        j = pl.program_id(0)

        @pl.when(j == 0)
        def _():
            dk_ref[...] = jnp.zeros(dk_ref.shape, F32)
            dv_ref[...] = jnp.zeros(dv_ref.shape, F32)
            dsink_ref[...] = jnp.zeros(dsink_ref.shape, F32)

        valid = _attn_mask(j)
        kall = jnp.concatenate([km_ref[...], kp_ref[...], kc_ref[...]], axis=0).astype(F32)
        vall = jnp.concatenate([vm_ref[...], vp_ref[...], vc_ref[...]], axis=0)
        lg = _lane_group(3 * TB)
        lgq = _lane_group(TB)
        lg4 = _lane_group(GROUP * TB)
        sinks = sink_ref[0:1, :]
        lane = lax.broadcasted_iota(jnp.int32, (1, 128), 1)
        def stages(heads):
            dsink = jnp.zeros((1, 128), F32)
            cols = {kv: slice(kv * KV_W, (kv + 1) * KV_W) for kv in heads}
            kvo = {kv: _attn_kv(kall, vall, lg, kv) for kv in heads}
            qst = {kv: _stack_heads(qr_ref[:, cols[kv]].astype(F32), kv, lgq) for kv in heads}
            s = {kv: jnp.where(valid, _dot(qst[kv], kvo[kv][0], NT), NEG_INF) for kv in heads}
            eb, es = {}, {}
            for kv in heads:
                sinkcol = _sink_column(sinks, kv)
                m = jnp.maximum(jnp.max(s[kv], axis=-1, keepdims=True), sinkcol)
                eb[kv] = jnp.exp(s[kv] - m).astype(BF16)
                es[kv] = jnp.exp(sinkcol - m)
            r = {kv: _dot(eb[kv], kvo[kv][1], NN) for kv in heads}
            inv = {kv: 1.0 / (_dot(eb[kv], kvo[kv][2], NN) + es[kv]) for kv in heads}
            dost, dcol = {}, {}
            for kv in heads:
                att = _unstack_heads(jnp.where(lg4 == kv, r[kv] * inv[kv], 0.0), kv)
                gate = gate_ref[:, cols[kv]]
                sgate = _sig(gate)
                dy = dy_ref[:, cols[kv]].astype(F32)
                dgate_ref[:, cols[kv]] = (dy * att * _dsilu(gate, sgate)).astype(BF16)
                dsc = dy * (gate * sgate) * _unstack_heads(jnp.where(lg4 == kv, inv[kv], 0.0), kv)
                dost[kv] = _stack_heads(dsc, kv, lgq)
                dd = dsc * att
                dcol[kv] = jnp.concatenate(
                    [jnp.sum(jnp.where(lgq == g, dd, 0.0), axis=1, keepdims=True) for g in range(GROUP)], axis=0)
            dp = {kv: _dot(dost[kv], kvo[kv][1], NT) for kv in heads}
            ds = {}
            for kv in heads:
                ds[kv] = (eb[kv].astype(F32) * (dp[kv] - dcol[kv])).astype(BF16)
                pd = es[kv] * dcol[kv]
                for g in range(GROUP):
                    tot = jnp.sum(pd[g * TB:(g + 1) * TB, :], axis=0, keepdims=True)
                    dsink = dsink - jnp.where(lane == kv * GROUP + g, tot, 0.0)
            dqs = {kv: _dot(ds[kv], kvo[kv][0], NN) for kv in heads}
            dks = [_dot(ds[kv], qst[kv], TN) for kv in heads]
            dvs = [_dot(eb[kv], dost[kv], TN) for kv in heads]
            for kv in heads:
                dq_ref[:, cols[kv]] = _unstack_heads(dqs[kv], kv)
            return sum(dks[1:], dks[0]), sum(dvs[1:], dvs[0]), dsink

        parts = [stages(hs) for hs in ATTN_BWD_HEAD_SETS]
        dkall = sum([p[0] for p in parts[1:]], parts[0][0])
        dvall = sum([p[1] for p in parts[1:]], parts[0][1])
        dsink = sum([p[2] for p in parts[1:]], parts[0][2])
        dsink_ref[0:1, :] += dsink
        prev = pl.multiple_of(jnp.maximum(j - 1, 0) * TB, TB)
        cur = pl.multiple_of(j * TB, TB)
        dk_ref[0:TB, :] += dkall[0:TB]
        dv_ref[0:TB, :] += dvall[0:TB]
        dk_ref[pl.ds(prev, TB), :] += dkall[TB:2 * TB]
        dv_ref[pl.ds(prev, TB), :] += dvall[TB:2 * TB]
        dk_ref[pl.ds(cur, TB), :] += dkall[2 * TB:3 * TB]
        dv_ref[pl.ds(cur, TB), :] += dvall[2 * TB:3 * TB]

    return pl.pallas_call(
        body, name=name, grid=(nb,),
        in_specs=[sp["qr"]] + sp["k"] + sp["v"] + [sp["gate"], sp["sinks"],
                                                   pl.BlockSpec((TB, ATT_W), lambda n: (n, 0)),
                                                   pl.BlockSpec(memory_space=pl.ANY)],
        out_specs=[pl.BlockSpec((TB, ATT_W), lambda n: (n, 0)),
                   pl.BlockSpec((TB, ATT_W), lambda n: (n, 0)),
                   pl.BlockSpec((lp, KV_W), lambda n: (0, 0)),
                   pl.BlockSpec((lp, KV_W), lambda n: (0, 0)),
                   pl.BlockSpec((8, 128), lambda n: (0, 0))],
        out_shape=[jax.ShapeDtypeStruct((lp, ATT_W), F32),
                   jax.ShapeDtypeStruct((lp, ATT_W), BF16),
                   jax.ShapeDtypeStruct((lp, KV_W), F32),
                   jax.ShapeDtypeStruct((lp, KV_W), F32),
                   jax.ShapeDtypeStruct((8, 128), F32)],
        compiler_params=_cp(1),
    )(qr, kr, kr, kr, proj, proj, proj, proj, sinks_row, dycat, dep)


def _attn_assemble(name, dq, dgate, dk, dv, tabs, dproj):
    lp = dq.shape[0]
    tr = _row_tile(lp, 11)

    def body(dq_ref, dg_ref, dk_ref, dv_ref, c_ref, sa_ref, sb_ref, din_ref, o_ref):
        del din_ref
        cidx = pl.program_id(1)
        c, sa, sb = c_ref[...], sa_ref[...], sb_ref[...]

        @pl.when(cidx < 2)
        def _():
            c2 = jnp.concatenate([c, c], axis=1)
            sa2 = jnp.concatenate([sa, sa], axis=1)
            sb2 = jnp.concatenate([sb, sb], axis=1)
            o_ref[...] = (_rot_t(dq_ref[...], c2, sa2, sb2) * ATT_SCALE).astype(BF16)

        @pl.when(cidx == 2)
        def _():
            o_ref[:, 0:KV_W] = _rot_t(dk_ref[...], c, sa, sb).astype(BF16)
            o_ref[:, KV_W:2 * KV_W] = dv_ref[...].astype(BF16)

        @pl.when(cidx > 2)
        def _():
            o_ref[...] = dg_ref[...]

    tab = pl.BlockSpec((tr, KV_W), lambda n, c: (n, 0))
    return pl.pallas_call(
        body, name=name, grid=(lp // tr, 5),
        in_specs=[pl.BlockSpec((tr, 512), lambda n, c: (n, jnp.minimum(c, 1))),
                  pl.BlockSpec((tr, 512), lambda n, c: (n, jnp.clip(c - 3, 0, 1))),
                  tab, tab,
                  tab, tab, tab,
                  pl.BlockSpec(memory_space=pl.ANY)],
        out_specs=pl.BlockSpec((tr, 512), lambda n, c: (n, COL_Q0 + c)),
        out_shape=jax.ShapeDtypeStruct((lp, IN_TOTAL), BF16),
        input_output_aliases={7: 0},
        compiler_params=_cp(2),
    )(dq, dgate, dk, dv, *tabs, dproj)


def _softplus_neg(lam):
    t = jnp.exp(-jnp.abs(lam))
    u = 1.0 + t
    den = jnp.where(u == 1.0, 1.0, u - 1.0)
    l1p = jnp.where(u == 1.0, t, jnp.log(u) * (t / den))
    return jnp.maximum(-lam, 0.0) + l1p


def _lru_chain(j, tb, rx_ref, rxp_ref, wl_ref, vec_ref, wa_ref, wx_ref):
    rx = rx_ref[...]
    rxp = jnp.where(j > 0, rxp_ref[...], 0.0)
    cat = jnp.concatenate([rxp, rx], axis=0)
    views = [cat[8:8 + tb, :]] + [pltpu.roll(cat, s, axis=0)[8:8 + tb, :] for s in range(1, LRU_CONV_K)]
    x1 = jnp.broadcast_to(vec_ref[0:1, :], (tb, LRU_W))
    for k in range(LRU_CONV_K):
        x1 = x1 + wl_ref[k:k + 1, :] * views[LRU_CONV_K - 1 - k]
    x1b = x1.astype(BF16)
    r = _sig(_dot(x1b, wa_ref[...], NN) + vec_ref[1:2, :])
    ig = _sig(_dot(x1b, wx_ref[...], NN) + vec_ref[2:3, :])
    sp = _softplus_neg(vec_ref[3:4, :])
    log_a = -LRU_C * r * sp
    rows = _row_ids((tb, LRU_W), j * tb)
    live = rows >= PAD0
    a = jnp.where(live, jnp.exp(log_a), 0.0)
    y2 = 2.0 * log_a
    em = -jnp.tanh(0.5 * y2) * (jnp.exp(y2) + 1.0)
    mult = jnp.sqrt(em)
    return dict(views=views, x1=x1, x1b=x1b, r=r, ig=ig, sp=sp, a=a, mult=mult, live=live, a_raw=jnp.exp(log_a))


def _scan_slabs(a, u, forward):
    tb = a.shape[0]
    rows = lax.broadcasted_iota(jnp.int32, (tb, SUB), 0)
    outs_a, outs_u = [], []
    for c0 in range(0, a.shape[1], SUB):
        ac, uc = a[:, c0:c0 + SUB], u[:, c0:c0 + SUB]
        d = 1
        while d < tb:
            if forward:
                keep, sh = rows >= d, d
            else:
                keep, sh = rows < tb - d, tb - d
            an = jnp.where(keep, pltpu.roll(ac, sh, axis=0), 1.0)
            un = jnp.where(keep, pltpu.roll(uc, sh, axis=0), 0.0)
            uc = ac * un + uc
            ac = ac * an
            d *= 2
        outs_a.append(ac)
        outs_u.append(uc)
    return jnp.concatenate(outs_a, axis=1), jnp.concatenate(outs_u, axis=1)


def _lru_specs(jmap, tb):
    return [pl.BlockSpec((tb, 512), lambda n: (jmap(n), COL_RX)),
            pl.BlockSpec((8, 512), lambda n: (jnp.maximum(jmap(n) * (tb // 8) - 1, 0), COL_RX)),
            pl.BlockSpec((tb, 512), lambda n: (jmap(n), COL_RGATE))]


def _lru_param_specs():
    return [pl.BlockSpec((8, LRU_W), lambda n: (0, 0)),
            pl.BlockSpec((8, LRU_W), lambda n: (0, 0)),
            pl.BlockSpec((LRU_W, LRU_W), lambda n: (0, 0)),
            pl.BlockSpec((LRU_W, LRU_W), lambda n: (0, 0))]


def _lru_fwd(name, proj, wl, vec, wa, wx, ycat):
    lp = proj.shape[0]
    tb = _row_tile(lp, 3)
    nb = lp // tb

    def body(rx_ref, rxp_ref, gate_ref, wl_ref, vec_ref, wa_ref, wx_ref, yin_ref, o_ref, h_ref, carry_ref):
        del yin_ref
        j = pl.program_id(0)

        @pl.when(j == 0)
        def _():
            carry_ref[...] = jnp.zeros(carry_ref.shape, F32)

        c = _lru_chain(j, tb, rx_ref, rxp_ref, wl_ref, vec_ref, wa_ref, wx_ref)
        u = jnp.where(c["live"], c["mult"] * (c["ig"] * c["x1"]), 0.0)
        a, u = _scan_slabs(c["a"], u, forward=True)
        h = u + a * carry_ref[0:1, :]
        carry_ref[...] = h[tb - 8:tb, :]
        carry_ref[0:1, :] = h[tb - 1:tb, :]
        h_ref[...] = h
        gate = gate_ref[...]
        o_ref[...] = (h * (gate * _sig(gate))).astype(BF16)

    return pl.pallas_call(
        body, name=name, grid=(nb,),
        in_specs=_lru_specs(lambda n: n, tb) + _lru_param_specs() + [pl.BlockSpec(memory_space=pl.ANY)],
        out_specs=[pl.BlockSpec((tb, 512), lambda n: (n, YC_LRU)),
                   pl.BlockSpec((tb, LRU_W), lambda n: (n, 0))],
        out_shape=[jax.ShapeDtypeStruct((lp, D_MODEL), BF16),
                   jax.ShapeDtypeStruct((lp, LRU_W), F32)],
        input_output_aliases={7: 0},
        scratch_shapes=[pltpu.VMEM((8, LRU_W), F32)],
        compiler_params=_cp(1),
    )(proj, proj, proj, wl, vec, wa, wx, ycat)


def _lru_bwd(name, proj, dycat, hstate, wl, vec, wa, wx, dproj):
    lp = proj.shape[0]
    tb = _row_tile(lp, 3)
    nb = lp // tb

    def body(rx_ref, rxp_ref, gate_ref, dy_ref, h_ref, hp_ref, wl_ref, vec_ref, wa_ref, wx_ref, din_ref,
             dp_ref, dwl_ref, dvec_ref, dwa_ref, dwx_ref, dhc_ref, anx_ref, dxc_ref):
        del din_ref
        n = pl.program_id(0)
        j = nb - 1 - n

        @pl.when(n == 0)
        def _():
            dhc_ref[...] = jnp.zeros(dhc_ref.shape, F32)
            anx_ref[...] = jnp.zeros(anx_ref.shape, F32)
            dxc_ref[...] = jnp.zeros(dxc_ref.shape, F32)
            dwl_ref[...] = jnp.zeros(dwl_ref.shape, F32)
            dvec_ref[...] = jnp.zeros(dvec_ref.shape, F32)
            dwa_ref[...] = jnp.zeros(dwa_ref.shape, F32)
            dwx_ref[...] = jnp.zeros(dwx_ref.shape, F32)

        c = _lru_chain(j, tb, rx_ref, rxp_ref, wl_ref, vec_ref, wa_ref, wx_ref)
        a, mult, r, ig, x1, live = c["a"], c["mult"], c["r"], c["ig"], c["x1"], c["live"]
        h = h_ref[...]
        gate = gate_ref[...]
        sgate = _sig(gate)
        dy = dy_ref[...].astype(F32)
        gsum = dy * (gate * sgate)
        dgate = dy * h * _dsilu(gate, sgate)
        rows = lax.broadcasted_iota(jnp.int32, (tb, LRU_W), 0)
        bb = jnp.where(rows == tb - 1, anx_ref[0:1, :], pltpu.roll(a, tb - 1, axis=0))
        bb, gg = _scan_slabs(bb, gsum, forward=False)
        dh = gg + bb * dhc_ref[0:1, :]
        dhc_ref[...] = dh[0:8, :]
        anx_ref[...] = a[0:8, :]
        hprev = jnp.where(rows == 0, jnp.where(j > 0, hp_ref[7:8, :], 0.0), pltpu.roll(h, 1, axis=0))
        du = jnp.where(live, dh, 0.0)
        da = jnp.where(live, dh * hprev, 0.0)
        ar = c["a_raw"]
        dmult = du * (ig * x1)
        di = du * mult * x1
        dx1 = du * mult * ig
        dloga = da * ar - dmult * ar * ar / mult
        dr = dloga * (-LRU_C * c["sp"])
        dvec_ref[3:4, :] += _colsum(dloga * (-LRU_C * r))
        dza = dr * r * (1.0 - r)
        dzx = di * ig * (1.0 - ig)
        dzab, dzxb = dza.astype(BF16), dzx.astype(BF16)
        dvec_ref[1:2, :] += _colsum(dza)
        dvec_ref[2:3, :] += _colsum(dzx)
        dwa_ref[...] += _dot(c["x1b"], dzab, TN)
        dwx_ref[...] += _dot(c["x1b"], dzxb, TN)
        dx1 = dx1 + _dot(dzab, wa_ref[...], NT) + _dot(dzxb, wx_ref[...], NT)
        dvec_ref[0:1, :] += _colsum(dx1)
        for k in range(LRU_CONV_K):
            dwl_ref[k:k + 1, :] += _colsum(dx1 * c["views"][LRU_CONV_K - 1 - k])
        dcat = jnp.concatenate([dx1, dxc_ref[...]], axis=0)
        drx = jnp.zeros((tb, LRU_W), F32)
        for k in range(LRU_CONV_K):
            s = LRU_CONV_K - 1 - k
            view = dcat[0:tb, :] if s == 0 else pltpu.roll(dcat, tb + 8 - s, axis=0)[0:tb, :]
            drx = drx + wl_ref[k:k + 1, :] * view
        dxc_ref[...] = dx1[0:8, :]
        dp_ref[:, 0:512] = drx.astype(BF16)
        dp_ref[:, 512:1024] = dgate.astype(BF16)

        @pl.when(n == nb - 1)
        def _():
            lam = vec_ref[3:4, :]
            dvec_ref[3:4, :] = dvec_ref[3:4, :] * (-_sig(-lam))

    jmap = lambda n: nb - 1 - n
    return pl.pallas_call(
        body, name=name, grid=(nb,),
        in_specs=(_lru_specs(jmap, tb)
                  + [pl.BlockSpec((tb, 512), lambda n: (jmap(n), YC_LRU)),
                     pl.BlockSpec((tb, LRU_W), lambda n: (jmap(n), 0)),
                     pl.BlockSpec((8, LRU_W), lambda n: (jnp.maximum(jmap(n) * (tb // 8) - 1, 0), 0))]
                  + _lru_param_specs() + [pl.BlockSpec(memory_space=pl.ANY)]),
        out_specs=[pl.BlockSpec((tb, 1024), lambda n: (jmap(n), 4)),
                   pl.BlockSpec((8, LRU_W), lambda n: (0, 0)),
                   pl.BlockSpec((8, LRU_W), lambda n: (0, 0)),
                   pl.BlockSpec((LRU_W, LRU_W), lambda n: (0, 0)),
                   pl.BlockSpec((LRU_W, LRU_W), lambda n: (0, 0))],
        out_shape=[jax.ShapeDtypeStruct((lp, IN_TOTAL), BF16),
                   jax.ShapeDtypeStruct((8, LRU_W), F32),
                   jax.ShapeDtypeStruct((8, LRU_W), F32),
                   jax.ShapeDtypeStruct((LRU_W, LRU_W), F32),
                   jax.ShapeDtypeStruct((LRU_W, LRU_W), F32)],
        input_output_aliases={10: 0},
        scratch_shapes=[pltpu.VMEM((8, LRU_W), F32), pltpu.VMEM((8, LRU_W), F32), pltpu.VMEM((8, LRU_W), F32)],
        compiler_params=_cp(1),
    )(proj, proj, proj, dycat, hstate, hstate, wl, vec, wa, wx, dproj)


_HBM = pl.BlockSpec(memory_space=pltpu.HBM)
_SEM = pl.BlockSpec(memory_space=pltpu.SEMAPHORE)
_ANY = pl.BlockSpec(memory_space=pl.ANY)
_EFFECT = pltpu.SideEffectType.DATAFLOW_SIDE_EFFECTING


def _hbm(a):
    return pltpu.with_memory_space_constraint(a, pltpu.HBM)


_ALL_PEERS = tuple(range(1, N_DEV))
_CHIP_PEERS = (1, 2, 4, 6)
_OTHER_CHIPS = (2, 4, 6)


def _spec_peers(mode):
    return {"ici": _CHIP_PEERS, "fwd": _OTHER_CHIPS}.get(mode, _ALL_PEERS)


def _split_descriptors(copies, srcs, lands, send_sems, recv_sems):
    x, y, c = lax.axis_index("x"), lax.axis_index("y"), lax.axis_index("c")
    me = 4 * x + 2 * y + c
    out, sem = [], 0
    for si, mode, li, ll in copies:
        for k in _spec_peers(mode):
            px = 1 - x if k & 4 else x
            py = 1 - y if k & 2 else y
            pc = 1 - c if k & 1 else c
            peer = 4 * px + 2 * py + pc
            if mode == "fwd":
                src = dst = lands[li].at[peer]
                target = (x, y, 1 - c)
            else:
                src = srcs[si].at[peer] if mode is True else srcs[si]
                dst = lands[li].at[me] if ll is None else lands[li].at[me, ll]
                target = (px, py, pc)
            out.append(pltpu.make_async_remote_copy(
                src_ref=src, dst_ref=dst, send_sem=send_sems.at[sem], recv_sem=recv_sems.at[sem],
                device_id=target, device_id_type=pl.DeviceIdType.MESH))
            sem += 1
    return out


def _n_copies(copies):
    return sum(len(_spec_peers(mode)) for _, mode, _, _ in copies)


def _xchg_start(name, groups):
    n_src = [len(g[0]) for g in groups]
    n_land = [len(g[1]) for g in groups]
    srcs = [s for g in groups for s in g[0]]
    lands = [l for g in groups for l in g[1]]
    ns, nl, ng = len(srcs), len(lands), len(groups)

    def body(*refs):
        src_refs, land_refs = refs[:ns], refs[ns:ns + nl]
        sems = refs[ns + nl:ns + nl + 2 * ng]
        token = refs[-1]
        so = lo = 0
        for gi, (_, _, copies) in enumerate(groups):
            for d in _split_descriptors(copies, src_refs[so:so + n_src[gi]], land_refs[lo:lo + n_land[gi]],
                                        sems[2 * gi], sems[2 * gi + 1]):
                d.start()
            so += n_src[gi]
            lo += n_land[gi]
        token[...] = jnp.zeros(token.shape, F32)

    out_shape, out_specs = [], []
    for g in groups:
        n = _n_copies(g[2])
        out_shape += [pltpu.SemaphoreType.DMA((n,)), pltpu.SemaphoreType.DMA((n,))]
        out_specs += [_SEM, _SEM]
    out_shape += [pltpu.HBM(l.shape, l.dtype) for l in lands]
    out_specs += [_HBM] * nl
    out_shape.append(jax.ShapeDtypeStruct((8, 128), F32))
    out_specs.append(pl.BlockSpec(memory_space=pltpu.VMEM))
    outs = pl.pallas_call(
        body, name=name, in_specs=[_HBM] * (ns + nl), out_specs=out_specs, out_shape=out_shape,
        input_output_aliases={ns + i: 2 * ng + i for i in range(nl)},
        compiler_params=pltpu.CompilerParams(has_side_effects=_EFFECT),
    )(*[_hbm(a) for a in srcs + lands])
    res, lo = [], 2 * ng
    for gi in range(ng):
        res.append((outs[2 * gi], outs[2 * gi + 1], list(outs[lo:lo + n_land[gi]])))
        lo += n_land[gi]
    return res, outs[-1]


def _xchg_wait(name, group, started, after):
    srcs, _, copies = group
    send_sems, recv_sems, lands = started
    ns, nl = len(srcs), len(lands)
    after = list(after)

    def body(*refs):
        src_refs, land_refs = refs[:ns], refs[ns:ns + nl]
        send_ref, recv_ref = refs[ns + nl], refs[ns + nl + 1]
        for d in _split_descriptors(copies, src_refs, land_refs, send_ref, recv_ref):
            d.wait_send()
            d.wait_recv()

    outs = pl.pallas_call(
        body, name=name, in_specs=[_HBM] * (ns + nl) + [_SEM, _SEM] + [_ANY] * len(after),
        out_specs=[_HBM] * nl, out_shape=[pltpu.HBM(l.shape, l.dtype) for l in lands],
        input_output_aliases={ns + i: i for i in range(nl)},
        compiler_params=pltpu.CompilerParams(has_side_effects=_EFFECT),
    )(*[_hbm(a) for a in srcs], *lands, send_sems, recv_sems, *after)
    return list(outs)


def _landing(own, me):
    land = lax.empty((N_DEV,) + own.shape, own.dtype)
    return lax.dynamic_update_slice(land, own[None], (me,) + (0,) * own.ndim)


def _adamw(name, w, m, v, recv, row0=0, prev=None):
    cdim = w.shape[1]
    r = recv.shape[1]
    tr = r
    for cand in (512, 256, 128, 64, 32, 16, 8):
        if r % cand == 0 and r > cand:
            tr = cand
            break
    assert row0 % tr == 0
    blk0 = row0 // tr
    n_prev = 0 if prev is None else 4

    def body(w_ref, m_ref, v_ref, r_ref, *rest):
        g_ref, d_ref, mo_ref, vo_ref = rest[n_prev:]
        g = r_ref[0].astype(F32)
        for s in range(1, N_DEV):
            g = g + r_ref[s].astype(F32)
        mn = ADAM_B1 * m_ref[...] + (1.0 - ADAM_B1) * g
        vn = ADAM_B2 * v_ref[...] + (1.0 - ADAM_B2) * (g * g)
        m_hat = mn / (1.0 - ADAM_B1 ** ADAM_STEP)
        v_hat = vn / (1.0 - ADAM_B2 ** ADAM_STEP)
        g_ref[...] = g
        d_ref[...] = -ADAM_LR * (m_hat / (jnp.sqrt(v_hat) + ADAM_EPS) + ADAM_WD * w_ref[...])
        mo_ref[...] = mn
        vo_ref[...] = vn

    blk = pl.BlockSpec((tr, cdim), lambda i: (i + blk0, 0))
    return pl.pallas_call(
        body, name=name, grid=(r // tr,),
        in_specs=[blk, blk, blk, pl.BlockSpec((N_DEV, tr, cdim), lambda i: (0, i, 0))] + [_ANY] * n_prev,
        out_specs=[blk, blk, blk, blk],
        out_shape=[jax.ShapeDtypeStruct(w.shape, F32)] * 4,
        input_output_aliases={4 + i: i for i in range(n_prev)},
        compiler_params=_cp(1),
    )(w, m, v, recv, *(prev or []))


def _pack_rows(arrs, lead=()):
    n = len(lead)
    flat = jnp.concatenate([a.reshape(a.shape[:n] + (-1,)) for a in arrs], axis=-1)
    size = flat.shape[-1]
    padded = -(-size // PACK_QUANTUM) * PACK_QUANTUM
    flat = jnp.pad(flat, [(0, 0)] * n + [(0, padded - size)])
    return flat.reshape(flat.shape[:n] + (padded // 128, 128))


def _unpack_rows(packed, shapes, lead=()):
    n = len(lead)
    flat = packed.reshape(packed.shape[:n] + (-1,))
    out, off = [], 0
    for s in shapes:
        size = int(np.prod(s))
        out.append(flat[..., off:off + size].reshape(packed.shape[:n] + tuple(s)))
        off += size
    return out


def _block_diag(w):
    eye = jnp.eye(LRU_HEADS, dtype=w.dtype)
    return (eye[:, None, :, None] * w[:, :, None, :]).reshape(LRU_W, LRU_W)


def _diag_blocks(dense):
    t = dense.reshape(LRU_HEADS, 64, LRU_HEADS, 64)
    eye = jnp.eye(LRU_HEADS, dtype=dense.dtype)
    return jnp.sum(t * eye[:, None, :, None], axis=2).reshape(LRU_HEADS * 64, 64)


_W512_NAMES = ("conv_dw_b", "conv_ln_g", "conv_ln_b", "conv_pw_b", "lru_conv_b", "lru_ba", "lru_bx", "lru_lambda")
_W512_ROWS = 12


def _pack_small(d):
    sinks = jnp.pad(d["attn_sinks"], ((0, 0), (0, 512 - N_HEADS)))
    t = jnp.stack([d[n] for n in _W512_NAMES] + [sinks], axis=1)
    w512 = jnp.pad(t, ((0, 0), (0, _W512_ROWS - t.shape[1]), (0, 0))).reshape(DEPTH * _W512_ROWS, 512)
    w2048 = jnp.concatenate([d["ln_in_g"][None], d["ln_in_b"][None], d["ln_post_g"], d["ln_post_b"],
                             jnp.zeros((2, D_MODEL), F32)], axis=0)
    w64 = jnp.concatenate([d["lru_wa"].reshape(-1, 64), d["lru_wx"].reshape(-1, 64)], axis=0)
    return [w512, w2048, w64.reshape(-1, 128)]


def _unpack_small(w512=None, w2048=None, w64=None):
    out = {}
    if w512 is not None:
        t = w512.reshape(DEPTH, _W512_ROWS, 512)
        out.update({n: t[:, i, :] for i, n in enumerate(_W512_NAMES)})
        out["attn_sinks"] = t[:, len(_W512_NAMES), :N_HEADS]
    if w2048 is not None:
        out["ln_in_g"], out["ln_in_b"] = w2048[0], w2048[1]
        out["ln_post_g"], out["ln_post_b"] = w2048[2:4], w2048[4:6]
    if w64 is not None:
        w64 = w64.reshape(-1, 64)
        half = w64.shape[0] // 2
        out["lru_wa"] = w64[:half].reshape(DEPTH, LRU_HEADS, 64, 64)
        out["lru_wx"] = w64[half:].reshape(DEPTH, LRU_HEADS, 64, 64)
    return out


def _cols_to_slots(full):
    lead = full.shape[:-1]
    t = full.reshape(lead + (N_DEV, full.shape[-1] // N_DEV))
    return jnp.moveaxis(t, -2, 0)


def _slots_to_cols(slots):
    t = jnp.moveaxis(slots, 0, -2)
    return t.reshape(t.shape[:-2] + (t.shape[-2] * t.shape[-1],))


def kernel(x, meta_tokens, ln_in_g, ln_in_b, w_in, conv_dw_w, conv_dw_b, conv_ln_g, conv_ln_b, conv_pw_w, conv_pw_b, attn_sinks, lru_conv_w, lru_conv_b, lru_wa, lru_ba, lru_wx, lru_bx, lru_lambda, w_out, ln_post_g, ln_post_b, loss_target, m_meta_tokens, m_ln_in_g, m_ln_in_b, m_w_in, m_conv_dw_w, m_conv_dw_b, m_conv_ln_g, m_conv_ln_b, m_conv_pw_w, m_conv_pw_b, m_attn_sinks, m_lru_conv_w, m_lru_conv_b, m_lru_wa, m_lru_ba, m_lru_wx, m_lru_bx, m_lru_lambda, m_w_out, m_ln_post_g, m_ln_post_b, v_meta_tokens, v_ln_in_g, v_ln_in_b, v_w_in, v_conv_dw_w, v_conv_dw_b, v_conv_ln_g, v_conv_ln_b, v_conv_pw_w, v_conv_pw_b, v_attn_sinks, v_lru_conv_w, v_lru_conv_b, v_lru_wa, v_lru_ba, v_lru_wx, v_lru_bx, v_lru_lambda, v_w_out, v_ln_post_g, v_ln_post_b):
    seq = x.shape[1]
    lp = seq + TB
    row = lambda a: a.reshape(1, -1)
    shard_small_names = ["conv_dw_w", "lru_conv_w", "meta_tokens"]
    weights = dict(meta_tokens=meta_tokens, ln_in_g=ln_in_g, ln_in_b=ln_in_b, w_in=w_in, conv_dw_w=conv_dw_w,
                   conv_dw_b=conv_dw_b, conv_ln_g=conv_ln_g, conv_ln_b=conv_ln_b, conv_pw_w=conv_pw_w,
                   conv_pw_b=conv_pw_b, attn_sinks=attn_sinks, lru_conv_w=lru_conv_w, lru_conv_b=lru_conv_b,
                   lru_wa=lru_wa, lru_ba=lru_ba, lru_wx=lru_wx, lru_bx=lru_bx, lru_lambda=lru_lambda,
                   w_out=w_out, ln_post_g=ln_post_g, ln_post_b=ln_post_b)
    mom1 = dict(meta_tokens=m_meta_tokens, ln_in_g=m_ln_in_g, ln_in_b=m_ln_in_b, w_in=m_w_in, conv_dw_w=m_conv_dw_w,
                conv_dw_b=m_conv_dw_b, conv_ln_g=m_conv_ln_g, conv_ln_b=m_conv_ln_b, conv_pw_w=m_conv_pw_w,
                conv_pw_b=m_conv_pw_b, attn_sinks=m_attn_sinks, lru_conv_w=m_lru_conv_w, lru_conv_b=m_lru_conv_b,
                lru_wa=m_lru_wa, lru_ba=m_lru_ba, lru_wx=m_lru_wx, lru_bx=m_lru_bx, lru_lambda=m_lru_lambda,
                w_out=m_w_out, ln_post_g=m_ln_post_g, ln_post_b=m_ln_post_b)
    mom2 = dict(meta_tokens=v_meta_tokens, ln_in_g=v_ln_in_g, ln_in_b=v_ln_in_b, w_in=v_w_in, conv_dw_w=v_conv_dw_w,
                conv_dw_b=v_conv_dw_b, conv_ln_g=v_conv_ln_g, conv_ln_b=v_conv_ln_b, conv_pw_w=v_conv_pw_w,
                conv_pw_b=v_conv_pw_b, attn_sinks=v_attn_sinks, lru_conv_w=v_lru_conv_w, lru_conv_b=v_lru_conv_b,
                lru_wa=v_lru_wa, lru_ba=v_lru_ba, lru_wx=v_lru_wx, lru_bx=v_lru_bx, lru_lambda=v_lru_lambda,
                w_out=v_w_out, ln_post_g=v_ln_post_g, ln_post_b=v_ln_post_b)
    shard_wmv = [_pack_rows([d[n] for n in shard_small_names]) for d in (weights, mom1, mom2)]
    rep_wmv = [_pack_small(d) for d in (weights, mom1, mom2)]
    gate_w = [(_block_diag(lru_wa[l]).astype(BF16), _block_diag(lru_wx[l]).astype(BF16)) for l in range(DEPTH)]
    tabs = _rope_tables(lp)
    prepared = (shard_wmv + [a for wmv in rep_wmv for a in wmv]
                + [w for pair in gate_w for w in pair] + list(tabs))

    small_shard_shapes = [conv_dw_w.shape, lru_conv_w.shape, meta_tokens.shape]
    small_shard = _pack_rows([conv_dw_w, lru_conv_w, meta_tokens])
    me = 4 * lax.axis_index("x") + 2 * lax.axis_index("y") + lax.axis_index("c")
    w_in_b = [w_in[l].astype(BF16) for l in range(DEPTH)]
    w_out_b = [w_out[l].astype(BF16) for l in range(DEPTH)]
    pw_b = conv_pw_w.astype(BF16)
    wgroups = [
        ([small_shard], [_landing(small_shard, me)], [(0, False, 0, None)]),
        ([w_in_b[0]], [_landing(w_in_b[0], me)], [(0, "ici", 0, None)]),
        ([pw_b, w_out_b[0]], [_landing(pw_b, me), _landing(w_out_b[0], me)],
         [(0, False, 0, None), (1, False, 1, None)]),
        ([w_in_b[1], w_out_b[1]], [_landing(w_in_b[1], me), _landing(w_out_b[1], me)],
         [(0, "ici", 0, None), (1, "ici", 1, None)]),
    ]
    wstarted, wtoken = _xchg_start("weights_start", wgroups)

    def pass_on(tag, parts):
        fwd = ([], list(parts), [(None, "fwd", i, None) for i in range(len(parts))])
        fstarted, ftoken = _xchg_start(f"weights_fwd_start_{tag}", [fwd])
        return (fwd, fstarted[0]), ftoken
    wg_small, = _xchg_wait("weights_wait_s", wgroups[0], wstarted[0], [wtoken])
    g_dw, g_lc, g_meta = _unpack_rows(wg_small, small_shard_shapes, lead=(N_DEV,))
    conv_dw_full = _slots_to_cols(g_dw)
    lru_conv_full = _slots_to_cols(g_lc)
    meta_full = _slots_to_cols(g_meta)
    wg_in = [None, None]
    wg_out = [None, None]
    wg_pw = None

    ln_g = [ln_in_g, ln_post_g[0], ln_post_g[1]]
    ln_b = [ln_in_b, ln_post_b[0], ln_post_b[1]]

    def layer_params(l):
        wdw = jnp.pad(conv_dw_full[l], ((0, 1), (0, 0)))
        cvec = jnp.pad(jnp.stack([conv_dw_b[l], conv_ln_g[l], conv_ln_b[l], conv_pw_b[l]]), ((0, 4), (0, 0)))
        wpw = wg_pw[:, l].reshape(CONV_W, CONV_W)
        sinks = jnp.pad(attn_sinks[l].reshape(1, N_HEADS), ((0, 7), (0, 128 - N_HEADS)))
        wl = jnp.pad(lru_conv_full[l], ((0, 4), (0, 0)))
        lvec = jnp.pad(jnp.stack([lru_conv_b[l], lru_ba[l], lru_bx[l], lru_lambda[l]]), ((0, 4), (0, 0)))
        wa, wx = gate_w[l]
        wo = wg_out[l].reshape(D_MODEL, D_MODEL)
        wout = jnp.concatenate([wo[512:1536], wo[0:512], wo[1536:]], axis=0)
        return dict(wdw=wdw, cvec=cvec, wpw=wpw, sinks=sinks, wl=wl, lvec=lvec, wa=wa, wx=wx, wout=wout)

    params = [None] * DEPTH

    z0, hb = _embed(x, meta_full, row(ln_g[0]), row(ln_b[0]))
    z = [z0]
    saved = []
    for l in range(DEPTH):
        if l == 0:
            parts = _xchg_wait("weights_wait_a", wgroups[1], wstarted[1], [hb] + prepared)
            pending, ftoken = pass_on("a", parts)
            wg_in[0], = _xchg_wait("weights_fwd_wait_a", *pending, [ftoken])
        else:
            wg_in[1], wg_out[1] = _xchg_wait("weights_fwd_wait_c", *pending_c, [hb])
        proj = _mm_proj(f"proj{l}", hb, wg_in[l])
        if l == 0:
            wg_pw, wg_out[0] = _xchg_wait("weights_wait_b", wgroups[2], wstarted[2], [proj])
        p = params[l] = layer_params(l)
        ycat, c1 = _conv_fwd(f"conv_fwd{l}", proj, p["wdw"], p["cvec"], p["wpw"])
        qr, kr = _rope_fwd(f"rope{l}", proj, tabs)
        ycat = _attn_fwd(f"attn_fwd{l}", qr, kr, proj, p["sinks"], ycat)
        ycat, hstate = _lru_fwd(f"lru_fwd{l}", proj, p["wl"], p["lvec"], p["wa"], p["wx"], ycat)
        if l == 0:
            pending_c, ftoken = pass_on("c", _xchg_wait("weights_wait_c", wgroups[3], wstarted[3], [ycat]))
        saved.append(dict(hb=hb, proj=proj, ycat=ycat, qr=qr, kr=kr, hstate=hstate, c1=c1))
        last = l == DEPTH - 1
        z_next, hb = _mm_out(f"out{l}", ycat, p["wout"], z[l], row(ln_g[l]), row(ln_b[l]),
                             None if last else row(ln_g[l + 1]), None if last else row(ln_b[l + 1]), ftoken)
        z.append(z_next)

    dz, st_post1, loss_blk = _loss_head(z[DEPTH], loss_target, row(ln_g[DEPTH]), row(ln_b[DEPTH]))

    ln_stats = {DEPTH: st_post1}
    g_layers = [None] * DEPTH
    dwin_l, dwout_l = [None] * DEPTH, [None] * DEPTH
    grad_x = gmeta = None
    token = wtoken
    ggroups = [None] * DEPTH
    own = lambda a: lax.dynamic_index_in_dim(a, me, 0, keepdims=False)
    for l in reversed(range(DEPTH)):
        p, s = params[l], saved[l]
        dycat = _mm_dycat(f"dycat{l}", dz, p["wout"], token)
        dwout_l[l] = _mm_dwout(f"dwout{l}", s["ycat"], dz)
        dproj, dwdw, dcvec, dwpw = _conv_bwd(f"conv_bwd{l}", s["proj"], dycat, s["c1"], p["wdw"], p["cvec"], p["wpw"])
        dwo = jnp.concatenate([dwout_l[l][1024:1536], dwout_l[l][0:1024], dwout_l[l][1536:]], axis=0)
        dwo = dwo.reshape(N_DEV, D_MODEL // N_DEV, D_MODEL)
        dpw = dwpw.reshape(N_DEV, CONV_W // N_DEV, CONV_W)
        early = ([dwo, dpw], [_landing(own(dwo), me), _landing(own(dpw), me)],
                 [(0, True, 0, None), (1, True, 1, None)])
        started_early, token = _xchg_start(f"grads_start_out{l}", [early])
        dq, dgate, dk, dv, dsink = _attn_bwd(f"attn_bwd{l}", s["qr"], s["kr"], s["proj"], p["sinks"], dycat, token)
        dproj = _attn_assemble(f"attn_asm{l}", dq, dgate, dk, dv, tabs, dproj)
        dproj, dwl, dlvec, dwa, dwx = _lru_bwd(f"lru_bwd{l}", s["proj"], dycat, s["hstate"],
                                                p["wl"], p["lvec"], p["wa"], p["wx"], dproj)
        g512 = jnp.concatenate([dcvec[0:4], dlvec[0:4], jnp.pad(dsink[0:1], ((0, 0), (0, 512 - 128))),
                                jnp.zeros((_W512_ROWS - 9, 512), F32)], axis=0)
        g_layers[l] = dict(dwdw=dwdw[:CONV_K], dwl=dwl[:LRU_CONV_K], g512=g512,
                           dwa=_diag_blocks(dwa), dwx=_diag_blocks(dwx))
        if l == 0:
            g512 = jnp.concatenate([g_layers[i]["g512"] for i in range(DEPTH)], axis=0)
            g64 = jnp.concatenate([g_layers[i][k] for k in ("dwa", "dwx") for i in range(DEPTH)], axis=0)
            g64 = g64.reshape(-1, 128)
            vgroup = ([g512, g64], [_landing(g512, me), _landing(g64, me)],
                      [(0, False, 0, None), (1, False, 1, None)])
            vstarted, token = _xchg_start("vector_grads_start", [vgroup])
        dwin_l[l] = _mm_dwin(f"dwin{l}", s["hb"], dproj, token)
        late = ([dwin_l[l]], [_landing(own(dwin_l[l]), me)], [(0, True, 0, None)])
        started_late, token = _xchg_start(f"grads_start_in{l}", [late])
        ggroups[l] = [(late, started_late[0]), (early, started_early[0])]
        dh = _mm_dh(f"dh{l}", dproj, wg_in[l], dz, token)
        if l > 0:
            dz, ln_stats[l] = _ln_bwd(f"ln_bwd{l}", dh, z[l], row(ln_g[l]))
        else:
            grad_x, gmeta, ln_stats[0] = _ln_bwd_input(dh, z[0], row(ln_g[0]))

    loss_row = jnp.pad(loss_blk[0:1, :], ((0, 0), (0, D_MODEL - 128)))
    g2048 = jnp.concatenate([ln_stats[0][0:2], ln_stats[1][0:1], ln_stats[2][0:1], ln_stats[1][1:2],
                             ln_stats[2][1:2], loss_row, jnp.zeros((1, D_MODEL), F32)], axis=0)
    g_dw_full = jnp.stack([g_layers[l]["dwdw"] for l in range(DEPTH)])
    g_lc_full = jnp.stack([g_layers[l]["dwl"] for l in range(DEPTH)])
    shard_pack = _pack_rows([_cols_to_slots(g_dw_full), _cols_to_slots(g_lc_full), _cols_to_slots(gmeta)],
                            lead=(N_DEV,))
    sgroup = ([shard_pack, g2048], [_landing(own(shard_pack), me), _landing(g2048, me)],
              [(0, True, 0, None), (1, False, 1, None)])
    sstarted, token = _xchg_start("small_grads_start", [sgroup])

    res = {}

    def flat2(a, cols):
        return a.reshape(-1, cols)

    big = (("w_in", 0, W_IN_SHARD), ("w_out", 1, D_MODEL), ("conv_pw_w", 2, CONV_W))
    prev = {n: None for n, _, _ in big}
    def update(name_, cols, recv, l):
        w_ = weights[name_]
        prev[name_] = _adamw(f"adamw_{name_}{l}", flat2(w_, cols), flat2(mom1[name_], cols),
                             flat2(mom2[name_], cols), recv, row0=l * w_.shape[1], prev=prev[name_])

    def small_update(tag, ci, recv):
        return _adamw(f"adamw_small_w{tag}", rep_wmv[0][ci], rep_wmv[1][ci], rep_wmv[2][ci], recv)

    def keep(unpacked, k):
        for n, a in unpacked.items():
            res.setdefault(n, [None] * 4)[k] = a

    r_512, r_64 = _xchg_wait("vector_grads_wait", vgroup, vstarted[0], [token])
    o512, o64 = small_update("512", 0, r_512), small_update("64", 2, r_64)
    early_done = []
    for k in range(4):
        unpacked = _unpack_small(w512=o512[k], w64=o64[k])
        keep(unpacked, k)
        early_done += list(unpacked.values())
    after = [token]
    for l in reversed(range(DEPTH)):
        late, early = ggroups[l]
        r_out, r_pw = _xchg_wait(f"grads_wait{l}_1", early[0], early[1], after)
        if l > 0:
            r_in, = _xchg_wait(f"grads_wait{l}_0", late[0], late[1], after)
            update("w_in", W_IN_SHARD, r_in, l)
        update("w_out", D_MODEL, r_out, l)
        update("conv_pw_w", CONV_W, r_pw, l)
        after = [prev["w_out"][0], prev["conv_pw_w"][0], prev["w_in"][0]]
    late = ggroups[0][0]
    r_in, = _xchg_wait("grads_wait0_0", late[0], late[1], after + early_done)
    update("w_in", W_IN_SHARD, r_in, 0)
    for name_, _, _ in big:
        res[name_] = [o.reshape(weights[name_].shape) for o in prev[name_]]

    r_small, r_2048 = _xchg_wait("small_grads_wait", sgroup, sstarted[0], [prev[n][0] for n, _, _ in big])
    sshapes = [weights[n].shape for n in shard_small_names]
    outs = _adamw("adamw_small_sharded", *shard_wmv, r_small)
    for k, o in enumerate(outs):
        for n, a in zip(shard_small_names, _unpack_rows(o, sshapes)):
            res.setdefault(n, [None] * 4)[k] = a
    o2048 = small_update("2048", 1, r_2048)
    for k in range(4):
        keep(_unpack_small(w2048=o2048[k]), k)
    loss = o2048[0][6, 0]

    order = ["meta_tokens", "ln_in_g", "ln_in_b", "w_in", "conv_dw_w", "conv_dw_b", "conv_ln_g", "conv_ln_b",
             "conv_pw_w", "conv_pw_b", "attn_sinks", "lru_conv_w", "lru_conv_b", "lru_wa", "lru_ba", "lru_wx",
             "lru_bx", "lru_lambda", "w_out", "ln_post_g", "ln_post_b"]
    return (loss, grad_x,
            *[res[n][0] for n in order], *[res[n][1] for n in order],
            *[res[n][2] for n in order], *[res[n][3] for n in order])
```

```python
import numpy as np
import jax
import jax.numpy as jnp
from jax import lax
from jax.experimental import pallas as pl
from jax.experimental.pallas import tpu as pltpu

F32 = jnp.float32
BF16 = jnp.bfloat16

D_MODEL = 2048
DEPTH = 2
N_META = 16
TB = 128
PAD0 = TB - N_META
CONV_W = 512
CONV_K = 31
HEAD_DIM = 64
N_HEADS = 16
N_KV = 4
GROUP = 4
ATT_W = 1024
KV_W = 256
ROT_DIM = 16
ROPE_THETA = 500000.0
LRU_W = 512
LRU_HEADS = 8
LRU_CONV_K = 4
LRU_C = 8.0
IN_TOTAL = 5120
N_DEV = 8
W_IN_SHARD = IN_TOTAL // N_DEV
LN_EPS = 1e-5
ALPHA = (2.0 * DEPTH) ** 0.25
NEG_INF = -1e30
ATT_SCALE = HEAD_DIM ** -0.5

ADAM_LR = 0.001
ADAM_B1 = 0.9
ADAM_B2 = 0.999
ADAM_EPS = 1e-08
ADAM_WD = 0.01
ADAM_STEP = 10

VMEM_LIMIT = 56 * 1024 * 1024
PACK_QUANTUM = 256 * 128

COL_CV, COL_CG, COL_CGATE = 0, 1, 2
COL_Q0 = 3
COL_K256 = 10
COL_V256 = 11
COL_AGATE1024 = 3
COL_RX, COL_RGATE = 8, 9
YC_CONV, YC_LRU = 2, 3


def _cp(n_axes, vmem=VMEM_LIMIT):
    return pltpu.CompilerParams(dimension_semantics=("arbitrary",) * n_axes, vmem_limit_bytes=vmem)


def _row_tile(lp, max_blocks):
    nb = lp // TB
    d = max(k for k in range(1, max_blocks + 1) if nb % k == 0)
    return TB * d


def _sig(x):
    return jax.nn.sigmoid(x)


def _dsilu(x, s):
    return s * (1.0 + x * (1.0 - s))


def _ln_core(z):
    mu = jnp.mean(z, axis=-1, keepdims=True)
    zc = z - mu
    var = jnp.mean(zc * zc, axis=-1, keepdims=True)
    rstd = lax.rsqrt(var + LN_EPS)
    return zc * rstd, rstd


def _ln_bwd_core(dy, xh, rstd, g):
    dxh = dy * g
    m1 = jnp.mean(dxh, axis=-1, keepdims=True)
    m2 = jnp.mean(dxh * xh, axis=-1, keepdims=True)
    return rstd * (dxh - m1 - xh * m2)


def _row_ids(shape, base):
    return lax.broadcasted_iota(jnp.int32, shape, 0) + base


def _colsum(x):
    return jnp.sum(x, axis=0, keepdims=True)


def _dot(a, b, dims):
    return lax.dot_general(a, b, (dims, ((), ())), preferred_element_type=F32)


NN = ((1,), (0,))
NT = ((1,), (1,))
TN = ((0,), (0,))


def _embed(x, meta_full, g, b):
    s = x.shape[1]
    lp = s + TB
    nb = lp // TB

    def body(x_ref, m_ref, g_ref, b_ref, x_hbm, z_hbm, hb_ref, head_ref, sems):
        i = pl.program_id(0)
        copies = [pltpu.make_async_copy(head_ref, z_hbm.at[pl.ds(0, TB), :], sems.at[0])]
        copies += [pltpu.make_async_copy(x_hbm.at[0, pl.ds(k * TB, TB), :], z_hbm.at[pl.ds((k + 1) * TB, TB), :],
                                         sems.at[k + 1]) for k in range(nb - 1)]

        def normed(t):
            xh, _ = _ln_core(t)
            h = xh * g_ref[...] + b_ref[...]
            return jnp.where(_row_ids(h.shape, i * TB) >= PAD0, h, 0.0).astype(BF16)

        @pl.when(i == 0)
        def _():
            head_ref[0:PAD0, :] = jnp.zeros((PAD0, D_MODEL), F32)
            head_ref[PAD0:TB, :] = m_ref[...]
            for cp in copies:
                cp.start()
            hb_ref[...] = normed(head_ref[...])

        @pl.when(i > 0)
        def _():
            hb_ref[...] = normed(x_ref[...])

        @pl.when(i == nb - 1)
        def _():
            for cp in copies:
                cp.wait()

    return pl.pallas_call(
        body, name="embed", grid=(nb,),
        in_specs=[pl.BlockSpec((None, TB, D_MODEL), lambda i: (0, jnp.maximum(i - 1, 0), 0)),
                  pl.BlockSpec((N_META, D_MODEL), lambda i: (0, 0)),
                  pl.BlockSpec((1, D_MODEL), lambda i: (0, 0)),
                  pl.BlockSpec((1, D_MODEL), lambda i: (0, 0)),
                  pl.BlockSpec(memory_space=pl.ANY)],
        out_specs=[pl.BlockSpec(memory_space=pl.ANY),
                   pl.BlockSpec((TB, D_MODEL), lambda i: (i, 0))],
        out_shape=[jax.ShapeDtypeStruct((lp, D_MODEL), F32),
                   jax.ShapeDtypeStruct((lp, D_MODEL), BF16)],
        scratch_shapes=[pltpu.VMEM((TB, D_MODEL), F32), pltpu.SemaphoreType.DMA((nb,))],
        compiler_params=_cp(1),
    )(x, meta_full, g, b, x)


def _loss_head(z, target, g, b):
    lp = z.shape[0]
    nb = lp // TB

    def body(z_ref, t_ref, g_ref, b_ref, dz_ref, st_ref, loss_ref):
        i = pl.program_id(0)

        @pl.when(i == 0)
        def _():
            st_ref[...] = jnp.zeros(st_ref.shape, F32)
            loss_ref[...] = jnp.zeros(loss_ref.shape, F32)
            dz_ref[...] = jnp.zeros(dz_ref.shape, F32)

        @pl.when(i > 0)
        def _():
            xh, rstd = _ln_core(z_ref[...])
            gg = g_ref[...]
            y = xh * gg + b_ref[...]
            e = y - t_ref[...]
            part = 0.5 * jnp.sum(jnp.mean(e * e, axis=-1, keepdims=True), axis=0, keepdims=True)
            loss_ref[...] += jnp.broadcast_to(part, loss_ref.shape)
            dy = e / float(D_MODEL)
            st_ref[0:1, :] += _colsum(dy * xh)
            st_ref[1:2, :] += _colsum(dy)
            dz_ref[...] = _ln_bwd_core(dy, xh, rstd, gg)

    return pl.pallas_call(
        body, name="loss_head", grid=(nb,),
        in_specs=[pl.BlockSpec((TB, D_MODEL), lambda i: (i, 0)),
                  pl.BlockSpec((None, TB, D_MODEL), lambda i: (0, jnp.maximum(i - 1, 0), 0)),
                  pl.BlockSpec((1, D_MODEL), lambda i: (0, 0)),
                  pl.BlockSpec((1, D_MODEL), lambda i: (0, 0))],
        out_specs=[pl.BlockSpec((TB, D_MODEL), lambda i: (i, 0)),
                   pl.BlockSpec((8, D_MODEL), lambda i: (0, 0)),
                   pl.BlockSpec((8, 128), lambda i: (0, 0))],
        out_shape=[jax.ShapeDtypeStruct((lp, D_MODEL), F32),
                   jax.ShapeDtypeStruct((8, D_MODEL), F32),
                   jax.ShapeDtypeStruct((8, 128), F32)],
        compiler_params=_cp(1),
    )(z, target, g, b)


def _ln_bwd(name, dh, z, g):
    lp = z.shape[0]
    tr = _row_tile(lp, 3)

    def body(dh_ref, z_ref, g_ref, dz_ref, st_ref):
        i = pl.program_id(0)

        @pl.when(i == 0)
        def _():
            st_ref[...] = jnp.zeros(st_ref.shape, F32)

        xh, rstd = _ln_core(z_ref[...])
        rows = _row_ids(xh.shape, i * tr)
        dy = jnp.where(rows >= PAD0, dh_ref[...], 0.0)
        st_ref[0:1, :] += _colsum(dy * xh)
        st_ref[1:2, :] += _colsum(dy)
        dz_ref[...] = _ln_bwd_core(dy, xh, rstd, g_ref[...])

    return pl.pallas_call(
        body, name=name, grid=(lp // tr,),
        in_specs=[pl.BlockSpec((tr, D_MODEL), lambda i: (i, 0)),
                  pl.BlockSpec((tr, D_MODEL), lambda i: (i, 0)),
                  pl.BlockSpec((1, D_MODEL), lambda i: (0, 0))],
        out_specs=[pl.BlockSpec((tr, D_MODEL), lambda i: (i, 0)),
                   pl.BlockSpec((8, D_MODEL), lambda i: (0, 0))],
        out_shape=[jax.ShapeDtypeStruct((lp, D_MODEL), F32),
                   jax.ShapeDtypeStruct((8, D_MODEL), F32)],
        compiler_params=_cp(1),
    )(dh, z, g)


def _ln_bwd_input(dh, z, g):
    lp = z.shape[0]
    nb = lp // TB
    s = lp - TB

    def body(dh_ref, z_ref, g_ref, gx_ref, gm_ref, st_ref):
        i = pl.program_id(0)

        @pl.when(i == 0)
        def _():
            st_ref[...] = jnp.zeros(st_ref.shape, F32)

        xh, rstd = _ln_core(z_ref[...])
        rows = _row_ids(xh.shape, i * TB)
        dy = jnp.where(rows >= PAD0, dh_ref[...], 0.0)
        st_ref[0:1, :] += _colsum(dy * xh)
        st_ref[1:2, :] += _colsum(dy)
        dz = _ln_bwd_core(dy, xh, rstd, g_ref[...])
        gx_ref[...] = dz

        @pl.when(i == 0)
        def _():
            gm_ref[...] = dz[PAD0:TB, :]

    return pl.pallas_call(
        body, name="ln_in_bwd", grid=(nb,),
        in_specs=[pl.BlockSpec((TB, D_MODEL), lambda i: (i, 0)),
                  pl.BlockSpec((TB, D_MODEL), lambda i: (i, 0)),
                  pl.BlockSpec((1, D_MODEL), lambda i: (0, 0))],
        out_specs=[pl.BlockSpec((None, TB, D_MODEL), lambda i: (0, jnp.maximum(i - 1, 0), 0)),
                   pl.BlockSpec((N_META, D_MODEL), lambda i: (0, 0)),
                   pl.BlockSpec((8, D_MODEL), lambda i: (0, 0))],
        out_shape=[jax.ShapeDtypeStruct((1, s, D_MODEL), F32),
                   jax.ShapeDtypeStruct((N_META, D_MODEL), F32),
                   jax.ShapeDtypeStruct((8, D_MODEL), F32)],
        compiler_params=_cp(1),
    )(dh, z, g)


def _mm_proj(name, hb, wg_in):
    lp = hb.shape[0]
    tm = lp // 3

    def body(a_ref, b_ref, o_ref):
        b = jnp.concatenate([b_ref[0], b_ref[1]], axis=1)
        o_ref[...] = _dot(a_ref[...], b, NN)

    return pl.pallas_call(
        body, name=name, grid=(3, N_DEV // 2),
        in_specs=[pl.BlockSpec((tm, D_MODEL), lambda i, j: (i, 0)),
                  pl.BlockSpec((2, D_MODEL, W_IN_SHARD), lambda i, j: (j, 0, 0))],
        out_specs=pl.BlockSpec((tm, 2 * W_IN_SHARD), lambda i, j: (i, j)),
        out_shape=jax.ShapeDtypeStruct((lp, IN_TOTAL), F32),
        compiler_params=_cp(2),
    )(hb, wg_in)


def _mm_out(name, ycat, wout, z, g, b, g2, b2, dep):
    lp = ycat.shape[0]
    tm = lp // 6
    with_next = g2 is not None

    def body(a_ref, w_ref, z_ref, g_ref, b_ref, *rest):
        i = pl.program_id(0)
        xh, _ = _ln_core(z_ref[...])
        h = xh * g_ref[...] + b_ref[...]
        live = _row_ids(h.shape, i * tm) >= PAD0
        h = jnp.where(live, h, 0.0)
        zn = ALPHA * h + _dot(a_ref[...], w_ref[...], NN)
        if with_next:
            g2_ref, b2_ref, _, o_ref, hb_ref = rest
            xh2, _ = _ln_core(zn)
            hb_ref[...] = jnp.where(live, xh2 * g2_ref[...] + b2_ref[...], 0.0).astype(BF16)
        else:
            _, o_ref = rest
        o_ref[...] = zn

    vec = pl.BlockSpec((1, D_MODEL), lambda i: (0, 0))
    row_blk = pl.BlockSpec((tm, D_MODEL), lambda i: (i, 0))
    outs = pl.pallas_call(
        body, name=name, grid=(6,),
        in_specs=([row_blk, pl.BlockSpec((D_MODEL, D_MODEL), lambda i: (0, 0), pipeline_mode=pl.Buffered(1)),
                   row_blk, vec, vec] + ([vec, vec] if with_next else [])
                  + [pl.BlockSpec(memory_space=pl.ANY)]),
        out_specs=[row_blk, row_blk] if with_next else [row_blk],
        out_shape=([jax.ShapeDtypeStruct((lp, D_MODEL), F32)]
                   + ([jax.ShapeDtypeStruct((lp, D_MODEL), BF16)] if with_next else [])),
        compiler_params=_cp(1),
    )(ycat, wout, z, g, b, *((g2, b2) if with_next else ()), dep)
    return (outs[0], outs[1]) if with_next else (outs[0], None)


def _mm_dycat(name, dz, wout, dep):
    lp = dz.shape[0]
    tm = lp // 6

    def body(a_ref, w_ref, dep_ref, o_ref):
        del dep_ref
        o_ref[...] = _dot(a_ref[...].astype(BF16), w_ref[...], NT).astype(BF16)

    return pl.pallas_call(
        body, name=name, grid=(6,),
        in_specs=[pl.BlockSpec((tm, D_MODEL), lambda i: (i, 0)),
                  pl.BlockSpec((D_MODEL, D_MODEL), lambda i: (0, 0), pipeline_mode=pl.Buffered(1)),
                  pl.BlockSpec(memory_space=pl.ANY)],
        out_specs=pl.BlockSpec((tm, D_MODEL), lambda i: (i, 0)),
        out_shape=jax.ShapeDtypeStruct((lp, D_MODEL), BF16),
        compiler_params=_cp(1),
    )(dz, wout, dep)


def _mm_dwout(name, ycat, dz):
    lp = ycat.shape[0]
    tk = _row_tile(lp, 11)
    nk = lp // tk
    half = D_MODEL // 2

    def body(a_ref, b_ref, o_ref, acc_ref):
        k = pl.program_id(1)

        @pl.when(k == 0)
        def _():
            acc_ref[...] = jnp.zeros(acc_ref.shape, F32)

        acc_ref[...] += _dot(a_ref[...], b_ref[...].astype(BF16), TN)

        @pl.when(k == nk - 1)
        def _():
            o_ref[...] = acc_ref[...].astype(BF16)

    return pl.pallas_call(
        body, name=name, grid=(2, nk),
        in_specs=[pl.BlockSpec((tk, half), lambda h, k: (k, h)),
                  pl.BlockSpec((tk, D_MODEL), lambda h, k: (k, 0))],
        out_specs=pl.BlockSpec((half, D_MODEL), lambda h, k: (h, 0)),
        out_shape=jax.ShapeDtypeStruct((D_MODEL, D_MODEL), BF16),
        scratch_shapes=[pltpu.VMEM((half, D_MODEL), F32)],
        compiler_params=_cp(2),
    )(ycat, dz)


def _mm_dwin(name, hb, dproj, dep):
    lp = hb.shape[0]
    tk = _row_tile(lp, 11)
    nk = lp // tk

    def body(a_ref, b_ref, dep_ref, o_ref, acc_ref):
        del dep_ref
        k = pl.program_id(1)

        @pl.when(k == 0)
        def _():
            acc_ref[...] = jnp.zeros(acc_ref.shape, F32)

        acc_ref[...] += _dot(a_ref[...], b_ref[...], TN)

        @pl.when(k == nk - 1)
        def _():
            o_ref[0] = acc_ref[:, 0:W_IN_SHARD].astype(BF16)
            o_ref[1] = acc_ref[:, W_IN_SHARD:2 * W_IN_SHARD].astype(BF16)

    return pl.pallas_call(
        body, name=name, grid=(4, nk),
        in_specs=[pl.BlockSpec((tk, D_MODEL), lambda j, k: (k, 0)),
                  pl.BlockSpec((tk, 2 * W_IN_SHARD), lambda j, k: (k, j)),
                  pl.BlockSpec(memory_space=pl.ANY)],
        out_specs=pl.BlockSpec((2, D_MODEL, W_IN_SHARD), lambda j, k: (j, 0, 0)),
        out_shape=jax.ShapeDtypeStruct((N_DEV, D_MODEL, W_IN_SHARD), BF16),
        scratch_shapes=[pltpu.VMEM((D_MODEL, 2 * W_IN_SHARD), F32)],
        compiler_params=_cp(2),
    )(hb, dproj, dep)


def _mm_dh(name, dproj, wg_in, dz, dep):
    lp = dproj.shape[0]
    tm = lp // 6

    def body(a_ref, w_ref, dz_ref, dep_ref, o_ref, acc_ref):
        del dep_ref
        k = pl.program_id(1)

        @pl.when(k == 0)
        def _():
            acc_ref[...] = jnp.zeros(acc_ref.shape, F32)

        w = jnp.concatenate([w_ref[0], w_ref[1]], axis=1)
        acc_ref[...] += _dot(a_ref[...], w, NT)

        @pl.when(k == N_DEV // 2 - 1)
        def _():
            o_ref[...] = acc_ref[...] + ALPHA * dz_ref[...]

    return pl.pallas_call(
        body, name=name, grid=(6, N_DEV // 2),
        in_specs=[pl.BlockSpec((tm, 2 * W_IN_SHARD), lambda i, k: (i, k)),
                  pl.BlockSpec((2, D_MODEL, W_IN_SHARD), lambda i, k: (k, 0, 0)),
                  pl.BlockSpec((tm, D_MODEL), lambda i, k: (i, 0)),
                  pl.BlockSpec(memory_space=pl.ANY)],
        out_specs=pl.BlockSpec((tm, D_MODEL), lambda i, k: (i, 0)),
        out_shape=jax.ShapeDtypeStruct((lp, D_MODEL), F32),
        scratch_shapes=[pltpu.VMEM((tm, D_MODEL), F32)],
        compiler_params=_cp(2),
    )(dproj, wg_in, dz, dep)


SUB = 128


def _shift_plan(cat, n_shift, base):
    rolled = [cat] + [pltpu.roll(cat, b, axis=0) for b in range(1, 8)]
    return [(rolled[s % 8], base - 8 * (s // 8)) for s in range(n_shift)]


def _tap_sum(w_ref, plan, rows, init=None):
    blocks = []
    for r0 in range(0, rows, SUB):
        row = []
        for c0 in range(0, CONV_W, SUB):
            acc = (jnp.zeros((SUB, SUB), F32) if init is None
                   else jnp.broadcast_to(init[:, c0:c0 + SUB], (SUB, SUB)))
            for k, (arr, off) in enumerate(plan):
                acc = acc + w_ref[k:k + 1, c0:c0 + SUB] * arr[off + r0:off + r0 + SUB, c0:c0 + SUB]
            row.append(acc)
        blocks.append(jnp.concatenate(row, axis=1))
    return jnp.concatenate(blocks, axis=0)


def _tap_grads(dw_ref, dy, plan, rows):
    for c0 in range(0, CONV_W, SUB):
        dys = [dy[r0:r0 + SUB, c0:c0 + SUB] for r0 in range(0, rows, SUB)]
        for k, (arr, off) in enumerate(plan):
            part = None
            for ri, r0 in enumerate(range(0, rows, SUB)):
                prod = dys[ri] * arr[off + r0:off + r0 + SUB, c0:c0 + SUB]
                for i in range(SUB // 8):
                    piece = prod[8 * i:8 * i + 8, :]
                    part = piece if part is None else part + piece
            dw_ref[k:k + 1, c0:c0 + SUB] += jnp.sum(part, axis=0, keepdims=True)


CONV_HALO = 32


def _conv_chain(j, tb, cv_ref, cg_ref, cvp_ref, cgp_ref, wdw_ref, vec_ref, wpw_ref, c1_ref=None):
    cv = cv_ref[...]
    sg = _sig(cg_ref[...])
    c0 = cv * sg
    c0p = jnp.where(j > 0, cvp_ref[...] * _sig(cgp_ref[...]), 0.0)
    cat = jnp.concatenate([c0p, c0], axis=0)
    shifts = _shift_plan(cat, CONV_K, CONV_HALO)
    taps = [shifts[CONV_K - 1 - k] for k in range(CONV_K)]
    if c1_ref is None:
        c1 = _tap_sum(wdw_ref, taps, tb, init=vec_ref[0:1, :])
    else:
        c1 = c1_ref[...]
    xh, rstd = _ln_core(c1)
    c2 = xh * vec_ref[1:2, :] + vec_ref[2:3, :]
    s2 = _sig(c2)
    c3 = c2 * s2
    c4 = _dot(c3.astype(BF16), wpw_ref[...], NN) + vec_ref[3:4, :]
    return dict(cv=cv, sg=sg, taps=taps, c1=c1, xh=xh, rstd=rstd, c2=c2, s2=s2, c3=c3, c4=c4)


def _conv_in_specs(jmap, tb):
    def cur(col):
        return pl.BlockSpec((tb, 512), lambda n: (jmap(n), col))

    def prev(col):
        return pl.BlockSpec((CONV_HALO, 512),
                            lambda n: (jnp.maximum(jmap(n) * (tb // CONV_HALO) - 1, 0), col))

    return [cur(COL_CV), cur(COL_CG), prev(COL_CV), prev(COL_CG), cur(COL_CGATE)]


def _conv_param_specs():
    return [pl.BlockSpec((32, CONV_W), lambda n: (0, 0)),
            pl.BlockSpec((8, CONV_W), lambda n: (0, 0)),
            pl.BlockSpec((CONV_W, CONV_W), lambda n: (0, 0))]


def _conv_fwd(name, proj, wdw, vec, wpw):
    lp = proj.shape[0]
    tb = _row_tile(lp, 3)
    nb = lp // tb

    def body(cv_ref, cg_ref, cvp_ref, cgp_ref, gate_ref, wdw_ref, vec_ref, wpw_ref, o_ref, c1_ref):
        j = pl.program_id(0)
        c = _conv_chain(j, tb, cv_ref, cg_ref, cvp_ref, cgp_ref, wdw_ref, vec_ref, wpw_ref)
        gate = gate_ref[...]
        o_ref[...] = (c["c4"] * (gate * _sig(gate))).astype(BF16)
        c1_ref[...] = c["c1"]

    return pl.pallas_call(
        body, name=name, grid=(nb,),
        in_specs=_conv_in_specs(lambda n: n, tb) + _conv_param_specs(),
        out_specs=[pl.BlockSpec((tb, 512), lambda n: (n, YC_CONV)),
                   pl.BlockSpec((tb, CONV_W), lambda n: (n, 0))],
        out_shape=[jax.ShapeDtypeStruct((lp, D_MODEL), BF16),
                   jax.ShapeDtypeStruct((lp, CONV_W), F32)],
        compiler_params=_cp(1),
    )(proj, proj, proj, proj, proj, wdw, vec, wpw)


def _conv_bwd(name, proj, dycat, c1, wdw, vec, wpw):
    lp = proj.shape[0]
    tb = _row_tile(lp, 3)
    nb = lp // tb
    halo = CONV_HALO

    def body(cv_ref, cg_ref, cvp_ref, cgp_ref, gate_ref, dy_ref, c1_ref, wdw_ref, vec_ref, wpw_ref,
             dp_ref, dwdw_ref, dvec_ref, dwpw_ref, carry_ref):
        n = pl.program_id(0)
        j = nb - 1 - n

        @pl.when(n == 0)
        def _():
            carry_ref[...] = jnp.zeros(carry_ref.shape, F32)
            dwdw_ref[...] = jnp.zeros(dwdw_ref.shape, F32)
            dvec_ref[...] = jnp.zeros(dvec_ref.shape, F32)
            dwpw_ref[...] = jnp.zeros(dwpw_ref.shape, F32)

        c = _conv_chain(j, tb, cv_ref, cg_ref, cvp_ref, cgp_ref, wdw_ref, vec_ref, wpw_ref, c1_ref)
        dy = dy_ref[...].astype(F32)
        gate = gate_ref[...]
        sgate = _sig(gate)
        dc4 = dy * (gate * sgate)
        dgate = dy * c["c4"] * _dsilu(gate, sgate)
        dc4b = dc4.astype(BF16)
        dvec_ref[3:4, :] += _colsum(dc4)
        dwpw_ref[...] += _dot(c["c3"].astype(BF16), dc4b, TN)
        dc3 = _dot(dc4b, wpw_ref[...], NT)
        dc2 = dc3 * _dsilu(c["c2"], c["s2"])
        dvec_ref[1:2, :] += _colsum(dc2 * c["xh"])
        dvec_ref[2:3, :] += _colsum(dc2)
        dc1 = _ln_bwd_core(dc2, c["xh"], c["rstd"], vec_ref[1:2, :])
        dvec_ref[0:1, :] += _colsum(dc1)
        _tap_grads(dwdw_ref, dc1, c["taps"], tb)
        dcat = jnp.concatenate([dc1, carry_ref[...]], axis=0)
        total = tb + halo
        up = [dcat] + [pltpu.roll(dcat, total - b, axis=0) for b in range(1, 8)]
        ahead = [(up[(CONV_K - 1 - k) % 8], 8 * ((CONV_K - 1 - k) // 8)) for k in range(CONV_K)]
        dc0 = _tap_sum(wdw_ref, ahead, tb)
        carry_ref[...] = dc1[0:halo, :]
        sg = c["sg"]
        dcv = dc0 * sg
        dcg = dc0 * c["cv"] * sg * (1.0 - sg)
        dp_ref[:, 0:512] = dcv.astype(BF16)
        dp_ref[:, 512:1024] = dcg.astype(BF16)
        dp_ref[:, 1024:1536] = dgate.astype(BF16)

    jmap = lambda n: nb - 1 - n
    return pl.pallas_call(
        body, name=name, grid=(nb,),
        in_specs=(_conv_in_specs(jmap, tb)
                  + [pl.BlockSpec((tb, 512), lambda n: (jmap(n), YC_CONV)),
                     pl.BlockSpec((tb, CONV_W), lambda n: (jmap(n), 0))]
                  + _conv_param_specs()),
        out_specs=[pl.BlockSpec((tb, 1536), lambda n: (jmap(n), 0)),
                   pl.BlockSpec((32, CONV_W), lambda n: (0, 0)),
                   pl.BlockSpec((8, CONV_W), lambda n: (0, 0)),
                   pl.BlockSpec((CONV_W, CONV_W), lambda n: (0, 0))],
        out_shape=[jax.ShapeDtypeStruct((lp, IN_TOTAL), BF16),
                   jax.ShapeDtypeStruct((32, CONV_W), F32),
                   jax.ShapeDtypeStruct((8, CONV_W), F32),
                   jax.ShapeDtypeStruct((CONV_W, CONV_W), F32)],
        scratch_shapes=[pltpu.VMEM((halo, CONV_W), F32)],
        compiler_params=_cp(1),
    )(proj, proj, proj, proj, proj, dycat, c1, wdw, vec, wpw)


def _rope_tables(lp):
    half = ROT_DIM // 2
    inv_freq = ROPE_THETA ** (-jnp.arange(half, dtype=F32) / half)
    pos = (jnp.arange(lp, dtype=jnp.int32) - PAD0).astype(F32)
    ang = pos[:, None] * inv_freq[None, :]
    cos, sin = jnp.cos(ang), jnp.sin(ang)
    ones = jnp.ones((lp, HEAD_DIM - ROT_DIM), F32)
    zeros = jnp.zeros((lp, HEAD_DIM - ROT_DIM), F32)
    zh = jnp.zeros((lp, half), F32)
    c = jnp.concatenate([cos, cos, ones], axis=1)
    sa = jnp.concatenate([-sin, zh, zeros], axis=1)
    sb = jnp.concatenate([zh, sin, zeros], axis=1)
    tile = lambda t: jnp.tile(t, (1, KV_W // HEAD_DIM))
    return tile(c), tile(sa), tile(sb)


def _rot(x, c, sa, sb):
    w = x.shape[1]
    return x * c + pltpu.roll(x, w - 8, axis=1) * sa + pltpu.roll(x, 8, axis=1) * sb


def _rot_t(dy, c, sa, sb):
    w = dy.shape[1]
    return dy * c + pltpu.roll(dy * sa, 8, axis=1) + pltpu.roll(dy * sb, w - 8, axis=1)


def _rope_fwd(name, proj, tabs):
    lp = proj.shape[0]
    tr = _row_tile(lp, 11)

    def body(q0_ref, q1_ref, k_ref, c_ref, sa_ref, sb_ref, qr_ref, kr_ref):
        c, sa, sb = c_ref[...], sa_ref[...], sb_ref[...]
        c2 = jnp.concatenate([c, c], axis=1)
        sa2 = jnp.concatenate([sa, sa], axis=1)
        sb2 = jnp.concatenate([sb, sb], axis=1)
        qr_ref[:, 0:512] = (_rot(q0_ref[...], c2, sa2, sb2) * ATT_SCALE).astype(BF16)
        qr_ref[:, 512:1024] = (_rot(q1_ref[...], c2, sa2, sb2) * ATT_SCALE).astype(BF16)
        kr_ref[...] = _rot(k_ref[...], c, sa, sb).astype(BF16)

    tab = pl.BlockSpec((tr, KV_W), lambda i: (i, 0))
    return pl.pallas_call(
        body, name=name, grid=(lp // tr,),
        in_specs=[pl.BlockSpec((tr, 512), lambda i: (i, COL_Q0)),
                  pl.BlockSpec((tr, 512), lambda i: (i, COL_Q0 + 1)),
                  pl.BlockSpec((tr, KV_W), lambda i: (i, COL_K256)),
                  tab, tab, tab],
        out_specs=[pl.BlockSpec((tr, ATT_W), lambda i: (i, 0)),
                   pl.BlockSpec((tr, KV_W), lambda i: (i, 0))],
        out_shape=[jax.ShapeDtypeStruct((lp, ATT_W), BF16),
                   jax.ShapeDtypeStruct((lp, KV_W), BF16)],
        compiler_params=_cp(1),
    )(proj, proj, proj, *tabs)


def _attn_mask(j):
    qi = lax.broadcasted_iota(jnp.int32, (GROUP * TB, 3 * TB), 0) & (TB - 1)
    cc = lax.broadcasted_iota(jnp.int32, (GROUP * TB, 3 * TB), 1)
    jj = cc & (TB - 1)
    is_meta = jj >= PAD0
    p0 = (cc < TB) & is_meta & (j >= 1)
    p1 = (cc >= TB) & (cc < 2 * TB) & (jj > qi) & (j >= 2)
    p2 = (cc >= 2 * TB) & (jj <= qi) & ((j >= 1) | is_meta)
    return p0 | p1 | p2


def _lane_group(rows):
    return lax.broadcasted_iota(jnp.int32, (rows, KV_W), 1) // HEAD_DIM


def _stack_heads(x, kv, lgq):
    parts = []
    for g in range(GROUP):
        sh = ((kv - g) % GROUP) * HEAD_DIM
        moved = x if sh == 0 else pltpu.roll(x, sh, axis=1)
        parts.append(jnp.where(lgq == kv, moved, 0.0))
    return jnp.concatenate(parts, axis=0).astype(BF16)


def _unstack_heads(r, kv):
    out = None
    for g in range(GROUP):
        blk = r[g * TB:(g + 1) * TB, :]
        sh = ((g - kv) % GROUP) * HEAD_DIM
        blk = blk if sh == 0 else pltpu.roll(blk, sh, axis=1)
        out = blk if out is None else out + blk
    return out


def _sink_column(sinks, kv):
    lane = lax.broadcasted_iota(jnp.int32, (1, 128), 1)
    cols = []
    for g in range(GROUP):
        sg = jnp.sum(jnp.where(lane == kv * GROUP + g, sinks, 0.0), axis=1, keepdims=True)
        cols.append(jnp.broadcast_to(sg, (TB, 1)))
    return jnp.concatenate(cols, axis=0)


def _attn_kv(kall, vall, lg, kv):
    km = jnp.where(lg == kv, kall, 0.0).astype(BF16)
    vm = jnp.where(lg == kv, vall, 0.0).astype(BF16)
    ones = jnp.where(lg == kv, 1.0, 0.0).astype(BF16)
    return km, vm, ones


def _attn_specs(jmap):
    blk = lambda col: pl.BlockSpec((TB, KV_W), lambda n: (jmap(n), col))
    prv = lambda col: pl.BlockSpec((TB, KV_W), lambda n: (jnp.maximum(jmap(n) - 1, 0), col))
    met = lambda col: pl.BlockSpec((TB, KV_W), lambda n: (0, col))
    return dict(
        qr=pl.BlockSpec((TB, ATT_W), lambda n: (jmap(n), 0)),
        k=[met(0), prv(0), blk(0)],
        v=[met(COL_V256), prv(COL_V256), blk(COL_V256)],
        gate=pl.BlockSpec((TB, ATT_W), lambda n: (jmap(n), COL_AGATE1024)),
        sinks=pl.BlockSpec((8, 128), lambda n: (0, 0)),
    )


ATTN_BWD_HEAD_SETS = ((0, 1, 2, 3),)


def _attn_fwd(name, qr, kr, proj, sinks_row, ycat):
    lp = proj.shape[0]
    nb = lp // TB
    sp = _attn_specs(lambda n: n)

    def body(qr_ref, km_ref, kp_ref, kc_ref, vm_ref, vp_ref, vc_ref, gate_ref, sink_ref, yin_ref, o_ref):
        del yin_ref
        j = pl.program_id(0)
        valid = _attn_mask(j)
        kall = jnp.concatenate([km_ref[...], kp_ref[...], kc_ref[...]], axis=0).astype(F32)
        vall = jnp.concatenate([vm_ref[...], vp_ref[...], vc_ref[...]], axis=0)
        lg = _lane_group(3 * TB)
        lgq = _lane_group(TB)
        lg4 = _lane_group(GROUP * TB)
        sinks = sink_ref[0:1, :]
        heads = range(N_KV)
        cols = [slice(kv * KV_W, (kv + 1) * KV_W) for kv in heads]
        kvo = [_attn_kv(kall, vall, lg, kv) for kv in heads]
        qst = [_stack_heads(qr_ref[:, cols[kv]].astype(F32), kv, lgq) for kv in heads]
        s = [jnp.where(valid, _dot(qst[kv], kvo[kv][0], NT), NEG_INF) for kv in heads]
        eb, es = [], []
        for kv in heads:
            sinkcol = _sink_column(sinks, kv)
            m = jnp.maximum(jnp.max(s[kv], axis=-1, keepdims=True), sinkcol)
            eb.append(jnp.exp(s[kv] - m).astype(BF16))
            es.append(jnp.exp(sinkcol - m))
        r = [_dot(eb[kv], kvo[kv][1], NN) for kv in heads]
        inv = [1.0 / (_dot(eb[kv], kvo[kv][2], NN) + es[kv]) for kv in heads]
        for kv in heads:
            out = jnp.where(lg4 == kv, r[kv] * inv[kv], 0.0)
            gate = gate_ref[:, cols[kv]]
            o_ref[:, cols[kv]] = (_unstack_heads(out, kv) * (gate * _sig(gate))).astype(BF16)

    return pl.pallas_call(
        body, name=name, grid=(nb,),
        in_specs=[sp["qr"]] + sp["k"] + sp["v"] + [sp["gate"], sp["sinks"],
                                                   pl.BlockSpec(memory_space=pl.ANY)],
        out_specs=pl.BlockSpec((TB, ATT_W), lambda n: (n, 0)),
        out_shape=jax.ShapeDtypeStruct((lp, D_MODEL), BF16),
        input_output_aliases={9: 0},
        compiler_params=_cp(1),
    )(qr, kr, kr, kr, proj, proj, proj, proj, sinks_row, ycat)


def _attn_bwd(name, qr, kr, proj, sinks_row, dycat, dep):
    lp = proj.shape[0]
    nb = lp // TB
    sp = _attn_specs(lambda n: n)

    def body(qr_ref, km_ref, kp_ref, kc_ref, vm_ref, vp_ref, vc_ref, gate_ref, sink_ref, dy_ref, dep_ref,
             dq_ref, dgate_ref, dk_ref, dv_ref, dsink_ref):
        del dep_ref
        j = pl.program_id(0)

        @pl.when(j == 0)
        def _():
            dk_ref[...] = jnp.zeros(dk_ref.shape, F32)
            dv_ref[...] = jnp.zeros(dv_ref.shape, F32)
            dsink_ref[...] = jnp.zeros(dsink_ref.shape, F32)

        valid = _attn_mask(j)
        kall = jnp.concatenate([km_ref[...], kp_ref[...], kc_ref[...]], axis=0).astype(F32)
        vall = jnp.concatenate([vm_ref[...], vp_ref[...], vc_ref[...]], axis=0)
        lg = _lane_group(3 * TB)
        lgq = _lane_group(TB)
        lg4 = _lane_group(GROUP * TB)
        sinks = sink_ref[0:1, :]
        lane = lax.broadcasted_iota(jnp.int32, (1, 128), 1)
        def stages(heads):
            dsink = jnp.zeros((1, 128), F32)
            cols = {kv: slice(kv * KV_W, (kv + 1) * KV_W) for kv in heads}
            kvo = {kv: _attn_kv(kall, vall, lg, kv) for kv in heads}
            qst = {kv: _stack_heads(qr_ref[:, cols[kv]].astype(F32), kv, lgq) for kv in heads}
            s = {kv: jnp.where(valid, _dot(qst[kv], kvo[kv][0], NT), NEG_INF) for kv in heads}
            eb, es = {}, {}
            for kv in heads:
                sinkcol = _sink_column(sinks, kv)
                m = jnp.maximum(jnp.max(s[kv], axis=-1, keepdims=True), sinkcol)
                eb[kv] = jnp.exp(s[kv] - m).astype(BF16)
                es[kv] = jnp.exp(sinkcol - m)
            r = {kv: _dot(eb[kv], kvo[kv][1], NN) for kv in heads}
            inv = {kv: 1.0 / (_dot(eb[kv], kvo[kv][2], NN) + es[kv]) for kv in heads}
            dost, dcol = {}, {}
            for kv in heads:
                att = _unstack_heads(jnp.where(lg4 == kv, r[kv] * inv[kv], 0.0), kv)
                gate = gate_ref[:, cols[kv]]
                sgate = _sig(gate)
                dy = dy_ref[:, cols[kv]].astype(F32)
                dgate_ref[:, cols[kv]] = (dy * att * _dsilu(gate, sgate)).astype(BF16)
                dsc = dy * (gate * sgate) * _unstack_heads(jnp.where(lg4 == kv, inv[kv], 0.0), kv)
                dost[kv] = _stack_heads(dsc, kv, lgq)
                dd = dsc * att
                dcol[kv] = jnp.concatenate(
                    [jnp.sum(jnp.where(lgq == g, dd, 0.0), axis=1, keepdims=True) for g in range(GROUP)], axis=0)
            dp = {kv: _dot(dost[kv], kvo[kv][1], NT) for kv in heads}
            ds = {}
            for kv in heads:
                ds[kv] = (eb[kv].astype(F32) * (dp[kv] - dcol[kv])).astype(BF16)
                pd = es[kv] * dcol[kv]
                for g in range(GROUP):
                    tot = jnp.sum(pd[g * TB:(g + 1) * TB, :], axis=0, keepdims=True)
                    dsink = dsink - jnp.where(lane == kv * GROUP + g, tot, 0.0)
            dqs = {kv: _dot(ds[kv], kvo[kv][0], NN) for kv in heads}
            dks = [_dot(ds[kv], qst[kv], TN) for kv in heads]
            dvs = [_dot(eb[kv], dost[kv], TN) for kv in heads]
            for kv in heads:
                dq_ref[:, cols[kv]] = _unstack_heads(dqs[kv], kv)
            return sum(dks[1:], dks[0]), sum(dvs[1:], dvs[0]), dsink

        parts = [stages(hs) for hs in ATTN_BWD_HEAD_SETS]
        dkall = sum([p[0] for p in parts[1:]], parts[0][0])
        dvall = sum([p[1] for p in parts[1:]], parts[0][1])
        dsink = sum([p[2] for p in parts[1:]], parts[0][2])
        dsink_ref[0:1, :] += dsink
        prev = pl.multiple_of(jnp.maximum(j - 1, 0) * TB, TB)
        cur = pl.multiple_of(j * TB, TB)
        dk_ref[0:TB, :] += dkall[0:TB]
        dv_ref[0:TB, :] += dvall[0:TB]
        dk_ref[pl.ds(prev, TB), :] += dkall[TB:2 * TB]
        dv_ref[pl.ds(prev, TB), :] += dvall[TB:2 * TB]
        dk_ref[pl.ds(cur, TB), :] += dkall[2 * TB:3 * TB]
        dv_ref[pl.ds(cur, TB), :] += dvall[2 * TB:3 * TB]

    return pl.pallas_call(
        body, name=name, grid=(nb,),
        in_specs=[sp["qr"]] + sp["k"] + sp["v"] + [sp["gate"], sp["sinks"],
                                                   pl.BlockSpec((TB, ATT_W), lambda n: (n, 0)),
                                                   pl.BlockSpec(memory_space=pl.ANY)],
        out_specs=[pl.BlockSpec((TB, ATT_W), lambda n: (n, 0)),
                   pl.BlockSpec((TB, ATT_W), lambda n: (n, 0)),
                   pl.BlockSpec((lp, KV_W), lambda n: (0, 0)),
                   pl.BlockSpec((lp, KV_W), lambda n: (0, 0)),
                   pl.BlockSpec((8, 128), lambda n: (0, 0))],
        out_shape=[jax.ShapeDtypeStruct((lp, ATT_W), F32),
                   jax.ShapeDtypeStruct((lp, ATT_W), BF16),
                   jax.ShapeDtypeStruct((lp, KV_W), F32),
                   jax.ShapeDtypeStruct((lp, KV_W), F32),
                   jax.ShapeDtypeStruct((8, 128), F32)],
        compiler_params=_cp(1),
    )(qr, kr, kr, kr, proj, proj, proj, proj, sinks_row, dycat, dep)


def _attn_assemble(name, dq, dgate, dk, dv, tabs, dproj):
    lp = dq.shape[0]
    tr = _row_tile(lp, 11)

    def body(dq_ref, dg_ref, dk_ref, dv_ref, c_ref, sa_ref, sb_ref, din_ref, o_ref):
        del din_ref
        cidx = pl.program_id(1)
        c, sa, sb = c_ref[...], sa_ref[...], sb_ref[...]

        @pl.when(cidx < 2)
        def _():
            c2 = jnp.concatenate([c, c], axis=1)
            sa2 = jnp.concatenate([sa, sa], axis=1)
            sb2 = jnp.concatenate([sb, sb], axis=1)
            o_ref[...] = (_rot_t(dq_ref[...], c2, sa2, sb2) * ATT_SCALE).astype(BF16)

        @pl.when(cidx == 2)
        def _():
            o_ref[:, 0:KV_W] = _rot_t(dk_ref[...], c, sa, sb).astype(BF16)
            o_ref[:, KV_W:2 * KV_W] = dv_ref[...].astype(BF16)

        @pl.when(cidx > 2)
        def _():
            o_ref[...] = dg_ref[...]

    tab = pl.BlockSpec((tr, KV_W), lambda n, c: (n, 0))
    return pl.pallas_call(
        body, name=name, grid=(lp // tr, 5),
        in_specs=[pl.BlockSpec((tr, 512), lambda n, c: (n, jnp.minimum(c, 1))),
                  pl.BlockSpec((tr, 512), lambda n, c: (n, jnp.clip(c - 3, 0, 1))),
                  tab, tab,
                  tab, tab, tab,
                  pl.BlockSpec(memory_space=pl.ANY)],
        out_specs=pl.BlockSpec((tr, 512), lambda n, c: (n, COL_Q0 + c)),
        out_shape=jax.ShapeDtypeStruct((lp, IN_TOTAL), BF16),
        input_output_aliases={7: 0},
        compiler_params=_cp(2),
    )(dq, dgate, dk, dv, *tabs, dproj)


def _softplus_neg(lam):
    t = jnp.exp(-jnp.abs(lam))
    u = 1.0 + t
    den = jnp.where(u == 1.0, 1.0, u - 1.0)
    l1p = jnp.where(u == 1.0, t, jnp.log(u) * (t / den))
    return jnp.maximum(-lam, 0.0) + l1p


def _lru_chain(j, tb, rx_ref, rxp_ref, wl_ref, vec_ref, wa_ref, wx_ref):
    rx = rx_ref[...]
    rxp = jnp.where(j > 0, rxp_ref[...], 0.0)
    cat = jnp.concatenate([rxp, rx], axis=0)
    views = [cat[8:8 + tb, :]] + [pltpu.roll(cat, s, axis=0)[8:8 + tb, :] for s in range(1, LRU_CONV_K)]
    x1 = jnp.broadcast_to(vec_ref[0:1, :], (tb, LRU_W))
    for k in range(LRU_CONV_K):
        x1 = x1 + wl_ref[k:k + 1, :] * views[LRU_CONV_K - 1 - k]
    x1b = x1.astype(BF16)
    r = _sig(_dot(x1b, wa_ref[...], NN) + vec_ref[1:2, :])
    ig = _sig(_dot(x1b, wx_ref[...], NN) + vec_ref[2:3, :])
    sp = _softplus_neg(vec_ref[3:4, :])
    log_a = -LRU_C * r * sp
    rows = _row_ids((tb, LRU_W), j * tb)
    live = rows >= PAD0
    a = jnp.where(live, jnp.exp(log_a), 0.0)
    y2 = 2.0 * log_a
    em = -jnp.tanh(0.5 * y2) * (jnp.exp(y2) + 1.0)
    mult = jnp.sqrt(em)
    return dict(views=views, x1=x1, x1b=x1b, r=r, ig=ig, sp=sp, a=a, mult=mult, live=live, a_raw=jnp.exp(log_a))


def _scan_slabs(a, u, forward):
    tb = a.shape[0]
    rows = lax.broadcasted_iota(jnp.int32, (tb, SUB), 0)
    outs_a, outs_u = [], []
    for c0 in range(0, a.shape[1], SUB):
        ac, uc = a[:, c0:c0 + SUB], u[:, c0:c0 + SUB]
        d = 1
        while d < tb:
            if forward:
                keep, sh = rows >= d, d
            else:
                keep, sh = rows < tb - d, tb - d
            an = jnp.where(keep, pltpu.roll(ac, sh, axis=0), 1.0)
            un = jnp.where(keep, pltpu.roll(uc, sh, axis=0), 0.0)
            uc = ac * un + uc
            ac = ac * an
            d *= 2
        outs_a.append(ac)
        outs_u.append(uc)
    return jnp.concatenate(outs_a, axis=1), jnp.concatenate(outs_u, axis=1)


def _lru_specs(jmap, tb):
    return [pl.BlockSpec((tb, 512), lambda n: (jmap(n), COL_RX)),
            pl.BlockSpec((8, 512), lambda n: (jnp.maximum(jmap(n) * (tb // 8) - 1, 0), COL_RX)),
            pl.BlockSpec((tb, 512), lambda n: (jmap(n), COL_RGATE))]


def _lru_param_specs():
    return [pl.BlockSpec((8, LRU_W), lambda n: (0, 0)),
            pl.BlockSpec((8, LRU_W), lambda n: (0, 0)),
            pl.BlockSpec((LRU_W, LRU_W), lambda n: (0, 0)),
            pl.BlockSpec((LRU_W, LRU_W), lambda n: (0, 0))]


def _lru_fwd(name, proj, wl, vec, wa, wx, ycat):
    lp = proj.shape[0]
    tb = _row_tile(lp, 3)
    nb = lp // tb

    def body(rx_ref, rxp_ref, gate_ref, wl_ref, vec_ref, wa_ref, wx_ref, yin_ref, o_ref, h_ref, carry_ref):
        del yin_ref
        j = pl.program_id(0)

        @pl.when(j == 0)
        def _():
            carry_ref[...] = jnp.zeros(carry_ref.shape, F32)

        c = _lru_chain(j, tb, rx_ref, rxp_ref, wl_ref, vec_ref, wa_ref, wx_ref)
        u = jnp.where(c["live"], c["mult"] * (c["ig"] * c["x1"]), 0.0)
        a, u = _scan_slabs(c["a"], u, forward=True)
        h = u + a * carry_ref[0:1, :]
        carry_ref[...] = h[tb - 8:tb, :]
        carry_ref[0:1, :] = h[tb - 1:tb, :]
        h_ref[...] = h
        gate = gate_ref[...]
        o_ref[...] = (h * (gate * _sig(gate))).astype(BF16)

    return pl.pallas_call(
        body, name=name, grid=(nb,),
        in_specs=_lru_specs(lambda n: n, tb) + _lru_param_specs() + [pl.BlockSpec(memory_space=pl.ANY)],
        out_specs=[pl.BlockSpec((tb, 512), lambda n: (n, YC_LRU)),
                   pl.BlockSpec((tb, LRU_W), lambda n: (n, 0))],
        out_shape=[jax.ShapeDtypeStruct((lp, D_MODEL), BF16),
                   jax.ShapeDtypeStruct((lp, LRU_W), F32)],
        input_output_aliases={7: 0},
        scratch_shapes=[pltpu.VMEM((8, LRU_W), F32)],
        compiler_params=_cp(1),
    )(proj, proj, proj, wl, vec, wa, wx, ycat)


def _lru_bwd(name, proj, dycat, hstate, wl, vec, wa, wx, dproj):
    lp = proj.shape[0]
    tb = _row_tile(lp, 3)
    nb = lp // tb

    def body(rx_ref, rxp_ref, gate_ref, dy_ref, h_ref, hp_ref, wl_ref, vec_ref, wa_ref, wx_ref, din_ref,
             dp_ref, dwl_ref, dvec_ref, dwa_ref, dwx_ref, dhc_ref, anx_ref, dxc_ref):
        del din_ref
        n = pl.program_id(0)
        j = nb - 1 - n

        @pl.when(n == 0)
        def _():
            dhc_ref[...] = jnp.zeros(dhc_ref.shape, F32)
            anx_ref[...] = jnp.zeros(anx_ref.shape, F32)
            dxc_ref[...] = jnp.zeros(dxc_ref.shape, F32)
            dwl_ref[...] = jnp.zeros(dwl_ref.shape, F32)
            dvec_ref[...] = jnp.zeros(dvec_ref.shape, F32)
            dwa_ref[...] = jnp.zeros(dwa_ref.shape, F32)
            dwx_ref[...] = jnp.zeros(dwx_ref.shape, F32)

        c = _lru_chain(j, tb, rx_ref, rxp_ref, wl_ref, vec_ref, wa_ref, wx_ref)
        a, mult, r, ig, x1, live = c["a"], c["mult"], c["r"], c["ig"], c["x1"], c["live"]
        h = h_ref[...]
        gate = gate_ref[...]
        sgate = _sig(gate)
        dy = dy_ref[...].astype(F32)
        gsum = dy * (gate * sgate)
        dgate = dy * h * _dsilu(gate, sgate)
        rows = lax.broadcasted_iota(jnp.int32, (tb, LRU_W), 0)
        bb = jnp.where(rows == tb - 1, anx_ref[0:1, :], pltpu.roll(a, tb - 1, axis=0))
        bb, gg = _scan_slabs(bb, gsum, forward=False)
        dh = gg + bb * dhc_ref[0:1, :]
        dhc_ref[...] = dh[0:8, :]
        anx_ref[...] = a[0:8, :]
        hprev = jnp.where(rows == 0, jnp.where(j > 0, hp_ref[7:8, :], 0.0), pltpu.roll(h, 1, axis=0))
        du = jnp.where(live, dh, 0.0)
        da = jnp.where(live, dh * hprev, 0.0)
        ar = c["a_raw"]
        dmult = du * (ig * x1)
        di = du * mult * x1
        dx1 = du * mult * ig
        dloga = da * ar - dmult * ar * ar / mult
        dr = dloga * (-LRU_C * c["sp"])
        dvec_ref[3:4, :] += _colsum(dloga * (-LRU_C * r))
        dza = dr * r * (1.0 - r)
        dzx = di * ig * (1.0 - ig)
        dzab, dzxb = dza.astype(BF16), dzx.astype(BF16)
        dvec_ref[1:2, :] += _colsum(dza)
        dvec_ref[2:3, :] += _colsum(dzx)
        dwa_ref[...] += _dot(c["x1b"], dzab, TN)
        dwx_ref[...] += _dot(c["x1b"], dzxb, TN)
        dx1 = dx1 + _dot(dzab, wa_ref[...], NT) + _dot(dzxb, wx_ref[...], NT)
        dvec_ref[0:1, :] += _colsum(dx1)
        for k in range(LRU_CONV_K):
            dwl_ref[k:k + 1, :] += _colsum(dx1 * c["views"][LRU_CONV_K - 1 - k])
        dcat = jnp.concatenate([dx1, dxc_ref[...]], axis=0)
        drx = jnp.zeros((tb, LRU_W), F32)
        for k in range(LRU_CONV_K):
            s = LRU_CONV_K - 1 - k
            view = dcat[0:tb, :] if s == 0 else pltpu.roll(dcat, tb + 8 - s, axis=0)[0:tb, :]
            drx = drx + wl_ref[k:k + 1, :] * view
        dxc_ref[...] = dx1[0:8, :]
        dp_ref[:, 0:512] = drx.astype(BF16)
        dp_ref[:, 512:1024] = dgate.astype(BF16)

        @pl.when(n == nb - 1)
        def _():
            lam = vec_ref[3:4, :]
            dvec_ref[3:4, :] = dvec_ref[3:4, :] * (-_sig(-lam))

    jmap = lambda n: nb - 1 - n
    return pl.pallas_call(
        body, name=name, grid=(nb,),
        in_specs=(_lru_specs(jmap, tb)
                  + [pl.BlockSpec((tb, 512), lambda n: (jmap(n), YC_LRU)),
                     pl.BlockSpec((tb, LRU_W), lambda n: (jmap(n), 0)),
                     pl.BlockSpec((8, LRU_W), lambda n: (jnp.maximum(jmap(n) * (tb // 8) - 1, 0), 0))]
                  + _lru_param_specs() + [pl.BlockSpec(memory_space=pl.ANY)]),
        out_specs=[pl.BlockSpec((tb, 1024), lambda n: (jmap(n), 4)),
                   pl.BlockSpec((8, LRU_W), lambda n: (0, 0)),
                   pl.BlockSpec((8, LRU_W), lambda n: (0, 0)),
                   pl.BlockSpec((LRU_W, LRU_W), lambda n: (0, 0)),
                   pl.BlockSpec((LRU_W, LRU_W), lambda n: (0, 0))],
        out_shape=[jax.ShapeDtypeStruct((lp, IN_TOTAL), BF16),
                   jax.ShapeDtypeStruct((8, LRU_W), F32),
                   jax.ShapeDtypeStruct((8, LRU_W), F32),
                   jax.ShapeDtypeStruct((LRU_W, LRU_W), F32),
                   jax.ShapeDtypeStruct((LRU_W, LRU_W), F32)],
        input_output_aliases={10: 0},
        scratch_shapes=[pltpu.VMEM((8, LRU_W), F32), pltpu.VMEM((8, LRU_W), F32), pltpu.VMEM((8, LRU_W), F32)],
        compiler_params=_cp(1),
    )(proj, proj, proj, dycat, hstate, hstate, wl, vec, wa, wx, dproj)


_HBM = pl.BlockSpec(memory_space=pltpu.HBM)
_SEM = pl.BlockSpec(memory_space=pltpu.SEMAPHORE)
_ANY = pl.BlockSpec(memory_space=pl.ANY)
_EFFECT = pltpu.SideEffectType.DATAFLOW_SIDE_EFFECTING


def _hbm(a):
    return pltpu.with_memory_space_constraint(a, pltpu.HBM)


_ALL_PEERS = tuple(range(1, N_DEV))
_CHIP_PEERS = (1, 2, 4, 6)
_OTHER_CHIPS = (2, 4, 6)


def _spec_peers(mode):
    return {"ici": _CHIP_PEERS, "fwd": _OTHER_CHIPS}.get(mode, _ALL_PEERS)


def _split_descriptors(copies, srcs, lands, send_sems, recv_sems):
    x, y, c = lax.axis_index("x"), lax.axis_index("y"), lax.axis_index("c")
    me = 4 * x + 2 * y + c
    out, sem = [], 0
    for si, mode, li, ll in copies:
        for k in _spec_peers(mode):
            px = 1 - x if k & 4 else x
            py = 1 - y if k & 2 else y
            pc = 1 - c if k & 1 else c
            peer = 4 * px + 2 * py + pc
            if mode == "fwd":
                src = dst = lands[li].at[peer]
                target = (x, y, 1 - c)
            else:
                src = srcs[si].at[peer] if mode is True else srcs[si]
                dst = lands[li].at[me] if ll is None else lands[li].at[me, ll]
                target = (px, py, pc)
            out.append(pltpu.make_async_remote_copy(
                src_ref=src, dst_ref=dst, send_sem=send_sems.at[sem], recv_sem=recv_sems.at[sem],
                device_id=target, device_id_type=pl.DeviceIdType.MESH))
            sem += 1
    return out


def _n_copies(copies):
    return sum(len(_spec_peers(mode)) for _, mode, _, _ in copies)


def _xchg_start(name, groups):
    n_src = [len(g[0]) for g in groups]
    n_land = [len(g[1]) for g in groups]
    srcs = [s for g in groups for s in g[0]]
    lands = [l for g in groups for l in g[1]]
    ns, nl, ng = len(srcs), len(lands), len(groups)

    def body(*refs):
        src_refs, land_refs = refs[:ns], refs[ns:ns + nl]
        sems = refs[ns + nl:ns + nl + 2 * ng]
        token = refs[-1]
        so = lo = 0
        for gi, (_, _, copies) in enumerate(groups):
            for d in _split_descriptors(copies, src_refs[so:so + n_src[gi]], land_refs[lo:lo + n_land[gi]],
                                        sems[2 * gi], sems[2 * gi + 1]):
                d.start()
            so += n_src[gi]
            lo += n_land[gi]
        token[...] = jnp.zeros(token.shape, F32)

    out_shape, out_specs = [], []
    for g in groups:
        n = _n_copies(g[2])
        out_shape += [pltpu.SemaphoreType.DMA((n,)), pltpu.SemaphoreType.DMA((n,))]
        out_specs += [_SEM, _SEM]
    out_shape += [pltpu.HBM(l.shape, l.dtype) for l in lands]
    out_specs += [_HBM] * nl
    out_shape.append(jax.ShapeDtypeStruct((8, 128), F32))
    out_specs.append(pl.BlockSpec(memory_space=pltpu.VMEM))
    outs = pl.pallas_call(
        body, name=name, in_specs=[_HBM] * (ns + nl), out_specs=out_specs, out_shape=out_shape,
        input_output_aliases={ns + i: 2 * ng + i for i in range(nl)},
        compiler_params=pltpu.CompilerParams(has_side_effects=_EFFECT),
    )(*[_hbm(a) for a in srcs + lands])
    res, lo = [], 2 * ng
    for gi in range(ng):
        res.append((outs[2 * gi], outs[2 * gi + 1], list(outs[lo:lo + n_land[gi]])))
        lo += n_land[gi]
    return res, outs[-1]


def _xchg_wait(name, group, started, after):
    srcs, _, copies = group
    send_sems, recv_sems, lands = started
    ns, nl = len(srcs), len(lands)
    after = list(after)

    def body(*refs):
        src_refs, land_refs = refs[:ns], refs[ns:ns + nl]
        send_ref, recv_ref = refs[ns + nl], refs[ns + nl + 1]
        for d in _split_descriptors(copies, src_refs, land_refs, send_ref, recv_ref):
            d.wait_send()
            d.wait_recv()

    outs = pl.pallas_call(
        body, name=name, in_specs=[_HBM] * (ns + nl) + [_SEM, _SEM] + [_ANY] * len(after),
        out_specs=[_HBM] * nl, out_shape=[pltpu.HBM(l.shape, l.dtype) for l in lands],
        input_output_aliases={ns + i: i for i in range(nl)},
        compiler_params=pltpu.CompilerParams(has_side_effects=_EFFECT),
    )(*[_hbm(a) for a in srcs], *lands, send_sems, recv_sems, *after)
    return list(outs)


def _landing(own, me):
    land = lax.empty((N_DEV,) + own.shape, own.dtype)
    return lax.dynamic_update_slice(land, own[None], (me,) + (0,) * own.ndim)


def _adamw(name, w, m, v, recv, row0=0, prev=None):
    cdim = w.shape[1]
    r = recv.shape[1]
    tr = r
    for cand in (512, 256, 128, 64, 32, 16, 8):
        if r % cand == 0 and r > cand:
            tr = cand
            break
    assert row0 % tr == 0
    blk0 = row0 // tr
    n_prev = 0 if prev is None else 4

    def body(w_ref, m_ref, v_ref, r_ref, *rest):
        g_ref, d_ref, mo_ref, vo_ref = rest[n_prev:]
        g = r_ref[0].astype(F32)
        for s in range(1, N_DEV):
            g = g + r_ref[s].astype(F32)
        mn = ADAM_B1 * m_ref[...] + (1.0 - ADAM_B1) * g
        vn = ADAM_B2 * v_ref[...] + (1.0 - ADAM_B2) * (g * g)
        m_hat = mn / (1.0 - ADAM_B1 ** ADAM_STEP)
        v_hat = vn / (1.0 - ADAM_B2 ** ADAM_STEP)
        g_ref[...] = g
        d_ref[...] = -ADAM_LR * (m_hat / (jnp.sqrt(v_hat) + ADAM_EPS) + ADAM_WD * w_ref[...])
        mo_ref[...] = mn
        vo_ref[...] = vn

    blk = pl.BlockSpec((tr, cdim), lambda i: (i + blk0, 0))
    return pl.pallas_call(
        body, name=name, grid=(r // tr,),
        in_specs=[blk, blk, blk, pl.BlockSpec((N_DEV, tr, cdim), lambda i: (0, i, 0))] + [_ANY] * n_prev,
        out_specs=[blk, blk, blk, blk],
        out_shape=[jax.ShapeDtypeStruct(w.shape, F32)] * 4,
        input_output_aliases={4 + i: i for i in range(n_prev)},
        compiler_params=_cp(1),
    )(w, m, v, recv, *(prev or []))


def _pack_rows(arrs, lead=()):
    n = len(lead)
    flat = jnp.concatenate([a.reshape(a.shape[:n] + (-1,)) for a in arrs], axis=-1)
    size = flat.shape[-1]
    padded = -(-size // PACK_QUANTUM) * PACK_QUANTUM
    flat = jnp.pad(flat, [(0, 0)] * n + [(0, padded - size)])
    return flat.reshape(flat.shape[:n] + (padded // 128, 128))


def _unpack_rows(packed, shapes, lead=()):
    n = len(lead)
    flat = packed.reshape(packed.shape[:n] + (-1,))
    out, off = [], 0
    for s in shapes:
        size = int(np.prod(s))
        out.append(flat[..., off:off + size].reshape(packed.shape[:n] + tuple(s)))
        off += size
    return out


def _block_diag(w):
    eye = jnp.eye(LRU_HEADS, dtype=w.dtype)
    return (eye[:, None, :, None] * w[:, :, None, :]).reshape(LRU_W, LRU_W)


def _diag_blocks(dense):
    t = dense.reshape(LRU_HEADS, 64, LRU_HEADS, 64)
    eye = jnp.eye(LRU_HEADS, dtype=dense.dtype)
    return jnp.sum(t * eye[:, None, :, None], axis=2).reshape(LRU_HEADS * 64, 64)


_W512_NAMES = ("conv_dw_b", "conv_ln_g", "conv_ln_b", "conv_pw_b", "lru_conv_b", "lru_ba", "lru_bx", "lru_lambda")
_W512_ROWS = 12


def _pack_small(d):
    sinks = jnp.pad(d["attn_sinks"], ((0, 0), (0, 512 - N_HEADS)))
    t = jnp.stack([d[n] for n in _W512_NAMES] + [sinks], axis=1)
    w512 = jnp.pad(t, ((0, 0), (0, _W512_ROWS - t.shape[1]), (0, 0))).reshape(DEPTH * _W512_ROWS, 512)
    w2048 = jnp.concatenate([d["ln_in_g"][None], d["ln_in_b"][None], d["ln_post_g"], d["ln_post_b"],
                             jnp.zeros((2, D_MODEL), F32)], axis=0)
    w64 = jnp.concatenate([d["lru_wa"].reshape(-1, 64), d["lru_wx"].reshape(-1, 64)], axis=0)
    return [w512, w2048, w64.reshape(-1, 128)]


def _unpack_small(w512=None, w2048=None, w64=None):
    out = {}
    if w512 is not None:
        t = w512.reshape(DEPTH, _W512_ROWS, 512)
        out.update({n: t[:, i, :] for i, n in enumerate(_W512_NAMES)})
        out["attn_sinks"] = t[:, len(_W512_NAMES), :N_HEADS]
    if w2048 is not None:
        out["ln_in_g"], out["ln_in_b"] = w2048[0], w2048[1]
        out["ln_post_g"], out["ln_post_b"] = w2048[2:4], w2048[4:6]
    if w64 is not None:
        w64 = w64.reshape(-1, 64)
        half = w64.shape[0] // 2
        out["lru_wa"] = w64[:half].reshape(DEPTH, LRU_HEADS, 64, 64)
        out["lru_wx"] = w64[half:].reshape(DEPTH, LRU_HEADS, 64, 64)
    return out


def _cols_to_slots(full):
    lead = full.shape[:-1]
    t = full.reshape(lead + (N_DEV, full.shape[-1] // N_DEV))
    return jnp.moveaxis(t, -2, 0)


def _slots_to_cols(slots):
    t = jnp.moveaxis(slots, 0, -2)
    return t.reshape(t.shape[:-2] + (t.shape[-2] * t.shape[-1],))


def kernel(x, meta_tokens, ln_in_g, ln_in_b, w_in, conv_dw_w, conv_dw_b, conv_ln_g, conv_ln_b, conv_pw_w, conv_pw_b, attn_sinks, lru_conv_w, lru_conv_b, lru_wa, lru_ba, lru_wx, lru_bx, lru_lambda, w_out, ln_post_g, ln_post_b, loss_target, m_meta_tokens, m_ln_in_g, m_ln_in_b, m_w_in, m_conv_dw_w, m_conv_dw_b, m_conv_ln_g, m_conv_ln_b, m_conv_pw_w, m_conv_pw_b, m_attn_sinks, m_lru_conv_w, m_lru_conv_b, m_lru_wa, m_lru_ba, m_lru_wx, m_lru_bx, m_lru_lambda, m_w_out, m_ln_post_g, m_ln_post_b, v_meta_tokens, v_ln_in_g, v_ln_in_b, v_w_in, v_conv_dw_w, v_conv_dw_b, v_conv_ln_g, v_conv_ln_b, v_conv_pw_w, v_conv_pw_b, v_attn_sinks, v_lru_conv_w, v_lru_conv_b, v_lru_wa, v_lru_ba, v_lru_wx, v_lru_bx, v_lru_lambda, v_w_out, v_ln_post_g, v_ln_post_b):
    seq = x.shape[1]
    lp = seq + TB
    row = lambda a: a.reshape(1, -1)
    shard_small_names = ["conv_dw_w", "lru_conv_w", "meta_tokens"]
    weights = dict(meta_tokens=meta_tokens, ln_in_g=ln_in_g, ln_in_b=ln_in_b, w_in=w_in, conv_dw_w=conv_dw_w,
                   conv_dw_b=conv_dw_b, conv_ln_g=conv_ln_g, conv_ln_b=conv_ln_b, conv_pw_w=conv_pw_w,
                   conv_pw_b=conv_pw_b, attn_sinks=attn_sinks, lru_conv_w=lru_conv_w, lru_conv_b=lru_conv_b,
                   lru_wa=lru_wa, lru_ba=lru_ba, lru_wx=lru_wx, lru_bx=lru_bx, lru_lambda=lru_lambda,
                   w_out=w_out, ln_post_g=ln_post_g, ln_post_b=ln_post_b)
    mom1 = dict(meta_tokens=m_meta_tokens, ln_in_g=m_ln_in_g, ln_in_b=m_ln_in_b, w_in=m_w_in, conv_dw_w=m_conv_dw_w,
                conv_dw_b=m_conv_dw_b, conv_ln_g=m_conv_ln_g, conv_ln_b=m_conv_ln_b, conv_pw_w=m_conv_pw_w,
                conv_pw_b=m_conv_pw_b, attn_sinks=m_attn_sinks, lru_conv_w=m_lru_conv_w, lru_conv_b=m_lru_conv_b,
                lru_wa=m_lru_wa, lru_ba=m_lru_ba, lru_wx=m_lru_wx, lru_bx=m_lru_bx, lru_lambda=m_lru_lambda,
                w_out=m_w_out, ln_post_g=m_ln_post_g, ln_post_b=m_ln_post_b)
    mom2 = dict(meta_tokens=v_meta_tokens, ln_in_g=v_ln_in_g, ln_in_b=v_ln_in_b, w_in=v_w_in, conv_dw_w=v_conv_dw_w,
                conv_dw_b=v_conv_dw_b, conv_ln_g=v_conv_ln_g, conv_ln_b=v_conv_ln_b, conv_pw_w=v_conv_pw_w,
                conv_pw_b=v_conv_pw_b, attn_sinks=v_attn_sinks, lru_conv_w=v_lru_conv_w, lru_conv_b=v_lru_conv_b,
                lru_wa=v_lru_wa, lru_ba=v_lru_ba, lru_wx=v_lru_wx, lru_bx=v_lru_bx, lru_lambda=v_lru_lambda,
                w_out=v_w_out, ln_post_g=v_ln_post_g, ln_post_b=v_ln_post_b)
    shard_wmv = [_pack_rows([d[n] for n in shard_small_names]) for d in (weights, mom1, mom2)]
    rep_wmv = [_pack_small(d) for d in (weights, mom1, mom2)]
    gate_w = [(_block_diag(lru_wa[l]).astype(BF16), _block_diag(lru_wx[l]).astype(BF16)) for l in range(DEPTH)]
    tabs = _rope_tables(lp)
    prepared = (shard_wmv + [a for wmv in rep_wmv for a in wmv]
                + [w for pair in gate_w for w in pair] + list(tabs))

    small_shard_shapes = [conv_dw_w.shape, lru_conv_w.shape, meta_tokens.shape]
    small_shard = _pack_rows([conv_dw_w, lru_conv_w, meta_tokens])
    me = 4 * lax.axis_index("x") + 2 * lax.axis_index("y") + lax.axis_index("c")
    w_in_b = [w_in[l].astype(BF16) for l in range(DEPTH)]
    w_out_b = [w_out[l].astype(BF16) for l in range(DEPTH)]
    pw_b = conv_pw_w.astype(BF16)
    wgroups = [
        ([small_shard], [_landing(small_shard, me)], [(0, False, 0, None)]),
        ([w_in_b[0]], [_landing(w_in_b[0], me)], [(0, "ici", 0, None)]),
        ([pw_b, w_out_b[0]], [_landing(pw_b, me), _landing(w_out_b[0], me)],
         [(0, False, 0, None), (1, False, 1, None)]),
        ([w_in_b[1], w_out_b[1]], [_landing(w_in_b[1], me), _landing(w_out_b[1], me)],
         [(0, "ici", 0, None), (1, "ici", 1, None)]),
    ]
    wstarted, wtoken = _xchg_start("weights_start", wgroups)

    def pass_on(tag, parts):
        fwd = ([], list(parts), [(None, "fwd", i, None) for i in range(len(parts))])
        fstarted, ftoken = _xchg_start(f"weights_fwd_start_{tag}", [fwd])
        return (fwd, fstarted[0]), ftoken
    wg_small, = _xchg_wait("weights_wait_s", wgroups[0], wstarted[0], [wtoken])
    g_dw, g_lc, g_meta = _unpack_rows(wg_small, small_shard_shapes, lead=(N_DEV,))
    conv_dw_full = _slots_to_cols(g_dw)
    lru_conv_full = _slots_to_cols(g_lc)
    meta_full = _slots_to_cols(g_meta)
    wg_in = [None, None]
    wg_out = [None, None]
    wg_pw = None

    ln_g = [ln_in_g, ln_post_g[0], ln_post_g[1]]
    ln_b = [ln_in_b, ln_post_b[0], ln_post_b[1]]

    def layer_params(l):
        wdw = jnp.pad(conv_dw_full[l], ((0, 1), (0, 0)))
        cvec = jnp.pad(jnp.stack([conv_dw_b[l], conv_ln_g[l], conv_ln_b[l], conv_pw_b[l]]), ((0, 4), (0, 0)))
        wpw = wg_pw[:, l].reshape(CONV_W, CONV_W)
        sinks = jnp.pad(attn_sinks[l].reshape(1, N_HEADS), ((0, 7), (0, 128 - N_HEADS)))
        wl = jnp.pad(lru_conv_full[l], ((0, 4), (0, 0)))
        lvec = jnp.pad(jnp.stack([lru_conv_b[l], lru_ba[l], lru_bx[l], lru_lambda[l]]), ((0, 4), (0, 0)))
        wa, wx = gate_w[l]
        wo = wg_out[l].reshape(D_MODEL, D_MODEL)
        wout = jnp.concatenate([wo[512:1536], wo[0:512], wo[1536:]], axis=0)
        return dict(wdw=wdw, cvec=cvec, wpw=wpw, sinks=sinks, wl=wl, lvec=lvec, wa=wa, wx=wx, wout=wout)

    params = [None] * DEPTH

    z0, hb = _embed(x, meta_full, row(ln_g[0]), row(ln_b[0]))
    z = [z0]
    saved = []
    for l in range(DEPTH):
        if l == 0:
            parts = _xchg_wait("weights_wait_a", wgroups[1], wstarted[1], [hb] + prepared)
            pending, ftoken = pass_on("a", parts)
            wg_in[0], = _xchg_wait("weights_fwd_wait_a", *pending, [ftoken])
        else:
            wg_in[1], wg_out[1] = _xchg_wait("weights_fwd_wait_c", *pending_c, [hb])
        proj = _mm_proj(f"proj{l}", hb, wg_in[l])
        if l == 0:
            wg_pw, wg_out[0] = _xchg_wait("weights_wait_b", wgroups[2], wstarted[2], [proj])
        p = params[l] = layer_params(l)
        ycat, c1 = _conv_fwd(f"conv_fwd{l}", proj, p["wdw"], p["cvec"], p["wpw"])
        qr, kr = _rope_fwd(f"rope{l}", proj, tabs)
        ycat = _attn_fwd(f"attn_fwd{l}", qr, kr, proj, p["sinks"], ycat)
        ycat, hstate = _lru_fwd(f"lru_fwd{l}", proj, p["wl"], p["lvec"], p["wa"], p["wx"], ycat)
        if l == 0:
            pending_c, ftoken = pass_on("c", _xchg_wait("weights_wait_c", wgroups[3], wstarted[3], [ycat]))
        saved.append(dict(hb=hb, proj=proj, ycat=ycat, qr=qr, kr=kr, hstate=hstate, c1=c1))
        last = l == DEPTH - 1
        z_next, hb = _mm_out(f"out{l}", ycat, p["wout"], z[l], row(ln_g[l]), row(ln_b[l]),
                             None if last else row(ln_g[l + 1]), None if last else row(ln_b[l + 1]), ftoken)
        z.append(z_next)

    dz, st_post1, loss_blk = _loss_head(z[DEPTH], loss_target, row(ln_g[DEPTH]), row(ln_b[DEPTH]))

    ln_stats = {DEPTH: st_post1}
    g_layers = [None] * DEPTH
    dwin_l, dwout_l = [None] * DEPTH, [None] * DEPTH
    grad_x = gmeta = None
    token = wtoken
    ggroups = [None] * DEPTH
    own = lambda a: lax.dynamic_index_in_dim(a, me, 0, keepdims=False)
    for l in reversed(range(DEPTH)):
        p, s = params[l], saved[l]
        dycat = _mm_dycat(f"dycat{l}", dz, p["wout"], token)
        dwout_l[l] = _mm_dwout(f"dwout{l}", s["ycat"], dz)
        dproj, dwdw, dcvec, dwpw = _conv_bwd(f"conv_bwd{l}", s["proj"], dycat, s["c1"], p["wdw"], p["cvec"], p["wpw"])
        dwo = jnp.concatenate([dwout_l[l][1024:1536], dwout_l[l][0:1024], dwout_l[l][1536:]], axis=0)
        dwo = dwo.reshape(N_DEV, D_MODEL // N_DEV, D_MODEL)
        dpw = dwpw.reshape(N_DEV, CONV_W // N_DEV, CONV_W)
        early = ([dwo, dpw], [_landing(own(dwo), me), _landing(own(dpw), me)],
                 [(0, True, 0, None), (1, True, 1, None)])
        started_early, token = _xchg_start(f"grads_start_out{l}", [early])
        dq, dgate, dk, dv, dsink = _attn_bwd(f"attn_bwd{l}", s["qr"], s["kr"], s["proj"], p["sinks"], dycat, token)
        dproj = _attn_assemble(f"attn_asm{l}", dq, dgate, dk, dv, tabs, dproj)
        dproj, dwl, dlvec, dwa, dwx = _lru_bwd(f"lru_bwd{l}", s["proj"], dycat, s["hstate"],
                                                p["wl"], p["lvec"], p["wa"], p["wx"], dproj)
        g512 = jnp.concatenate([dcvec[0:4], dlvec[0:4], jnp.pad(dsink[0:1], ((0, 0), (0, 512 - 128))),
                                jnp.zeros((_W512_ROWS - 9, 512), F32)], axis=0)
        g_layers[l] = dict(dwdw=dwdw[:CONV_K], dwl=dwl[:LRU_CONV_K], g512=g512,
                           dwa=_diag_blocks(dwa), dwx=_diag_blocks(dwx))
        if l == 0:
            g512 = jnp.concatenate([g_layers[i]["g512"] for i in range(DEPTH)], axis=0)
            g64 = jnp.concatenate([g_layers[i][k] for k in ("dwa", "dwx") for i in range(DEPTH)], axis=0)
            g64 = g64.reshape(-1, 128)
            vgroup = ([g512, g64], [_landing(g512, me), _landing(g64, me)],
                      [(0, False, 0, None), (1, False, 1, None)])
            vstarted, token = _xchg_start("vector_grads_start", [vgroup])
        dwin_l[l] = _mm_dwin(f"dwin{l}", s["hb"], dproj, token)
        late = ([dwin_l[l]], [_landing(own(dwin_l[l]), me)], [(0, True, 0, None)])
        started_late, token = _xchg_start(f"grads_start_in{l}", [late])
        ggroups[l] = [(late, started_late[0]), (early, started_early[0])]
        dh = _mm_dh(f"dh{l}", dproj, wg_in[l], dz, token)
        if l > 0:
            dz, ln_stats[l] = _ln_bwd(f"ln_bwd{l}", dh, z[l], row(ln_g[l]))
        else:
            grad_x, gmeta, ln_stats[0] = _ln_bwd_input(dh, z[0], row(ln_g[0]))

    loss_row = jnp.pad(loss_blk[0:1, :], ((0, 0), (0, D_MODEL - 128)))
    g2048 = jnp.concatenate([ln_stats[0][0:2], ln_stats[1][0:1], ln_stats[2][0:1], ln_stats[1][1:2],
                             ln_stats[2][1:2], loss_row, jnp.zeros((1, D_MODEL), F32)], axis=0)
    g_dw_full = jnp.stack([g_layers[l]["dwdw"] for l in range(DEPTH)])
    g_lc_full = jnp.stack([g_layers[l]["dwl"] for l in range(DEPTH)])
    shard_pack = _pack_rows([_cols_to_slots(g_dw_full), _cols_to_slots(g_lc_full), _cols_to_slots(gmeta)],
                            lead=(N_DEV,))
    sgroup = ([shard_pack, g2048], [_landing(own(shard_pack), me), _landing(g2048, me)],
              [(0, True, 0, None), (1, False, 1, None)])
    sstarted, token = _xchg_start("small_grads_start", [sgroup])

    res = {}

    def flat2(a, cols):
        return a.reshape(-1, cols)

    big = (("w_in", 0, W_IN_SHARD), ("w_out", 1, D_MODEL), ("conv_pw_w", 2, CONV_W))
    prev = {n: None for n, _, _ in big}
    def update(name_, cols, recv, l):
        w_ = weights[name_]
        prev[name_] = _adamw(f"adamw_{name_}{l}", flat2(w_, cols), flat2(mom1[name_], cols),
                             flat2(mom2[name_], cols), recv, row0=l * w_.shape[1], prev=prev[name_])

    def small_update(tag, ci, recv):
        return _adamw(f"adamw_small_w{tag}", rep_wmv[0][ci], rep_wmv[1][ci], rep_wmv[2][ci], recv)

    def keep(unpacked, k):
        for n, a in unpacked.items():
            res.setdefault(n, [None] * 4)[k] = a

    r_512, r_64 = _xchg_wait("vector_grads_wait", vgroup, vstarted[0], [token])
    o512, o64 = small_update("512", 0, r_512), small_update("64", 2, r_64)
    early_done = []
    for k in range(4):
        unpacked = _unpack_small(w512=o512[k], w64=o64[k])
        keep(unpacked, k)
        early_done += list(unpacked.values())
    after = [token]
    for l in reversed(range(DEPTH)):
        late, early = ggroups[l]
        r_out, r_pw = _xchg_wait(f"grads_wait{l}_1", early[0], early[1], after)
        if l > 0:
            r_in, = _xchg_wait(f"grads_wait{l}_0", late[0], late[1], after)
            update("w_in", W_IN_SHARD, r_in, l)
        update("w_out", D_MODEL, r_out, l)
        update("conv_pw_w", CONV_W, r_pw, l)
        after = [prev["w_out"][0], prev["conv_pw_w"][0], prev["w_in"][0]]
    late = ggroups[0][0]
    r_in, = _xchg_wait("grads_wait0_0", late[0], late[1], after + early_done)
    update("w_in", W_IN_SHARD, r_in, 0)
    for name_, _, _ in big:
        res[name_] = [o.reshape(weights[name_].shape) for o in prev[name_]]

    r_small, r_2048 = _xchg_wait("small_grads_wait", sgroup, sstarted[0], [prev[n][0] for n, _, _ in big])
    sshapes = [weights[n].shape for n in shard_small_names]
    outs = _adamw("adamw_small_sharded", *shard_wmv, r_small)
    for k, o in enumerate(outs):
        for n, a in zip(shard_small_names, _unpack_rows(o, sshapes)):
            res.setdefault(n, [None] * 4)[k] = a
    o2048 = small_update("2048", 1, r_2048)
    for k in range(4):
        keep(_unpack_small(w2048=o2048[k]), k)
    loss = o2048[0][6, 0]

    order = ["meta_tokens", "ln_in_g", "ln_in_b", "w_in", "conv_dw_w", "conv_dw_b", "conv_ln_g", "conv_ln_b",
             "conv_pw_w", "conv_pw_b", "attn_sinks", "lru_conv_w", "lru_conv_b", "lru_wa", "lru_ba", "lru_wx",
             "lru_bx", "lru_lambda", "w_out", "ln_post_g", "ln_post_b"]
    return (loss, grad_x,
            *[res[n][0] for n in order], *[res[n][1] for n in order],
            *[res[n][2] for n in order], *[res[n][3] for n in order])
```
